```python
import jax, jax.numpy as jnp
from jax import lax
import numpy as np

D_MODEL = 1024
BATCH = 8
SEQ = 4096
DEPTH = 4

CHUNK = 64
Q_BLOCK = 128
SC_WIDTH = 512
SC_GROUPS = 8
SC_KERNEL = 3
MLA_HEADS = 8
QK_NOPE = 64
QK_ROPE = 32
V_HEAD = 64
Q_LORA = 256
KV_LORA = 128
MLA_WIDTH = MLA_HEADS * V_HEAD
ROPE_THETA = 10000.0
CONF_WIDTH = D_MODEL
CONF_KERNEL = 31
EVEN_SPLITS = (SC_WIDTH, SC_WIDTH, SC_WIDTH, SC_WIDTH, Q_LORA, KV_LORA, QK_ROPE, MLA_WIDTH)
EVEN_IN = sum(EVEN_SPLITS)
ODD_IN = 3 * CONF_WIDTH
N_EVEN = (DEPTH + 1) // 2
N_ODD = DEPTH // 2
EPS = 1e-6

kernel_name = 'hybrid_chunk_causal_conv_mla_conformer_trunk'


def rms_norm(x, g):
    xf = x.astype(jnp.float32)
    y = xf * lax.rsqrt(jnp.mean(xf * xf, axis=-1, keepdims=True) + EPS)
    return (y * g.astype(jnp.float32)).astype(x.dtype)


def layer_norm(x, g, b):
    xf = x.astype(jnp.float32)
    mu = jnp.mean(xf, axis=-1, keepdims=True)
    var = jnp.mean(jnp.square(xf - mu), axis=-1, keepdims=True)
    y = (xf - mu) * lax.rsqrt(var + EPS)
    return (y * g.astype(jnp.float32) + b.astype(jnp.float32)).astype(x.dtype)


def split_cols(z, sizes):
    idx = np.cumsum(sizes)[:-1].tolist()
    return jnp.split(z, idx, axis=-1)


def causal_depthwise_conv(u, w, b):
    k, ch = w.shape
    y = lax.conv_general_dilated(u, w[:, None, :].astype(u.dtype), window_strides=(1,),
                                 padding=[(k - 1, 0)], dimension_numbers=('NWC', 'WIO', 'NWC'),
                                 feature_group_count=ch)
    return y + b.astype(u.dtype)


def rope_tables(positions):
    inv_freq = 1.0 / (ROPE_THETA ** (jnp.arange(0, QK_ROPE, 2, dtype=jnp.float32) / QK_ROPE))
    ang = positions.astype(jnp.float32)[..., None] * inv_freq
    return jnp.cos(ang), jnp.sin(ang)


def apply_rope(t, cos, sin):
    tf = t.astype(jnp.float32)
    t1, t2 = jnp.split(tf, 2, axis=-1)
    return jnp.concatenate([t1 * cos - t2 * sin, t2 * cos + t1 * sin], axis=-1).astype(t.dtype)


def block_causal_attention(q, k, v):
    b, h, s, dqk = q.shape
    nb = s // Q_BLOCK
    scale = 1.0 / np.sqrt(dqk)
    k_chunk = jnp.arange(s) // CHUNK
    qb = q.reshape(b, h, nb, Q_BLOCK, dqk).transpose(2, 0, 1, 3, 4)

    def one_block(args):
        q_blk, blk = args
        scores = jnp.einsum('bhqd,bhkd->bhqk', q_blk, k, preferred_element_type=jnp.float32) * scale
        q_chunk = (blk * Q_BLOCK + jnp.arange(Q_BLOCK)) // CHUNK
        allowed = k_chunk[None, :] <= q_chunk[:, None]
        scores = jnp.where(allowed, scores, jnp.finfo(jnp.float32).min)
        p = jax.nn.softmax(scores, axis=-1)
        return jnp.einsum('bhqk,bhkd->bhqd', p.astype(v.dtype), v)

    out = lax.map(one_block, (qb, jnp.arange(nb)))
    return out.transpose(1, 0, 3, 2, 4).reshape(b, s, h * v.shape[-1])


def even_mixer(h, cos, sin, w_in, sc_conv_w, sc_conv_b, q_norm_g, kv_norm_g, w_uq, w_ukv, w_out):
    bsz, s, _ = h.shape
    z = h @ w_in
    a_b, a_c, a_x, a_gate, c_q, c_kv, k_rope_raw, b_gate = split_cols(z, EVEN_SPLITS)
    y_a = a_b * causal_depthwise_conv(a_c * a_x, sc_conv_w, sc_conv_b)
    y_a = y_a * jax.nn.silu(a_gate)
    q = (rms_norm(c_q, q_norm_g) @ w_uq).reshape(bsz, s, MLA_HEADS, QK_NOPE + QK_ROPE)
    q_nope, q_rope = q[..., :QK_NOPE], q[..., QK_NOPE:]
    q_rope = apply_rope(q_rope, cos[:, :, None, :], sin[:, :, None, :])
    kv = (rms_norm(c_kv, kv_norm_g) @ w_ukv).reshape(bsz, s, MLA_HEADS, QK_NOPE + V_HEAD)
    k_nope, v = kv[..., :QK_NOPE], kv[..., QK_NOPE:]
    k_rope = apply_rope(k_rope_raw, cos, sin)
    k_rope = jnp.broadcast_to(k_rope[:, :, None, :], (bsz, s, MLA_HEADS, QK_ROPE))
    q_full = jnp.concatenate([q_nope, q_rope], axis=-1).transpose(0, 2, 1, 3)
    k_full = jnp.concatenate([k_nope, k_rope], axis=-1).transpose(0, 2, 1, 3)
    y_b = block_causal_attention(q_full, k_full, v.transpose(0, 2, 1, 3))
    y_b = y_b * jax.nn.silu(b_gate)
    return jnp.concatenate([y_a, y_b], axis=-1) @ w_out


def odd_mixer(h, w_in, conv_w, conv_b, ln_g, ln_b, w_out):
    z = h @ w_in
    val, glu_gate, silu_gate = jnp.split(z, 3, axis=-1)
    u = val * jax.nn.sigmoid(glu_gate)
    u = causal_depthwise_conv(u, conv_w, conv_b)
    u = jax.nn.silu(layer_norm(u, ln_g, ln_b))
    return (u * jax.nn.silu(silu_gate)) @ w_out


def _fwd_setup_inputs(seed: int = 0) -> dict:
    key = jax.random.key(seed)
    ks = iter(jax.random.split(key, 32))

    def nrm(shape, scale):
        return jax.random.normal(next(ks), shape, jnp.float32) * scale

    d = D_MODEL
    return {
        'x': nrm((BATCH, SEQ, d), 1.0),
        'c': nrm((BATCH, d), 1.0),
        'positions': (jax.random.randint(next(ks), (BATCH, 1), 0, 4096, dtype=jnp.int32)
                      + jnp.arange(SEQ, dtype=jnp.int32)[None, :]),
        'ada_w': nrm((DEPTH, d, 3 * d), 0.5 * d ** -0.5),
        'ada_b': nrm((DEPTH, 3 * d), 0.1),
        'pre_norm_g': 1.0 + nrm((DEPTH, d), 0.05),
        'post_norm_g': 1.0 + nrm((DEPTH, d), 0.05),
        'even_w_in': nrm((N_EVEN, d, EVEN_IN), d ** -0.5),
        'even_sc_conv_w': nrm((N_EVEN, SC_KERNEL, SC_WIDTH), SC_KERNEL ** -0.5),
        'even_sc_conv_b': nrm((N_EVEN, SC_WIDTH), 0.01),
        'even_q_norm_g': 1.0 + nrm((N_EVEN, Q_LORA), 0.05),
        'even_kv_norm_g': 1.0 + nrm((N_EVEN, KV_LORA), 0.05),
        'even_w_uq': nrm((N_EVEN, Q_LORA, MLA_HEADS * (QK_NOPE + QK_ROPE)), Q_LORA ** -0.5),
        'even_w_ukv': nrm((N_EVEN, KV_LORA, MLA_HEADS * (QK_NOPE + V_HEAD)), KV_LORA ** -0.5),
        'even_w_out': nrm((N_EVEN, SC_WIDTH + MLA_WIDTH, d), (SC_WIDTH + MLA_WIDTH) ** -0.5),
        'odd_w_in': nrm((N_ODD, d, ODD_IN), d ** -0.5),
        'odd_conv_w': nrm((N_ODD, CONF_KERNEL, CONF_WIDTH), CONF_KERNEL ** -0.5),
        'odd_conv_b': nrm((N_ODD, CONF_WIDTH), 0.01),
        'odd_ln_g': 1.0 + nrm((N_ODD, CONF_WIDTH), 0.05),
        'odd_ln_b': nrm((N_ODD, CONF_WIDTH), 0.01),
        'odd_w_out': nrm((N_ODD, CONF_WIDTH, d), CONF_WIDTH ** -0.5),
    }


def _fwd_reference(x, c, positions, ada_w, ada_b, pre_norm_g, post_norm_g,
              even_w_in, even_sc_conv_w, even_sc_conv_b, even_q_norm_g, even_kv_norm_g,
              even_w_uq, even_w_ukv, even_w_out,
              odd_w_in, odd_conv_w, odd_conv_b, odd_ln_g, odd_ln_b, odd_w_out):
    cos, sin = rope_tables(positions)
    c_act = jax.nn.silu(c)
    for layer in range(DEPTH):
        mod = c_act @ ada_w[layer] + ada_b[layer]
        shift, scale, gate = jnp.split(mod, 3, axis=-1)
        h = rms_norm(x, pre_norm_g[layer]) * (1.0 + scale[:, None, :]) + shift[:, None, :]
        i = layer // 2
        if layer % 2 == 0:
            y = even_mixer(h, cos, sin, even_w_in[i], even_sc_conv_w[i], even_sc_conv_b[i],
                           even_q_norm_g[i], even_kv_norm_g[i], even_w_uq[i], even_w_ukv[i],
                           even_w_out[i])
        else:
            y = odd_mixer(h, odd_w_in[i], odd_conv_w[i], odd_conv_b[i], odd_ln_g[i],
                          odd_ln_b[i], odd_w_out[i])
        x = x + gate[:, None, :] * rms_norm(y, post_norm_g[layer])
    return x


import jax as _jax
import jax.numpy as _jnp

TWIN_FORMAT = 'train_step'
FWD_PARAMS = ['x', 'c', 'positions', 'ada_w', 'ada_b', 'pre_norm_g', 'post_norm_g', 'even_w_in', 'even_sc_conv_w', 'even_sc_conv_b', 'even_q_norm_g', 'even_kv_norm_g', 'even_w_uq', 'even_w_ukv', 'even_w_out', 'odd_w_in', 'odd_conv_w', 'odd_conv_b', 'odd_ln_g', 'odd_ln_b', 'odd_w_out']
TWIN_WEIGHTS = ['ada_w', 'ada_b', 'pre_norm_g', 'post_norm_g', 'even_w_in', 'even_sc_conv_w', 'even_sc_conv_b', 'even_q_norm_g', 'even_kv_norm_g', 'even_w_uq', 'even_w_ukv', 'even_w_out', 'odd_w_in', 'odd_conv_w', 'odd_conv_b', 'odd_ln_g', 'odd_ln_b', 'odd_w_out']
TWIN_DIFF_INPUT = 'x'
TWIN_INPUTS = ['x', 'c', 'positions', 'ada_w', 'ada_b', 'pre_norm_g', 'post_norm_g', 'even_w_in', 'even_sc_conv_w', 'even_sc_conv_b', 'even_q_norm_g', 'even_kv_norm_g', 'even_w_uq', 'even_w_ukv', 'even_w_out', 'odd_w_in', 'odd_conv_w', 'odd_conv_b', 'odd_ln_g', 'odd_ln_b', 'odd_w_out', 'loss_target', 'm_ada_w', 'm_ada_b', 'm_pre_norm_g', 'm_post_norm_g', 'm_even_w_in', 'm_even_sc_conv_w', 'm_even_sc_conv_b', 'm_even_q_norm_g', 'm_even_kv_norm_g', 'm_even_w_uq', 'm_even_w_ukv', 'm_even_w_out', 'm_odd_w_in', 'm_odd_conv_w', 'm_odd_conv_b', 'm_odd_ln_g', 'm_odd_ln_b', 'm_odd_w_out', 'v_ada_w', 'v_ada_b', 'v_pre_norm_g', 'v_post_norm_g', 'v_even_w_in', 'v_even_sc_conv_w', 'v_even_sc_conv_b', 'v_even_q_norm_g', 'v_even_kv_norm_g', 'v_even_w_uq', 'v_even_w_ukv', 'v_even_w_out', 'v_odd_w_in', 'v_odd_conv_w', 'v_odd_conv_b', 'v_odd_ln_g', 'v_odd_ln_b', 'v_odd_w_out']
TWIN_OUTPUTS = ['loss', 'grad_x', 'grad_ada_w', 'grad_ada_b', 'grad_pre_norm_g', 'grad_post_norm_g', 'grad_even_w_in', 'grad_even_sc_conv_w', 'grad_even_sc_conv_b', 'grad_even_q_norm_g', 'grad_even_kv_norm_g', 'grad_even_w_uq', 'grad_even_w_ukv', 'grad_even_w_out', 'grad_odd_w_in', 'grad_odd_conv_w', 'grad_odd_conv_b', 'grad_odd_ln_g', 'grad_odd_ln_b', 'grad_odd_w_out', 'delta_ada_w', 'delta_ada_b', 'delta_pre_norm_g', 'delta_post_norm_g', 'delta_even_w_in', 'delta_even_sc_conv_w', 'delta_even_sc_conv_b', 'delta_even_q_norm_g', 'delta_even_kv_norm_g', 'delta_even_w_uq', 'delta_even_w_ukv', 'delta_even_w_out', 'delta_odd_w_in', 'delta_odd_conv_w', 'delta_odd_conv_b', 'delta_odd_ln_g', 'delta_odd_ln_b', 'delta_odd_w_out', 'new_m_ada_w', 'new_m_ada_b', 'new_m_pre_norm_g', 'new_m_post_norm_g', 'new_m_even_w_in', 'new_m_even_sc_conv_w', 'new_m_even_sc_conv_b', 'new_m_even_q_norm_g', 'new_m_even_kv_norm_g', 'new_m_even_w_uq', 'new_m_even_w_ukv', 'new_m_even_w_out', 'new_m_odd_w_in', 'new_m_odd_conv_w', 'new_m_odd_conv_b', 'new_m_odd_ln_g', 'new_m_odd_ln_b', 'new_m_odd_w_out', 'new_v_ada_w', 'new_v_ada_b', 'new_v_pre_norm_g', 'new_v_post_norm_g', 'new_v_even_w_in', 'new_v_even_sc_conv_w', 'new_v_even_sc_conv_b', 'new_v_even_q_norm_g', 'new_v_even_kv_norm_g', 'new_v_even_w_uq', 'new_v_even_w_ukv', 'new_v_even_w_out', 'new_v_odd_w_in', 'new_v_odd_conv_w', 'new_v_odd_conv_b', 'new_v_odd_ln_g', 'new_v_odd_ln_b', 'new_v_odd_w_out']
TWIN_LEAF_KINDS = {'loss': 'loss', 'grad_x': 'grad_x', 'grad_ada_w': 'grad_w', 'grad_ada_b': 'grad_w', 'grad_pre_norm_g': 'grad_w', 'grad_post_norm_g': 'grad_w', 'grad_even_w_in': 'grad_w', 'grad_even_sc_conv_w': 'grad_w', 'grad_even_sc_conv_b': 'grad_w', 'grad_even_q_norm_g': 'grad_w', 'grad_even_kv_norm_g': 'grad_w', 'grad_even_w_uq': 'grad_w', 'grad_even_w_ukv': 'grad_w', 'grad_even_w_out': 'grad_w', 'grad_odd_w_in': 'grad_w', 'grad_odd_conv_w': 'grad_w', 'grad_odd_conv_b': 'grad_w', 'grad_odd_ln_g': 'grad_w', 'grad_odd_ln_b': 'grad_w', 'grad_odd_w_out': 'grad_w', 'delta_ada_w': 'delta_w', 'delta_ada_b': 'delta_w', 'delta_pre_norm_g': 'delta_w', 'delta_post_norm_g': 'delta_w', 'delta_even_w_in': 'delta_w', 'delta_even_sc_conv_w': 'delta_w', 'delta_even_sc_conv_b': 'delta_w', 'delta_even_q_norm_g': 'delta_w', 'delta_even_kv_norm_g': 'delta_w', 'delta_even_w_uq': 'delta_w', 'delta_even_w_ukv': 'delta_w', 'delta_even_w_out': 'delta_w', 'delta_odd_w_in': 'delta_w', 'delta_odd_conv_w': 'delta_w', 'delta_odd_conv_b': 'delta_w', 'delta_odd_ln_g': 'delta_w', 'delta_odd_ln_b': 'delta_w', 'delta_odd_w_out': 'delta_w', 'new_m_ada_w': 'new_m', 'new_m_ada_b': 'new_m', 'new_m_pre_norm_g': 'new_m', 'new_m_post_norm_g': 'new_m', 'new_m_even_w_in': 'new_m', 'new_m_even_sc_conv_w': 'new_m', 'new_m_even_sc_conv_b': 'new_m', 'new_m_even_q_norm_g': 'new_m', 'new_m_even_kv_norm_g': 'new_m', 'new_m_even_w_uq': 'new_m', 'new_m_even_w_ukv': 'new_m', 'new_m_even_w_out': 'new_m', 'new_m_odd_w_in': 'new_m', 'new_m_odd_conv_w': 'new_m', 'new_m_odd_conv_b': 'new_m', 'new_m_odd_ln_g': 'new_m', 'new_m_odd_ln_b': 'new_m', 'new_m_odd_w_out': 'new_m', 'new_v_ada_w': 'new_v', 'new_v_ada_b': 'new_v', 'new_v_pre_norm_g': 'new_v', 'new_v_post_norm_g': 'new_v', 'new_v_even_w_in': 'new_v', 'new_v_even_sc_conv_w': 'new_v', 'new_v_even_sc_conv_b': 'new_v', 'new_v_even_q_norm_g': 'new_v', 'new_v_even_kv_norm_g': 'new_v', 'new_v_even_w_uq': 'new_v', 'new_v_even_w_ukv': 'new_v', 'new_v_even_w_out': 'new_v', 'new_v_odd_w_in': 'new_v', 'new_v_odd_conv_w': 'new_v', 'new_v_odd_conv_b': 'new_v', 'new_v_odd_ln_g': 'new_v', 'new_v_odd_ln_b': 'new_v', 'new_v_odd_w_out': 'new_v'}


def _forward(args):
    return _fwd_reference(*[args[k] for k in FWD_PARAMS])


def _output_shape():
    def fwd():
        inp = _fwd_setup_inputs(0)
        return _fwd_reference(*[inp[k] for k in FWD_PARAMS])
    out = _jax.eval_shape(fwd)
    return out.shape, out.dtype

N_MICROBATCH = 1
ADAM_LR = 0.001
ADAM_B1 = 0.9
ADAM_B2 = 0.999
ADAM_EPS = 1e-08
ADAM_WD = 0.01
ADAM_STEP = 10
PER_EXAMPLE_BATCH_AXIS = {'x': 0, 'c': 0, 'positions': 0, 'loss_target': 0}
SHARED_INPUTS = []
_WEIGHT_DTYPES = {'ada_w': _jnp.float32, 'ada_b': _jnp.float32, 'pre_norm_g': _jnp.float32, 'post_norm_g': _jnp.float32, 'even_w_in': _jnp.float32, 'even_sc_conv_w': _jnp.float32, 'even_sc_conv_b': _jnp.float32, 'even_q_norm_g': _jnp.float32, 'even_kv_norm_g': _jnp.float32, 'even_w_uq': _jnp.float32, 'even_w_ukv': _jnp.float32, 'even_w_out': _jnp.float32, 'odd_w_in': _jnp.float32, 'odd_conv_w': _jnp.float32, 'odd_conv_b': _jnp.float32, 'odd_ln_g': _jnp.float32, 'odd_ln_b': _jnp.float32, 'odd_w_out': _jnp.float32}
MOMENT_SCALE = {'ada_w': 1.344470e+00, 'ada_b': 3.266049e+00, 'pre_norm_g': 1.890870e-01, 'post_norm_g': 3.769803e+00, 'even_w_in': 1.362517e-01, 'even_sc_conv_w': 1.655933e-01, 'even_sc_conv_b': 1.610673e-01, 'even_q_norm_g': 2.904427e-02, 'even_kv_norm_g': 1.252618e-01, 'even_w_uq': 1.573512e-02, 'even_w_ukv': 4.051327e-02, 'even_w_out': 1.358493e-01, 'odd_w_in': 9.820824e-02, 'odd_conv_w': 1.156245e-01, 'odd_conv_b': 2.992933e-01, 'odd_ln_g': 1.805880e-01, 'odd_ln_b': 1.983286e-01, 'odd_w_out': 1.323917e-01}


def _to_microbatches(a, axis):
    t = _jnp.moveaxis(a, axis, 0)
    t = t.reshape((N_MICROBATCH, t.shape[0] // N_MICROBATCH) + t.shape[1:])
    return _jnp.moveaxis(t, 1, axis + 1)


def setup_inputs(seed: int = 0) -> dict:
    inp = _fwd_setup_inputs(seed)
    key = _jax.random.fold_in(_jax.random.key(seed), 7919)
    shape, _ = _output_shape()
    out = dict(inp)
    out["loss_target"] = _jax.random.normal(_jax.random.fold_in(key, 0), shape, _jnp.float32)
    for i, name in enumerate(TWIN_WEIGHTS):
        w = inp[name].astype(_jnp.float32)
        if MOMENT_SCALE is None:
            s = _jnp.sqrt(_jnp.mean(_jnp.square(w)) + 1e-30)
        else:
            s = MOMENT_SCALE[name]
        km, kv = _jax.random.split(_jax.random.fold_in(key, i + 1))
        out[name] = w
        out["m_" + name] = s * _jax.random.normal(km, w.shape, _jnp.float32)
        out["v_" + name] = (s * s) * _jax.random.uniform(kv, w.shape, _jnp.float32, 0.5, 1.5)
    if N_MICROBATCH > 1:
        for name, axis in PER_EXAMPLE_BATCH_AXIS.items():
            out[name] = _to_microbatches(out[name], axis)
    return {'x': out['x'], 'c': out['c'], 'positions': out['positions'], 'ada_w': out['ada_w'], 'ada_b': out['ada_b'], 'pre_norm_g': out['pre_norm_g'], 'post_norm_g': out['post_norm_g'], 'even_w_in': out['even_w_in'], 'even_sc_conv_w': out['even_sc_conv_w'], 'even_sc_conv_b': out['even_sc_conv_b'], 'even_q_norm_g': out['even_q_norm_g'], 'even_kv_norm_g': out['even_kv_norm_g'], 'even_w_uq': out['even_w_uq'], 'even_w_ukv': out['even_w_ukv'], 'even_w_out': out['even_w_out'], 'odd_w_in': out['odd_w_in'], 'odd_conv_w': out['odd_conv_w'], 'odd_conv_b': out['odd_conv_b'], 'odd_ln_g': out['odd_ln_g'], 'odd_ln_b': out['odd_ln_b'], 'odd_w_out': out['odd_w_out'], 'loss_target': out['loss_target'], 'm_ada_w': out['m_ada_w'], 'm_ada_b': out['m_ada_b'], 'm_pre_norm_g': out['m_pre_norm_g'], 'm_post_norm_g': out['m_post_norm_g'], 'm_even_w_in': out['m_even_w_in'], 'm_even_sc_conv_w': out['m_even_sc_conv_w'], 'm_even_sc_conv_b': out['m_even_sc_conv_b'], 'm_even_q_norm_g': out['m_even_q_norm_g'], 'm_even_kv_norm_g': out['m_even_kv_norm_g'], 'm_even_w_uq': out['m_even_w_uq'], 'm_even_w_ukv': out['m_even_w_ukv'], 'm_even_w_out': out['m_even_w_out'], 'm_odd_w_in': out['m_odd_w_in'], 'm_odd_conv_w': out['m_odd_conv_w'], 'm_odd_conv_b': out['m_odd_conv_b'], 'm_odd_ln_g': out['m_odd_ln_g'], 'm_odd_ln_b': out['m_odd_ln_b'], 'm_odd_w_out': out['m_odd_w_out'], 'v_ada_w': out['v_ada_w'], 'v_ada_b': out['v_ada_b'], 'v_pre_norm_g': out['v_pre_norm_g'], 'v_post_norm_g': out['v_post_norm_g'], 'v_even_w_in': out['v_even_w_in'], 'v_even_sc_conv_w': out['v_even_sc_conv_w'], 'v_even_sc_conv_b': out['v_even_sc_conv_b'], 'v_even_q_norm_g': out['v_even_q_norm_g'], 'v_even_kv_norm_g': out['v_even_kv_norm_g'], 'v_even_w_uq': out['v_even_w_uq'], 'v_even_w_ukv': out['v_even_w_ukv'], 'v_even_w_out': out['v_even_w_out'], 'v_odd_w_in': out['v_odd_w_in'], 'v_odd_conv_w': out['v_odd_conv_w'], 'v_odd_conv_b': out['v_odd_conv_b'], 'v_odd_ln_g': out['v_odd_ln_g'], 'v_odd_ln_b': out['v_odd_ln_b'], 'v_odd_w_out': out['v_odd_w_out']}


def _loss(weights, diff, rest, loss_target):
    with _jax.named_scope("forward"):
        args = {**rest, TWIN_DIFF_INPUT: diff, **{k: w.astype(_WEIGHT_DTYPES[k]) for k, w in weights.items()}}
        y = _forward(args)
    with _jax.named_scope("loss_head"):
        err = _jnp.square(y.astype(_jnp.float32) - loss_target)
        return 0.5 * _jnp.sum(_jnp.mean(err, axis=-1)) if err.ndim else 0.5 * err


def _adamw(w, g, m, v):
    m = ADAM_B1 * m + (1.0 - ADAM_B1) * g
    v = ADAM_B2 * v + (1.0 - ADAM_B2) * _jnp.square(g)
    m_hat = m / (1.0 - ADAM_B1 ** ADAM_STEP)
    v_hat = v / (1.0 - ADAM_B2 ** ADAM_STEP)
    delta = -ADAM_LR * (m_hat / (_jnp.sqrt(v_hat) + ADAM_EPS) + ADAM_WD * w)
    return delta, m, v


def reference(x, c, positions, ada_w, ada_b, pre_norm_g, post_norm_g, even_w_in, even_sc_conv_w, even_sc_conv_b, even_q_norm_g, even_kv_norm_g, even_w_uq, even_w_ukv, even_w_out, odd_w_in, odd_conv_w, odd_conv_b, odd_ln_g, odd_ln_b, odd_w_out, loss_target, m_ada_w, m_ada_b, m_pre_norm_g, m_post_norm_g, m_even_w_in, m_even_sc_conv_w, m_even_sc_conv_b, m_even_q_norm_g, m_even_kv_norm_g, m_even_w_uq, m_even_w_ukv, m_even_w_out, m_odd_w_in, m_odd_conv_w, m_odd_conv_b, m_odd_ln_g, m_odd_ln_b, m_odd_w_out, v_ada_w, v_ada_b, v_pre_norm_g, v_post_norm_g, v_even_w_in, v_even_sc_conv_w, v_even_sc_conv_b, v_even_q_norm_g, v_even_kv_norm_g, v_even_w_uq, v_even_w_ukv, v_even_w_out, v_odd_w_in, v_odd_conv_w, v_odd_conv_b, v_odd_ln_g, v_odd_ln_b, v_odd_w_out):
    given = dict(x=x, c=c, positions=positions, ada_w=ada_w, ada_b=ada_b, pre_norm_g=pre_norm_g, post_norm_g=post_norm_g, even_w_in=even_w_in, even_sc_conv_w=even_sc_conv_w, even_sc_conv_b=even_sc_conv_b, even_q_norm_g=even_q_norm_g, even_kv_norm_g=even_kv_norm_g, even_w_uq=even_w_uq, even_w_ukv=even_w_ukv, even_w_out=even_w_out, odd_w_in=odd_w_in, odd_conv_w=odd_conv_w, odd_conv_b=odd_conv_b, odd_ln_g=odd_ln_g, odd_ln_b=odd_ln_b, odd_w_out=odd_w_out, loss_target=loss_target, m_ada_w=m_ada_w, m_ada_b=m_ada_b, m_pre_norm_g=m_pre_norm_g, m_post_norm_g=m_post_norm_g, m_even_w_in=m_even_w_in, m_even_sc_conv_w=m_even_sc_conv_w, m_even_sc_conv_b=m_even_sc_conv_b, m_even_q_norm_g=m_even_q_norm_g, m_even_kv_norm_g=m_even_kv_norm_g, m_even_w_uq=m_even_w_uq, m_even_w_ukv=m_even_w_ukv, m_even_w_out=m_even_w_out, m_odd_w_in=m_odd_w_in, m_odd_conv_w=m_odd_conv_w, m_odd_conv_b=m_odd_conv_b, m_odd_ln_g=m_odd_ln_g, m_odd_ln_b=m_odd_ln_b, m_odd_w_out=m_odd_w_out, v_ada_w=v_ada_w, v_ada_b=v_ada_b, v_pre_norm_g=v_pre_norm_g, v_post_norm_g=v_post_norm_g, v_even_w_in=v_even_w_in, v_even_sc_conv_w=v_even_sc_conv_w, v_even_sc_conv_b=v_even_sc_conv_b, v_even_q_norm_g=v_even_q_norm_g, v_even_kv_norm_g=v_even_kv_norm_g, v_even_w_uq=v_even_w_uq, v_even_w_ukv=v_even_w_ukv, v_even_w_out=v_even_w_out, v_odd_w_in=v_odd_w_in, v_odd_conv_w=v_odd_conv_w, v_odd_conv_b=v_odd_conv_b, v_odd_ln_g=v_odd_ln_g, v_odd_ln_b=v_odd_ln_b, v_odd_w_out=v_odd_w_out)
    weights = {n: given[n] for n in TWIN_WEIGHTS}
    shared = {n: given[n] for n in SHARED_INPUTS}
    per_example = {n: given[n] for n in ['x', 'c', 'positions']}
    grad_fn = _jax.value_and_grad(_loss, argnums=(0, 1))

    def one_microbatch(ex, loss_target):
        ex = dict(ex)
        diff = ex.pop(TWIN_DIFF_INPUT)
        return grad_fn(weights, diff, {**shared, **ex}, loss_target)

    if N_MICROBATCH == 1:
        loss, (grad_w, grad_x) = one_microbatch(per_example, given["loss_target"])
    else:
        def body(carry, xs):
            loss_sum, grad_sum = carry
            l_k, (gw_k, gx_k) = one_microbatch(xs[0], xs[1])
            with _jax.named_scope("update"):
                return (loss_sum + l_k, _jax.tree.map(_jnp.add, grad_sum, gw_k)), gx_k

        init = (_jnp.zeros((), _jnp.float32), _jax.tree.map(_jnp.zeros_like, weights))
        (loss, grad_w), grad_x = _jax.lax.scan(body, init, (per_example, given["loss_target"]))
    with _jax.named_scope("update"):
        delta_w, new_m, new_v = {}, {}, {}
        for n in TWIN_WEIGHTS:
            delta_w[n], new_m[n], new_v[n] = _adamw(weights[n], grad_w[n], given["m_" + n], given["v_" + n])
    return (loss, grad_x, *[grad_w[n] for n in TWIN_WEIGHTS], *[delta_w[n] for n in TWIN_WEIGHTS],
            *[new_m[n] for n in TWIN_WEIGHTS], *[new_v[n] for n in TWIN_WEIGHTS])
```

```python
import functools
import math

import jax
import jax.numpy as jnp
from jax import lax
from jax.experimental import pallas as pl
from jax.experimental.pallas import tpu as pltpu

F32 = jnp.float32
BF16 = jnp.bfloat16
MESH_AXES = ("x", "y", "c")
N_DEV = 8
EPS = 1e-6
CHUNK = 64
HEADS = 8
QK_NOPE = 64
QK_ROPE = 32
V_HEAD = 64
HEAD_PAD = 128
ROPE_THETA = 10000.0
SC_KERNEL = 3
CONF_KERNEL = 31
LANES = 128
SUBLANES = 8
PACK_COLS = 1024
VMEM_LIMIT = 48 * 1024 * 1024
NEG = -1e30

ADAM_LR = 0.001
ADAM_B1 = 0.9
ADAM_B2 = 0.999
ADAM_EPS = 1e-08
ADAM_WD = 0.01
ADAM_STEP = 10


def _cparams():
    return pltpu.CompilerParams(vmem_limit_bytes=VMEM_LIMIT)


def _sigmoid(x):
    return 1.0 / (1.0 + jnp.exp(-x))


def _silu(x):
    return x * _sigmoid(x)


def _dsilu(x):
    s = _sigmoid(x)
    return s * (1.0 + x * (1.0 - s))


def _rows(T, width, cb=0):
    return pl.BlockSpec((T, width), lambda i: (i, cb))


def _const(shape):
    nd = len(shape)
    return pl.BlockSpec(shape, lambda i: (0,) * nd)


def _row_tile(S):
    return min(256, S)


def _exchange(src, scatter, name):
    shape = src.shape[1:] if scatter else src.shape

    def body(src_ref, out_ref, send_sems, recv_sems, local_sem):
        x, y, c = lax.axis_index("x"), lax.axis_index("y"), lax.axis_index("c")
        me = 4 * x + 2 * y + c

        def piece(d):
            return src_ref.at[d] if scatter else src_ref

        own = pltpu.make_async_copy(piece(me), out_ref.at[me], local_sem)
        own.start()
        copies = []
        for k in range(1, N_DEV):
            px, py, pc = x ^ ((k >> 2) & 1), y ^ ((k >> 1) & 1), c ^ (k & 1)
            peer = 4 * px + 2 * py + pc
            cp = pltpu.make_async_remote_copy(
                src_ref=piece(peer), dst_ref=out_ref.at[me],
                send_sem=send_sems.at[k - 1], recv_sem=recv_sems.at[k - 1],
                device_id=(px, py, pc), device_id_type=pl.DeviceIdType.MESH)
            cp.start()
            copies.append((cp, peer))
        for k, (cp, peer) in enumerate(copies):
            pltpu.make_async_remote_copy(
                src_ref=piece(peer), dst_ref=out_ref.at[peer],
                send_sem=send_sems.at[k], recv_sem=recv_sems.at[k],
                device_id=(x, y, c), device_id_type=pl.DeviceIdType.MESH).wait_recv()
        for cp, _ in copies:
            cp.wait_send()
        own.wait()

    return pl.pallas_call(
        body, name=name,
        out_shape=jax.ShapeDtypeStruct((N_DEV,) + tuple(shape), src.dtype),
        in_specs=[pl.BlockSpec(memory_space=pl.ANY)],
        out_specs=pl.BlockSpec(memory_space=pl.ANY),
        scratch_shapes=[pltpu.SemaphoreType.DMA((N_DEV - 1,)),
                        pltpu.SemaphoreType.DMA((N_DEV - 1,)),
                        pltpu.SemaphoreType.DMA],
    )(src)


def _pack(parts, dtype, row_mult):
    flat = jnp.concatenate([p.reshape(-1).astype(dtype) for p in parts])
    n = flat.shape[0]
    rows = -(-n // PACK_COLS)
    rows = -(-rows // row_mult) * row_mult
    flat = jnp.pad(flat, (0, rows * PACK_COLS - n))
    return flat.reshape(rows, PACK_COLS)


def _unpack(flat, shapes):
    out, off = [], 0
    for shp in shapes:
        n = math.prod(shp)
        out.append(flat[..., off:off + n].reshape(flat.shape[:-1] + tuple(shp)))
        off += n
    return out


_DIMS = {"nn": (((1,), (0,)), ((), ())), "nt": (((1,), (1,)), ((), ())), "tn": (((0,), (0,)), ((), ()))}


def _matmul(a, b, mode, out_dtype, name, tm=512, tn=512, tk=None):
    if mode == "nn":
        (M, K), (_, N) = a.shape, b.shape
    elif mode == "nt":
        (M, K), (N, _) = a.shape, b.shape
    else:
        (K, M), (_, N) = a.shape, b.shape
    tm, tn = min(tm, M), min(tn, N)
    tk = K if tk is None else min(tk, K)
    nk = K // tk
    assert M % tm == 0 and N % tn == 0 and K % tk == 0, (name, a.shape, b.shape)

    def body(a_ref, b_ref, o_ref, *scratch):
        p = lax.dot_general(a_ref[...].astype(BF16), b_ref[...].astype(BF16), _DIMS[mode],
                            preferred_element_type=F32)
        if nk == 1:
            o_ref[...] = p.astype(out_dtype)
        else:
            acc = scratch[0]
            k = pl.program_id(2)

            @pl.when(k == 0)
            def _():
                acc[...] = p

            @pl.when(k > 0)
            def _():
                acc[...] += p

            @pl.when(k == nk - 1)
            def _():
                o_ref[...] = acc[...].astype(out_dtype)

    a_spec = (pl.BlockSpec((tk, tm), lambda i, j, k: (k, i)) if mode == "tn"
              else pl.BlockSpec((tm, tk), lambda i, j, k: (i, k)))
    b_spec = (pl.BlockSpec((tn, tk), lambda i, j, k: (j, k)) if mode == "nt"
              else pl.BlockSpec((tk, tn), lambda i, j, k: (k, j)))
    return pl.pallas_call(
        body, name=name, grid=(M // tm, N // tn, nk),
        out_shape=jax.ShapeDtypeStruct((M, N), out_dtype),
        in_specs=[a_spec, b_spec],
        out_specs=pl.BlockSpec((tm, tn), lambda i, j, k: (i, j)),
        scratch_shapes=[pltpu.VMEM((tm, tn), F32)] if nk > 1 else [],
        compiler_params=_cparams(),
    )(a, b)


def _ada_fwd(c_all, ada_w, ada_b_cols):
    L, D, n = ada_w.shape

    def body(c_ref, w_ref, b_ref, o_ref):
        act = _silu(c_ref[...]).astype(BF16)
        o_ref[0] = jnp.dot(act, w_ref[0].astype(BF16), preferred_element_type=F32) + b_ref[0]

    return pl.pallas_call(
        body, name="ada_fwd", grid=(L,),
        out_shape=jax.ShapeDtypeStruct((L, N_DEV, n), F32),
        in_specs=[pl.BlockSpec((N_DEV, D), lambda l: (0, 0)),
                  pl.BlockSpec((1, D, n), lambda l: (l, 0, 0)),
                  pl.BlockSpec((1, 1, n), lambda l: (l, 0, 0))],
        out_specs=pl.BlockSpec((1, N_DEV, n), lambda l: (l, 0, 0)),
        compiler_params=_cparams(),
    )(c_all, ada_w, ada_b_cols)


def _ada_bwd(c_all_t, dmod_cols):
    D = c_all_t.shape[0]
    L, _, n = dmod_cols.shape

    def body(c_ref, d_ref, o_ref):
        act = _silu(c_ref[...])
        dm = d_ref[0]
        acc = act[:, 0:1] * dm[0:1, :]
        for b in range(1, N_DEV):
            acc = acc + act[:, b:b + 1] * dm[b:b + 1, :]
        o_ref[0] = acc

    return pl.pallas_call(
        body, name="ada_bwd", grid=(L,),
        out_shape=jax.ShapeDtypeStruct((L, D, n), F32),
        in_specs=[pl.BlockSpec((D, N_DEV), lambda l: (0, 0)),
                  pl.BlockSpec((1, N_DEV, n), lambda l: (l, 0, 0))],
        out_specs=pl.BlockSpec((1, D, n), lambda l: (l, 0, 0)),
        compiler_params=_cparams(),
    )(c_all_t, dmod_cols)


def _rope_tables(pos_col, inv_lane):
    S = pos_col.shape[0]
    T = _row_tile(S)
    half = QK_ROPE // 2

    def body(p_ref, f_ref, c_ref, up_ref, dn_ref):
        ang = p_ref[...] * f_ref[...]
        lane = lax.broadcasted_iota(jnp.int32, ang.shape, 1)
        first = (lane >= QK_NOPE) & (lane < QK_NOPE + half)
        second = (lane >= QK_NOPE + half) & (lane < QK_NOPE + QK_ROPE)
        cs, sn = jnp.cos(ang), jnp.sin(ang)
        c_ref[...] = jnp.where(first | second, cs, 1.0)
        up_ref[...] = jnp.where(first, -sn, 0.0)
        dn_ref[...] = jnp.where(second, sn, 0.0)

    tab = jax.ShapeDtypeStruct((S, HEAD_PAD), F32)
    return pl.pallas_call(
        body, name="rope_tables", grid=(S // T,),
        out_shape=(tab, tab, tab),
        in_specs=[_rows(T, 1), _const((1, HEAD_PAD))],
        out_specs=(_rows(T, HEAD_PAD),) * 3,
        compiler_params=_cparams(),
    )(pos_col, inv_lane)


def _rope(blk, ct, ut, dt):
    half = QK_ROPE // 2
    up = pltpu.roll(blk, HEAD_PAD - half, 1)
    dn = pltpu.roll(blk, half, 1)
    return blk * ct + up * ut + dn * dt


def _rope_t(d, ct, ut, dt):
    half = QK_ROPE // 2
    return d * ct + pltpu.roll(d * ut, half, 1) + pltpu.roll(d * dt, HEAD_PAD - half, 1)


def _pre_norm(x, g, scale, shift, name):
    S, D = x.shape
    T = _row_tile(S)

    def body(x_ref, g_ref, sc_ref, sh_ref, h_ref):
        xv = x_ref[...]
        rstd = lax.rsqrt(jnp.mean(xv * xv, axis=-1, keepdims=True) + EPS)
        h_ref[...] = ((xv * rstd) * g_ref[...] * (1.0 + sc_ref[...]) + sh_ref[...]).astype(BF16)

    return pl.pallas_call(
        body, name=name, grid=(S // T,),
        out_shape=jax.ShapeDtypeStruct((S, D), BF16),
        in_specs=[_rows(T, D), _const((1, D)), _const((1, D)), _const((1, D))],
        out_specs=_rows(T, D), compiler_params=_cparams(),
    )(x, g, scale, shift)


def _post_norm(x, y, g, gate, name):
    S, D = x.shape
    T = _row_tile(S)

    def body(x_ref, y_ref, g_ref, gt_ref, o_ref):
        yv = y_ref[...]
        rstd = lax.rsqrt(jnp.mean(yv * yv, axis=-1, keepdims=True) + EPS)
        o_ref[...] = x_ref[...] + gt_ref[...] * ((yv * rstd) * g_ref[...])

    return pl.pallas_call(
        body, name=name, grid=(S // T,),
        out_shape=jax.ShapeDtypeStruct((S, D), F32),
        in_specs=[_rows(T, D), _rows(T, D), _const((1, D)), _const((1, D))],
        out_specs=_rows(T, D), compiler_params=_cparams(),
    )(x, y, g, gate)


def _fold8(v):
    T, C = v.shape
    return v.reshape(T // SUBLANES, SUBLANES, C).sum(axis=0)


def _col_sums(n_sums, body_fn, ins, in_specs, outs, out_specs, S, T, widths, name):
    n_in, n_out = len(ins), len(outs)
    nt = S // T

    def body(*refs):
        in_refs = refs[:n_in]
        out_refs = refs[n_in:n_in + n_out]
        sum_refs = refs[n_in + n_out:n_in + n_out + n_sums]
        accs = refs[n_in + n_out + n_sums:]
        i = pl.program_id(0)
        terms = body_fn(in_refs, out_refs)

        @pl.when(i == 0)
        def _():
            for acc, t in zip(accs, terms):
                acc[...] = _fold8(t)

        @pl.when(i > 0)
        def _():
            for acc, t in zip(accs, terms):
                acc[...] += _fold8(t)

        @pl.when(i == nt - 1)
        def _():
            for acc, s_ref in zip(accs, sum_refs):
                s_ref[...] = jnp.sum(acc[...], axis=0, keepdims=True)

    return pl.pallas_call(
        body, name=name, grid=(nt,),
        out_shape=tuple(outs) + tuple(jax.ShapeDtypeStruct((1, w), F32) for w in widths),
        in_specs=in_specs,
        out_specs=tuple(out_specs) + tuple(_const((1, w)) for w in widths),
        scratch_shapes=[pltpu.VMEM((SUBLANES, w), F32) for w in widths],
        compiler_params=_cparams(),
    )(*ins)


def _post_norm_bwd(dxo, y, g, gate, name):
    S, D = y.shape
    T = _row_tile(S)

    def fn(ins, outs):
        dxo_ref, y_ref, g_ref, gt_ref = ins
        yv, dv = y_ref[...], dxo_ref[...]
        rstd = lax.rsqrt(jnp.mean(yv * yv, axis=-1, keepdims=True) + EPS)
        yh = yv * rstd
        dn = dv * gt_ref[...]
        dyh = dn * g_ref[...]
        outs[0][...] = (rstd * (dyh - yh * jnp.mean(dyh * yh, axis=-1, keepdims=True))).astype(BF16)
        return [dv * (yh * g_ref[...]), dn * yh]

    return _col_sums(2, fn, [dxo, y, g, gate],
                     [_rows(T, D), _rows(T, D), _const((1, D)), _const((1, D))],
                     [jax.ShapeDtypeStruct((S, D), BF16)], [_rows(T, D)], S, T, [D, D], name)


def _pre_norm_bwd(dh, x, dxo, g, scale, name):
    S, D = x.shape
    T = _row_tile(S)

    def fn(ins, outs):
        dh_ref, x_ref, dxo_ref, g_ref, sc_ref = ins
        xv, dv = x_ref[...], dh_ref[...]
        rstd = lax.rsqrt(jnp.mean(xv * xv, axis=-1, keepdims=True) + EPS)
        xh = xv * rstd
        dr = dv * (1.0 + sc_ref[...])
        dxh = dr * g_ref[...]
        outs[0][...] = dxo_ref[...] + rstd * (dxh - xh * jnp.mean(dxh * xh, axis=-1, keepdims=True))
        return [dv, dv * (xh * g_ref[...]), dr * xh]

    return _col_sums(3, fn, [dh, x, dxo, g, scale],
                     [_rows(T, D), _rows(T, D), _rows(T, D), _const((1, D)), _const((1, D))],
                     [jax.ShapeDtypeStruct((S, D), F32)], [_rows(T, D)], S, T, [D, D, D], name)


def _loss_head(x, target):
    S, D = x.shape
    T = _row_tile(S)
    nt = S // T

    def body(x_ref, t_ref, l_ref, dx_ref, acc):
        i = pl.program_id(0)
        e = x_ref[...] - t_ref[...]
        dx_ref[...] = e * (1.0 / D)
        part = _fold8(e * e)

        @pl.when(i == 0)
        def _():
            acc[...] = part

        @pl.when(i > 0)
        def _():
            acc[...] += part

        @pl.when(i == nt - 1)
        def _():
            tot = jnp.sum(jnp.sum(acc[...], axis=0, keepdims=True), axis=1, keepdims=True)
            l_ref[...] = jnp.broadcast_to(tot * (0.5 / D), (1, LANES))

    return pl.pallas_call(
        body, name="loss_head", grid=(nt,),
        out_shape=(jax.ShapeDtypeStruct((1, LANES), F32), jax.ShapeDtypeStruct((S, D), F32)),
        in_specs=[_rows(T, D), _rows(T, D)],
        out_specs=(_const((1, LANES)), _rows(T, D)),
        scratch_shapes=[pltpu.VMEM((SUBLANES, D), F32)],
        compiler_params=_cparams(),
    )(x, target)


CONV_ROWS = 64


def _conv_halo(K):
    return SUBLANES if K - 1 <= SUBLANES else 32


def _conv_fwd(u, w, b, K, name):
    S, C = u.shape
    KP = w.shape[0]
    T, HB, RS = min(512, S), _conv_halo(K), CONV_ROWS
    ratio = T // HB

    def body(u_ref, h_ref, w_ref, b_ref, o_ref, ext):
        i = pl.program_id(1)
        ext[0:HB, :] = jnp.where(i > 0, h_ref[...], 0.0)
        ext[HB:HB + T, :] = u_ref[...]
        for r0 in range(0, T, RS):
            acc = jnp.broadcast_to(b_ref[...], (RS, LANES))
            for k in range(K):
                off = HB - (K - 1) + k + r0
                acc = acc + w_ref[k:k + 1, :] * ext[off:off + RS, :]
            o_ref[r0:r0 + RS, :] = acc

    return pl.pallas_call(
        body, name=name, grid=(C // LANES, S // T),
        out_shape=jax.ShapeDtypeStruct((S, C), F32),
        in_specs=[pl.BlockSpec((T, LANES), lambda c, i: (i, c)),
                  pl.BlockSpec((HB, LANES), lambda c, i: (jnp.maximum(i * ratio - 1, 0), c)),
                  pl.BlockSpec((KP, LANES), lambda c, i: (0, c)),
                  pl.BlockSpec((1, LANES), lambda c, i: (0, c))],
        out_specs=pl.BlockSpec((T, LANES), lambda c, i: (i, c)),
        scratch_shapes=[pltpu.VMEM((HB + T, LANES), F32)],
        compiler_params=_cparams(),
    )(u, u, w, b)


def _conv_bwd(d, u, w, K, name):
    S, C = u.shape
    KP = w.shape[0]
    T, HB, RS = min(512, S), _conv_halo(K), CONV_ROWS
    ratio = T // HB
    nt = S // T
    last_halo = S // HB - 1

    def body(d_ref, dn_ref, u_ref, up_ref, w_ref, du_ref, dw_ref, db_ref, extd, extu, dws, dbs):
        i = pl.program_id(1)
        extd[0:T, :] = d_ref[...]
        extd[T:T + HB, :] = jnp.where(i < nt - 1, dn_ref[...], 0.0)
        extu[0:HB, :] = jnp.where(i > 0, up_ref[...], 0.0)
        extu[HB:HB + T, :] = u_ref[...]

        @pl.when(i == 0)
        def _():
            dws[...] = jnp.zeros_like(dws)
            dbs[...] = jnp.zeros_like(dbs)

        for r0 in range(0, T, RS):
            acc = jnp.zeros((RS, LANES), F32)
            for k in range(K):
                off = (K - 1 - k) + r0
                acc = acc + w_ref[k:k + 1, :] * extd[off:off + RS, :]
            du_ref[r0:r0 + RS, :] = acc
            dch = d_ref[r0:r0 + RS, :]
            dbs[...] += _fold8(dch)
            for k in range(K):
                off = HB - (K - 1) + k + r0
                dws[k * SUBLANES:(k + 1) * SUBLANES, :] += _fold8(dch * extu[off:off + RS, :])

        @pl.when(i == nt - 1)
        def _():
            dw_ref[...] = jnp.zeros_like(dw_ref)
            for k in range(K):
                dw_ref[k:k + 1, :] = jnp.sum(dws[k * SUBLANES:(k + 1) * SUBLANES, :], axis=0, keepdims=True)
            db_ref[...] = jnp.sum(dbs[...], axis=0, keepdims=True)

    return pl.pallas_call(
        body, name=name, grid=(C // LANES, nt),
        out_shape=(jax.ShapeDtypeStruct((S, C), F32), jax.ShapeDtypeStruct((KP, C), F32),
                   jax.ShapeDtypeStruct((1, C), F32)),
        in_specs=[pl.BlockSpec((T, LANES), lambda c, i: (i, c)),
                  pl.BlockSpec((HB, LANES), lambda c, i: (jnp.minimum((i + 1) * ratio, last_halo), c)),
                  pl.BlockSpec((T, LANES), lambda c, i: (i, c)),
                  pl.BlockSpec((HB, LANES), lambda c, i: (jnp.maximum(i * ratio - 1, 0), c)),
                  pl.BlockSpec((KP, LANES), lambda c, i: (0, c))],
        out_specs=(pl.BlockSpec((T, LANES), lambda c, i: (i, c)),
                   pl.BlockSpec((KP, LANES), lambda c, i: (0, c)),
                   pl.BlockSpec((1, LANES), lambda c, i: (0, c))),
        scratch_shapes=[pltpu.VMEM((T + HB, LANES), F32), pltpu.VMEM((HB + T, LANES), F32),
                        pltpu.VMEM((KP * SUBLANES, LANES), F32), pltpu.VMEM((SUBLANES, LANES), F32)],
        compiler_params=_cparams(),
    )(d, d, u, u, w)


SCW = 512
ZE = 3072
QL = 256
KVL = 128


def _rms_rows(x, g):
    rstd = lax.rsqrt(jnp.mean(x * x, axis=-1, keepdims=True) + EPS)
    return (x * rstd) * g


def _even_pre(z, qg, kvg, name):
    S = z.shape[0]
    T = _row_tile(S)

    def body(ac_ref, ax_ref, cq_ref, ckv_ref, qg_ref, kvg_ref, u_ref, qn_ref, kvn_ref):
        u_ref[...] = ac_ref[...] * ax_ref[...]
        qn_ref[...] = _rms_rows(cq_ref[...], qg_ref[...]).astype(BF16)
        kvn_ref[...] = _rms_rows(ckv_ref[...], kvg_ref[...]).astype(BF16)

    return pl.pallas_call(
        body, name=name, grid=(S // T,),
        out_shape=(jax.ShapeDtypeStruct((S, SCW), F32), jax.ShapeDtypeStruct((S, QL), BF16),
                   jax.ShapeDtypeStruct((S, KVL), BF16)),
        in_specs=[_rows(T, SCW, 1), _rows(T, SCW, 2), _rows(T, QL, 10), _rows(T, KVL, 22),
                  _const((1, QL)), _const((1, KVL))],
        out_specs=(_rows(T, SCW), _rows(T, QL), _rows(T, KVL)),
        compiler_params=_cparams(),
    )(z, z, z, z, qg, kvg)


def _qkv_fwd(qn, kvn, z, tabs, w_q, w_kv, name):
    S = qn.shape[0]
    T = _row_tile(S)
    HW = HEADS * HEAD_PAD
    scale = 1.0 / math.sqrt(QK_NOPE + QK_ROPE)

    def body(qn_ref, kvn_ref, kr_ref, ct_ref, ut_ref, dt_ref, wq_ref, wkv_ref, q_ref, k_ref, v_ref):
        ct, ut, dt = ct_ref[...], ut_ref[...], dt_ref[...]
        qa = jnp.dot(qn_ref[...], wq_ref[...], preferred_element_type=F32)
        kva = jnp.dot(kvn_ref[...], wkv_ref[...], preferred_element_type=F32)
        kr = kr_ref[...]
        for h in range(HEADS):
            sl = slice(h * HEAD_PAD, (h + 1) * HEAD_PAD)
            q_ref[:, sl] = (_rope(qa[:, sl], ct, ut, dt) * scale).astype(BF16)
            k_ref[:, sl] = _rope(kva[:, sl] + kr, ct, ut, dt).astype(BF16)
        v_ref[...] = kva[:, HW:].astype(BF16)

    return pl.pallas_call(
        body, name=name, grid=(S // T,),
        out_shape=(jax.ShapeDtypeStruct((S, HW), BF16), jax.ShapeDtypeStruct((S, HW), BF16),
                   jax.ShapeDtypeStruct((S, HEADS * V_HEAD), BF16)),
        in_specs=[_rows(T, QL), _rows(T, KVL), _rows(T, HEAD_PAD, 23),
                  _rows(T, HEAD_PAD), _rows(T, HEAD_PAD), _rows(T, HEAD_PAD),
                  _const(w_q.shape), _const(w_kv.shape)],
        out_specs=(_rows(T, HW), _rows(T, HW), _rows(T, HEADS * V_HEAD)),
        compiler_params=_cparams(),
    )(qn, kvn, z, *tabs, w_q, w_kv)


def _attn_tile(S):
    return min(256, S)


def _chunk_mask(TQ):
    r = lax.broadcasted_iota(jnp.int32, (TQ, TQ), 0) // CHUNK
    c = lax.broadcasted_iota(jnp.int32, (TQ, TQ), 1) // CHUNK
    return c <= r


_NT = (((1,), (1,)), ((), ()))
_TN = (((0,), (0,)), ((), ()))


def _attn_fwd(q, k, v, name):
    S = q.shape[0]
    TQ = _attn_tile(S)
    nq = S // TQ
    PW = 2 * HEAD_PAD

    def body(q_ref, k_ref, v_ref, o_ref, lse_ref, m_s, l_s, acc_s):
        i = pl.program_id(1)
        left = lax.broadcasted_iota(jnp.int32, (TQ, LANES), 1) < V_HEAD
        m_s[...] = jnp.full_like(m_s, NEG)
        l_s[...] = jnp.zeros_like(l_s)
        acc_s[...] = jnp.zeros_like(acc_s)
        qv = q_ref[...]

        def step(j, masked):
            r0 = pl.multiple_of(j * TQ, TQ)
            kb = k_ref[pl.ds(r0, TQ), :]
            vb = v_ref[pl.ds(r0, TQ), :]
            alphas, pvs = [], []
            for h in range(2):
                sl = slice(h * HEAD_PAD, (h + 1) * HEAD_PAD)
                s = lax.dot_general(qv[:, sl], kb[:, sl], _NT, preferred_element_type=F32)
                if masked:
                    s = jnp.where(_chunk_mask(TQ), s, NEG)
                m_prev = m_s[h]
                m_new = jnp.maximum(m_prev, jnp.max(s, axis=1, keepdims=True))
                alpha = jnp.exp(m_prev - m_new)
                p = jnp.exp(s - m_new[:, 0:1])
                l_s[h] = alpha * l_s[h] + jnp.sum(p, axis=1, keepdims=True)
                m_s[h] = m_new
                alphas.append(alpha)
                pvs.append(jnp.dot(p.astype(BF16), vb, preferred_element_type=F32))
            acc_s[...] = acc_s[...] * jnp.where(left, alphas[0], alphas[1]) + jnp.where(left, pvs[0], pvs[1])

        def loop_body(j, carry):
            step(j, False)
            return carry

        lax.fori_loop(0, i, loop_body, 0)
        step(i, True)
        o_ref[...] = acc_s[...] / jnp.where(left, l_s[0], l_s[1])
        lse_ref[...] = jnp.where(left, m_s[0] + jnp.log(l_s[0]), m_s[1] + jnp.log(l_s[1]))

    return pl.pallas_call(
        body, name=name, grid=(HEADS // 2, nq),
        out_shape=(jax.ShapeDtypeStruct((S, HEADS * V_HEAD), F32), jax.ShapeDtypeStruct((S, HEADS * V_HEAD), F32)),
        in_specs=[pl.BlockSpec((TQ, PW), lambda p, i: (i, p)),
                  pl.BlockSpec((S, PW), lambda p, i: (0, p)),
                  pl.BlockSpec((S, LANES), lambda p, i: (0, p))],
        out_specs=(pl.BlockSpec((TQ, LANES), lambda p, i: (i, p)),
                   pl.BlockSpec((TQ, LANES), lambda p, i: (i, p))),
        scratch_shapes=[pltpu.VMEM((2, TQ, LANES), F32), pltpu.VMEM((2, TQ, LANES), F32),
                        pltpu.VMEM((TQ, LANES), F32)],
        compiler_params=_cparams(),
    )(q, k, v)


def _attn_dq(q, k, v, do, lse, delta, name):
    S = q.shape[0]
    TQ = _attn_tile(S)
    nq = S // TQ
    PW = 2 * HEAD_PAD

    def body(q_ref, k_ref, v_ref, do_ref, lse_ref, dl_ref, dq_ref, acc_s):
        i = pl.program_id(1)
        left = lax.broadcasted_iota(jnp.int32, (TQ, LANES), 1) < V_HEAD
        acc_s[...] = jnp.zeros_like(acc_s)
        qv = q_ref[...]
        dov = do_ref[...]
        dos = [jnp.where(left, dov, jnp.zeros_like(dov)), jnp.where(left, jnp.zeros_like(dov), dov)]
        lses = [lse_ref[:, 0:1], lse_ref[:, V_HEAD:V_HEAD + 1]]
        dls = [dl_ref[:, 0:1], dl_ref[:, V_HEAD:V_HEAD + 1]]

        def step(j, masked):
            r0 = pl.multiple_of(j * TQ, TQ)
            kb = k_ref[pl.ds(r0, TQ), :]
            vb = v_ref[pl.ds(r0, TQ), :]
            for h in range(2):
                sl = slice(h * HEAD_PAD, (h + 1) * HEAD_PAD)
                s = lax.dot_general(qv[:, sl], kb[:, sl], _NT, preferred_element_type=F32)
                p = jnp.exp(s - lses[h])
                if masked:
                    p = jnp.where(_chunk_mask(TQ), p, 0.0)
                dp = lax.dot_general(dos[h], vb, _NT, preferred_element_type=F32)
                ds = (p * (dp - dls[h])).astype(BF16)
                acc_s[:, sl] += jnp.dot(ds, kb[:, sl], preferred_element_type=F32)

        def loop_body(j, carry):
            step(j, False)
            return carry

        lax.fori_loop(0, i, loop_body, 0)
        step(i, True)
        dq_ref[...] = acc_s[...]

    return pl.pallas_call(
        body, name=name, grid=(HEADS // 2, nq),
        out_shape=jax.ShapeDtypeStruct((S, HEADS * HEAD_PAD), F32),
        in_specs=[pl.BlockSpec((TQ, PW), lambda p, i: (i, p)),
                  pl.BlockSpec((S, PW), lambda p, i: (0, p)),
                  pl.BlockSpec((S, LANES), lambda p, i: (0, p)),
                  pl.BlockSpec((TQ, LANES), lambda p, i: (i, p)),
                  pl.BlockSpec((TQ, LANES), lambda p, i: (i, p)),
                  pl.BlockSpec((TQ, LANES), lambda p, i: (i, p))],
        out_specs=pl.BlockSpec((TQ, PW), lambda p, i: (i, p)),
        scratch_shapes=[pltpu.VMEM((TQ, PW), F32)],
        compiler_params=_cparams(),
    )(q, k, v, do, lse, delta)


def _attn_dkv(q, k, v, do, lse, delta, name):
    S = q.shape[0]
    TQ = _attn_tile(S)
    nq = S // TQ
    PW = 2 * HEAD_PAD

    def body(q_ref, k_ref, v_ref, do_ref, lse_ref, dl_ref, dk_ref, dv_ref, dk_s, dv_s):
        j = pl.program_id(1)
        left = lax.broadcasted_iota(jnp.int32, (TQ, LANES), 1) < V_HEAD
        dk_s[...] = jnp.zeros_like(dk_s)
        dv_s[...] = jnp.zeros_like(dv_s)
        kb = k_ref[...]
        vb = v_ref[...]

        def step(i, masked):
            r0 = pl.multiple_of(i * TQ, TQ)
            qb = q_ref[pl.ds(r0, TQ), :]
            dov = do_ref[pl.ds(r0, TQ), :]
            lse = lse_ref[pl.ds(r0, TQ), :]
            dl = dl_ref[pl.ds(r0, TQ), :]
            dos = [jnp.where(left, dov, jnp.zeros_like(dov)), jnp.where(left, jnp.zeros_like(dov), dov)]
            for h in range(2):
                sl = slice(h * HEAD_PAD, (h + 1) * HEAD_PAD)
                c0 = h * V_HEAD
                s = lax.dot_general(qb[:, sl], kb[:, sl], _NT, preferred_element_type=F32)
                p = jnp.exp(s - lse[:, c0:c0 + 1])
                if masked:
                    p = jnp.where(_chunk_mask(TQ), p, 0.0)
                dv_s[...] += lax.dot_general(p.astype(BF16), dos[h], _TN, preferred_element_type=F32)
                dp = lax.dot_general(dos[h], vb, _NT, preferred_element_type=F32)
                ds = (p * (dp - dl[:, c0:c0 + 1])).astype(BF16)
                dk_s[:, sl] += lax.dot_general(ds, qb[:, sl], _TN, preferred_element_type=F32)

        def loop_body(i, carry):
            step(i, False)
            return carry

        step(j, True)
        lax.fori_loop(j + 1, nq, loop_body, 0)
        dk_ref[...] = dk_s[...]
        dv_ref[...] = dv_s[...]

    return pl.pallas_call(
        body, name=name, grid=(HEADS // 2, nq),
        out_shape=(jax.ShapeDtypeStruct((S, HEADS * HEAD_PAD), F32), jax.ShapeDtypeStruct((S, HEADS * V_HEAD), F32)),
        in_specs=[pl.BlockSpec((S, PW), lambda p, j: (0, p)),
                  pl.BlockSpec((TQ, PW), lambda p, j: (j, p)),
                  pl.BlockSpec((TQ, LANES), lambda p, j: (j, p)),
                  pl.BlockSpec((S, LANES), lambda p, j: (0, p)),
                  pl.BlockSpec((S, LANES), lambda p, j: (0, p)),
                  pl.BlockSpec((S, LANES), lambda p, j: (0, p))],
        out_specs=(pl.BlockSpec((TQ, PW), lambda p, j: (j, p)),
                   pl.BlockSpec((TQ, LANES), lambda p, j: (j, p))),
        scratch_shapes=[pltpu.VMEM((TQ, PW), F32), pltpu.VMEM((TQ, LANES), F32)],
        compiler_params=_cparams(),
    )(q, k, v, do, lse, delta)


def _even_post(z, cv, o, name):
    S = z.shape[0]
    T = _row_tile(S)

    def body(ab_ref, ag_ref, bg_ref, cv_ref, o_ref, y_ref):
        y_ref[:, 0:SCW] = (ab_ref[...] * cv_ref[...] * _silu(ag_ref[...])).astype(BF16)
        y_ref[:, SCW:2 * SCW] = (o_ref[...] * _silu(bg_ref[...])).astype(BF16)

    return pl.pallas_call(
        body, name=name, grid=(S // T,),
        out_shape=jax.ShapeDtypeStruct((S, 2 * SCW), BF16),
        in_specs=[_rows(T, SCW, 0), _rows(T, SCW, 3), _rows(T, SCW, 4), _rows(T, SCW), _rows(T, SCW)],
        out_specs=_rows(T, 2 * SCW), compiler_params=_cparams(),
    )(z, z, z, cv, o)


def _even_bwd_gates(dyc, z, cv, o, name):
    S = z.shape[0]
    T = _row_tile(S)

    def body(dya_ref, dyb_ref, ab_ref, ag_ref, bg_ref, cv_ref, o_ref,
             dab_ref, dag_ref, dbg_ref, dcv_ref, do_ref, dl_ref):
        dya, ab, ag, cv = dya_ref[...], ab_ref[...], ag_ref[...], cv_ref[...]
        sg = _silu(ag)
        dab_ref[...] = (dya * cv * sg).astype(BF16)
        dcv_ref[...] = dya * ab * sg
        dag_ref[...] = (dya * ab * cv * _dsilu(ag)).astype(BF16)
        dyb, bg, ov = dyb_ref[...], bg_ref[...], o_ref[...]
        dov = dyb * _silu(bg)
        do_ref[...] = dov.astype(BF16)
        dbg_ref[...] = (dyb * ov * _dsilu(bg)).astype(BF16)
        prod = dov * ov
        left = lax.broadcasted_iota(jnp.int32, (T, LANES), 1) < V_HEAD
        for p in range(HEADS // 2):
            blk = prod[:, p * LANES:(p + 1) * LANES]
            s0 = jnp.sum(jnp.where(left, blk, 0.0), axis=1, keepdims=True)
            s1 = jnp.sum(jnp.where(left, 0.0, blk), axis=1, keepdims=True)
            dl_ref[:, p * LANES:(p + 1) * LANES] = jnp.where(left, s0, s1)

    bf = jax.ShapeDtypeStruct((S, SCW), BF16)
    ff = jax.ShapeDtypeStruct((S, SCW), F32)
    return pl.pallas_call(
        body, name=name, grid=(S // T,),
        out_shape=(bf, bf, bf, ff, bf, ff),
        in_specs=[_rows(T, SCW, 0), _rows(T, SCW, 1), _rows(T, SCW, 0), _rows(T, SCW, 3), _rows(T, SCW, 4),
                  _rows(T, SCW), _rows(T, SCW)],
        out_specs=(_rows(T, SCW),) * 6, compiler_params=_cparams(),
    )(dyc, dyc, z, z, z, cv, o)


def _qkv_bwd(dq, dk, dv, z, tabs, w_q, w_kv, qg, kvg, name):
    S = dq.shape[0]
    T = _row_tile(S)
    HW = HEADS * HEAD_PAD
    VW = HEADS * V_HEAD
    scale = 1.0 / math.sqrt(QK_NOPE + QK_ROPE)

    def fn(ins, outs):
        dq_ref, dk_ref, dv_ref, cq_ref, ckv_ref, ct_ref, ut_ref, dt_ref, wq_ref, wkv_ref, qg_ref, kvg_ref = ins
        dqp_ref, dkvp_ref, dcq_ref, dckv_ref, dkr_ref = outs
        ct, ut, dt = ct_ref[...], ut_ref[...], dt_ref[...]
        dkr = jnp.zeros((T, HEAD_PAD), F32)
        for h in range(HEADS):
            sl = slice(h * HEAD_PAD, (h + 1) * HEAD_PAD)
            dqp_ref[:, sl] = (_rope_t(dq_ref[:, sl], ct, ut, dt) * scale).astype(BF16)
            dkh = _rope_t(dk_ref[:, sl], ct, ut, dt)
            dkr = dkr + dkh
            dkvp_ref[:, sl] = dkh.astype(BF16)
        dkvp_ref[:, HW:] = dv_ref[...].astype(BF16)
        dkr_ref[...] = dkr.astype(BF16)
        sums = []
        for lat_ref, g_ref, dpre_ref, w_ref, dlat_ref in ((cq_ref, qg_ref, dqp_ref, wq_ref, dcq_ref),
                                                         (ckv_ref, kvg_ref, dkvp_ref, wkv_ref, dckv_ref)):
            dn = lax.dot_general(dpre_ref[...], w_ref[...], _NT, preferred_element_type=F32)
            xv = lat_ref[...]
            rstd = lax.rsqrt(jnp.mean(xv * xv, axis=-1, keepdims=True) + EPS)
            xh = xv * rstd
            dxh = dn * g_ref[...]
            dlat_ref[...] = (rstd * (dxh - xh * jnp.mean(dxh * xh, axis=-1, keepdims=True))).astype(BF16)
            sums.append(dn * xh)
        return sums

    return _col_sums(
        2, fn, [dq, dk, dv, z, z, *tabs, w_q, w_kv, qg, kvg],
        [_rows(T, HW), _rows(T, HW), _rows(T, VW), _rows(T, QL, 10), _rows(T, KVL, 22),
         _rows(T, HEAD_PAD), _rows(T, HEAD_PAD), _rows(T, HEAD_PAD),
         _const(w_q.shape), _const(w_kv.shape), _const((1, QL)), _const((1, KVL))],
        [jax.ShapeDtypeStruct((S, HW), BF16), jax.ShapeDtypeStruct((S, HW + VW), BF16),
         jax.ShapeDtypeStruct((S, QL), BF16), jax.ShapeDtypeStruct((S, KVL), BF16),
         jax.ShapeDtypeStruct((S, HEAD_PAD), BF16)],
        [_rows(T, HW), _rows(T, HW + VW), _rows(T, QL), _rows(T, KVL), _rows(T, HEAD_PAD)],
        S, T, [QL, KVL], name)


def _even_dz(dab, du, z, dag, dbg, dcq, dckv, dkr, name):
    S = z.shape[0]
    T = _row_tile(S)

    def body(dab_ref, du_ref, ac_ref, ax_ref, dag_ref, dbg_ref, dcq_ref, dckv_ref, dkr_ref, dz_ref):
        duv = du_ref[...]
        dz_ref[:, 0:SCW] = dab_ref[...]
        dz_ref[:, SCW:2 * SCW] = (duv * ax_ref[...]).astype(BF16)
        dz_ref[:, 2 * SCW:3 * SCW] = (duv * ac_ref[...]).astype(BF16)
        dz_ref[:, 3 * SCW:4 * SCW] = dag_ref[...]
        dz_ref[:, 4 * SCW:5 * SCW] = dbg_ref[...]
        dz_ref[:, 5 * SCW:5 * SCW + QL] = dcq_ref[...]
        dz_ref[:, 5 * SCW + QL:5 * SCW + QL + KVL] = dckv_ref[...]
        dz_ref[:, 5 * SCW + QL + KVL:ZE] = dkr_ref[...]

    return pl.pallas_call(
        body, name=name, grid=(S // T,),
        out_shape=jax.ShapeDtypeStruct((S, ZE), BF16),
        in_specs=[_rows(T, SCW), _rows(T, SCW), _rows(T, SCW, 1), _rows(T, SCW, 2), _rows(T, SCW), _rows(T, SCW),
                  _rows(T, QL), _rows(T, KVL), _rows(T, HEAD_PAD)],
        out_specs=_rows(T, ZE), compiler_params=_cparams(),
    )(dab, du, z, z, dag, dbg, dcq, dckv, dkr)


def _odd_pre(z, name):
    S, D = z.shape[0], z.shape[1] // 3
    T = _row_tile(S)

    def body(val_ref, glu_ref, u_ref):
        u_ref[...] = val_ref[...] * _sigmoid(glu_ref[...])

    return pl.pallas_call(
        body, name=name, grid=(S // T,),
        out_shape=jax.ShapeDtypeStruct((S, D), F32),
        in_specs=[_rows(T, D, 0), _rows(T, D, 1)], out_specs=_rows(T, D),
        compiler_params=_cparams(),
    )(z, z)


def _layer_norm_stats(cv):
    mu = jnp.mean(cv, axis=-1, keepdims=True)
    cen = cv - mu
    rstd = lax.rsqrt(jnp.mean(cen * cen, axis=-1, keepdims=True) + EPS)
    return cen * rstd, rstd


def _odd_post(cv, z, ln_g, ln_b, name):
    S, D = cv.shape
    T = _row_tile(S)

    def body(cv_ref, sg_ref, g_ref, b_ref, y_ref):
        cvh, _ = _layer_norm_stats(cv_ref[...])
        y_ref[...] = (_silu(cvh * g_ref[...] + b_ref[...]) * _silu(sg_ref[...])).astype(BF16)

    return pl.pallas_call(
        body, name=name, grid=(S // T,),
        out_shape=jax.ShapeDtypeStruct((S, D), BF16),
        in_specs=[_rows(T, D), _rows(T, D, 2), _const((1, D)), _const((1, D))],
        out_specs=_rows(T, D), compiler_params=_cparams(),
    )(cv, z, ln_g, ln_b)


def _odd_bwd_norm(dyi, cv, z, ln_g, ln_b, name):
    S, D = cv.shape
    T = _row_tile(S)

    def fn(ins, outs):
        dy_ref, cv_ref, sg_ref, g_ref, b_ref = ins
        dcv_ref, dsg_ref = outs
        cvh, rstd = _layer_norm_stats(cv_ref[...])
        ln = cvh * g_ref[...] + b_ref[...]
        sgv, dy = sg_ref[...], dy_ref[...]
        dsg_ref[...] = (dy * _silu(ln) * _dsilu(sgv)).astype(BF16)
        dln = dy * _silu(sgv) * _dsilu(ln)
        dh = dln * g_ref[...]
        dcv_ref[...] = rstd * (dh - jnp.mean(dh, axis=-1, keepdims=True)
                               - cvh * jnp.mean(dh * cvh, axis=-1, keepdims=True))
        return [dln * cvh, dln]

    return _col_sums(2, fn, [dyi, cv, z, ln_g, ln_b],
                     [_rows(T, D), _rows(T, D), _rows(T, D, 2), _const((1, D)), _const((1, D))],
                     [jax.ShapeDtypeStruct((S, D), F32), jax.ShapeDtypeStruct((S, D), BF16)],
                     [_rows(T, D), _rows(T, D)], S, T, [D, D], name)


def _odd_dz(du, z, dsg, name):
    S, D = du.shape
    T = _row_tile(S)

    def body(du_ref, val_ref, glu_ref, dsg_ref, dz_ref):
        duv = du_ref[...]
        sig = _sigmoid(glu_ref[...])
        dz_ref[:, 0:D] = (duv * sig).astype(BF16)
        dz_ref[:, D:2 * D] = (duv * val_ref[...] * sig * (1.0 - sig)).astype(BF16)
        dz_ref[:, 2 * D:3 * D] = dsg_ref[...]

    return pl.pallas_call(
        body, name=name, grid=(S // T,),
        out_shape=jax.ShapeDtypeStruct((S, 3 * D), BF16),
        in_specs=[_rows(T, D), _rows(T, D, 0), _rows(T, D, 1), _rows(T, D)],
        out_specs=_rows(T, 3 * D), compiler_params=_cparams(),
    )(du, z, z, dsg)


ADAM_ROWS = 128


def _adamw(g_parts, w, m, v, name):
    P, R, C = g_parts.shape
    TR = min(ADAM_ROWS, R)
    assert R % TR == 0

    def body(g_ref, w_ref, m_ref, v_ref, go_ref, d_ref, mo_ref, vo_ref):
        g = g_ref[0]
        for p in range(1, P):
            g = g + g_ref[p]
        mn = ADAM_B1 * m_ref[...] + (1.0 - ADAM_B1) * g
        vn = ADAM_B2 * v_ref[...] + (1.0 - ADAM_B2) * (g * g)
        m_hat = mn / (1.0 - ADAM_B1 ** ADAM_STEP)
        v_hat = vn / (1.0 - ADAM_B2 ** ADAM_STEP)
        go_ref[...] = g
        d_ref[...] = -ADAM_LR * (m_hat / (jnp.sqrt(v_hat) + ADAM_EPS) + ADAM_WD * w_ref[...])
        mo_ref[...] = mn
        vo_ref[...] = vn

    slab = jax.ShapeDtypeStruct((R, C), F32)
    return pl.pallas_call(
        body, name=name, grid=(R // TR,),
        out_shape=(slab,) * 4,
        in_specs=[pl.BlockSpec((P, TR, C), lambda i: (0, i, 0))] + [_rows(TR, C)] * 3,
        out_specs=(_rows(TR, C),) * 4, compiler_params=_cparams(),
    )(g_parts, w, m, v)


def _gather_cols(g, shape):
    nd = len(shape)
    t = jnp.moveaxis(g, 0, nd - 1)
    return t.reshape(tuple(shape[:-1]) + (N_DEV * shape[-1],))


def _scatter_cols(full, n):
    t = full.reshape(full.shape[:-1] + (N_DEV, n))
    return jnp.moveaxis(t, -2, 0)


def kernel(x, c, positions, ada_w, ada_b, pre_norm_g, post_norm_g, even_w_in, even_sc_conv_w, even_sc_conv_b, even_q_norm_g, even_kv_norm_g, even_w_uq, even_w_ukv, even_w_out, odd_w_in, odd_conv_w, odd_conv_b, odd_ln_g, odd_ln_b, odd_w_out, loss_target, m_ada_w, m_ada_b, m_pre_norm_g, m_post_norm_g, m_even_w_in, m_even_sc_conv_w, m_even_sc_conv_b, m_even_q_norm_g, m_even_kv_norm_g, m_even_w_uq, m_even_w_ukv, m_even_w_out, m_odd_w_in, m_odd_conv_w, m_odd_conv_b, m_odd_ln_g, m_odd_ln_b, m_odd_w_out, v_ada_w, v_ada_b, v_pre_norm_g, v_post_norm_g, v_even_w_in, v_even_sc_conv_w, v_even_sc_conv_b, v_even_q_norm_g, v_even_kv_norm_g, v_even_w_uq, v_even_w_ukv, v_even_w_out, v_odd_w_in, v_odd_conv_w, v_odd_conv_b, v_odd_ln_g, v_odd_ln_b, v_odd_w_out):
    S, D = x.shape[1], x.shape[2]
    L = ada_w.shape[0]
    NE, NO = even_w_in.shape[0], odd_w_in.shape[0]
    me = 4 * lax.axis_index("x") + 2 * lax.axis_index("y") + lax.axis_index("c")
    x0 = x[0]
    target = loss_target[0]

    small_parts = [c, even_sc_conv_w, odd_conv_w, odd_conv_b, odd_ln_g, odd_ln_b]
    small_shapes = [p.shape for p in small_parts]
    sg = _exchange(_pack(small_parts, F32, SUBLANES), False, "gather_small").reshape(N_DEV, -1)
    c_all, scw_g, ocw_g, ocb_g, olg_g, olb_g = _unpack(sg, small_shapes)
    c_all = c_all.reshape(N_DEV, D)
    sc_conv_w = _gather_cols(scw_g, even_sc_conv_w.shape)
    o_conv_w = _gather_cols(ocw_g, odd_conv_w.shape)
    o_conv_b = _gather_cols(ocb_g, odd_conv_b.shape)
    o_ln_g = _gather_cols(olg_g, odd_ln_g.shape)
    o_ln_b = _gather_cols(olb_g, odd_ln_b.shape)

    big_parts = [even_w_in, even_w_uq, even_w_ukv, even_w_out, odd_w_in, odd_w_out]
    big_shapes = [p.shape for p in big_parts]
    bg = _exchange(_pack(big_parts, BF16, 2 * SUBLANES), False, "gather_weights").reshape(N_DEV, -1)
    ewin_g, euq_g, eukv_g, eout_g, owin_g, oout_g = _unpack(bg, big_shapes)
    e_w_in = _gather_cols(ewin_g, even_w_in.shape)
    e_w_uq = _gather_cols(euq_g, even_w_uq.shape)
    e_w_ukv = _gather_cols(eukv_g, even_w_ukv.shape)
    e_w_out = jnp.moveaxis(eout_g, 0, 1).reshape(NE, -1, D)
    o_w_in = _gather_cols(owin_g, odd_w_in.shape)
    o_w_out = jnp.moveaxis(oout_g, 0, 1).reshape(NO, -1, D)

    zc = lambda n: jnp.zeros((NE, D, n), BF16)
    e_w_in_k = jnp.concatenate([e_w_in[..., :2048], e_w_in[..., 2464:2976], e_w_in[..., 2048:2432],
                                zc(QK_NOPE), e_w_in[..., 2432:2464], zc(HEAD_PAD - QK_NOPE - QK_ROPE)], axis=-1)
    e_w_q_k = jnp.pad(e_w_uq.reshape(NE, QL, HEADS, QK_NOPE + QK_ROPE),
                      ((0, 0), (0, 0), (0, 0), (0, HEAD_PAD - QK_NOPE - QK_ROPE))).reshape(NE, QL, HEADS * HEAD_PAD)
    ukv = e_w_ukv.reshape(NE, KVL, HEADS, QK_NOPE + V_HEAD)
    e_w_kv_k = jnp.concatenate([
        jnp.pad(ukv[..., :QK_NOPE], ((0, 0), (0, 0), (0, 0), (0, HEAD_PAD - QK_NOPE))).reshape(NE, KVL, HEADS * HEAD_PAD),
        ukv[..., QK_NOPE:].reshape(NE, KVL, HEADS * V_HEAD)], axis=-1)

    n_ada = ada_w.shape[2]
    ada_b_cols = lax.dynamic_slice_in_dim(ada_b, me * n_ada, n_ada, axis=1).reshape(L, 1, n_ada)
    mod_slab = _ada_fwd(c_all, ada_w, ada_b_cols)
    mod_g = _exchange(_pack([mod_slab], F32, SUBLANES), False, "gather_mod").reshape(N_DEV, -1)
    mod_all = mod_g[:, :L * N_DEV * n_ada].reshape(N_DEV, L, N_DEV, n_ada)
    mod = lax.dynamic_index_in_dim(mod_all, me, axis=2, keepdims=False)
    mod = jnp.moveaxis(mod, 0, 1).reshape(L, 3 * D)
    shift, scale, gate = mod[:, :D], mod[:, D:2 * D], mod[:, 2 * D:]

    half = QK_ROPE // 2
    inv_freq = 1.0 / (ROPE_THETA ** (jnp.arange(0, QK_ROPE, 2, dtype=F32) / QK_ROPE))
    inv_lane = jnp.zeros((HEAD_PAD,), F32).at[QK_NOPE:QK_NOPE + QK_ROPE].set(jnp.concatenate([inv_freq, inv_freq]))
    tabs = _rope_tables(positions.astype(F32).reshape(S, 1), inv_lane.reshape(1, HEAD_PAD))
    del half

    row = lambda a: a.reshape(1, -1)
    scb = even_sc_conv_b
    KP3, KP31 = SUBLANES, 32

    saved = []
    xs = x0
    for layer in range(L):
        i = layer // 2
        tag = f"l{layer}"
        h = _pre_norm(xs, row(pre_norm_g[layer]), row(scale[layer]), row(shift[layer]), f"pre_norm_{tag}")
        if layer % 2 == 0:
            z = _matmul(h, e_w_in_k[i], "nn", F32, f"w_in_{tag}", tn=1024)
            u, qn, kvn = _even_pre(z, row(even_q_norm_g[i]), row(even_kv_norm_g[i]), f"even_pre_{tag}")
            cw = jnp.pad(sc_conv_w[i], ((0, KP3 - SC_KERNEL), (0, 0)))
            cv = _conv_fwd(u, cw, row(scb[i]), SC_KERNEL, f"conv_{tag}")
            q, k, v = _qkv_fwd(qn, kvn, z, tabs, e_w_q_k[i], e_w_kv_k[i], f"qkv_{tag}")
            o, lse = _attn_fwd(q, k, v, f"attn_{tag}")
            ycat = _even_post(z, cv, o, f"even_post_{tag}")
            y = _matmul(ycat, e_w_out[i], "nn", F32, f"w_out_{tag}", tn=1024)
            saved.append(dict(x=xs, h=h, z=z, u=u, qn=qn, kvn=kvn, cw=cw, cv=cv, q=q, k=k, v=v, o=o, lse=lse,
                              ycat=ycat, y=y))
        else:
            z = _matmul(h, o_w_in[i], "nn", F32, f"w_in_{tag}", tn=1024)
            u = _odd_pre(z, f"odd_pre_{tag}")
            cw = jnp.pad(o_conv_w[i], ((0, KP31 - CONF_KERNEL), (0, 0)))
            cv = _conv_fwd(u, cw, row(o_conv_b[i]), CONF_KERNEL, f"conv_{tag}")
            yin = _odd_post(cv, z, row(o_ln_g[i]), row(o_ln_b[i]), f"odd_post_{tag}")
            y = _matmul(yin, o_w_out[i], "nn", F32, f"w_out_{tag}", tn=1024)
            saved.append(dict(x=xs, h=h, z=z, u=u, cw=cw, cv=cv, yin=yin, y=y))
        xs = _post_norm(xs, y, row(post_norm_g[layer]), row(gate[layer]), f"post_norm_{tag}")

    loss_row, dx = _loss_head(xs, target)
    loss = lax.psum(loss_row[0, 0], MESH_AXES)

    g_pre, g_post, dmod = [None] * L, [None] * L, [None] * L
    g_e_w_in, g_e_w_uq, g_e_w_ukv, g_e_w_out = [None] * NE, [None] * NE, [None] * NE, [None] * NE
    g_scw, g_scb, g_qg, g_kvg = [None] * NE, [None] * NE, [None] * NE, [None] * NE
    g_o_w_in, g_o_w_out, g_ocw, g_ocb, g_olg, g_olb = ([None] * NO for _ in range(6))
    for layer in reversed(range(L)):
        i = layer // 2
        tag = f"l{layer}"
        sv = saved[layer]
        dy, dgate, g_post[layer] = _post_norm_bwd(dx, sv["y"], row(post_norm_g[layer]), row(gate[layer]),
                                                  f"post_norm_bwd_{tag}")
        if layer % 2 == 0:
            dyc = _matmul(dy, e_w_out[i], "nt", F32, f"d_ycat_{tag}", tn=1024)
            g_e_w_out[i] = _matmul(sv["ycat"], dy, "tn", F32, f"g_w_out_{tag}", tk=512)
            dab, dag, dbg, dcv, do, delta = _even_bwd_gates(dyc, sv["z"], sv["cv"], sv["o"], f"even_gates_bwd_{tag}")
            du, dcw, g_scb[i] = _conv_bwd(dcv, sv["u"], sv["cw"], SC_KERNEL, f"conv_bwd_{tag}")
            g_scw[i] = dcw[:SC_KERNEL]
            dq = _attn_dq(sv["q"], sv["k"], sv["v"], do, sv["lse"], delta, f"attn_dq_{tag}")
            dk, dv = _attn_dkv(sv["q"], sv["k"], sv["v"], do, sv["lse"], delta, f"attn_dkv_{tag}")
            (dqp, dkvp, dcq, dckv, dkr, g_qg[i], g_kvg[i]) = _qkv_bwd(
                dq, dk, dv, sv["z"], tabs, e_w_q_k[i], e_w_kv_k[i],
                row(even_q_norm_g[i]), row(even_kv_norm_g[i]), f"qkv_bwd_{tag}")
            gq = _matmul(sv["qn"], dqp, "tn", F32, f"g_w_uq_{tag}", tk=512)
            gkv = _matmul(sv["kvn"], dkvp, "tn", F32, f"g_w_ukv_{tag}", tk=512)
            g_e_w_uq[i] = gq.reshape(QL, HEADS, HEAD_PAD)[..., :QK_NOPE + QK_ROPE].reshape(QL, -1)
            g_e_w_ukv[i] = jnp.concatenate(
                [gkv[:, :HEADS * HEAD_PAD].reshape(KVL, HEADS, HEAD_PAD)[..., :QK_NOPE],
                 gkv[:, HEADS * HEAD_PAD:].reshape(KVL, HEADS, V_HEAD)], axis=-1).reshape(KVL, -1)
            dz = _even_dz(dab, du, sv["z"], dag, dbg, dcq, dckv, dkr, f"even_dz_{tag}")
            gk = _matmul(sv["h"], dz, "tn", F32, f"g_w_in_{tag}", tk=512, tn=1024)
            g_e_w_in[i] = jnp.concatenate([gk[:, :2048], gk[:, 2560:2944],
                                           gk[:, 2944 + QK_NOPE:2944 + QK_NOPE + QK_ROPE], gk[:, 2048:2560]], axis=-1)
            dh = _matmul(dz, e_w_in_k[i], "nt", F32, f"d_h_{tag}", tn=1024)
        else:
            dyi = _matmul(dy, o_w_out[i], "nt", F32, f"d_yin_{tag}", tn=1024)
            g_o_w_out[i] = _matmul(sv["yin"], dy, "tn", F32, f"g_w_out_{tag}", tk=512)
            dcv, dsg, g_olg[i], g_olb[i] = _odd_bwd_norm(dyi, sv["cv"], sv["z"], row(o_ln_g[i]), row(o_ln_b[i]),
                                                         f"odd_norm_bwd_{tag}")
            du, dcw, g_ocb[i] = _conv_bwd(dcv, sv["u"], sv["cw"], CONF_KERNEL, f"conv_bwd_{tag}")
            g_ocw[i] = dcw[:CONF_KERNEL]
            dz = _odd_dz(du, sv["z"], dsg, f"odd_dz_{tag}")
            g_o_w_in[i] = _matmul(sv["h"], dz, "tn", F32, f"g_w_in_{tag}", tk=512, tn=1024)
            dh = _matmul(dz, o_w_in[i], "nt", F32, f"d_h_{tag}", tn=1024)
        dx, dshift, dscale, g_pre[layer] = _pre_norm_bwd(dh, sv["x"], dx, row(pre_norm_g[layer]), row(scale[layer]),
                                                         f"pre_norm_bwd_{tag}")
        dmod[layer] = jnp.concatenate([dshift, dscale, dgate], axis=-1)
    grad_x = dx.reshape(1, S, D)

    rep_g = [jnp.concatenate(dmod, 0), jnp.concatenate(g_pre, 0), jnp.concatenate(g_post, 0),
             jnp.stack(g_scb), jnp.stack(g_qg), jnp.stack(g_kvg)]
    rep_w = [ada_b, pre_norm_g, post_norm_g, even_sc_conv_b, even_q_norm_g, even_kv_norm_g]
    rep_m = [m_ada_b, m_pre_norm_g, m_post_norm_g, m_even_sc_conv_b, m_even_q_norm_g, m_even_kv_norm_g]
    rep_v = [v_ada_b, v_pre_norm_g, v_post_norm_g, v_even_sc_conv_b, v_even_q_norm_g, v_even_kv_norm_g]
    rep_shapes = [w.shape for w in rep_w]
    rep_all = _exchange(_pack(rep_g, F32, SUBLANES), False, "gather_small_grads")
    rep_out = _adamw(rep_all, _pack(rep_w, F32, SUBLANES), _pack(rep_m, F32, SUBLANES), _pack(rep_v, F32, SUBLANES),
                     "adamw_replicated")
    rep_res = [_unpack(o.reshape(-1), rep_shapes) for o in rep_out]

    dmod_all = rep_all.reshape(N_DEV, -1)[:, :L * 3 * D].reshape(N_DEV, L, 3 * D)
    dmod_cols = jnp.moveaxis(lax.dynamic_slice_in_dim(dmod_all, me * n_ada, n_ada, axis=2), 0, 1)
    g_ada_w = _ada_bwd(c_all.T, dmod_cols)
    ada_out = _adamw(g_ada_w.reshape(1, -1, PACK_COLS), ada_w.reshape(-1, PACK_COLS),
                     m_ada_w.reshape(-1, PACK_COLS), v_ada_w.reshape(-1, PACK_COLS), "adamw_ada_w")
    ada_res = [o.reshape(ada_w.shape) for o in ada_out]

    sh_full = [_scatter_cols(jnp.stack(g_e_w_in), even_w_in.shape[-1]),
               _scatter_cols(jnp.stack(g_e_w_uq), even_w_uq.shape[-1]),
               _scatter_cols(jnp.stack(g_e_w_ukv), even_w_ukv.shape[-1]),
               jnp.moveaxis(jnp.stack(g_e_w_out).reshape(NE, N_DEV, -1, D), 1, 0),
               _scatter_cols(jnp.stack(g_o_w_in), odd_w_in.shape[-1]),
               jnp.moveaxis(jnp.stack(g_o_w_out).reshape(NO, N_DEV, -1, D), 1, 0),
               _scatter_cols(jnp.stack(g_scw), even_sc_conv_w.shape[-1]),
               _scatter_cols(jnp.stack(g_ocw), odd_conv_w.shape[-1]),
               _scatter_cols(jnp.concatenate(g_ocb, 0), odd_conv_b.shape[-1]),
               _scatter_cols(jnp.concatenate(g_olg, 0), odd_ln_g.shape[-1]),
               _scatter_cols(jnp.concatenate(g_olb, 0), odd_ln_b.shape[-1])]
    sh_w = [even_w_in, even_w_uq, even_w_ukv, even_w_out, odd_w_in, odd_w_out, even_sc_conv_w, odd_conv_w,
            odd_conv_b, odd_ln_g, odd_ln_b]
    sh_m = [m_even_w_in, m_even_w_uq, m_even_w_ukv, m_even_w_out, m_odd_w_in, m_odd_w_out, m_even_sc_conv_w,
            m_odd_conv_w, m_odd_conv_b, m_odd_ln_g, m_odd_ln_b]
    sh_v = [v_even_w_in, v_even_w_uq, v_even_w_ukv, v_even_w_out, v_odd_w_in, v_odd_w_out, v_even_sc_conv_w,
            v_odd_conv_w, v_odd_conv_b, v_odd_ln_g, v_odd_ln_b]
    sh_shapes = [w.shape for w in sh_w]
    flat = jnp.concatenate([g.reshape(N_DEV, -1) for g in sh_full], axis=1)
    n_sh = flat.shape[1]
    rows = -(-n_sh // (PACK_COLS * ADAM_ROWS)) * ADAM_ROWS
    slabs = jnp.pad(flat, ((0, 0), (0, rows * PACK_COLS - n_sh))).reshape(N_DEV, rows, PACK_COLS)
    recv = _exchange(slabs, True, "scatter_grads")
    sh_out = _adamw(recv, _pack(sh_w, F32, ADAM_ROWS), _pack(sh_m, F32, ADAM_ROWS), _pack(sh_v, F32, ADAM_ROWS),
                    "adamw_sharded")
    sh_res = [_unpack(o.reshape(-1), sh_shapes) for o in sh_out]

    order = ["ada_w", "ada_b", "pre_norm_g", "post_norm_g", "even_w_in", "even_sc_conv_w", "even_sc_conv_b",
             "even_q_norm_g", "even_kv_norm_g", "even_w_uq", "even_w_ukv", "even_w_out", "odd_w_in", "odd_conv_w",
             "odd_conv_b", "odd_ln_g", "odd_ln_b", "odd_w_out"]
    rep_names = ["ada_b", "pre_norm_g", "post_norm_g", "even_sc_conv_b", "even_q_norm_g", "even_kv_norm_g"]
    sh_names = ["even_w_in", "even_w_uq", "even_w_ukv", "even_w_out", "odd_w_in", "odd_w_out", "even_sc_conv_w",
                "odd_conv_w", "odd_conv_b", "odd_ln_g", "odd_ln_b"]
    outs = [loss, grad_x]
    for kind in range(4):
        for name in order:
            if name == "ada_w":
                outs.append(ada_res[kind])
            elif name in rep_names:
                outs.append(rep_res[kind][rep_names.index(name)])
            else:
                outs.append(sh_res[kind][sh_names.index(name)])
    return tuple(outs)
```

```python
import functools
import math

import jax
import jax.numpy as jnp
from jax import lax
from jax.experimental import pallas as pl
from jax.experimental.pallas import tpu as pltpu

F32 = jnp.float32
BF16 = jnp.bfloat16
MESH_AXES = ("x", "y", "c")
N_DEV = 8
EPS = 1e-6
CHUNK = 64
HEADS = 8
QK_NOPE = 64
QK_ROPE = 32
V_HEAD = 64
HEAD_PAD = 128
ROPE_THETA = 10000.0
SC_KERNEL = 3
CONF_KERNEL = 31
LANES = 128
SUBLANES = 8
PACK_COLS = 1024
VMEM_LIMIT = 48 * 1024 * 1024
NEG = -1e30

ADAM_LR = 0.001
ADAM_B1 = 0.9
ADAM_B2 = 0.999
ADAM_EPS = 1e-08
ADAM_WD = 0.01
ADAM_STEP = 10


def _cparams():
    return pltpu.CompilerParams(vmem_limit_bytes=VMEM_LIMIT)


def _sigmoid(x):
    return 1.0 / (1.0 + jnp.exp(-x))


def _silu(x):
    return x * _sigmoid(x)


def _dsilu(x):
    s = _sigmoid(x)
    return s * (1.0 + x * (1.0 - s))


def _rows(T, width, cb=0):
    return pl.BlockSpec((T, width), lambda i: (i, cb))


def _const(shape):
    nd = len(shape)
    return pl.BlockSpec(shape, lambda i: (0,) * nd)


def _row_tile(S):
    return min(256, S)


def _exchange(srcs, scatter, name):
    n = len(srcs)
    shapes = [tuple(s.shape[1:]) if scatter else tuple(s.shape) for s in srcs]

    def body(*refs):
        src_refs, out_refs = refs[:n], refs[n:2 * n]
        send_sems, recv_sems, local_sems = refs[2 * n:]
        x, y, c = lax.axis_index("x"), lax.axis_index("y"), lax.axis_index("c")
        me = 4 * x + 2 * y + c
        owns, copies = [], []
        for a in range(n):
            def piece(d, a=a):
                return src_refs[a].at[d] if scatter else src_refs[a]

            own = pltpu.make_async_copy(piece(me), out_refs[a].at[me], local_sems.at[a])
            own.start()
            owns.append(own)
            for k in range(1, N_DEV):
                px, py, pc = x ^ ((k >> 2) & 1), y ^ ((k >> 1) & 1), c ^ (k & 1)
                peer = 4 * px + 2 * py + pc
                sem = a * (N_DEV - 1) + k - 1
                cp = pltpu.make_async_remote_copy(
                    src_ref=piece(peer), dst_ref=out_refs[a].at[me],
                    send_sem=send_sems.at[sem], recv_sem=recv_sems.at[sem],
                    device_id=(px, py, pc), device_id_type=pl.DeviceIdType.MESH)
                cp.start()
                arrival = pltpu.make_async_remote_copy(
                    src_ref=piece(peer), dst_ref=out_refs[a].at[peer],
                    send_sem=send_sems.at[sem], recv_sem=recv_sems.at[sem],
                    device_id=(x, y, c), device_id_type=pl.DeviceIdType.MESH)
                copies.append((cp, arrival))
        for _, arrival in copies:
            arrival.wait_recv()
        for cp, _ in copies:
            cp.wait_send()
        for own in owns:
            own.wait()

    return pl.pallas_call(
        body, name=name,
        out_shape=tuple(jax.ShapeDtypeStruct((N_DEV,) + shp, s.dtype) for shp, s in zip(shapes, srcs)),
        in_specs=[pl.BlockSpec(memory_space=pl.ANY)] * n,
        out_specs=tuple(pl.BlockSpec(memory_space=pl.ANY) for _ in range(n)),
        scratch_shapes=[pltpu.SemaphoreType.DMA((n * (N_DEV - 1),)),
                        pltpu.SemaphoreType.DMA((n * (N_DEV - 1),)),
                        pltpu.SemaphoreType.DMA((n,))],
    )(*srcs)


def _pack(parts, dtype, row_mult):
    flat = jnp.concatenate([p.reshape(-1).astype(dtype) for p in parts])
    n = flat.shape[0]
    rows = -(-n // PACK_COLS)
    rows = -(-rows // row_mult) * row_mult
    flat = jnp.pad(flat, (0, rows * PACK_COLS - n))
    return flat.reshape(rows, PACK_COLS)


def _unpack(flat, shapes):
    out, off = [], 0
    for shp in shapes:
        n = math.prod(shp)
        out.append(flat[..., off:off + n].reshape(flat.shape[:-1] + tuple(shp)))
        off += n
    return out


_DIMS = {"nn": (((1,), (0,)), ((), ())), "nt": (((1,), (1,)), ((), ())), "tn": (((0,), (0,)), ((), ()))}


def _matmul(a, b, mode, out_dtype, name, tm=512, tn=512, tk=None):
    if mode == "nn":
        (M, K), (_, N) = a.shape, b.shape
    elif mode == "nt":
        (M, K), (N, _) = a.shape, b.shape
    else:
        (K, M), (_, N) = a.shape, b.shape
    tm, tn = min(tm, M), min(tn, N)
    tk = K if tk is None else min(tk, K)
    nk = K // tk
    assert M % tm == 0 and N % tn == 0 and K % tk == 0, (name, a.shape, b.shape)

    def body(a_ref, b_ref, o_ref, *scratch):
        p = lax.dot_general(a_ref[...].astype(BF16), b_ref[...].astype(BF16), _DIMS[mode],
                            preferred_element_type=F32)
        if nk == 1:
            o_ref[...] = p.astype(out_dtype)
        else:
            acc = scratch[0]
            k = pl.program_id(2)

            @pl.when(k == 0)
            def _():
                acc[...] = p

            @pl.when(k > 0)
            def _():
                acc[...] += p

            @pl.when(k == nk - 1)
            def _():
                o_ref[...] = acc[...].astype(out_dtype)

    a_spec = (pl.BlockSpec((tk, tm), lambda i, j, k: (k, i)) if mode == "tn"
              else pl.BlockSpec((tm, tk), lambda i, j, k: (i, k)))
    b_spec = (pl.BlockSpec((tn, tk), lambda i, j, k: (j, k)) if mode == "nt"
              else pl.BlockSpec((tk, tn), lambda i, j, k: (k, j)))
    return pl.pallas_call(
        body, name=name, grid=(M // tm, N // tn, nk),
        out_shape=jax.ShapeDtypeStruct((M, N), out_dtype),
        in_specs=[a_spec, b_spec],
        out_specs=pl.BlockSpec((tm, tn), lambda i, j, k: (i, j)),
        scratch_shapes=[pltpu.VMEM((tm, tn), F32)] if nk > 1 else [],
        compiler_params=_cparams(),
    )(a, b)


def _ada_fwd(c_all, ada_w, ada_b_cols):
    L, D, n = ada_w.shape

    def body(c_ref, w_ref, b_ref, o_ref):
        act = _silu(c_ref[...]).astype(BF16)
        o_ref[0] = jnp.dot(act, w_ref[0].astype(BF16), preferred_element_type=F32) + b_ref[0]

    return pl.pallas_call(
        body, name="ada_fwd", grid=(L,),
        out_shape=jax.ShapeDtypeStruct((L, N_DEV, n), F32),
        in_specs=[pl.BlockSpec((N_DEV, D), lambda l: (0, 0)),
                  pl.BlockSpec((1, D, n), lambda l: (l, 0, 0)),
                  pl.BlockSpec((1, 1, n), lambda l: (l, 0, 0))],
        out_specs=pl.BlockSpec((1, N_DEV, n), lambda l: (l, 0, 0)),
        compiler_params=_cparams(),
    )(c_all, ada_w, ada_b_cols)


def _ada_bwd(c_all_t, dmod_cols):
    D = c_all_t.shape[0]
    L, _, n = dmod_cols.shape

    def body(c_ref, d_ref, o_ref):
        act = _silu(c_ref[...])
        dm = d_ref[0]
        acc = act[:, 0:1] * dm[0:1, :]
        for b in range(1, N_DEV):
            acc = acc + act[:, b:b + 1] * dm[b:b + 1, :]
        o_ref[0] = acc

    return pl.pallas_call(
        body, name="ada_bwd", grid=(L,),
        out_shape=jax.ShapeDtypeStruct((L, D, n), F32),
        in_specs=[pl.BlockSpec((D, N_DEV), lambda l: (0, 0)),
                  pl.BlockSpec((1, N_DEV, n), lambda l: (l, 0, 0))],
        out_specs=pl.BlockSpec((1, D, n), lambda l: (l, 0, 0)),
        compiler_params=_cparams(),
    )(c_all_t, dmod_cols)


def _rope_tables(pos_col, inv_lane):
    S = pos_col.shape[0]
    T = _row_tile(S)
    half = QK_ROPE // 2

    def body(p_ref, f_ref, c_ref, up_ref, dn_ref):
        ang = p_ref[...] * f_ref[...]
        lane = lax.broadcasted_iota(jnp.int32, ang.shape, 1)
        first = (lane >= QK_NOPE) & (lane < QK_NOPE + half)
        second = (lane >= QK_NOPE + half) & (lane < QK_NOPE + QK_ROPE)
        cs, sn = jnp.cos(ang), jnp.sin(ang)
        c_ref[...] = jnp.where(first | second, cs, 1.0)
        up_ref[...] = jnp.where(first, -sn, 0.0)
        dn_ref[...] = jnp.where(second, sn, 0.0)

    tab = jax.ShapeDtypeStruct((S, HEAD_PAD), F32)
    return pl.pallas_call(
        body, name="rope_tables", grid=(S // T,),
        out_shape=(tab, tab, tab),
        in_specs=[_rows(T, 1), _const((1, HEAD_PAD))],
        out_specs=(_rows(T, HEAD_PAD),) * 3,
        compiler_params=_cparams(),
    )(pos_col, inv_lane)


def _rope(blk, ct, ut, dt):
    half = QK_ROPE // 2
    up = pltpu.roll(blk, HEAD_PAD - half, 1)
    dn = pltpu.roll(blk, half, 1)
    return blk * ct + up * ut + dn * dt


def _rope_t(d, ct, ut, dt):
    half = QK_ROPE // 2
    return d * ct + pltpu.roll(d * ut, half, 1) + pltpu.roll(d * dt, HEAD_PAD - half, 1)


def _pre_norm(x, g, scale, shift, name):
    S, D = x.shape
    T = _row_tile(S)

    def body(x_ref, g_ref, sc_ref, sh_ref, h_ref):
        xv = x_ref[...]
        rstd = lax.rsqrt(jnp.mean(xv * xv, axis=-1, keepdims=True) + EPS)
        h_ref[...] = ((xv * rstd) * g_ref[...] * (1.0 + sc_ref[...]) + sh_ref[...]).astype(BF16)

    return pl.pallas_call(
        body, name=name, grid=(S // T,),
        out_shape=jax.ShapeDtypeStruct((S, D), BF16),
        in_specs=[_rows(T, D), _const((1, D)), _const((1, D)), _const((1, D))],
        out_specs=_rows(T, D), compiler_params=_cparams(),
    )(x, g, scale, shift)


def _post_norm(x, y, g, gate, name):
    S, D = x.shape
    T = _row_tile(S)

    def body(x_ref, y_ref, g_ref, gt_ref, o_ref):
        yv = y_ref[...]
        rstd = lax.rsqrt(jnp.mean(yv * yv, axis=-1, keepdims=True) + EPS)
        o_ref[...] = x_ref[...] + gt_ref[...] * ((yv * rstd) * g_ref[...])

    return pl.pallas_call(
        body, name=name, grid=(S // T,),
        out_shape=jax.ShapeDtypeStruct((S, D), F32),
        in_specs=[_rows(T, D), _rows(T, D), _const((1, D)), _const((1, D))],
        out_specs=_rows(T, D), compiler_params=_cparams(),
    )(x, y, g, gate)


def _fold8(v):
    T, C = v.shape
    return v.reshape(T // SUBLANES, SUBLANES, C).sum(axis=0)


def _col_sums(n_sums, body_fn, ins, in_specs, outs, out_specs, S, T, widths, name):
    n_in, n_out = len(ins), len(outs)
    nt = S // T

    def body(*refs):
        in_refs = refs[:n_in]
        out_refs = refs[n_in:n_in + n_out]
        sum_refs = refs[n_in + n_out:n_in + n_out + n_sums]
        accs = refs[n_in + n_out + n_sums:]
        i = pl.program_id(0)
        terms = body_fn(in_refs, out_refs)

        @pl.when(i == 0)
        def _():
            for acc, t in zip(accs, terms):
                acc[...] = _fold8(t)

        @pl.when(i > 0)
        def _():
            for acc, t in zip(accs, terms):
                acc[...] += _fold8(t)

        @pl.when(i == nt - 1)
        def _():
            for acc, s_ref in zip(accs, sum_refs):
                s_ref[...] = jnp.sum(acc[...], axis=0, keepdims=True)

    return pl.pallas_call(
        body, name=name, grid=(nt,),
        out_shape=tuple(outs) + tuple(jax.ShapeDtypeStruct((1, w), F32) for w in widths),
        in_specs=in_specs,
        out_specs=tuple(out_specs) + tuple(_const((1, w)) for w in widths),
        scratch_shapes=[pltpu.VMEM((SUBLANES, w), F32) for w in widths],
        compiler_params=_cparams(),
    )(*ins)


def _post_norm_bwd(dxo, y, g, gate, name):
    S, D = y.shape
    T = _row_tile(S)

    def fn(ins, outs):
        dxo_ref, y_ref, g_ref, gt_ref = ins
        yv, dv = y_ref[...], dxo_ref[...]
        rstd = lax.rsqrt(jnp.mean(yv * yv, axis=-1, keepdims=True) + EPS)
        yh = yv * rstd
        dn = dv * gt_ref[...]
        dyh = dn * g_ref[...]
        outs[0][...] = (rstd * (dyh - yh * jnp.mean(dyh * yh, axis=-1, keepdims=True))).astype(BF16)
        return [dv * (yh * g_ref[...]), dn * yh]

    return _col_sums(2, fn, [dxo, y, g, gate],
                     [_rows(T, D), _rows(T, D), _const((1, D)), _const((1, D))],
                     [jax.ShapeDtypeStruct((S, D), BF16)], [_rows(T, D)], S, T, [D, D], name)


def _pre_norm_bwd(dh, x, dxo, g, scale, name):
    S, D = x.shape
    T = _row_tile(S)

    def fn(ins, outs):
        dh_ref, x_ref, dxo_ref, g_ref, sc_ref = ins
        xv, dv = x_ref[...], dh_ref[...]
        rstd = lax.rsqrt(jnp.mean(xv * xv, axis=-1, keepdims=True) + EPS)
        xh = xv * rstd
        dr = dv * (1.0 + sc_ref[...])
        dxh = dr * g_ref[...]
        outs[0][...] = dxo_ref[...] + rstd * (dxh - xh * jnp.mean(dxh * xh, axis=-1, keepdims=True))
        return [dv, dv * (xh * g_ref[...]), dr * xh]

    return _col_sums(3, fn, [dh, x, dxo, g, scale],
                     [_rows(T, D), _rows(T, D), _rows(T, D), _const((1, D)), _const((1, D))],
                     [jax.ShapeDtypeStruct((S, D), F32)], [_rows(T, D)], S, T, [D, D, D], name)


def _loss_head(x, target):
    S, D = x.shape
    T = _row_tile(S)
    nt = S // T

    def body(x_ref, t_ref, l_ref, dx_ref, acc):
        i = pl.program_id(0)
        e = x_ref[...] - t_ref[...]
        dx_ref[...] = e * (1.0 / D)
        part = _fold8(e * e)

        @pl.when(i == 0)
        def _():
            acc[...] = part

        @pl.when(i > 0)
        def _():
            acc[...] += part

        @pl.when(i == nt - 1)
        def _():
            tot = jnp.sum(jnp.sum(acc[...], axis=0, keepdims=True), axis=1, keepdims=True)
            l_ref[...] = jnp.broadcast_to(tot * (0.5 / D), (1, LANES))

    return pl.pallas_call(
        body, name="loss_head", grid=(nt,),
        out_shape=(jax.ShapeDtypeStruct((1, LANES), F32), jax.ShapeDtypeStruct((S, D), F32)),
        in_specs=[_rows(T, D), _rows(T, D)],
        out_specs=(_const((1, LANES)), _rows(T, D)),
        scratch_shapes=[pltpu.VMEM((SUBLANES, D), F32)],
        compiler_params=_cparams(),
    )(x, target)


CONV_ROWS = 64


def _conv_halo(K):
    return SUBLANES if K - 1 <= SUBLANES else 32


def _conv_fwd(u, w, b, K, name):
    S, C = u.shape
    KP = w.shape[0]
    T, HB, RS = min(512, S), _conv_halo(K), CONV_ROWS
    ratio = T // HB

    def body(u_ref, h_ref, w_ref, b_ref, o_ref, ext):
        i = pl.program_id(1)
        ext[0:HB, :] = jnp.where(i > 0, h_ref[...], 0.0)
        ext[HB:HB + T, :] = u_ref[...]
        for r0 in range(0, T, RS):
            acc = jnp.broadcast_to(b_ref[...], (RS, LANES))
            for k in range(K):
                off = HB - (K - 1) + k + r0
                acc = acc + w_ref[k:k + 1, :] * ext[off:off + RS, :]
            o_ref[r0:r0 + RS, :] = acc

    return pl.pallas_call(
        body, name=name, grid=(C // LANES, S // T),
        out_shape=jax.ShapeDtypeStruct((S, C), F32),
        in_specs=[pl.BlockSpec((T, LANES), lambda c, i: (i, c)),
                  pl.BlockSpec((HB, LANES), lambda c, i: (jnp.maximum(i * ratio - 1, 0), c)),
                  pl.BlockSpec((KP, LANES), lambda c, i: (0, c)),
                  pl.BlockSpec((1, LANES), lambda c, i: (0, c))],
        out_specs=pl.BlockSpec((T, LANES), lambda c, i: (i, c)),
        scratch_shapes=[pltpu.VMEM((HB + T, LANES), F32)],
        compiler_params=_cparams(),
    )(u, u, w, b)


def _conv_bwd(d, u, w, K, name):
    S, C = u.shape
    KP = w.shape[0]
    T, HB, RS = min(512, S), _conv_halo(K), CONV_ROWS
    ratio = T // HB
    nt = S // T
    last_halo = S // HB - 1

    def body(d_ref, dn_ref, u_ref, up_ref, w_ref, du_ref, dw_ref, db_ref, extd, extu, dws, dbs):
        i = pl.program_id(1)
        extd[0:T, :] = d_ref[...]
        extd[T:T + HB, :] = jnp.where(i < nt - 1, dn_ref[...], 0.0)
        extu[0:HB, :] = jnp.where(i > 0, up_ref[...], 0.0)
        extu[HB:HB + T, :] = u_ref[...]

        @pl.when(i == 0)
        def _():
            dws[...] = jnp.zeros_like(dws)
            dbs[...] = jnp.zeros_like(dbs)

        for r0 in range(0, T, RS):
            acc = jnp.zeros((RS, LANES), F32)
            for k in range(K):
                off = (K - 1 - k) + r0
                acc = acc + w_ref[k:k + 1, :] * extd[off:off + RS, :]
            du_ref[r0:r0 + RS, :] = acc
            dch = d_ref[r0:r0 + RS, :]
            dbs[...] += _fold8(dch)
            for k in range(K):
                off = HB - (K - 1) + k + r0
                dws[k * SUBLANES:(k + 1) * SUBLANES, :] += _fold8(dch * extu[off:off + RS, :])

        @pl.when(i == nt - 1)
        def _():
            dw_ref[...] = jnp.zeros_like(dw_ref)
            for k in range(K):
                dw_ref[k:k + 1, :] = jnp.sum(dws[k * SUBLANES:(k + 1) * SUBLANES, :], axis=0, keepdims=True)
            db_ref[...] = jnp.sum(dbs[...], axis=0, keepdims=True)

    return pl.pallas_call(
        body, name=name, grid=(C // LANES, nt),
        out_shape=(jax.ShapeDtypeStruct((S, C), F32), jax.ShapeDtypeStruct((KP, C), F32),
                   jax.ShapeDtypeStruct((1, C), F32)),
        in_specs=[pl.BlockSpec((T, LANES), lambda c, i: (i, c)),
                  pl.BlockSpec((HB, LANES), lambda c, i: (jnp.minimum((i + 1) * ratio, last_halo), c)),
                  pl.BlockSpec((T, LANES), lambda c, i: (i, c)),
                  pl.BlockSpec((HB, LANES), lambda c, i: (jnp.maximum(i * ratio - 1, 0), c)),
                  pl.BlockSpec((KP, LANES), lambda c, i: (0, c))],
        out_specs=(pl.BlockSpec((T, LANES), lambda c, i: (i, c)),
                   pl.BlockSpec((KP, LANES), lambda c, i: (0, c)),
                   pl.BlockSpec((1, LANES), lambda c, i: (0, c))),
        scratch_shapes=[pltpu.VMEM((T + HB, LANES), F32), pltpu.VMEM((HB + T, LANES), F32),
                        pltpu.VMEM((KP * SUBLANES, LANES), F32), pltpu.VMEM((SUBLANES, LANES), F32)],
        compiler_params=_cparams(),
    )(d, d, u, u, w)


SCW = 512
ZE = 3072
QL = 256
KVL = 128


def _rms_rows(x, g):
    rstd = lax.rsqrt(jnp.mean(x * x, axis=-1, keepdims=True) + EPS)
    return (x * rstd) * g


def _even_pre(z, qg, kvg, name):
    S = z.shape[0]
    T = _row_tile(S)

    def body(ac_ref, ax_ref, cq_ref, ckv_ref, qg_ref, kvg_ref, u_ref, qn_ref, kvn_ref):
        u_ref[...] = ac_ref[...] * ax_ref[...]
        qn_ref[...] = _rms_rows(cq_ref[...], qg_ref[...]).astype(BF16)
        kvn_ref[...] = _rms_rows(ckv_ref[...], kvg_ref[...]).astype(BF16)

    return pl.pallas_call(
        body, name=name, grid=(S // T,),
        out_shape=(jax.ShapeDtypeStruct((S, SCW), F32), jax.ShapeDtypeStruct((S, QL), BF16),
                   jax.ShapeDtypeStruct((S, KVL), BF16)),
        in_specs=[_rows(T, SCW, 1), _rows(T, SCW, 2), _rows(T, QL, 10), _rows(T, KVL, 22),
                  _const((1, QL)), _const((1, KVL))],
        out_specs=(_rows(T, SCW), _rows(T, QL), _rows(T, KVL)),
        compiler_params=_cparams(),
    )(z, z, z, z, qg, kvg)


def _qkv_fwd(qn, kvn, z, tabs, w_q, w_kv, name):
    S = qn.shape[0]
    T = _row_tile(S)
    HW = HEADS * HEAD_PAD
    scale = 1.0 / math.sqrt(QK_NOPE + QK_ROPE)

    def body(qn_ref, kvn_ref, kr_ref, ct_ref, ut_ref, dt_ref, wq_ref, wkv_ref, q_ref, k_ref, v_ref):
        ct, ut, dt = ct_ref[...], ut_ref[...], dt_ref[...]
        qa = jnp.dot(qn_ref[...], wq_ref[...], preferred_element_type=F32)
        kva = jnp.dot(kvn_ref[...], wkv_ref[...], preferred_element_type=F32)
        kr = kr_ref[...]
        for h in range(HEADS):
            sl = slice(h * HEAD_PAD, (h + 1) * HEAD_PAD)
            q_ref[:, sl] = (_rope(qa[:, sl], ct, ut, dt) * scale).astype(BF16)
            k_ref[:, sl] = _rope(kva[:, sl] + kr, ct, ut, dt).astype(BF16)
        v_ref[...] = kva[:, HW:].astype(BF16)

    return pl.pallas_call(
        body, name=name, grid=(S // T,),
        out_shape=(jax.ShapeDtypeStruct((S, HW), BF16), jax.ShapeDtypeStruct((S, HW), BF16),
                   jax.ShapeDtypeStruct((S, HEADS * V_HEAD), BF16)),
        in_specs=[_rows(T, QL), _rows(T, KVL), _rows(T, HEAD_PAD, 23),
                  _rows(T, HEAD_PAD), _rows(T, HEAD_PAD), _rows(T, HEAD_PAD),
                  _const(w_q.shape), _const(w_kv.shape)],
        out_specs=(_rows(T, HW), _rows(T, HW), _rows(T, HEADS * V_HEAD)),
        compiler_params=_cparams(),
    )(qn, kvn, z, *tabs, w_q, w_kv)


def _attn_tile(S):
    return min(256, S)


def _chunk_mask(TQ):
    r = lax.broadcasted_iota(jnp.int32, (TQ, TQ), 0) // CHUNK
    c = lax.broadcasted_iota(jnp.int32, (TQ, TQ), 1) // CHUNK
    return c <= r


_NT = (((1,), (1,)), ((), ()))
_TN = (((0,), (0,)), ((), ()))


def _attn_fwd(q, k, v, name):
    S = q.shape[0]
    TQ = _attn_tile(S)
    nq = S // TQ
    PW = 2 * HEAD_PAD

    def body(q_ref, k_ref, v_ref, o_ref, lse_ref, m_s, l_s, acc_s):
        i = pl.program_id(1)
        left = lax.broadcasted_iota(jnp.int32, (TQ, LANES), 1) < V_HEAD
        m_s[...] = jnp.full_like(m_s, NEG)
        l_s[...] = jnp.zeros_like(l_s)
        acc_s[...] = jnp.zeros_like(acc_s)
        qv = q_ref[...]

        def step(j, masked):
            r0 = pl.multiple_of(j * TQ, TQ)
            kb = k_ref[pl.ds(r0, TQ), :]
            vb = v_ref[pl.ds(r0, TQ), :]
            alphas, pvs = [], []
            for h in range(2):
                sl = slice(h * HEAD_PAD, (h + 1) * HEAD_PAD)
                s = lax.dot_general(qv[:, sl], kb[:, sl], _NT, preferred_element_type=F32)
                if masked:
                    s = jnp.where(_chunk_mask(TQ), s, NEG)
                m_prev = m_s[h]
                m_new = jnp.maximum(m_prev, jnp.max(s, axis=1, keepdims=True))
                alpha = jnp.exp(m_prev - m_new)
                p = jnp.exp(s - m_new[:, 0:1])
                l_s[h] = alpha * l_s[h] + jnp.sum(p, axis=1, keepdims=True)
                m_s[h] = m_new
                alphas.append(alpha)
                pvs.append(jnp.dot(p.astype(BF16), vb, preferred_element_type=F32))
            acc_s[...] = acc_s[...] * jnp.where(left, alphas[0], alphas[1]) + jnp.where(left, pvs[0], pvs[1])

        def loop_body(j, carry):
            step(j, False)
            return carry

        lax.fori_loop(0, i, loop_body, 0)
        step(i, True)
        o_ref[...] = acc_s[...] / jnp.where(left, l_s[0], l_s[1])
        lse_ref[...] = jnp.where(left, m_s[0] + jnp.log(l_s[0]), m_s[1] + jnp.log(l_s[1]))

    return pl.pallas_call(
        body, name=name, grid=(HEADS // 2, nq),
        out_shape=(jax.ShapeDtypeStruct((S, HEADS * V_HEAD), F32), jax.ShapeDtypeStruct((S, HEADS * V_HEAD), F32)),
        in_specs=[pl.BlockSpec((TQ, PW), lambda p, i: (i, p)),
                  pl.BlockSpec((S, PW), lambda p, i: (0, p)),
                  pl.BlockSpec((S, LANES), lambda p, i: (0, p))],
        out_specs=(pl.BlockSpec((TQ, LANES), lambda p, i: (i, p)),
                   pl.BlockSpec((TQ, LANES), lambda p, i: (i, p))),
        scratch_shapes=[pltpu.VMEM((2, TQ, LANES), F32), pltpu.VMEM((2, TQ, LANES), F32),
                        pltpu.VMEM((TQ, LANES), F32)],
        compiler_params=_cparams(),
    )(q, k, v)


def _attn_dq(q, k, v, do, lse, delta, name):
    S = q.shape[0]
    TQ = _attn_tile(S)
    nq = S // TQ
    PW = 2 * HEAD_PAD

    def body(q_ref, k_ref, v_ref, do_ref, lse_ref, dl_ref, dq_ref, acc_s):
        i = pl.program_id(1)
        left = lax.broadcasted_iota(jnp.int32, (TQ, LANES), 1) < V_HEAD
        acc_s[...] = jnp.zeros_like(acc_s)
        qv = q_ref[...]
        dov = do_ref[...]
        dos = [jnp.where(left, dov, jnp.zeros_like(dov)), jnp.where(left, jnp.zeros_like(dov), dov)]
        lses = [lse_ref[:, 0:1], lse_ref[:, V_HEAD:V_HEAD + 1]]
        dls = [dl_ref[:, 0:1], dl_ref[:, V_HEAD:V_HEAD + 1]]

        def step(j, masked):
            r0 = pl.multiple_of(j * TQ, TQ)
            kb = k_ref[pl.ds(r0, TQ), :]
            vb = v_ref[pl.ds(r0, TQ), :]
            for h in range(2):
                sl = slice(h * HEAD_PAD, (h + 1) * HEAD_PAD)
                s = lax.dot_general(qv[:, sl], kb[:, sl], _NT, preferred_element_type=F32)
                p = jnp.exp(s - lses[h])
                if masked:
                    p = jnp.where(_chunk_mask(TQ), p, 0.0)
                dp = lax.dot_general(dos[h], vb, _NT, preferred_element_type=F32)
                ds = (p * (dp - dls[h])).astype(BF16)
                acc_s[:, sl] += jnp.dot(ds, kb[:, sl], preferred_element_type=F32)

        def loop_body(j, carry):
            step(j, False)
            return carry

        lax.fori_loop(0, i, loop_body, 0)
        step(i, True)
        dq_ref[...] = acc_s[...]

    return pl.pallas_call(
        body, name=name, grid=(HEADS // 2, nq),
        out_shape=jax.ShapeDtypeStruct((S, HEADS * HEAD_PAD), F32),
        in_specs=[pl.BlockSpec((TQ, PW), lambda p, i: (i, p)),
                  pl.BlockSpec((S, PW), lambda p, i: (0, p)),
                  pl.BlockSpec((S, LANES), lambda p, i: (0, p)),
                  pl.BlockSpec((TQ, LANES), lambda p, i: (i, p)),
                  pl.BlockSpec((TQ, LANES), lambda p, i: (i, p)),
                  pl.BlockSpec((TQ, LANES), lambda p, i: (i, p))],
        out_specs=pl.BlockSpec((TQ, PW), lambda p, i: (i, p)),
        scratch_shapes=[pltpu.VMEM((TQ, PW), F32)],
        compiler_params=_cparams(),
    )(q, k, v, do, lse, delta)


def _attn_dkv(q, k, v, do, lse, delta, name):
    S = q.shape[0]
    TQ = _attn_tile(S)
    nq = S // TQ
    PW = 2 * HEAD_PAD

    def body(q_ref, k_ref, v_ref, do_ref, lse_ref, dl_ref, dk_ref, dv_ref, dk_s, dv_s):
        j = pl.program_id(1)
        left = lax.broadcasted_iota(jnp.int32, (TQ, LANES), 1) < V_HEAD
        dk_s[...] = jnp.zeros_like(dk_s)
        dv_s[...] = jnp.zeros_like(dv_s)
        kb = k_ref[...]
        vb = v_ref[...]

        def step(i, masked):
            r0 = pl.multiple_of(i * TQ, TQ)
            qb = q_ref[pl.ds(r0, TQ), :]
            dov = do_ref[pl.ds(r0, TQ), :]
            lse = lse_ref[pl.ds(r0, TQ), :]
            dl = dl_ref[pl.ds(r0, TQ), :]
            dos = [jnp.where(left, dov, jnp.zeros_like(dov)), jnp.where(left, jnp.zeros_like(dov), dov)]
            for h in range(2):
                sl = slice(h * HEAD_PAD, (h + 1) * HEAD_PAD)
                c0 = h * V_HEAD
                s = lax.dot_general(qb[:, sl], kb[:, sl], _NT, preferred_element_type=F32)
                p = jnp.exp(s - lse[:, c0:c0 + 1])
                if masked:
                    p = jnp.where(_chunk_mask(TQ), p, 0.0)
                dv_s[...] += lax.dot_general(p.astype(BF16), dos[h], _TN, preferred_element_type=F32)
                dp = lax.dot_general(dos[h], vb, _NT, preferred_element_type=F32)
                ds = (p * (dp - dl[:, c0:c0 + 1])).astype(BF16)
                dk_s[:, sl] += lax.dot_general(ds, qb[:, sl], _TN, preferred_element_type=F32)

        def loop_body(i, carry):
            step(i, False)
            return carry

        step(j, True)
        lax.fori_loop(j + 1, nq, loop_body, 0)
        dk_ref[...] = dk_s[...]
        dv_ref[...] = dv_s[...]

    return pl.pallas_call(
        body, name=name, grid=(HEADS // 2, nq),
        out_shape=(jax.ShapeDtypeStruct((S, HEADS * HEAD_PAD), F32), jax.ShapeDtypeStruct((S, HEADS * V_HEAD), F32)),
        in_specs=[pl.BlockSpec((S, PW), lambda p, j: (0, p)),
                  pl.BlockSpec((TQ, PW), lambda p, j: (j, p)),
                  pl.BlockSpec((TQ, LANES), lambda p, j: (j, p)),
                  pl.BlockSpec((S, LANES), lambda p, j: (0, p)),
                  pl.BlockSpec((S, LANES), lambda p, j: (0, p)),
                  pl.BlockSpec((S, LANES), lambda p, j: (0, p))],
        out_specs=(pl.BlockSpec((TQ, PW), lambda p, j: (j, p)),
                   pl.BlockSpec((TQ, LANES), lambda p, j: (j, p))),
        scratch_shapes=[pltpu.VMEM((TQ, PW), F32), pltpu.VMEM((TQ, LANES), F32)],
        compiler_params=_cparams(),
    )(q, k, v, do, lse, delta)


def _even_post(z, cv, o, name):
    S = z.shape[0]
    T = _row_tile(S)

    def body(ab_ref, ag_ref, bg_ref, cv_ref, o_ref, y_ref):
        y_ref[:, 0:SCW] = (ab_ref[...] * cv_ref[...] * _silu(ag_ref[...])).astype(BF16)
        y_ref[:, SCW:2 * SCW] = (o_ref[...] * _silu(bg_ref[...])).astype(BF16)

    return pl.pallas_call(
        body, name=name, grid=(S // T,),
        out_shape=jax.ShapeDtypeStruct((S, 2 * SCW), BF16),
        in_specs=[_rows(T, SCW, 0), _rows(T, SCW, 3), _rows(T, SCW, 4), _rows(T, SCW), _rows(T, SCW)],
        out_specs=_rows(T, 2 * SCW), compiler_params=_cparams(),
    )(z, z, z, cv, o)


def _even_bwd_gates(dyc, z, cv, o, name):
    S = z.shape[0]
    T = _row_tile(S)

    def body(dya_ref, dyb_ref, ab_ref, ag_ref, bg_ref, cv_ref, o_ref,
             dab_ref, dag_ref, dbg_ref, dcv_ref, do_ref, dl_ref):
        dya, ab, ag, cv = dya_ref[...], ab_ref[...], ag_ref[...], cv_ref[...]
        sg = _silu(ag)
        dab_ref[...] = (dya * cv * sg).astype(BF16)
        dcv_ref[...] = dya * ab * sg
        dag_ref[...] = (dya * ab * cv * _dsilu(ag)).astype(BF16)
        dyb, bg, ov = dyb_ref[...], bg_ref[...], o_ref[...]
        dov = dyb * _silu(bg)
        do_ref[...] = dov.astype(BF16)
        dbg_ref[...] = (dyb * ov * _dsilu(bg)).astype(BF16)
        prod = dov * ov
        left = lax.broadcasted_iota(jnp.int32, (T, LANES), 1) < V_HEAD
        for p in range(HEADS // 2):
            blk = prod[:, p * LANES:(p + 1) * LANES]
            s0 = jnp.sum(jnp.where(left, blk, 0.0), axis=1, keepdims=True)
            s1 = jnp.sum(jnp.where(left, 0.0, blk), axis=1, keepdims=True)
            dl_ref[:, p * LANES:(p + 1) * LANES] = jnp.where(left, s0, s1)

    bf = jax.ShapeDtypeStruct((S, SCW), BF16)
    ff = jax.ShapeDtypeStruct((S, SCW), F32)
    return pl.pallas_call(
        body, name=name, grid=(S // T,),
        out_shape=(bf, bf, bf, ff, bf, ff),
        in_specs=[_rows(T, SCW, 0), _rows(T, SCW, 1), _rows(T, SCW, 0), _rows(T, SCW, 3), _rows(T, SCW, 4),
                  _rows(T, SCW), _rows(T, SCW)],
        out_specs=(_rows(T, SCW),) * 6, compiler_params=_cparams(),
    )(dyc, dyc, z, z, z, cv, o)


def _qkv_bwd(dq, dk, dv, z, tabs, w_q, w_kv, qg, kvg, name):
    S = dq.shape[0]
    T = _row_tile(S)
    HW = HEADS * HEAD_PAD
    VW = HEADS * V_HEAD
    scale = 1.0 / math.sqrt(QK_NOPE + QK_ROPE)

    def fn(ins, outs):
        dq_ref, dk_ref, dv_ref, cq_ref, ckv_ref, ct_ref, ut_ref, dt_ref, wq_ref, wkv_ref, qg_ref, kvg_ref = ins
        dqp_ref, dkvp_ref, dcq_ref, dckv_ref, dkr_ref = outs
        ct, ut, dt = ct_ref[...], ut_ref[...], dt_ref[...]
        dkr = jnp.zeros((T, HEAD_PAD), F32)
        for h in range(HEADS):
            sl = slice(h * HEAD_PAD, (h + 1) * HEAD_PAD)
            dqp_ref[:, sl] = (_rope_t(dq_ref[:, sl], ct, ut, dt) * scale).astype(BF16)
            dkh = _rope_t(dk_ref[:, sl], ct, ut, dt)
            dkr = dkr + dkh
            dkvp_ref[:, sl] = dkh.astype(BF16)
        dkvp_ref[:, HW:] = dv_ref[...].astype(BF16)
        dkr_ref[...] = dkr.astype(BF16)
        sums = []
        for lat_ref, g_ref, dpre_ref, w_ref, dlat_ref in ((cq_ref, qg_ref, dqp_ref, wq_ref, dcq_ref),
                                                         (ckv_ref, kvg_ref, dkvp_ref, wkv_ref, dckv_ref)):
            dn = lax.dot_general(dpre_ref[...], w_ref[...], _NT, preferred_element_type=F32)
            xv = lat_ref[...]
            rstd = lax.rsqrt(jnp.mean(xv * xv, axis=-1, keepdims=True) + EPS)
            xh = xv * rstd
            dxh = dn * g_ref[...]
            dlat_ref[...] = (rstd * (dxh - xh * jnp.mean(dxh * xh, axis=-1, keepdims=True))).astype(BF16)
            sums.append(dn * xh)
        return sums

    return _col_sums(
        2, fn, [dq, dk, dv, z, z, *tabs, w_q, w_kv, qg, kvg],
        [_rows(T, HW), _rows(T, HW), _rows(T, VW), _rows(T, QL, 10), _rows(T, KVL, 22),
         _rows(T, HEAD_PAD), _rows(T, HEAD_PAD), _rows(T, HEAD_PAD),
         _const(w_q.shape), _const(w_kv.shape), _const((1, QL)), _const((1, KVL))],
        [jax.ShapeDtypeStruct((S, HW), BF16), jax.ShapeDtypeStruct((S, HW + VW), BF16),
         jax.ShapeDtypeStruct((S, QL), BF16), jax.ShapeDtypeStruct((S, KVL), BF16),
         jax.ShapeDtypeStruct((S, HEAD_PAD), BF16)],
        [_rows(T, HW), _rows(T, HW + VW), _rows(T, QL), _rows(T, KVL), _rows(T, HEAD_PAD)],
        S, T, [QL, KVL], name)


def _even_dz(dab, du, z, dag, dbg, dcq, dckv, dkr, name):
    S = z.shape[0]
    T = _row_tile(S)

    def body(dab_ref, du_ref, ac_ref, ax_ref, dag_ref, dbg_ref, dcq_ref, dckv_ref, dkr_ref, dz_ref):
        duv = du_ref[...]
        dz_ref[:, 0:SCW] = dab_ref[...]
        dz_ref[:, SCW:2 * SCW] = (duv * ax_ref[...]).astype(BF16)
        dz_ref[:, 2 * SCW:3 * SCW] = (duv * ac_ref[...]).astype(BF16)
        dz_ref[:, 3 * SCW:4 * SCW] = dag_ref[...]
        dz_ref[:, 4 * SCW:5 * SCW] = dbg_ref[...]
        dz_ref[:, 5 * SCW:5 * SCW + QL] = dcq_ref[...]
        dz_ref[:, 5 * SCW + QL:5 * SCW + QL + KVL] = dckv_ref[...]
        dz_ref[:, 5 * SCW + QL + KVL:ZE] = dkr_ref[...]

    return pl.pallas_call(
        body, name=name, grid=(S // T,),
        out_shape=jax.ShapeDtypeStruct((S, ZE), BF16),
        in_specs=[_rows(T, SCW), _rows(T, SCW), _rows(T, SCW, 1), _rows(T, SCW, 2), _rows(T, SCW), _rows(T, SCW),
                  _rows(T, QL), _rows(T, KVL), _rows(T, HEAD_PAD)],
        out_specs=_rows(T, ZE), compiler_params=_cparams(),
    )(dab, du, z, z, dag, dbg, dcq, dckv, dkr)


def _odd_pre(z, name):
    S, D = z.shape[0], z.shape[1] // 3
    T = _row_tile(S)

    def body(val_ref, glu_ref, u_ref):
        u_ref[...] = val_ref[...] * _sigmoid(glu_ref[...])

    return pl.pallas_call(
        body, name=name, grid=(S // T,),
        out_shape=jax.ShapeDtypeStruct((S, D), F32),
        in_specs=[_rows(T, D, 0), _rows(T, D, 1)], out_specs=_rows(T, D),
        compiler_params=_cparams(),
    )(z, z)


def _layer_norm_stats(cv):
    mu = jnp.mean(cv, axis=-1, keepdims=True)
    cen = cv - mu
    rstd = lax.rsqrt(jnp.mean(cen * cen, axis=-1, keepdims=True) + EPS)
    return cen * rstd, rstd


def _odd_post(cv, z, ln_g, ln_b, name):
    S, D = cv.shape
    T = _row_tile(S)

    def body(cv_ref, sg_ref, g_ref, b_ref, y_ref):
        cvh, _ = _layer_norm_stats(cv_ref[...])
        y_ref[...] = (_silu(cvh * g_ref[...] + b_ref[...]) * _silu(sg_ref[...])).astype(BF16)

    return pl.pallas_call(
        body, name=name, grid=(S // T,),
        out_shape=jax.ShapeDtypeStruct((S, D), BF16),
        in_specs=[_rows(T, D), _rows(T, D, 2), _const((1, D)), _const((1, D))],
        out_specs=_rows(T, D), compiler_params=_cparams(),
    )(cv, z, ln_g, ln_b)


def _odd_bwd_norm(dyi, cv, z, ln_g, ln_b, name):
    S, D = cv.shape
    T = _row_tile(S)

    def fn(ins, outs):
        dy_ref, cv_ref, sg_ref, g_ref, b_ref = ins
        dcv_ref, dsg_ref = outs
        cvh, rstd = _layer_norm_stats(cv_ref[...])
        ln = cvh * g_ref[...] + b_ref[...]
        sgv, dy = sg_ref[...], dy_ref[...]
        dsg_ref[...] = (dy * _silu(ln) * _dsilu(sgv)).astype(BF16)
        dln = dy * _silu(sgv) * _dsilu(ln)
        dh = dln * g_ref[...]
        dcv_ref[...] = rstd * (dh - jnp.mean(dh, axis=-1, keepdims=True)
                               - cvh * jnp.mean(dh * cvh, axis=-1, keepdims=True))
        return [dln * cvh, dln]

    return _col_sums(2, fn, [dyi, cv, z, ln_g, ln_b],
                     [_rows(T, D), _rows(T, D), _rows(T, D, 2), _const((1, D)), _const((1, D))],
                     [jax.ShapeDtypeStruct((S, D), F32), jax.ShapeDtypeStruct((S, D), BF16)],
                     [_rows(T, D), _rows(T, D)], S, T, [D, D], name)


def _odd_dz(du, z, dsg, name):
    S, D = du.shape
    T = _row_tile(S)

    def body(du_ref, val_ref, glu_ref, dsg_ref, dz_ref):
        duv = du_ref[...]
        sig = _sigmoid(glu_ref[...])
        dz_ref[:, 0:D] = (duv * sig).astype(BF16)
        dz_ref[:, D:2 * D] = (duv * val_ref[...] * sig * (1.0 - sig)).astype(BF16)
        dz_ref[:, 2 * D:3 * D] = dsg_ref[...]

    return pl.pallas_call(
        body, name=name, grid=(S // T,),
        out_shape=jax.ShapeDtypeStruct((S, 3 * D), BF16),
        in_specs=[_rows(T, D), _rows(T, D, 0), _rows(T, D, 1), _rows(T, D)],
        out_specs=_rows(T, 3 * D), compiler_params=_cparams(),
    )(du, z, z, dsg)


ADAM_BLOCK_ELEMS = 128 * 1024


def _adam_tiles(R, C):
    if R * C <= ADAM_BLOCK_ELEMS:
        return R, C
    tr = R
    for cand in range(SUBLANES, R, SUBLANES):
        if R % cand == 0 and cand * C <= ADAM_BLOCK_ELEMS:
            tr = cand
    if tr < R:
        return tr, C
    tc = C
    for cand in range(LANES, C, LANES):
        if C % cand == 0 and R * cand <= ADAM_BLOCK_ELEMS:
            tc = cand
    return R, tc


def _adamw(g_parts, w, m, v, name):
    P, R, C = g_parts.shape
    tr, tc = _adam_tiles(R, C)

    def body(g_ref, w_ref, m_ref, v_ref, go_ref, d_ref, mo_ref, vo_ref):
        g = g_ref[0]
        for p in range(1, P):
            g = g + g_ref[p]
        mn = ADAM_B1 * m_ref[...] + (1.0 - ADAM_B1) * g
        vn = ADAM_B2 * v_ref[...] + (1.0 - ADAM_B2) * (g * g)
        m_hat = mn / (1.0 - ADAM_B1 ** ADAM_STEP)
        v_hat = vn / (1.0 - ADAM_B2 ** ADAM_STEP)
        go_ref[...] = g
        d_ref[...] = -ADAM_LR * (m_hat / (jnp.sqrt(v_hat) + ADAM_EPS) + ADAM_WD * w_ref[...])
        mo_ref[...] = mn
        vo_ref[...] = vn

    slab = jax.ShapeDtypeStruct((R, C), F32)
    blk = pl.BlockSpec((tr, tc), lambda i, j: (i, j))
    return pl.pallas_call(
        body, name=name, grid=(R // tr, C // tc),
        out_shape=(slab,) * 4,
        in_specs=[pl.BlockSpec((P, tr, tc), lambda i, j: (0, i, j)), blk, blk, blk],
        out_specs=(blk,) * 4, compiler_params=_cparams(),
    )(g_parts, w, m, v)


def _gather_cols(g, shape):
    nd = len(shape)
    t = jnp.moveaxis(g, 0, nd - 1)
    return t.reshape(tuple(shape[:-1]) + (N_DEV * shape[-1],))


def _scatter_cols(full, n):
    t = full.reshape(full.shape[:-1] + (N_DEV, n))
    return jnp.moveaxis(t, -2, 0)


def kernel(x, c, positions, ada_w, ada_b, pre_norm_g, post_norm_g, even_w_in, even_sc_conv_w, even_sc_conv_b, even_q_norm_g, even_kv_norm_g, even_w_uq, even_w_ukv, even_w_out, odd_w_in, odd_conv_w, odd_conv_b, odd_ln_g, odd_ln_b, odd_w_out, loss_target, m_ada_w, m_ada_b, m_pre_norm_g, m_post_norm_g, m_even_w_in, m_even_sc_conv_w, m_even_sc_conv_b, m_even_q_norm_g, m_even_kv_norm_g, m_even_w_uq, m_even_w_ukv, m_even_w_out, m_odd_w_in, m_odd_conv_w, m_odd_conv_b, m_odd_ln_g, m_odd_ln_b, m_odd_w_out, v_ada_w, v_ada_b, v_pre_norm_g, v_post_norm_g, v_even_w_in, v_even_sc_conv_w, v_even_sc_conv_b, v_even_q_norm_g, v_even_kv_norm_g, v_even_w_uq, v_even_w_ukv, v_even_w_out, v_odd_w_in, v_odd_conv_w, v_odd_conv_b, v_odd_ln_g, v_odd_ln_b, v_odd_w_out):
    S, D = x.shape[1], x.shape[2]
    L = ada_w.shape[0]
    NE, NO = even_w_in.shape[0], odd_w_in.shape[0]
    me = 4 * lax.axis_index("x") + 2 * lax.axis_index("y") + lax.axis_index("c")
    x0 = x[0]
    target = loss_target[0]

    small_parts = [c, even_sc_conv_w, odd_conv_w, odd_conv_b, odd_ln_g, odd_ln_b]
    small_shapes = [p.shape for p in small_parts]
    sg = _exchange([_pack(small_parts, F32, SUBLANES)], False, "gather_small")[0].reshape(N_DEV, -1)
    c_all, scw_g, ocw_g, ocb_g, olg_g, olb_g = _unpack(sg, small_shapes)
    c_all = c_all.reshape(N_DEV, D)
    sc_conv_w = _gather_cols(scw_g, even_sc_conv_w.shape)
    o_conv_w = _gather_cols(ocw_g, odd_conv_w.shape)
    o_conv_b = _gather_cols(ocb_g, odd_conv_b.shape)
    o_ln_g = _gather_cols(olg_g, odd_ln_g.shape)
    o_ln_b = _gather_cols(olb_g, odd_ln_b.shape)

    pad_q = HEAD_PAD - QK_NOPE - QK_ROPE
    w_local = [jnp.swapaxes(even_w_in, 1, 2).astype(BF16),
               jnp.pad(even_w_uq, ((0, 0), (0, 0), (0, pad_q))).astype(BF16),
               jnp.pad(even_w_ukv[..., :QK_NOPE], ((0, 0), (0, 0), (0, HEAD_PAD - QK_NOPE))).astype(BF16),
               even_w_ukv[..., QK_NOPE:].astype(BF16),
               even_w_out.astype(BF16), odd_w_in.astype(BF16), odd_w_out.astype(BF16)]
    ewt_g, eq_g, ek_g, ev_g, eout_g, owin_g, oout_g = _exchange(w_local, False, "gather_weights")
    heads_to_cols = lambda g: jnp.moveaxis(g, 0, 1).reshape(g.shape[1], -1)
    e_w_in_k, e_w_q_k, e_w_kv_k, e_w_out, o_w_in, o_w_out = [], [], [], [], [], []
    for i in range(NE):
        wt = ewt_g[:, i].reshape(-1, D)
        e_w_in_k.append(jnp.concatenate([wt[:2048], wt[2464:2976], wt[2048:2432], jnp.zeros((QK_NOPE, D), BF16),
                                         wt[2432:2464], jnp.zeros((pad_q, D), BF16)], axis=0))
        e_w_q_k.append(heads_to_cols(eq_g[:, i]))
        e_w_kv_k.append(jnp.concatenate([heads_to_cols(ek_g[:, i]), heads_to_cols(ev_g[:, i])], axis=-1))
        e_w_out.append(eout_g[:, i].reshape(-1, D))
    for i in range(NO):
        o_w_in.append(heads_to_cols(owin_g[:, i]))
        o_w_out.append(oout_g[:, i].reshape(-1, D))

    n_ada = ada_w.shape[2]
    ada_b_cols = lax.dynamic_slice_in_dim(ada_b, me * n_ada, n_ada, axis=1).reshape(L, 1, n_ada)
    mod_slab = _ada_fwd(c_all, ada_w, ada_b_cols)
    mod_g = _exchange([_pack([mod_slab], F32, SUBLANES)], False, "gather_mod")[0].reshape(N_DEV, -1)
    mod_all = mod_g[:, :L * N_DEV * n_ada].reshape(N_DEV, L, N_DEV, n_ada)
    mod = lax.dynamic_index_in_dim(mod_all, me, axis=2, keepdims=False)
    mod = jnp.moveaxis(mod, 0, 1).reshape(L, 3 * D)
    shift, scale, gate = mod[:, :D], mod[:, D:2 * D], mod[:, 2 * D:]

    half = QK_ROPE // 2
    inv_freq = 1.0 / (ROPE_THETA ** (jnp.arange(0, QK_ROPE, 2, dtype=F32) / QK_ROPE))
    inv_lane = jnp.zeros((HEAD_PAD,), F32).at[QK_NOPE:QK_NOPE + QK_ROPE].set(jnp.concatenate([inv_freq, inv_freq]))
    tabs = _rope_tables(positions.astype(F32).reshape(S, 1), inv_lane.reshape(1, HEAD_PAD))
    del half

    row = lambda a: a.reshape(1, -1)
    scb = even_sc_conv_b
    KP3, KP31 = SUBLANES, 32

    saved = []
    xs = x0
    for layer in range(L):
        i = layer // 2
        tag = f"l{layer}"
        h = _pre_norm(xs, row(pre_norm_g[layer]), row(scale[layer]), row(shift[layer]), f"pre_norm_{tag}")
        if layer % 2 == 0:
            z = _matmul(h, e_w_in_k[i], "nt", F32, f"w_in_{tag}", tn=1024)
            u, qn, kvn = _even_pre(z, row(even_q_norm_g[i]), row(even_kv_norm_g[i]), f"even_pre_{tag}")
            cw = jnp.pad(sc_conv_w[i], ((0, KP3 - SC_KERNEL), (0, 0)))
            cv = _conv_fwd(u, cw, row(scb[i]), SC_KERNEL, f"conv_{tag}")
            q, k, v = _qkv_fwd(qn, kvn, z, tabs, e_w_q_k[i], e_w_kv_k[i], f"qkv_{tag}")
            o, lse = _attn_fwd(q, k, v, f"attn_{tag}")
            ycat = _even_post(z, cv, o, f"even_post_{tag}")
            y = _matmul(ycat, e_w_out[i], "nn", F32, f"w_out_{tag}", tn=1024)
            saved.append(dict(x=xs, h=h, z=z, u=u, qn=qn, kvn=kvn, cw=cw, cv=cv, q=q, k=k, v=v, o=o, lse=lse,
                              ycat=ycat, y=y))
        else:
            z = _matmul(h, o_w_in[i], "nn", F32, f"w_in_{tag}", tn=1024)
            u = _odd_pre(z, f"odd_pre_{tag}")
            cw = jnp.pad(o_conv_w[i], ((0, KP31 - CONF_KERNEL), (0, 0)))
            cv = _conv_fwd(u, cw, row(o_conv_b[i]), CONF_KERNEL, f"conv_{tag}")
            yin = _odd_post(cv, z, row(o_ln_g[i]), row(o_ln_b[i]), f"odd_post_{tag}")
            y = _matmul(yin, o_w_out[i], "nn", F32, f"w_out_{tag}", tn=1024)
            saved.append(dict(x=xs, h=h, z=z, u=u, cw=cw, cv=cv, yin=yin, y=y))
        xs = _post_norm(xs, y, row(post_norm_g[layer]), row(gate[layer]), f"post_norm_{tag}")

    loss_row, dx = _loss_head(xs, target)
    loss = lax.psum(loss_row[0, 0], MESH_AXES)

    g_pre, g_post, dmod = [None] * L, [None] * L, [None] * L
    g_e_w_in, g_e_w_uq, g_e_w_ukv, g_e_w_out = [None] * NE, [None] * NE, [None] * NE, [None] * NE
    g_scw, g_scb, g_qg, g_kvg = [None] * NE, [None] * NE, [None] * NE, [None] * NE
    g_o_w_in, g_o_w_out, g_ocw, g_ocb, g_olg, g_olb = ([None] * NO for _ in range(6))
    for layer in reversed(range(L)):
        i = layer // 2
        tag = f"l{layer}"
        sv = saved[layer]
        dy, dgate, g_post[layer] = _post_norm_bwd(dx, sv["y"], row(post_norm_g[layer]), row(gate[layer]),
                                                  f"post_norm_bwd_{tag}")
        if layer % 2 == 0:
            dyc = _matmul(dy, e_w_out[i], "nt", F32, f"d_ycat_{tag}", tn=1024)
            g_e_w_out[i] = _matmul(sv["ycat"], dy, "tn", F32, f"g_w_out_{tag}", tk=512)
            dab, dag, dbg, dcv, do, delta = _even_bwd_gates(dyc, sv["z"], sv["cv"], sv["o"], f"even_gates_bwd_{tag}")
            du, dcw, g_scb[i] = _conv_bwd(dcv, sv["u"], sv["cw"], SC_KERNEL, f"conv_bwd_{tag}")
            g_scw[i] = dcw[:SC_KERNEL]
            dq = _attn_dq(sv["q"], sv["k"], sv["v"], do, sv["lse"], delta, f"attn_dq_{tag}")
            dk, dv = _attn_dkv(sv["q"], sv["k"], sv["v"], do, sv["lse"], delta, f"attn_dkv_{tag}")
            (dqp, dkvp, dcq, dckv, dkr, g_qg[i], g_kvg[i]) = _qkv_bwd(
                dq, dk, dv, sv["z"], tabs, e_w_q_k[i], e_w_kv_k[i],
                row(even_q_norm_g[i]), row(even_kv_norm_g[i]), f"qkv_bwd_{tag}")
            gq = _matmul(sv["qn"], dqp, "tn", F32, f"g_w_uq_{tag}", tk=512)
            gkv = _matmul(sv["kvn"], dkvp, "tn", F32, f"g_w_ukv_{tag}", tk=512)
            g_e_w_uq[i] = jnp.moveaxis(gq.reshape(QL, HEADS, HEAD_PAD)[..., :QK_NOPE + QK_ROPE], 1, 0)
            g_e_w_ukv[i] = jnp.moveaxis(jnp.concatenate(
                [gkv[:, :HEADS * HEAD_PAD].reshape(KVL, HEADS, HEAD_PAD)[..., :QK_NOPE],
                 gkv[:, HEADS * HEAD_PAD:].reshape(KVL, HEADS, V_HEAD)], axis=-1), 1, 0)
            g_e_w_out[i] = g_e_w_out[i].reshape(N_DEV, -1, D)
            dz = _even_dz(dab, du, sv["z"], dag, dbg, dcq, dckv, dkr, f"even_dz_{tag}")
            gt = _matmul(dz, sv["h"], "tn", F32, f"g_w_in_{tag}", tk=512, tn=1024)
            g_e_w_in[i] = jnp.concatenate([gt[:2048], gt[2560:2944], gt[2944 + QK_NOPE:2944 + QK_NOPE + QK_ROPE],
                                           gt[2048:2560]], axis=0).reshape(N_DEV, -1, D)
            dh = _matmul(dz, e_w_in_k[i], "nn", F32, f"d_h_{tag}", tn=1024)
        else:
            dyi = _matmul(dy, o_w_out[i], "nt", F32, f"d_yin_{tag}", tn=1024)
            g_o_w_out[i] = _matmul(sv["yin"], dy, "tn", F32, f"g_w_out_{tag}", tk=512).reshape(N_DEV, -1, D)
            dcv, dsg, g_olg[i], g_olb[i] = _odd_bwd_norm(dyi, sv["cv"], sv["z"], row(o_ln_g[i]), row(o_ln_b[i]),
                                                         f"odd_norm_bwd_{tag}")
            du, dcw, g_ocb[i] = _conv_bwd(dcv, sv["u"], sv["cw"], CONF_KERNEL, f"conv_bwd_{tag}")
            g_ocw[i] = dcw[:CONF_KERNEL]
            dz = _odd_dz(du, sv["z"], dsg, f"odd_dz_{tag}")
            g_o_w_in[i] = jnp.moveaxis(_matmul(sv["h"], dz, "tn", F32, f"g_w_in_{tag}", tk=512, tn=1024)
                                       .reshape(D, N_DEV, -1), 1, 0)
            dh = _matmul(dz, o_w_in[i], "nt", F32, f"d_h_{tag}", tn=1024)
        dx, dshift, dscale, g_pre[layer] = _pre_norm_bwd(dh, sv["x"], dx, row(pre_norm_g[layer]), row(scale[layer]),
                                                         f"pre_norm_bwd_{tag}")
        dmod[layer] = jnp.concatenate([dshift, dscale, dgate], axis=-1)
    grad_x = dx.reshape(1, S, D)

    rep_g = [jnp.concatenate(dmod, 0), jnp.concatenate(g_pre, 0), jnp.concatenate(g_post, 0),
             jnp.stack(g_scb), jnp.stack(g_qg), jnp.stack(g_kvg)]
    rep_w = [ada_b, pre_norm_g, post_norm_g, even_sc_conv_b, even_q_norm_g, even_kv_norm_g]
    rep_m = [m_ada_b, m_pre_norm_g, m_post_norm_g, m_even_sc_conv_b, m_even_q_norm_g, m_even_kv_norm_g]
    rep_v = [v_ada_b, v_pre_norm_g, v_post_norm_g, v_even_sc_conv_b, v_even_q_norm_g, v_even_kv_norm_g]
    rep_shapes = [w.shape for w in rep_w]
    rep_all = _exchange([_pack(rep_g, F32, SUBLANES)], False, "gather_small_grads")[0]
    rep_out = _adamw(rep_all, _pack(rep_w, F32, SUBLANES), _pack(rep_m, F32, SUBLANES), _pack(rep_v, F32, SUBLANES),
                     "adamw_replicated")
    rep_res = [_unpack(o.reshape(-1), rep_shapes) for o in rep_out]

    dmod_all = rep_all.reshape(N_DEV, -1)[:, :L * 3 * D].reshape(N_DEV, L, 3 * D)
    dmod_cols = jnp.moveaxis(lax.dynamic_slice_in_dim(dmod_all, me * n_ada, n_ada, axis=2), 0, 1)
    g_ada_w = _ada_bwd(c_all.T, dmod_cols)
    ada_out = _adamw(g_ada_w.reshape(1, -1, PACK_COLS), ada_w.reshape(-1, PACK_COLS),
                     m_ada_w.reshape(-1, PACK_COLS), v_ada_w.reshape(-1, PACK_COLS), "adamw_ada_w")
    ada_res = [o.reshape(ada_w.shape) for o in ada_out]

    sm_full = [_scatter_cols(jnp.stack(g_scw), even_sc_conv_w.shape[-1]),
               _scatter_cols(jnp.stack(g_ocw), odd_conv_w.shape[-1]),
               _scatter_cols(jnp.concatenate(g_ocb, 0), odd_conv_b.shape[-1]),
               _scatter_cols(jnp.concatenate(g_olg, 0), odd_ln_g.shape[-1]),
               _scatter_cols(jnp.concatenate(g_olb, 0), odd_ln_b.shape[-1])]
    sm_w = [even_sc_conv_w, odd_conv_w, odd_conv_b, odd_ln_g, odd_ln_b]
    sm_m = [m_even_sc_conv_w, m_odd_conv_w, m_odd_conv_b, m_odd_ln_g, m_odd_ln_b]
    sm_v = [v_even_sc_conv_w, v_odd_conv_w, v_odd_conv_b, v_odd_ln_g, v_odd_ln_b]
    sm_shapes = [w.shape for w in sm_w]
    sm_flat = jnp.concatenate([g.reshape(N_DEV, -1) for g in sm_full], axis=1)
    sm_rows = _pack(sm_w, F32, SUBLANES).shape[0]
    sm_slab = jnp.pad(sm_flat, ((0, 0), (0, sm_rows * PACK_COLS - sm_flat.shape[1]))).reshape(N_DEV, sm_rows, PACK_COLS)

    big = ([("even_w_in", i, g_e_w_in[i]) for i in range(NE)] + [("even_w_uq", i, g_e_w_uq[i]) for i in range(NE)]
           + [("even_w_ukv", i, g_e_w_ukv[i]) for i in range(NE)] + [("even_w_out", i, g_e_w_out[i]) for i in range(NE)]
           + [("odd_w_in", i, g_o_w_in[i]) for i in range(NO)] + [("odd_w_out", i, g_o_w_out[i]) for i in range(NO)])
    recv = _exchange([g for _, _, g in big] + [sm_slab], True, "scatter_grads")
    sm_out = _adamw(recv[-1], _pack(sm_w, F32, SUBLANES), _pack(sm_m, F32, SUBLANES), _pack(sm_v, F32, SUBLANES),
                    "adamw_small_sharded")
    sm_res = [_unpack(o.reshape(-1), sm_shapes) for o in sm_out]
    state = {"even_w_in": (even_w_in, m_even_w_in, v_even_w_in), "even_w_uq": (even_w_uq, m_even_w_uq, v_even_w_uq),
             "even_w_ukv": (even_w_ukv, m_even_w_ukv, v_even_w_ukv), "even_w_out": (even_w_out, m_even_w_out, v_even_w_out),
             "odd_w_in": (odd_w_in, m_odd_w_in, v_odd_w_in), "odd_w_out": (odd_w_out, m_odd_w_out, v_odd_w_out)}
    big_res = {name: [[None] * len(state[name][0]) for _ in range(4)] for name in state}
    for (name, i, _), parts in zip(big, recv[:-1]):
        transposed = name == "even_w_in"
        wmv = [t[i].T if transposed else t[i] for t in state[name]]
        res = _adamw(parts, *wmv, f"adamw_{name}_{i}")
        for kind in range(4):
            big_res[name][kind][i] = res[kind].T if transposed else res[kind]
    sh_res = [dict(zip(["even_sc_conv_w", "odd_conv_w", "odd_conv_b", "odd_ln_g", "odd_ln_b"], sm_res[kind]))
              for kind in range(4)]
    for name in state:
        for kind in range(4):
            sh_res[kind][name] = jnp.stack(big_res[name][kind])

    order = ["ada_w", "ada_b", "pre_norm_g", "post_norm_g", "even_w_in", "even_sc_conv_w", "even_sc_conv_b",
             "even_q_norm_g", "even_kv_norm_g", "even_w_uq", "even_w_ukv", "even_w_out", "odd_w_in", "odd_conv_w",
             "odd_conv_b", "odd_ln_g", "odd_ln_b", "odd_w_out"]
    rep_names = ["ada_b", "pre_norm_g", "post_norm_g", "even_sc_conv_b", "even_q_norm_g", "even_kv_norm_g"]
    outs = [loss, grad_x]
    for kind in range(4):
        for name in order:
            if name == "ada_w":
                outs.append(ada_res[kind])
            elif name in rep_names:
                outs.append(rep_res[kind][rep_names.index(name)])
            else:
                outs.append(sh_res[kind][name])
    return tuple(outs)
```

```python
import functools
import math

import jax
import jax.numpy as jnp
from jax import lax
from jax.experimental import pallas as pl
from jax.experimental.pallas import tpu as pltpu

F32 = jnp.float32
BF16 = jnp.bfloat16
MESH_AXES = ("x", "y", "c")
N_DEV = 8
EPS = 1e-6
CHUNK = 64
HEADS = 8
QK_NOPE = 64
QK_ROPE = 32
V_HEAD = 64
HEAD_PAD = 128
ROPE_THETA = 10000.0
SC_KERNEL = 3
CONF_KERNEL = 31
LANES = 128
SUBLANES = 8
PACK_COLS = 1024
VMEM_LIMIT = 48 * 1024 * 1024
NEG = -1e30

ADAM_LR = 0.001
ADAM_B1 = 0.9
ADAM_B2 = 0.999
ADAM_EPS = 1e-08
ADAM_WD = 0.01
ADAM_STEP = 10


def _cparams():
    return pltpu.CompilerParams(vmem_limit_bytes=VMEM_LIMIT)


def _sigmoid(x):
    return 1.0 / (1.0 + jnp.exp(-x))


def _silu(x):
    return x * _sigmoid(x)


def _dsilu(x):
    s = _sigmoid(x)
    return s * (1.0 + x * (1.0 - s))


def _rows(T, width, cb=0):
    return pl.BlockSpec((T, width), lambda i: (i, cb))


def _const(shape):
    nd = len(shape)
    return pl.BlockSpec(shape, lambda i: (0,) * nd)


def _row_tile(S):
    return min(256, S)


def _exchange(srcs, scatter, name):
    n = len(srcs)
    shapes = [tuple(s.shape[1:]) if scatter else tuple(s.shape) for s in srcs]

    def body(*refs):
        src_refs, out_refs = refs[:n], refs[n:2 * n]
        send_sems, recv_sems, local_sems = refs[2 * n:]
        x, y, c = lax.axis_index("x"), lax.axis_index("y"), lax.axis_index("c")
        me = 4 * x + 2 * y + c
        owns, copies = [], []
        for a in range(n):
            def piece(d, a=a):
                return src_refs[a].at[d] if scatter else src_refs[a]

            own = pltpu.make_async_copy(piece(me), out_refs[a].at[me], local_sems.at[a])
            own.start()
            owns.append(own)
            for k in range(1, N_DEV):
                px, py, pc = x ^ ((k >> 2) & 1), y ^ ((k >> 1) & 1), c ^ (k & 1)
                peer = 4 * px + 2 * py + pc
                sem = a * (N_DEV - 1) + k - 1
                cp = pltpu.make_async_remote_copy(
                    src_ref=piece(peer), dst_ref=out_refs[a].at[me],
                    send_sem=send_sems.at[sem], recv_sem=recv_sems.at[sem],
                    device_id=(px, py, pc), device_id_type=pl.DeviceIdType.MESH)
                cp.start()
                arrival = pltpu.make_async_remote_copy(
                    src_ref=piece(peer), dst_ref=out_refs[a].at[peer],
                    send_sem=send_sems.at[sem], recv_sem=recv_sems.at[sem],
                    device_id=(x, y, c), device_id_type=pl.DeviceIdType.MESH)
                copies.append((cp, arrival))
        for _, arrival in copies:
            arrival.wait_recv()
        for cp, _ in copies:
            cp.wait_send()
        for own in owns:
            own.wait()

    return pl.pallas_call(
        body, name=name,
        out_shape=tuple(jax.ShapeDtypeStruct((N_DEV,) + shp, s.dtype) for shp, s in zip(shapes, srcs)),
        in_specs=[pl.BlockSpec(memory_space=pl.ANY)] * n,
        out_specs=tuple(pl.BlockSpec(memory_space=pl.ANY) for _ in range(n)),
        scratch_shapes=[pltpu.SemaphoreType.DMA((n * (N_DEV - 1),)),
                        pltpu.SemaphoreType.DMA((n * (N_DEV - 1),)),
                        pltpu.SemaphoreType.DMA((n,))],
    )(*srcs)


def _pack(parts, dtype, row_mult):
    flat = jnp.concatenate([p.reshape(-1).astype(dtype) for p in parts])
    n = flat.shape[0]
    rows = -(-n // PACK_COLS)
    rows = -(-rows // row_mult) * row_mult
    flat = jnp.pad(flat, (0, rows * PACK_COLS - n))
    return flat.reshape(rows, PACK_COLS)


def _unpack(flat, shapes):
    out, off = [], 0
    for shp in shapes:
        n = math.prod(shp)
        out.append(flat[..., off:off + n].reshape(flat.shape[:-1] + tuple(shp)))
        off += n
    return out


_DIMS = {"nn": (((1,), (0,)), ((), ())), "nt": (((1,), (1,)), ((), ())), "tn": (((0,), (0,)), ((), ()))}


def _matmul(a, b, mode, out_dtype, name, tm=512, tn=512, tk=None):
    if mode == "nn":
        (M, K), (_, N) = a.shape, b.shape
    elif mode == "nt":
        (M, K), (N, _) = a.shape, b.shape
    else:
        (K, M), (_, N) = a.shape, b.shape
    tm, tn = min(tm, M), min(tn, N)
    tk = K if tk is None else min(tk, K)
    nk = K // tk
    assert M % tm == 0 and N % tn == 0 and K % tk == 0, (name, a.shape, b.shape)

    def body(a_ref, b_ref, o_ref, *scratch):
        p = lax.dot_general(a_ref[...].astype(BF16), b_ref[...].astype(BF16), _DIMS[mode],
                            preferred_element_type=F32)
        if nk == 1:
            o_ref[...] = p.astype(out_dtype)
        else:
            acc = scratch[0]
            k = pl.program_id(2)

            @pl.when(k == 0)
            def _():
                acc[...] = p

            @pl.when(k > 0)
            def _():
                acc[...] += p

            @pl.when(k == nk - 1)
            def _():
                o_ref[...] = acc[...].astype(out_dtype)

    a_spec = (pl.BlockSpec((tk, tm), lambda i, j, k: (k, i)) if mode == "tn"
              else pl.BlockSpec((tm, tk), lambda i, j, k: (i, k)))
    b_spec = (pl.BlockSpec((tn, tk), lambda i, j, k: (j, k)) if mode == "nt"
              else pl.BlockSpec((tk, tn), lambda i, j, k: (k, j)))
    return pl.pallas_call(
        body, name=name, grid=(M // tm, N // tn, nk),
        out_shape=jax.ShapeDtypeStruct((M, N), out_dtype),
        in_specs=[a_spec, b_spec],
        out_specs=pl.BlockSpec((tm, tn), lambda i, j, k: (i, j)),
        scratch_shapes=[pltpu.VMEM((tm, tn), F32)] if nk > 1 else [],
        compiler_params=_cparams(),
    )(a, b)


def _ada_fwd(c_all, ada_w, ada_b_cols):
    L, D, n = ada_w.shape

    def body(c_ref, w_ref, b_ref, o_ref):
        act = _silu(c_ref[...]).astype(BF16)
        o_ref[0] = jnp.dot(act, w_ref[0].astype(BF16), preferred_element_type=F32) + b_ref[0]

    return pl.pallas_call(
        body, name="ada_fwd", grid=(L,),
        out_shape=jax.ShapeDtypeStruct((L, N_DEV, n), F32),
        in_specs=[pl.BlockSpec((N_DEV, D), lambda l: (0, 0)),
                  pl.BlockSpec((1, D, n), lambda l: (l, 0, 0)),
                  pl.BlockSpec((1, 1, n), lambda l: (l, 0, 0))],
        out_specs=pl.BlockSpec((1, N_DEV, n), lambda l: (l, 0, 0)),
        compiler_params=_cparams(),
    )(c_all, ada_w, ada_b_cols)


def _ada_bwd(c_all_t, dmod_cols):
    D = c_all_t.shape[0]
    L, _, n = dmod_cols.shape

    def body(c_ref, d_ref, o_ref):
        act = _silu(c_ref[...])
        dm = d_ref[0]
        acc = act[:, 0:1] * dm[0:1, :]
        for b in range(1, N_DEV):
            acc = acc + act[:, b:b + 1] * dm[b:b + 1, :]
        o_ref[0] = acc

    return pl.pallas_call(
        body, name="ada_bwd", grid=(L,),
        out_shape=jax.ShapeDtypeStruct((L, D, n), F32),
        in_specs=[pl.BlockSpec((D, N_DEV), lambda l: (0, 0)),
                  pl.BlockSpec((1, N_DEV, n), lambda l: (l, 0, 0))],
        out_specs=pl.BlockSpec((1, D, n), lambda l: (l, 0, 0)),
        compiler_params=_cparams(),
    )(c_all_t, dmod_cols)


def _rope_tables(pos_col, inv_lane):
    S = pos_col.shape[0]
    T = _row_tile(S)
    half = QK_ROPE // 2

    def body(p_ref, f_ref, c_ref, up_ref, dn_ref):
        ang = p_ref[...] * f_ref[...]
        lane = lax.broadcasted_iota(jnp.int32, ang.shape, 1)
        first = (lane >= QK_NOPE) & (lane < QK_NOPE + half)
        second = (lane >= QK_NOPE + half) & (lane < QK_NOPE + QK_ROPE)
        cs, sn = jnp.cos(ang), jnp.sin(ang)
        c_ref[...] = jnp.where(first | second, cs, 1.0)
        up_ref[...] = jnp.where(first, -sn, 0.0)
        dn_ref[...] = jnp.where(second, sn, 0.0)

    tab = jax.ShapeDtypeStruct((S, HEAD_PAD), F32)
    return pl.pallas_call(
        body, name="rope_tables", grid=(S // T,),
        out_shape=(tab, tab, tab),
        in_specs=[_rows(T, 1), _const((1, HEAD_PAD))],
        out_specs=(_rows(T, HEAD_PAD),) * 3,
        compiler_params=_cparams(),
    )(pos_col, inv_lane)


def _rope(blk, ct, ut, dt):
    half = QK_ROPE // 2
    up = pltpu.roll(blk, HEAD_PAD - half, 1)
    dn = pltpu.roll(blk, half, 1)
    return blk * ct + up * ut + dn * dt


def _rope_t(d, ct, ut, dt):
    half = QK_ROPE // 2
    return d * ct + pltpu.roll(d * ut, half, 1) + pltpu.roll(d * dt, HEAD_PAD - half, 1)


def _pre_norm(x, g, scale, shift, name):
    S, D = x.shape
    T = _row_tile(S)

    def body(x_ref, g_ref, sc_ref, sh_ref, h_ref):
        xv = x_ref[...]
        rstd = lax.rsqrt(jnp.mean(xv * xv, axis=-1, keepdims=True) + EPS)
        h_ref[...] = ((xv * rstd) * g_ref[...] * (1.0 + sc_ref[...]) + sh_ref[...]).astype(BF16)

    return pl.pallas_call(
        body, name=name, grid=(S // T,),
        out_shape=jax.ShapeDtypeStruct((S, D), BF16),
        in_specs=[_rows(T, D), _const((1, D)), _const((1, D)), _const((1, D))],
        out_specs=_rows(T, D), compiler_params=_cparams(),
    )(x, g, scale, shift)


def _post_norm(x, y, g, gate, name):
    S, D = x.shape
    T = _row_tile(S)

    def body(x_ref, y_ref, g_ref, gt_ref, o_ref):
        yv = y_ref[...]
        rstd = lax.rsqrt(jnp.mean(yv * yv, axis=-1, keepdims=True) + EPS)
        o_ref[...] = x_ref[...] + gt_ref[...] * ((yv * rstd) * g_ref[...])

    return pl.pallas_call(
        body, name=name, grid=(S // T,),
        out_shape=jax.ShapeDtypeStruct((S, D), F32),
        in_specs=[_rows(T, D), _rows(T, D), _const((1, D)), _const((1, D))],
        out_specs=_rows(T, D), compiler_params=_cparams(),
    )(x, y, g, gate)


def _fold8(v):
    T, C = v.shape
    return v.reshape(T // SUBLANES, SUBLANES, C).sum(axis=0)


def _col_sums(n_sums, body_fn, ins, in_specs, outs, out_specs, S, T, widths, name):
    n_in, n_out = len(ins), len(outs)
    nt = S // T

    def body(*refs):
        in_refs = refs[:n_in]
        out_refs = refs[n_in:n_in + n_out]
        sum_refs = refs[n_in + n_out:n_in + n_out + n_sums]
        accs = refs[n_in + n_out + n_sums:]
        i = pl.program_id(0)
        terms = body_fn(in_refs, out_refs)

        @pl.when(i == 0)
        def _():
            for acc, t in zip(accs, terms):
                acc[...] = _fold8(t)

        @pl.when(i > 0)
        def _():
            for acc, t in zip(accs, terms):
                acc[...] += _fold8(t)

        @pl.when(i == nt - 1)
        def _():
            for acc, s_ref in zip(accs, sum_refs):
                s_ref[...] = jnp.sum(acc[...], axis=0, keepdims=True)

    return pl.pallas_call(
        body, name=name, grid=(nt,),
        out_shape=tuple(outs) + tuple(jax.ShapeDtypeStruct((1, w), F32) for w in widths),
        in_specs=in_specs,
        out_specs=tuple(out_specs) + tuple(_const((1, w)) for w in widths),
        scratch_shapes=[pltpu.VMEM((SUBLANES, w), F32) for w in widths],
        compiler_params=_cparams(),
    )(*ins)


def _post_norm_bwd(dxo, y, g, gate, name):
    S, D = y.shape
    T = _row_tile(S)

    def fn(ins, outs):
        dxo_ref, y_ref, g_ref, gt_ref = ins
        yv, dv = y_ref[...], dxo_ref[...]
        rstd = lax.rsqrt(jnp.mean(yv * yv, axis=-1, keepdims=True) + EPS)
        yh = yv * rstd
        dn = dv * gt_ref[...]
        dyh = dn * g_ref[...]
        outs[0][...] = (rstd * (dyh - yh * jnp.mean(dyh * yh, axis=-1, keepdims=True))).astype(BF16)
        return [dv * (yh * g_ref[...]), dn * yh]

    return _col_sums(2, fn, [dxo, y, g, gate],
                     [_rows(T, D), _rows(T, D), _const((1, D)), _const((1, D))],
                     [jax.ShapeDtypeStruct((S, D), BF16)], [_rows(T, D)], S, T, [D, D], name)


def _pre_norm_bwd(dh, x, dxo, g, scale, name):
    S, D = x.shape
    T = _row_tile(S)

    def fn(ins, outs):
        dh_ref, x_ref, dxo_ref, g_ref, sc_ref = ins
        xv, dv = x_ref[...], dh_ref[...]
        rstd = lax.rsqrt(jnp.mean(xv * xv, axis=-1, keepdims=True) + EPS)
        xh = xv * rstd
        dr = dv * (1.0 + sc_ref[...])
        dxh = dr * g_ref[...]
        outs[0][...] = dxo_ref[...] + rstd * (dxh - xh * jnp.mean(dxh * xh, axis=-1, keepdims=True))
        return [dv, dv * (xh * g_ref[...]), dr * xh]

    return _col_sums(3, fn, [dh, x, dxo, g, scale],
                     [_rows(T, D), _rows(T, D), _rows(T, D), _const((1, D)), _const((1, D))],
                     [jax.ShapeDtypeStruct((S, D), F32)], [_rows(T, D)], S, T, [D, D, D], name)


def _loss_head(x, target):
    S, D = x.shape
    T = _row_tile(S)
    nt = S // T

    def body(x_ref, t_ref, l_ref, dx_ref, acc):
        i = pl.program_id(0)
        e = x_ref[...] - t_ref[...]
        dx_ref[...] = e * (1.0 / D)
        part = _fold8(e * e)

        @pl.when(i == 0)
        def _():
            acc[...] = part

        @pl.when(i > 0)
        def _():
            acc[...] += part

        @pl.when(i == nt - 1)
        def _():
            tot = jnp.sum(jnp.sum(acc[...], axis=0, keepdims=True), axis=1, keepdims=True)
            l_ref[...] = jnp.broadcast_to(tot * (0.5 / D), (1, LANES))

    return pl.pallas_call(
        body, name="loss_head", grid=(nt,),
        out_shape=(jax.ShapeDtypeStruct((1, LANES), F32), jax.ShapeDtypeStruct((S, D), F32)),
        in_specs=[_rows(T, D), _rows(T, D)],
        out_specs=(_const((1, LANES)), _rows(T, D)),
        scratch_shapes=[pltpu.VMEM((SUBLANES, D), F32)],
        compiler_params=_cparams(),
    )(x, target)


CONV_ROWS = 64


def _conv_halo(K):
    return SUBLANES if K - 1 <= SUBLANES else 32


def _conv_fwd(u, w, b, K, name):
    S, C = u.shape
    KP = w.shape[0]
    T, HB, RS = min(512, S), _conv_halo(K), CONV_ROWS
    ratio = T // HB

    def body(u_ref, h_ref, w_ref, b_ref, o_ref, ext):
        i = pl.program_id(1)
        ext[0:HB, :] = jnp.where(i > 0, h_ref[...], 0.0)
        ext[HB:HB + T, :] = u_ref[...]
        for r0 in range(0, T, RS):
            acc = jnp.broadcast_to(b_ref[...], (RS, LANES))
            for k in range(K):
                off = HB - (K - 1) + k + r0
                acc = acc + w_ref[k:k + 1, :] * ext[off:off + RS, :]
            o_ref[r0:r0 + RS, :] = acc

    return pl.pallas_call(
        body, name=name, grid=(C // LANES, S // T),
        out_shape=jax.ShapeDtypeStruct((S, C), F32),
        in_specs=[pl.BlockSpec((T, LANES), lambda c, i: (i, c)),
                  pl.BlockSpec((HB, LANES), lambda c, i: (jnp.maximum(i * ratio - 1, 0), c)),
                  pl.BlockSpec((KP, LANES), lambda c, i: (0, c)),
                  pl.BlockSpec((1, LANES), lambda c, i: (0, c))],
        out_specs=pl.BlockSpec((T, LANES), lambda c, i: (i, c)),
        scratch_shapes=[pltpu.VMEM((HB + T, LANES), F32)],
        compiler_params=_cparams(),
    )(u, u, w, b)


def _conv_bwd(d, u, w, K, name):
    S, C = u.shape
    KP = w.shape[0]
    T, HB, RS = min(512, S), _conv_halo(K), CONV_ROWS
    ratio = T // HB
    nt = S // T
    last_halo = S // HB - 1

    def body(d_ref, dn_ref, u_ref, up_ref, w_ref, du_ref, dw_ref, db_ref, extd, extu, dws, dbs):
        i = pl.program_id(1)
        extd[0:T, :] = d_ref[...]
        extd[T:T + HB, :] = jnp.where(i < nt - 1, dn_ref[...], 0.0)
        extu[0:HB, :] = jnp.where(i > 0, up_ref[...], 0.0)
        extu[HB:HB + T, :] = u_ref[...]

        @pl.when(i == 0)
        def _():
            dws[...] = jnp.zeros_like(dws)
            dbs[...] = jnp.zeros_like(dbs)

        for r0 in range(0, T, RS):
            acc = jnp.zeros((RS, LANES), F32)
            for k in range(K):
                off = (K - 1 - k) + r0
                acc = acc + w_ref[k:k + 1, :] * extd[off:off + RS, :]
            du_ref[r0:r0 + RS, :] = acc
            dch = d_ref[r0:r0 + RS, :]
            dbs[...] += _fold8(dch)
            for k in range(K):
                off = HB - (K - 1) + k + r0
                dws[k * SUBLANES:(k + 1) * SUBLANES, :] += _fold8(dch * extu[off:off + RS, :])

        @pl.when(i == nt - 1)
        def _():
            dw_ref[...] = jnp.zeros_like(dw_ref)
            for k in range(K):
                dw_ref[k:k + 1, :] = jnp.sum(dws[k * SUBLANES:(k + 1) * SUBLANES, :], axis=0, keepdims=True)
            db_ref[...] = jnp.sum(dbs[...], axis=0, keepdims=True)

    return pl.pallas_call(
        body, name=name, grid=(C // LANES, nt),
        out_shape=(jax.ShapeDtypeStruct((S, C), F32), jax.ShapeDtypeStruct((KP, C), F32),
                   jax.ShapeDtypeStruct((1, C), F32)),
        in_specs=[pl.BlockSpec((T, LANES), lambda c, i: (i, c)),
                  pl.BlockSpec((HB, LANES), lambda c, i: (jnp.minimum((i + 1) * ratio, last_halo), c)),
                  pl.BlockSpec((T, LANES), lambda c, i: (i, c)),
                  pl.BlockSpec((HB, LANES), lambda c, i: (jnp.maximum(i * ratio - 1, 0), c)),
                  pl.BlockSpec((KP, LANES), lambda c, i: (0, c))],
        out_specs=(pl.BlockSpec((T, LANES), lambda c, i: (i, c)),
                   pl.BlockSpec((KP, LANES), lambda c, i: (0, c)),
                   pl.BlockSpec((1, LANES), lambda c, i: (0, c))),
        scratch_shapes=[pltpu.VMEM((T + HB, LANES), F32), pltpu.VMEM((HB + T, LANES), F32),
                        pltpu.VMEM((KP * SUBLANES, LANES), F32), pltpu.VMEM((SUBLANES, LANES), F32)],
        compiler_params=_cparams(),
    )(d, d, u, u, w)


SCW = 512
ZE = 3072
QL = 256
KVL = 128


def _rms_rows(x, g):
    rstd = lax.rsqrt(jnp.mean(x * x, axis=-1, keepdims=True) + EPS)
    return (x * rstd) * g


def _even_pre(z, qg, kvg, name):
    S = z.shape[0]
    T = _row_tile(S)

    def body(ac_ref, ax_ref, cq_ref, ckv_ref, qg_ref, kvg_ref, u_ref, qn_ref, kvn_ref):
        u_ref[...] = ac_ref[...] * ax_ref[...]
        qn_ref[...] = _rms_rows(cq_ref[...], qg_ref[...]).astype(BF16)
        kvn_ref[...] = _rms_rows(ckv_ref[...], kvg_ref[...]).astype(BF16)

    return pl.pallas_call(
        body, name=name, grid=(S // T,),
        out_shape=(jax.ShapeDtypeStruct((S, SCW), F32), jax.ShapeDtypeStruct((S, QL), BF16),
                   jax.ShapeDtypeStruct((S, KVL), BF16)),
        in_specs=[_rows(T, SCW, 1), _rows(T, SCW, 2), _rows(T, QL, 10), _rows(T, KVL, 22),
                  _const((1, QL)), _const((1, KVL))],
        out_specs=(_rows(T, SCW), _rows(T, QL), _rows(T, KVL)),
        compiler_params=_cparams(),
    )(z, z, z, z, qg, kvg)


def _qkv_fwd(qn, kvn, z, tabs, w_q, w_kv, name):
    S = qn.shape[0]
    T = _row_tile(S)
    HW = HEADS * HEAD_PAD
    scale = 1.0 / math.sqrt(QK_NOPE + QK_ROPE)

    def body(qn_ref, kvn_ref, kr_ref, ct_ref, ut_ref, dt_ref, wq_ref, wkv_ref, q_ref, k_ref, v_ref):
        ct, ut, dt = ct_ref[...], ut_ref[...], dt_ref[...]
        qa = jnp.dot(qn_ref[...], wq_ref[...], preferred_element_type=F32)
        kva = jnp.dot(kvn_ref[...], wkv_ref[...], preferred_element_type=F32)
        kr = kr_ref[...]
        for h in range(HEADS):
            sl = slice(h * HEAD_PAD, (h + 1) * HEAD_PAD)
            q_ref[:, sl] = (_rope(qa[:, sl], ct, ut, dt) * scale).astype(BF16)
            k_ref[:, sl] = _rope(kva[:, sl] + kr, ct, ut, dt).astype(BF16)
        v_ref[...] = kva[:, HW:].astype(BF16)

    return pl.pallas_call(
        body, name=name, grid=(S // T,),
        out_shape=(jax.ShapeDtypeStruct((S, HW), BF16), jax.ShapeDtypeStruct((S, HW), BF16),
                   jax.ShapeDtypeStruct((S, HEADS * V_HEAD), BF16)),
        in_specs=[_rows(T, QL), _rows(T, KVL), _rows(T, HEAD_PAD, 23),
                  _rows(T, HEAD_PAD), _rows(T, HEAD_PAD), _rows(T, HEAD_PAD),
                  _const(w_q.shape), _const(w_kv.shape)],
        out_specs=(_rows(T, HW), _rows(T, HW), _rows(T, HEADS * V_HEAD)),
        compiler_params=_cparams(),
    )(qn, kvn, z, *tabs, w_q, w_kv)


def _attn_tile(S):
    return min(256, S)


def _chunk_mask(TQ):
    r = lax.broadcasted_iota(jnp.int32, (TQ, TQ), 0) // CHUNK
    c = lax.broadcasted_iota(jnp.int32, (TQ, TQ), 1) // CHUNK
    return c <= r


_NT = (((1,), (1,)), ((), ()))
_TN = (((0,), (0,)), ((), ()))


def _attn_fwd(q, k, v, name):
    S = q.shape[0]
    TQ = _attn_tile(S)
    nq = S // TQ
    PW = 2 * HEAD_PAD

    def body(q_ref, k_ref, v_ref, o_ref, lse_ref, m_s, l_s, acc_s):
        i = pl.program_id(1)
        left = lax.broadcasted_iota(jnp.int32, (TQ, LANES), 1) < V_HEAD
        m_s[...] = jnp.full_like(m_s, NEG)
        l_s[...] = jnp.zeros_like(l_s)
        acc_s[...] = jnp.zeros_like(acc_s)
        qv = q_ref[...]

        def step(j, masked):
            r0 = pl.multiple_of(j * TQ, TQ)
            kb = k_ref[pl.ds(r0, TQ), :]
            vb = v_ref[pl.ds(r0, TQ), :]
            alphas, pvs = [], []
            for h in range(2):
                sl = slice(h * HEAD_PAD, (h + 1) * HEAD_PAD)
                s = lax.dot_general(qv[:, sl], kb[:, sl], _NT, preferred_element_type=F32)
                if masked:
                    s = jnp.where(_chunk_mask(TQ), s, NEG)
                m_prev = m_s[h]
                m_new = jnp.maximum(m_prev, jnp.max(s, axis=1, keepdims=True))
                alpha = jnp.exp(m_prev - m_new)
                p = jnp.exp(s - m_new[:, 0:1])
                l_s[h] = alpha * l_s[h] + jnp.sum(p, axis=1, keepdims=True)
                m_s[h] = m_new
                alphas.append(alpha)
                pvs.append(jnp.dot(p.astype(BF16), vb, preferred_element_type=F32))
            acc_s[...] = acc_s[...] * jnp.where(left, alphas[0], alphas[1]) + jnp.where(left, pvs[0], pvs[1])

        def loop_body(j, carry):
            step(j, False)
            return carry

        lax.fori_loop(0, i, loop_body, 0)
        step(i, True)
        o_ref[...] = acc_s[...] / jnp.where(left, l_s[0], l_s[1])
        lse_ref[...] = jnp.where(left, m_s[0] + jnp.log(l_s[0]), m_s[1] + jnp.log(l_s[1]))

    return pl.pallas_call(
        body, name=name, grid=(HEADS // 2, nq),
        out_shape=(jax.ShapeDtypeStruct((S, HEADS * V_HEAD), F32), jax.ShapeDtypeStruct((S, HEADS * V_HEAD), F32)),
        in_specs=[pl.BlockSpec((TQ, PW), lambda p, i: (i, p)),
                  pl.BlockSpec((S, PW), lambda p, i: (0, p)),
                  pl.BlockSpec((S, LANES), lambda p, i: (0, p))],
        out_specs=(pl.BlockSpec((TQ, LANES), lambda p, i: (i, p)),
                   pl.BlockSpec((TQ, LANES), lambda p, i: (i, p))),
        scratch_shapes=[pltpu.VMEM((2, TQ, LANES), F32), pltpu.VMEM((2, TQ, LANES), F32),
                        pltpu.VMEM((TQ, LANES), F32)],
        compiler_params=_cparams(),
    )(q, k, v)


def _attn_dq(q, k, v, do, lse, delta, name):
    S = q.shape[0]
    TQ = _attn_tile(S)
    nq = S // TQ
    PW = 2 * HEAD_PAD

    def body(q_ref, k_ref, v_ref, do_ref, lse_ref, dl_ref, dq_ref, acc_s):
        i = pl.program_id(1)
        left = lax.broadcasted_iota(jnp.int32, (TQ, LANES), 1) < V_HEAD
        acc_s[...] = jnp.zeros_like(acc_s)
        qv = q_ref[...]
        dov = do_ref[...]
        dos = [jnp.where(left, dov, jnp.zeros_like(dov)), jnp.where(left, jnp.zeros_like(dov), dov)]
        lses = [lse_ref[:, 0:1], lse_ref[:, V_HEAD:V_HEAD + 1]]
        dls = [dl_ref[:, 0:1], dl_ref[:, V_HEAD:V_HEAD + 1]]

        def step(j, masked):
            r0 = pl.multiple_of(j * TQ, TQ)
            kb = k_ref[pl.ds(r0, TQ), :]
            vb = v_ref[pl.ds(r0, TQ), :]
            for h in range(2):
                sl = slice(h * HEAD_PAD, (h + 1) * HEAD_PAD)
                s = lax.dot_general(qv[:, sl], kb[:, sl], _NT, preferred_element_type=F32)
                p = jnp.exp(s - lses[h])
                if masked:
                    p = jnp.where(_chunk_mask(TQ), p, 0.0)
                dp = lax.dot_general(dos[h], vb, _NT, preferred_element_type=F32)
                ds = (p * (dp - dls[h])).astype(BF16)
                acc_s[:, sl] += jnp.dot(ds, kb[:, sl], preferred_element_type=F32)

        def loop_body(j, carry):
            step(j, False)
            return carry

        lax.fori_loop(0, i, loop_body, 0)
        step(i, True)
        dq_ref[...] = acc_s[...]

    return pl.pallas_call(
        body, name=name, grid=(HEADS // 2, nq),
        out_shape=jax.ShapeDtypeStruct((S, HEADS * HEAD_PAD), F32),
        in_specs=[pl.BlockSpec((TQ, PW), lambda p, i: (i, p)),
                  pl.BlockSpec((S, PW), lambda p, i: (0, p)),
                  pl.BlockSpec((S, LANES), lambda p, i: (0, p)),
                  pl.BlockSpec((TQ, LANES), lambda p, i: (i, p)),
                  pl.BlockSpec((TQ, LANES), lambda p, i: (i, p)),
                  pl.BlockSpec((TQ, LANES), lambda p, i: (i, p))],
        out_specs=pl.BlockSpec((TQ, PW), lambda p, i: (i, p)),
        scratch_shapes=[pltpu.VMEM((TQ, PW), F32)],
        compiler_params=_cparams(),
    )(q, k, v, do, lse, delta)


def _attn_dkv(q, k, v, do, lse, delta, name):
    S = q.shape[0]
    TQ = _attn_tile(S)
    nq = S // TQ
    PW = 2 * HEAD_PAD

    def body(q_ref, k_ref, v_ref, do_ref, lse_ref, dl_ref, dk_ref, dv_ref, dk_s, dv_s):
        j = pl.program_id(1)
        left = lax.broadcasted_iota(jnp.int32, (TQ, LANES), 1) < V_HEAD
        dk_s[...] = jnp.zeros_like(dk_s)
        dv_s[...] = jnp.zeros_like(dv_s)
        kb = k_ref[...]
        vb = v_ref[...]

        def step(i, masked):
            r0 = pl.multiple_of(i * TQ, TQ)
            qb = q_ref[pl.ds(r0, TQ), :]
            dov = do_ref[pl.ds(r0, TQ), :]
            lse = lse_ref[pl.ds(r0, TQ), :]
            dl = dl_ref[pl.ds(r0, TQ), :]
            dos = [jnp.where(left, dov, jnp.zeros_like(dov)), jnp.where(left, jnp.zeros_like(dov), dov)]
            for h in range(2):
                sl = slice(h * HEAD_PAD, (h + 1) * HEAD_PAD)
                c0 = h * V_HEAD
                s = lax.dot_general(qb[:, sl], kb[:, sl], _NT, preferred_element_type=F32)
                p = jnp.exp(s - lse[:, c0:c0 + 1])
                if masked:
                    p = jnp.where(_chunk_mask(TQ), p, 0.0)
                dv_s[...] += lax.dot_general(p.astype(BF16), dos[h], _TN, preferred_element_type=F32)
                dp = lax.dot_general(dos[h], vb, _NT, preferred_element_type=F32)
                ds = (p * (dp - dl[:, c0:c0 + 1])).astype(BF16)
                dk_s[:, sl] += lax.dot_general(ds, qb[:, sl], _TN, preferred_element_type=F32)

        def loop_body(i, carry):
            step(i, False)
            return carry

        step(j, True)
        lax.fori_loop(j + 1, nq, loop_body, 0)
        dk_ref[...] = dk_s[...]
        dv_ref[...] = dv_s[...]

    return pl.pallas_call(
        body, name=name, grid=(HEADS // 2, nq),
        out_shape=(jax.ShapeDtypeStruct((S, HEADS * HEAD_PAD), F32), jax.ShapeDtypeStruct((S, HEADS * V_HEAD), F32)),
        in_specs=[pl.BlockSpec((S, PW), lambda p, j: (0, p)),
                  pl.BlockSpec((TQ, PW), lambda p, j: (j, p)),
                  pl.BlockSpec((TQ, LANES), lambda p, j: (j, p)),
                  pl.BlockSpec((S, LANES), lambda p, j: (0, p)),
                  pl.BlockSpec((S, LANES), lambda p, j: (0, p)),
                  pl.BlockSpec((S, LANES), lambda p, j: (0, p))],
        out_specs=(pl.BlockSpec((TQ, PW), lambda p, j: (j, p)),
                   pl.BlockSpec((TQ, LANES), lambda p, j: (j, p))),
        scratch_shapes=[pltpu.VMEM((TQ, PW), F32), pltpu.VMEM((TQ, LANES), F32)],
        compiler_params=_cparams(),
    )(q, k, v, do, lse, delta)


ATTN_FWD_HEADS = 8
ATTN_BWD_HEADS = 4


def _chunk_mask_t(T):
    key = lax.broadcasted_iota(jnp.int32, (T, T), 0) // CHUNK
    qry = lax.broadcasted_iota(jnp.int32, (T, T), 1) // CHUNK
    return key <= qry


def _qkv_fwd_t(qn, kvn, z, tabs, w_q, w_kv, name):
    S = qn.shape[0]
    T = _attn_tile(S)
    HW = HEADS * HEAD_PAD
    scale = 1.0 / math.sqrt(QK_NOPE + QK_ROPE)

    def body(qn_ref, kvn_ref, kr_ref, ct_ref, ut_ref, dt_ref, wq_ref, wkv_ref, q_ref, k_ref, v_ref, kt_ref, vt_ref):
        ct, ut, dt = ct_ref[...], ut_ref[...], dt_ref[...]
        qa = jnp.dot(qn_ref[...], wq_ref[...], preferred_element_type=F32)
        kva = jnp.dot(kvn_ref[...], wkv_ref[...], preferred_element_type=F32)
        kr = kr_ref[...]
        ones_row = (lax.broadcasted_iota(jnp.int32, (V_HEAD, T), 0) == 0).astype(F32)
        for h in range(HEADS):
            sl = slice(h * HEAD_PAD, (h + 1) * HEAD_PAD)
            q_ref[:, sl] = (_rope(qa[:, sl], ct, ut, dt) * scale).astype(BF16)
            kh = _rope(kva[:, sl] + kr, ct, ut, dt)
            k_ref[:, sl] = kh.astype(BF16)
            kt_ref[0, sl, :] = kh.T.astype(BF16)
        v_ref[...] = kva[:, HW:].astype(BF16)
        for p in range(HEADS // 2):
            vpt = kva[:, HW + p * LANES:HW + (p + 1) * LANES].T
            for h in range(2):
                r0 = (2 * p + h) * HEAD_PAD
                vt_ref[0, r0:r0 + V_HEAD, :] = vpt[h * V_HEAD:(h + 1) * V_HEAD, :].astype(BF16)
                vt_ref[0, r0 + V_HEAD:r0 + HEAD_PAD, :] = ones_row.astype(BF16)

    t3 = jax.ShapeDtypeStruct((S // T, HW, T), BF16)
    return pl.pallas_call(
        body, name=name, grid=(S // T,),
        out_shape=(jax.ShapeDtypeStruct((S, HW), BF16), jax.ShapeDtypeStruct((S, HW), BF16),
                   jax.ShapeDtypeStruct((S, HEADS * V_HEAD), BF16), t3, t3),
        in_specs=[_rows(T, QL), _rows(T, KVL), _rows(T, HEAD_PAD, 23),
                  _rows(T, HEAD_PAD), _rows(T, HEAD_PAD), _rows(T, HEAD_PAD),
                  _const(w_q.shape), _const(w_kv.shape)],
        out_specs=(_rows(T, HW), _rows(T, HW), _rows(T, HEADS * V_HEAD),
                   pl.BlockSpec((1, HW, T), lambda i: (i, 0, 0)), pl.BlockSpec((1, HW, T), lambda i: (i, 0, 0))),
        compiler_params=_cparams(),
    )(qn, kvn, z, *tabs, w_q, w_kv)


def _attn_fwd_t(q, k, vT3, name):
    S = q.shape[0]
    T = _attn_tile(S)
    nq = S // T
    NH = ATTN_FWD_HEADS
    PW = NH * HEAD_PAD

    def body(q_ref, k_ref, vt_ref, o_ref, lse_ref, m_s, acc_s):
        i = pl.program_id(1)
        m_s[...] = jnp.full_like(m_s, NEG)
        acc_s[...] = jnp.zeros_like(acc_s)
        qv = q_ref[...]

        def step(j, masked):
            kb = k_ref[pl.ds(pl.multiple_of(j * T, T), T), :]
            vt = vt_ref[j]
            heads = [slice(h * HEAD_PAD, (h + 1) * HEAD_PAD) for h in range(NH)]
            sts = [lax.dot_general(kb[:, sl], qv[:, sl], _NT, preferred_element_type=F32) for sl in heads]
            alphas, pvs = [], []
            for h, sl in enumerate(heads):
                st = jnp.where(_chunk_mask_t(T), sts[h], NEG) if masked else sts[h]
                m_prev = m_s[h]
                m_new = jnp.maximum(m_prev, jnp.max(st, axis=0, keepdims=True))
                alphas.append(jnp.exp(m_prev[0:1] - m_new[0:1]))
                pt = jnp.exp(st - m_new[0:1]).astype(BF16)
                m_s[h] = m_new
                pvs.append(jnp.dot(vt[sl, :], pt, preferred_element_type=F32))
            for h in range(NH):
                acc_s[h] = acc_s[h] * alphas[h] + pvs[h]

        def loop_body(j, carry):
            step(j, False)
            return carry

        lax.fori_loop(0, i, loop_body, 0)
        step(i, True)
        for g in range(NH // 2):
            outs = []
            for h in (2 * g, 2 * g + 1):
                acc = acc_s[h]
                l_row = acc[V_HEAD:V_HEAD + 1, :]
                outs.append(acc[0:V_HEAD, :] / l_row)
                lse_ref[0, h * SUBLANES:(h + 1) * SUBLANES, :] = m_s[h] + jnp.log(l_row)
            o_ref[:, g * LANES:(g + 1) * LANES] = jnp.concatenate(outs, axis=0).T

    return pl.pallas_call(
        body, name=name, grid=(HEADS // NH, nq),
        out_shape=(jax.ShapeDtypeStruct((S, HEADS * V_HEAD), F32),
                   jax.ShapeDtypeStruct((nq, HEADS * SUBLANES, T), F32)),
        in_specs=[pl.BlockSpec((T, PW), lambda p, i: (i, p)),
                  pl.BlockSpec((S, PW), lambda p, i: (0, p)),
                  pl.BlockSpec((nq, PW, T), lambda p, i: (0, p, 0))],
        out_specs=(pl.BlockSpec((T, NH * V_HEAD), lambda p, i: (i, p)),
                   pl.BlockSpec((1, NH * SUBLANES, T), lambda p, i: (i, p, 0))),
        scratch_shapes=[pltpu.VMEM((NH, SUBLANES, T), F32), pltpu.VMEM((NH, HEAD_PAD, T), F32)],
        compiler_params=_cparams(),
    )(q, k, vT3)


def _attn_bwd_t(q, k, v, kT3, do, lse3, dl3, name):
    S = q.shape[0]
    T = _attn_tile(S)
    nq = S // T
    NH = ATTN_BWD_HEADS
    PW = NH * HEAD_PAD
    VW = NH * V_HEAD

    def body(q_ref, k_ref, v_ref, kt_ref, do_ref, lse_ref, dl_ref, dq_ref, dk_ref, dv_ref, dk_s, dv_s):
        j = pl.program_id(1)
        left = lax.broadcasted_iota(jnp.int32, (T, LANES), 1) < V_HEAD

        @pl.when(j == 0)
        def _():
            dq_ref[...] = jnp.zeros_like(dq_ref)

        dk_s[...] = jnp.zeros_like(dk_s)
        dv_s[...] = jnp.zeros_like(dv_s)
        kb = k_ref[...]
        vms = []
        for g in range(NH // 2):
            vb = v_ref[:, g * LANES:(g + 1) * LANES]
            vms += [jnp.where(left, vb, jnp.zeros_like(vb)), jnp.where(left, jnp.zeros_like(vb), vb)]
        kt = kt_ref[0]

        def step(i, masked):
            r0 = pl.multiple_of(i * T, T)
            qb = q_ref[pl.ds(r0, T), :]
            do_all = do_ref[pl.ds(r0, T), :]
            lse = lse_ref[i]
            dl = dl_ref[i]
            heads = [slice(h * HEAD_PAD, (h + 1) * HEAD_PAD) for h in range(NH)]
            dobs = [do_all[:, (h // 2) * LANES:(h // 2 + 1) * LANES] for h in range(NH)]
            sts = [lax.dot_general(kb[:, sl], qb[:, sl], _NT, preferred_element_type=F32) for sl in heads]
            dpts = [lax.dot_general(vms[h], dobs[h], _NT, preferred_element_type=F32) for h in range(NH)]
            res = []
            for h, sl in enumerate(heads):
                r8 = h * SUBLANES
                pt = jnp.exp(sts[h] - lse[r8:r8 + 1, :])
                if masked:
                    pt = jnp.where(_chunk_mask_t(T), pt, 0.0)
                dst = (pt * (dpts[h] - dl[r8:r8 + 1, :])).astype(BF16)
                res.append((jnp.dot(pt.astype(BF16), dobs[h], preferred_element_type=F32),
                            jnp.dot(dst, qb[:, sl], preferred_element_type=F32),
                            jnp.dot(kt[sl, :], dst, preferred_element_type=F32)))
            for h, sl in enumerate(heads):
                dv_s[h] += res[h][0]
                dk_s[:, sl] += res[h][1]
                dq_ref[i, sl, :] += res[h][2]

        def loop_body(i, carry):
            step(i, False)
            return carry

        step(j, True)
        lax.fori_loop(j + 1, nq, loop_body, 0)
        dk_ref[...] = dk_s[...]
        for g in range(NH // 2):
            dv_ref[:, g * LANES:(g + 1) * LANES] = jnp.where(left, dv_s[2 * g], dv_s[2 * g + 1])

    return pl.pallas_call(
        body, name=name, grid=(HEADS // NH, nq),
        out_shape=(jax.ShapeDtypeStruct((nq, HEADS * HEAD_PAD, T), F32),
                   jax.ShapeDtypeStruct((S, HEADS * HEAD_PAD), F32), jax.ShapeDtypeStruct((S, HEADS * V_HEAD), F32)),
        in_specs=[pl.BlockSpec((S, PW), lambda p, j: (0, p)),
                  pl.BlockSpec((T, PW), lambda p, j: (j, p)),
                  pl.BlockSpec((T, VW), lambda p, j: (j, p)),
                  pl.BlockSpec((1, PW, T), lambda p, j: (j, p, 0)),
                  pl.BlockSpec((S, VW), lambda p, j: (0, p)),
                  pl.BlockSpec((nq, NH * SUBLANES, T), lambda p, j: (0, p, 0)),
                  pl.BlockSpec((nq, NH * SUBLANES, T), lambda p, j: (0, p, 0))],
        out_specs=(pl.BlockSpec((nq, PW, T), lambda p, j: (0, p, 0)),
                   pl.BlockSpec((T, PW), lambda p, j: (j, p)),
                   pl.BlockSpec((T, VW), lambda p, j: (j, p))),
        scratch_shapes=[pltpu.VMEM((T, PW), F32), pltpu.VMEM((NH, T, LANES), F32)],
        compiler_params=_cparams(),
    )(q, k, v, kT3, do, lse3, dl3)


def _even_post(z, cv, o, name):
    S = z.shape[0]
    T = _row_tile(S)

    def body(ab_ref, ag_ref, bg_ref, cv_ref, o_ref, y_ref):
        y_ref[:, 0:SCW] = (ab_ref[...] * cv_ref[...] * _silu(ag_ref[...])).astype(BF16)
        y_ref[:, SCW:2 * SCW] = (o_ref[...] * _silu(bg_ref[...])).astype(BF16)

    return pl.pallas_call(
        body, name=name, grid=(S // T,),
        out_shape=jax.ShapeDtypeStruct((S, 2 * SCW), BF16),
        in_specs=[_rows(T, SCW, 0), _rows(T, SCW, 3), _rows(T, SCW, 4), _rows(T, SCW), _rows(T, SCW)],
        out_specs=_rows(T, 2 * SCW), compiler_params=_cparams(),
    )(z, z, z, cv, o)


def _even_bwd_gates(dyc, z, cv, o, name):
    S = z.shape[0]
    T = _row_tile(S)

    def body(dya_ref, dyb_ref, ab_ref, ag_ref, bg_ref, cv_ref, o_ref,
             dab_ref, dag_ref, dbg_ref, dcv_ref, do_ref, dl_ref):
        dya, ab, ag, cv = dya_ref[...], ab_ref[...], ag_ref[...], cv_ref[...]
        sg = _silu(ag)
        dab_ref[...] = (dya * cv * sg).astype(BF16)
        dcv_ref[...] = dya * ab * sg
        dag_ref[...] = (dya * ab * cv * _dsilu(ag)).astype(BF16)
        dyb, bg, ov = dyb_ref[...], bg_ref[...], o_ref[...]
        dov = dyb * _silu(bg)
        do_ref[...] = dov.astype(BF16)
        dbg_ref[...] = (dyb * ov * _dsilu(bg)).astype(BF16)
        prod = dov * ov
        left = lax.broadcasted_iota(jnp.int32, (T, LANES), 1) < V_HEAD
        for p in range(HEADS // 2):
            blk = prod[:, p * LANES:(p + 1) * LANES]
            s0 = jnp.sum(jnp.where(left, blk, 0.0), axis=1, keepdims=True)
            s1 = jnp.sum(jnp.where(left, 0.0, blk), axis=1, keepdims=True)
            dt = jnp.where(left, s0, s1).T
            dl_ref[0, 2 * p * SUBLANES:(2 * p + 1) * SUBLANES, :] = dt[0:SUBLANES, :]
            dl_ref[0, (2 * p + 1) * SUBLANES:(2 * p + 2) * SUBLANES, :] = dt[V_HEAD:V_HEAD + SUBLANES, :]

    assert T == _attn_tile(S)
    bf = jax.ShapeDtypeStruct((S, SCW), BF16)
    ff = jax.ShapeDtypeStruct((S, SCW), F32)
    return pl.pallas_call(
        body, name=name, grid=(S // T,),
        out_shape=(bf, bf, bf, ff, bf, jax.ShapeDtypeStruct((S // T, HEADS * SUBLANES, T), F32)),
        in_specs=[_rows(T, SCW, 0), _rows(T, SCW, 1), _rows(T, SCW, 0), _rows(T, SCW, 3), _rows(T, SCW, 4),
                  _rows(T, SCW), _rows(T, SCW)],
        out_specs=(_rows(T, SCW),) * 5 + (pl.BlockSpec((1, HEADS * SUBLANES, T), lambda i: (i, 0, 0)),),
        compiler_params=_cparams(),
    )(dyc, dyc, z, z, z, cv, o)


def _qkv_bwd(dq, dk, dv, z, tabs, w_q, w_kv, qg, kvg, name):
    S = dk.shape[0]
    T = _attn_tile(S)
    HW = HEADS * HEAD_PAD
    VW = HEADS * V_HEAD
    scale = 1.0 / math.sqrt(QK_NOPE + QK_ROPE)

    def fn(ins, outs):
        dq_ref, dk_ref, dv_ref, cq_ref, ckv_ref, ct_ref, ut_ref, dt_ref, wq_ref, wkv_ref, qg_ref, kvg_ref = ins
        dqp_ref, dkvp_ref, dcq_ref, dckv_ref, dkr_ref = outs
        ct, ut, dt = ct_ref[...], ut_ref[...], dt_ref[...]
        dkr = jnp.zeros((T, HEAD_PAD), F32)
        for h in range(HEADS):
            sl = slice(h * HEAD_PAD, (h + 1) * HEAD_PAD)
            dqp_ref[:, sl] = (_rope_t(dq_ref[0, sl, :].T, ct, ut, dt) * scale).astype(BF16)
            dkh = _rope_t(dk_ref[:, sl], ct, ut, dt)
            dkr = dkr + dkh
            dkvp_ref[:, sl] = dkh.astype(BF16)
        dkvp_ref[:, HW:] = dv_ref[...].astype(BF16)
        dkr_ref[...] = dkr.astype(BF16)
        sums = []
        for lat_ref, g_ref, dpre_ref, w_ref, dlat_ref in ((cq_ref, qg_ref, dqp_ref, wq_ref, dcq_ref),
                                                         (ckv_ref, kvg_ref, dkvp_ref, wkv_ref, dckv_ref)):
            dn = lax.dot_general(dpre_ref[...], w_ref[...], _NT, preferred_element_type=F32)
            xv = lat_ref[...]
            rstd = lax.rsqrt(jnp.mean(xv * xv, axis=-1, keepdims=True) + EPS)
            xh = xv * rstd
            dxh = dn * g_ref[...]
            dlat_ref[...] = (rstd * (dxh - xh * jnp.mean(dxh * xh, axis=-1, keepdims=True))).astype(BF16)
            sums.append(dn * xh)
        return sums

    return _col_sums(
        2, fn, [dq, dk, dv, z, z, *tabs, w_q, w_kv, qg, kvg],
        [pl.BlockSpec((1, HW, T), lambda i: (i, 0, 0)), _rows(T, HW), _rows(T, VW), _rows(T, QL, 10), _rows(T, KVL, 22),
         _rows(T, HEAD_PAD), _rows(T, HEAD_PAD), _rows(T, HEAD_PAD),
         _const(w_q.shape), _const(w_kv.shape), _const((1, QL)), _const((1, KVL))],
        [jax.ShapeDtypeStruct((S, HW), BF16), jax.ShapeDtypeStruct((S, HW + VW), BF16),
         jax.ShapeDtypeStruct((S, QL), BF16), jax.ShapeDtypeStruct((S, KVL), BF16),
         jax.ShapeDtypeStruct((S, HEAD_PAD), BF16)],
        [_rows(T, HW), _rows(T, HW + VW), _rows(T, QL), _rows(T, KVL), _rows(T, HEAD_PAD)],
        S, T, [QL, KVL], name)


def _even_dz(dab, du, z, dag, dbg, dcq, dckv, dkr, name):
    S = z.shape[0]
    T = _row_tile(S)

    def body(dab_ref, du_ref, ac_ref, ax_ref, dag_ref, dbg_ref, dcq_ref, dckv_ref, dkr_ref, dz_ref):
        duv = du_ref[...]
        dz_ref[:, 0:SCW] = dab_ref[...]
        dz_ref[:, SCW:2 * SCW] = (duv * ax_ref[...]).astype(BF16)
        dz_ref[:, 2 * SCW:3 * SCW] = (duv * ac_ref[...]).astype(BF16)
        dz_ref[:, 3 * SCW:4 * SCW] = dag_ref[...]
        dz_ref[:, 4 * SCW:5 * SCW] = dbg_ref[...]
        dz_ref[:, 5 * SCW:5 * SCW + QL] = dcq_ref[...]
        dz_ref[:, 5 * SCW + QL:5 * SCW + QL + KVL] = dckv_ref[...]
        dz_ref[:, 5 * SCW + QL + KVL:ZE] = dkr_ref[...]

    return pl.pallas_call(
        body, name=name, grid=(S // T,),
        out_shape=jax.ShapeDtypeStruct((S, ZE), BF16),
        in_specs=[_rows(T, SCW), _rows(T, SCW), _rows(T, SCW, 1), _rows(T, SCW, 2), _rows(T, SCW), _rows(T, SCW),
                  _rows(T, QL), _rows(T, KVL), _rows(T, HEAD_PAD)],
        out_specs=_rows(T, ZE), compiler_params=_cparams(),
    )(dab, du, z, z, dag, dbg, dcq, dckv, dkr)


def _odd_pre(z, name):
    S, D = z.shape[0], z.shape[1] // 3
    T = _row_tile(S)

    def body(val_ref, glu_ref, u_ref):
        u_ref[...] = val_ref[...] * _sigmoid(glu_ref[...])

    return pl.pallas_call(
        body, name=name, grid=(S // T,),
        out_shape=jax.ShapeDtypeStruct((S, D), F32),
        in_specs=[_rows(T, D, 0), _rows(T, D, 1)], out_specs=_rows(T, D),
        compiler_params=_cparams(),
    )(z, z)


def _layer_norm_stats(cv):
    mu = jnp.mean(cv, axis=-1, keepdims=True)
    cen = cv - mu
    rstd = lax.rsqrt(jnp.mean(cen * cen, axis=-1, keepdims=True) + EPS)
    return cen * rstd, rstd


def _odd_post(cv, z, ln_g, ln_b, name):
    S, D = cv.shape
    T = _row_tile(S)

    def body(cv_ref, sg_ref, g_ref, b_ref, y_ref):
        cvh, _ = _layer_norm_stats(cv_ref[...])
        y_ref[...] = (_silu(cvh * g_ref[...] + b_ref[...]) * _silu(sg_ref[...])).astype(BF16)

    return pl.pallas_call(
        body, name=name, grid=(S // T,),
        out_shape=jax.ShapeDtypeStruct((S, D), BF16),
        in_specs=[_rows(T, D), _rows(T, D, 2), _const((1, D)), _const((1, D))],
        out_specs=_rows(T, D), compiler_params=_cparams(),
    )(cv, z, ln_g, ln_b)


def _odd_bwd_norm(dyi, cv, z, ln_g, ln_b, name):
    S, D = cv.shape
    T = _row_tile(S)

    def fn(ins, outs):
        dy_ref, cv_ref, sg_ref, g_ref, b_ref = ins
        dcv_ref, dsg_ref = outs
        cvh, rstd = _layer_norm_stats(cv_ref[...])
        ln = cvh * g_ref[...] + b_ref[...]
        sgv, dy = sg_ref[...], dy_ref[...]
        dsg_ref[...] = (dy * _silu(ln) * _dsilu(sgv)).astype(BF16)
        dln = dy * _silu(sgv) * _dsilu(ln)
        dh = dln * g_ref[...]
        dcv_ref[...] = rstd * (dh - jnp.mean(dh, axis=-1, keepdims=True)
                               - cvh * jnp.mean(dh * cvh, axis=-1, keepdims=True))
        return [dln * cvh, dln]

    return _col_sums(2, fn, [dyi, cv, z, ln_g, ln_b],
                     [_rows(T, D), _rows(T, D), _rows(T, D, 2), _const((1, D)), _const((1, D))],
                     [jax.ShapeDtypeStruct((S, D), F32), jax.ShapeDtypeStruct((S, D), BF16)],
                     [_rows(T, D), _rows(T, D)], S, T, [D, D], name)


def _odd_dz(du, z, dsg, name):
    S, D = du.shape
    T = _row_tile(S)

    def body(du_ref, val_ref, glu_ref, dsg_ref, dz_ref):
        duv = du_ref[...]
        sig = _sigmoid(glu_ref[...])
        dz_ref[:, 0:D] = (duv * sig).astype(BF16)
        dz_ref[:, D:2 * D] = (duv * val_ref[...] * sig * (1.0 - sig)).astype(BF16)
        dz_ref[:, 2 * D:3 * D] = dsg_ref[...]

    return pl.pallas_call(
        body, name=name, grid=(S // T,),
        out_shape=jax.ShapeDtypeStruct((S, 3 * D), BF16),
        in_specs=[_rows(T, D), _rows(T, D, 0), _rows(T, D, 1), _rows(T, D)],
        out_specs=_rows(T, 3 * D), compiler_params=_cparams(),
    )(du, z, z, dsg)


ADAM_BLOCK_ELEMS = 128 * 1024


def _adam_tiles(R, C):
    if R * C <= ADAM_BLOCK_ELEMS:
        return R, C
    tr = R
    for cand in range(SUBLANES, R, SUBLANES):
        if R % cand == 0 and cand * C <= ADAM_BLOCK_ELEMS:
            tr = cand
    if tr < R:
        return tr, C
    tc = C
    for cand in range(LANES, C, LANES):
        if C % cand == 0 and R * cand <= ADAM_BLOCK_ELEMS:
            tc = cand
    return R, tc


def _adamw(g_parts, w, m, v, name):
    P, R, C = g_parts.shape
    tr, tc = _adam_tiles(R, C)

    def body(g_ref, w_ref, m_ref, v_ref, go_ref, d_ref, mo_ref, vo_ref):
        g = g_ref[0]
        for p in range(1, P):
            g = g + g_ref[p]
        mn = ADAM_B1 * m_ref[...] + (1.0 - ADAM_B1) * g
        vn = ADAM_B2 * v_ref[...] + (1.0 - ADAM_B2) * (g * g)
        m_hat = mn / (1.0 - ADAM_B1 ** ADAM_STEP)
        v_hat = vn / (1.0 - ADAM_B2 ** ADAM_STEP)
        go_ref[...] = g
        d_ref[...] = -ADAM_LR * (m_hat / (jnp.sqrt(v_hat) + ADAM_EPS) + ADAM_WD * w_ref[...])
        mo_ref[...] = mn
        vo_ref[...] = vn

    slab = jax.ShapeDtypeStruct((R, C), F32)
    blk = pl.BlockSpec((tr, tc), lambda i, j: (i, j))
    return pl.pallas_call(
        body, name=name, grid=(R // tr, C // tc),
        out_shape=(slab,) * 4,
        in_specs=[pl.BlockSpec((P, tr, tc), lambda i, j: (0, i, j)), blk, blk, blk],
        out_specs=(blk,) * 4, compiler_params=_cparams(),
    )(g_parts, w, m, v)


def _gather_cols(g, shape):
    nd = len(shape)
    t = jnp.moveaxis(g, 0, nd - 1)
    return t.reshape(tuple(shape[:-1]) + (N_DEV * shape[-1],))


def _scatter_cols(full, n):
    t = full.reshape(full.shape[:-1] + (N_DEV, n))
    return jnp.moveaxis(t, -2, 0)


def kernel(x, c, positions, ada_w, ada_b, pre_norm_g, post_norm_g, even_w_in, even_sc_conv_w, even_sc_conv_b, even_q_norm_g, even_kv_norm_g, even_w_uq, even_w_ukv, even_w_out, odd_w_in, odd_conv_w, odd_conv_b, odd_ln_g, odd_ln_b, odd_w_out, loss_target, m_ada_w, m_ada_b, m_pre_norm_g, m_post_norm_g, m_even_w_in, m_even_sc_conv_w, m_even_sc_conv_b, m_even_q_norm_g, m_even_kv_norm_g, m_even_w_uq, m_even_w_ukv, m_even_w_out, m_odd_w_in, m_odd_conv_w, m_odd_conv_b, m_odd_ln_g, m_odd_ln_b, m_odd_w_out, v_ada_w, v_ada_b, v_pre_norm_g, v_post_norm_g, v_even_w_in, v_even_sc_conv_w, v_even_sc_conv_b, v_even_q_norm_g, v_even_kv_norm_g, v_even_w_uq, v_even_w_ukv, v_even_w_out, v_odd_w_in, v_odd_conv_w, v_odd_conv_b, v_odd_ln_g, v_odd_ln_b, v_odd_w_out):
    S, D = x.shape[1], x.shape[2]
    L = ada_w.shape[0]
    NE, NO = even_w_in.shape[0], odd_w_in.shape[0]
    me = 4 * lax.axis_index("x") + 2 * lax.axis_index("y") + lax.axis_index("c")
    x0 = x[0]
    target = loss_target[0]

    small_parts = [c, even_sc_conv_w, odd_conv_w, odd_conv_b, odd_ln_g, odd_ln_b]
    small_shapes = [p.shape for p in small_parts]
    sg = _exchange([_pack(small_parts, F32, SUBLANES)], False, "gather_small")[0].reshape(N_DEV, -1)
    c_all, scw_g, ocw_g, ocb_g, olg_g, olb_g = _unpack(sg, small_shapes)
    c_all = c_all.reshape(N_DEV, D)
    sc_conv_w = _gather_cols(scw_g, even_sc_conv_w.shape)
    o_conv_w = _gather_cols(ocw_g, odd_conv_w.shape)
    o_conv_b = _gather_cols(ocb_g, odd_conv_b.shape)
    o_ln_g = _gather_cols(olg_g, odd_ln_g.shape)
    o_ln_b = _gather_cols(olb_g, odd_ln_b.shape)

    pad_q = HEAD_PAD - QK_NOPE - QK_ROPE
    w_local = [jnp.swapaxes(even_w_in, 1, 2).astype(BF16),
               jnp.pad(even_w_uq, ((0, 0), (0, 0), (0, pad_q))).astype(BF16),
               jnp.pad(even_w_ukv[..., :QK_NOPE], ((0, 0), (0, 0), (0, HEAD_PAD - QK_NOPE))).astype(BF16),
               even_w_ukv[..., QK_NOPE:].astype(BF16),
               even_w_out.astype(BF16), odd_w_in.astype(BF16), odd_w_out.astype(BF16)]
    ewt_g, eq_g, ek_g, ev_g, eout_g, owin_g, oout_g = _exchange(w_local, False, "gather_weights")
    heads_to_cols = lambda g: jnp.moveaxis(g, 0, 1).reshape(g.shape[1], -1)
    e_w_in_k, e_w_q_k, e_w_kv_k, e_w_out, o_w_in, o_w_out = [], [], [], [], [], []
    for i in range(NE):
        wt = ewt_g[:, i].reshape(-1, D)
        e_w_in_k.append(jnp.concatenate([wt[:2048], wt[2464:2976], wt[2048:2432], jnp.zeros((QK_NOPE, D), BF16),
                                         wt[2432:2464], jnp.zeros((pad_q, D), BF16)], axis=0))
        e_w_q_k.append(heads_to_cols(eq_g[:, i]))
        e_w_kv_k.append(jnp.concatenate([heads_to_cols(ek_g[:, i]), heads_to_cols(ev_g[:, i])], axis=-1))
        e_w_out.append(eout_g[:, i].reshape(-1, D))
    for i in range(NO):
        o_w_in.append(heads_to_cols(owin_g[:, i]))
        o_w_out.append(oout_g[:, i].reshape(-1, D))

    n_ada = ada_w.shape[2]
    ada_b_cols = lax.dynamic_slice_in_dim(ada_b, me * n_ada, n_ada, axis=1).reshape(L, 1, n_ada)
    mod_slab = _ada_fwd(c_all, ada_w, ada_b_cols)
    mod_g = _exchange([_pack([mod_slab], F32, SUBLANES)], False, "gather_mod")[0].reshape(N_DEV, -1)
    mod_all = mod_g[:, :L * N_DEV * n_ada].reshape(N_DEV, L, N_DEV, n_ada)
    mod = lax.dynamic_index_in_dim(mod_all, me, axis=2, keepdims=False)
    mod = jnp.moveaxis(mod, 0, 1).reshape(L, 3 * D)
    shift, scale, gate = mod[:, :D], mod[:, D:2 * D], mod[:, 2 * D:]

    half = QK_ROPE // 2
    inv_freq = 1.0 / (ROPE_THETA ** (jnp.arange(0, QK_ROPE, 2, dtype=F32) / QK_ROPE))
    inv_lane = jnp.zeros((HEAD_PAD,), F32).at[QK_NOPE:QK_NOPE + QK_ROPE].set(jnp.concatenate([inv_freq, inv_freq]))
    tabs = _rope_tables(positions.astype(F32).reshape(S, 1), inv_lane.reshape(1, HEAD_PAD))
    del half

    row = lambda a: a.reshape(1, -1)
    scb = even_sc_conv_b
    KP3, KP31 = SUBLANES, 32

    saved = []
    xs = x0
    for layer in range(L):
        i = layer // 2
        tag = f"l{layer}"
        h = _pre_norm(xs, row(pre_norm_g[layer]), row(scale[layer]), row(shift[layer]), f"pre_norm_{tag}")
        if layer % 2 == 0:
            z = _matmul(h, e_w_in_k[i], "nt", F32, f"w_in_{tag}", tn=1024)
            u, qn, kvn = _even_pre(z, row(even_q_norm_g[i]), row(even_kv_norm_g[i]), f"even_pre_{tag}")
            cw = jnp.pad(sc_conv_w[i], ((0, KP3 - SC_KERNEL), (0, 0)))
            cv = _conv_fwd(u, cw, row(scb[i]), SC_KERNEL, f"conv_{tag}")
            q, k, v, kT3, vT3 = _qkv_fwd_t(qn, kvn, z, tabs, e_w_q_k[i], e_w_kv_k[i], f"qkv_{tag}")
            o, lse = _attn_fwd_t(q, k, vT3, f"attn_{tag}")
            ycat = _even_post(z, cv, o, f"even_post_{tag}")
            y = _matmul(ycat, e_w_out[i], "nn", F32, f"w_out_{tag}", tn=1024)
            saved.append(dict(x=xs, h=h, z=z, u=u, qn=qn, kvn=kvn, cw=cw, cv=cv, q=q, k=k, v=v, kT3=kT3, o=o, lse=lse,
                              ycat=ycat, y=y))
        else:
            z = _matmul(h, o_w_in[i], "nn", F32, f"w_in_{tag}", tn=1024)
            u = _odd_pre(z, f"odd_pre_{tag}")
            cw = jnp.pad(o_conv_w[i], ((0, KP31 - CONF_KERNEL), (0, 0)))
            cv = _conv_fwd(u, cw, row(o_conv_b[i]), CONF_KERNEL, f"conv_{tag}")
            yin = _odd_post(cv, z, row(o_ln_g[i]), row(o_ln_b[i]), f"odd_post_{tag}")
            y = _matmul(yin, o_w_out[i], "nn", F32, f"w_out_{tag}", tn=1024)
            saved.append(dict(x=xs, h=h, z=z, u=u, cw=cw, cv=cv, yin=yin, y=y))
        xs = _post_norm(xs, y, row(post_norm_g[layer]), row(gate[layer]), f"post_norm_{tag}")

    loss_row, dx = _loss_head(xs, target)
    loss = lax.psum(loss_row[0, 0], MESH_AXES)

    g_pre, g_post, dmod = [None] * L, [None] * L, [None] * L
    g_e_w_in, g_e_w_uq, g_e_w_ukv, g_e_w_out = [None] * NE, [None] * NE, [None] * NE, [None] * NE
    g_scw, g_scb, g_qg, g_kvg = [None] * NE, [None] * NE, [None] * NE, [None] * NE
    g_o_w_in, g_o_w_out, g_ocw, g_ocb, g_olg, g_olb = ([None] * NO for _ in range(6))
    for layer in reversed(range(L)):
        i = layer // 2
        tag = f"l{layer}"
        sv = saved[layer]
        dy, dgate, g_post[layer] = _post_norm_bwd(dx, sv["y"], row(post_norm_g[layer]), row(gate[layer]),
                                                  f"post_norm_bwd_{tag}")
        if layer % 2 == 0:
            dyc = _matmul(dy, e_w_out[i], "nt", F32, f"d_ycat_{tag}", tn=1024)
            g_e_w_out[i] = _matmul(sv["ycat"], dy, "tn", F32, f"g_w_out_{tag}", tk=512)
            dab, dag, dbg, dcv, do, delta = _even_bwd_gates(dyc, sv["z"], sv["cv"], sv["o"], f"even_gates_bwd_{tag}")
            du, dcw, g_scb[i] = _conv_bwd(dcv, sv["u"], sv["cw"], SC_KERNEL, f"conv_bwd_{tag}")
            g_scw[i] = dcw[:SC_KERNEL]
            dq, dk, dv = _attn_bwd_t(sv["q"], sv["k"], sv["v"], sv["kT3"], do, sv["lse"], delta, f"attn_bwd_{tag}")
            (dqp, dkvp, dcq, dckv, dkr, g_qg[i], g_kvg[i]) = _qkv_bwd(
                dq, dk, dv, sv["z"], tabs, e_w_q_k[i], e_w_kv_k[i],
                row(even_q_norm_g[i]), row(even_kv_norm_g[i]), f"qkv_bwd_{tag}")
            gq = _matmul(sv["qn"], dqp, "tn", F32, f"g_w_uq_{tag}", tk=512)
            gkv = _matmul(sv["kvn"], dkvp, "tn", F32, f"g_w_ukv_{tag}", tk=512)
            g_e_w_uq[i] = jnp.moveaxis(gq.reshape(QL, HEADS, HEAD_PAD)[..., :QK_NOPE + QK_ROPE], 1, 0)
            g_e_w_ukv[i] = jnp.moveaxis(jnp.concatenate(
                [gkv[:, :HEADS * HEAD_PAD].reshape(KVL, HEADS, HEAD_PAD)[..., :QK_NOPE],
                 gkv[:, HEADS * HEAD_PAD:].reshape(KVL, HEADS, V_HEAD)], axis=-1), 1, 0)
            g_e_w_out[i] = g_e_w_out[i].reshape(N_DEV, -1, D)
            dz = _even_dz(dab, du, sv["z"], dag, dbg, dcq, dckv, dkr, f"even_dz_{tag}")
            gt = _matmul(dz, sv["h"], "tn", F32, f"g_w_in_{tag}", tk=512, tn=1024)
            g_e_w_in[i] = jnp.concatenate([gt[:2048], gt[2560:2944], gt[2944 + QK_NOPE:2944 + QK_NOPE + QK_ROPE],
                                           gt[2048:2560]], axis=0).reshape(N_DEV, -1, D)
            dh = _matmul(dz, e_w_in_k[i], "nn", F32, f"d_h_{tag}", tn=1024)
        else:
            dyi = _matmul(dy, o_w_out[i], "nt", F32, f"d_yin_{tag}", tn=1024)
            g_o_w_out[i] = _matmul(sv["yin"], dy, "tn", F32, f"g_w_out_{tag}", tk=512).reshape(N_DEV, -1, D)
            dcv, dsg, g_olg[i], g_olb[i] = _odd_bwd_norm(dyi, sv["cv"], sv["z"], row(o_ln_g[i]), row(o_ln_b[i]),
                                                         f"odd_norm_bwd_{tag}")
            du, dcw, g_ocb[i] = _conv_bwd(dcv, sv["u"], sv["cw"], CONF_KERNEL, f"conv_bwd_{tag}")
            g_ocw[i] = dcw[:CONF_KERNEL]
            dz = _odd_dz(du, sv["z"], dsg, f"odd_dz_{tag}")
            g_o_w_in[i] = jnp.moveaxis(_matmul(sv["h"], dz, "tn", F32, f"g_w_in_{tag}", tk=512, tn=1024)
                                       .reshape(D, N_DEV, -1), 1, 0)
            dh = _matmul(dz, o_w_in[i], "nt", F32, f"d_h_{tag}", tn=1024)
        dx, dshift, dscale, g_pre[layer] = _pre_norm_bwd(dh, sv["x"], dx, row(pre_norm_g[layer]), row(scale[layer]),
                                                         f"pre_norm_bwd_{tag}")
        dmod[layer] = jnp.concatenate([dshift, dscale, dgate], axis=-1)
    grad_x = dx.reshape(1, S, D)

    rep_g = [jnp.concatenate(dmod, 0), jnp.concatenate(g_pre, 0), jnp.concatenate(g_post, 0),
             jnp.stack(g_scb), jnp.stack(g_qg), jnp.stack(g_kvg)]
    rep_w = [ada_b, pre_norm_g, post_norm_g, even_sc_conv_b, even_q_norm_g, even_kv_norm_g]
    rep_m = [m_ada_b, m_pre_norm_g, m_post_norm_g, m_even_sc_conv_b, m_even_q_norm_g, m_even_kv_norm_g]
    rep_v = [v_ada_b, v_pre_norm_g, v_post_norm_g, v_even_sc_conv_b, v_even_q_norm_g, v_even_kv_norm_g]
    rep_shapes = [w.shape for w in rep_w]
    rep_all = _exchange([_pack(rep_g, F32, SUBLANES)], False, "gather_small_grads")[0]
    rep_out = _adamw(rep_all, _pack(rep_w, F32, SUBLANES), _pack(rep_m, F32, SUBLANES), _pack(rep_v, F32, SUBLANES),
                     "adamw_replicated")
    rep_res = [_unpack(o.reshape(-1), rep_shapes) for o in rep_out]

    dmod_all = rep_all.reshape(N_DEV, -1)[:, :L * 3 * D].reshape(N_DEV, L, 3 * D)
    dmod_cols = jnp.moveaxis(lax.dynamic_slice_in_dim(dmod_all, me * n_ada, n_ada, axis=2), 0, 1)
    g_ada_w = _ada_bwd(c_all.T, dmod_cols)
    ada_out = _adamw(g_ada_w.reshape(1, -1, PACK_COLS), ada_w.reshape(-1, PACK_COLS),
                     m_ada_w.reshape(-1, PACK_COLS), v_ada_w.reshape(-1, PACK_COLS), "adamw_ada_w")
    ada_res = [o.reshape(ada_w.shape) for o in ada_out]

    sm_full = [_scatter_cols(jnp.stack(g_scw), even_sc_conv_w.shape[-1]),
               _scatter_cols(jnp.stack(g_ocw), odd_conv_w.shape[-1]),
               _scatter_cols(jnp.concatenate(g_ocb, 0), odd_conv_b.shape[-1]),
               _scatter_cols(jnp.concatenate(g_olg, 0), odd_ln_g.shape[-1]),
               _scatter_cols(jnp.concatenate(g_olb, 0), odd_ln_b.shape[-1])]
    sm_w = [even_sc_conv_w, odd_conv_w, odd_conv_b, odd_ln_g, odd_ln_b]
    sm_m = [m_even_sc_conv_w, m_odd_conv_w, m_odd_conv_b, m_odd_ln_g, m_odd_ln_b]
    sm_v = [v_even_sc_conv_w, v_odd_conv_w, v_odd_conv_b, v_odd_ln_g, v_odd_ln_b]
    sm_shapes = [w.shape for w in sm_w]
    sm_flat = jnp.concatenate([g.reshape(N_DEV, -1) for g in sm_full], axis=1)
    sm_rows = _pack(sm_w, F32, SUBLANES).shape[0]
    sm_slab = jnp.pad(sm_flat, ((0, 0), (0, sm_rows * PACK_COLS - sm_flat.shape[1]))).reshape(N_DEV, sm_rows, PACK_COLS)

    big = ([("even_w_in", i, g_e_w_in[i]) for i in range(NE)] + [("even_w_uq", i, g_e_w_uq[i]) for i in range(NE)]
           + [("even_w_ukv", i, g_e_w_ukv[i]) for i in range(NE)] + [("even_w_out", i, g_e_w_out[i]) for i in range(NE)]
           + [("odd_w_in", i, g_o_w_in[i]) for i in range(NO)] + [("odd_w_out", i, g_o_w_out[i]) for i in range(NO)])
    recv = _exchange([g for _, _, g in big] + [sm_slab], True, "scatter_grads")
    sm_out = _adamw(recv[-1], _pack(sm_w, F32, SUBLANES), _pack(sm_m, F32, SUBLANES), _pack(sm_v, F32, SUBLANES),
                    "adamw_small_sharded")
    sm_res = [_unpack(o.reshape(-1), sm_shapes) for o in sm_out]
    state = {"even_w_in": (even_w_in, m_even_w_in, v_even_w_in), "even_w_uq": (even_w_uq, m_even_w_uq, v_even_w_uq),
             "even_w_ukv": (even_w_ukv, m_even_w_ukv, v_even_w_ukv), "even_w_out": (even_w_out, m_even_w_out, v_even_w_out),
             "odd_w_in": (odd_w_in, m_odd_w_in, v_odd_w_in), "odd_w_out": (odd_w_out, m_odd_w_out, v_odd_w_out)}
    big_res = {name: [[None] * len(state[name][0]) for _ in range(4)] for name in state}
    for (name, i, _), parts in zip(big, recv[:-1]):
        transposed = name == "even_w_in"
        wmv = [t[i].T if transposed else t[i] for t in state[name]]
        res = _adamw(parts, *wmv, f"adamw_{name}_{i}")
        for kind in range(4):
            big_res[name][kind][i] = res[kind].T if transposed else res[kind]
    sh_res = [dict(zip(["even_sc_conv_w", "odd_conv_w", "odd_conv_b", "odd_ln_g", "odd_ln_b"], sm_res[kind]))
              for kind in range(4)]
    for name in state:
        for kind in range(4):
            sh_res[kind][name] = jnp.stack(big_res[name][kind])

    order = ["ada_w", "ada_b", "pre_norm_g", "post_norm_g", "even_w_in", "even_sc_conv_w", "even_sc_conv_b",
             "even_q_norm_g", "even_kv_norm_g", "even_w_uq", "even_w_ukv", "even_w_out", "odd_w_in", "odd_conv_w",
             "odd_conv_b", "odd_ln_g", "odd_ln_b", "odd_w_out"]
    rep_names = ["ada_b", "pre_norm_g", "post_norm_g", "even_sc_conv_b", "even_q_norm_g", "even_kv_norm_g"]
    outs = [loss, grad_x]
    for kind in range(4):
        for name in order:
            if name == "ada_w":
                outs.append(ada_res[kind])
            elif name in rep_names:
                outs.append(rep_res[kind][rep_names.index(name)])
            else:
                outs.append(sh_res[kind][name])
    return tuple(outs)
```

```python
import functools
import math

import jax
import jax.numpy as jnp
from jax import lax
from jax.experimental import pallas as pl
from jax.experimental.pallas import tpu as pltpu

F32 = jnp.float32
BF16 = jnp.bfloat16
MESH_AXES = ("x", "y", "c")
N_DEV = 8
EPS = 1e-6
CHUNK = 64
HEADS = 8
QK_NOPE = 64
QK_ROPE = 32
V_HEAD = 64
HEAD_PAD = 128
ROPE_THETA = 10000.0
SC_KERNEL = 3
CONF_KERNEL = 31
LANES = 128
SUBLANES = 8
PACK_COLS = 1024
VMEM_LIMIT = 48 * 1024 * 1024
NEG = -1e30

ADAM_LR = 0.001
ADAM_B1 = 0.9
ADAM_B2 = 0.999
ADAM_EPS = 1e-08
ADAM_WD = 0.01
ADAM_STEP = 10


def _cparams():
    return pltpu.CompilerParams(vmem_limit_bytes=VMEM_LIMIT)


def _sigmoid(x):
    return 1.0 / (1.0 + jnp.exp(-x))


def _silu(x):
    return x * _sigmoid(x)


def _dsilu(x):
    s = _sigmoid(x)
    return s * (1.0 + x * (1.0 - s))


def _rows(T, width, cb=0):
    return pl.BlockSpec((T, width), lambda i: (i, cb))


def _const(shape):
    nd = len(shape)
    return pl.BlockSpec(shape, lambda i: (0,) * nd)


def _row_tile(S):
    return min(256, S)


def _exchange(srcs, scatter, name):
    n = len(srcs)
    shapes = [tuple(s.shape[1:]) if scatter else tuple(s.shape) for s in srcs]

    def body(*refs):
        src_refs, out_refs = refs[:n], refs[n:2 * n]
        send_sems, recv_sems, local_sems = refs[2 * n:]
        x, y, c = lax.axis_index("x"), lax.axis_index("y"), lax.axis_index("c")
        me = 4 * x + 2 * y + c
        owns, copies = [], []
        for a in range(n):
            def piece(d, a=a):
                return src_refs[a].at[d] if scatter else src_refs[a]

            own = pltpu.make_async_copy(piece(me), out_refs[a].at[me], local_sems.at[a])
            own.start()
            owns.append(own)
            for k in range(1, N_DEV):
                px, py, pc = x ^ ((k >> 2) & 1), y ^ ((k >> 1) & 1), c ^ (k & 1)
                peer = 4 * px + 2 * py + pc
                sem = a * (N_DEV - 1) + k - 1
                cp = pltpu.make_async_remote_copy(
                    src_ref=piece(peer), dst_ref=out_refs[a].at[me],
                    send_sem=send_sems.at[sem], recv_sem=recv_sems.at[sem],
                    device_id=(px, py, pc), device_id_type=pl.DeviceIdType.MESH)
                cp.start()
                arrival = pltpu.make_async_remote_copy(
                    src_ref=piece(peer), dst_ref=out_refs[a].at[peer],
                    send_sem=send_sems.at[sem], recv_sem=recv_sems.at[sem],
                    device_id=(x, y, c), device_id_type=pl.DeviceIdType.MESH)
                copies.append((cp, arrival))
        for _, arrival in copies:
            arrival.wait_recv()
        for cp, _ in copies:
            cp.wait_send()
        for own in owns:
            own.wait()

    return pl.pallas_call(
        body, name=name,
        out_shape=tuple(jax.ShapeDtypeStruct((N_DEV,) + shp, s.dtype) for shp, s in zip(shapes, srcs)),
        in_specs=[pl.BlockSpec(memory_space=pl.ANY)] * n,
        out_specs=tuple(pl.BlockSpec(memory_space=pl.ANY) for _ in range(n)),
        scratch_shapes=[pltpu.SemaphoreType.DMA((n * (N_DEV - 1),)),
                        pltpu.SemaphoreType.DMA((n * (N_DEV - 1),)),
                        pltpu.SemaphoreType.DMA((n,))],
    )(*srcs)


_HBM = pl.BlockSpec(memory_space=pltpu.HBM)
_SEM = pl.BlockSpec(memory_space=pltpu.SEMAPHORE)


def _peer(k):
    x, y, c = lax.axis_index("x"), lax.axis_index("y"), lax.axis_index("c")
    return x ^ ((k >> 2) & 1), y ^ ((k >> 1) & 1), c ^ (k & 1)


def _exchange_start(srcs, scatter, name):
    n = len(srcs)
    shapes = [tuple(s.shape[1:]) if scatter else tuple(s.shape) for s in srcs]
    slots = N_DEV - 1 if scatter else N_DEV
    lands = [lax.empty((slots,) + shp, s.dtype) for shp, s in zip(shapes, srcs)]

    def body(*refs):
        src_refs, land_refs = refs[:n], refs[n:2 * n]
        send_sems, recv_sems = refs[2 * n], refs[2 * n + 1]
        token = refs[4 * n + 2]
        me = 4 * lax.axis_index("x") + 2 * lax.axis_index("y") + lax.axis_index("c")
        for a in range(n):
            if not scatter:
                own = pltpu.make_async_copy(src_refs[a], land_refs[a].at[me], refs[4 * n + 3].at[a])
                own.start()
                own.wait()
            for k in range(1, N_DEV):
                px, py, pc = _peer(k)
                peer = 4 * px + 2 * py + pc
                pltpu.make_async_remote_copy(
                    src_ref=src_refs[a].at[peer] if scatter else src_refs[a],
                    dst_ref=land_refs[a].at[k - 1] if scatter else land_refs[a].at[me],
                    send_sem=send_sems.at[a * (N_DEV - 1) + k - 1], recv_sem=recv_sems.at[a * (N_DEV - 1) + k - 1],
                    device_id=(px, py, pc), device_id_type=pl.DeviceIdType.MESH).start()
        token[...] = jnp.zeros_like(token)

    hbm = lambda arrs: [pltpu.HBM(a.shape, a.dtype) for a in arrs]
    out = pl.pallas_call(
        body, name=name,
        out_shape=(pltpu.SemaphoreType.DMA((n * (N_DEV - 1),)), pltpu.SemaphoreType.DMA((n * (N_DEV - 1),)),
                   *hbm(srcs), *hbm(lands), jax.ShapeDtypeStruct((SUBLANES, LANES), F32)),
        in_specs=[_HBM] * (2 * n),
        out_specs=(_SEM, _SEM, *([_HBM] * (2 * n)), pl.BlockSpec(memory_space=pltpu.VMEM)),
        input_output_aliases={a: 2 + a for a in range(2 * n)},
        scratch_shapes=[] if scatter else [pltpu.SemaphoreType.DMA((n,))],
        compiler_params=pltpu.CompilerParams(has_side_effects=pltpu.SideEffectType.DATAFLOW_SIDE_EFFECTING),
    )(*[pltpu.with_memory_space_constraint(s, pltpu.HBM) for s in srcs],
      *[pltpu.with_memory_space_constraint(l, pltpu.HBM) for l in lands])
    return (out[0], out[1], list(out[2:2 + n]), list(out[2 + n:2 + 2 * n])), out[2 + 2 * n]


def _exchange_wait(handle, scatter, after, name):
    send_sems, recv_sems, srcs, lands = handle
    n = len(srcs)

    def body(*refs):
        src_refs, land_refs = refs[:n], refs[n:2 * n]
        send_sems, recv_sems = refs[2 * n], refs[2 * n + 1]
        for a in range(n):
            for k in range(1, N_DEV):
                px, py, pc = _peer(k)
                peer = 4 * px + 2 * py + pc
                cp = pltpu.make_async_remote_copy(
                    src_ref=src_refs[a].at[peer] if scatter else src_refs[a],
                    dst_ref=land_refs[a].at[k - 1] if scatter else land_refs[a].at[peer],
                    send_sem=send_sems.at[a * (N_DEV - 1) + k - 1], recv_sem=recv_sems.at[a * (N_DEV - 1) + k - 1],
                    device_id=(px, py, pc), device_id_type=pl.DeviceIdType.MESH)
                cp.wait_send()
                cp.wait_recv()

    out = pl.pallas_call(
        body, name=name,
        out_shape=tuple(pltpu.HBM(a.shape, a.dtype) for a in srcs + lands),
        in_specs=[_HBM] * (2 * n) + [_SEM, _SEM, pl.BlockSpec(memory_space=pl.ANY)],
        out_specs=tuple([_HBM] * (2 * n)),
        input_output_aliases={a: a for a in range(2 * n)},
        compiler_params=pltpu.CompilerParams(has_side_effects=pltpu.SideEffectType.DATAFLOW_SIDE_EFFECTING),
    )(*srcs, *lands, send_sems, recv_sems, after)
    return list(out[:n]), list(out[n:])


def _pack(parts, dtype, row_mult):
    flat = jnp.concatenate([p.reshape(-1).astype(dtype) for p in parts])
    n = flat.shape[0]
    rows = -(-n // PACK_COLS)
    rows = -(-rows // row_mult) * row_mult
    flat = jnp.pad(flat, (0, rows * PACK_COLS - n))
    return flat.reshape(rows, PACK_COLS)


def _unpack(flat, shapes):
    out, off = [], 0
    for shp in shapes:
        n = math.prod(shp)
        out.append(flat[..., off:off + n].reshape(flat.shape[:-1] + tuple(shp)))
        off += n
    return out


_DIMS = {"nn": (((1,), (0,)), ((), ())), "nt": (((1,), (1,)), ((), ())), "tn": (((0,), (0,)), ((), ()))}


def _matmul(a, b, mode, out_dtype, name, tm=512, tn=512, tk=None):
    if mode == "nn":
        (M, K), (_, N) = a.shape, b.shape
    elif mode == "nt":
        (M, K), (N, _) = a.shape, b.shape
    else:
        (K, M), (_, N) = a.shape, b.shape
    tm, tn = min(tm, M), min(tn, N)
    tk = K if tk is None else min(tk, K)
    nk = K // tk
    assert M % tm == 0 and N % tn == 0 and K % tk == 0, (name, a.shape, b.shape)

    def body(a_ref, b_ref, o_ref, *scratch):
        p = lax.dot_general(a_ref[...].astype(BF16), b_ref[...].astype(BF16), _DIMS[mode],
                            preferred_element_type=F32)
        if nk == 1:
            o_ref[...] = p.astype(out_dtype)
        else:
            acc = scratch[0]
            k = pl.program_id(2)

            @pl.when(k == 0)
            def _():
                acc[...] = p

            @pl.when(k > 0)
            def _():
                acc[...] += p

            @pl.when(k == nk - 1)
            def _():
                o_ref[...] = acc[...].astype(out_dtype)

    a_spec = (pl.BlockSpec((tk, tm), lambda i, j, k: (k, i)) if mode == "tn"
              else pl.BlockSpec((tm, tk), lambda i, j, k: (i, k)))
    b_spec = (pl.BlockSpec((tn, tk), lambda i, j, k: (j, k)) if mode == "nt"
              else pl.BlockSpec((tk, tn), lambda i, j, k: (k, j)))
    return pl.pallas_call(
        body, name=name, grid=(M // tm, N // tn, nk),
        out_shape=jax.ShapeDtypeStruct((M, N), out_dtype),
        in_specs=[a_spec, b_spec],
        out_specs=pl.BlockSpec((tm, tn), lambda i, j, k: (i, j)),
        scratch_shapes=[pltpu.VMEM((tm, tn), F32)] if nk > 1 else [],
        compiler_params=_cparams(),
    )(a, b)


def _ada_fwd(c_all, ada_w, ada_b_cols):
    L, D, n = ada_w.shape

    def body(c_ref, w_ref, b_ref, o_ref):
        act = _silu(c_ref[...]).astype(BF16)
        o_ref[0] = jnp.dot(act, w_ref[0].astype(BF16), preferred_element_type=F32) + b_ref[0]

    return pl.pallas_call(
        body, name="ada_fwd", grid=(L,),
        out_shape=jax.ShapeDtypeStruct((L, N_DEV, n), F32),
        in_specs=[pl.BlockSpec((N_DEV, D), lambda l: (0, 0)),
                  pl.BlockSpec((1, D, n), lambda l: (l, 0, 0)),
                  pl.BlockSpec((1, 1, n), lambda l: (l, 0, 0))],
        out_specs=pl.BlockSpec((1, N_DEV, n), lambda l: (l, 0, 0)),
        compiler_params=_cparams(),
    )(c_all, ada_w, ada_b_cols)


def _ada_bwd(c_all_t, dmod_cols):
    D = c_all_t.shape[0]
    L, _, n = dmod_cols.shape

    def body(c_ref, d_ref, o_ref):
        act = _silu(c_ref[...])
        dm = d_ref[0]
        acc = act[:, 0:1] * dm[0:1, :]
        for b in range(1, N_DEV):
            acc = acc + act[:, b:b + 1] * dm[b:b + 1, :]
        o_ref[0] = acc

    return pl.pallas_call(
        body, name="ada_bwd", grid=(L,),
        out_shape=jax.ShapeDtypeStruct((L, D, n), F32),
        in_specs=[pl.BlockSpec((D, N_DEV), lambda l: (0, 0)),
                  pl.BlockSpec((1, N_DEV, n), lambda l: (l, 0, 0))],
        out_specs=pl.BlockSpec((1, D, n), lambda l: (l, 0, 0)),
        compiler_params=_cparams(),
    )(c_all_t, dmod_cols)


def _rope_tables(pos_col, inv_lane):
    S = pos_col.shape[0]
    T = _row_tile(S)
    half = QK_ROPE // 2

    def body(p_ref, f_ref, c_ref, up_ref, dn_ref):
        ang = p_ref[...] * f_ref[...]
        lane = lax.broadcasted_iota(jnp.int32, ang.shape, 1)
        first = (lane >= QK_NOPE) & (lane < QK_NOPE + half)
        second = (lane >= QK_NOPE + half) & (lane < QK_NOPE + QK_ROPE)
        cs, sn = jnp.cos(ang), jnp.sin(ang)
        c_ref[...] = jnp.where(first | second, cs, 1.0)
        up_ref[...] = jnp.where(first, -sn, 0.0)
        dn_ref[...] = jnp.where(second, sn, 0.0)

    tab = jax.ShapeDtypeStruct((S, HEAD_PAD), F32)
    return pl.pallas_call(
        body, name="rope_tables", grid=(S // T,),
        out_shape=(tab, tab, tab),
        in_specs=[_rows(T, 1), _const((1, HEAD_PAD))],
        out_specs=(_rows(T, HEAD_PAD),) * 3,
        compiler_params=_cparams(),
    )(pos_col, inv_lane)


def _rope(blk, ct, ut, dt):
    half = QK_ROPE // 2
    up = pltpu.roll(blk, HEAD_PAD - half, 1)
    dn = pltpu.roll(blk, half, 1)
    return blk * ct + up * ut + dn * dt


def _rope_t(d, ct, ut, dt):
    half = QK_ROPE // 2
    return d * ct + pltpu.roll(d * ut, half, 1) + pltpu.roll(d * dt, HEAD_PAD - half, 1)


def _pre_norm(x, g, scale, shift, name):
    S, D = x.shape
    T = _row_tile(S)

    def body(x_ref, g_ref, sc_ref, sh_ref, h_ref):
        xv = x_ref[...]
        rstd = lax.rsqrt(jnp.mean(xv * xv, axis=-1, keepdims=True) + EPS)
        h_ref[...] = ((xv * rstd) * g_ref[...] * (1.0 + sc_ref[...]) + sh_ref[...]).astype(BF16)

    return pl.pallas_call(
        body, name=name, grid=(S // T,),
        out_shape=jax.ShapeDtypeStruct((S, D), BF16),
        in_specs=[_rows(T, D), _const((1, D)), _const((1, D)), _const((1, D))],
        out_specs=_rows(T, D), compiler_params=_cparams(),
    )(x, g, scale, shift)


def _post_norm(x, y, g, gate, name):
    S, D = x.shape
    T = _row_tile(S)

    def body(x_ref, y_ref, g_ref, gt_ref, o_ref):
        yv = y_ref[...]
        rstd = lax.rsqrt(jnp.mean(yv * yv, axis=-1, keepdims=True) + EPS)
        o_ref[...] = x_ref[...] + gt_ref[...] * ((yv * rstd) * g_ref[...])

    return pl.pallas_call(
        body, name=name, grid=(S // T,),
        out_shape=jax.ShapeDtypeStruct((S, D), F32),
        in_specs=[_rows(T, D), _rows(T, D), _const((1, D)), _const((1, D))],
        out_specs=_rows(T, D), compiler_params=_cparams(),
    )(x, y, g, gate)


def _fold8(v):
    T, C = v.shape
    return v.reshape(T // SUBLANES, SUBLANES, C).sum(axis=0)


def _col_sums(n_sums, body_fn, ins, in_specs, outs, out_specs, S, T, widths, name):
    n_in, n_out = len(ins), len(outs)
    nt = S // T

    def body(*refs):
        in_refs = refs[:n_in]
        out_refs = refs[n_in:n_in + n_out]
        sum_refs = refs[n_in + n_out:n_in + n_out + n_sums]
        accs = refs[n_in + n_out + n_sums:]
        i = pl.program_id(0)
        terms = body_fn(in_refs, out_refs)

        @pl.when(i == 0)
        def _():
            for acc, t in zip(accs, terms):
                acc[...] = _fold8(t)

        @pl.when(i > 0)
        def _():
            for acc, t in zip(accs, terms):
                acc[...] += _fold8(t)

        @pl.when(i == nt - 1)
        def _():
            for acc, s_ref in zip(accs, sum_refs):
                s_ref[...] = jnp.sum(acc[...], axis=0, keepdims=True)

    return pl.pallas_call(
        body, name=name, grid=(nt,),
        out_shape=tuple(outs) + tuple(jax.ShapeDtypeStruct((1, w), F32) for w in widths),
        in_specs=in_specs,
        out_specs=tuple(out_specs) + tuple(_const((1, w)) for w in widths),
        scratch_shapes=[pltpu.VMEM((SUBLANES, w), F32) for w in widths],
        compiler_params=_cparams(),
    )(*ins)


def _post_norm_bwd(dxo, y, g, gate, name):
    S, D = y.shape
    T = _row_tile(S)

    def fn(ins, outs):
        dxo_ref, y_ref, g_ref, gt_ref = ins
        yv, dv = y_ref[...], dxo_ref[...]
        rstd = lax.rsqrt(jnp.mean(yv * yv, axis=-1, keepdims=True) + EPS)
        yh = yv * rstd
        dn = dv * gt_ref[...]
        dyh = dn * g_ref[...]
        outs[0][...] = (rstd * (dyh - yh * jnp.mean(dyh * yh, axis=-1, keepdims=True))).astype(BF16)
        return [dv * (yh * g_ref[...]), dn * yh]

    return _col_sums(2, fn, [dxo, y, g, gate],
                     [_rows(T, D), _rows(T, D), _const((1, D)), _const((1, D))],
                     [jax.ShapeDtypeStruct((S, D), BF16)], [_rows(T, D)], S, T, [D, D], name)


def _pre_norm_bwd(dh, x, dxo, g, scale, name):
    S, D = x.shape
    T = _row_tile(S)

    def fn(ins, outs):
        dh_ref, x_ref, dxo_ref, g_ref, sc_ref = ins
        xv, dv = x_ref[...], dh_ref[...]
        rstd = lax.rsqrt(jnp.mean(xv * xv, axis=-1, keepdims=True) + EPS)
        xh = xv * rstd
        dr = dv * (1.0 + sc_ref[...])
        dxh = dr * g_ref[...]
        outs[0][...] = dxo_ref[...] + rstd * (dxh - xh * jnp.mean(dxh * xh, axis=-1, keepdims=True))
        return [dv, dv * (xh * g_ref[...]), dr * xh]

    return _col_sums(3, fn, [dh, x, dxo, g, scale],
                     [_rows(T, D), _rows(T, D), _rows(T, D), _const((1, D)), _const((1, D))],
                     [jax.ShapeDtypeStruct((S, D), F32)], [_rows(T, D)], S, T, [D, D, D], name)


def _loss_head(x, target):
    S, D = x.shape
    T = _row_tile(S)
    nt = S // T

    def body(x_ref, t_ref, l_ref, dx_ref, acc):
        i = pl.program_id(0)
        e = x_ref[...] - t_ref[...]
        dx_ref[...] = e * (1.0 / D)
        part = _fold8(e * e)

        @pl.when(i == 0)
        def _():
            acc[...] = part

        @pl.when(i > 0)
        def _():
            acc[...] += part

        @pl.when(i == nt - 1)
        def _():
            tot = jnp.sum(jnp.sum(acc[...], axis=0, keepdims=True), axis=1, keepdims=True)
            l_ref[...] = jnp.broadcast_to(tot * (0.5 / D), (1, LANES))

    return pl.pallas_call(
        body, name="loss_head", grid=(nt,),
        out_shape=(jax.ShapeDtypeStruct((1, LANES), F32), jax.ShapeDtypeStruct((S, D), F32)),
        in_specs=[_rows(T, D), _rows(T, D)],
        out_specs=(_const((1, LANES)), _rows(T, D)),
        scratch_shapes=[pltpu.VMEM((SUBLANES, D), F32)],
        compiler_params=_cparams(),
    )(x, target)


CONV_ROWS = 64


def _conv_halo(K):
    return SUBLANES if K - 1 <= SUBLANES else 32


def _conv_fwd(u, w, b, K, name):
    S, C = u.shape
    KP = w.shape[0]
    T, HB, RS = min(512, S), _conv_halo(K), CONV_ROWS
    ratio = T // HB

    def body(u_ref, h_ref, w_ref, b_ref, o_ref, ext):
        i = pl.program_id(1)
        ext[0:HB, :] = jnp.where(i > 0, h_ref[...], 0.0)
        ext[HB:HB + T, :] = u_ref[...]
        for r0 in range(0, T, RS):
            acc = jnp.broadcast_to(b_ref[...], (RS, LANES))
            for k in range(K):
                off = HB - (K - 1) + k + r0
                acc = acc + w_ref[k:k + 1, :] * ext[off:off + RS, :]
            o_ref[r0:r0 + RS, :] = acc

    return pl.pallas_call(
        body, name=name, grid=(C // LANES, S // T),
        out_shape=jax.ShapeDtypeStruct((S, C), F32),
        in_specs=[pl.BlockSpec((T, LANES), lambda c, i: (i, c)),
                  pl.BlockSpec((HB, LANES), lambda c, i: (jnp.maximum(i * ratio - 1, 0), c)),
                  pl.BlockSpec((KP, LANES), lambda c, i: (0, c)),
                  pl.BlockSpec((1, LANES), lambda c, i: (0, c))],
        out_specs=pl.BlockSpec((T, LANES), lambda c, i: (i, c)),
        scratch_shapes=[pltpu.VMEM((HB + T, LANES), F32)],
        compiler_params=_cparams(),
    )(u, u, w, b)


def _conv_bwd(d, u, w, K, name):
    S, C = u.shape
    KP = w.shape[0]
    T, HB, RS = min(512, S), _conv_halo(K), CONV_ROWS
    ratio = T // HB
    nt = S // T
    last_halo = S // HB - 1

    def body(d_ref, dn_ref, u_ref, up_ref, w_ref, du_ref, dw_ref, db_ref, extd, extu, dws, dbs):
        i = pl.program_id(1)
        extd[0:T, :] = d_ref[...]
        extd[T:T + HB, :] = jnp.where(i < nt - 1, dn_ref[...], 0.0)
        extu[0:HB, :] = jnp.where(i > 0, up_ref[...], 0.0)
        extu[HB:HB + T, :] = u_ref[...]

        @pl.when(i == 0)
        def _():
            dws[...] = jnp.zeros_like(dws)
            dbs[...] = jnp.zeros_like(dbs)

        for r0 in range(0, T, RS):
            acc = jnp.zeros((RS, LANES), F32)
            for k in range(K):
                off = (K - 1 - k) + r0
                acc = acc + w_ref[k:k + 1, :] * extd[off:off + RS, :]
            du_ref[r0:r0 + RS, :] = acc
            dch = d_ref[r0:r0 + RS, :]
            dbs[...] += _fold8(dch)
            for k in range(K):
                off = HB - (K - 1) + k + r0
                dws[k * SUBLANES:(k + 1) * SUBLANES, :] += _fold8(dch * extu[off:off + RS, :])

        @pl.when(i == nt - 1)
        def _():
            dw_ref[...] = jnp.zeros_like(dw_ref)
            for k in range(K):
                dw_ref[k:k + 1, :] = jnp.sum(dws[k * SUBLANES:(k + 1) * SUBLANES, :], axis=0, keepdims=True)
            db_ref[...] = jnp.sum(dbs[...], axis=0, keepdims=True)

    return pl.pallas_call(
        body, name=name, grid=(C // LANES, nt),
        out_shape=(jax.ShapeDtypeStruct((S, C), F32), jax.ShapeDtypeStruct((KP, C), F32),
                   jax.ShapeDtypeStruct((1, C), F32)),
        in_specs=[pl.BlockSpec((T, LANES), lambda c, i: (i, c)),
                  pl.BlockSpec((HB, LANES), lambda c, i: (jnp.minimum((i + 1) * ratio, last_halo), c)),
                  pl.BlockSpec((T, LANES), lambda c, i: (i, c)),
                  pl.BlockSpec((HB, LANES), lambda c, i: (jnp.maximum(i * ratio - 1, 0), c)),
                  pl.BlockSpec((KP, LANES), lambda c, i: (0, c))],
        out_specs=(pl.BlockSpec((T, LANES), lambda c, i: (i, c)),
                   pl.BlockSpec((KP, LANES), lambda c, i: (0, c)),
                   pl.BlockSpec((1, LANES), lambda c, i: (0, c))),
        scratch_shapes=[pltpu.VMEM((T + HB, LANES), F32), pltpu.VMEM((HB + T, LANES), F32),
                        pltpu.VMEM((KP * SUBLANES, LANES), F32), pltpu.VMEM((SUBLANES, LANES), F32)],
        compiler_params=_cparams(),
    )(d, d, u, u, w)


SCW = 512
ZE = 3072
QL = 256
KVL = 128


def _rms_rows(x, g):
    rstd = lax.rsqrt(jnp.mean(x * x, axis=-1, keepdims=True) + EPS)
    return (x * rstd) * g


def _even_pre(z, qg, kvg, name):
    S = z.shape[0]
    T = _row_tile(S)

    def body(ac_ref, ax_ref, cq_ref, ckv_ref, qg_ref, kvg_ref, u_ref, qn_ref, kvn_ref):
        u_ref[...] = ac_ref[...] * ax_ref[...]
        qn_ref[...] = _rms_rows(cq_ref[...], qg_ref[...]).astype(BF16)
        kvn_ref[...] = _rms_rows(ckv_ref[...], kvg_ref[...]).astype(BF16)

    return pl.pallas_call(
        body, name=name, grid=(S // T,),
        out_shape=(jax.ShapeDtypeStruct((S, SCW), F32), jax.ShapeDtypeStruct((S, QL), BF16),
                   jax.ShapeDtypeStruct((S, KVL), BF16)),
        in_specs=[_rows(T, SCW, 1), _rows(T, SCW, 2), _rows(T, QL, 10), _rows(T, KVL, 22),
                  _const((1, QL)), _const((1, KVL))],
        out_specs=(_rows(T, SCW), _rows(T, QL), _rows(T, KVL)),
        compiler_params=_cparams(),
    )(z, z, z, z, qg, kvg)


def _qkv_fwd(qn, kvn, z, tabs, w_q, w_kv, name):
    S = qn.shape[0]
    T = _row_tile(S)
    HW = HEADS * HEAD_PAD
    scale = 1.0 / math.sqrt(QK_NOPE + QK_ROPE)

    def body(qn_ref, kvn_ref, kr_ref, ct_ref, ut_ref, dt_ref, wq_ref, wkv_ref, q_ref, k_ref, v_ref):
        ct, ut, dt = ct_ref[...], ut_ref[...], dt_ref[...]
        qa = jnp.dot(qn_ref[...], wq_ref[...], preferred_element_type=F32)
        kva = jnp.dot(kvn_ref[...], wkv_ref[...], preferred_element_type=F32)
        kr = kr_ref[...]
        for h in range(HEADS):
            sl = slice(h * HEAD_PAD, (h + 1) * HEAD_PAD)
            q_ref[:, sl] = (_rope(qa[:, sl], ct, ut, dt) * scale).astype(BF16)
            k_ref[:, sl] = _rope(kva[:, sl] + kr, ct, ut, dt).astype(BF16)
        v_ref[...] = kva[:, HW:].astype(BF16)

    return pl.pallas_call(
        body, name=name, grid=(S // T,),
        out_shape=(jax.ShapeDtypeStruct((S, HW), BF16), jax.ShapeDtypeStruct((S, HW), BF16),
                   jax.ShapeDtypeStruct((S, HEADS * V_HEAD), BF16)),
        in_specs=[_rows(T, QL), _rows(T, KVL), _rows(T, HEAD_PAD, 23),
                  _rows(T, HEAD_PAD), _rows(T, HEAD_PAD), _rows(T, HEAD_PAD),
                  _const(w_q.shape), _const(w_kv.shape)],
        out_specs=(_rows(T, HW), _rows(T, HW), _rows(T, HEADS * V_HEAD)),
        compiler_params=_cparams(),
    )(qn, kvn, z, *tabs, w_q, w_kv)


def _attn_tile(S):
    return min(256, S)


def _chunk_mask(TQ):
    r = lax.broadcasted_iota(jnp.int32, (TQ, TQ), 0) // CHUNK
    c = lax.broadcasted_iota(jnp.int32, (TQ, TQ), 1) // CHUNK
    return c <= r


_NT = (((1,), (1,)), ((), ()))
_TN = (((0,), (0,)), ((), ()))


def _attn_fwd(q, k, v, name):
    S = q.shape[0]
    TQ = _attn_tile(S)
    nq = S // TQ
    PW = 2 * HEAD_PAD

    def body(q_ref, k_ref, v_ref, o_ref, lse_ref, m_s, l_s, acc_s):
        i = pl.program_id(1)
        left = lax.broadcasted_iota(jnp.int32, (TQ, LANES), 1) < V_HEAD
        m_s[...] = jnp.full_like(m_s, NEG)
        l_s[...] = jnp.zeros_like(l_s)
        acc_s[...] = jnp.zeros_like(acc_s)
        qv = q_ref[...]

        def step(j, masked):
            r0 = pl.multiple_of(j * TQ, TQ)
            kb = k_ref[pl.ds(r0, TQ), :]
            vb = v_ref[pl.ds(r0, TQ), :]
            alphas, pvs = [], []
            for h in range(2):
                sl = slice(h * HEAD_PAD, (h + 1) * HEAD_PAD)
                s = lax.dot_general(qv[:, sl], kb[:, sl], _NT, preferred_element_type=F32)
                if masked:
                    s = jnp.where(_chunk_mask(TQ), s, NEG)
                m_prev = m_s[h]
                m_new = jnp.maximum(m_prev, jnp.max(s, axis=1, keepdims=True))
                alpha = jnp.exp(m_prev - m_new)
                p = jnp.exp(s - m_new[:, 0:1])
                l_s[h] = alpha * l_s[h] + jnp.sum(p, axis=1, keepdims=True)
                m_s[h] = m_new
                alphas.append(alpha)
                pvs.append(jnp.dot(p.astype(BF16), vb, preferred_element_type=F32))
            acc_s[...] = acc_s[...] * jnp.where(left, alphas[0], alphas[1]) + jnp.where(left, pvs[0], pvs[1])

        def loop_body(j, carry):
            step(j, False)
            return carry

        lax.fori_loop(0, i, loop_body, 0)
        step(i, True)
        o_ref[...] = acc_s[...] / jnp.where(left, l_s[0], l_s[1])
        lse_ref[...] = jnp.where(left, m_s[0] + jnp.log(l_s[0]), m_s[1] + jnp.log(l_s[1]))

    return pl.pallas_call(
        body, name=name, grid=(HEADS // 2, nq),
        out_shape=(jax.ShapeDtypeStruct((S, HEADS * V_HEAD), F32), jax.ShapeDtypeStruct((S, HEADS * V_HEAD), F32)),
        in_specs=[pl.BlockSpec((TQ, PW), lambda p, i: (i, p)),
                  pl.BlockSpec((S, PW), lambda p, i: (0, p)),
                  pl.BlockSpec((S, LANES), lambda p, i: (0, p))],
        out_specs=(pl.BlockSpec((TQ, LANES), lambda p, i: (i, p)),
                   pl.BlockSpec((TQ, LANES), lambda p, i: (i, p))),
        scratch_shapes=[pltpu.VMEM((2, TQ, LANES), F32), pltpu.VMEM((2, TQ, LANES), F32),
                        pltpu.VMEM((TQ, LANES), F32)],
        compiler_params=_cparams(),
    )(q, k, v)


def _attn_dq(q, k, v, do, lse, delta, name):
    S = q.shape[0]
    TQ = _attn_tile(S)
    nq = S // TQ
    PW = 2 * HEAD_PAD

    def body(q_ref, k_ref, v_ref, do_ref, lse_ref, dl_ref, dq_ref, acc_s):
        i = pl.program_id(1)
        left = lax.broadcasted_iota(jnp.int32, (TQ, LANES), 1) < V_HEAD
        acc_s[...] = jnp.zeros_like(acc_s)
        qv = q_ref[...]
        dov = do_ref[...]
        dos = [jnp.where(left, dov, jnp.zeros_like(dov)), jnp.where(left, jnp.zeros_like(dov), dov)]
        lses = [lse_ref[:, 0:1], lse_ref[:, V_HEAD:V_HEAD + 1]]
        dls = [dl_ref[:, 0:1], dl_ref[:, V_HEAD:V_HEAD + 1]]

        def step(j, masked):
            r0 = pl.multiple_of(j * TQ, TQ)
            kb = k_ref[pl.ds(r0, TQ), :]
            vb = v_ref[pl.ds(r0, TQ), :]
            for h in range(2):
                sl = slice(h * HEAD_PAD, (h + 1) * HEAD_PAD)
                s = lax.dot_general(qv[:, sl], kb[:, sl], _NT, preferred_element_type=F32)
                p = jnp.exp(s - lses[h])
                if masked:
                    p = jnp.where(_chunk_mask(TQ), p, 0.0)
                dp = lax.dot_general(dos[h], vb, _NT, preferred_element_type=F32)
                ds = (p * (dp - dls[h])).astype(BF16)
                acc_s[:, sl] += jnp.dot(ds, kb[:, sl], preferred_element_type=F32)

        def loop_body(j, carry):
            step(j, False)
            return carry

        lax.fori_loop(0, i, loop_body, 0)
        step(i, True)
        dq_ref[...] = acc_s[...]

    return pl.pallas_call(
        body, name=name, grid=(HEADS // 2, nq),
        out_shape=jax.ShapeDtypeStruct((S, HEADS * HEAD_PAD), F32),
        in_specs=[pl.BlockSpec((TQ, PW), lambda p, i: (i, p)),
                  pl.BlockSpec((S, PW), lambda p, i: (0, p)),
                  pl.BlockSpec((S, LANES), lambda p, i: (0, p)),
                  pl.BlockSpec((TQ, LANES), lambda p, i: (i, p)),
                  pl.BlockSpec((TQ, LANES), lambda p, i: (i, p)),
                  pl.BlockSpec((TQ, LANES), lambda p, i: (i, p))],
        out_specs=pl.BlockSpec((TQ, PW), lambda p, i: (i, p)),
        scratch_shapes=[pltpu.VMEM((TQ, PW), F32)],
        compiler_params=_cparams(),
    )(q, k, v, do, lse, delta)


def _attn_dkv(q, k, v, do, lse, delta, name):
    S = q.shape[0]
    TQ = _attn_tile(S)
    nq = S // TQ
    PW = 2 * HEAD_PAD

    def body(q_ref, k_ref, v_ref, do_ref, lse_ref, dl_ref, dk_ref, dv_ref, dk_s, dv_s):
        j = pl.program_id(1)
        left = lax.broadcasted_iota(jnp.int32, (TQ, LANES), 1) < V_HEAD
        dk_s[...] = jnp.zeros_like(dk_s)
        dv_s[...] = jnp.zeros_like(dv_s)
        kb = k_ref[...]
        vb = v_ref[...]

        def step(i, masked):
            r0 = pl.multiple_of(i * TQ, TQ)
            qb = q_ref[pl.ds(r0, TQ), :]
            dov = do_ref[pl.ds(r0, TQ), :]
            lse = lse_ref[pl.ds(r0, TQ), :]
            dl = dl_ref[pl.ds(r0, TQ), :]
            dos = [jnp.where(left, dov, jnp.zeros_like(dov)), jnp.where(left, jnp.zeros_like(dov), dov)]
            for h in range(2):
                sl = slice(h * HEAD_PAD, (h + 1) * HEAD_PAD)
                c0 = h * V_HEAD
                s = lax.dot_general(qb[:, sl], kb[:, sl], _NT, preferred_element_type=F32)
                p = jnp.exp(s - lse[:, c0:c0 + 1])
                if masked:
                    p = jnp.where(_chunk_mask(TQ), p, 0.0)
                dv_s[...] += lax.dot_general(p.astype(BF16), dos[h], _TN, preferred_element_type=F32)
                dp = lax.dot_general(dos[h], vb, _NT, preferred_element_type=F32)
                ds = (p * (dp - dl[:, c0:c0 + 1])).astype(BF16)
                dk_s[:, sl] += lax.dot_general(ds, qb[:, sl], _TN, preferred_element_type=F32)

        def loop_body(i, carry):
            step(i, False)
            return carry

        step(j, True)
        lax.fori_loop(j + 1, nq, loop_body, 0)
        dk_ref[...] = dk_s[...]
        dv_ref[...] = dv_s[...]

    return pl.pallas_call(
        body, name=name, grid=(HEADS // 2, nq),
        out_shape=(jax.ShapeDtypeStruct((S, HEADS * HEAD_PAD), F32), jax.ShapeDtypeStruct((S, HEADS * V_HEAD), F32)),
        in_specs=[pl.BlockSpec((S, PW), lambda p, j: (0, p)),
                  pl.BlockSpec((TQ, PW), lambda p, j: (j, p)),
                  pl.BlockSpec((TQ, LANES), lambda p, j: (j, p)),
                  pl.BlockSpec((S, LANES), lambda p, j: (0, p)),
                  pl.BlockSpec((S, LANES), lambda p, j: (0, p)),
                  pl.BlockSpec((S, LANES), lambda p, j: (0, p))],
        out_specs=(pl.BlockSpec((TQ, PW), lambda p, j: (j, p)),
                   pl.BlockSpec((TQ, LANES), lambda p, j: (j, p))),
        scratch_shapes=[pltpu.VMEM((TQ, PW), F32), pltpu.VMEM((TQ, LANES), F32)],
        compiler_params=_cparams(),
    )(q, k, v, do, lse, delta)


ATTN_FWD_HEADS = 8
ATTN_BWD_HEADS = 4


def _chunk_mask_t(T):
    key = lax.broadcasted_iota(jnp.int32, (T, T), 0) // CHUNK
    qry = lax.broadcasted_iota(jnp.int32, (T, T), 1) // CHUNK
    return key <= qry


def _qkv_fwd_t(qn, kvn, z, tabs, w_q, w_kv, name):
    S = qn.shape[0]
    T = _attn_tile(S)
    HW = HEADS * HEAD_PAD
    scale = 1.0 / math.sqrt(QK_NOPE + QK_ROPE)

    def body(qn_ref, kvn_ref, kr_ref, ct_ref, ut_ref, dt_ref, wq_ref, wkv_ref, q_ref, k_ref, v_ref, kt_ref, vt_ref):
        ct, ut, dt = ct_ref[...], ut_ref[...], dt_ref[...]
        qa = jnp.dot(qn_ref[...], wq_ref[...], preferred_element_type=F32)
        kva = jnp.dot(kvn_ref[...], wkv_ref[...], preferred_element_type=F32)
        kr = kr_ref[...]
        ones_row = (lax.broadcasted_iota(jnp.int32, (V_HEAD, T), 0) == 0).astype(F32)
        for h in range(HEADS):
            sl = slice(h * HEAD_PAD, (h + 1) * HEAD_PAD)
            q_ref[:, sl] = (_rope(qa[:, sl], ct, ut, dt) * scale).astype(BF16)
            kh = _rope(kva[:, sl] + kr, ct, ut, dt)
            k_ref[:, sl] = kh.astype(BF16)
            kt_ref[0, sl, :] = kh.T.astype(BF16)
        v_ref[...] = kva[:, HW:].astype(BF16)
        for p in range(HEADS // 2):
            vpt = kva[:, HW + p * LANES:HW + (p + 1) * LANES].T
            for h in range(2):
                r0 = (2 * p + h) * HEAD_PAD
                vt_ref[0, r0:r0 + V_HEAD, :] = vpt[h * V_HEAD:(h + 1) * V_HEAD, :].astype(BF16)
                vt_ref[0, r0 + V_HEAD:r0 + HEAD_PAD, :] = ones_row.astype(BF16)

    t3 = jax.ShapeDtypeStruct((S // T, HW, T), BF16)
    return pl.pallas_call(
        body, name=name, grid=(S // T,),
        out_shape=(jax.ShapeDtypeStruct((S, HW), BF16), jax.ShapeDtypeStruct((S, HW), BF16),
                   jax.ShapeDtypeStruct((S, HEADS * V_HEAD), BF16), t3, t3),
        in_specs=[_rows(T, QL), _rows(T, KVL), _rows(T, HEAD_PAD, 23),
                  _rows(T, HEAD_PAD), _rows(T, HEAD_PAD), _rows(T, HEAD_PAD),
                  _const(w_q.shape), _const(w_kv.shape)],
        out_specs=(_rows(T, HW), _rows(T, HW), _rows(T, HEADS * V_HEAD),
                   pl.BlockSpec((1, HW, T), lambda i: (i, 0, 0)), pl.BlockSpec((1, HW, T), lambda i: (i, 0, 0))),
        compiler_params=_cparams(),
    )(qn, kvn, z, *tabs, w_q, w_kv)


def _attn_fwd_t(q, k, vT3, name):
    S = q.shape[0]
    T = _attn_tile(S)
    nq = S // T
    NH = ATTN_FWD_HEADS
    PW = NH * HEAD_PAD

    def body(q_ref, k_ref, vt_ref, o_ref, lse_ref, m_s, acc_s):
        i = pl.program_id(1)
        m_s[...] = jnp.full_like(m_s, NEG)
        acc_s[...] = jnp.zeros_like(acc_s)
        qv = q_ref[...]

        def step(j, masked):
            kb = k_ref[pl.ds(pl.multiple_of(j * T, T), T), :]
            vt = vt_ref[j]
            heads = [slice(h * HEAD_PAD, (h + 1) * HEAD_PAD) for h in range(NH)]
            sts = [lax.dot_general(kb[:, sl], qv[:, sl], _NT, preferred_element_type=F32) for sl in heads]
            alphas, pvs = [], []
            for h, sl in enumerate(heads):
                st = jnp.where(_chunk_mask_t(T), sts[h], NEG) if masked else sts[h]
                m_prev = m_s[h]
                m_new = jnp.maximum(m_prev, jnp.max(st, axis=0, keepdims=True))
                alphas.append(jnp.exp(m_prev[0:1] - m_new[0:1]))
                pt = jnp.exp(st - m_new[0:1]).astype(BF16)
                m_s[h] = m_new
                pvs.append(jnp.dot(vt[sl, :], pt, preferred_element_type=F32))
            for h in range(NH):
                acc_s[h] = acc_s[h] * alphas[h] + pvs[h]

        def loop_body(j, carry):
            step(j, False)
            return carry

        lax.fori_loop(0, i, loop_body, 0)
        step(i, True)
        for g in range(NH // 2):
            outs = []
            for h in (2 * g, 2 * g + 1):
                acc = acc_s[h]
                l_row = acc[V_HEAD:V_HEAD + 1, :]
                outs.append(acc[0:V_HEAD, :] / l_row)
                lse_ref[0, h * SUBLANES:(h + 1) * SUBLANES, :] = m_s[h] + jnp.log(l_row)
            o_ref[:, g * LANES:(g + 1) * LANES] = jnp.concatenate(outs, axis=0).T

    return pl.pallas_call(
        body, name=name, grid=(HEADS // NH, nq),
        out_shape=(jax.ShapeDtypeStruct((S, HEADS * V_HEAD), F32),
                   jax.ShapeDtypeStruct((nq, HEADS * SUBLANES, T), F32)),
        in_specs=[pl.BlockSpec((T, PW), lambda p, i: (i, p)),
                  pl.BlockSpec((S, PW), lambda p, i: (0, p)),
                  pl.BlockSpec((nq, PW, T), lambda p, i: (0, p, 0))],
        out_specs=(pl.BlockSpec((T, NH * V_HEAD), lambda p, i: (i, p)),
                   pl.BlockSpec((1, NH * SUBLANES, T), lambda p, i: (i, p, 0))),
        scratch_shapes=[pltpu.VMEM((NH, SUBLANES, T), F32), pltpu.VMEM((NH, HEAD_PAD, T), F32)],
        compiler_params=_cparams(),
    )(q, k, vT3)


def _attn_bwd_t(q, k, v, kT3, do, lse3, dl3, name):
    S = q.shape[0]
    T = _attn_tile(S)
    nq = S // T
    NH = ATTN_BWD_HEADS
    PW = NH * HEAD_PAD
    VW = NH * V_HEAD

    def body(q_ref, k_ref, v_ref, kt_ref, do_ref, lse_ref, dl_ref, dq_ref, dk_ref, dv_ref, dk_s, dv_s):
        j = pl.program_id(1)
        left = lax.broadcasted_iota(jnp.int32, (T, LANES), 1) < V_HEAD

        @pl.when(j == 0)
        def _():
            dq_ref[...] = jnp.zeros_like(dq_ref)

        dk_s[...] = jnp.zeros_like(dk_s)
        dv_s[...] = jnp.zeros_like(dv_s)
        kb = k_ref[...]
        vms = []
        for g in range(NH // 2):
            vb = v_ref[:, g * LANES:(g + 1) * LANES]
            vms += [jnp.where(left, vb, jnp.zeros_like(vb)), jnp.where(left, jnp.zeros_like(vb), vb)]
        kt = kt_ref[0]

        def step(i, masked):
            r0 = pl.multiple_of(i * T, T)
            qb = q_ref[pl.ds(r0, T), :]
            do_all = do_ref[pl.ds(r0, T), :]
            lse = lse_ref[i]
            dl = dl_ref[i]
            heads = [slice(h * HEAD_PAD, (h + 1) * HEAD_PAD) for h in range(NH)]
            dobs = [do_all[:, (h // 2) * LANES:(h // 2 + 1) * LANES] for h in range(NH)]
            sts = [lax.dot_general(kb[:, sl], qb[:, sl], _NT, preferred_element_type=F32) for sl in heads]
            dpts = [lax.dot_general(vms[h], dobs[h], _NT, preferred_element_type=F32) for h in range(NH)]
            res = []
            for h, sl in enumerate(heads):
                r8 = h * SUBLANES
                pt = jnp.exp(sts[h] - lse[r8:r8 + 1, :])
                if masked:
                    pt = jnp.where(_chunk_mask_t(T), pt, 0.0)
                dst = (pt * (dpts[h] - dl[r8:r8 + 1, :])).astype(BF16)
                res.append((jnp.dot(pt.astype(BF16), dobs[h], preferred_element_type=F32),
                            jnp.dot(dst, qb[:, sl], preferred_element_type=F32),
                            jnp.dot(kt[sl, :], dst, preferred_element_type=F32)))
            for h, sl in enumerate(heads):
                dv_s[h] += res[h][0]
                dk_s[:, sl] += res[h][1]
                dq_ref[i, sl, :] += res[h][2]

        def loop_body(i, carry):
            step(i, False)
            return carry

        step(j, True)
        lax.fori_loop(j + 1, nq, loop_body, 0)
        dk_ref[...] = dk_s[...]
        for g in range(NH // 2):
            dv_ref[:, g * LANES:(g + 1) * LANES] = jnp.where(left, dv_s[2 * g], dv_s[2 * g + 1])

    return pl.pallas_call(
        body, name=name, grid=(HEADS // NH, nq),
        out_shape=(jax.ShapeDtypeStruct((nq, HEADS * HEAD_PAD, T), F32),
                   jax.ShapeDtypeStruct((S, HEADS * HEAD_PAD), F32), jax.ShapeDtypeStruct((S, HEADS * V_HEAD), F32)),
        in_specs=[pl.BlockSpec((S, PW), lambda p, j: (0, p)),
                  pl.BlockSpec((T, PW), lambda p, j: (j, p)),
                  pl.BlockSpec((T, VW), lambda p, j: (j, p)),
                  pl.BlockSpec((1, PW, T), lambda p, j: (j, p, 0)),
                  pl.BlockSpec((S, VW), lambda p, j: (0, p)),
                  pl.BlockSpec((nq, NH * SUBLANES, T), lambda p, j: (0, p, 0)),
                  pl.BlockSpec((nq, NH * SUBLANES, T), lambda p, j: (0, p, 0))],
        out_specs=(pl.BlockSpec((nq, PW, T), lambda p, j: (0, p, 0)),
                   pl.BlockSpec((T, PW), lambda p, j: (j, p)),
                   pl.BlockSpec((T, VW), lambda p, j: (j, p))),
        scratch_shapes=[pltpu.VMEM((T, PW), F32), pltpu.VMEM((NH, T, LANES), F32)],
        compiler_params=_cparams(),
    )(q, k, v, kT3, do, lse3, dl3)


def _even_post(z, cv, o, name):
    S = z.shape[0]
    T = _row_tile(S)

    def body(ab_ref, ag_ref, bg_ref, cv_ref, o_ref, y_ref):
        y_ref[:, 0:SCW] = (ab_ref[...] * cv_ref[...] * _silu(ag_ref[...])).astype(BF16)
        y_ref[:, SCW:2 * SCW] = (o_ref[...] * _silu(bg_ref[...])).astype(BF16)

    return pl.pallas_call(
        body, name=name, grid=(S // T,),
        out_shape=jax.ShapeDtypeStruct((S, 2 * SCW), BF16),
        in_specs=[_rows(T, SCW, 0), _rows(T, SCW, 3), _rows(T, SCW, 4), _rows(T, SCW), _rows(T, SCW)],
        out_specs=_rows(T, 2 * SCW), compiler_params=_cparams(),
    )(z, z, z, cv, o)


def _even_bwd_gates(dyc, z, cv, o, name):
    S = z.shape[0]
    T = _row_tile(S)

    def body(dya_ref, dyb_ref, ab_ref, ag_ref, bg_ref, cv_ref, o_ref,
             dab_ref, dag_ref, dbg_ref, dcv_ref, do_ref, dl_ref):
        dya, ab, ag, cv = dya_ref[...], ab_ref[...], ag_ref[...], cv_ref[...]
        sg = _silu(ag)
        dab_ref[...] = (dya * cv * sg).astype(BF16)
        dcv_ref[...] = dya * ab * sg
        dag_ref[...] = (dya * ab * cv * _dsilu(ag)).astype(BF16)
        dyb, bg, ov = dyb_ref[...], bg_ref[...], o_ref[...]
        dov = dyb * _silu(bg)
        do_ref[...] = dov.astype(BF16)
        dbg_ref[...] = (dyb * ov * _dsilu(bg)).astype(BF16)
        prod = dov * ov
        left = lax.broadcasted_iota(jnp.int32, (T, LANES), 1) < V_HEAD
        for p in range(HEADS // 2):
            blk = prod[:, p * LANES:(p + 1) * LANES]
            s0 = jnp.sum(jnp.where(left, blk, 0.0), axis=1, keepdims=True)
            s1 = jnp.sum(jnp.where(left, 0.0, blk), axis=1, keepdims=True)
            dt = jnp.where(left, s0, s1).T
            dl_ref[0, 2 * p * SUBLANES:(2 * p + 1) * SUBLANES, :] = dt[0:SUBLANES, :]
            dl_ref[0, (2 * p + 1) * SUBLANES:(2 * p + 2) * SUBLANES, :] = dt[V_HEAD:V_HEAD + SUBLANES, :]

    assert T == _attn_tile(S)
    bf = jax.ShapeDtypeStruct((S, SCW), BF16)
    ff = jax.ShapeDtypeStruct((S, SCW), F32)
    return pl.pallas_call(
        body, name=name, grid=(S // T,),
        out_shape=(bf, bf, bf, ff, bf, jax.ShapeDtypeStruct((S // T, HEADS * SUBLANES, T), F32)),
        in_specs=[_rows(T, SCW, 0), _rows(T, SCW, 1), _rows(T, SCW, 0), _rows(T, SCW, 3), _rows(T, SCW, 4),
                  _rows(T, SCW), _rows(T, SCW)],
        out_specs=(_rows(T, SCW),) * 5 + (pl.BlockSpec((1, HEADS * SUBLANES, T), lambda i: (i, 0, 0)),),
        compiler_params=_cparams(),
    )(dyc, dyc, z, z, z, cv, o)


def _qkv_bwd(dq, dk, dv, z, tabs, w_q, w_kv, qg, kvg, name):
    S = dk.shape[0]
    T = _attn_tile(S)
    HW = HEADS * HEAD_PAD
    VW = HEADS * V_HEAD
    scale = 1.0 / math.sqrt(QK_NOPE + QK_ROPE)

    def fn(ins, outs):
        dq_ref, dk_ref, dv_ref, cq_ref, ckv_ref, ct_ref, ut_ref, dt_ref, wq_ref, wkv_ref, qg_ref, kvg_ref = ins
        dqp_ref, dkvp_ref, dcq_ref, dckv_ref, dkr_ref = outs
        ct, ut, dt = ct_ref[...], ut_ref[...], dt_ref[...]
        dkr = jnp.zeros((T, HEAD_PAD), F32)
        for h in range(HEADS):
            sl = slice(h * HEAD_PAD, (h + 1) * HEAD_PAD)
            dqp_ref[:, sl] = (_rope_t(dq_ref[0, sl, :].T, ct, ut, dt) * scale).astype(BF16)
            dkh = _rope_t(dk_ref[:, sl], ct, ut, dt)
            dkr = dkr + dkh
            dkvp_ref[:, sl] = dkh.astype(BF16)
        dkvp_ref[:, HW:] = dv_ref[...].astype(BF16)
        dkr_ref[...] = dkr.astype(BF16)
        sums = []
        for lat_ref, g_ref, dpre_ref, w_ref, dlat_ref in ((cq_ref, qg_ref, dqp_ref, wq_ref, dcq_ref),
                                                         (ckv_ref, kvg_ref, dkvp_ref, wkv_ref, dckv_ref)):
            dn = lax.dot_general(dpre_ref[...], w_ref[...], _NT, preferred_element_type=F32)
            xv = lat_ref[...]
            rstd = lax.rsqrt(jnp.mean(xv * xv, axis=-1, keepdims=True) + EPS)
            xh = xv * rstd
            dxh = dn * g_ref[...]
            dlat_ref[...] = (rstd * (dxh - xh * jnp.mean(dxh * xh, axis=-1, keepdims=True))).astype(BF16)
            sums.append(dn * xh)
        return sums

    return _col_sums(
        2, fn, [dq, dk, dv, z, z, *tabs, w_q, w_kv, qg, kvg],
        [pl.BlockSpec((1, HW, T), lambda i: (i, 0, 0)), _rows(T, HW), _rows(T, VW), _rows(T, QL, 10), _rows(T, KVL, 22),
         _rows(T, HEAD_PAD), _rows(T, HEAD_PAD), _rows(T, HEAD_PAD),
         _const(w_q.shape), _const(w_kv.shape), _const((1, QL)), _const((1, KVL))],
        [jax.ShapeDtypeStruct((S, HW), BF16), jax.ShapeDtypeStruct((S, HW + VW), BF16),
         jax.ShapeDtypeStruct((S, QL), BF16), jax.ShapeDtypeStruct((S, KVL), BF16),
         jax.ShapeDtypeStruct((S, HEAD_PAD), BF16)],
        [_rows(T, HW), _rows(T, HW + VW), _rows(T, QL), _rows(T, KVL), _rows(T, HEAD_PAD)],
        S, T, [QL, KVL], name)


def _even_dz(dab, du, z, dag, dbg, dcq, dckv, dkr, name):
    S = z.shape[0]
    T = _row_tile(S)

    def body(dab_ref, du_ref, ac_ref, ax_ref, dag_ref, dbg_ref, dcq_ref, dckv_ref, dkr_ref, dz_ref):
        duv = du_ref[...]
        dz_ref[:, 0:SCW] = dab_ref[...]
        dz_ref[:, SCW:2 * SCW] = (duv * ax_ref[...]).astype(BF16)
        dz_ref[:, 2 * SCW:3 * SCW] = (duv * ac_ref[...]).astype(BF16)
        dz_ref[:, 3 * SCW:4 * SCW] = dag_ref[...]
        dz_ref[:, 4 * SCW:5 * SCW] = dbg_ref[...]
        dz_ref[:, 5 * SCW:5 * SCW + QL] = dcq_ref[...]
        dz_ref[:, 5 * SCW + QL:5 * SCW + QL + KVL] = dckv_ref[...]
        dz_ref[:, 5 * SCW + QL + KVL:ZE] = dkr_ref[...]

    return pl.pallas_call(
        body, name=name, grid=(S // T,),
        out_shape=jax.ShapeDtypeStruct((S, ZE), BF16),
        in_specs=[_rows(T, SCW), _rows(T, SCW), _rows(T, SCW, 1), _rows(T, SCW, 2), _rows(T, SCW), _rows(T, SCW),
                  _rows(T, QL), _rows(T, KVL), _rows(T, HEAD_PAD)],
        out_specs=_rows(T, ZE), compiler_params=_cparams(),
    )(dab, du, z, z, dag, dbg, dcq, dckv, dkr)


def _odd_pre(z, name):
    S, D = z.shape[0], z.shape[1] // 3
    T = _row_tile(S)

    def body(val_ref, glu_ref, u_ref):
        u_ref[...] = val_ref[...] * _sigmoid(glu_ref[...])

    return pl.pallas_call(
        body, name=name, grid=(S // T,),
        out_shape=jax.ShapeDtypeStruct((S, D), F32),
        in_specs=[_rows(T, D, 0), _rows(T, D, 1)], out_specs=_rows(T, D),
        compiler_params=_cparams(),
    )(z, z)


def _layer_norm_stats(cv):
    mu = jnp.mean(cv, axis=-1, keepdims=True)
    cen = cv - mu
    rstd = lax.rsqrt(jnp.mean(cen * cen, axis=-1, keepdims=True) + EPS)
    return cen * rstd, rstd


def _odd_post(cv, z, ln_g, ln_b, name):
    S, D = cv.shape
    T = _row_tile(S)

    def body(cv_ref, sg_ref, g_ref, b_ref, y_ref):
        cvh, _ = _layer_norm_stats(cv_ref[...])
        y_ref[...] = (_silu(cvh * g_ref[...] + b_ref[...]) * _silu(sg_ref[...])).astype(BF16)

    return pl.pallas_call(
        body, name=name, grid=(S // T,),
        out_shape=jax.ShapeDtypeStruct((S, D), BF16),
        in_specs=[_rows(T, D), _rows(T, D, 2), _const((1, D)), _const((1, D))],
        out_specs=_rows(T, D), compiler_params=_cparams(),
    )(cv, z, ln_g, ln_b)


def _odd_bwd_norm(dyi, cv, z, ln_g, ln_b, name):
    S, D = cv.shape
    T = _row_tile(S)

    def fn(ins, outs):
        dy_ref, cv_ref, sg_ref, g_ref, b_ref = ins
        dcv_ref, dsg_ref = outs
        cvh, rstd = _layer_norm_stats(cv_ref[...])
        ln = cvh * g_ref[...] + b_ref[...]
        sgv, dy = sg_ref[...], dy_ref[...]
        dsg_ref[...] = (dy * _silu(ln) * _dsilu(sgv)).astype(BF16)
        dln = dy * _silu(sgv) * _dsilu(ln)
        dh = dln * g_ref[...]
        dcv_ref[...] = rstd * (dh - jnp.mean(dh, axis=-1, keepdims=True)
                               - cvh * jnp.mean(dh * cvh, axis=-1, keepdims=True))
        return [dln * cvh, dln]

    return _col_sums(2, fn, [dyi, cv, z, ln_g, ln_b],
                     [_rows(T, D), _rows(T, D), _rows(T, D, 2), _const((1, D)), _const((1, D))],
                     [jax.ShapeDtypeStruct((S, D), F32), jax.ShapeDtypeStruct((S, D), BF16)],
                     [_rows(T, D), _rows(T, D)], S, T, [D, D], name)


def _odd_dz(du, z, dsg, name):
    S, D = du.shape
    T = _row_tile(S)

    def body(du_ref, val_ref, glu_ref, dsg_ref, dz_ref):
        duv = du_ref[...]
        sig = _sigmoid(glu_ref[...])
        dz_ref[:, 0:D] = (duv * sig).astype(BF16)
        dz_ref[:, D:2 * D] = (duv * val_ref[...] * sig * (1.0 - sig)).astype(BF16)
        dz_ref[:, 2 * D:3 * D] = dsg_ref[...]

    return pl.pallas_call(
        body, name=name, grid=(S // T,),
        out_shape=jax.ShapeDtypeStruct((S, 3 * D), BF16),
        in_specs=[_rows(T, D), _rows(T, D, 0), _rows(T, D, 1), _rows(T, D)],
        out_specs=_rows(T, 3 * D), compiler_params=_cparams(),
    )(du, z, z, dsg)


ADAM_BLOCK_ELEMS = 128 * 1024


def _adam_tiles(R, C):
    if R * C <= ADAM_BLOCK_ELEMS:
        return R, C
    tr = R
    for cand in range(SUBLANES, R, SUBLANES):
        if R % cand == 0 and cand * C <= ADAM_BLOCK_ELEMS:
            tr = cand
    if tr < R:
        return tr, C
    tc = C
    for cand in range(LANES, C, LANES):
        if C % cand == 0 and R * cand <= ADAM_BLOCK_ELEMS:
            tc = cand
    return R, tc


def _adamw(g_parts, w, m, v, name):
    if not isinstance(g_parts, (list, tuple)):
        g_parts = [g_parts]
    ng = len(g_parts)
    _, R, C = g_parts[0].shape
    tr, tc = _adam_tiles(R, C)

    def body(*refs):
        g_refs = refs[:ng]
        w_ref, m_ref, v_ref, go_ref, d_ref, mo_ref, vo_ref = refs[ng:]
        g = None
        for g_ref in g_refs:
            for p in range(g_ref.shape[0]):
                g = g_ref[p] if g is None else g + g_ref[p]
        mn = ADAM_B1 * m_ref[...] + (1.0 - ADAM_B1) * g
        vn = ADAM_B2 * v_ref[...] + (1.0 - ADAM_B2) * (g * g)
        m_hat = mn / (1.0 - ADAM_B1 ** ADAM_STEP)
        v_hat = vn / (1.0 - ADAM_B2 ** ADAM_STEP)
        go_ref[...] = g
        d_ref[...] = -ADAM_LR * (m_hat / (jnp.sqrt(v_hat) + ADAM_EPS) + ADAM_WD * w_ref[...])
        mo_ref[...] = mn
        vo_ref[...] = vn

    slab = jax.ShapeDtypeStruct((R, C), F32)
    blk = pl.BlockSpec((tr, tc), lambda i, j: (i, j))
    return pl.pallas_call(
        body, name=name, grid=(R // tr, C // tc),
        out_shape=(slab,) * 4,
        in_specs=[pl.BlockSpec((g.shape[0], tr, tc), lambda i, j: (0, i, j)) for g in g_parts] + [blk, blk, blk],
        out_specs=(blk,) * 4, compiler_params=_cparams(),
    )(*g_parts, w, m, v)


def _gather_cols(g, shape):
    nd = len(shape)
    t = jnp.moveaxis(g, 0, nd - 1)
    return t.reshape(tuple(shape[:-1]) + (N_DEV * shape[-1],))


def _scatter_cols(full, n):
    t = full.reshape(full.shape[:-1] + (N_DEV, n))
    return jnp.moveaxis(t, -2, 0)


def kernel(x, c, positions, ada_w, ada_b, pre_norm_g, post_norm_g, even_w_in, even_sc_conv_w, even_sc_conv_b, even_q_norm_g, even_kv_norm_g, even_w_uq, even_w_ukv, even_w_out, odd_w_in, odd_conv_w, odd_conv_b, odd_ln_g, odd_ln_b, odd_w_out, loss_target, m_ada_w, m_ada_b, m_pre_norm_g, m_post_norm_g, m_even_w_in, m_even_sc_conv_w, m_even_sc_conv_b, m_even_q_norm_g, m_even_kv_norm_g, m_even_w_uq, m_even_w_ukv, m_even_w_out, m_odd_w_in, m_odd_conv_w, m_odd_conv_b, m_odd_ln_g, m_odd_ln_b, m_odd_w_out, v_ada_w, v_ada_b, v_pre_norm_g, v_post_norm_g, v_even_w_in, v_even_sc_conv_w, v_even_sc_conv_b, v_even_q_norm_g, v_even_kv_norm_g, v_even_w_uq, v_even_w_ukv, v_even_w_out, v_odd_w_in, v_odd_conv_w, v_odd_conv_b, v_odd_ln_g, v_odd_ln_b, v_odd_w_out):
    S, D = x.shape[1], x.shape[2]
    L = ada_w.shape[0]
    NE, NO = even_w_in.shape[0], odd_w_in.shape[0]
    me = 4 * lax.axis_index("x") + 2 * lax.axis_index("y") + lax.axis_index("c")
    x0 = x[0]
    target = loss_target[0]

    small_parts = [c, even_sc_conv_w, odd_conv_w, odd_conv_b, odd_ln_g, odd_ln_b]
    small_shapes = [p.shape for p in small_parts]
    sg = _exchange([_pack(small_parts, F32, SUBLANES)], False, "gather_small")[0].reshape(N_DEV, -1)
    c_all, scw_g, ocw_g, ocb_g, olg_g, olb_g = _unpack(sg, small_shapes)
    c_all = c_all.reshape(N_DEV, D)
    sc_conv_w = _gather_cols(scw_g, even_sc_conv_w.shape)
    o_conv_w = _gather_cols(ocw_g, odd_conv_w.shape)
    o_conv_b = _gather_cols(ocb_g, odd_conv_b.shape)
    o_ln_g = _gather_cols(olg_g, odd_ln_g.shape)
    o_ln_b = _gather_cols(olb_g, odd_ln_b.shape)

    pad_q = HEAD_PAD - QK_NOPE - QK_ROPE
    w_local = [jnp.swapaxes(even_w_in, 1, 2).astype(BF16),
               jnp.pad(even_w_uq, ((0, 0), (0, 0), (0, pad_q))).astype(BF16),
               jnp.pad(even_w_ukv[..., :QK_NOPE], ((0, 0), (0, 0), (0, HEAD_PAD - QK_NOPE))).astype(BF16),
               even_w_ukv[..., QK_NOPE:].astype(BF16),
               even_w_out.astype(BF16), odd_w_in.astype(BF16), odd_w_out.astype(BF16)]
    heads_to_cols = lambda g: jnp.moveaxis(g, 0, 1).reshape(g.shape[1], -1)
    w_handles = []
    token = jnp.zeros((SUBLANES, LANES), F32)
    for layer in range(L):
        i = layer // 2
        mine = [w[i] for w in w_local[:5]] if layer % 2 == 0 else [w[i] for w in w_local[5:]]
        mine = [w + token[0, 0].astype(BF16) for w in mine]
        handle, token = _exchange_start(mine, False, f"gather_weights_start_l{layer}")
        w_handles.append(handle)
    w_token = token

    def layer_weights(layer, after):
        _, got = _exchange_wait(w_handles[layer], False, after, f"gather_weights_wait_l{layer}")
        if layer % 2 == 0:
            ewt_g, eq_g, ek_g, ev_g, eout_g = got
            wt = ewt_g.reshape(-1, D)
            w_in_k = jnp.concatenate([wt[:2048], wt[2464:2976], wt[2048:2432], jnp.zeros((QK_NOPE, D), BF16),
                                      wt[2432:2464], jnp.zeros((pad_q, D), BF16)], axis=0)
            w_kv_k = jnp.concatenate([heads_to_cols(ek_g), heads_to_cols(ev_g)], axis=-1)
            return w_in_k, heads_to_cols(eq_g), w_kv_k, eout_g.reshape(-1, D)
        owin_g, oout_g = got
        return heads_to_cols(owin_g), oout_g.reshape(-1, D)

    e_w_in_k, e_w_q_k, e_w_kv_k, e_w_out, o_w_in, o_w_out = ([None] * NE, [None] * NE, [None] * NE, [None] * NE,
                                                             [None] * NO, [None] * NO)

    n_ada = ada_w.shape[2]
    ada_b_cols = lax.dynamic_slice_in_dim(ada_b, me * n_ada, n_ada, axis=1).reshape(L, 1, n_ada)
    mod_slab = _ada_fwd(c_all, ada_w, ada_b_cols)
    mod_g = _exchange([_pack([mod_slab], F32, SUBLANES)], False, "gather_mod")[0].reshape(N_DEV, -1)
    mod_all = mod_g[:, :L * N_DEV * n_ada].reshape(N_DEV, L, N_DEV, n_ada)
    mod = lax.dynamic_index_in_dim(mod_all, me, axis=2, keepdims=False)
    mod = jnp.moveaxis(mod, 0, 1).reshape(L, 3 * D)
    shift, scale, gate = mod[:, :D], mod[:, D:2 * D], mod[:, 2 * D:]

    half = QK_ROPE // 2
    inv_freq = 1.0 / (ROPE_THETA ** (jnp.arange(0, QK_ROPE, 2, dtype=F32) / QK_ROPE))
    inv_lane = jnp.zeros((HEAD_PAD,), F32).at[QK_NOPE:QK_NOPE + QK_ROPE].set(jnp.concatenate([inv_freq, inv_freq]))
    tabs = _rope_tables(positions.astype(F32).reshape(S, 1), inv_lane.reshape(1, HEAD_PAD))
    del half

    row = lambda a: a.reshape(1, -1)
    scb = even_sc_conv_b
    KP3, KP31 = SUBLANES, 32

    saved = []
    xs = x0
    for layer in range(L):
        i = layer // 2
        tag = f"l{layer}"
        h = _pre_norm(xs, row(pre_norm_g[layer]), row(scale[layer]), row(shift[layer]), f"pre_norm_{tag}")
        arrived = layer_weights(layer, w_token if layer == 0 else h)
        if layer % 2 == 0:
            e_w_in_k[i], e_w_q_k[i], e_w_kv_k[i], e_w_out[i] = arrived
            z = _matmul(h, e_w_in_k[i], "nt", F32, f"w_in_{tag}", tn=1024)
            u, qn, kvn = _even_pre(z, row(even_q_norm_g[i]), row(even_kv_norm_g[i]), f"even_pre_{tag}")
            cw = jnp.pad(sc_conv_w[i], ((0, KP3 - SC_KERNEL), (0, 0)))
            cv = _conv_fwd(u, cw, row(scb[i]), SC_KERNEL, f"conv_{tag}")
            q, k, v, kT3, vT3 = _qkv_fwd_t(qn, kvn, z, tabs, e_w_q_k[i], e_w_kv_k[i], f"qkv_{tag}")
            o, lse = _attn_fwd_t(q, k, vT3, f"attn_{tag}")
            ycat = _even_post(z, cv, o, f"even_post_{tag}")
            y = _matmul(ycat, e_w_out[i], "nn", F32, f"w_out_{tag}", tn=1024)
            saved.append(dict(x=xs, h=h, z=z, u=u, qn=qn, kvn=kvn, cw=cw, cv=cv, q=q, k=k, v=v, kT3=kT3, o=o, lse=lse,
                              ycat=ycat, y=y))
        else:
            o_w_in[i], o_w_out[i] = arrived
            z = _matmul(h, o_w_in[i], "nn", F32, f"w_in_{tag}", tn=1024)
            u = _odd_pre(z, f"odd_pre_{tag}")
            cw = jnp.pad(o_conv_w[i], ((0, KP31 - CONF_KERNEL), (0, 0)))
            cv = _conv_fwd(u, cw, row(o_conv_b[i]), CONF_KERNEL, f"conv_{tag}")
            yin = _odd_post(cv, z, row(o_ln_g[i]), row(o_ln_b[i]), f"odd_post_{tag}")
            y = _matmul(yin, o_w_out[i], "nn", F32, f"w_out_{tag}", tn=1024)
            saved.append(dict(x=xs, h=h, z=z, u=u, cw=cw, cv=cv, yin=yin, y=y))
        xs = _post_norm(xs, y, row(post_norm_g[layer]), row(gate[layer]), f"post_norm_{tag}")

    loss_row, dx = _loss_head(xs, target)
    loss = lax.psum(loss_row[0, 0], MESH_AXES)

    g_pre, g_post, dmod = [None] * L, [None] * L, [None] * L
    g_e_w_in, g_e_w_uq, g_e_w_ukv, g_e_w_out = [None] * NE, [None] * NE, [None] * NE, [None] * NE
    g_scw, g_scb, g_qg, g_kvg = [None] * NE, [None] * NE, [None] * NE, [None] * NE
    g_o_w_in, g_o_w_out, g_ocw, g_ocb, g_olg, g_olb = ([None] * NO for _ in range(6))
    sm_w = [even_sc_conv_w, odd_conv_w, odd_conv_b, odd_ln_g, odd_ln_b]
    sm_rows = _pack(sm_w, F32, SUBLANES).shape[0]

    def small_slab():
        full = [_scatter_cols(jnp.stack(g_scw), even_sc_conv_w.shape[-1]),
                _scatter_cols(jnp.stack(g_ocw), odd_conv_w.shape[-1]),
                _scatter_cols(jnp.concatenate(g_ocb, 0), odd_conv_b.shape[-1]),
                _scatter_cols(jnp.concatenate(g_olg, 0), odd_ln_g.shape[-1]),
                _scatter_cols(jnp.concatenate(g_olb, 0), odd_ln_b.shape[-1])]
        flat = jnp.concatenate([g.reshape(N_DEV, -1) for g in full], axis=1)
        return jnp.pad(flat, ((0, 0), (0, sm_rows * PACK_COLS - flat.shape[1]))).reshape(N_DEV, sm_rows, PACK_COLS)

    s_handles = [None] * L
    bw_token = jnp.zeros((SUBLANES, LANES), F32)
    for layer in reversed(range(L)):
        i = layer // 2
        tag = f"l{layer}"
        sv = saved[layer]
        dy, dgate, g_post[layer] = _post_norm_bwd(dx, sv["y"], row(post_norm_g[layer]) + bw_token[0, 0],
                                                  row(gate[layer]), f"post_norm_bwd_{tag}")
        if layer % 2 == 0:
            dyc = _matmul(dy, e_w_out[i], "nt", F32, f"d_ycat_{tag}", tn=1024)
            g_e_w_out[i] = _matmul(sv["ycat"], dy, "tn", F32, f"g_w_out_{tag}", tk=512)
            dab, dag, dbg, dcv, do, delta = _even_bwd_gates(dyc, sv["z"], sv["cv"], sv["o"], f"even_gates_bwd_{tag}")
            du, dcw, g_scb[i] = _conv_bwd(dcv, sv["u"], sv["cw"], SC_KERNEL, f"conv_bwd_{tag}")
            g_scw[i] = dcw[:SC_KERNEL]
            dq, dk, dv = _attn_bwd_t(sv["q"], sv["k"], sv["v"], sv["kT3"], do, sv["lse"], delta, f"attn_bwd_{tag}")
            (dqp, dkvp, dcq, dckv, dkr, g_qg[i], g_kvg[i]) = _qkv_bwd(
                dq, dk, dv, sv["z"], tabs, e_w_q_k[i], e_w_kv_k[i],
                row(even_q_norm_g[i]), row(even_kv_norm_g[i]), f"qkv_bwd_{tag}")
            gq = _matmul(sv["qn"], dqp, "tn", F32, f"g_w_uq_{tag}", tk=512)
            gkv = _matmul(sv["kvn"], dkvp, "tn", F32, f"g_w_ukv_{tag}", tk=512)
            g_e_w_uq[i] = jnp.moveaxis(gq.reshape(QL, HEADS, HEAD_PAD)[..., :QK_NOPE + QK_ROPE], 1, 0)
            g_e_w_ukv[i] = jnp.moveaxis(jnp.concatenate(
                [gkv[:, :HEADS * HEAD_PAD].reshape(KVL, HEADS, HEAD_PAD)[..., :QK_NOPE],
                 gkv[:, HEADS * HEAD_PAD:].reshape(KVL, HEADS, V_HEAD)], axis=-1), 1, 0)
            g_e_w_out[i] = g_e_w_out[i].reshape(N_DEV, -1, D)
            dz = _even_dz(dab, du, sv["z"], dag, dbg, dcq, dckv, dkr, f"even_dz_{tag}")
            gt = _matmul(dz, sv["h"], "tn", F32, f"g_w_in_{tag}", tk=512, tn=1024)
            g_e_w_in[i] = jnp.concatenate([gt[:2048], gt[2560:2944], gt[2944 + QK_NOPE:2944 + QK_NOPE + QK_ROPE],
                                           gt[2048:2560]], axis=0).reshape(N_DEV, -1, D)
            dh = _matmul(dz, e_w_in_k[i], "nn", F32, f"d_h_{tag}", tn=1024)
        else:
            dyi = _matmul(dy, o_w_out[i], "nt", F32, f"d_yin_{tag}", tn=1024)
            g_o_w_out[i] = _matmul(sv["yin"], dy, "tn", F32, f"g_w_out_{tag}", tk=512).reshape(N_DEV, -1, D)
            dcv, dsg, g_olg[i], g_olb[i] = _odd_bwd_norm(dyi, sv["cv"], sv["z"], row(o_ln_g[i]), row(o_ln_b[i]),
                                                         f"odd_norm_bwd_{tag}")
            du, dcw, g_ocb[i] = _conv_bwd(dcv, sv["u"], sv["cw"], CONF_KERNEL, f"conv_bwd_{tag}")
            g_ocw[i] = dcw[:CONF_KERNEL]
            dz = _odd_dz(du, sv["z"], dsg, f"odd_dz_{tag}")
            g_o_w_in[i] = jnp.moveaxis(_matmul(sv["h"], dz, "tn", F32, f"g_w_in_{tag}", tk=512, tn=1024)
                                       .reshape(D, N_DEV, -1), 1, 0)
            dh = _matmul(dz, o_w_in[i], "nt", F32, f"d_h_{tag}", tn=1024)
        dx, dshift, dscale, g_pre[layer] = _pre_norm_bwd(dh, sv["x"], dx, row(pre_norm_g[layer]), row(scale[layer]),
                                                         f"pre_norm_bwd_{tag}")
        dmod[layer] = jnp.concatenate([dshift, dscale, dgate], axis=-1)
        parts = ([g_e_w_in[i], g_e_w_uq[i], g_e_w_ukv[i], g_e_w_out[i]] if layer % 2 == 0
                 else [g_o_w_in[i], g_o_w_out[i]])
        if layer == 0:
            parts.append(small_slab())
        s_handles[layer], bw_token = _exchange_start(parts, True, f"scatter_grads_start_{tag}")
    grad_x = dx.reshape(1, S, D)

    rep_g = [jnp.concatenate(dmod, 0), jnp.concatenate(g_pre, 0), jnp.concatenate(g_post, 0),
             jnp.stack(g_scb), jnp.stack(g_qg), jnp.stack(g_kvg)]
    rep_w = [ada_b, pre_norm_g, post_norm_g, even_sc_conv_b, even_q_norm_g, even_kv_norm_g]
    rep_m = [m_ada_b, m_pre_norm_g, m_post_norm_g, m_even_sc_conv_b, m_even_q_norm_g, m_even_kv_norm_g]
    rep_v = [v_ada_b, v_pre_norm_g, v_post_norm_g, v_even_sc_conv_b, v_even_q_norm_g, v_even_kv_norm_g]
    rep_shapes = [w.shape for w in rep_w]
    rep_all = _exchange([_pack(rep_g, F32, SUBLANES)], False, "gather_small_grads")[0]
    rep_out = _adamw(rep_all, _pack(rep_w, F32, SUBLANES), _pack(rep_m, F32, SUBLANES), _pack(rep_v, F32, SUBLANES),
                     "adamw_replicated")
    rep_res = [_unpack(o.reshape(-1), rep_shapes) for o in rep_out]

    dmod_all = rep_all.reshape(N_DEV, -1)[:, :L * 3 * D].reshape(N_DEV, L, 3 * D)
    dmod_cols = jnp.moveaxis(lax.dynamic_slice_in_dim(dmod_all, me * n_ada, n_ada, axis=2), 0, 1)
    g_ada_w = _ada_bwd(c_all.T, dmod_cols)
    ada_out = _adamw(g_ada_w.reshape(1, -1, PACK_COLS), ada_w.reshape(-1, PACK_COLS),
                     m_ada_w.reshape(-1, PACK_COLS), v_ada_w.reshape(-1, PACK_COLS), "adamw_ada_w")
    ada_res = [o.reshape(ada_w.shape) for o in ada_out]

    sm_m = [m_even_sc_conv_w, m_odd_conv_w, m_odd_conv_b, m_odd_ln_g, m_odd_ln_b]
    sm_v = [v_even_sc_conv_w, v_odd_conv_w, v_odd_conv_b, v_odd_ln_g, v_odd_ln_b]
    sm_shapes = [w.shape for w in sm_w]
    state = {"even_w_in": (even_w_in, m_even_w_in, v_even_w_in), "even_w_uq": (even_w_uq, m_even_w_uq, v_even_w_uq),
             "even_w_ukv": (even_w_ukv, m_even_w_ukv, v_even_w_ukv), "even_w_out": (even_w_out, m_even_w_out, v_even_w_out),
             "odd_w_in": (odd_w_in, m_odd_w_in, v_odd_w_in), "odd_w_out": (odd_w_out, m_odd_w_out, v_odd_w_out)}
    big_res = {name: [[None] * len(state[name][0]) for _ in range(4)] for name in state}
    after = bw_token
    sm_res = None
    for layer in reversed(range(L)):
        i = layer // 2
        names = ["even_w_in", "even_w_uq", "even_w_ukv", "even_w_out"] if layer % 2 == 0 else ["odd_w_in", "odd_w_out"]
        sent, landed = _exchange_wait(s_handles[layer], True, after, f"scatter_grads_wait_l{layer}")
        own = [lax.dynamic_slice_in_dim(s, me, 1, axis=0) for s in sent]
        for a, name in enumerate(names):
            transposed = name == "even_w_in"
            wmv = [t[i].T if transposed else t[i] for t in state[name]]
            res = _adamw([own[a], landed[a]], *wmv, f"adamw_{name}_{i}")
            for kind in range(4):
                big_res[name][kind][i] = res[kind].T if transposed else res[kind]
            after = res[0]
        if layer == 0:
            sm_out = _adamw([own[-1], landed[-1]], _pack(sm_w, F32, SUBLANES), _pack(sm_m, F32, SUBLANES),
                            _pack(sm_v, F32, SUBLANES), "adamw_small_sharded")
            sm_res = [_unpack(o.reshape(-1), sm_shapes) for o in sm_out]
    sh_res = [dict(zip(["even_sc_conv_w", "odd_conv_w", "odd_conv_b", "odd_ln_g", "odd_ln_b"], sm_res[kind]))
              for kind in range(4)]
    for name in state:
        for kind in range(4):
            sh_res[kind][name] = jnp.stack(big_res[name][kind])

    order = ["ada_w", "ada_b", "pre_norm_g", "post_norm_g", "even_w_in", "even_sc_conv_w", "even_sc_conv_b",
             "even_q_norm_g", "even_kv_norm_g", "even_w_uq", "even_w_ukv", "even_w_out", "odd_w_in", "odd_conv_w",
             "odd_conv_b", "odd_ln_g", "odd_ln_b", "odd_w_out"]
    rep_names = ["ada_b", "pre_norm_g", "post_norm_g", "even_sc_conv_b", "even_q_norm_g", "even_kv_norm_g"]
    outs = [loss, grad_x]
    for kind in range(4):
        for name in order:
            if name == "ada_w":
                outs.append(ada_res[kind])
            elif name in rep_names:
                outs.append(rep_res[kind][rep_names.index(name)])
            else:
                outs.append(sh_res[kind][name])
    return tuple(outs)
```

```python
import functools
import math

import jax
import jax.numpy as jnp
from jax import lax
from jax.experimental import pallas as pl
from jax.experimental.pallas import tpu as pltpu

F32 = jnp.float32
BF16 = jnp.bfloat16
MESH_AXES = ("x", "y", "c")
N_DEV = 8
EPS = 1e-6
CHUNK = 64
HEADS = 8
QK_NOPE = 64
QK_ROPE = 32
V_HEAD = 64
HEAD_PAD = 128
ROPE_THETA = 10000.0
SC_KERNEL = 3
CONF_KERNEL = 31
LANES = 128
SUBLANES = 8
PACK_COLS = 1024
VMEM_LIMIT = 48 * 1024 * 1024
NEG = -1e30

ADAM_LR = 0.001
ADAM_B1 = 0.9
ADAM_B2 = 0.999
ADAM_EPS = 1e-08
ADAM_WD = 0.01
ADAM_STEP = 10


def _cparams():
    return pltpu.CompilerParams(vmem_limit_bytes=VMEM_LIMIT)


def _sigmoid(x):
    return 1.0 / (1.0 + jnp.exp(-x))


def _silu(x):
    return x * _sigmoid(x)


def _dsilu(x):
    s = _sigmoid(x)
    return s * (1.0 + x * (1.0 - s))


def _rows(T, width, cb=0):
    return pl.BlockSpec((T, width), lambda i: (i, cb))


def _const(shape):
    nd = len(shape)
    return pl.BlockSpec(shape, lambda i: (0,) * nd)


def _row_tile(S):
    return min(256, S)


def _exchange(srcs, scatter, name):
    n = len(srcs)
    shapes = [tuple(s.shape[1:]) if scatter else tuple(s.shape) for s in srcs]

    def body(*refs):
        src_refs, out_refs = refs[:n], refs[n:2 * n]
        send_sems, recv_sems, local_sems = refs[2 * n:]
        x, y, c = lax.axis_index("x"), lax.axis_index("y"), lax.axis_index("c")
        me = 4 * x + 2 * y + c
        owns, copies = [], []
        for a in range(n):
            def piece(d, a=a):
                return src_refs[a].at[d] if scatter else src_refs[a]

            own = pltpu.make_async_copy(piece(me), out_refs[a].at[me], local_sems.at[a])
            own.start()
            owns.append(own)
            for k in range(1, N_DEV):
                px, py, pc = x ^ ((k >> 2) & 1), y ^ ((k >> 1) & 1), c ^ (k & 1)
                peer = 4 * px + 2 * py + pc
                sem = a * (N_DEV - 1) + k - 1
                cp = pltpu.make_async_remote_copy(
                    src_ref=piece(peer), dst_ref=out_refs[a].at[me],
                    send_sem=send_sems.at[sem], recv_sem=recv_sems.at[sem],
                    device_id=(px, py, pc), device_id_type=pl.DeviceIdType.MESH)
                cp.start()
                arrival = pltpu.make_async_remote_copy(
                    src_ref=piece(peer), dst_ref=out_refs[a].at[peer],
                    send_sem=send_sems.at[sem], recv_sem=recv_sems.at[sem],
                    device_id=(x, y, c), device_id_type=pl.DeviceIdType.MESH)
                copies.append((cp, arrival))
        for _, arrival in copies:
            arrival.wait_recv()
        for cp, _ in copies:
            cp.wait_send()
        for own in owns:
            own.wait()

    return pl.pallas_call(
        body, name=name,
        out_shape=tuple(jax.ShapeDtypeStruct((N_DEV,) + shp, s.dtype) for shp, s in zip(shapes, srcs)),
        in_specs=[pl.BlockSpec(memory_space=pl.ANY)] * n,
        out_specs=tuple(pl.BlockSpec(memory_space=pl.ANY) for _ in range(n)),
        scratch_shapes=[pltpu.SemaphoreType.DMA((n * (N_DEV - 1),)),
                        pltpu.SemaphoreType.DMA((n * (N_DEV - 1),)),
                        pltpu.SemaphoreType.DMA((n,))],
    )(*srcs)


_HBM = pl.BlockSpec(memory_space=pltpu.HBM)
_SEM = pl.BlockSpec(memory_space=pltpu.SEMAPHORE)


def _peer(k):
    x, y, c = lax.axis_index("x"), lax.axis_index("y"), lax.axis_index("c")
    return x ^ ((k >> 2) & 1), y ^ ((k >> 1) & 1), c ^ (k & 1)


def _exchange_start(srcs, scatter, name):
    n = len(srcs)
    shapes = [tuple(s.shape[1:]) if scatter else tuple(s.shape) for s in srcs]
    slots = N_DEV - 1 if scatter else N_DEV
    lands = [lax.empty((slots,) + shp, s.dtype) for shp, s in zip(shapes, srcs)]

    def body(*refs):
        src_refs, land_refs = refs[:n], refs[n:2 * n]
        send_sems, recv_sems = refs[2 * n], refs[2 * n + 1]
        token = refs[4 * n + 2]
        me = 4 * lax.axis_index("x") + 2 * lax.axis_index("y") + lax.axis_index("c")
        owns = []
        if not scatter:
            owns = [pltpu.make_async_copy(src_refs[a], land_refs[a].at[me], refs[4 * n + 3].at[a]) for a in range(n)]
            for own in owns:
                own.start()
        for a in range(n):
            for k in range(1, N_DEV):
                px, py, pc = _peer(k)
                peer = 4 * px + 2 * py + pc
                pltpu.make_async_remote_copy(
                    src_ref=src_refs[a].at[peer] if scatter else src_refs[a],
                    dst_ref=land_refs[a].at[k - 1] if scatter else land_refs[a].at[me],
                    send_sem=send_sems.at[a * (N_DEV - 1) + k - 1], recv_sem=recv_sems.at[a * (N_DEV - 1) + k - 1],
                    device_id=(px, py, pc), device_id_type=pl.DeviceIdType.MESH).start()
        for own in owns:
            own.wait()
        token[...] = jnp.zeros_like(token)

    hbm = lambda arrs: [pltpu.HBM(a.shape, a.dtype) for a in arrs]
    out = pl.pallas_call(
        body, name=name,
        out_shape=(pltpu.SemaphoreType.DMA((n * (N_DEV - 1),)), pltpu.SemaphoreType.DMA((n * (N_DEV - 1),)),
                   *hbm(srcs), *hbm(lands), jax.ShapeDtypeStruct((SUBLANES, LANES), F32)),
        in_specs=[_HBM] * (2 * n),
        out_specs=(_SEM, _SEM, *([_HBM] * (2 * n)), pl.BlockSpec(memory_space=pltpu.VMEM)),
        input_output_aliases={a: 2 + a for a in range(2 * n)},
        scratch_shapes=[] if scatter else [pltpu.SemaphoreType.DMA((n,))],
        compiler_params=pltpu.CompilerParams(has_side_effects=pltpu.SideEffectType.DATAFLOW_SIDE_EFFECTING),
    )(*[pltpu.with_memory_space_constraint(s, pltpu.HBM) for s in srcs],
      *[pltpu.with_memory_space_constraint(l, pltpu.HBM) for l in lands])
    return (out[0], out[1], list(out[2:2 + n]), list(out[2 + n:2 + 2 * n])), out[2 + 2 * n]


def _exchange_wait(handle, scatter, after, name):
    send_sems, recv_sems, srcs, lands = handle
    n = len(srcs)

    def body(*refs):
        src_refs, land_refs = refs[:n], refs[n:2 * n]
        send_sems, recv_sems = refs[2 * n], refs[2 * n + 1]
        for a in range(n):
            for k in range(1, N_DEV):
                px, py, pc = _peer(k)
                peer = 4 * px + 2 * py + pc
                cp = pltpu.make_async_remote_copy(
                    src_ref=src_refs[a].at[peer] if scatter else src_refs[a],
                    dst_ref=land_refs[a].at[k - 1] if scatter else land_refs[a].at[peer],
                    send_sem=send_sems.at[a * (N_DEV - 1) + k - 1], recv_sem=recv_sems.at[a * (N_DEV - 1) + k - 1],
                    device_id=(px, py, pc), device_id_type=pl.DeviceIdType.MESH)
                cp.wait_send()
                cp.wait_recv()

    out = pl.pallas_call(
        body, name=name,
        out_shape=tuple(pltpu.HBM(a.shape, a.dtype) for a in srcs + lands),
        in_specs=[_HBM] * (2 * n) + [_SEM, _SEM, pl.BlockSpec(memory_space=pl.ANY)],
        out_specs=tuple([_HBM] * (2 * n)),
        input_output_aliases={a: a for a in range(2 * n)},
        compiler_params=pltpu.CompilerParams(has_side_effects=pltpu.SideEffectType.DATAFLOW_SIDE_EFFECTING),
    )(*srcs, *lands, send_sems, recv_sems, after)
    return list(out[:n]), list(out[n:])


def _pack(parts, dtype, row_mult):
    flat = jnp.concatenate([p.reshape(-1).astype(dtype) for p in parts])
    n = flat.shape[0]
    rows = -(-n // PACK_COLS)
    rows = -(-rows // row_mult) * row_mult
    flat = jnp.pad(flat, (0, rows * PACK_COLS - n))
    return flat.reshape(rows, PACK_COLS)


def _unpack(flat, shapes):
    out, off = [], 0
    for shp in shapes:
        n = math.prod(shp)
        out.append(flat[..., off:off + n].reshape(flat.shape[:-1] + tuple(shp)))
        off += n
    return out


_DIMS = {"nn": (((1,), (0,)), ((), ())), "nt": (((1,), (1,)), ((), ())), "tn": (((0,), (0,)), ((), ()))}


def _matmul(a, b, mode, out_dtype, name, tm=512, tn=512, tk=None):
    if mode == "nn":
        (M, K), (_, N) = a.shape, b.shape
    elif mode == "nt":
        (M, K), (N, _) = a.shape, b.shape
    else:
        (K, M), (_, N) = a.shape, b.shape
    tm, tn = min(tm, M), min(tn, N)
    tk = K if tk is None else min(tk, K)
    nk = K // tk
    assert M % tm == 0 and N % tn == 0 and K % tk == 0, (name, a.shape, b.shape)

    def body(a_ref, b_ref, o_ref, *scratch):
        p = lax.dot_general(a_ref[...].astype(BF16), b_ref[...].astype(BF16), _DIMS[mode],
                            preferred_element_type=F32)
        if nk == 1:
            o_ref[...] = p.astype(out_dtype)
        else:
            acc = scratch[0]
            k = pl.program_id(2)

            @pl.when(k == 0)
            def _():
                acc[...] = p

            @pl.when(k > 0)
            def _():
                acc[...] += p

            @pl.when(k == nk - 1)
            def _():
                o_ref[...] = acc[...].astype(out_dtype)

    a_spec = (pl.BlockSpec((tk, tm), lambda i, j, k: (k, i)) if mode == "tn"
              else pl.BlockSpec((tm, tk), lambda i, j, k: (i, k)))
    b_spec = (pl.BlockSpec((tn, tk), lambda i, j, k: (j, k)) if mode == "nt"
              else pl.BlockSpec((tk, tn), lambda i, j, k: (k, j)))
    return pl.pallas_call(
        body, name=name, grid=(M // tm, N // tn, nk),
        out_shape=jax.ShapeDtypeStruct((M, N), out_dtype),
        in_specs=[a_spec, b_spec],
        out_specs=pl.BlockSpec((tm, tn), lambda i, j, k: (i, j)),
        scratch_shapes=[pltpu.VMEM((tm, tn), F32)] if nk > 1 else [],
        compiler_params=_cparams(),
    )(a, b)


def _ada_fwd(c_all, ada_w, ada_b_cols):
    L, D, n = ada_w.shape

    def body(c_ref, w_ref, b_ref, o_ref):
        act = _silu(c_ref[...]).astype(BF16)
        o_ref[0] = jnp.dot(act, w_ref[0].astype(BF16), preferred_element_type=F32) + b_ref[0]

    return pl.pallas_call(
        body, name="ada_fwd", grid=(L,),
        out_shape=jax.ShapeDtypeStruct((L, N_DEV, n), F32),
        in_specs=[pl.BlockSpec((N_DEV, D), lambda l: (0, 0)),
                  pl.BlockSpec((1, D, n), lambda l: (l, 0, 0)),
                  pl.BlockSpec((1, 1, n), lambda l: (l, 0, 0))],
        out_specs=pl.BlockSpec((1, N_DEV, n), lambda l: (l, 0, 0)),
        compiler_params=_cparams(),
    )(c_all, ada_w, ada_b_cols)


def _ada_bwd(c_all_t, dmod_cols):
    D = c_all_t.shape[0]
    L, _, n = dmod_cols.shape

    def body(c_ref, d_ref, o_ref):
        act = _silu(c_ref[...])
        dm = d_ref[0]
        acc = act[:, 0:1] * dm[0:1, :]
        for b in range(1, N_DEV):
            acc = acc + act[:, b:b + 1] * dm[b:b + 1, :]
        o_ref[0] = acc

    return pl.pallas_call(
        body, name="ada_bwd", grid=(L,),
        out_shape=jax.ShapeDtypeStruct((L, D, n), F32),
        in_specs=[pl.BlockSpec((D, N_DEV), lambda l: (0, 0)),
                  pl.BlockSpec((1, N_DEV, n), lambda l: (l, 0, 0))],
        out_specs=pl.BlockSpec((1, D, n), lambda l: (l, 0, 0)),
        compiler_params=_cparams(),
    )(c_all_t, dmod_cols)


def _rope_tables(pos_col, inv_lane):
    S = pos_col.shape[0]
    T = _row_tile(S)
    half = QK_ROPE // 2

    def body(p_ref, f_ref, c_ref, up_ref, dn_ref):
        ang = p_ref[...] * f_ref[...]
        lane = lax.broadcasted_iota(jnp.int32, ang.shape, 1)
        first = (lane >= QK_NOPE) & (lane < QK_NOPE + half)
        second = (lane >= QK_NOPE + half) & (lane < QK_NOPE + QK_ROPE)
        cs, sn = jnp.cos(ang), jnp.sin(ang)
        c_ref[...] = jnp.where(first | second, cs, 1.0)
        up_ref[...] = jnp.where(first, -sn, 0.0)
        dn_ref[...] = jnp.where(second, sn, 0.0)

    tab = jax.ShapeDtypeStruct((S, HEAD_PAD), F32)
    return pl.pallas_call(
        body, name="rope_tables", grid=(S // T,),
        out_shape=(tab, tab, tab),
        in_specs=[_rows(T, 1), _const((1, HEAD_PAD))],
        out_specs=(_rows(T, HEAD_PAD),) * 3,
        compiler_params=_cparams(),
    )(pos_col, inv_lane)


def _rope(blk, ct, ut, dt):
    half = QK_ROPE // 2
    up = pltpu.roll(blk, HEAD_PAD - half, 1)
    dn = pltpu.roll(blk, half, 1)
    return blk * ct + up * ut + dn * dt


def _rope_t(d, ct, ut, dt):
    half = QK_ROPE // 2
    return d * ct + pltpu.roll(d * ut, half, 1) + pltpu.roll(d * dt, HEAD_PAD - half, 1)


def _pre_norm(x, g, scale, shift, name):
    S, D = x.shape
    T = _row_tile(S)

    def body(x_ref, g_ref, sc_ref, sh_ref, h_ref):
        xv = x_ref[...]
        rstd = lax.rsqrt(jnp.mean(xv * xv, axis=-1, keepdims=True) + EPS)
        h_ref[...] = ((xv * rstd) * g_ref[...] * (1.0 + sc_ref[...]) + sh_ref[...]).astype(BF16)

    return pl.pallas_call(
        body, name=name, grid=(S // T,),
        out_shape=jax.ShapeDtypeStruct((S, D), BF16),
        in_specs=[_rows(T, D), _const((1, D)), _const((1, D)), _const((1, D))],
        out_specs=_rows(T, D), compiler_params=_cparams(),
    )(x, g, scale, shift)


def _post_norm(x, y, g, gate, name):
    S, D = x.shape
    T = _row_tile(S)

    def body(x_ref, y_ref, g_ref, gt_ref, o_ref):
        yv = y_ref[...]
        rstd = lax.rsqrt(jnp.mean(yv * yv, axis=-1, keepdims=True) + EPS)
        o_ref[...] = x_ref[...] + gt_ref[...] * ((yv * rstd) * g_ref[...])

    return pl.pallas_call(
        body, name=name, grid=(S // T,),
        out_shape=jax.ShapeDtypeStruct((S, D), F32),
        in_specs=[_rows(T, D), _rows(T, D), _const((1, D)), _const((1, D))],
        out_specs=_rows(T, D), compiler_params=_cparams(),
    )(x, y, g, gate)


def _fold8(v):
    T, C = v.shape
    return v.reshape(T // SUBLANES, SUBLANES, C).sum(axis=0)


def _col_sums(n_sums, body_fn, ins, in_specs, outs, out_specs, S, T, widths, name):
    n_in, n_out = len(ins), len(outs)
    nt = S // T

    def body(*refs):
        in_refs = refs[:n_in]
        out_refs = refs[n_in:n_in + n_out]
        sum_refs = refs[n_in + n_out:n_in + n_out + n_sums]
        accs = refs[n_in + n_out + n_sums:]
        i = pl.program_id(0)
        terms = body_fn(in_refs, out_refs)

        @pl.when(i == 0)
        def _():
            for acc, t in zip(accs, terms):
                acc[...] = _fold8(t)

        @pl.when(i > 0)
        def _():
            for acc, t in zip(accs, terms):
                acc[...] += _fold8(t)

        @pl.when(i == nt - 1)
        def _():
            for acc, s_ref in zip(accs, sum_refs):
                s_ref[...] = jnp.sum(acc[...], axis=0, keepdims=True)

    return pl.pallas_call(
        body, name=name, grid=(nt,),
        out_shape=tuple(outs) + tuple(jax.ShapeDtypeStruct((1, w), F32) for w in widths),
        in_specs=in_specs,
        out_specs=tuple(out_specs) + tuple(_const((1, w)) for w in widths),
        scratch_shapes=[pltpu.VMEM((SUBLANES, w), F32) for w in widths],
        compiler_params=_cparams(),
    )(*ins)


def _post_norm_bwd(dxo, y, g, gate, name):
    S, D = y.shape
    T = _row_tile(S)

    def fn(ins, outs):
        dxo_ref, y_ref, g_ref, gt_ref = ins
        yv, dv = y_ref[...], dxo_ref[...]
        rstd = lax.rsqrt(jnp.mean(yv * yv, axis=-1, keepdims=True) + EPS)
        yh = yv * rstd
        dn = dv * gt_ref[...]
        dyh = dn * g_ref[...]
        outs[0][...] = (rstd * (dyh - yh * jnp.mean(dyh * yh, axis=-1, keepdims=True))).astype(BF16)
        return [dv * (yh * g_ref[...]), dn * yh]

    return _col_sums(2, fn, [dxo, y, g, gate],
                     [_rows(T, D), _rows(T, D), _const((1, D)), _const((1, D))],
                     [jax.ShapeDtypeStruct((S, D), BF16)], [_rows(T, D)], S, T, [D, D], name)


def _pre_norm_bwd(dh, x, dxo, g, scale, name):
    S, D = x.shape
    T = _row_tile(S)

    def fn(ins, outs):
        dh_ref, x_ref, dxo_ref, g_ref, sc_ref = ins
        xv, dv = x_ref[...], dh_ref[...]
        rstd = lax.rsqrt(jnp.mean(xv * xv, axis=-1, keepdims=True) + EPS)
        xh = xv * rstd
        dr = dv * (1.0 + sc_ref[...])
        dxh = dr * g_ref[...]
        outs[0][...] = dxo_ref[...] + rstd * (dxh - xh * jnp.mean(dxh * xh, axis=-1, keepdims=True))
        return [dv, dv * (xh * g_ref[...]), dr * xh]

    return _col_sums(3, fn, [dh, x, dxo, g, scale],
                     [_rows(T, D), _rows(T, D), _rows(T, D), _const((1, D)), _const((1, D))],
                     [jax.ShapeDtypeStruct((S, D), F32)], [_rows(T, D)], S, T, [D, D, D], name)


def _loss_head(x, target):
    S, D = x.shape
    T = _row_tile(S)
    nt = S // T

    def body(x_ref, t_ref, l_ref, dx_ref, acc):
        i = pl.program_id(0)
        e = x_ref[...] - t_ref[...]
        dx_ref[...] = e * (1.0 / D)
        part = _fold8(e * e)

        @pl.when(i == 0)
        def _():
            acc[...] = part

        @pl.when(i > 0)
        def _():
            acc[...] += part

        @pl.when(i == nt - 1)
        def _():
            tot = jnp.sum(jnp.sum(acc[...], axis=0, keepdims=True), axis=1, keepdims=True)
            l_ref[...] = jnp.broadcast_to(tot * (0.5 / D), (1, LANES))

    return pl.pallas_call(
        body, name="loss_head", grid=(nt,),
        out_shape=(jax.ShapeDtypeStruct((1, LANES), F32), jax.ShapeDtypeStruct((S, D), F32)),
        in_specs=[_rows(T, D), _rows(T, D)],
        out_specs=(_const((1, LANES)), _rows(T, D)),
        scratch_shapes=[pltpu.VMEM((SUBLANES, D), F32)],
        compiler_params=_cparams(),
    )(x, target)


CONV_ROWS = 64


def _conv_halo(K):
    return SUBLANES if K - 1 <= SUBLANES else 32


def _conv_fwd(u, w, b, K, name):
    S, C = u.shape
    KP = w.shape[0]
    T, HB, RS = min(512, S), _conv_halo(K), CONV_ROWS
    ratio = T // HB

    def body(u_ref, h_ref, w_ref, b_ref, o_ref, ext):
        i = pl.program_id(1)
        ext[0:HB, :] = jnp.where(i > 0, h_ref[...], 0.0)
        ext[HB:HB + T, :] = u_ref[...]
        for r0 in range(0, T, RS):
            acc = jnp.broadcast_to(b_ref[...], (RS, LANES))
            for k in range(K):
                off = HB - (K - 1) + k + r0
                acc = acc + w_ref[k:k + 1, :] * ext[off:off + RS, :]
            o_ref[r0:r0 + RS, :] = acc

    return pl.pallas_call(
        body, name=name, grid=(C // LANES, S // T),
        out_shape=jax.ShapeDtypeStruct((S, C), F32),
        in_specs=[pl.BlockSpec((T, LANES), lambda c, i: (i, c)),
                  pl.BlockSpec((HB, LANES), lambda c, i: (jnp.maximum(i * ratio - 1, 0), c)),
                  pl.BlockSpec((KP, LANES), lambda c, i: (0, c)),
                  pl.BlockSpec((1, LANES), lambda c, i: (0, c))],
        out_specs=pl.BlockSpec((T, LANES), lambda c, i: (i, c)),
        scratch_shapes=[pltpu.VMEM((HB + T, LANES), F32)],
        compiler_params=_cparams(),
    )(u, u, w, b)


def _conv_bwd(d, u, w, K, name):
    S, C = u.shape
    KP = w.shape[0]
    T, HB, RS = min(512, S), _conv_halo(K), CONV_ROWS
    ratio = T // HB
    nt = S // T
    last_halo = S // HB - 1

    def body(d_ref, dn_ref, u_ref, up_ref, w_ref, du_ref, dw_ref, db_ref, extd, extu, dws, dbs):
        i = pl.program_id(1)
        extd[0:T, :] = d_ref[...]
        extd[T:T + HB, :] = jnp.where(i < nt - 1, dn_ref[...], 0.0)
        extu[0:HB, :] = jnp.where(i > 0, up_ref[...], 0.0)
        extu[HB:HB + T, :] = u_ref[...]

        @pl.when(i == 0)
        def _():
            dws[...] = jnp.zeros_like(dws)
            dbs[...] = jnp.zeros_like(dbs)

        for r0 in range(0, T, RS):
            acc = jnp.zeros((RS, LANES), F32)
            for k in range(K):
                off = (K - 1 - k) + r0
                acc = acc + w_ref[k:k + 1, :] * extd[off:off + RS, :]
            du_ref[r0:r0 + RS, :] = acc
            dch = d_ref[r0:r0 + RS, :]
            dbs[...] += _fold8(dch)
            for k in range(K):
                off = HB - (K - 1) + k + r0
                dws[k * SUBLANES:(k + 1) * SUBLANES, :] += _fold8(dch * extu[off:off + RS, :])

        @pl.when(i == nt - 1)
        def _():
            dw_ref[...] = jnp.zeros_like(dw_ref)
            for k in range(K):
                dw_ref[k:k + 1, :] = jnp.sum(dws[k * SUBLANES:(k + 1) * SUBLANES, :], axis=0, keepdims=True)
            db_ref[...] = jnp.sum(dbs[...], axis=0, keepdims=True)

    return pl.pallas_call(
        body, name=name, grid=(C // LANES, nt),
        out_shape=(jax.ShapeDtypeStruct((S, C), F32), jax.ShapeDtypeStruct((KP, C), F32),
                   jax.ShapeDtypeStruct((1, C), F32)),
        in_specs=[pl.BlockSpec((T, LANES), lambda c, i: (i, c)),
                  pl.BlockSpec((HB, LANES), lambda c, i: (jnp.minimum((i + 1) * ratio, last_halo), c)),
                  pl.BlockSpec((T, LANES), lambda c, i: (i, c)),
                  pl.BlockSpec((HB, LANES), lambda c, i: (jnp.maximum(i * ratio - 1, 0), c)),
                  pl.BlockSpec((KP, LANES), lambda c, i: (0, c))],
        out_specs=(pl.BlockSpec((T, LANES), lambda c, i: (i, c)),
                   pl.BlockSpec((KP, LANES), lambda c, i: (0, c)),
                   pl.BlockSpec((1, LANES), lambda c, i: (0, c))),
        scratch_shapes=[pltpu.VMEM((T + HB, LANES), F32), pltpu.VMEM((HB + T, LANES), F32),
                        pltpu.VMEM((KP * SUBLANES, LANES), F32), pltpu.VMEM((SUBLANES, LANES), F32)],
        compiler_params=_cparams(),
    )(d, d, u, u, w)


SCW = 512
ZE = 3072
QL = 256
KVL = 128


def _rms_rows(x, g):
    rstd = lax.rsqrt(jnp.mean(x * x, axis=-1, keepdims=True) + EPS)
    return (x * rstd) * g


def _even_pre(z, qg, kvg, name):
    S = z.shape[0]
    T = _row_tile(S)

    def body(ac_ref, ax_ref, cq_ref, ckv_ref, qg_ref, kvg_ref, u_ref, qn_ref, kvn_ref):
        u_ref[...] = ac_ref[...] * ax_ref[...]
        qn_ref[...] = _rms_rows(cq_ref[...], qg_ref[...]).astype(BF16)
        kvn_ref[...] = _rms_rows(ckv_ref[...], kvg_ref[...]).astype(BF16)

    return pl.pallas_call(
        body, name=name, grid=(S // T,),
        out_shape=(jax.ShapeDtypeStruct((S, SCW), F32), jax.ShapeDtypeStruct((S, QL), BF16),
                   jax.ShapeDtypeStruct((S, KVL), BF16)),
        in_specs=[_rows(T, SCW, 1), _rows(T, SCW, 2), _rows(T, QL, 10), _rows(T, KVL, 22),
                  _const((1, QL)), _const((1, KVL))],
        out_specs=(_rows(T, SCW), _rows(T, QL), _rows(T, KVL)),
        compiler_params=_cparams(),
    )(z, z, z, z, qg, kvg)


def _qkv_fwd(qn, kvn, z, tabs, w_q, w_kv, name):
    S = qn.shape[0]
    T = _row_tile(S)
    HW = HEADS * HEAD_PAD
    scale = 1.0 / math.sqrt(QK_NOPE + QK_ROPE)

    def body(qn_ref, kvn_ref, kr_ref, ct_ref, ut_ref, dt_ref, wq_ref, wkv_ref, q_ref, k_ref, v_ref):
        ct, ut, dt = ct_ref[...], ut_ref[...], dt_ref[...]
        qa = jnp.dot(qn_ref[...], wq_ref[...], preferred_element_type=F32)
        kva = jnp.dot(kvn_ref[...], wkv_ref[...], preferred_element_type=F32)
        kr = kr_ref[...]
        for h in range(HEADS):
            sl = slice(h * HEAD_PAD, (h + 1) * HEAD_PAD)
            q_ref[:, sl] = (_rope(qa[:, sl], ct, ut, dt) * scale).astype(BF16)
            k_ref[:, sl] = _rope(kva[:, sl] + kr, ct, ut, dt).astype(BF16)
        v_ref[...] = kva[:, HW:].astype(BF16)

    return pl.pallas_call(
        body, name=name, grid=(S // T,),
        out_shape=(jax.ShapeDtypeStruct((S, HW), BF16), jax.ShapeDtypeStruct((S, HW), BF16),
                   jax.ShapeDtypeStruct((S, HEADS * V_HEAD), BF16)),
        in_specs=[_rows(T, QL), _rows(T, KVL), _rows(T, HEAD_PAD, 23),
                  _rows(T, HEAD_PAD), _rows(T, HEAD_PAD), _rows(T, HEAD_PAD),
                  _const(w_q.shape), _const(w_kv.shape)],
        out_specs=(_rows(T, HW), _rows(T, HW), _rows(T, HEADS * V_HEAD)),
        compiler_params=_cparams(),
    )(qn, kvn, z, *tabs, w_q, w_kv)


def _attn_tile(S):
    return min(256, S)


def _chunk_mask(TQ):
    r = lax.broadcasted_iota(jnp.int32, (TQ, TQ), 0) // CHUNK
    c = lax.broadcasted_iota(jnp.int32, (TQ, TQ), 1) // CHUNK
    return c <= r


_NT = (((1,), (1,)), ((), ()))
_TN = (((0,), (0,)), ((), ()))


def _attn_fwd(q, k, v, name):
    S = q.shape[0]
    TQ = _attn_tile(S)
    nq = S // TQ
    PW = 2 * HEAD_PAD

    def body(q_ref, k_ref, v_ref, o_ref, lse_ref, m_s, l_s, acc_s):
        i = pl.program_id(1)
        left = lax.broadcasted_iota(jnp.int32, (TQ, LANES), 1) < V_HEAD
        m_s[...] = jnp.full_like(m_s, NEG)
        l_s[...] = jnp.zeros_like(l_s)
        acc_s[...] = jnp.zeros_like(acc_s)
        qv = q_ref[...]

        def step(j, masked):
            r0 = pl.multiple_of(j * TQ, TQ)
            kb = k_ref[pl.ds(r0, TQ), :]
            vb = v_ref[pl.ds(r0, TQ), :]
            alphas, pvs = [], []
            for h in range(2):
                sl = slice(h * HEAD_PAD, (h + 1) * HEAD_PAD)
                s = lax.dot_general(qv[:, sl], kb[:, sl], _NT, preferred_element_type=F32)
                if masked:
                    s = jnp.where(_chunk_mask(TQ), s, NEG)
                m_prev = m_s[h]
                m_new = jnp.maximum(m_prev, jnp.max(s, axis=1, keepdims=True))
                alpha = jnp.exp(m_prev - m_new)
                p = jnp.exp(s - m_new[:, 0:1])
                l_s[h] = alpha * l_s[h] + jnp.sum(p, axis=1, keepdims=True)
                m_s[h] = m_new
                alphas.append(alpha)
                pvs.append(jnp.dot(p.astype(BF16), vb, preferred_element_type=F32))
            acc_s[...] = acc_s[...] * jnp.where(left, alphas[0], alphas[1]) + jnp.where(left, pvs[0], pvs[1])

        def loop_body(j, carry):
            step(j, False)
            return carry

        lax.fori_loop(0, i, loop_body, 0)
        step(i, True)
        o_ref[...] = acc_s[...] / jnp.where(left, l_s[0], l_s[1])
        lse_ref[...] = jnp.where(left, m_s[0] + jnp.log(l_s[0]), m_s[1] + jnp.log(l_s[1]))

    return pl.pallas_call(
        body, name=name, grid=(HEADS // 2, nq),
        out_shape=(jax.ShapeDtypeStruct((S, HEADS * V_HEAD), F32), jax.ShapeDtypeStruct((S, HEADS * V_HEAD), F32)),
        in_specs=[pl.BlockSpec((TQ, PW), lambda p, i: (i, p)),
                  pl.BlockSpec((S, PW), lambda p, i: (0, p)),
                  pl.BlockSpec((S, LANES), lambda p, i: (0, p))],
        out_specs=(pl.BlockSpec((TQ, LANES), lambda p, i: (i, p)),
                   pl.BlockSpec((TQ, LANES), lambda p, i: (i, p))),
        scratch_shapes=[pltpu.VMEM((2, TQ, LANES), F32), pltpu.VMEM((2, TQ, LANES), F32),
                        pltpu.VMEM((TQ, LANES), F32)],
        compiler_params=_cparams(),
    )(q, k, v)


def _attn_dq(q, k, v, do, lse, delta, name):
    S = q.shape[0]
    TQ = _attn_tile(S)
    nq = S // TQ
    PW = 2 * HEAD_PAD

    def body(q_ref, k_ref, v_ref, do_ref, lse_ref, dl_ref, dq_ref, acc_s):
        i = pl.program_id(1)
        left = lax.broadcasted_iota(jnp.int32, (TQ, LANES), 1) < V_HEAD
        acc_s[...] = jnp.zeros_like(acc_s)
        qv = q_ref[...]
        dov = do_ref[...]
        dos = [jnp.where(left, dov, jnp.zeros_like(dov)), jnp.where(left, jnp.zeros_like(dov), dov)]
        lses = [lse_ref[:, 0:1], lse_ref[:, V_HEAD:V_HEAD + 1]]
        dls = [dl_ref[:, 0:1], dl_ref[:, V_HEAD:V_HEAD + 1]]

        def step(j, masked):
            r0 = pl.multiple_of(j * TQ, TQ)
            kb = k_ref[pl.ds(r0, TQ), :]
            vb = v_ref[pl.ds(r0, TQ), :]
            for h in range(2):
                sl = slice(h * HEAD_PAD, (h + 1) * HEAD_PAD)
                s = lax.dot_general(qv[:, sl], kb[:, sl], _NT, preferred_element_type=F32)
                p = jnp.exp(s - lses[h])
                if masked:
                    p = jnp.where(_chunk_mask(TQ), p, 0.0)
                dp = lax.dot_general(dos[h], vb, _NT, preferred_element_type=F32)
                ds = (p * (dp - dls[h])).astype(BF16)
                acc_s[:, sl] += jnp.dot(ds, kb[:, sl], preferred_element_type=F32)

        def loop_body(j, carry):
            step(j, False)
            return carry

        lax.fori_loop(0, i, loop_body, 0)
        step(i, True)
        dq_ref[...] = acc_s[...]

    return pl.pallas_call(
        body, name=name, grid=(HEADS // 2, nq),
        out_shape=jax.ShapeDtypeStruct((S, HEADS * HEAD_PAD), F32),
        in_specs=[pl.BlockSpec((TQ, PW), lambda p, i: (i, p)),
                  pl.BlockSpec((S, PW), lambda p, i: (0, p)),
                  pl.BlockSpec((S, LANES), lambda p, i: (0, p)),
                  pl.BlockSpec((TQ, LANES), lambda p, i: (i, p)),
                  pl.BlockSpec((TQ, LANES), lambda p, i: (i, p)),
                  pl.BlockSpec((TQ, LANES), lambda p, i: (i, p))],
        out_specs=pl.BlockSpec((TQ, PW), lambda p, i: (i, p)),
        scratch_shapes=[pltpu.VMEM((TQ, PW), F32)],
        compiler_params=_cparams(),
    )(q, k, v, do, lse, delta)


def _attn_dkv(q, k, v, do, lse, delta, name):
    S = q.shape[0]
    TQ = _attn_tile(S)
    nq = S // TQ
    PW = 2 * HEAD_PAD

    def body(q_ref, k_ref, v_ref, do_ref, lse_ref, dl_ref, dk_ref, dv_ref, dk_s, dv_s):
        j = pl.program_id(1)
        left = lax.broadcasted_iota(jnp.int32, (TQ, LANES), 1) < V_HEAD
        dk_s[...] = jnp.zeros_like(dk_s)
        dv_s[...] = jnp.zeros_like(dv_s)
        kb = k_ref[...]
        vb = v_ref[...]

        def step(i, masked):
            r0 = pl.multiple_of(i * TQ, TQ)
            qb = q_ref[pl.ds(r0, TQ), :]
            dov = do_ref[pl.ds(r0, TQ), :]
            lse = lse_ref[pl.ds(r0, TQ), :]
            dl = dl_ref[pl.ds(r0, TQ), :]
            dos = [jnp.where(left, dov, jnp.zeros_like(dov)), jnp.where(left, jnp.zeros_like(dov), dov)]
            for h in range(2):
                sl = slice(h * HEAD_PAD, (h + 1) * HEAD_PAD)
                c0 = h * V_HEAD
                s = lax.dot_general(qb[:, sl], kb[:, sl], _NT, preferred_element_type=F32)
                p = jnp.exp(s - lse[:, c0:c0 + 1])
                if masked:
                    p = jnp.where(_chunk_mask(TQ), p, 0.0)
                dv_s[...] += lax.dot_general(p.astype(BF16), dos[h], _TN, preferred_element_type=F32)
                dp = lax.dot_general(dos[h], vb, _NT, preferred_element_type=F32)
                ds = (p * (dp - dl[:, c0:c0 + 1])).astype(BF16)
                dk_s[:, sl] += lax.dot_general(ds, qb[:, sl], _TN, preferred_element_type=F32)

        def loop_body(i, carry):
            step(i, False)
            return carry

        step(j, True)
        lax.fori_loop(j + 1, nq, loop_body, 0)
        dk_ref[...] = dk_s[...]
        dv_ref[...] = dv_s[...]

    return pl.pallas_call(
        body, name=name, grid=(HEADS // 2, nq),
        out_shape=(jax.ShapeDtypeStruct((S, HEADS * HEAD_PAD), F32), jax.ShapeDtypeStruct((S, HEADS * V_HEAD), F32)),
        in_specs=[pl.BlockSpec((S, PW), lambda p, j: (0, p)),
                  pl.BlockSpec((TQ, PW), lambda p, j: (j, p)),
                  pl.BlockSpec((TQ, LANES), lambda p, j: (j, p)),
                  pl.BlockSpec((S, LANES), lambda p, j: (0, p)),
                  pl.BlockSpec((S, LANES), lambda p, j: (0, p)),
                  pl.BlockSpec((S, LANES), lambda p, j: (0, p))],
        out_specs=(pl.BlockSpec((TQ, PW), lambda p, j: (j, p)),
                   pl.BlockSpec((TQ, LANES), lambda p, j: (j, p))),
        scratch_shapes=[pltpu.VMEM((TQ, PW), F32), pltpu.VMEM((TQ, LANES), F32)],
        compiler_params=_cparams(),
    )(q, k, v, do, lse, delta)


ATTN_FWD_HEADS = 8
ATTN_BWD_HEADS = 4


def _chunk_mask_t(T):
    key = lax.broadcasted_iota(jnp.int32, (T, T), 0) // CHUNK
    qry = lax.broadcasted_iota(jnp.int32, (T, T), 1) // CHUNK
    return key <= qry


def _qkv_fwd_t(qn, kvn, z, tabs, w_q, w_kv, name):
    S = qn.shape[0]
    T = _attn_tile(S)
    HW = HEADS * HEAD_PAD
    scale = 1.0 / math.sqrt(QK_NOPE + QK_ROPE)

    def body(qn_ref, kvn_ref, kr_ref, ct_ref, ut_ref, dt_ref, wq_ref, wkv_ref, q_ref, k_ref, v_ref, kt_ref, vt_ref):
        ct, ut, dt = ct_ref[...], ut_ref[...], dt_ref[...]
        qa = jnp.dot(qn_ref[...], wq_ref[...], preferred_element_type=F32)
        kva = jnp.dot(kvn_ref[...], wkv_ref[...], preferred_element_type=F32)
        kr = kr_ref[...]
        ones_row = (lax.broadcasted_iota(jnp.int32, (V_HEAD, T), 0) == 0).astype(F32)
        for h in range(HEADS):
            sl = slice(h * HEAD_PAD, (h + 1) * HEAD_PAD)
            q_ref[:, sl] = (_rope(qa[:, sl], ct, ut, dt) * scale).astype(BF16)
            kh = _rope(kva[:, sl] + kr, ct, ut, dt)
            k_ref[:, sl] = kh.astype(BF16)
            kt_ref[0, sl, :] = kh.T.astype(BF16)
        v_ref[...] = kva[:, HW:].astype(BF16)
        for p in range(HEADS // 2):
            vpt = kva[:, HW + p * LANES:HW + (p + 1) * LANES].T
            for h in range(2):
                r0 = (2 * p + h) * HEAD_PAD
                vt_ref[0, r0:r0 + V_HEAD, :] = vpt[h * V_HEAD:(h + 1) * V_HEAD, :].astype(BF16)
                vt_ref[0, r0 + V_HEAD:r0 + HEAD_PAD, :] = ones_row.astype(BF16)

    t3 = jax.ShapeDtypeStruct((S // T, HW, T), BF16)
    return pl.pallas_call(
        body, name=name, grid=(S // T,),
        out_shape=(jax.ShapeDtypeStruct((S, HW), BF16), jax.ShapeDtypeStruct((S, HW), BF16),
                   jax.ShapeDtypeStruct((S, HEADS * V_HEAD), BF16), t3, t3),
        in_specs=[_rows(T, QL), _rows(T, KVL), _rows(T, HEAD_PAD, 23),
                  _rows(T, HEAD_PAD), _rows(T, HEAD_PAD), _rows(T, HEAD_PAD),
                  _const(w_q.shape), _const(w_kv.shape)],
        out_specs=(_rows(T, HW), _rows(T, HW), _rows(T, HEADS * V_HEAD),
                   pl.BlockSpec((1, HW, T), lambda i: (i, 0, 0)), pl.BlockSpec((1, HW, T), lambda i: (i, 0, 0))),
        compiler_params=_cparams(),
    )(qn, kvn, z, *tabs, w_q, w_kv)


def _attn_fwd_t(q, k, vT3, name):
    S = q.shape[0]
    T = _attn_tile(S)
    nq = S // T
    NH = ATTN_FWD_HEADS
    PW = NH * HEAD_PAD

    def body(q_ref, k_ref, vt_ref, o_ref, lse_ref, m_s, acc_s):
        i = pl.program_id(1)
        m_s[...] = jnp.full_like(m_s, NEG)
        acc_s[...] = jnp.zeros_like(acc_s)
        qv = q_ref[...]

        def step(j, masked):
            kb = k_ref[pl.ds(pl.multiple_of(j * T, T), T), :]
            vt = vt_ref[j]
            heads = [slice(h * HEAD_PAD, (h + 1) * HEAD_PAD) for h in range(NH)]
            sts = [lax.dot_general(kb[:, sl], qv[:, sl], _NT, preferred_element_type=F32) for sl in heads]
            alphas, pvs = [], []
            for h, sl in enumerate(heads):
                st = jnp.where(_chunk_mask_t(T), sts[h], NEG) if masked else sts[h]
                m_prev = m_s[h]
                m_new = jnp.maximum(m_prev, jnp.max(st, axis=0, keepdims=True))
                alphas.append(jnp.exp(m_prev[0:1] - m_new[0:1]))
                pt = jnp.exp(st - m_new[0:1]).astype(BF16)
                m_s[h] = m_new
                pvs.append(jnp.dot(vt[sl, :], pt, preferred_element_type=F32))
            for h in range(NH):
                acc_s[h] = acc_s[h] * alphas[h] + pvs[h]

        def loop_body(j, carry):
            step(j, False)
            return carry

        lax.fori_loop(0, i, loop_body, 0)
        step(i, True)
        for g in range(NH // 2):
            outs = []
            for h in (2 * g, 2 * g + 1):
                acc = acc_s[h]
                l_row = acc[V_HEAD:V_HEAD + 1, :]
                outs.append(acc[0:V_HEAD, :] / l_row)
                lse_ref[0, h * SUBLANES:(h + 1) * SUBLANES, :] = m_s[h] + jnp.log(l_row)
            o_ref[:, g * LANES:(g + 1) * LANES] = jnp.concatenate(outs, axis=0).T

    return pl.pallas_call(
        body, name=name, grid=(HEADS // NH, nq),
        out_shape=(jax.ShapeDtypeStruct((S, HEADS * V_HEAD), F32),
                   jax.ShapeDtypeStruct((nq, HEADS * SUBLANES, T), F32)),
        in_specs=[pl.BlockSpec((T, PW), lambda p, i: (i, p)),
                  pl.BlockSpec((S, PW), lambda p, i: (0, p)),
                  pl.BlockSpec((nq, PW, T), lambda p, i: (0, p, 0))],
        out_specs=(pl.BlockSpec((T, NH * V_HEAD), lambda p, i: (i, p)),
                   pl.BlockSpec((1, NH * SUBLANES, T), lambda p, i: (i, p, 0))),
        scratch_shapes=[pltpu.VMEM((NH, SUBLANES, T), F32), pltpu.VMEM((NH, HEAD_PAD, T), F32)],
        compiler_params=_cparams(),
    )(q, k, vT3)


def _attn_bwd_t(q, k, v, kT3, do, lse3, dl3, name):
    S = q.shape[0]
    T = _attn_tile(S)
    nq = S // T
    NH = ATTN_BWD_HEADS
    PW = NH * HEAD_PAD
    VW = NH * V_HEAD

    def body(q_ref, k_ref, v_ref, kt_ref, do_ref, lse_ref, dl_ref, dq_ref, dk_ref, dv_ref, dk_s, dv_s):
        j = pl.program_id(1)
        left = lax.broadcasted_iota(jnp.int32, (T, LANES), 1) < V_HEAD

        @pl.when(j == 0)
        def _():
            dq_ref[...] = jnp.zeros_like(dq_ref)

        dk_s[...] = jnp.zeros_like(dk_s)
        dv_s[...] = jnp.zeros_like(dv_s)
        kb = k_ref[...]
        vms = []
        for g in range(NH // 2):
            vb = v_ref[:, g * LANES:(g + 1) * LANES]
            vms += [jnp.where(left, vb, jnp.zeros_like(vb)), jnp.where(left, jnp.zeros_like(vb), vb)]
        kt = kt_ref[0]

        def step(i, masked):
            r0 = pl.multiple_of(i * T, T)
            qb = q_ref[pl.ds(r0, T), :]
            do_all = do_ref[pl.ds(r0, T), :]
            lse = lse_ref[i]
            dl = dl_ref[i]
            heads = [slice(h * HEAD_PAD, (h + 1) * HEAD_PAD) for h in range(NH)]
            dobs = [do_all[:, (h // 2) * LANES:(h // 2 + 1) * LANES] for h in range(NH)]
            sts = [lax.dot_general(kb[:, sl], qb[:, sl], _NT, preferred_element_type=F32) for sl in heads]
            dpts = [lax.dot_general(vms[h], dobs[h], _NT, preferred_element_type=F32) for h in range(NH)]
            res = []
            for h, sl in enumerate(heads):
                r8 = h * SUBLANES
                pt = jnp.exp(sts[h] - lse[r8:r8 + 1, :])
                if masked:
                    pt = jnp.where(_chunk_mask_t(T), pt, 0.0)
                dst = (pt * (dpts[h] - dl[r8:r8 + 1, :])).astype(BF16)
                res.append((jnp.dot(pt.astype(BF16), dobs[h], preferred_element_type=F32),
                            jnp.dot(dst, qb[:, sl], preferred_element_type=F32),
                            jnp.dot(kt[sl, :], dst, preferred_element_type=F32)))
            for h, sl in enumerate(heads):
                dv_s[h] += res[h][0]
                dk_s[:, sl] += res[h][1]
                dq_ref[i, sl, :] += res[h][2]

        def loop_body(i, carry):
            step(i, False)
            return carry

        step(j, True)
        lax.fori_loop(j + 1, nq, loop_body, 0)
        dk_ref[...] = dk_s[...]
        for g in range(NH // 2):
            dv_ref[:, g * LANES:(g + 1) * LANES] = jnp.where(left, dv_s[2 * g], dv_s[2 * g + 1])

    return pl.pallas_call(
        body, name=name, grid=(HEADS // NH, nq),
        out_shape=(jax.ShapeDtypeStruct((nq, HEADS * HEAD_PAD, T), F32),
                   jax.ShapeDtypeStruct((S, HEADS * HEAD_PAD), F32), jax.ShapeDtypeStruct((S, HEADS * V_HEAD), F32)),
        in_specs=[pl.BlockSpec((S, PW), lambda p, j: (0, p)),
                  pl.BlockSpec((T, PW), lambda p, j: (j, p)),
                  pl.BlockSpec((T, VW), lambda p, j: (j, p)),
                  pl.BlockSpec((1, PW, T), lambda p, j: (j, p, 0)),
                  pl.BlockSpec((S, VW), lambda p, j: (0, p)),
                  pl.BlockSpec((nq, NH * SUBLANES, T), lambda p, j: (0, p, 0)),
                  pl.BlockSpec((nq, NH * SUBLANES, T), lambda p, j: (0, p, 0))],
        out_specs=(pl.BlockSpec((nq, PW, T), lambda p, j: (0, p, 0)),
                   pl.BlockSpec((T, PW), lambda p, j: (j, p)),
                   pl.BlockSpec((T, VW), lambda p, j: (j, p))),
        scratch_shapes=[pltpu.VMEM((T, PW), F32), pltpu.VMEM((NH, T, LANES), F32)],
        compiler_params=_cparams(),
    )(q, k, v, kT3, do, lse3, dl3)


def _even_post(z, cv, o, name):
    S = z.shape[0]
    T = _row_tile(S)

    def body(ab_ref, ag_ref, bg_ref, cv_ref, o_ref, y_ref):
        y_ref[:, 0:SCW] = (ab_ref[...] * cv_ref[...] * _silu(ag_ref[...])).astype(BF16)
        y_ref[:, SCW:2 * SCW] = (o_ref[...] * _silu(bg_ref[...])).astype(BF16)

    return pl.pallas_call(
        body, name=name, grid=(S // T,),
        out_shape=jax.ShapeDtypeStruct((S, 2 * SCW), BF16),
        in_specs=[_rows(T, SCW, 0), _rows(T, SCW, 3), _rows(T, SCW, 4), _rows(T, SCW), _rows(T, SCW)],
        out_specs=_rows(T, 2 * SCW), compiler_params=_cparams(),
    )(z, z, z, cv, o)


def _even_bwd_gates(dyc, z, cv, o, name):
    S = z.shape[0]
    T = _row_tile(S)

    def body(dya_ref, dyb_ref, ab_ref, ag_ref, bg_ref, cv_ref, o_ref,
             dab_ref, dag_ref, dbg_ref, dcv_ref, do_ref, dl_ref):
        dya, ab, ag, cv = dya_ref[...], ab_ref[...], ag_ref[...], cv_ref[...]
        sg = _silu(ag)
        dab_ref[...] = (dya * cv * sg).astype(BF16)
        dcv_ref[...] = dya * ab * sg
        dag_ref[...] = (dya * ab * cv * _dsilu(ag)).astype(BF16)
        dyb, bg, ov = dyb_ref[...], bg_ref[...], o_ref[...]
        dov = dyb * _silu(bg)
        do_ref[...] = dov.astype(BF16)
        dbg_ref[...] = (dyb * ov * _dsilu(bg)).astype(BF16)
        prod = dov * ov
        left = lax.broadcasted_iota(jnp.int32, (T, LANES), 1) < V_HEAD
        for p in range(HEADS // 2):
            blk = prod[:, p * LANES:(p + 1) * LANES]
            s0 = jnp.sum(jnp.where(left, blk, 0.0), axis=1, keepdims=True)
            s1 = jnp.sum(jnp.where(left, 0.0, blk), axis=1, keepdims=True)
            dt = jnp.where(left, s0, s1).T
            dl_ref[0, 2 * p * SUBLANES:(2 * p + 1) * SUBLANES, :] = dt[0:SUBLANES, :]
            dl_ref[0, (2 * p + 1) * SUBLANES:(2 * p + 2) * SUBLANES, :] = dt[V_HEAD:V_HEAD + SUBLANES, :]

    assert T == _attn_tile(S)
    bf = jax.ShapeDtypeStruct((S, SCW), BF16)
    ff = jax.ShapeDtypeStruct((S, SCW), F32)
    return pl.pallas_call(
        body, name=name, grid=(S // T,),
        out_shape=(bf, bf, bf, ff, bf, jax.ShapeDtypeStruct((S // T, HEADS * SUBLANES, T), F32)),
        in_specs=[_rows(T, SCW, 0), _rows(T, SCW, 1), _rows(T, SCW, 0), _rows(T, SCW, 3), _rows(T, SCW, 4),
                  _rows(T, SCW), _rows(T, SCW)],
        out_specs=(_rows(T, SCW),) * 5 + (pl.BlockSpec((1, HEADS * SUBLANES, T), lambda i: (i, 0, 0)),),
        compiler_params=_cparams(),
    )(dyc, dyc, z, z, z, cv, o)


def _qkv_bwd(dq, dk, dv, z, tabs, w_q, w_kv, qg, kvg, name):
    S = dk.shape[0]
    T = _attn_tile(S)
    HW = HEADS * HEAD_PAD
    VW = HEADS * V_HEAD
    scale = 1.0 / math.sqrt(QK_NOPE + QK_ROPE)

    def fn(ins, outs):
        dq_ref, dk_ref, dv_ref, cq_ref, ckv_ref, ct_ref, ut_ref, dt_ref, wq_ref, wkv_ref, qg_ref, kvg_ref = ins
        dqp_ref, dkvp_ref, dcq_ref, dckv_ref, dkr_ref = outs
        ct, ut, dt = ct_ref[...], ut_ref[...], dt_ref[...]
        dkr = jnp.zeros((T, HEAD_PAD), F32)
        for h in range(HEADS):
            sl = slice(h * HEAD_PAD, (h + 1) * HEAD_PAD)
            dqp_ref[:, sl] = (_rope_t(dq_ref[0, sl, :].T, ct, ut, dt) * scale).astype(BF16)
            dkh = _rope_t(dk_ref[:, sl], ct, ut, dt)
            dkr = dkr + dkh
            dkvp_ref[:, sl] = dkh.astype(BF16)
        dkvp_ref[:, HW:] = dv_ref[...].astype(BF16)
        dkr_ref[...] = dkr.astype(BF16)
        sums = []
        for lat_ref, g_ref, dpre_ref, w_ref, dlat_ref in ((cq_ref, qg_ref, dqp_ref, wq_ref, dcq_ref),
                                                         (ckv_ref, kvg_ref, dkvp_ref, wkv_ref, dckv_ref)):
            dn = lax.dot_general(dpre_ref[...], w_ref[...], _NT, preferred_element_type=F32)
            xv = lat_ref[...]
            rstd = lax.rsqrt(jnp.mean(xv * xv, axis=-1, keepdims=True) + EPS)
            xh = xv * rstd
            dxh = dn * g_ref[...]
            dlat_ref[...] = (rstd * (dxh - xh * jnp.mean(dxh * xh, axis=-1, keepdims=True))).astype(BF16)
            sums.append(dn * xh)
        return sums

    return _col_sums(
        2, fn, [dq, dk, dv, z, z, *tabs, w_q, w_kv, qg, kvg],
        [pl.BlockSpec((1, HW, T), lambda i: (i, 0, 0)), _rows(T, HW), _rows(T, VW), _rows(T, QL, 10), _rows(T, KVL, 22),
         _rows(T, HEAD_PAD), _rows(T, HEAD_PAD), _rows(T, HEAD_PAD),
         _const(w_q.shape), _const(w_kv.shape), _const((1, QL)), _const((1, KVL))],
        [jax.ShapeDtypeStruct((S, HW), BF16), jax.ShapeDtypeStruct((S, HW + VW), BF16),
         jax.ShapeDtypeStruct((S, QL), BF16), jax.ShapeDtypeStruct((S, KVL), BF16),
         jax.ShapeDtypeStruct((S, HEAD_PAD), BF16)],
        [_rows(T, HW), _rows(T, HW + VW), _rows(T, QL), _rows(T, KVL), _rows(T, HEAD_PAD)],
        S, T, [QL, KVL], name)


def _even_dz(dab, du, z, dag, dbg, dcq, dckv, dkr, name):
    S = z.shape[0]
    T = _row_tile(S)

    def body(dab_ref, du_ref, ac_ref, ax_ref, dag_ref, dbg_ref, dcq_ref, dckv_ref, dkr_ref, dz_ref):
        duv = du_ref[...]
        dz_ref[:, 0:SCW] = dab_ref[...]
        dz_ref[:, SCW:2 * SCW] = (duv * ax_ref[...]).astype(BF16)
        dz_ref[:, 2 * SCW:3 * SCW] = (duv * ac_ref[...]).astype(BF16)
        dz_ref[:, 3 * SCW:4 * SCW] = dag_ref[...]
        dz_ref[:, 4 * SCW:5 * SCW] = dbg_ref[...]
        dz_ref[:, 5 * SCW:5 * SCW + QL] = dcq_ref[...]
        dz_ref[:, 5 * SCW + QL:5 * SCW + QL + KVL] = dckv_ref[...]
        dz_ref[:, 5 * SCW + QL + KVL:ZE] = dkr_ref[...]

    return pl.pallas_call(
        body, name=name, grid=(S // T,),
        out_shape=jax.ShapeDtypeStruct((S, ZE), BF16),
        in_specs=[_rows(T, SCW), _rows(T, SCW), _rows(T, SCW, 1), _rows(T, SCW, 2), _rows(T, SCW), _rows(T, SCW),
                  _rows(T, QL), _rows(T, KVL), _rows(T, HEAD_PAD)],
        out_specs=_rows(T, ZE), compiler_params=_cparams(),
    )(dab, du, z, z, dag, dbg, dcq, dckv, dkr)


def _odd_pre(z, name):
    S, D = z.shape[0], z.shape[1] // 3
    T = _row_tile(S)

    def body(val_ref, glu_ref, u_ref):
        u_ref[...] = val_ref[...] * _sigmoid(glu_ref[...])

    return pl.pallas_call(
        body, name=name, grid=(S // T,),
        out_shape=jax.ShapeDtypeStruct((S, D), F32),
        in_specs=[_rows(T, D, 0), _rows(T, D, 1)], out_specs=_rows(T, D),
        compiler_params=_cparams(),
    )(z, z)


def _layer_norm_stats(cv):
    mu = jnp.mean(cv, axis=-1, keepdims=True)
    cen = cv - mu
    rstd = lax.rsqrt(jnp.mean(cen * cen, axis=-1, keepdims=True) + EPS)
    return cen * rstd, rstd


def _odd_post(cv, z, ln_g, ln_b, name):
    S, D = cv.shape
    T = _row_tile(S)

    def body(cv_ref, sg_ref, g_ref, b_ref, y_ref):
        cvh, _ = _layer_norm_stats(cv_ref[...])
        y_ref[...] = (_silu(cvh * g_ref[...] + b_ref[...]) * _silu(sg_ref[...])).astype(BF16)

    return pl.pallas_call(
        body, name=name, grid=(S // T,),
        out_shape=jax.ShapeDtypeStruct((S, D), BF16),
        in_specs=[_rows(T, D), _rows(T, D, 2), _const((1, D)), _const((1, D))],
        out_specs=_rows(T, D), compiler_params=_cparams(),
    )(cv, z, ln_g, ln_b)


def _odd_bwd_norm(dyi, cv, z, ln_g, ln_b, name):
    S, D = cv.shape
    T = _row_tile(S)

    def fn(ins, outs):
        dy_ref, cv_ref, sg_ref, g_ref, b_ref = ins
        dcv_ref, dsg_ref = outs
        cvh, rstd = _layer_norm_stats(cv_ref[...])
        ln = cvh * g_ref[...] + b_ref[...]
        sgv, dy = sg_ref[...], dy_ref[...]
        dsg_ref[...] = (dy * _silu(ln) * _dsilu(sgv)).astype(BF16)
        dln = dy * _silu(sgv) * _dsilu(ln)
        dh = dln * g_ref[...]
        dcv_ref[...] = rstd * (dh - jnp.mean(dh, axis=-1, keepdims=True)
                               - cvh * jnp.mean(dh * cvh, axis=-1, keepdims=True))
        return [dln * cvh, dln]

    return _col_sums(2, fn, [dyi, cv, z, ln_g, ln_b],
                     [_rows(T, D), _rows(T, D), _rows(T, D, 2), _const((1, D)), _const((1, D))],
                     [jax.ShapeDtypeStruct((S, D), F32), jax.ShapeDtypeStruct((S, D), BF16)],
                     [_rows(T, D), _rows(T, D)], S, T, [D, D], name)


def _odd_dz(du, z, dsg, name):
    S, D = du.shape
    T = _row_tile(S)

    def body(du_ref, val_ref, glu_ref, dsg_ref, dz_ref):
        duv = du_ref[...]
        sig = _sigmoid(glu_ref[...])
        dz_ref[:, 0:D] = (duv * sig).astype(BF16)
        dz_ref[:, D:2 * D] = (duv * val_ref[...] * sig * (1.0 - sig)).astype(BF16)
        dz_ref[:, 2 * D:3 * D] = dsg_ref[...]

    return pl.pallas_call(
        body, name=name, grid=(S // T,),
        out_shape=jax.ShapeDtypeStruct((S, 3 * D), BF16),
        in_specs=[_rows(T, D), _rows(T, D, 0), _rows(T, D, 1), _rows(T, D)],
        out_specs=_rows(T, 3 * D), compiler_params=_cparams(),
    )(du, z, z, dsg)


ADAM_BLOCK_ELEMS = 128 * 1024


def _adam_tiles(R, C):
    if R * C <= ADAM_BLOCK_ELEMS:
        return R, C
    tr = R
    for cand in range(SUBLANES, R, SUBLANES):
        if R % cand == 0 and cand * C <= ADAM_BLOCK_ELEMS:
            tr = cand
    if tr < R:
        return tr, C
    tc = C
    for cand in range(LANES, C, LANES):
        if C % cand == 0 and R * cand <= ADAM_BLOCK_ELEMS:
            tc = cand
    return R, tc


def _adamw(g_parts, w, m, v, name):
    if not isinstance(g_parts, (list, tuple)):
        g_parts = [g_parts]
    ng = len(g_parts)
    _, R, C = g_parts[0].shape
    tr, tc = _adam_tiles(R, C)

    def body(*refs):
        g_refs = refs[:ng]
        w_ref, m_ref, v_ref, go_ref, d_ref, mo_ref, vo_ref = refs[ng:]
        g = None
        for g_ref in g_refs:
            for p in range(g_ref.shape[0]):
                part = g_ref[p].astype(F32)
                g = part if g is None else g + part
        mn = ADAM_B1 * m_ref[...] + (1.0 - ADAM_B1) * g
        vn = ADAM_B2 * v_ref[...] + (1.0 - ADAM_B2) * (g * g)
        m_hat = mn / (1.0 - ADAM_B1 ** ADAM_STEP)
        v_hat = vn / (1.0 - ADAM_B2 ** ADAM_STEP)
        go_ref[...] = g
        d_ref[...] = -ADAM_LR * (m_hat / (jnp.sqrt(v_hat) + ADAM_EPS) + ADAM_WD * w_ref[...])
        mo_ref[...] = mn
        vo_ref[...] = vn

    slab = jax.ShapeDtypeStruct((R, C), F32)
    blk = pl.BlockSpec((tr, tc), lambda i, j: (i, j))
    return pl.pallas_call(
        body, name=name, grid=(R // tr, C // tc),
        out_shape=(slab,) * 4,
        in_specs=[pl.BlockSpec((g.shape[0], tr, tc), lambda i, j: (0, i, j)) for g in g_parts] + [blk, blk, blk],
        out_specs=(blk,) * 4, compiler_params=_cparams(),
    )(*g_parts, w, m, v)


def _gather_cols(g, shape):
    nd = len(shape)
    t = jnp.moveaxis(g, 0, nd - 1)
    return t.reshape(tuple(shape[:-1]) + (N_DEV * shape[-1],))


def _scatter_cols(full, n):
    t = full.reshape(full.shape[:-1] + (N_DEV, n))
    return jnp.moveaxis(t, -2, 0)


def kernel(x, c, positions, ada_w, ada_b, pre_norm_g, post_norm_g, even_w_in, even_sc_conv_w, even_sc_conv_b, even_q_norm_g, even_kv_norm_g, even_w_uq, even_w_ukv, even_w_out, odd_w_in, odd_conv_w, odd_conv_b, odd_ln_g, odd_ln_b, odd_w_out, loss_target, m_ada_w, m_ada_b, m_pre_norm_g, m_post_norm_g, m_even_w_in, m_even_sc_conv_w, m_even_sc_conv_b, m_even_q_norm_g, m_even_kv_norm_g, m_even_w_uq, m_even_w_ukv, m_even_w_out, m_odd_w_in, m_odd_conv_w, m_odd_conv_b, m_odd_ln_g, m_odd_ln_b, m_odd_w_out, v_ada_w, v_ada_b, v_pre_norm_g, v_post_norm_g, v_even_w_in, v_even_sc_conv_w, v_even_sc_conv_b, v_even_q_norm_g, v_even_kv_norm_g, v_even_w_uq, v_even_w_ukv, v_even_w_out, v_odd_w_in, v_odd_conv_w, v_odd_conv_b, v_odd_ln_g, v_odd_ln_b, v_odd_w_out):
    S, D = x.shape[1], x.shape[2]
    L = ada_w.shape[0]
    NE, NO = even_w_in.shape[0], odd_w_in.shape[0]
    me = 4 * lax.axis_index("x") + 2 * lax.axis_index("y") + lax.axis_index("c")
    x0 = x[0]
    target = loss_target[0]

    small_parts = [c, even_sc_conv_w, odd_conv_w, odd_conv_b, odd_ln_g, odd_ln_b]
    small_shapes = [p.shape for p in small_parts]
    sg = _exchange([_pack(small_parts, F32, SUBLANES)], False, "gather_small")[0].reshape(N_DEV, -1)
    c_all, scw_g, ocw_g, ocb_g, olg_g, olb_g = _unpack(sg, small_shapes)
    c_all = c_all.reshape(N_DEV, D)
    sc_conv_w = _gather_cols(scw_g, even_sc_conv_w.shape)
    o_conv_w = _gather_cols(ocw_g, odd_conv_w.shape)
    o_conv_b = _gather_cols(ocb_g, odd_conv_b.shape)
    o_ln_g = _gather_cols(olg_g, odd_ln_g.shape)
    o_ln_b = _gather_cols(olb_g, odd_ln_b.shape)

    pad_q = HEAD_PAD - QK_NOPE - QK_ROPE
    w_local = [jnp.swapaxes(even_w_in, 1, 2).astype(BF16),
               jnp.pad(even_w_uq, ((0, 0), (0, 0), (0, pad_q))).astype(BF16),
               jnp.pad(even_w_ukv[..., :QK_NOPE], ((0, 0), (0, 0), (0, HEAD_PAD - QK_NOPE))).astype(BF16),
               even_w_ukv[..., QK_NOPE:].astype(BF16),
               even_w_out.astype(BF16), odd_w_in.astype(BF16), odd_w_out.astype(BF16)]
    n_ada = ada_w.shape[2]
    ada_b_cols = lax.dynamic_slice_in_dim(ada_b, me * n_ada, n_ada, axis=1).reshape(L, 1, n_ada)
    mod_slab = _ada_fwd(c_all, ada_w, ada_b_cols)
    mod_g = _exchange([_pack([mod_slab], F32, SUBLANES)], False, "gather_mod")[0].reshape(N_DEV, -1)
    mod_all = mod_g[:, :L * N_DEV * n_ada].reshape(N_DEV, L, N_DEV, n_ada)
    mod = lax.dynamic_index_in_dim(mod_all, me, axis=2, keepdims=False)
    mod = jnp.moveaxis(mod, 0, 1).reshape(L, 3 * D)
    shift, scale, gate = mod[:, :D], mod[:, D:2 * D], mod[:, 2 * D:]

    heads_to_cols = lambda g: jnp.moveaxis(g, 0, 1).reshape(g.shape[1], -1)
    w_handles = {}
    token = jnp.broadcast_to(jnp.minimum(jnp.abs(mod[0, 0]), 0.0), (SUBLANES, LANES))
    for layer in range(L):
        i = layer // 2
        groups = ({"in": [w_local[0][i]], "rest": [w[i] for w in w_local[1:5]]} if layer % 2 == 0
                  else {"all": [w[i] for w in w_local[5:]]})
        for key, mine in groups.items():
            mine = [w + token[0, 0].astype(BF16) for w in mine]
            w_handles[layer, key], token = _exchange_start(mine, False, f"gather_weights_start_l{layer}_{key}")
    w_token = token

    def arrived(layer, key, after):
        return _exchange_wait(w_handles[layer, key], False, after, f"gather_weights_wait_l{layer}_{key}")[1]

    e_w_in_k, e_w_q_k, e_w_kv_k, e_w_out, o_w_in, o_w_out = ([None] * NE, [None] * NE, [None] * NE, [None] * NE,
                                                             [None] * NO, [None] * NO)

    half = QK_ROPE // 2
    inv_freq = 1.0 / (ROPE_THETA ** (jnp.arange(0, QK_ROPE, 2, dtype=F32) / QK_ROPE))
    inv_lane = jnp.zeros((HEAD_PAD,), F32).at[QK_NOPE:QK_NOPE + QK_ROPE].set(jnp.concatenate([inv_freq, inv_freq]))
    tabs = _rope_tables(positions.astype(F32).reshape(S, 1), inv_lane.reshape(1, HEAD_PAD))
    del half

    row = lambda a: a.reshape(1, -1)
    scb = even_sc_conv_b
    KP3, KP31 = SUBLANES, 32

    saved = []
    xs = x0
    for layer in range(L):
        i = layer // 2
        tag = f"l{layer}"
        h = _pre_norm(xs, row(pre_norm_g[layer]), row(scale[layer]), row(shift[layer]), f"pre_norm_{tag}")
        first = w_token if layer == 0 else h
        if layer % 2 == 0:
            wt = arrived(layer, "in", first)[0].reshape(-1, D)
            e_w_in_k[i] = jnp.concatenate([wt[:2048], wt[2464:2976], wt[2048:2432], jnp.zeros((QK_NOPE, D), BF16),
                                           wt[2432:2464], jnp.zeros((pad_q, D), BF16)], axis=0)
            z = _matmul(h, e_w_in_k[i], "nt", F32, f"w_in_{tag}", tn=1024)
            eq_g, ek_g, ev_g, eout_g = arrived(layer, "rest", z)
            e_w_q_k[i] = heads_to_cols(eq_g)
            e_w_kv_k[i] = jnp.concatenate([heads_to_cols(ek_g), heads_to_cols(ev_g)], axis=-1)
            e_w_out[i] = eout_g.reshape(-1, D)
            u, qn, kvn = _even_pre(z, row(even_q_norm_g[i]), row(even_kv_norm_g[i]), f"even_pre_{tag}")
            cw = jnp.pad(sc_conv_w[i], ((0, KP3 - SC_KERNEL), (0, 0)))
            cv = _conv_fwd(u, cw, row(scb[i]), SC_KERNEL, f"conv_{tag}")
            q, k, v, kT3, vT3 = _qkv_fwd_t(qn, kvn, z, tabs, e_w_q_k[i], e_w_kv_k[i], f"qkv_{tag}")
            o, lse = _attn_fwd_t(q, k, vT3, f"attn_{tag}")
            ycat = _even_post(z, cv, o, f"even_post_{tag}")
            y = _matmul(ycat, e_w_out[i], "nn", F32, f"w_out_{tag}", tn=1024)
            saved.append(dict(x=xs, h=h, z=z, u=u, qn=qn, kvn=kvn, cw=cw, cv=cv, q=q, k=k, v=v, kT3=kT3, o=o, lse=lse,
                              ycat=ycat, y=y))
        else:
            owin_g, oout_g = arrived(layer, "all", first)
            o_w_in[i], o_w_out[i] = heads_to_cols(owin_g), oout_g.reshape(-1, D)
            z = _matmul(h, o_w_in[i], "nn", F32, f"w_in_{tag}", tn=1024)
            u = _odd_pre(z, f"odd_pre_{tag}")
            cw = jnp.pad(o_conv_w[i], ((0, KP31 - CONF_KERNEL), (0, 0)))
            cv = _conv_fwd(u, cw, row(o_conv_b[i]), CONF_KERNEL, f"conv_{tag}")
            yin = _odd_post(cv, z, row(o_ln_g[i]), row(o_ln_b[i]), f"odd_post_{tag}")
            y = _matmul(yin, o_w_out[i], "nn", F32, f"w_out_{tag}", tn=1024)
            saved.append(dict(x=xs, h=h, z=z, u=u, cw=cw, cv=cv, yin=yin, y=y))
        xs = _post_norm(xs, y, row(post_norm_g[layer]), row(gate[layer]), f"post_norm_{tag}")

    loss_row, dx = _loss_head(xs, target)
    loss = lax.psum(loss_row[0, 0], MESH_AXES)

    g_pre, g_post, dmod = [None] * L, [None] * L, [None] * L
    g_e_w_in, g_e_w_uq, g_e_w_ukv, g_e_w_out = [None] * NE, [None] * NE, [None] * NE, [None] * NE
    g_scw, g_scb, g_qg, g_kvg = [None] * NE, [None] * NE, [None] * NE, [None] * NE
    g_o_w_in, g_o_w_out, g_ocw, g_ocb, g_olg, g_olb = ([None] * NO for _ in range(6))
    sm_w = [even_sc_conv_w, odd_conv_w, odd_conv_b, odd_ln_g, odd_ln_b]
    sm_rows = _pack(sm_w, F32, SUBLANES).shape[0]

    def small_slab():
        full = [_scatter_cols(jnp.stack(g_scw), even_sc_conv_w.shape[-1]),
                _scatter_cols(jnp.stack(g_ocw), odd_conv_w.shape[-1]),
                _scatter_cols(jnp.concatenate(g_ocb, 0), odd_conv_b.shape[-1]),
                _scatter_cols(jnp.concatenate(g_olg, 0), odd_ln_g.shape[-1]),
                _scatter_cols(jnp.concatenate(g_olb, 0), odd_ln_b.shape[-1])]
        flat = jnp.concatenate([g.reshape(N_DEV, -1) for g in full], axis=1)
        return jnp.pad(flat, ((0, 0), (0, sm_rows * PACK_COLS - flat.shape[1]))).reshape(N_DEV, sm_rows, PACK_COLS)

    s_handles, own_parts = [None] * L, [None] * L
    bw_token = jnp.zeros((SUBLANES, LANES), F32)

    def start_scatter(layer, parts):
        if layer == 0:
            parts = parts + [small_slab()]
        own_parts[layer] = [lax.dynamic_slice_in_dim(g, me, 1, axis=0) for g in parts]
        s_handles[layer], token = _exchange_start([g.astype(BF16) for g in parts], True,
                                                  f"scatter_grads_start_l{layer}")
        return token

    for layer in reversed(range(L)):
        i = layer // 2
        tag = f"l{layer}"
        sv = saved[layer]
        dy, dgate, g_post[layer] = _post_norm_bwd(dx, sv["y"], row(post_norm_g[layer]) + bw_token[0, 0],
                                                  row(gate[layer]), f"post_norm_bwd_{tag}")
        if layer % 2 == 0:
            dyc = _matmul(dy, e_w_out[i], "nt", F32, f"d_ycat_{tag}", tn=1024)
            g_e_w_out[i] = _matmul(sv["ycat"], dy, "tn", F32, f"g_w_out_{tag}", tk=512)
            dab, dag, dbg, dcv, do, delta = _even_bwd_gates(dyc, sv["z"], sv["cv"], sv["o"], f"even_gates_bwd_{tag}")
            du, dcw, g_scb[i] = _conv_bwd(dcv, sv["u"], sv["cw"], SC_KERNEL, f"conv_bwd_{tag}")
            g_scw[i] = dcw[:SC_KERNEL]
            dq, dk, dv = _attn_bwd_t(sv["q"], sv["k"], sv["v"], sv["kT3"], do, sv["lse"], delta, f"attn_bwd_{tag}")
            (dqp, dkvp, dcq, dckv, dkr, g_qg[i], g_kvg[i]) = _qkv_bwd(
                dq, dk, dv, sv["z"], tabs, e_w_q_k[i], e_w_kv_k[i],
                row(even_q_norm_g[i]), row(even_kv_norm_g[i]), f"qkv_bwd_{tag}")
            gq = _matmul(sv["qn"], dqp, "tn", F32, f"g_w_uq_{tag}", tk=512)
            gkv = _matmul(sv["kvn"], dkvp, "tn", F32, f"g_w_ukv_{tag}", tk=512)
            g_e_w_uq[i] = jnp.moveaxis(gq.reshape(QL, HEADS, HEAD_PAD)[..., :QK_NOPE + QK_ROPE], 1, 0)
            g_e_w_ukv[i] = jnp.moveaxis(jnp.concatenate(
                [gkv[:, :HEADS * HEAD_PAD].reshape(KVL, HEADS, HEAD_PAD)[..., :QK_NOPE],
                 gkv[:, HEADS * HEAD_PAD:].reshape(KVL, HEADS, V_HEAD)], axis=-1), 1, 0)
            g_e_w_out[i] = g_e_w_out[i].reshape(N_DEV, -1, D)
            dz = _even_dz(dab, du, sv["z"], dag, dbg, dcq, dckv, dkr, f"even_dz_{tag}")
            gt = _matmul(dz, sv["h"], "tn", F32, f"g_w_in_{tag}", tk=512, tn=1024)
            g_e_w_in[i] = jnp.concatenate([gt[:2048], gt[2560:2944], gt[2944 + QK_NOPE:2944 + QK_NOPE + QK_ROPE],
                                           gt[2048:2560]], axis=0).reshape(N_DEV, -1, D)
            bw_token = start_scatter(layer, [g_e_w_in[i], g_e_w_uq[i], g_e_w_ukv[i], g_e_w_out[i]])
            dh = _matmul(dz, e_w_in_k[i], "nn", F32, f"d_h_{tag}", tn=1024)
        else:
            dyi = _matmul(dy, o_w_out[i], "nt", F32, f"d_yin_{tag}", tn=1024)
            g_o_w_out[i] = _matmul(sv["yin"], dy, "tn", F32, f"g_w_out_{tag}", tk=512).reshape(N_DEV, -1, D)
            dcv, dsg, g_olg[i], g_olb[i] = _odd_bwd_norm(dyi, sv["cv"], sv["z"], row(o_ln_g[i]), row(o_ln_b[i]),
                                                         f"odd_norm_bwd_{tag}")
            du, dcw, g_ocb[i] = _conv_bwd(dcv, sv["u"], sv["cw"], CONF_KERNEL, f"conv_bwd_{tag}")
            g_ocw[i] = dcw[:CONF_KERNEL]
            dz = _odd_dz(du, sv["z"], dsg, f"odd_dz_{tag}")
            g_o_w_in[i] = jnp.moveaxis(_matmul(sv["h"], dz, "tn", F32, f"g_w_in_{tag}", tk=512, tn=1024)
                                       .reshape(D, N_DEV, -1), 1, 0)
            bw_token = start_scatter(layer, [g_o_w_in[i], g_o_w_out[i]])
            dh = _matmul(dz, o_w_in[i], "nt", F32, f"d_h_{tag}", tn=1024)
        dx, dshift, dscale, g_pre[layer] = _pre_norm_bwd(dh, sv["x"], dx, row(pre_norm_g[layer]) + bw_token[0, 0],
                                                         row(scale[layer]), f"pre_norm_bwd_{tag}")
        dmod[layer] = jnp.concatenate([dshift, dscale, dgate], axis=-1)
    grad_x = dx.reshape(1, S, D)

    rep_g = [jnp.concatenate(dmod, 0), jnp.concatenate(g_pre, 0), jnp.concatenate(g_post, 0),
             jnp.stack(g_scb), jnp.stack(g_qg), jnp.stack(g_kvg)]
    rep_w = [ada_b, pre_norm_g, post_norm_g, even_sc_conv_b, even_q_norm_g, even_kv_norm_g]
    rep_m = [m_ada_b, m_pre_norm_g, m_post_norm_g, m_even_sc_conv_b, m_even_q_norm_g, m_even_kv_norm_g]
    rep_v = [v_ada_b, v_pre_norm_g, v_post_norm_g, v_even_sc_conv_b, v_even_q_norm_g, v_even_kv_norm_g]
    rep_shapes = [w.shape for w in rep_w]
    rep_all = _exchange([_pack(rep_g, F32, SUBLANES)], False, "gather_small_grads")[0]
    rep_out = _adamw(rep_all, _pack(rep_w, F32, SUBLANES), _pack(rep_m, F32, SUBLANES), _pack(rep_v, F32, SUBLANES),
                     "adamw_replicated")
    rep_res = [_unpack(o.reshape(-1), rep_shapes) for o in rep_out]

    dmod_all = rep_all.reshape(N_DEV, -1)[:, :L * 3 * D].reshape(N_DEV, L, 3 * D)
    dmod_cols = jnp.moveaxis(lax.dynamic_slice_in_dim(dmod_all, me * n_ada, n_ada, axis=2), 0, 1)
    g_ada_w = _ada_bwd(c_all.T, dmod_cols)
    ada_out = _adamw(g_ada_w.reshape(1, -1, PACK_COLS), ada_w.reshape(-1, PACK_COLS),
                     m_ada_w.reshape(-1, PACK_COLS), v_ada_w.reshape(-1, PACK_COLS), "adamw_ada_w")
    ada_res = [o.reshape(ada_w.shape) for o in ada_out]

    sm_m = [m_even_sc_conv_w, m_odd_conv_w, m_odd_conv_b, m_odd_ln_g, m_odd_ln_b]
    sm_v = [v_even_sc_conv_w, v_odd_conv_w, v_odd_conv_b, v_odd_ln_g, v_odd_ln_b]
    sm_shapes = [w.shape for w in sm_w]
    state = {"even_w_in": (even_w_in, m_even_w_in, v_even_w_in), "even_w_uq": (even_w_uq, m_even_w_uq, v_even_w_uq),
             "even_w_ukv": (even_w_ukv, m_even_w_ukv, v_even_w_ukv), "even_w_out": (even_w_out, m_even_w_out, v_even_w_out),
             "odd_w_in": (odd_w_in, m_odd_w_in, v_odd_w_in), "odd_w_out": (odd_w_out, m_odd_w_out, v_odd_w_out)}
    big_res = {name: [[None] * len(state[name][0]) for _ in range(4)] for name in state}
    after = bw_token
    sm_res = None
    for layer in reversed(range(L)):
        i = layer // 2
        names = ["even_w_in", "even_w_uq", "even_w_ukv", "even_w_out"] if layer % 2 == 0 else ["odd_w_in", "odd_w_out"]
        _, landed = _exchange_wait(s_handles[layer], True, after, f"scatter_grads_wait_l{layer}")
        own = own_parts[layer]
        for a, name in enumerate(names):
            transposed = name == "even_w_in"
            wmv = [t[i].T if transposed else t[i] for t in state[name]]
            res = _adamw([own[a], landed[a]], *wmv, f"adamw_{name}_{i}")
            for kind in range(4):
                big_res[name][kind][i] = res[kind].T if transposed else res[kind]
            after = res[0]
        if layer == 0:
            sm_out = _adamw([own[-1], landed[-1]], _pack(sm_w, F32, SUBLANES), _pack(sm_m, F32, SUBLANES),
                            _pack(sm_v, F32, SUBLANES), "adamw_small_sharded")
            sm_res = [_unpack(o.reshape(-1), sm_shapes) for o in sm_out]
    sh_res = [dict(zip(["even_sc_conv_w", "odd_conv_w", "odd_conv_b", "odd_ln_g", "odd_ln_b"], sm_res[kind]))
              for kind in range(4)]
    for name in state:
        for kind in range(4):
            sh_res[kind][name] = jnp.stack(big_res[name][kind])

    order = ["ada_w", "ada_b", "pre_norm_g", "post_norm_g", "even_w_in", "even_sc_conv_w", "even_sc_conv_b",
             "even_q_norm_g", "even_kv_norm_g", "even_w_uq", "even_w_ukv", "even_w_out", "odd_w_in", "odd_conv_w",
             "odd_conv_b", "odd_ln_g", "odd_ln_b", "odd_w_out"]
    rep_names = ["ada_b", "pre_norm_g", "post_norm_g", "even_sc_conv_b", "even_q_norm_g", "even_kv_norm_g"]
    outs = [loss, grad_x]
    for kind in range(4):
        for name in order:
            if name == "ada_w":
                outs.append(ada_res[kind])
            elif name in rep_names:
                outs.append(rep_res[kind][rep_names.index(name)])
            else:
                outs.append(sh_res[kind][name])
    return tuple(outs)
```

```python
import functools
import math

import jax
import jax.numpy as jnp
from jax import lax
from jax.experimental import pallas as pl
from jax.experimental.pallas import tpu as pltpu

F32 = jnp.float32
BF16 = jnp.bfloat16
MESH_AXES = ("x", "y", "c")
N_DEV = 8
EPS = 1e-6
CHUNK = 64
HEADS = 8
QK_NOPE = 64
QK_ROPE = 32
V_HEAD = 64
HEAD_PAD = 128
ROPE_THETA = 10000.0
SC_KERNEL = 3
CONF_KERNEL = 31
LANES = 128
SUBLANES = 8
PACK_COLS = 1024
VMEM_LIMIT = 48 * 1024 * 1024
NEG = -1e30

ADAM_LR = 0.001
ADAM_B1 = 0.9
ADAM_B2 = 0.999
ADAM_EPS = 1e-08
ADAM_WD = 0.01
ADAM_STEP = 10


def _cparams():
    return pltpu.CompilerParams(vmem_limit_bytes=VMEM_LIMIT)


def _sigmoid(x):
    return 1.0 / (1.0 + jnp.exp(-x))


def _silu(x):
    return x * _sigmoid(x)


def _dsilu(x):
    s = _sigmoid(x)
    return s * (1.0 + x * (1.0 - s))


def _rows(T, width, cb=0):
    return pl.BlockSpec((T, width), lambda i: (i, cb))


def _const(shape):
    nd = len(shape)
    return pl.BlockSpec(shape, lambda i: (0,) * nd)


def _row_tile(S):
    return min(256, S)


def _exchange(srcs, scatter, name):
    n = len(srcs)
    shapes = [tuple(s.shape[1:]) if scatter else tuple(s.shape) for s in srcs]

    def body(*refs):
        src_refs, out_refs = refs[:n], refs[n:2 * n]
        send_sems, recv_sems, local_sems = refs[2 * n:]
        x, y, c = lax.axis_index("x"), lax.axis_index("y"), lax.axis_index("c")
        me = 4 * x + 2 * y + c
        owns, copies = [], []
        for a in range(n):
            def piece(d, a=a):
                return src_refs[a].at[d] if scatter else src_refs[a]

            own = pltpu.make_async_copy(piece(me), out_refs[a].at[me], local_sems.at[a])
            own.start()
            owns.append(own)
            for k in range(1, N_DEV):
                px, py, pc = x ^ ((k >> 2) & 1), y ^ ((k >> 1) & 1), c ^ (k & 1)
                peer = 4 * px + 2 * py + pc
                sem = a * (N_DEV - 1) + k - 1
                cp = pltpu.make_async_remote_copy(
                    src_ref=piece(peer), dst_ref=out_refs[a].at[me],
                    send_sem=send_sems.at[sem], recv_sem=recv_sems.at[sem],
                    device_id=(px, py, pc), device_id_type=pl.DeviceIdType.MESH)
                cp.start()
                arrival = pltpu.make_async_remote_copy(
                    src_ref=piece(peer), dst_ref=out_refs[a].at[peer],
                    send_sem=send_sems.at[sem], recv_sem=recv_sems.at[sem],
                    device_id=(x, y, c), device_id_type=pl.DeviceIdType.MESH)
                copies.append((cp, arrival))
        for _, arrival in copies:
            arrival.wait_recv()
        for cp, _ in copies:
            cp.wait_send()
        for own in owns:
            own.wait()

    return pl.pallas_call(
        body, name=name,
        out_shape=tuple(jax.ShapeDtypeStruct((N_DEV,) + shp, s.dtype) for shp, s in zip(shapes, srcs)),
        in_specs=[pl.BlockSpec(memory_space=pl.ANY)] * n,
        out_specs=tuple(pl.BlockSpec(memory_space=pl.ANY) for _ in range(n)),
        scratch_shapes=[pltpu.SemaphoreType.DMA((n * (N_DEV - 1),)),
                        pltpu.SemaphoreType.DMA((n * (N_DEV - 1),)),
                        pltpu.SemaphoreType.DMA((n,))],
    )(*srcs)


_HBM = pl.BlockSpec(memory_space=pltpu.HBM)
_SEM = pl.BlockSpec(memory_space=pltpu.SEMAPHORE)


def _peer(k):
    x, y, c = lax.axis_index("x"), lax.axis_index("y"), lax.axis_index("c")
    return x ^ ((k >> 2) & 1), y ^ ((k >> 1) & 1), c ^ (k & 1)


def _exchange_start(srcs, scatter, name):
    n = len(srcs)
    shapes = [tuple(s.shape[1:]) if scatter else tuple(s.shape) for s in srcs]
    slots = N_DEV - 1 if scatter else N_DEV
    lands = [lax.empty((slots,) + shp, s.dtype) for shp, s in zip(shapes, srcs)]
    if not scatter:
        here = 4 * lax.axis_index("x") + 2 * lax.axis_index("y") + lax.axis_index("c")
        lands = [lax.dynamic_update_index_in_dim(l, s, here, 0) for l, s in zip(lands, srcs)]

    def body(*refs):
        src_refs, land_refs = refs[:n], refs[n:2 * n]
        send_sems, recv_sems = refs[2 * n], refs[2 * n + 1]
        token = refs[4 * n + 2]
        me = 4 * lax.axis_index("x") + 2 * lax.axis_index("y") + lax.axis_index("c")
        for a in range(n):
            for k in range(1, N_DEV):
                px, py, pc = _peer(k)
                peer = 4 * px + 2 * py + pc
                pltpu.make_async_remote_copy(
                    src_ref=src_refs[a].at[peer] if scatter else src_refs[a],
                    dst_ref=land_refs[a].at[k - 1] if scatter else land_refs[a].at[me],
                    send_sem=send_sems.at[a * (N_DEV - 1) + k - 1], recv_sem=recv_sems.at[a * (N_DEV - 1) + k - 1],
                    device_id=(px, py, pc), device_id_type=pl.DeviceIdType.MESH).start()
        token[...] = jnp.zeros_like(token)

    hbm = lambda arrs: [pltpu.HBM(a.shape, a.dtype) for a in arrs]
    out = pl.pallas_call(
        body, name=name,
        out_shape=(pltpu.SemaphoreType.DMA((n * (N_DEV - 1),)), pltpu.SemaphoreType.DMA((n * (N_DEV - 1),)),
                   *hbm(srcs), *hbm(lands), jax.ShapeDtypeStruct((SUBLANES, LANES), F32)),
        in_specs=[_HBM] * (2 * n),
        out_specs=(_SEM, _SEM, *([_HBM] * (2 * n)), pl.BlockSpec(memory_space=pltpu.VMEM)),
        input_output_aliases={a: 2 + a for a in range(2 * n)},
        compiler_params=pltpu.CompilerParams(has_side_effects=pltpu.SideEffectType.DATAFLOW_SIDE_EFFECTING),
    )(*[pltpu.with_memory_space_constraint(s, pltpu.HBM) for s in srcs],
      *[pltpu.with_memory_space_constraint(l, pltpu.HBM) for l in lands])
    return (out[0], out[1], list(out[2:2 + n]), list(out[2 + n:2 + 2 * n])), out[2 + 2 * n]


def _exchange_wait(handle, scatter, after, name):
    send_sems, recv_sems, srcs, lands = handle
    n = len(srcs)

    def body(*refs):
        src_refs, land_refs = refs[:n], refs[n:2 * n]
        send_sems, recv_sems = refs[2 * n], refs[2 * n + 1]
        for a in range(n):
            for k in range(1, N_DEV):
                px, py, pc = _peer(k)
                peer = 4 * px + 2 * py + pc
                cp = pltpu.make_async_remote_copy(
                    src_ref=src_refs[a].at[peer] if scatter else src_refs[a],
                    dst_ref=land_refs[a].at[k - 1] if scatter else land_refs[a].at[peer],
                    send_sem=send_sems.at[a * (N_DEV - 1) + k - 1], recv_sem=recv_sems.at[a * (N_DEV - 1) + k - 1],
                    device_id=(px, py, pc), device_id_type=pl.DeviceIdType.MESH)
                cp.wait_send()
                cp.wait_recv()

    out = pl.pallas_call(
        body, name=name,
        out_shape=tuple(pltpu.HBM(a.shape, a.dtype) for a in srcs + lands),
        in_specs=[_HBM] * (2 * n) + [_SEM, _SEM, pl.BlockSpec(memory_space=pl.ANY)],
        out_specs=tuple([_HBM] * (2 * n)),
        input_output_aliases={a: a for a in range(2 * n)},
        compiler_params=pltpu.CompilerParams(has_side_effects=pltpu.SideEffectType.DATAFLOW_SIDE_EFFECTING),
    )(*srcs, *lands, send_sems, recv_sems, after)
    return list(out[:n]), list(out[n:])


def _pack(parts, dtype, row_mult):
    flat = jnp.concatenate([p.reshape(-1).astype(dtype) for p in parts])
    n = flat.shape[0]
    rows = -(-n // PACK_COLS)
    rows = -(-rows // row_mult) * row_mult
    flat = jnp.pad(flat, (0, rows * PACK_COLS - n))
    return flat.reshape(rows, PACK_COLS)


def _unpack(flat, shapes):
    out, off = [], 0
    for shp in shapes:
        n = math.prod(shp)
        out.append(flat[..., off:off + n].reshape(flat.shape[:-1] + tuple(shp)))
        off += n
    return out


_DIMS = {"nn": (((1,), (0,)), ((), ())), "nt": (((1,), (1,)), ((), ())), "tn": (((0,), (0,)), ((), ()))}


def _matmul(a, b, mode, out_dtype, name, tm=512, tn=512, tk=None):
    if mode == "nn":
        (M, K), (_, N) = a.shape, b.shape
    elif mode == "nt":
        (M, K), (N, _) = a.shape, b.shape
    else:
        (K, M), (_, N) = a.shape, b.shape
    tm, tn = min(tm, M), min(tn, N)
    tk = K if tk is None else min(tk, K)
    nk = K // tk
    assert M % tm == 0 and N % tn == 0 and K % tk == 0, (name, a.shape, b.shape)

    def body(a_ref, b_ref, o_ref, *scratch):
        p = lax.dot_general(a_ref[...].astype(BF16), b_ref[...].astype(BF16), _DIMS[mode],
                            preferred_element_type=F32)
        if nk == 1:
            o_ref[...] = p.astype(out_dtype)
        else:
            acc = scratch[0]
            k = pl.program_id(2)

            @pl.when(k == 0)
            def _():
                acc[...] = p

            @pl.when(k > 0)
            def _():
                acc[...] += p

            @pl.when(k == nk - 1)
            def _():
                o_ref[...] = acc[...].astype(out_dtype)

    a_spec = (pl.BlockSpec((tk, tm), lambda i, j, k: (k, i)) if mode == "tn"
              else pl.BlockSpec((tm, tk), lambda i, j, k: (i, k)))
    b_spec = (pl.BlockSpec((tn, tk), lambda i, j, k: (j, k)) if mode == "nt"
              else pl.BlockSpec((tk, tn), lambda i, j, k: (k, j)))
    return pl.pallas_call(
        body, name=name, grid=(M // tm, N // tn, nk),
        out_shape=jax.ShapeDtypeStruct((M, N), out_dtype),
        in_specs=[a_spec, b_spec],
        out_specs=pl.BlockSpec((tm, tn), lambda i, j, k: (i, j)),
        scratch_shapes=[pltpu.VMEM((tm, tn), F32)] if nk > 1 else [],
        compiler_params=_cparams(),
    )(a, b)


def _ada_fwd(c_all, ada_w, ada_b_cols):
    L, D, n = ada_w.shape

    def body(c_ref, w_ref, b_ref, o_ref):
        act = _silu(c_ref[...]).astype(BF16)
        o_ref[0] = jnp.dot(act, w_ref[0].astype(BF16), preferred_element_type=F32) + b_ref[0]

    return pl.pallas_call(
        body, name="ada_fwd", grid=(L,),
        out_shape=jax.ShapeDtypeStruct((L, N_DEV, n), F32),
        in_specs=[pl.BlockSpec((N_DEV, D), lambda l: (0, 0)),
                  pl.BlockSpec((1, D, n), lambda l: (l, 0, 0)),
                  pl.BlockSpec((1, 1, n), lambda l: (l, 0, 0))],
        out_specs=pl.BlockSpec((1, N_DEV, n), lambda l: (l, 0, 0)),
        compiler_params=_cparams(),
    )(c_all, ada_w, ada_b_cols)


def _ada_bwd(c_all_t, dmod_cols):
    D = c_all_t.shape[0]
    L, _, n = dmod_cols.shape

    def body(c_ref, d_ref, o_ref):
        act = _silu(c_ref[...])
        dm = d_ref[0]
        acc = act[:, 0:1] * dm[0:1, :]
        for b in range(1, N_DEV):
            acc = acc + act[:, b:b + 1] * dm[b:b + 1, :]
        o_ref[0] = acc

    return pl.pallas_call(
        body, name="ada_bwd", grid=(L,),
        out_shape=jax.ShapeDtypeStruct((L, D, n), F32),
        in_specs=[pl.BlockSpec((D, N_DEV), lambda l: (0, 0)),
                  pl.BlockSpec((1, N_DEV, n), lambda l: (l, 0, 0))],
        out_specs=pl.BlockSpec((1, D, n), lambda l: (l, 0, 0)),
        compiler_params=_cparams(),
    )(c_all_t, dmod_cols)


def _rope_tables(pos_col, inv_lane):
    S = pos_col.shape[0]
    T = _row_tile(S)
    half = QK_ROPE // 2

    def body(p_ref, f_ref, c_ref, up_ref, dn_ref):
        ang = p_ref[...] * f_ref[...]
        lane = lax.broadcasted_iota(jnp.int32, ang.shape, 1)
        first = (lane >= QK_NOPE) & (lane < QK_NOPE + half)
        second = (lane >= QK_NOPE + half) & (lane < QK_NOPE + QK_ROPE)
        cs, sn = jnp.cos(ang), jnp.sin(ang)
        c_ref[...] = jnp.where(first | second, cs, 1.0)
        up_ref[...] = jnp.where(first, -sn, 0.0)
        dn_ref[...] = jnp.where(second, sn, 0.0)

    tab = jax.ShapeDtypeStruct((S, HEAD_PAD), F32)
    return pl.pallas_call(
        body, name="rope_tables", grid=(S // T,),
        out_shape=(tab, tab, tab),
        in_specs=[_rows(T, 1), _const((1, HEAD_PAD))],
        out_specs=(_rows(T, HEAD_PAD),) * 3,
        compiler_params=_cparams(),
    )(pos_col, inv_lane)


def _rope(blk, ct, ut, dt):
    half = QK_ROPE // 2
    up = pltpu.roll(blk, HEAD_PAD - half, 1)
    dn = pltpu.roll(blk, half, 1)
    return blk * ct + up * ut + dn * dt


def _rope_t(d, ct, ut, dt):
    half = QK_ROPE // 2
    return d * ct + pltpu.roll(d * ut, half, 1) + pltpu.roll(d * dt, HEAD_PAD - half, 1)


def _pre_norm(x, g, scale, shift, name):
    S, D = x.shape
    T = _row_tile(S)

    def body(x_ref, g_ref, sc_ref, sh_ref, h_ref):
        xv = x_ref[...]
        rstd = lax.rsqrt(jnp.mean(xv * xv, axis=-1, keepdims=True) + EPS)
        h_ref[...] = ((xv * rstd) * g_ref[...] * (1.0 + sc_ref[...]) + sh_ref[...]).astype(BF16)

    return pl.pallas_call(
        body, name=name, grid=(S // T,),
        out_shape=jax.ShapeDtypeStruct((S, D), BF16),
        in_specs=[_rows(T, D), _const((1, D)), _const((1, D)), _const((1, D))],
        out_specs=_rows(T, D), compiler_params=_cparams(),
    )(x, g, scale, shift)


def _post_norm(x, y, g, gate, name):
    S, D = x.shape
    T = _row_tile(S)

    def body(x_ref, y_ref, g_ref, gt_ref, o_ref):
        yv = y_ref[...]
        rstd = lax.rsqrt(jnp.mean(yv * yv, axis=-1, keepdims=True) + EPS)
        o_ref[...] = x_ref[...] + gt_ref[...] * ((yv * rstd) * g_ref[...])

    return pl.pallas_call(
        body, name=name, grid=(S // T,),
        out_shape=jax.ShapeDtypeStruct((S, D), F32),
        in_specs=[_rows(T, D), _rows(T, D), _const((1, D)), _const((1, D))],
        out_specs=_rows(T, D), compiler_params=_cparams(),
    )(x, y, g, gate)


def _fold8(v):
    T, C = v.shape
    return v.reshape(T // SUBLANES, SUBLANES, C).sum(axis=0)


def _col_sums(n_sums, body_fn, ins, in_specs, outs, out_specs, S, T, widths, name):
    n_in, n_out = len(ins), len(outs)
    nt = S // T

    def body(*refs):
        in_refs = refs[:n_in]
        out_refs = refs[n_in:n_in + n_out]
        sum_refs = refs[n_in + n_out:n_in + n_out + n_sums]
        accs = refs[n_in + n_out + n_sums:]
        i = pl.program_id(0)
        terms = body_fn(in_refs, out_refs)

        @pl.when(i == 0)
        def _():
            for acc, t in zip(accs, terms):
                acc[...] = _fold8(t)

        @pl.when(i > 0)
        def _():
            for acc, t in zip(accs, terms):
                acc[...] += _fold8(t)

        @pl.when(i == nt - 1)
        def _():
            for acc, s_ref in zip(accs, sum_refs):
                s_ref[...] = jnp.sum(acc[...], axis=0, keepdims=True)

    return pl.pallas_call(
        body, name=name, grid=(nt,),
        out_shape=tuple(outs) + tuple(jax.ShapeDtypeStruct((1, w), F32) for w in widths),
        in_specs=in_specs,
        out_specs=tuple(out_specs) + tuple(_const((1, w)) for w in widths),
        scratch_shapes=[pltpu.VMEM((SUBLANES, w), F32) for w in widths],
        compiler_params=_cparams(),
    )(*ins)


def _post_norm_bwd(dxo, y, g, gate, name):
    S, D = y.shape
    T = _row_tile(S)

    def fn(ins, outs):
        dxo_ref, y_ref, g_ref, gt_ref = ins
        yv, dv = y_ref[...], dxo_ref[...]
        rstd = lax.rsqrt(jnp.mean(yv * yv, axis=-1, keepdims=True) + EPS)
        yh = yv * rstd
        dn = dv * gt_ref[...]
        dyh = dn * g_ref[...]
        outs[0][...] = (rstd * (dyh - yh * jnp.mean(dyh * yh, axis=-1, keepdims=True))).astype(BF16)
        return [dv * (yh * g_ref[...]), dn * yh]

    return _col_sums(2, fn, [dxo, y, g, gate],
                     [_rows(T, D), _rows(T, D), _const((1, D)), _const((1, D))],
                     [jax.ShapeDtypeStruct((S, D), BF16)], [_rows(T, D)], S, T, [D, D], name)


def _pre_norm_bwd(dh, x, dxo, g, scale, name):
    S, D = x.shape
    T = _row_tile(S)

    def fn(ins, outs):
        dh_ref, x_ref, dxo_ref, g_ref, sc_ref = ins
        xv, dv = x_ref[...], dh_ref[...]
        rstd = lax.rsqrt(jnp.mean(xv * xv, axis=-1, keepdims=True) + EPS)
        xh = xv * rstd
        dr = dv * (1.0 + sc_ref[...])
        dxh = dr * g_ref[...]
        outs[0][...] = dxo_ref[...] + rstd * (dxh - xh * jnp.mean(dxh * xh, axis=-1, keepdims=True))
        return [dv, dv * (xh * g_ref[...]), dr * xh]

    return _col_sums(3, fn, [dh, x, dxo, g, scale],
                     [_rows(T, D), _rows(T, D), _rows(T, D), _const((1, D)), _const((1, D))],
                     [jax.ShapeDtypeStruct((S, D), F32)], [_rows(T, D)], S, T, [D, D, D], name)


def _loss_head(x, target):
    S, D = x.shape
    T = _row_tile(S)
    nt = S // T

    def body(x_ref, t_ref, l_ref, dx_ref, acc):
        i = pl.program_id(0)
        e = x_ref[...] - t_ref[...]
        dx_ref[...] = e * (1.0 / D)
        part = _fold8(e * e)

        @pl.when(i == 0)
        def _():
            acc[...] = part

        @pl.when(i > 0)
        def _():
            acc[...] += part

        @pl.when(i == nt - 1)
        def _():
            tot = jnp.sum(jnp.sum(acc[...], axis=0, keepdims=True), axis=1, keepdims=True)
            l_ref[...] = jnp.broadcast_to(tot * (0.5 / D), (1, LANES))

    return pl.pallas_call(
        body, name="loss_head", grid=(nt,),
        out_shape=(jax.ShapeDtypeStruct((1, LANES), F32), jax.ShapeDtypeStruct((S, D), F32)),
        in_specs=[_rows(T, D), _rows(T, D)],
        out_specs=(_const((1, LANES)), _rows(T, D)),
        scratch_shapes=[pltpu.VMEM((SUBLANES, D), F32)],
        compiler_params=_cparams(),
    )(x, target)


CONV_ROWS = 64


def _conv_halo(K):
    return SUBLANES if K - 1 <= SUBLANES else 32


def _conv_fwd(u, w, b, K, name):
    S, C = u.shape
    KP = w.shape[0]
    T, HB, RS = min(512, S), _conv_halo(K), CONV_ROWS
    ratio = T // HB

    def body(u_ref, h_ref, w_ref, b_ref, o_ref, ext):
        i = pl.program_id(1)
        ext[0:HB, :] = jnp.where(i > 0, h_ref[...], 0.0)
        ext[HB:HB + T, :] = u_ref[...]
        for r0 in range(0, T, RS):
            acc = jnp.broadcast_to(b_ref[...], (RS, LANES))
            for k in range(K):
                off = HB - (K - 1) + k + r0
                acc = acc + w_ref[k:k + 1, :] * ext[off:off + RS, :]
            o_ref[r0:r0 + RS, :] = acc

    return pl.pallas_call(
        body, name=name, grid=(C // LANES, S // T),
        out_shape=jax.ShapeDtypeStruct((S, C), F32),
        in_specs=[pl.BlockSpec((T, LANES), lambda c, i: (i, c)),
                  pl.BlockSpec((HB, LANES), lambda c, i: (jnp.maximum(i * ratio - 1, 0), c)),
                  pl.BlockSpec((KP, LANES), lambda c, i: (0, c)),
                  pl.BlockSpec((1, LANES), lambda c, i: (0, c))],
        out_specs=pl.BlockSpec((T, LANES), lambda c, i: (i, c)),
        scratch_shapes=[pltpu.VMEM((HB + T, LANES), F32)],
        compiler_params=_cparams(),
    )(u, u, w, b)


def _conv_bwd(d, u, w, K, name):
    S, C = u.shape
    KP = w.shape[0]
    T, HB, RS = min(512, S), _conv_halo(K), CONV_ROWS
    ratio = T // HB
    nt = S // T
    last_halo = S // HB - 1

    def body(d_ref, dn_ref, u_ref, up_ref, w_ref, du_ref, dw_ref, db_ref, extd, extu, dws, dbs):
        i = pl.program_id(1)
        extd[0:T, :] = d_ref[...]
        extd[T:T + HB, :] = jnp.where(i < nt - 1, dn_ref[...], 0.0)
        extu[0:HB, :] = jnp.where(i > 0, up_ref[...], 0.0)
        extu[HB:HB + T, :] = u_ref[...]

        @pl.when(i == 0)
        def _():
            dws[...] = jnp.zeros_like(dws)
            dbs[...] = jnp.zeros_like(dbs)

        for r0 in range(0, T, RS):
            acc = jnp.zeros((RS, LANES), F32)
            for k in range(K):
                off = (K - 1 - k) + r0
                acc = acc + w_ref[k:k + 1, :] * extd[off:off + RS, :]
            du_ref[r0:r0 + RS, :] = acc
            dch = d_ref[r0:r0 + RS, :]
            dbs[...] += _fold8(dch)
            for k in range(K):
                off = HB - (K - 1) + k + r0
                dws[k * SUBLANES:(k + 1) * SUBLANES, :] += _fold8(dch * extu[off:off + RS, :])

        @pl.when(i == nt - 1)
        def _():
            dw_ref[...] = jnp.zeros_like(dw_ref)
            for k in range(K):
                dw_ref[k:k + 1, :] = jnp.sum(dws[k * SUBLANES:(k + 1) * SUBLANES, :], axis=0, keepdims=True)
            db_ref[...] = jnp.sum(dbs[...], axis=0, keepdims=True)

    return pl.pallas_call(
        body, name=name, grid=(C // LANES, nt),
        out_shape=(jax.ShapeDtypeStruct((S, C), F32), jax.ShapeDtypeStruct((KP, C), F32),
                   jax.ShapeDtypeStruct((1, C), F32)),
        in_specs=[pl.BlockSpec((T, LANES), lambda c, i: (i, c)),
                  pl.BlockSpec((HB, LANES), lambda c, i: (jnp.minimum((i + 1) * ratio, last_halo), c)),
                  pl.BlockSpec((T, LANES), lambda c, i: (i, c)),
                  pl.BlockSpec((HB, LANES), lambda c, i: (jnp.maximum(i * ratio - 1, 0), c)),
                  pl.BlockSpec((KP, LANES), lambda c, i: (0, c))],
        out_specs=(pl.BlockSpec((T, LANES), lambda c, i: (i, c)),
                   pl.BlockSpec((KP, LANES), lambda c, i: (0, c)),
                   pl.BlockSpec((1, LANES), lambda c, i: (0, c))),
        scratch_shapes=[pltpu.VMEM((T + HB, LANES), F32), pltpu.VMEM((HB + T, LANES), F32),
                        pltpu.VMEM((KP * SUBLANES, LANES), F32), pltpu.VMEM((SUBLANES, LANES), F32)],
        compiler_params=_cparams(),
    )(d, d, u, u, w)


SCW = 512
ZE = 3072
QL = 256
KVL = 128


def _rms_rows(x, g):
    rstd = lax.rsqrt(jnp.mean(x * x, axis=-1, keepdims=True) + EPS)
    return (x * rstd) * g


def _even_pre(z, qg, kvg, name):
    S = z.shape[0]
    T = _row_tile(S)

    def body(ac_ref, ax_ref, cq_ref, ckv_ref, qg_ref, kvg_ref, u_ref, qn_ref, kvn_ref):
        u_ref[...] = ac_ref[...] * ax_ref[...]
        qn_ref[...] = _rms_rows(cq_ref[...], qg_ref[...]).astype(BF16)
        kvn_ref[...] = _rms_rows(ckv_ref[...], kvg_ref[...]).astype(BF16)

    return pl.pallas_call(
        body, name=name, grid=(S // T,),
        out_shape=(jax.ShapeDtypeStruct((S, SCW), F32), jax.ShapeDtypeStruct((S, QL), BF16),
                   jax.ShapeDtypeStruct((S, KVL), BF16)),
        in_specs=[_rows(T, SCW, 1), _rows(T, SCW, 2), _rows(T, QL, 10), _rows(T, KVL, 22),
                  _const((1, QL)), _const((1, KVL))],
        out_specs=(_rows(T, SCW), _rows(T, QL), _rows(T, KVL)),
        compiler_params=_cparams(),
    )(z, z, z, z, qg, kvg)


def _qkv_fwd(qn, kvn, z, tabs, w_q, w_kv, name):
    S = qn.shape[0]
    T = _row_tile(S)
    HW = HEADS * HEAD_PAD
    scale = 1.0 / math.sqrt(QK_NOPE + QK_ROPE)

    def body(qn_ref, kvn_ref, kr_ref, ct_ref, ut_ref, dt_ref, wq_ref, wkv_ref, q_ref, k_ref, v_ref):
        ct, ut, dt = ct_ref[...], ut_ref[...], dt_ref[...]
        qa = jnp.dot(qn_ref[...], wq_ref[...], preferred_element_type=F32)
        kva = jnp.dot(kvn_ref[...], wkv_ref[...], preferred_element_type=F32)
        kr = kr_ref[...]
        for h in range(HEADS):
            sl = slice(h * HEAD_PAD, (h + 1) * HEAD_PAD)
            q_ref[:, sl] = (_rope(qa[:, sl], ct, ut, dt) * scale).astype(BF16)
            k_ref[:, sl] = _rope(kva[:, sl] + kr, ct, ut, dt).astype(BF16)
        v_ref[...] = kva[:, HW:].astype(BF16)

    return pl.pallas_call(
        body, name=name, grid=(S // T,),
        out_shape=(jax.ShapeDtypeStruct((S, HW), BF16), jax.ShapeDtypeStruct((S, HW), BF16),
                   jax.ShapeDtypeStruct((S, HEADS * V_HEAD), BF16)),
        in_specs=[_rows(T, QL), _rows(T, KVL), _rows(T, HEAD_PAD, 23),
                  _rows(T, HEAD_PAD), _rows(T, HEAD_PAD), _rows(T, HEAD_PAD),
                  _const(w_q.shape), _const(w_kv.shape)],
        out_specs=(_rows(T, HW), _rows(T, HW), _rows(T, HEADS * V_HEAD)),
        compiler_params=_cparams(),
    )(qn, kvn, z, *tabs, w_q, w_kv)


def _attn_tile(S):
    return min(256, S)


def _chunk_mask(TQ):
    r = lax.broadcasted_iota(jnp.int32, (TQ, TQ), 0) // CHUNK
    c = lax.broadcasted_iota(jnp.int32, (TQ, TQ), 1) // CHUNK
    return c <= r


_NT = (((1,), (1,)), ((), ()))
_TN = (((0,), (0,)), ((), ()))


def _attn_fwd(q, k, v, name):
    S = q.shape[0]
    TQ = _attn_tile(S)
    nq = S // TQ
    PW = 2 * HEAD_PAD

    def body(q_ref, k_ref, v_ref, o_ref, lse_ref, m_s, l_s, acc_s):
        i = pl.program_id(1)
        left = lax.broadcasted_iota(jnp.int32, (TQ, LANES), 1) < V_HEAD
        m_s[...] = jnp.full_like(m_s, NEG)
        l_s[...] = jnp.zeros_like(l_s)
        acc_s[...] = jnp.zeros_like(acc_s)
        qv = q_ref[...]

        def step(j, masked):
            r0 = pl.multiple_of(j * TQ, TQ)
            kb = k_ref[pl.ds(r0, TQ), :]
            vb = v_ref[pl.ds(r0, TQ), :]
            alphas, pvs = [], []
            for h in range(2):
                sl = slice(h * HEAD_PAD, (h + 1) * HEAD_PAD)
                s = lax.dot_general(qv[:, sl], kb[:, sl], _NT, preferred_element_type=F32)
                if masked:
                    s = jnp.where(_chunk_mask(TQ), s, NEG)
                m_prev = m_s[h]
                m_new = jnp.maximum(m_prev, jnp.max(s, axis=1, keepdims=True))
                alpha = jnp.exp(m_prev - m_new)
                p = jnp.exp(s - m_new[:, 0:1])
                l_s[h] = alpha * l_s[h] + jnp.sum(p, axis=1, keepdims=True)
                m_s[h] = m_new
                alphas.append(alpha)
                pvs.append(jnp.dot(p.astype(BF16), vb, preferred_element_type=F32))
            acc_s[...] = acc_s[...] * jnp.where(left, alphas[0], alphas[1]) + jnp.where(left, pvs[0], pvs[1])

        def loop_body(j, carry):
            step(j, False)
            return carry

        lax.fori_loop(0, i, loop_body, 0)
        step(i, True)
        o_ref[...] = acc_s[...] / jnp.where(left, l_s[0], l_s[1])
        lse_ref[...] = jnp.where(left, m_s[0] + jnp.log(l_s[0]), m_s[1] + jnp.log(l_s[1]))

    return pl.pallas_call(
        body, name=name, grid=(HEADS // 2, nq),
        out_shape=(jax.ShapeDtypeStruct((S, HEADS * V_HEAD), F32), jax.ShapeDtypeStruct((S, HEADS * V_HEAD), F32)),
        in_specs=[pl.BlockSpec((TQ, PW), lambda p, i: (i, p)),
                  pl.BlockSpec((S, PW), lambda p, i: (0, p)),
                  pl.BlockSpec((S, LANES), lambda p, i: (0, p))],
        out_specs=(pl.BlockSpec((TQ, LANES), lambda p, i: (i, p)),
                   pl.BlockSpec((TQ, LANES), lambda p, i: (i, p))),
        scratch_shapes=[pltpu.VMEM((2, TQ, LANES), F32), pltpu.VMEM((2, TQ, LANES), F32),
                        pltpu.VMEM((TQ, LANES), F32)],
        compiler_params=_cparams(),
    )(q, k, v)


def _attn_dq(q, k, v, do, lse, delta, name):
    S = q.shape[0]
    TQ = _attn_tile(S)
    nq = S // TQ
    PW = 2 * HEAD_PAD

    def body(q_ref, k_ref, v_ref, do_ref, lse_ref, dl_ref, dq_ref, acc_s):
        i = pl.program_id(1)
        left = lax.broadcasted_iota(jnp.int32, (TQ, LANES), 1) < V_HEAD
        acc_s[...] = jnp.zeros_like(acc_s)
        qv = q_ref[...]
        dov = do_ref[...]
        dos = [jnp.where(left, dov, jnp.zeros_like(dov)), jnp.where(left, jnp.zeros_like(dov), dov)]
        lses = [lse_ref[:, 0:1], lse_ref[:, V_HEAD:V_HEAD + 1]]
        dls = [dl_ref[:, 0:1], dl_ref[:, V_HEAD:V_HEAD + 1]]

        def step(j, masked):
            r0 = pl.multiple_of(j * TQ, TQ)
            kb = k_ref[pl.ds(r0, TQ), :]
            vb = v_ref[pl.ds(r0, TQ), :]
            for h in range(2):
                sl = slice(h * HEAD_PAD, (h + 1) * HEAD_PAD)
                s = lax.dot_general(qv[:, sl], kb[:, sl], _NT, preferred_element_type=F32)
                p = jnp.exp(s - lses[h])
                if masked:
                    p = jnp.where(_chunk_mask(TQ), p, 0.0)
                dp = lax.dot_general(dos[h], vb, _NT, preferred_element_type=F32)
                ds = (p * (dp - dls[h])).astype(BF16)
                acc_s[:, sl] += jnp.dot(ds, kb[:, sl], preferred_element_type=F32)

        def loop_body(j, carry):
            step(j, False)
            return carry

        lax.fori_loop(0, i, loop_body, 0)
        step(i, True)
        dq_ref[...] = acc_s[...]

    return pl.pallas_call(
        body, name=name, grid=(HEADS // 2, nq),
        out_shape=jax.ShapeDtypeStruct((S, HEADS * HEAD_PAD), F32),
        in_specs=[pl.BlockSpec((TQ, PW), lambda p, i: (i, p)),
                  pl.BlockSpec((S, PW), lambda p, i: (0, p)),
                  pl.BlockSpec((S, LANES), lambda p, i: (0, p)),
                  pl.BlockSpec((TQ, LANES), lambda p, i: (i, p)),
                  pl.BlockSpec((TQ, LANES), lambda p, i: (i, p)),
                  pl.BlockSpec((TQ, LANES), lambda p, i: (i, p))],
        out_specs=pl.BlockSpec((TQ, PW), lambda p, i: (i, p)),
        scratch_shapes=[pltpu.VMEM((TQ, PW), F32)],
        compiler_params=_cparams(),
    )(q, k, v, do, lse, delta)


def _attn_dkv(q, k, v, do, lse, delta, name):
    S = q.shape[0]
    TQ = _attn_tile(S)
    nq = S // TQ
    PW = 2 * HEAD_PAD

    def body(q_ref, k_ref, v_ref, do_ref, lse_ref, dl_ref, dk_ref, dv_ref, dk_s, dv_s):
        j = pl.program_id(1)
        left = lax.broadcasted_iota(jnp.int32, (TQ, LANES), 1) < V_HEAD
        dk_s[...] = jnp.zeros_like(dk_s)
        dv_s[...] = jnp.zeros_like(dv_s)
        kb = k_ref[...]
        vb = v_ref[...]

        def step(i, masked):
            r0 = pl.multiple_of(i * TQ, TQ)
            qb = q_ref[pl.ds(r0, TQ), :]
            dov = do_ref[pl.ds(r0, TQ), :]
            lse = lse_ref[pl.ds(r0, TQ), :]
            dl = dl_ref[pl.ds(r0, TQ), :]
            dos = [jnp.where(left, dov, jnp.zeros_like(dov)), jnp.where(left, jnp.zeros_like(dov), dov)]
            for h in range(2):
                sl = slice(h * HEAD_PAD, (h + 1) * HEAD_PAD)
                c0 = h * V_HEAD
                s = lax.dot_general(qb[:, sl], kb[:, sl], _NT, preferred_element_type=F32)
                p = jnp.exp(s - lse[:, c0:c0 + 1])
                if masked:
                    p = jnp.where(_chunk_mask(TQ), p, 0.0)
                dv_s[...] += lax.dot_general(p.astype(BF16), dos[h], _TN, preferred_element_type=F32)
                dp = lax.dot_general(dos[h], vb, _NT, preferred_element_type=F32)
                ds = (p * (dp - dl[:, c0:c0 + 1])).astype(BF16)
                dk_s[:, sl] += lax.dot_general(ds, qb[:, sl], _TN, preferred_element_type=F32)

        def loop_body(i, carry):
            step(i, False)
            return carry

        step(j, True)
        lax.fori_loop(j + 1, nq, loop_body, 0)
        dk_ref[...] = dk_s[...]
        dv_ref[...] = dv_s[...]

    return pl.pallas_call(
        body, name=name, grid=(HEADS // 2, nq),
        out_shape=(jax.ShapeDtypeStruct((S, HEADS * HEAD_PAD), F32), jax.ShapeDtypeStruct((S, HEADS * V_HEAD), F32)),
        in_specs=[pl.BlockSpec((S, PW), lambda p, j: (0, p)),
                  pl.BlockSpec((TQ, PW), lambda p, j: (j, p)),
                  pl.BlockSpec((TQ, LANES), lambda p, j: (j, p)),
                  pl.BlockSpec((S, LANES), lambda p, j: (0, p)),
                  pl.BlockSpec((S, LANES), lambda p, j: (0, p)),
                  pl.BlockSpec((S, LANES), lambda p, j: (0, p))],
        out_specs=(pl.BlockSpec((TQ, PW), lambda p, j: (j, p)),
                   pl.BlockSpec((TQ, LANES), lambda p, j: (j, p))),
        scratch_shapes=[pltpu.VMEM((TQ, PW), F32), pltpu.VMEM((TQ, LANES), F32)],
        compiler_params=_cparams(),
    )(q, k, v, do, lse, delta)


ATTN_FWD_HEADS = 8
ATTN_BWD_HEADS = 4


def _chunk_mask_t(T):
    key = lax.broadcasted_iota(jnp.int32, (T, T), 0) // CHUNK
    qry = lax.broadcasted_iota(jnp.int32, (T, T), 1) // CHUNK
    return key <= qry


def _qkv_fwd_t(qn, kvn, z, tabs, w_q, w_kv, name):
    S = qn.shape[0]
    T = _attn_tile(S)
    HW = HEADS * HEAD_PAD
    scale = 1.0 / math.sqrt(QK_NOPE + QK_ROPE)

    def body(qn_ref, kvn_ref, kr_ref, ct_ref, ut_ref, dt_ref, wq_ref, wkv_ref, q_ref, k_ref, v_ref, kt_ref, vt_ref):
        ct, ut, dt = ct_ref[...], ut_ref[...], dt_ref[...]
        qa = jnp.dot(qn_ref[...], wq_ref[...], preferred_element_type=F32)
        kva = jnp.dot(kvn_ref[...], wkv_ref[...], preferred_element_type=F32)
        kr = kr_ref[...]
        ones_row = (lax.broadcasted_iota(jnp.int32, (V_HEAD, T), 0) == 0).astype(F32)
        for h in range(HEADS):
            sl = slice(h * HEAD_PAD, (h + 1) * HEAD_PAD)
            q_ref[:, sl] = (_rope(qa[:, sl], ct, ut, dt) * scale).astype(BF16)
            kh = _rope(kva[:, sl] + kr, ct, ut, dt)
            k_ref[:, sl] = kh.astype(BF16)
            kt_ref[0, sl, :] = kh.T.astype(BF16)
        v_ref[...] = kva[:, HW:].astype(BF16)
        for p in range(HEADS // 2):
            vpt = kva[:, HW + p * LANES:HW + (p + 1) * LANES].T
            for h in range(2):
                r0 = (2 * p + h) * HEAD_PAD
                vt_ref[0, r0:r0 + V_HEAD, :] = vpt[h * V_HEAD:(h + 1) * V_HEAD, :].astype(BF16)
                vt_ref[0, r0 + V_HEAD:r0 + HEAD_PAD, :] = ones_row.astype(BF16)

    t3 = jax.ShapeDtypeStruct((S // T, HW, T), BF16)
    return pl.pallas_call(
        body, name=name, grid=(S // T,),
        out_shape=(jax.ShapeDtypeStruct((S, HW), BF16), jax.ShapeDtypeStruct((S, HW), BF16),
                   jax.ShapeDtypeStruct((S, HEADS * V_HEAD), BF16), t3, t3),
        in_specs=[_rows(T, QL), _rows(T, KVL), _rows(T, HEAD_PAD, 23),
                  _rows(T, HEAD_PAD), _rows(T, HEAD_PAD), _rows(T, HEAD_PAD),
                  _const(w_q.shape), _const(w_kv.shape)],
        out_specs=(_rows(T, HW), _rows(T, HW), _rows(T, HEADS * V_HEAD),
                   pl.BlockSpec((1, HW, T), lambda i: (i, 0, 0)), pl.BlockSpec((1, HW, T), lambda i: (i, 0, 0))),
        compiler_params=_cparams(),
    )(qn, kvn, z, *tabs, w_q, w_kv)


def _attn_fwd_t(q, k, vT3, name):
    S = q.shape[0]
    T = _attn_tile(S)
    nq = S // T
    NH = ATTN_FWD_HEADS
    PW = NH * HEAD_PAD

    def body(q_ref, k_ref, vt_ref, o_ref, lse_ref, m_s, acc_s):
        i = pl.program_id(1)
        m_s[...] = jnp.full_like(m_s, NEG)
        acc_s[...] = jnp.zeros_like(acc_s)
        qv = q_ref[...]

        def step(j, masked):
            kb = k_ref[pl.ds(pl.multiple_of(j * T, T), T), :]
            vt = vt_ref[j]
            heads = [slice(h * HEAD_PAD, (h + 1) * HEAD_PAD) for h in range(NH)]
            sts = [lax.dot_general(kb[:, sl], qv[:, sl], _NT, preferred_element_type=F32) for sl in heads]
            alphas, pvs = [], []
            for h, sl in enumerate(heads):
                st = jnp.where(_chunk_mask_t(T), sts[h], NEG) if masked else sts[h]
                m_prev = m_s[h]
                m_new = jnp.maximum(m_prev, jnp.max(st, axis=0, keepdims=True))
                alphas.append(jnp.exp(m_prev[0:1] - m_new[0:1]))
                pt = jnp.exp(st - m_new[0:1]).astype(BF16)
                m_s[h] = m_new
                pvs.append(jnp.dot(vt[sl, :], pt, preferred_element_type=F32))
            for h in range(NH):
                acc_s[h] = acc_s[h] * alphas[h] + pvs[h]

        def loop_body(j, carry):
            step(j, False)
            return carry

        lax.fori_loop(0, i, loop_body, 0)
        step(i, True)
        for g in range(NH // 2):
            outs = []
            for h in (2 * g, 2 * g + 1):
                acc = acc_s[h]
                l_row = acc[V_HEAD:V_HEAD + 1, :]
                outs.append(acc[0:V_HEAD, :] / l_row)
                lse_ref[0, h * SUBLANES:(h + 1) * SUBLANES, :] = m_s[h] + jnp.log(l_row)
            o_ref[:, g * LANES:(g + 1) * LANES] = jnp.concatenate(outs, axis=0).T

    return pl.pallas_call(
        body, name=name, grid=(HEADS // NH, nq),
        out_shape=(jax.ShapeDtypeStruct((S, HEADS * V_HEAD), F32),
                   jax.ShapeDtypeStruct((nq, HEADS * SUBLANES, T), F32)),
        in_specs=[pl.BlockSpec((T, PW), lambda p, i: (i, p)),
                  pl.BlockSpec((S, PW), lambda p, i: (0, p)),
                  pl.BlockSpec((nq, PW, T), lambda p, i: (0, p, 0))],
        out_specs=(pl.BlockSpec((T, NH * V_HEAD), lambda p, i: (i, p)),
                   pl.BlockSpec((1, NH * SUBLANES, T), lambda p, i: (i, p, 0))),
        scratch_shapes=[pltpu.VMEM((NH, SUBLANES, T), F32), pltpu.VMEM((NH, HEAD_PAD, T), F32)],
        compiler_params=_cparams(),
    )(q, k, vT3)


def _attn_bwd_t(q, k, v, kT3, do, lse3, dl3, name):
    S = q.shape[0]
    T = _attn_tile(S)
    nq = S // T
    NH = ATTN_BWD_HEADS
    PW = NH * HEAD_PAD
    VW = NH * V_HEAD

    def body(q_ref, k_ref, v_ref, kt_ref, do_ref, lse_ref, dl_ref, dq_ref, dk_ref, dv_ref, dk_s, dv_s):
        j = pl.program_id(1)
        left = lax.broadcasted_iota(jnp.int32, (T, LANES), 1) < V_HEAD

        @pl.when(j == 0)
        def _():
            dq_ref[...] = jnp.zeros_like(dq_ref)

        dk_s[...] = jnp.zeros_like(dk_s)
        dv_s[...] = jnp.zeros_like(dv_s)
        kb = k_ref[...]
        vms = []
        for g in range(NH // 2):
            vb = v_ref[:, g * LANES:(g + 1) * LANES]
            vms += [jnp.where(left, vb, jnp.zeros_like(vb)), jnp.where(left, jnp.zeros_like(vb), vb)]
        kt = kt_ref[0]

        def step(i, masked):
            r0 = pl.multiple_of(i * T, T)
            qb = q_ref[pl.ds(r0, T), :]
            do_all = do_ref[pl.ds(r0, T), :]
            lse = lse_ref[i]
            dl = dl_ref[i]
            heads = [slice(h * HEAD_PAD, (h + 1) * HEAD_PAD) for h in range(NH)]
            dobs = [do_all[:, (h // 2) * LANES:(h // 2 + 1) * LANES] for h in range(NH)]
            sts = [lax.dot_general(kb[:, sl], qb[:, sl], _NT, preferred_element_type=F32) for sl in heads]
            dpts = [lax.dot_general(vms[h], dobs[h], _NT, preferred_element_type=F32) for h in range(NH)]
            res = []
            for h, sl in enumerate(heads):
                r8 = h * SUBLANES
                pt = jnp.exp(sts[h] - lse[r8:r8 + 1, :])
                if masked:
                    pt = jnp.where(_chunk_mask_t(T), pt, 0.0)
                dst = (pt * (dpts[h] - dl[r8:r8 + 1, :])).astype(BF16)
                res.append((jnp.dot(pt.astype(BF16), dobs[h], preferred_element_type=F32),
                            jnp.dot(dst, qb[:, sl], preferred_element_type=F32),
                            jnp.dot(kt[sl, :], dst, preferred_element_type=F32)))
            for h, sl in enumerate(heads):
                dv_s[h] += res[h][0]
                dk_s[:, sl] += res[h][1]
                dq_ref[i, sl, :] += res[h][2]

        def loop_body(i, carry):
            step(i, False)
            return carry

        step(j, True)
        lax.fori_loop(j + 1, nq, loop_body, 0)
        dk_ref[...] = dk_s[...]
        for g in range(NH // 2):
            dv_ref[:, g * LANES:(g + 1) * LANES] = jnp.where(left, dv_s[2 * g], dv_s[2 * g + 1])

    return pl.pallas_call(
        body, name=name, grid=(HEADS // NH, nq),
        out_shape=(jax.ShapeDtypeStruct((nq, HEADS * HEAD_PAD, T), F32),
                   jax.ShapeDtypeStruct((S, HEADS * HEAD_PAD), F32), jax.ShapeDtypeStruct((S, HEADS * V_HEAD), F32)),
        in_specs=[pl.BlockSpec((S, PW), lambda p, j: (0, p)),
                  pl.BlockSpec((T, PW), lambda p, j: (j, p)),
                  pl.BlockSpec((T, VW), lambda p, j: (j, p)),
                  pl.BlockSpec((1, PW, T), lambda p, j: (j, p, 0)),
                  pl.BlockSpec((S, VW), lambda p, j: (0, p)),
                  pl.BlockSpec((nq, NH * SUBLANES, T), lambda p, j: (0, p, 0)),
                  pl.BlockSpec((nq, NH * SUBLANES, T), lambda p, j: (0, p, 0))],
        out_specs=(pl.BlockSpec((nq, PW, T), lambda p, j: (0, p, 0)),
                   pl.BlockSpec((T, PW), lambda p, j: (j, p)),
                   pl.BlockSpec((T, VW), lambda p, j: (j, p))),
        scratch_shapes=[pltpu.VMEM((T, PW), F32), pltpu.VMEM((NH, T, LANES), F32)],
        compiler_params=_cparams(),
    )(q, k, v, kT3, do, lse3, dl3)


def _even_post(z, cv, o, name):
    S = z.shape[0]
    T = _row_tile(S)

    def body(ab_ref, ag_ref, bg_ref, cv_ref, o_ref, y_ref):
        y_ref[:, 0:SCW] = (ab_ref[...] * cv_ref[...] * _silu(ag_ref[...])).astype(BF16)
        y_ref[:, SCW:2 * SCW] = (o_ref[...] * _silu(bg_ref[...])).astype(BF16)

    return pl.pallas_call(
        body, name=name, grid=(S // T,),
        out_shape=jax.ShapeDtypeStruct((S, 2 * SCW), BF16),
        in_specs=[_rows(T, SCW, 0), _rows(T, SCW, 3), _rows(T, SCW, 4), _rows(T, SCW), _rows(T, SCW)],
        out_specs=_rows(T, 2 * SCW), compiler_params=_cparams(),
    )(z, z, z, cv, o)


def _even_bwd_gates(dyc, z, cv, o, name):
    S = z.shape[0]
    T = _row_tile(S)

    def body(dya_ref, dyb_ref, ab_ref, ag_ref, bg_ref, cv_ref, o_ref,
             dab_ref, dag_ref, dbg_ref, dcv_ref, do_ref, dl_ref):
        dya, ab, ag, cv = dya_ref[...], ab_ref[...], ag_ref[...], cv_ref[...]
        sg = _silu(ag)
        dab_ref[...] = (dya * cv * sg).astype(BF16)
        dcv_ref[...] = dya * ab * sg
        dag_ref[...] = (dya * ab * cv * _dsilu(ag)).astype(BF16)
        dyb, bg, ov = dyb_ref[...], bg_ref[...], o_ref[...]
        dov = dyb * _silu(bg)
        do_ref[...] = dov.astype(BF16)
        dbg_ref[...] = (dyb * ov * _dsilu(bg)).astype(BF16)
        prod = dov * ov
        left = lax.broadcasted_iota(jnp.int32, (T, LANES), 1) < V_HEAD
        for p in range(HEADS // 2):
            blk = prod[:, p * LANES:(p + 1) * LANES]
            s0 = jnp.sum(jnp.where(left, blk, 0.0), axis=1, keepdims=True)
            s1 = jnp.sum(jnp.where(left, 0.0, blk), axis=1, keepdims=True)
            dt = jnp.where(left, s0, s1).T
            dl_ref[0, 2 * p * SUBLANES:(2 * p + 1) * SUBLANES, :] = dt[0:SUBLANES, :]
            dl_ref[0, (2 * p + 1) * SUBLANES:(2 * p + 2) * SUBLANES, :] = dt[V_HEAD:V_HEAD + SUBLANES, :]

    assert T == _attn_tile(S)
    bf = jax.ShapeDtypeStruct((S, SCW), BF16)
    ff = jax.ShapeDtypeStruct((S, SCW), F32)
    return pl.pallas_call(
        body, name=name, grid=(S // T,),
        out_shape=(bf, bf, bf, ff, bf, jax.ShapeDtypeStruct((S // T, HEADS * SUBLANES, T), F32)),
        in_specs=[_rows(T, SCW, 0), _rows(T, SCW, 1), _rows(T, SCW, 0), _rows(T, SCW, 3), _rows(T, SCW, 4),
                  _rows(T, SCW), _rows(T, SCW)],
        out_specs=(_rows(T, SCW),) * 5 + (pl.BlockSpec((1, HEADS * SUBLANES, T), lambda i: (i, 0, 0)),),
        compiler_params=_cparams(),
    )(dyc, dyc, z, z, z, cv, o)


def _qkv_bwd(dq, dk, dv, z, tabs, w_q, w_kv, qg, kvg, name):
    S = dk.shape[0]
    T = _attn_tile(S)
    HW = HEADS * HEAD_PAD
    VW = HEADS * V_HEAD
    scale = 1.0 / math.sqrt(QK_NOPE + QK_ROPE)

    def fn(ins, outs):
        dq_ref, dk_ref, dv_ref, cq_ref, ckv_ref, ct_ref, ut_ref, dt_ref, wq_ref, wkv_ref, qg_ref, kvg_ref = ins
        dqp_ref, dkvp_ref, dcq_ref, dckv_ref, dkr_ref = outs
        ct, ut, dt = ct_ref[...], ut_ref[...], dt_ref[...]
        dkr = jnp.zeros((T, HEAD_PAD), F32)
        for h in range(HEADS):
            sl = slice(h * HEAD_PAD, (h + 1) * HEAD_PAD)
            dqp_ref[:, sl] = (_rope_t(dq_ref[0, sl, :].T, ct, ut, dt) * scale).astype(BF16)
            dkh = _rope_t(dk_ref[:, sl], ct, ut, dt)
            dkr = dkr + dkh
            dkvp_ref[:, sl] = dkh.astype(BF16)
        dkvp_ref[:, HW:] = dv_ref[...].astype(BF16)
        dkr_ref[...] = dkr.astype(BF16)
        sums = []
        for lat_ref, g_ref, dpre_ref, w_ref, dlat_ref in ((cq_ref, qg_ref, dqp_ref, wq_ref, dcq_ref),
                                                         (ckv_ref, kvg_ref, dkvp_ref, wkv_ref, dckv_ref)):
            dn = lax.dot_general(dpre_ref[...], w_ref[...], _NT, preferred_element_type=F32)
            xv = lat_ref[...]
            rstd = lax.rsqrt(jnp.mean(xv * xv, axis=-1, keepdims=True) + EPS)
            xh = xv * rstd
            dxh = dn * g_ref[...]
            dlat_ref[...] = (rstd * (dxh - xh * jnp.mean(dxh * xh, axis=-1, keepdims=True))).astype(BF16)
            sums.append(dn * xh)
        return sums

    return _col_sums(
        2, fn, [dq, dk, dv, z, z, *tabs, w_q, w_kv, qg, kvg],
        [pl.BlockSpec((1, HW, T), lambda i: (i, 0, 0)), _rows(T, HW), _rows(T, VW), _rows(T, QL, 10), _rows(T, KVL, 22),
         _rows(T, HEAD_PAD), _rows(T, HEAD_PAD), _rows(T, HEAD_PAD),
         _const(w_q.shape), _const(w_kv.shape), _const((1, QL)), _const((1, KVL))],
        [jax.ShapeDtypeStruct((S, HW), BF16), jax.ShapeDtypeStruct((S, HW + VW), BF16),
         jax.ShapeDtypeStruct((S, QL), BF16), jax.ShapeDtypeStruct((S, KVL), BF16),
         jax.ShapeDtypeStruct((S, HEAD_PAD), BF16)],
        [_rows(T, HW), _rows(T, HW + VW), _rows(T, QL), _rows(T, KVL), _rows(T, HEAD_PAD)],
        S, T, [QL, KVL], name)


def _even_dz(dab, du, z, dag, dbg, dcq, dckv, dkr, name):
    S = z.shape[0]
    T = _row_tile(S)

    def body(dab_ref, du_ref, ac_ref, ax_ref, dag_ref, dbg_ref, dcq_ref, dckv_ref, dkr_ref, dz_ref):
        duv = du_ref[...]
        dz_ref[:, 0:SCW] = dab_ref[...]
        dz_ref[:, SCW:2 * SCW] = (duv * ax_ref[...]).astype(BF16)
        dz_ref[:, 2 * SCW:3 * SCW] = (duv * ac_ref[...]).astype(BF16)
        dz_ref[:, 3 * SCW:4 * SCW] = dag_ref[...]
        dz_ref[:, 4 * SCW:5 * SCW] = dbg_ref[...]
        dz_ref[:, 5 * SCW:5 * SCW + QL] = dcq_ref[...]
        dz_ref[:, 5 * SCW + QL:5 * SCW + QL + KVL] = dckv_ref[...]
        dz_ref[:, 5 * SCW + QL + KVL:ZE] = dkr_ref[...]

    return pl.pallas_call(
        body, name=name, grid=(S // T,),
        out_shape=jax.ShapeDtypeStruct((S, ZE), BF16),
        in_specs=[_rows(T, SCW), _rows(T, SCW), _rows(T, SCW, 1), _rows(T, SCW, 2), _rows(T, SCW), _rows(T, SCW),
                  _rows(T, QL), _rows(T, KVL), _rows(T, HEAD_PAD)],
        out_specs=_rows(T, ZE), compiler_params=_cparams(),
    )(dab, du, z, z, dag, dbg, dcq, dckv, dkr)


def _odd_pre(z, name):
    S, D = z.shape[0], z.shape[1] // 3
    T = _row_tile(S)

    def body(val_ref, glu_ref, u_ref):
        u_ref[...] = val_ref[...] * _sigmoid(glu_ref[...])

    return pl.pallas_call(
        body, name=name, grid=(S // T,),
        out_shape=jax.ShapeDtypeStruct((S, D), F32),
        in_specs=[_rows(T, D, 0), _rows(T, D, 1)], out_specs=_rows(T, D),
        compiler_params=_cparams(),
    )(z, z)


def _layer_norm_stats(cv):
    mu = jnp.mean(cv, axis=-1, keepdims=True)
    cen = cv - mu
    rstd = lax.rsqrt(jnp.mean(cen * cen, axis=-1, keepdims=True) + EPS)
    return cen * rstd, rstd


def _odd_post(cv, z, ln_g, ln_b, name):
    S, D = cv.shape
    T = _row_tile(S)

    def body(cv_ref, sg_ref, g_ref, b_ref, y_ref):
        cvh, _ = _layer_norm_stats(cv_ref[...])
        y_ref[...] = (_silu(cvh * g_ref[...] + b_ref[...]) * _silu(sg_ref[...])).astype(BF16)

    return pl.pallas_call(
        body, name=name, grid=(S // T,),
        out_shape=jax.ShapeDtypeStruct((S, D), BF16),
        in_specs=[_rows(T, D), _rows(T, D, 2), _const((1, D)), _const((1, D))],
        out_specs=_rows(T, D), compiler_params=_cparams(),
    )(cv, z, ln_g, ln_b)


def _odd_bwd_norm(dyi, cv, z, ln_g, ln_b, name):
    S, D = cv.shape
    T = _row_tile(S)

    def fn(ins, outs):
        dy_ref, cv_ref, sg_ref, g_ref, b_ref = ins
        dcv_ref, dsg_ref = outs
        cvh, rstd = _layer_norm_stats(cv_ref[...])
        ln = cvh * g_ref[...] + b_ref[...]
        sgv, dy = sg_ref[...], dy_ref[...]
        dsg_ref[...] = (dy * _silu(ln) * _dsilu(sgv)).astype(BF16)
        dln = dy * _silu(sgv) * _dsilu(ln)
        dh = dln * g_ref[...]
        dcv_ref[...] = rstd * (dh - jnp.mean(dh, axis=-1, keepdims=True)
                               - cvh * jnp.mean(dh * cvh, axis=-1, keepdims=True))
        return [dln * cvh, dln]

    return _col_sums(2, fn, [dyi, cv, z, ln_g, ln_b],
                     [_rows(T, D), _rows(T, D), _rows(T, D, 2), _const((1, D)), _const((1, D))],
                     [jax.ShapeDtypeStruct((S, D), F32), jax.ShapeDtypeStruct((S, D), BF16)],
                     [_rows(T, D), _rows(T, D)], S, T, [D, D], name)


def _odd_dz(du, z, dsg, name):
    S, D = du.shape
    T = _row_tile(S)

    def body(du_ref, val_ref, glu_ref, dsg_ref, dz_ref):
        duv = du_ref[...]
        sig = _sigmoid(glu_ref[...])
        dz_ref[:, 0:D] = (duv * sig).astype(BF16)
        dz_ref[:, D:2 * D] = (duv * val_ref[...] * sig * (1.0 - sig)).astype(BF16)
        dz_ref[:, 2 * D:3 * D] = dsg_ref[...]

    return pl.pallas_call(
        body, name=name, grid=(S // T,),
        out_shape=jax.ShapeDtypeStruct((S, 3 * D), BF16),
        in_specs=[_rows(T, D), _rows(T, D, 0), _rows(T, D, 1), _rows(T, D)],
        out_specs=_rows(T, 3 * D), compiler_params=_cparams(),
    )(du, z, z, dsg)


ADAM_BLOCK_ELEMS = 128 * 1024


def _adam_tiles(R, C):
    if R * C <= ADAM_BLOCK_ELEMS:
        return R, C
    tr = R
    for cand in range(SUBLANES, R, SUBLANES):
        if R % cand == 0 and cand * C <= ADAM_BLOCK_ELEMS:
            tr = cand
    if tr < R:
        return tr, C
    tc = C
    for cand in range(LANES, C, LANES):
        if C % cand == 0 and R * cand <= ADAM_BLOCK_ELEMS:
            tc = cand
    return R, tc


def _adamw(g_parts, w, m, v, name):
    if not isinstance(g_parts, (list, tuple)):
        g_parts = [g_parts]
    ng = len(g_parts)
    _, R, C = g_parts[0].shape
    tr, tc = _adam_tiles(R, C)

    def body(*refs):
        g_refs = refs[:ng]
        w_ref, m_ref, v_ref, go_ref, d_ref, mo_ref, vo_ref = refs[ng:]
        g = None
        for g_ref in g_refs:
            for p in range(g_ref.shape[0]):
                part = g_ref[p].astype(F32)
                g = part if g is None else g + part
        mn = ADAM_B1 * m_ref[...] + (1.0 - ADAM_B1) * g
        vn = ADAM_B2 * v_ref[...] + (1.0 - ADAM_B2) * (g * g)
        m_hat = mn / (1.0 - ADAM_B1 ** ADAM_STEP)
        v_hat = vn / (1.0 - ADAM_B2 ** ADAM_STEP)
        go_ref[...] = g
        d_ref[...] = -ADAM_LR * (m_hat / (jnp.sqrt(v_hat) + ADAM_EPS) + ADAM_WD * w_ref[...])
        mo_ref[...] = mn
        vo_ref[...] = vn

    slab = jax.ShapeDtypeStruct((R, C), F32)
    blk = pl.BlockSpec((tr, tc), lambda i, j: (i, j))
    return pl.pallas_call(
        body, name=name, grid=(R // tr, C // tc),
        out_shape=(slab,) * 4,
        in_specs=[pl.BlockSpec((g.shape[0], tr, tc), lambda i, j: (0, i, j)) for g in g_parts] + [blk, blk, blk],
        out_specs=(blk,) * 4, compiler_params=_cparams(),
    )(*g_parts, w, m, v)


def _gather_cols(g, shape):
    nd = len(shape)
    t = jnp.moveaxis(g, 0, nd - 1)
    return t.reshape(tuple(shape[:-1]) + (N_DEV * shape[-1],))


def _scatter_cols(full, n):
    t = full.reshape(full.shape[:-1] + (N_DEV, n))
    return jnp.moveaxis(t, -2, 0)


def kernel(x, c, positions, ada_w, ada_b, pre_norm_g, post_norm_g, even_w_in, even_sc_conv_w, even_sc_conv_b, even_q_norm_g, even_kv_norm_g, even_w_uq, even_w_ukv, even_w_out, odd_w_in, odd_conv_w, odd_conv_b, odd_ln_g, odd_ln_b, odd_w_out, loss_target, m_ada_w, m_ada_b, m_pre_norm_g, m_post_norm_g, m_even_w_in, m_even_sc_conv_w, m_even_sc_conv_b, m_even_q_norm_g, m_even_kv_norm_g, m_even_w_uq, m_even_w_ukv, m_even_w_out, m_odd_w_in, m_odd_conv_w, m_odd_conv_b, m_odd_ln_g, m_odd_ln_b, m_odd_w_out, v_ada_w, v_ada_b, v_pre_norm_g, v_post_norm_g, v_even_w_in, v_even_sc_conv_w, v_even_sc_conv_b, v_even_q_norm_g, v_even_kv_norm_g, v_even_w_uq, v_even_w_ukv, v_even_w_out, v_odd_w_in, v_odd_conv_w, v_odd_conv_b, v_odd_ln_g, v_odd_ln_b, v_odd_w_out):
    S, D = x.shape[1], x.shape[2]
    L = ada_w.shape[0]
    NE, NO = even_w_in.shape[0], odd_w_in.shape[0]
    me = 4 * lax.axis_index("x") + 2 * lax.axis_index("y") + lax.axis_index("c")
    x0 = x[0]
    target = loss_target[0]

    small_parts = [c, even_sc_conv_w, odd_conv_w, odd_conv_b, odd_ln_g, odd_ln_b]
    small_shapes = [p.shape for p in small_parts]
    sg = _exchange([_pack(small_parts, F32, SUBLANES)], False, "gather_small")[0].reshape(N_DEV, -1)
    c_all, scw_g, ocw_g, ocb_g, olg_g, olb_g = _unpack(sg, small_shapes)
    c_all = c_all.reshape(N_DEV, D)
    sc_conv_w = _gather_cols(scw_g, even_sc_conv_w.shape)
    o_conv_w = _gather_cols(ocw_g, odd_conv_w.shape)
    o_conv_b = _gather_cols(ocb_g, odd_conv_b.shape)
    o_ln_g = _gather_cols(olg_g, odd_ln_g.shape)
    o_ln_b = _gather_cols(olb_g, odd_ln_b.shape)

    pad_q = HEAD_PAD - QK_NOPE - QK_ROPE
    w_local = [jnp.swapaxes(even_w_in, 1, 2).astype(BF16),
               jnp.pad(even_w_uq, ((0, 0), (0, 0), (0, pad_q))).astype(BF16),
               jnp.pad(even_w_ukv[..., :QK_NOPE], ((0, 0), (0, 0), (0, HEAD_PAD - QK_NOPE))).astype(BF16),
               even_w_ukv[..., QK_NOPE:].astype(BF16),
               even_w_out.astype(BF16), odd_w_in.astype(BF16), odd_w_out.astype(BF16)]
    n_ada = ada_w.shape[2]
    ada_b_cols = lax.dynamic_slice_in_dim(ada_b, me * n_ada, n_ada, axis=1).reshape(L, 1, n_ada)
    mod_slab = _ada_fwd(c_all, ada_w, ada_b_cols)
    mod_g = _exchange([_pack([mod_slab], F32, SUBLANES)], False, "gather_mod")[0].reshape(N_DEV, -1)
    mod_all = mod_g[:, :L * N_DEV * n_ada].reshape(N_DEV, L, N_DEV, n_ada)
    mod = lax.dynamic_index_in_dim(mod_all, me, axis=2, keepdims=False)
    mod = jnp.moveaxis(mod, 0, 1).reshape(L, 3 * D)
    shift, scale, gate = mod[:, :D], mod[:, D:2 * D], mod[:, 2 * D:]

    heads_to_cols = lambda g: jnp.moveaxis(g, 0, 1).reshape(g.shape[1], -1)
    w_handles = {}
    token = jnp.broadcast_to(jnp.minimum(jnp.abs(mod[0, 0]), 0.0), (SUBLANES, LANES))
    for layer in range(L):
        i = layer // 2
        groups = ({"in": [w_local[0][i]], "rest": [w[i] for w in w_local[1:5]]} if layer % 2 == 0
                  else {"all": [w[i] for w in w_local[5:]]})
        for key, mine in groups.items():
            mine = [w + token[0, 0].astype(BF16) for w in mine]
            w_handles[layer, key], token = _exchange_start(mine, False, f"gather_weights_start_l{layer}_{key}")
    w_token = token

    def arrived(layer, key, after):
        return _exchange_wait(w_handles[layer, key], False, after, f"gather_weights_wait_l{layer}_{key}")[1]

    e_w_in_k, e_w_q_k, e_w_kv_k, e_w_out, o_w_in, o_w_out = ([None] * NE, [None] * NE, [None] * NE, [None] * NE,
                                                             [None] * NO, [None] * NO)

    half = QK_ROPE // 2
    inv_freq = 1.0 / (ROPE_THETA ** (jnp.arange(0, QK_ROPE, 2, dtype=F32) / QK_ROPE))
    inv_lane = jnp.zeros((HEAD_PAD,), F32).at[QK_NOPE:QK_NOPE + QK_ROPE].set(jnp.concatenate([inv_freq, inv_freq]))
    tabs = _rope_tables(positions.astype(F32).reshape(S, 1), inv_lane.reshape(1, HEAD_PAD))
    del half

    row = lambda a: a.reshape(1, -1)
    scb = even_sc_conv_b
    KP3, KP31 = SUBLANES, 32

    saved = []
    xs = x0
    for layer in range(L):
        i = layer // 2
        tag = f"l{layer}"
        h = _pre_norm(xs, row(pre_norm_g[layer]), row(scale[layer]), row(shift[layer]), f"pre_norm_{tag}")
        first = w_token if layer == 0 else h
        if layer % 2 == 0:
            wt = arrived(layer, "in", first)[0].reshape(-1, D)
            e_w_in_k[i] = jnp.concatenate([wt[:2048], wt[2464:2976], wt[2048:2432], jnp.zeros((QK_NOPE, D), BF16),
                                           wt[2432:2464], jnp.zeros((pad_q, D), BF16)], axis=0)
            z = _matmul(h, e_w_in_k[i], "nt", F32, f"w_in_{tag}", tn=1024)
            eq_g, ek_g, ev_g, eout_g = arrived(layer, "rest", z)
            e_w_q_k[i] = heads_to_cols(eq_g)
            e_w_kv_k[i] = jnp.concatenate([heads_to_cols(ek_g), heads_to_cols(ev_g)], axis=-1)
            e_w_out[i] = eout_g.reshape(-1, D)
            u, qn, kvn = _even_pre(z, row(even_q_norm_g[i]), row(even_kv_norm_g[i]), f"even_pre_{tag}")
            cw = jnp.pad(sc_conv_w[i], ((0, KP3 - SC_KERNEL), (0, 0)))
            cv = _conv_fwd(u, cw, row(scb[i]), SC_KERNEL, f"conv_{tag}")
            q, k, v, kT3, vT3 = _qkv_fwd_t(qn, kvn, z, tabs, e_w_q_k[i], e_w_kv_k[i], f"qkv_{tag}")
            o, lse = _attn_fwd_t(q, k, vT3, f"attn_{tag}")
            ycat = _even_post(z, cv, o, f"even_post_{tag}")
            y = _matmul(ycat, e_w_out[i], "nn", F32, f"w_out_{tag}", tn=1024)
            saved.append(dict(x=xs, h=h, z=z, u=u, qn=qn, kvn=kvn, cw=cw, cv=cv, q=q, k=k, v=v, kT3=kT3, o=o, lse=lse,
                              ycat=ycat, y=y))
        else:
            owin_g, oout_g = arrived(layer, "all", first)
            o_w_in[i], o_w_out[i] = heads_to_cols(owin_g), oout_g.reshape(-1, D)
            z = _matmul(h, o_w_in[i], "nn", F32, f"w_in_{tag}", tn=1024)
            u = _odd_pre(z, f"odd_pre_{tag}")
            cw = jnp.pad(o_conv_w[i], ((0, KP31 - CONF_KERNEL), (0, 0)))
            cv = _conv_fwd(u, cw, row(o_conv_b[i]), CONF_KERNEL, f"conv_{tag}")
            yin = _odd_post(cv, z, row(o_ln_g[i]), row(o_ln_b[i]), f"odd_post_{tag}")
            y = _matmul(yin, o_w_out[i], "nn", F32, f"w_out_{tag}", tn=1024)
            saved.append(dict(x=xs, h=h, z=z, u=u, cw=cw, cv=cv, yin=yin, y=y))
        xs = _post_norm(xs, y, row(post_norm_g[layer]), row(gate[layer]), f"post_norm_{tag}")

    loss_row, dx = _loss_head(xs, target)
    loss = lax.psum(loss_row[0, 0], MESH_AXES)

    g_pre, g_post, dmod = [None] * L, [None] * L, [None] * L
    g_e_w_in, g_e_w_uq, g_e_w_ukv, g_e_w_out = [None] * NE, [None] * NE, [None] * NE, [None] * NE
    g_scw, g_scb, g_qg, g_kvg = [None] * NE, [None] * NE, [None] * NE, [None] * NE
    g_o_w_in, g_o_w_out, g_ocw, g_ocb, g_olg, g_olb = ([None] * NO for _ in range(6))
    sm_w = [even_sc_conv_w, odd_conv_w, odd_conv_b, odd_ln_g, odd_ln_b]
    sm_rows = _pack(sm_w, F32, SUBLANES).shape[0]

    def small_slab():
        full = [_scatter_cols(jnp.stack(g_scw), even_sc_conv_w.shape[-1]),
                _scatter_cols(jnp.stack(g_ocw), odd_conv_w.shape[-1]),
                _scatter_cols(jnp.concatenate(g_ocb, 0), odd_conv_b.shape[-1]),
                _scatter_cols(jnp.concatenate(g_olg, 0), odd_ln_g.shape[-1]),
                _scatter_cols(jnp.concatenate(g_olb, 0), odd_ln_b.shape[-1])]
        flat = jnp.concatenate([g.reshape(N_DEV, -1) for g in full], axis=1)
        return jnp.pad(flat, ((0, 0), (0, sm_rows * PACK_COLS - flat.shape[1]))).reshape(N_DEV, sm_rows, PACK_COLS)

    s_handles, own_parts = [None] * L, [None] * L
    bw_token = jnp.zeros((SUBLANES, LANES), F32)

    def start_scatter(layer, parts):
        if layer == 0:
            parts = parts + [small_slab()]
        own_parts[layer] = [lax.dynamic_slice_in_dim(g, me, 1, axis=0) for g in parts]
        s_handles[layer], token = _exchange_start([g.astype(BF16) for g in parts], True,
                                                  f"scatter_grads_start_l{layer}")
        return token

    for layer in reversed(range(L)):
        i = layer // 2
        tag = f"l{layer}"
        sv = saved[layer]
        dy, dgate, g_post[layer] = _post_norm_bwd(dx, sv["y"], row(post_norm_g[layer]) + bw_token[0, 0],
                                                  row(gate[layer]), f"post_norm_bwd_{tag}")
        if layer % 2 == 0:
            dyc = _matmul(dy, e_w_out[i], "nt", F32, f"d_ycat_{tag}", tn=1024)
            g_e_w_out[i] = _matmul(sv["ycat"], dy, "tn", F32, f"g_w_out_{tag}", tn=1024)
            dab, dag, dbg, dcv, do, delta = _even_bwd_gates(dyc, sv["z"], sv["cv"], sv["o"], f"even_gates_bwd_{tag}")
            du, dcw, g_scb[i] = _conv_bwd(dcv, sv["u"], sv["cw"], SC_KERNEL, f"conv_bwd_{tag}")
            g_scw[i] = dcw[:SC_KERNEL]
            dq, dk, dv = _attn_bwd_t(sv["q"], sv["k"], sv["v"], sv["kT3"], do, sv["lse"], delta, f"attn_bwd_{tag}")
            (dqp, dkvp, dcq, dckv, dkr, g_qg[i], g_kvg[i]) = _qkv_bwd(
                dq, dk, dv, sv["z"], tabs, e_w_q_k[i], e_w_kv_k[i],
                row(even_q_norm_g[i]), row(even_kv_norm_g[i]), f"qkv_bwd_{tag}")
            gq = _matmul(sv["qn"], dqp, "tn", F32, f"g_w_uq_{tag}", tn=1024)
            gkv = _matmul(sv["kvn"], dkvp, "tn", F32, f"g_w_ukv_{tag}")
            g_e_w_uq[i] = jnp.moveaxis(gq.reshape(QL, HEADS, HEAD_PAD)[..., :QK_NOPE + QK_ROPE], 1, 0)
            g_e_w_ukv[i] = jnp.moveaxis(jnp.concatenate(
                [gkv[:, :HEADS * HEAD_PAD].reshape(KVL, HEADS, HEAD_PAD)[..., :QK_NOPE],
                 gkv[:, HEADS * HEAD_PAD:].reshape(KVL, HEADS, V_HEAD)], axis=-1), 1, 0)
            g_e_w_out[i] = g_e_w_out[i].reshape(N_DEV, -1, D)
            dz = _even_dz(dab, du, sv["z"], dag, dbg, dcq, dckv, dkr, f"even_dz_{tag}")
            gt = _matmul(dz, sv["h"], "tn", F32, f"g_w_in_{tag}", tn=1024)
            g_e_w_in[i] = jnp.concatenate([gt[:2048], gt[2560:2944], gt[2944 + QK_NOPE:2944 + QK_NOPE + QK_ROPE],
                                           gt[2048:2560]], axis=0).reshape(N_DEV, -1, D)
            bw_token = start_scatter(layer, [g_e_w_in[i], g_e_w_uq[i], g_e_w_ukv[i], g_e_w_out[i]])
            dh = _matmul(dz, e_w_in_k[i], "nn", F32, f"d_h_{tag}", tn=1024)
        else:
            dyi = _matmul(dy, o_w_out[i], "nt", F32, f"d_yin_{tag}", tn=1024)
            g_o_w_out[i] = _matmul(sv["yin"], dy, "tn", F32, f"g_w_out_{tag}", tn=1024).reshape(N_DEV, -1, D)
            dcv, dsg, g_olg[i], g_olb[i] = _odd_bwd_norm(dyi, sv["cv"], sv["z"], row(o_ln_g[i]), row(o_ln_b[i]),
                                                         f"odd_norm_bwd_{tag}")
            du, dcw, g_ocb[i] = _conv_bwd(dcv, sv["u"], sv["cw"], CONF_KERNEL, f"conv_bwd_{tag}")
            g_ocw[i] = dcw[:CONF_KERNEL]
            dz = _odd_dz(du, sv["z"], dsg, f"odd_dz_{tag}")
            g_o_w_in[i] = jnp.moveaxis(_matmul(sv["h"], dz, "tn", F32, f"g_w_in_{tag}", tn=1024)
                                       .reshape(D, N_DEV, -1), 1, 0)
            bw_token = start_scatter(layer, [g_o_w_in[i], g_o_w_out[i]])
            dh = _matmul(dz, o_w_in[i], "nt", F32, f"d_h_{tag}", tn=1024)
        dx, dshift, dscale, g_pre[layer] = _pre_norm_bwd(dh, sv["x"], dx, row(pre_norm_g[layer]) + bw_token[0, 0],
                                                         row(scale[layer]), f"pre_norm_bwd_{tag}")
        dmod[layer] = jnp.concatenate([dshift, dscale, dgate], axis=-1)
    grad_x = dx.reshape(1, S, D)

    rep_g = [jnp.concatenate(dmod, 0), jnp.concatenate(g_pre, 0), jnp.concatenate(g_post, 0),
             jnp.stack(g_scb), jnp.stack(g_qg), jnp.stack(g_kvg)]
    rep_w = [ada_b, pre_norm_g, post_norm_g, even_sc_conv_b, even_q_norm_g, even_kv_norm_g]
    rep_m = [m_ada_b, m_pre_norm_g, m_post_norm_g, m_even_sc_conv_b, m_even_q_norm_g, m_even_kv_norm_g]
    rep_v = [v_ada_b, v_pre_norm_g, v_post_norm_g, v_even_sc_conv_b, v_even_q_norm_g, v_even_kv_norm_g]
    rep_shapes = [w.shape for w in rep_w]
    rep_all = _exchange([_pack(rep_g, F32, SUBLANES)], False, "gather_small_grads")[0]
    rep_out = _adamw(rep_all, _pack(rep_w, F32, SUBLANES), _pack(rep_m, F32, SUBLANES), _pack(rep_v, F32, SUBLANES),
                     "adamw_replicated")
    rep_res = [_unpack(o.reshape(-1), rep_shapes) for o in rep_out]

    dmod_all = rep_all.reshape(N_DEV, -1)[:, :L * 3 * D].reshape(N_DEV, L, 3 * D)
    dmod_cols = jnp.moveaxis(lax.dynamic_slice_in_dim(dmod_all, me * n_ada, n_ada, axis=2), 0, 1)
    g_ada_w = _ada_bwd(c_all.T, dmod_cols)
    ada_out = _adamw(g_ada_w.reshape(1, -1, PACK_COLS), ada_w.reshape(-1, PACK_COLS),
                     m_ada_w.reshape(-1, PACK_COLS), v_ada_w.reshape(-1, PACK_COLS), "adamw_ada_w")
    ada_res = [o.reshape(ada_w.shape) for o in ada_out]

    sm_m = [m_even_sc_conv_w, m_odd_conv_w, m_odd_conv_b, m_odd_ln_g, m_odd_ln_b]
    sm_v = [v_even_sc_conv_w, v_odd_conv_w, v_odd_conv_b, v_odd_ln_g, v_odd_ln_b]
    sm_shapes = [w.shape for w in sm_w]
    state = {"even_w_in": (even_w_in, m_even_w_in, v_even_w_in), "even_w_uq": (even_w_uq, m_even_w_uq, v_even_w_uq),
             "even_w_ukv": (even_w_ukv, m_even_w_ukv, v_even_w_ukv), "even_w_out": (even_w_out, m_even_w_out, v_even_w_out),
             "odd_w_in": (odd_w_in, m_odd_w_in, v_odd_w_in), "odd_w_out": (odd_w_out, m_odd_w_out, v_odd_w_out)}
    big_res = {name: [[None] * len(state[name][0]) for _ in range(4)] for name in state}
    after = bw_token
    sm_res = None
    for layer in reversed(range(L)):
        i = layer // 2
        names = ["even_w_in", "even_w_uq", "even_w_ukv", "even_w_out"] if layer % 2 == 0 else ["odd_w_in", "odd_w_out"]
        _, landed = _exchange_wait(s_handles[layer], True, after, f"scatter_grads_wait_l{layer}")
        own = own_parts[layer]
        for a, name in enumerate(names):
            transposed = name == "even_w_in"
            wmv = [t[i].T if transposed else t[i] for t in state[name]]
            res = _adamw([own[a], landed[a]], *wmv, f"adamw_{name}_{i}")
            for kind in range(4):
                big_res[name][kind][i] = res[kind].T if transposed else res[kind]
            after = res[0]
        if layer == 0:
            sm_out = _adamw([own[-1], landed[-1]], _pack(sm_w, F32, SUBLANES), _pack(sm_m, F32, SUBLANES),
                            _pack(sm_v, F32, SUBLANES), "adamw_small_sharded")
            sm_res = [_unpack(o.reshape(-1), sm_shapes) for o in sm_out]
    sh_res = [dict(zip(["even_sc_conv_w", "odd_conv_w", "odd_conv_b", "odd_ln_g", "odd_ln_b"], sm_res[kind]))
              for kind in range(4)]
    for name in state:
        for kind in range(4):
            sh_res[kind][name] = jnp.stack(big_res[name][kind])

    order = ["ada_w", "ada_b", "pre_norm_g", "post_norm_g", "even_w_in", "even_sc_conv_w", "even_sc_conv_b",
             "even_q_norm_g", "even_kv_norm_g", "even_w_uq", "even_w_ukv", "even_w_out", "odd_w_in", "odd_conv_w",
             "odd_conv_b", "odd_ln_g", "odd_ln_b", "odd_w_out"]
    rep_names = ["ada_b", "pre_norm_g", "post_norm_g", "even_sc_conv_b", "even_q_norm_g", "even_kv_norm_g"]
    outs = [loss, grad_x]
    for kind in range(4):
        for name in order:
            if name == "ada_w":
                outs.append(ada_res[kind])
            elif name in rep_names:
                outs.append(rep_res[kind][rep_names.index(name)])
            else:
                outs.append(sh_res[kind][name])
    return tuple(outs)
```

```python
import functools
import math

import jax
import jax.numpy as jnp
from jax import lax
from jax.experimental import pallas as pl
from jax.experimental.pallas import tpu as pltpu

F32 = jnp.float32
BF16 = jnp.bfloat16
MESH_AXES = ("x", "y", "c")
N_DEV = 8
EPS = 1e-6
CHUNK = 64
HEADS = 8
QK_NOPE = 64
QK_ROPE = 32
V_HEAD = 64
HEAD_PAD = 128
ROPE_THETA = 10000.0
SC_KERNEL = 3
CONF_KERNEL = 31
LANES = 128
SUBLANES = 8
PACK_COLS = 1024
VMEM_LIMIT = 48 * 1024 * 1024
NEG = -1e30

ADAM_LR = 0.001
ADAM_B1 = 0.9
ADAM_B2 = 0.999
ADAM_EPS = 1e-08
ADAM_WD = 0.01
ADAM_STEP = 10


def _cparams():
    return pltpu.CompilerParams(vmem_limit_bytes=VMEM_LIMIT)


def _sigmoid(x):
    return 1.0 / (1.0 + jnp.exp(-x))


def _f32(ref):
    return ref[...].astype(F32)


def _silu(x):
    return x * _sigmoid(x)


def _dsilu(x):
    s = _sigmoid(x)
    return s * (1.0 + x * (1.0 - s))


def _rows(T, width, cb=0):
    return pl.BlockSpec((T, width), lambda i: (i, cb))


def _const(shape):
    nd = len(shape)
    return pl.BlockSpec(shape, lambda i: (0,) * nd)


def _row_tile(S):
    return min(256, S)


def _exchange(srcs, scatter, name):
    n = len(srcs)
    shapes = [tuple(s.shape[1:]) if scatter else tuple(s.shape) for s in srcs]

    def body(*refs):
        src_refs, out_refs = refs[:n], refs[n:2 * n]
        send_sems, recv_sems, local_sems = refs[2 * n:]
        x, y, c = lax.axis_index("x"), lax.axis_index("y"), lax.axis_index("c")
        me = 4 * x + 2 * y + c
        owns, copies = [], []
        for a in range(n):
            def piece(d, a=a):
                return src_refs[a].at[d] if scatter else src_refs[a]

            own = pltpu.make_async_copy(piece(me), out_refs[a].at[me], local_sems.at[a])
            own.start()
            owns.append(own)
            for k in range(1, N_DEV):
                px, py, pc = x ^ ((k >> 2) & 1), y ^ ((k >> 1) & 1), c ^ (k & 1)
                peer = 4 * px + 2 * py + pc
                sem = a * (N_DEV - 1) + k - 1
                cp = pltpu.make_async_remote_copy(
                    src_ref=piece(peer), dst_ref=out_refs[a].at[me],
                    send_sem=send_sems.at[sem], recv_sem=recv_sems.at[sem],
                    device_id=(px, py, pc), device_id_type=pl.DeviceIdType.MESH)
                cp.start()
                arrival = pltpu.make_async_remote_copy(
                    src_ref=piece(peer), dst_ref=out_refs[a].at[peer],
                    send_sem=send_sems.at[sem], recv_sem=recv_sems.at[sem],
                    device_id=(x, y, c), device_id_type=pl.DeviceIdType.MESH)
                copies.append((cp, arrival))
        for _, arrival in copies:
            arrival.wait_recv()
        for cp, _ in copies:
            cp.wait_send()
        for own in owns:
            own.wait()

    return pl.pallas_call(
        body, name=name,
        out_shape=tuple(jax.ShapeDtypeStruct((N_DEV,) + shp, s.dtype) for shp, s in zip(shapes, srcs)),
        in_specs=[pl.BlockSpec(memory_space=pl.ANY)] * n,
        out_specs=tuple(pl.BlockSpec(memory_space=pl.ANY) for _ in range(n)),
        scratch_shapes=[pltpu.SemaphoreType.DMA((n * (N_DEV - 1),)),
                        pltpu.SemaphoreType.DMA((n * (N_DEV - 1),)),
                        pltpu.SemaphoreType.DMA((n,))],
    )(*srcs)


_HBM = pl.BlockSpec(memory_space=pltpu.HBM)
_SEM = pl.BlockSpec(memory_space=pltpu.SEMAPHORE)


def _peer(k):
    x, y, c = lax.axis_index("x"), lax.axis_index("y"), lax.axis_index("c")
    return x ^ ((k >> 2) & 1), y ^ ((k >> 1) & 1), c ^ (k & 1)


def _exchange_start(srcs, scatter, name):
    n = len(srcs)
    shapes = [tuple(s.shape[1:]) if scatter else tuple(s.shape) for s in srcs]
    slots = N_DEV - 1 if scatter else N_DEV
    lands = [lax.empty((slots,) + shp, s.dtype) for shp, s in zip(shapes, srcs)]
    if not scatter:
        here = 4 * lax.axis_index("x") + 2 * lax.axis_index("y") + lax.axis_index("c")
        lands = [lax.dynamic_update_index_in_dim(l, s, here, 0) for l, s in zip(lands, srcs)]

    def body(*refs):
        src_refs, land_refs = refs[:n], refs[n:2 * n]
        send_sems, recv_sems = refs[2 * n], refs[2 * n + 1]
        token = refs[4 * n + 2]
        me = 4 * lax.axis_index("x") + 2 * lax.axis_index("y") + lax.axis_index("c")
        for a in range(n):
            for k in range(1, N_DEV):
                px, py, pc = _peer(k)
                peer = 4 * px + 2 * py + pc
                pltpu.make_async_remote_copy(
                    src_ref=src_refs[a].at[peer] if scatter else src_refs[a],
                    dst_ref=land_refs[a].at[k - 1] if scatter else land_refs[a].at[me],
                    send_sem=send_sems.at[a * (N_DEV - 1) + k - 1], recv_sem=recv_sems.at[a * (N_DEV - 1) + k - 1],
                    device_id=(px, py, pc), device_id_type=pl.DeviceIdType.MESH).start()
        token[...] = jnp.zeros_like(token)

    hbm = lambda arrs: [pltpu.HBM(a.shape, a.dtype) for a in arrs]
    out = pl.pallas_call(
        body, name=name,
        out_shape=(pltpu.SemaphoreType.DMA((n * (N_DEV - 1),)), pltpu.SemaphoreType.DMA((n * (N_DEV - 1),)),
                   *hbm(srcs), *hbm(lands), jax.ShapeDtypeStruct((SUBLANES, LANES), F32)),
        in_specs=[_HBM] * (2 * n),
        out_specs=(_SEM, _SEM, *([_HBM] * (2 * n)), pl.BlockSpec(memory_space=pltpu.VMEM)),
        input_output_aliases={a: 2 + a for a in range(2 * n)},
        compiler_params=pltpu.CompilerParams(has_side_effects=pltpu.SideEffectType.DATAFLOW_SIDE_EFFECTING),
    )(*[pltpu.with_memory_space_constraint(s, pltpu.HBM) for s in srcs],
      *[pltpu.with_memory_space_constraint(l, pltpu.HBM) for l in lands])
    return (out[0], out[1], list(out[2:2 + n]), list(out[2 + n:2 + 2 * n])), out[2 + 2 * n]


def _exchange_wait(handle, scatter, after, name):
    send_sems, recv_sems, srcs, lands = handle
    n = len(srcs)

    def body(*refs):
        src_refs, land_refs = refs[:n], refs[n:2 * n]
        send_sems, recv_sems = refs[2 * n], refs[2 * n + 1]
        for a in range(n):
            for k in range(1, N_DEV):
                px, py, pc = _peer(k)
                peer = 4 * px + 2 * py + pc
                cp = pltpu.make_async_remote_copy(
                    src_ref=src_refs[a].at[peer] if scatter else src_refs[a],
                    dst_ref=land_refs[a].at[k - 1] if scatter else land_refs[a].at[peer],
                    send_sem=send_sems.at[a * (N_DEV - 1) + k - 1], recv_sem=recv_sems.at[a * (N_DEV - 1) + k - 1],
                    device_id=(px, py, pc), device_id_type=pl.DeviceIdType.MESH)
                cp.wait_send()
                cp.wait_recv()

    out = pl.pallas_call(
        body, name=name,
        out_shape=tuple(pltpu.HBM(a.shape, a.dtype) for a in srcs + lands),
        in_specs=[_HBM] * (2 * n) + [_SEM, _SEM, pl.BlockSpec(memory_space=pl.ANY)],
        out_specs=tuple([_HBM] * (2 * n)),
        input_output_aliases={a: a for a in range(2 * n)},
        compiler_params=pltpu.CompilerParams(has_side_effects=pltpu.SideEffectType.DATAFLOW_SIDE_EFFECTING),
    )(*srcs, *lands, send_sems, recv_sems, after)
    return list(out[:n]), list(out[n:])


def _pack(parts, dtype, row_mult):
    flat = jnp.concatenate([p.reshape(-1).astype(dtype) for p in parts])
    n = flat.shape[0]
    rows = -(-n // PACK_COLS)
    rows = -(-rows // row_mult) * row_mult
    flat = jnp.pad(flat, (0, rows * PACK_COLS - n))
    return flat.reshape(rows, PACK_COLS)


def _unpack(flat, shapes):
    out, off = [], 0
    for shp in shapes:
        n = math.prod(shp)
        out.append(flat[..., off:off + n].reshape(flat.shape[:-1] + tuple(shp)))
        off += n
    return out


_DIMS = {"nn": (((1,), (0,)), ((), ())), "nt": (((1,), (1,)), ((), ())), "tn": (((0,), (0,)), ((), ()))}


def _matmul(a, b, mode, out_dtype, name, tm=512, tn=512, tk=None):
    if mode == "nn":
        (M, K), (_, N) = a.shape, b.shape
    elif mode == "nt":
        (M, K), (N, _) = a.shape, b.shape
    else:
        (K, M), (_, N) = a.shape, b.shape
    tm, tn = min(tm, M), min(tn, N)
    tk = K if tk is None else min(tk, K)
    nk = K // tk
    assert M % tm == 0 and N % tn == 0 and K % tk == 0, (name, a.shape, b.shape)

    def body(a_ref, b_ref, o_ref, *scratch):
        p = lax.dot_general(a_ref[...].astype(BF16), b_ref[...].astype(BF16), _DIMS[mode],
                            preferred_element_type=F32)
        if nk == 1:
            o_ref[...] = p.astype(out_dtype)
        else:
            acc = scratch[0]
            k = pl.program_id(2)

            @pl.when(k == 0)
            def _():
                acc[...] = p

            @pl.when(k > 0)
            def _():
                acc[...] += p

            @pl.when(k == nk - 1)
            def _():
                o_ref[...] = acc[...].astype(out_dtype)

    a_spec = (pl.BlockSpec((tk, tm), lambda i, j, k: (k, i)) if mode == "tn"
              else pl.BlockSpec((tm, tk), lambda i, j, k: (i, k)))
    b_spec = (pl.BlockSpec((tn, tk), lambda i, j, k: (j, k)) if mode == "nt"
              else pl.BlockSpec((tk, tn), lambda i, j, k: (k, j)))
    return pl.pallas_call(
        body, name=name, grid=(M // tm, N // tn, nk),
        out_shape=jax.ShapeDtypeStruct((M, N), out_dtype),
        in_specs=[a_spec, b_spec],
        out_specs=pl.BlockSpec((tm, tn), lambda i, j, k: (i, j)),
        scratch_shapes=[pltpu.VMEM((tm, tn), F32)] if nk > 1 else [],
        compiler_params=_cparams(),
    )(a, b)


def _ada_fwd(c_all, ada_w, ada_b_cols):
    L, D, n = ada_w.shape

    def body(c_ref, w_ref, b_ref, o_ref):
        act = _silu(c_ref[...]).astype(BF16)
        o_ref[0] = jnp.dot(act, w_ref[0].astype(BF16), preferred_element_type=F32) + b_ref[0]

    return pl.pallas_call(
        body, name="ada_fwd", grid=(L,),
        out_shape=jax.ShapeDtypeStruct((L, N_DEV, n), F32),
        in_specs=[pl.BlockSpec((N_DEV, D), lambda l: (0, 0)),
                  pl.BlockSpec((1, D, n), lambda l: (l, 0, 0)),
                  pl.BlockSpec((1, 1, n), lambda l: (l, 0, 0))],
        out_specs=pl.BlockSpec((1, N_DEV, n), lambda l: (l, 0, 0)),
        compiler_params=_cparams(),
    )(c_all, ada_w, ada_b_cols)


def _ada_bwd(c_all_t, dmod_cols):
    D = c_all_t.shape[0]
    L, _, n = dmod_cols.shape

    def body(c_ref, d_ref, o_ref):
        act = _silu(c_ref[...])
        dm = d_ref[0]
        acc = act[:, 0:1] * dm[0:1, :]
        for b in range(1, N_DEV):
            acc = acc + act[:, b:b + 1] * dm[b:b + 1, :]
        o_ref[0] = acc

    return pl.pallas_call(
        body, name="ada_bwd", grid=(L,),
        out_shape=jax.ShapeDtypeStruct((L, D, n), F32),
        in_specs=[pl.BlockSpec((D, N_DEV), lambda l: (0, 0)),
                  pl.BlockSpec((1, N_DEV, n), lambda l: (l, 0, 0))],
        out_specs=pl.BlockSpec((1, D, n), lambda l: (l, 0, 0)),
        compiler_params=_cparams(),
    )(c_all_t, dmod_cols)


def _rope_tables(pos_col, inv_lane):
    S = pos_col.shape[0]
    T = _row_tile(S)
    half = QK_ROPE // 2

    def body(p_ref, f_ref, c_ref, up_ref, dn_ref):
        ang = p_ref[...] * f_ref[...]
        lane = lax.broadcasted_iota(jnp.int32, ang.shape, 1)
        first = (lane >= QK_NOPE) & (lane < QK_NOPE + half)
        second = (lane >= QK_NOPE + half) & (lane < QK_NOPE + QK_ROPE)
        cs, sn = jnp.cos(ang), jnp.sin(ang)
        c_ref[...] = jnp.where(first | second, cs, 1.0)
        up_ref[...] = jnp.where(first, -sn, 0.0)
        dn_ref[...] = jnp.where(second, sn, 0.0)

    tab = jax.ShapeDtypeStruct((S, HEAD_PAD), F32)
    return pl.pallas_call(
        body, name="rope_tables", grid=(S // T,),
        out_shape=(tab, tab, tab),
        in_specs=[_rows(T, 1), _const((1, HEAD_PAD))],
        out_specs=(_rows(T, HEAD_PAD),) * 3,
        compiler_params=_cparams(),
    )(pos_col, inv_lane)


def _rope(blk, ct, ut, dt):
    half = QK_ROPE // 2
    up = pltpu.roll(blk, HEAD_PAD - half, 1)
    dn = pltpu.roll(blk, half, 1)
    return blk * ct + up * ut + dn * dt


def _rope_t(d, ct, ut, dt):
    half = QK_ROPE // 2
    return d * ct + pltpu.roll(d * ut, half, 1) + pltpu.roll(d * dt, HEAD_PAD - half, 1)


def _pre_norm(x, g, scale, shift, name):
    S, D = x.shape
    T = _row_tile(S)

    def body(x_ref, g_ref, sc_ref, sh_ref, h_ref):
        xv = x_ref[...]
        rstd = lax.rsqrt(jnp.mean(xv * xv, axis=-1, keepdims=True) + EPS)
        h_ref[...] = ((xv * rstd) * g_ref[...] * (1.0 + sc_ref[...]) + sh_ref[...]).astype(BF16)

    return pl.pallas_call(
        body, name=name, grid=(S // T,),
        out_shape=jax.ShapeDtypeStruct((S, D), BF16),
        in_specs=[_rows(T, D), _const((1, D)), _const((1, D)), _const((1, D))],
        out_specs=_rows(T, D), compiler_params=_cparams(),
    )(x, g, scale, shift)


def _post_norm(x, y, g, gate, name):
    S, D = x.shape
    T = _row_tile(S)

    def body(x_ref, y_ref, g_ref, gt_ref, o_ref):
        yv = y_ref[...]
        rstd = lax.rsqrt(jnp.mean(yv * yv, axis=-1, keepdims=True) + EPS)
        o_ref[...] = x_ref[...] + gt_ref[...] * ((yv * rstd) * g_ref[...])

    return pl.pallas_call(
        body, name=name, grid=(S // T,),
        out_shape=jax.ShapeDtypeStruct((S, D), F32),
        in_specs=[_rows(T, D), _rows(T, D), _const((1, D)), _const((1, D))],
        out_specs=_rows(T, D), compiler_params=_cparams(),
    )(x, y, g, gate)


def _fold8(v):
    T, C = v.shape
    return v.reshape(T // SUBLANES, SUBLANES, C).sum(axis=0)


def _col_sums(n_sums, body_fn, ins, in_specs, outs, out_specs, S, T, widths, name):
    n_in, n_out = len(ins), len(outs)
    nt = S // T

    def body(*refs):
        in_refs = refs[:n_in]
        out_refs = refs[n_in:n_in + n_out]
        sum_refs = refs[n_in + n_out:n_in + n_out + n_sums]
        accs = refs[n_in + n_out + n_sums:]
        i = pl.program_id(0)
        terms = body_fn(in_refs, out_refs)

        @pl.when(i == 0)
        def _():
            for acc, t in zip(accs, terms):
                acc[...] = _fold8(t)

        @pl.when(i > 0)
        def _():
            for acc, t in zip(accs, terms):
                acc[...] += _fold8(t)

        @pl.when(i == nt - 1)
        def _():
            for acc, s_ref in zip(accs, sum_refs):
                s_ref[...] = jnp.sum(acc[...], axis=0, keepdims=True)

    return pl.pallas_call(
        body, name=name, grid=(nt,),
        out_shape=tuple(outs) + tuple(jax.ShapeDtypeStruct((1, w), F32) for w in widths),
        in_specs=in_specs,
        out_specs=tuple(out_specs) + tuple(_const((1, w)) for w in widths),
        scratch_shapes=[pltpu.VMEM((SUBLANES, w), F32) for w in widths],
        compiler_params=_cparams(),
    )(*ins)


def _post_norm_bwd(dxo, y, g, gate, name):
    S, D = y.shape
    T = _row_tile(S)

    def fn(ins, outs):
        dxo_ref, y_ref, g_ref, gt_ref = ins
        yv, dv = y_ref[...], dxo_ref[...]
        rstd = lax.rsqrt(jnp.mean(yv * yv, axis=-1, keepdims=True) + EPS)
        yh = yv * rstd
        dn = dv * gt_ref[...]
        dyh = dn * g_ref[...]
        outs[0][...] = (rstd * (dyh - yh * jnp.mean(dyh * yh, axis=-1, keepdims=True))).astype(BF16)
        return [dv * (yh * g_ref[...]), dn * yh]

    return _col_sums(2, fn, [dxo, y, g, gate],
                     [_rows(T, D), _rows(T, D), _const((1, D)), _const((1, D))],
                     [jax.ShapeDtypeStruct((S, D), BF16)], [_rows(T, D)], S, T, [D, D], name)


def _pre_norm_bwd(dh, x, dxo, g, scale, name):
    S, D = x.shape
    T = _row_tile(S)

    def fn(ins, outs):
        dh_ref, x_ref, dxo_ref, g_ref, sc_ref = ins
        xv, dv = x_ref[...], dh_ref[...]
        rstd = lax.rsqrt(jnp.mean(xv * xv, axis=-1, keepdims=True) + EPS)
        xh = xv * rstd
        dr = dv * (1.0 + sc_ref[...])
        dxh = dr * g_ref[...]
        outs[0][...] = dxo_ref[...] + rstd * (dxh - xh * jnp.mean(dxh * xh, axis=-1, keepdims=True))
        return [dv, dv * (xh * g_ref[...]), dr * xh]

    return _col_sums(3, fn, [dh, x, dxo, g, scale],
                     [_rows(T, D), _rows(T, D), _rows(T, D), _const((1, D)), _const((1, D))],
                     [jax.ShapeDtypeStruct((S, D), F32)], [_rows(T, D)], S, T, [D, D, D], name)


def _post_pre_norm(x, y, g_post, gate, g_pre, scale, shift, name):
    S, D = x.shape
    T = _row_tile(S)

    def body(x_ref, y_ref, gp_ref, gt_ref, g_ref, sc_ref, sh_ref, xn_ref, h_ref):
        yv = y_ref[...]
        rstd_y = lax.rsqrt(jnp.mean(yv * yv, axis=-1, keepdims=True) + EPS)
        xn = x_ref[...] + gt_ref[...] * ((yv * rstd_y) * gp_ref[...])
        xn_ref[...] = xn
        rstd = lax.rsqrt(jnp.mean(xn * xn, axis=-1, keepdims=True) + EPS)
        h_ref[...] = ((xn * rstd) * g_ref[...] * (1.0 + sc_ref[...]) + sh_ref[...]).astype(BF16)

    return pl.pallas_call(
        body, name=name, grid=(S // T,),
        out_shape=(jax.ShapeDtypeStruct((S, D), F32), jax.ShapeDtypeStruct((S, D), BF16)),
        in_specs=[_rows(T, D), _rows(T, D)] + [_const((1, D))] * 5,
        out_specs=(_rows(T, D), _rows(T, D)), compiler_params=_cparams(),
    )(x, y, g_post, gate, g_pre, scale, shift)


def _pre_post_norm_bwd(dh, x, dxo, g_pre, scale, y_prev, g_post_prev, gate_prev, name):
    S, D = x.shape
    T = _row_tile(S)

    def fn(ins, outs):
        dh_ref, x_ref, dxo_ref, g_ref, sc_ref, y_ref, gp_ref, gt_ref = ins
        xv, dv = x_ref[...], dh_ref[...]
        rstd = lax.rsqrt(jnp.mean(xv * xv, axis=-1, keepdims=True) + EPS)
        xh = xv * rstd
        dr = dv * (1.0 + sc_ref[...])
        dxh = dr * g_ref[...]
        dx = dxo_ref[...] + rstd * (dxh - xh * jnp.mean(dxh * xh, axis=-1, keepdims=True))
        outs[0][...] = dx
        yv = y_ref[...]
        rstd_y = lax.rsqrt(jnp.mean(yv * yv, axis=-1, keepdims=True) + EPS)
        yh = yv * rstd_y
        dn = dx * gt_ref[...]
        dyh = dn * gp_ref[...]
        outs[1][...] = (rstd_y * (dyh - yh * jnp.mean(dyh * yh, axis=-1, keepdims=True))).astype(BF16)
        return [dv, dv * (xh * g_ref[...]), dr * xh, dx * (yh * gp_ref[...]), dn * yh]

    return _col_sums(5, fn, [dh, x, dxo, g_pre, scale, y_prev, g_post_prev, gate_prev],
                     [_rows(T, D), _rows(T, D), _rows(T, D), _const((1, D)), _const((1, D)),
                      _rows(T, D), _const((1, D)), _const((1, D))],
                     [jax.ShapeDtypeStruct((S, D), F32), jax.ShapeDtypeStruct((S, D), BF16)],
                     [_rows(T, D), _rows(T, D)], S, T, [D] * 5, name)


def _loss_post_norm_bwd(x, y, g_post, gate, target, name):
    S, D = x.shape
    T = _row_tile(S)

    def fn(ins, outs):
        x_ref, y_ref, gp_ref, gt_ref, t_ref = ins
        yv = y_ref[...]
        rstd_y = lax.rsqrt(jnp.mean(yv * yv, axis=-1, keepdims=True) + EPS)
        yh = yv * rstd_y
        e = x_ref[...] + gt_ref[...] * (yh * gp_ref[...]) - t_ref[...]
        dx = e * (1.0 / D)
        outs[0][...] = dx
        dn = dx * gt_ref[...]
        dyh = dn * gp_ref[...]
        outs[1][...] = (rstd_y * (dyh - yh * jnp.mean(dyh * yh, axis=-1, keepdims=True))).astype(BF16)
        return [e * e, dx * (yh * gp_ref[...]), dn * yh]

    return _col_sums(3, fn, [x, y, g_post, gate, target],
                     [_rows(T, D), _rows(T, D), _const((1, D)), _const((1, D)), _rows(T, D)],
                     [jax.ShapeDtypeStruct((S, D), F32), jax.ShapeDtypeStruct((S, D), BF16)],
                     [_rows(T, D), _rows(T, D)], S, T, [D] * 3, name)


def _scaled_total(v, coef, name):
    def body(v_ref, o_ref):
        o_ref[...] = jnp.broadcast_to(jnp.sum(v_ref[...], axis=1, keepdims=True) * coef, (1, LANES))

    return pl.pallas_call(body, name=name, out_shape=jax.ShapeDtypeStruct((1, LANES), F32))(v)


def _loss_head(x, target):
    S, D = x.shape
    T = _row_tile(S)
    nt = S // T

    def body(x_ref, t_ref, l_ref, dx_ref, acc):
        i = pl.program_id(0)
        e = x_ref[...] - t_ref[...]
        dx_ref[...] = e * (1.0 / D)
        part = _fold8(e * e)

        @pl.when(i == 0)
        def _():
            acc[...] = part

        @pl.when(i > 0)
        def _():
            acc[...] += part

        @pl.when(i == nt - 1)
        def _():
            tot = jnp.sum(jnp.sum(acc[...], axis=0, keepdims=True), axis=1, keepdims=True)
            l_ref[...] = jnp.broadcast_to(tot * (0.5 / D), (1, LANES))

    return pl.pallas_call(
        body, name="loss_head", grid=(nt,),
        out_shape=(jax.ShapeDtypeStruct((1, LANES), F32), jax.ShapeDtypeStruct((S, D), F32)),
        in_specs=[_rows(T, D), _rows(T, D)],
        out_specs=(_const((1, LANES)), _rows(T, D)),
        scratch_shapes=[pltpu.VMEM((SUBLANES, D), F32)],
        compiler_params=_cparams(),
    )(x, target)


CONV_ROWS = 64


def _conv_halo(K):
    return SUBLANES if K - 1 <= SUBLANES else 32


def _conv_fwd(u, w, b, K, name):
    S, C = u.shape
    KP = w.shape[0]
    T, HB, RS = min(512, S), _conv_halo(K), CONV_ROWS
    ratio = T // HB

    def body(u_ref, h_ref, w_ref, b_ref, o_ref, ext):
        i = pl.program_id(1)
        ext[0:HB, :] = jnp.where(i > 0, h_ref[...], 0.0)
        ext[HB:HB + T, :] = u_ref[...]
        for r0 in range(0, T, RS):
            acc = jnp.broadcast_to(b_ref[...], (RS, LANES))
            for k in range(K):
                off = HB - (K - 1) + k + r0
                acc = acc + w_ref[k:k + 1, :] * ext[off:off + RS, :]
            o_ref[r0:r0 + RS, :] = acc

    return pl.pallas_call(
        body, name=name, grid=(C // LANES, S // T),
        out_shape=jax.ShapeDtypeStruct((S, C), F32),
        in_specs=[pl.BlockSpec((T, LANES), lambda c, i: (i, c)),
                  pl.BlockSpec((HB, LANES), lambda c, i: (jnp.maximum(i * ratio - 1, 0), c)),
                  pl.BlockSpec((KP, LANES), lambda c, i: (0, c)),
                  pl.BlockSpec((1, LANES), lambda c, i: (0, c))],
        out_specs=pl.BlockSpec((T, LANES), lambda c, i: (i, c)),
        scratch_shapes=[pltpu.VMEM((HB + T, LANES), F32)],
        compiler_params=_cparams(),
    )(u, u, w, b)


def _conv_bwd(d, u, w, K, name):
    S, C = u.shape
    KP = w.shape[0]
    T, HB, RS = min(512, S), _conv_halo(K), CONV_ROWS
    ratio = T // HB
    nt = S // T
    last_halo = S // HB - 1

    def body(d_ref, dn_ref, u_ref, up_ref, w_ref, du_ref, dw_ref, db_ref, extd, extu, dws, dbs):
        i = pl.program_id(1)
        extd[0:T, :] = d_ref[...]
        extd[T:T + HB, :] = jnp.where(i < nt - 1, dn_ref[...], 0.0)
        extu[0:HB, :] = jnp.where(i > 0, up_ref[...], 0.0)
        extu[HB:HB + T, :] = u_ref[...]

        @pl.when(i == 0)
        def _():
            dws[...] = jnp.zeros_like(dws)
            dbs[...] = jnp.zeros_like(dbs)

        for r0 in range(0, T, RS):
            acc = jnp.zeros((RS, LANES), F32)
            for k in range(K):
                off = (K - 1 - k) + r0
                acc = acc + w_ref[k:k + 1, :] * extd[off:off + RS, :]
            du_ref[r0:r0 + RS, :] = acc
            dch = d_ref[r0:r0 + RS, :]
            dbs[...] += _fold8(dch)
            for k in range(K):
                off = HB - (K - 1) + k + r0
                dws[k * SUBLANES:(k + 1) * SUBLANES, :] += _fold8(dch * extu[off:off + RS, :])

        @pl.when(i == nt - 1)
        def _():
            dw_ref[...] = jnp.zeros_like(dw_ref)
            for k in range(K):
                dw_ref[k:k + 1, :] = jnp.sum(dws[k * SUBLANES:(k + 1) * SUBLANES, :], axis=0, keepdims=True)
            db_ref[...] = jnp.sum(dbs[...], axis=0, keepdims=True)

    return pl.pallas_call(
        body, name=name, grid=(C // LANES, nt),
        out_shape=(jax.ShapeDtypeStruct((S, C), F32), jax.ShapeDtypeStruct((KP, C), F32),
                   jax.ShapeDtypeStruct((1, C), F32)),
        in_specs=[pl.BlockSpec((T, LANES), lambda c, i: (i, c)),
                  pl.BlockSpec((HB, LANES), lambda c, i: (jnp.minimum((i + 1) * ratio, last_halo), c)),
                  pl.BlockSpec((T, LANES), lambda c, i: (i, c)),
                  pl.BlockSpec((HB, LANES), lambda c, i: (jnp.maximum(i * ratio - 1, 0), c)),
                  pl.BlockSpec((KP, LANES), lambda c, i: (0, c))],
        out_specs=(pl.BlockSpec((T, LANES), lambda c, i: (i, c)),
                   pl.BlockSpec((KP, LANES), lambda c, i: (0, c)),
                   pl.BlockSpec((1, LANES), lambda c, i: (0, c))),
        scratch_shapes=[pltpu.VMEM((T + HB, LANES), F32), pltpu.VMEM((HB + T, LANES), F32),
                        pltpu.VMEM((KP * SUBLANES, LANES), F32), pltpu.VMEM((SUBLANES, LANES), F32)],
        compiler_params=_cparams(),
    )(d, d, u, u, w)


SCW = 512
ZE = 3072
QL = 256
KVL = 128


def _rms_rows(x, g):
    rstd = lax.rsqrt(jnp.mean(x * x, axis=-1, keepdims=True) + EPS)
    return (x * rstd) * g


def _even_pre(z, qg, kvg, name):
    S = z.shape[0]
    T = _row_tile(S)

    def body(ac_ref, ax_ref, cq_ref, ckv_ref, qg_ref, kvg_ref, u_ref, qn_ref, kvn_ref):
        u_ref[...] = _f32(ac_ref) * _f32(ax_ref)
        qn_ref[...] = _rms_rows(_f32(cq_ref), qg_ref[...]).astype(BF16)
        kvn_ref[...] = _rms_rows(_f32(ckv_ref), kvg_ref[...]).astype(BF16)

    return pl.pallas_call(
        body, name=name, grid=(S // T,),
        out_shape=(jax.ShapeDtypeStruct((S, SCW), F32), jax.ShapeDtypeStruct((S, QL), BF16),
                   jax.ShapeDtypeStruct((S, KVL), BF16)),
        in_specs=[_rows(T, SCW, 1), _rows(T, SCW, 2), _rows(T, QL, 10), _rows(T, KVL, 22),
                  _const((1, QL)), _const((1, KVL))],
        out_specs=(_rows(T, SCW), _rows(T, QL), _rows(T, KVL)),
        compiler_params=_cparams(),
    )(z, z, z, z, qg, kvg)


def _qkv_fwd(qn, kvn, z, tabs, w_q, w_kv, name):
    S = qn.shape[0]
    T = _row_tile(S)
    HW = HEADS * HEAD_PAD
    scale = 1.0 / math.sqrt(QK_NOPE + QK_ROPE)

    def body(qn_ref, kvn_ref, kr_ref, ct_ref, ut_ref, dt_ref, wq_ref, wkv_ref, q_ref, k_ref, v_ref):
        ct, ut, dt = ct_ref[...], ut_ref[...], dt_ref[...]
        qa = jnp.dot(qn_ref[...], wq_ref[...], preferred_element_type=F32)
        kva = jnp.dot(kvn_ref[...], wkv_ref[...], preferred_element_type=F32)
        kr = _f32(kr_ref)
        for h in range(HEADS):
            sl = slice(h * HEAD_PAD, (h + 1) * HEAD_PAD)
            q_ref[:, sl] = (_rope(qa[:, sl], ct, ut, dt) * scale).astype(BF16)
            k_ref[:, sl] = _rope(kva[:, sl] + kr, ct, ut, dt).astype(BF16)
        v_ref[...] = kva[:, HW:].astype(BF16)

    return pl.pallas_call(
        body, name=name, grid=(S // T,),
        out_shape=(jax.ShapeDtypeStruct((S, HW), BF16), jax.ShapeDtypeStruct((S, HW), BF16),
                   jax.ShapeDtypeStruct((S, HEADS * V_HEAD), BF16)),
        in_specs=[_rows(T, QL), _rows(T, KVL), _rows(T, HEAD_PAD, 23),
                  _rows(T, HEAD_PAD), _rows(T, HEAD_PAD), _rows(T, HEAD_PAD),
                  _const(w_q.shape), _const(w_kv.shape)],
        out_specs=(_rows(T, HW), _rows(T, HW), _rows(T, HEADS * V_HEAD)),
        compiler_params=_cparams(),
    )(qn, kvn, z, *tabs, w_q, w_kv)


def _attn_tile(S):
    return min(256, S)


def _chunk_mask(TQ):
    r = lax.broadcasted_iota(jnp.int32, (TQ, TQ), 0) // CHUNK
    c = lax.broadcasted_iota(jnp.int32, (TQ, TQ), 1) // CHUNK
    return c <= r


_NT = (((1,), (1,)), ((), ()))
_TN = (((0,), (0,)), ((), ()))


def _attn_fwd(q, k, v, name):
    S = q.shape[0]
    TQ = _attn_tile(S)
    nq = S // TQ
    PW = 2 * HEAD_PAD

    def body(q_ref, k_ref, v_ref, o_ref, lse_ref, m_s, l_s, acc_s):
        i = pl.program_id(1)
        left = lax.broadcasted_iota(jnp.int32, (TQ, LANES), 1) < V_HEAD
        m_s[...] = jnp.full_like(m_s, NEG)
        l_s[...] = jnp.zeros_like(l_s)
        acc_s[...] = jnp.zeros_like(acc_s)
        qv = q_ref[...]

        def step(j, masked):
            r0 = pl.multiple_of(j * TQ, TQ)
            kb = k_ref[pl.ds(r0, TQ), :]
            vb = v_ref[pl.ds(r0, TQ), :]
            alphas, pvs = [], []
            for h in range(2):
                sl = slice(h * HEAD_PAD, (h + 1) * HEAD_PAD)
                s = lax.dot_general(qv[:, sl], kb[:, sl], _NT, preferred_element_type=F32)
                if masked:
                    s = jnp.where(_chunk_mask(TQ), s, NEG)
                m_prev = m_s[h]
                m_new = jnp.maximum(m_prev, jnp.max(s, axis=1, keepdims=True))
                alpha = jnp.exp(m_prev - m_new)
                p = jnp.exp(s - m_new[:, 0:1])
                l_s[h] = alpha * l_s[h] + jnp.sum(p, axis=1, keepdims=True)
                m_s[h] = m_new
                alphas.append(alpha)
                pvs.append(jnp.dot(p.astype(BF16), vb, preferred_element_type=F32))
            acc_s[...] = acc_s[...] * jnp.where(left, alphas[0], alphas[1]) + jnp.where(left, pvs[0], pvs[1])

        def loop_body(j, carry):
            step(j, False)
            return carry

        lax.fori_loop(0, i, loop_body, 0)
        step(i, True)
        o_ref[...] = acc_s[...] / jnp.where(left, l_s[0], l_s[1])
        lse_ref[...] = jnp.where(left, m_s[0] + jnp.log(l_s[0]), m_s[1] + jnp.log(l_s[1]))

    return pl.pallas_call(
        body, name=name, grid=(HEADS // 2, nq),
        out_shape=(jax.ShapeDtypeStruct((S, HEADS * V_HEAD), F32), jax.ShapeDtypeStruct((S, HEADS * V_HEAD), F32)),
        in_specs=[pl.BlockSpec((TQ, PW), lambda p, i: (i, p)),
                  pl.BlockSpec((S, PW), lambda p, i: (0, p)),
                  pl.BlockSpec((S, LANES), lambda p, i: (0, p))],
        out_specs=(pl.BlockSpec((TQ, LANES), lambda p, i: (i, p)),
                   pl.BlockSpec((TQ, LANES), lambda p, i: (i, p))),
        scratch_shapes=[pltpu.VMEM((2, TQ, LANES), F32), pltpu.VMEM((2, TQ, LANES), F32),
                        pltpu.VMEM((TQ, LANES), F32)],
        compiler_params=_cparams(),
    )(q, k, v)


def _attn_dq(q, k, v, do, lse, delta, name):
    S = q.shape[0]
    TQ = _attn_tile(S)
    nq = S // TQ
    PW = 2 * HEAD_PAD

    def body(q_ref, k_ref, v_ref, do_ref, lse_ref, dl_ref, dq_ref, acc_s):
        i = pl.program_id(1)
        left = lax.broadcasted_iota(jnp.int32, (TQ, LANES), 1) < V_HEAD
        acc_s[...] = jnp.zeros_like(acc_s)
        qv = q_ref[...]
        dov = do_ref[...]
        dos = [jnp.where(left, dov, jnp.zeros_like(dov)), jnp.where(left, jnp.zeros_like(dov), dov)]
        lses = [lse_ref[:, 0:1], lse_ref[:, V_HEAD:V_HEAD + 1]]
        dls = [dl_ref[:, 0:1], dl_ref[:, V_HEAD:V_HEAD + 1]]

        def step(j, masked):
            r0 = pl.multiple_of(j * TQ, TQ)
            kb = k_ref[pl.ds(r0, TQ), :]
            vb = v_ref[pl.ds(r0, TQ), :]
            for h in range(2):
                sl = slice(h * HEAD_PAD, (h + 1) * HEAD_PAD)
                s = lax.dot_general(qv[:, sl], kb[:, sl], _NT, preferred_element_type=F32)
                p = jnp.exp(s - lses[h])
                if masked:
                    p = jnp.where(_chunk_mask(TQ), p, 0.0)
                dp = lax.dot_general(dos[h], vb, _NT, preferred_element_type=F32)
                ds = (p * (dp - dls[h])).astype(BF16)
                acc_s[:, sl] += jnp.dot(ds, kb[:, sl], preferred_element_type=F32)

        def loop_body(j, carry):
            step(j, False)
            return carry

        lax.fori_loop(0, i, loop_body, 0)
        step(i, True)
        dq_ref[...] = acc_s[...]

    return pl.pallas_call(
        body, name=name, grid=(HEADS // 2, nq),
        out_shape=jax.ShapeDtypeStruct((S, HEADS * HEAD_PAD), F32),
        in_specs=[pl.BlockSpec((TQ, PW), lambda p, i: (i, p)),
                  pl.BlockSpec((S, PW), lambda p, i: (0, p)),
                  pl.BlockSpec((S, LANES), lambda p, i: (0, p)),
                  pl.BlockSpec((TQ, LANES), lambda p, i: (i, p)),
                  pl.BlockSpec((TQ, LANES), lambda p, i: (i, p)),
                  pl.BlockSpec((TQ, LANES), lambda p, i: (i, p))],
        out_specs=pl.BlockSpec((TQ, PW), lambda p, i: (i, p)),
        scratch_shapes=[pltpu.VMEM((TQ, PW), F32)],
        compiler_params=_cparams(),
    )(q, k, v, do, lse, delta)


def _attn_dkv(q, k, v, do, lse, delta, name):
    S = q.shape[0]
    TQ = _attn_tile(S)
    nq = S // TQ
    PW = 2 * HEAD_PAD

    def body(q_ref, k_ref, v_ref, do_ref, lse_ref, dl_ref, dk_ref, dv_ref, dk_s, dv_s):
        j = pl.program_id(1)
        left = lax.broadcasted_iota(jnp.int32, (TQ, LANES), 1) < V_HEAD
        dk_s[...] = jnp.zeros_like(dk_s)
        dv_s[...] = jnp.zeros_like(dv_s)
        kb = k_ref[...]
        vb = v_ref[...]

        def step(i, masked):
            r0 = pl.multiple_of(i * TQ, TQ)
            qb = q_ref[pl.ds(r0, TQ), :]
            dov = do_ref[pl.ds(r0, TQ), :]
            lse = lse_ref[pl.ds(r0, TQ), :]
            dl = dl_ref[pl.ds(r0, TQ), :]
            dos = [jnp.where(left, dov, jnp.zeros_like(dov)), jnp.where(left, jnp.zeros_like(dov), dov)]
            for h in range(2):
                sl = slice(h * HEAD_PAD, (h + 1) * HEAD_PAD)
                c0 = h * V_HEAD
                s = lax.dot_general(qb[:, sl], kb[:, sl], _NT, preferred_element_type=F32)
                p = jnp.exp(s - lse[:, c0:c0 + 1])
                if masked:
                    p = jnp.where(_chunk_mask(TQ), p, 0.0)
                dv_s[...] += lax.dot_general(p.astype(BF16), dos[h], _TN, preferred_element_type=F32)
                dp = lax.dot_general(dos[h], vb, _NT, preferred_element_type=F32)
                ds = (p * (dp - dl[:, c0:c0 + 1])).astype(BF16)
                dk_s[:, sl] += lax.dot_general(ds, qb[:, sl], _TN, preferred_element_type=F32)

        def loop_body(i, carry):
            step(i, False)
            return carry

        step(j, True)
        lax.fori_loop(j + 1, nq, loop_body, 0)
        dk_ref[...] = dk_s[...]
        dv_ref[...] = dv_s[...]

    return pl.pallas_call(
        body, name=name, grid=(HEADS // 2, nq),
        out_shape=(jax.ShapeDtypeStruct((S, HEADS * HEAD_PAD), F32), jax.ShapeDtypeStruct((S, HEADS * V_HEAD), F32)),
        in_specs=[pl.BlockSpec((S, PW), lambda p, j: (0, p)),
                  pl.BlockSpec((TQ, PW), lambda p, j: (j, p)),
                  pl.BlockSpec((TQ, LANES), lambda p, j: (j, p)),
                  pl.BlockSpec((S, LANES), lambda p, j: (0, p)),
                  pl.BlockSpec((S, LANES), lambda p, j: (0, p)),
                  pl.BlockSpec((S, LANES), lambda p, j: (0, p))],
        out_specs=(pl.BlockSpec((TQ, PW), lambda p, j: (j, p)),
                   pl.BlockSpec((TQ, LANES), lambda p, j: (j, p))),
        scratch_shapes=[pltpu.VMEM((TQ, PW), F32), pltpu.VMEM((TQ, LANES), F32)],
        compiler_params=_cparams(),
    )(q, k, v, do, lse, delta)


ATTN_FWD_HEADS = 8
ATTN_BWD_HEADS = 4


def _chunk_mask_t(T):
    key = lax.broadcasted_iota(jnp.int32, (T, T), 0) // CHUNK
    qry = lax.broadcasted_iota(jnp.int32, (T, T), 1) // CHUNK
    return key <= qry


def _qkv_fwd_t(qn, kvn, z, tabs, w_q, w_kv, name):
    S = qn.shape[0]
    T = _attn_tile(S)
    HW = HEADS * HEAD_PAD
    scale = 1.0 / math.sqrt(QK_NOPE + QK_ROPE)

    def body(qn_ref, kvn_ref, kr_ref, ct_ref, ut_ref, dt_ref, wq_ref, wkv_ref, q_ref, k_ref, v_ref, kt_ref, vt_ref):
        ct, ut, dt = ct_ref[...], ut_ref[...], dt_ref[...]
        qa = jnp.dot(qn_ref[...], wq_ref[...], preferred_element_type=F32)
        kva = jnp.dot(kvn_ref[...], wkv_ref[...], preferred_element_type=F32)
        kr = _f32(kr_ref)
        ones_row = (lax.broadcasted_iota(jnp.int32, (V_HEAD, T), 0) == 0).astype(F32)
        for h in range(HEADS):
            sl = slice(h * HEAD_PAD, (h + 1) * HEAD_PAD)
            q_ref[:, sl] = (_rope(qa[:, sl], ct, ut, dt) * scale).astype(BF16)
            kh = _rope(kva[:, sl] + kr, ct, ut, dt)
            k_ref[:, sl] = kh.astype(BF16)
            kt_ref[0, sl, :] = kh.T.astype(BF16)
        v_ref[...] = kva[:, HW:].astype(BF16)
        for p in range(HEADS // 2):
            vpt = kva[:, HW + p * LANES:HW + (p + 1) * LANES].T
            for h in range(2):
                r0 = (2 * p + h) * HEAD_PAD
                vt_ref[0, r0:r0 + V_HEAD, :] = vpt[h * V_HEAD:(h + 1) * V_HEAD, :].astype(BF16)
                vt_ref[0, r0 + V_HEAD:r0 + HEAD_PAD, :] = ones_row.astype(BF16)

    t3 = jax.ShapeDtypeStruct((S // T, HW, T), BF16)
    return pl.pallas_call(
        body, name=name, grid=(S // T,),
        out_shape=(jax.ShapeDtypeStruct((S, HW), BF16), jax.ShapeDtypeStruct((S, HW), BF16),
                   jax.ShapeDtypeStruct((S, HEADS * V_HEAD), BF16), t3, t3),
        in_specs=[_rows(T, QL), _rows(T, KVL), _rows(T, HEAD_PAD, 23),
                  _rows(T, HEAD_PAD), _rows(T, HEAD_PAD), _rows(T, HEAD_PAD),
                  _const(w_q.shape), _const(w_kv.shape)],
        out_specs=(_rows(T, HW), _rows(T, HW), _rows(T, HEADS * V_HEAD),
                   pl.BlockSpec((1, HW, T), lambda i: (i, 0, 0)), pl.BlockSpec((1, HW, T), lambda i: (i, 0, 0))),
        compiler_params=_cparams(),
    )(qn, kvn, z, *tabs, w_q, w_kv)


def _attn_fwd_t(q, k, vT3, name):
    S = q.shape[0]
    T = _attn_tile(S)
    nq = S // T
    NH = ATTN_FWD_HEADS
    PW = NH * HEAD_PAD

    def body(q_ref, k_ref, vt_ref, o_ref, lse_ref, m_s, acc_s):
        i = pl.program_id(1)
        m_s[...] = jnp.full_like(m_s, NEG)
        acc_s[...] = jnp.zeros_like(acc_s)
        qv = q_ref[...]

        def step(j, masked):
            kb = k_ref[pl.ds(pl.multiple_of(j * T, T), T), :]
            vt = vt_ref[j]
            heads = [slice(h * HEAD_PAD, (h + 1) * HEAD_PAD) for h in range(NH)]
            sts = [lax.dot_general(kb[:, sl], qv[:, sl], _NT, preferred_element_type=F32) for sl in heads]
            alphas, pvs = [], []
            for h, sl in enumerate(heads):
                st = jnp.where(_chunk_mask_t(T), sts[h], NEG) if masked else sts[h]
                m_prev = m_s[h]
                m_new = jnp.maximum(m_prev, jnp.max(st, axis=0, keepdims=True))
                alphas.append(jnp.exp(m_prev[0:1] - m_new[0:1]))
                pt = jnp.exp(st - m_new[0:1]).astype(BF16)
                m_s[h] = m_new
                pvs.append(jnp.dot(vt[sl, :], pt, preferred_element_type=F32))
            for h in range(NH):
                acc_s[h] = acc_s[h] * alphas[h] + pvs[h]

        def loop_body(j, carry):
            step(j, False)
            return carry

        lax.fori_loop(0, i, loop_body, 0)
        step(i, True)
        for g in range(NH // 2):
            outs = []
            for h in (2 * g, 2 * g + 1):
                acc = acc_s[h]
                l_row = acc[V_HEAD:V_HEAD + 1, :]
                outs.append(acc[0:V_HEAD, :] / l_row)
                lse_ref[0, h * SUBLANES:(h + 1) * SUBLANES, :] = m_s[h] + jnp.log(l_row)
            o_ref[:, g * LANES:(g + 1) * LANES] = jnp.concatenate(outs, axis=0).T

    return pl.pallas_call(
        body, name=name, grid=(HEADS // NH, nq),
        out_shape=(jax.ShapeDtypeStruct((S, HEADS * V_HEAD), F32),
                   jax.ShapeDtypeStruct((nq, HEADS * SUBLANES, T), F32)),
        in_specs=[pl.BlockSpec((T, PW), lambda p, i: (i, p)),
                  pl.BlockSpec((S, PW), lambda p, i: (0, p)),
                  pl.BlockSpec((nq, PW, T), lambda p, i: (0, p, 0))],
        out_specs=(pl.BlockSpec((T, NH * V_HEAD), lambda p, i: (i, p)),
                   pl.BlockSpec((1, NH * SUBLANES, T), lambda p, i: (i, p, 0))),
        scratch_shapes=[pltpu.VMEM((NH, SUBLANES, T), F32), pltpu.VMEM((NH, HEAD_PAD, T), F32)],
        compiler_params=_cparams(),
    )(q, k, vT3)


def _attn_bwd_t(q, k, v, kT3, do, lse3, dl3, name):
    S = q.shape[0]
    T = _attn_tile(S)
    nq = S // T
    NH = ATTN_BWD_HEADS
    PW = NH * HEAD_PAD
    VW = NH * V_HEAD

    def body(q_ref, k_ref, v_ref, kt_ref, do_ref, lse_ref, dl_ref, dq_ref, dk_ref, dv_ref, dk_s, dv_s):
        j = pl.program_id(1)
        left = lax.broadcasted_iota(jnp.int32, (T, LANES), 1) < V_HEAD

        @pl.when(j == 0)
        def _():
            dq_ref[...] = jnp.zeros_like(dq_ref)

        dk_s[...] = jnp.zeros_like(dk_s)
        dv_s[...] = jnp.zeros_like(dv_s)
        kb = k_ref[...]
        vms = []
        for g in range(NH // 2):
            vb = v_ref[:, g * LANES:(g + 1) * LANES]
            vms += [jnp.where(left, vb, jnp.zeros_like(vb)), jnp.where(left, jnp.zeros_like(vb), vb)]
        kt = kt_ref[0]

        def step(i, masked):
            r0 = pl.multiple_of(i * T, T)
            qb = q_ref[pl.ds(r0, T), :]
            do_all = do_ref[pl.ds(r0, T), :]
            lse = lse_ref[i]
            dl = dl_ref[i]
            heads = [slice(h * HEAD_PAD, (h + 1) * HEAD_PAD) for h in range(NH)]
            dobs = [do_all[:, (h // 2) * LANES:(h // 2 + 1) * LANES] for h in range(NH)]
            sts = [lax.dot_general(kb[:, sl], qb[:, sl], _NT, preferred_element_type=F32) for sl in heads]
            dpts = [lax.dot_general(vms[h], dobs[h], _NT, preferred_element_type=F32) for h in range(NH)]
            res = []
            for h, sl in enumerate(heads):
                r8 = h * SUBLANES
                pt = jnp.exp(sts[h] - lse[r8:r8 + 1, :])
                if masked:
                    pt = jnp.where(_chunk_mask_t(T), pt, 0.0)
                dst = (pt * (dpts[h] - dl[r8:r8 + 1, :])).astype(BF16)
                res.append((jnp.dot(pt.astype(BF16), dobs[h], preferred_element_type=F32),
                            jnp.dot(dst, qb[:, sl], preferred_element_type=F32),
                            jnp.dot(kt[sl, :], dst, preferred_element_type=F32)))
            for h, sl in enumerate(heads):
                dv_s[h] += res[h][0]
                dk_s[:, sl] += res[h][1]
                dq_ref[i, sl, :] += res[h][2]

        def loop_body(i, carry):
            step(i, False)
            return carry

        step(j, True)
        lax.fori_loop(j + 1, nq, loop_body, 0)
        dk_ref[...] = dk_s[...]
        for g in range(NH // 2):
            dv_ref[:, g * LANES:(g + 1) * LANES] = jnp.where(left, dv_s[2 * g], dv_s[2 * g + 1])

    return pl.pallas_call(
        body, name=name, grid=(HEADS // NH, nq),
        out_shape=(jax.ShapeDtypeStruct((nq, HEADS * HEAD_PAD, T), F32),
                   jax.ShapeDtypeStruct((S, HEADS * HEAD_PAD), F32), jax.ShapeDtypeStruct((S, HEADS * V_HEAD), F32)),
        in_specs=[pl.BlockSpec((S, PW), lambda p, j: (0, p)),
                  pl.BlockSpec((T, PW), lambda p, j: (j, p)),
                  pl.BlockSpec((T, VW), lambda p, j: (j, p)),
                  pl.BlockSpec((1, PW, T), lambda p, j: (j, p, 0)),
                  pl.BlockSpec((S, VW), lambda p, j: (0, p)),
                  pl.BlockSpec((nq, NH * SUBLANES, T), lambda p, j: (0, p, 0)),
                  pl.BlockSpec((nq, NH * SUBLANES, T), lambda p, j: (0, p, 0))],
        out_specs=(pl.BlockSpec((nq, PW, T), lambda p, j: (0, p, 0)),
                   pl.BlockSpec((T, PW), lambda p, j: (j, p)),
                   pl.BlockSpec((T, VW), lambda p, j: (j, p))),
        scratch_shapes=[pltpu.VMEM((T, PW), F32), pltpu.VMEM((NH, T, LANES), F32)],
        compiler_params=_cparams(),
    )(q, k, v, kT3, do, lse3, dl3)


def _even_post(z, cv, o, name):
    S = z.shape[0]
    T = _row_tile(S)

    def body(ab_ref, ag_ref, bg_ref, cv_ref, o_ref, y_ref):
        y_ref[:, 0:SCW] = (_f32(ab_ref) * cv_ref[...] * _silu(_f32(ag_ref))).astype(BF16)
        y_ref[:, SCW:2 * SCW] = (o_ref[...] * _silu(_f32(bg_ref))).astype(BF16)

    return pl.pallas_call(
        body, name=name, grid=(S // T,),
        out_shape=jax.ShapeDtypeStruct((S, 2 * SCW), BF16),
        in_specs=[_rows(T, SCW, 0), _rows(T, SCW, 3), _rows(T, SCW, 4), _rows(T, SCW), _rows(T, SCW)],
        out_specs=_rows(T, 2 * SCW), compiler_params=_cparams(),
    )(z, z, z, cv, o)


def _even_bwd_gates(dyc, z, cv, o, name):
    S = z.shape[0]
    T = _row_tile(S)

    def body(dya_ref, dyb_ref, ab_ref, ag_ref, bg_ref, cv_ref, o_ref,
             dab_ref, dag_ref, dbg_ref, dcv_ref, do_ref, dl_ref):
        dya, ab, ag, cv = dya_ref[...], _f32(ab_ref), _f32(ag_ref), cv_ref[...]
        sg = _silu(ag)
        dab_ref[...] = (dya * cv * sg).astype(BF16)
        dcv_ref[...] = dya * ab * sg
        dag_ref[...] = (dya * ab * cv * _dsilu(ag)).astype(BF16)
        dyb, bg, ov = dyb_ref[...], _f32(bg_ref), o_ref[...]
        dov = dyb * _silu(bg)
        do_ref[...] = dov.astype(BF16)
        dbg_ref[...] = (dyb * ov * _dsilu(bg)).astype(BF16)
        prod = dov * ov
        left = lax.broadcasted_iota(jnp.int32, (T, LANES), 1) < V_HEAD
        for p in range(HEADS // 2):
            blk = prod[:, p * LANES:(p + 1) * LANES]
            s0 = jnp.sum(jnp.where(left, blk, 0.0), axis=1, keepdims=True)
            s1 = jnp.sum(jnp.where(left, 0.0, blk), axis=1, keepdims=True)
            dt = jnp.where(left, s0, s1).T
            dl_ref[0, 2 * p * SUBLANES:(2 * p + 1) * SUBLANES, :] = dt[0:SUBLANES, :]
            dl_ref[0, (2 * p + 1) * SUBLANES:(2 * p + 2) * SUBLANES, :] = dt[V_HEAD:V_HEAD + SUBLANES, :]

    assert T == _attn_tile(S)
    bf = jax.ShapeDtypeStruct((S, SCW), BF16)
    ff = jax.ShapeDtypeStruct((S, SCW), F32)
    return pl.pallas_call(
        body, name=name, grid=(S // T,),
        out_shape=(bf, bf, bf, ff, bf, jax.ShapeDtypeStruct((S // T, HEADS * SUBLANES, T), F32)),
        in_specs=[_rows(T, SCW, 0), _rows(T, SCW, 1), _rows(T, SCW, 0), _rows(T, SCW, 3), _rows(T, SCW, 4),
                  _rows(T, SCW), _rows(T, SCW)],
        out_specs=(_rows(T, SCW),) * 5 + (pl.BlockSpec((1, HEADS * SUBLANES, T), lambda i: (i, 0, 0)),),
        compiler_params=_cparams(),
    )(dyc, dyc, z, z, z, cv, o)


def _qkv_bwd(dq, dk, dv, z, tabs, w_q, w_kv, qg, kvg, name):
    S = dk.shape[0]
    T = _attn_tile(S)
    HW = HEADS * HEAD_PAD
    VW = HEADS * V_HEAD
    scale = 1.0 / math.sqrt(QK_NOPE + QK_ROPE)

    def fn(ins, outs):
        dq_ref, dk_ref, dv_ref, cq_ref, ckv_ref, ct_ref, ut_ref, dt_ref, wq_ref, wkv_ref, qg_ref, kvg_ref = ins
        dqp_ref, dkvp_ref, dcq_ref, dckv_ref, dkr_ref = outs
        ct, ut, dt = ct_ref[...], ut_ref[...], dt_ref[...]
        dkr = jnp.zeros((T, HEAD_PAD), F32)
        for h in range(HEADS):
            sl = slice(h * HEAD_PAD, (h + 1) * HEAD_PAD)
            dqp_ref[:, sl] = (_rope_t(dq_ref[0, sl, :].T, ct, ut, dt) * scale).astype(BF16)
            dkh = _rope_t(dk_ref[:, sl], ct, ut, dt)
            dkr = dkr + dkh
            dkvp_ref[:, sl] = dkh.astype(BF16)
        dkvp_ref[:, HW:] = dv_ref[...].astype(BF16)
        dkr_ref[...] = dkr.astype(BF16)
        sums = []
        for lat_ref, g_ref, dpre_ref, w_ref, dlat_ref in ((cq_ref, qg_ref, dqp_ref, wq_ref, dcq_ref),
                                                         (ckv_ref, kvg_ref, dkvp_ref, wkv_ref, dckv_ref)):
            dn = lax.dot_general(dpre_ref[...], w_ref[...], _NT, preferred_element_type=F32)
            xv = _f32(lat_ref)
            rstd = lax.rsqrt(jnp.mean(xv * xv, axis=-1, keepdims=True) + EPS)
            xh = xv * rstd
            dxh = dn * g_ref[...]
            dlat_ref[...] = (rstd * (dxh - xh * jnp.mean(dxh * xh, axis=-1, keepdims=True))).astype(BF16)
            sums.append(dn * xh)
        return sums

    return _col_sums(
        2, fn, [dq, dk, dv, z, z, *tabs, w_q, w_kv, qg, kvg],
        [pl.BlockSpec((1, HW, T), lambda i: (i, 0, 0)), _rows(T, HW), _rows(T, VW), _rows(T, QL, 10), _rows(T, KVL, 22),
         _rows(T, HEAD_PAD), _rows(T, HEAD_PAD), _rows(T, HEAD_PAD),
         _const(w_q.shape), _const(w_kv.shape), _const((1, QL)), _const((1, KVL))],
        [jax.ShapeDtypeStruct((S, HW), BF16), jax.ShapeDtypeStruct((S, HW + VW), BF16),
         jax.ShapeDtypeStruct((S, QL), BF16), jax.ShapeDtypeStruct((S, KVL), BF16),
         jax.ShapeDtypeStruct((S, HEAD_PAD), BF16)],
        [_rows(T, HW), _rows(T, HW + VW), _rows(T, QL), _rows(T, KVL), _rows(T, HEAD_PAD)],
        S, T, [QL, KVL], name)


def _even_dz(dab, du, z, dag, dbg, dcq, dckv, dkr, name):
    S = z.shape[0]
    T = _row_tile(S)

    def body(dab_ref, du_ref, ac_ref, ax_ref, dag_ref, dbg_ref, dcq_ref, dckv_ref, dkr_ref, dz_ref):
        duv = du_ref[...]
        dz_ref[:, 0:SCW] = dab_ref[...]
        dz_ref[:, SCW:2 * SCW] = (duv * _f32(ax_ref)).astype(BF16)
        dz_ref[:, 2 * SCW:3 * SCW] = (duv * _f32(ac_ref)).astype(BF16)
        dz_ref[:, 3 * SCW:4 * SCW] = dag_ref[...]
        dz_ref[:, 4 * SCW:5 * SCW] = dbg_ref[...]
        dz_ref[:, 5 * SCW:5 * SCW + QL] = dcq_ref[...]
        dz_ref[:, 5 * SCW + QL:5 * SCW + QL + KVL] = dckv_ref[...]
        dz_ref[:, 5 * SCW + QL + KVL:ZE] = dkr_ref[...]

    return pl.pallas_call(
        body, name=name, grid=(S // T,),
        out_shape=jax.ShapeDtypeStruct((S, ZE), BF16),
        in_specs=[_rows(T, SCW), _rows(T, SCW), _rows(T, SCW, 1), _rows(T, SCW, 2), _rows(T, SCW), _rows(T, SCW),
                  _rows(T, QL), _rows(T, KVL), _rows(T, HEAD_PAD)],
        out_specs=_rows(T, ZE), compiler_params=_cparams(),
    )(dab, du, z, z, dag, dbg, dcq, dckv, dkr)


def _odd_pre(z, name):
    S, D = z.shape[0], z.shape[1] // 3
    T = _row_tile(S)

    def body(val_ref, glu_ref, u_ref):
        u_ref[...] = _f32(val_ref) * _sigmoid(_f32(glu_ref))

    return pl.pallas_call(
        body, name=name, grid=(S // T,),
        out_shape=jax.ShapeDtypeStruct((S, D), F32),
        in_specs=[_rows(T, D, 0), _rows(T, D, 1)], out_specs=_rows(T, D),
        compiler_params=_cparams(),
    )(z, z)


def _layer_norm_stats(cv):
    mu = jnp.mean(cv, axis=-1, keepdims=True)
    cen = cv - mu
    rstd = lax.rsqrt(jnp.mean(cen * cen, axis=-1, keepdims=True) + EPS)
    return cen * rstd, rstd


def _odd_post(cv, z, ln_g, ln_b, name):
    S, D = cv.shape
    T = _row_tile(S)

    def body(cv_ref, sg_ref, g_ref, b_ref, y_ref):
        cvh, _ = _layer_norm_stats(cv_ref[...])
        y_ref[...] = (_silu(cvh * g_ref[...] + b_ref[...]) * _silu(_f32(sg_ref))).astype(BF16)

    return pl.pallas_call(
        body, name=name, grid=(S // T,),
        out_shape=jax.ShapeDtypeStruct((S, D), BF16),
        in_specs=[_rows(T, D), _rows(T, D, 2), _const((1, D)), _const((1, D))],
        out_specs=_rows(T, D), compiler_params=_cparams(),
    )(cv, z, ln_g, ln_b)


def _odd_bwd_norm(dyi, cv, z, ln_g, ln_b, name):
    S, D = cv.shape
    T = _row_tile(S)

    def fn(ins, outs):
        dy_ref, cv_ref, sg_ref, g_ref, b_ref = ins
        dcv_ref, dsg_ref = outs
        cvh, rstd = _layer_norm_stats(cv_ref[...])
        ln = cvh * g_ref[...] + b_ref[...]
        sgv, dy = _f32(sg_ref), dy_ref[...]
        dsg_ref[...] = (dy * _silu(ln) * _dsilu(sgv)).astype(BF16)
        dln = dy * _silu(sgv) * _dsilu(ln)
        dh = dln * g_ref[...]
        dcv_ref[...] = rstd * (dh - jnp.mean(dh, axis=-1, keepdims=True)
                               - cvh * jnp.mean(dh * cvh, axis=-1, keepdims=True))
        return [dln * cvh, dln]

    return _col_sums(2, fn, [dyi, cv, z, ln_g, ln_b],
                     [_rows(T, D), _rows(T, D), _rows(T, D, 2), _const((1, D)), _const((1, D))],
                     [jax.ShapeDtypeStruct((S, D), F32), jax.ShapeDtypeStruct((S, D), BF16)],
                     [_rows(T, D), _rows(T, D)], S, T, [D, D], name)


def _odd_dz(du, z, dsg, name):
    S, D = du.shape
    T = _row_tile(S)

    def body(du_ref, val_ref, glu_ref, dsg_ref, dz_ref):
        duv = du_ref[...]
        sig = _sigmoid(_f32(glu_ref))
        dz_ref[:, 0:D] = (duv * sig).astype(BF16)
        dz_ref[:, D:2 * D] = (duv * _f32(val_ref) * sig * (1.0 - sig)).astype(BF16)
        dz_ref[:, 2 * D:3 * D] = dsg_ref[...]

    return pl.pallas_call(
        body, name=name, grid=(S // T,),
        out_shape=jax.ShapeDtypeStruct((S, 3 * D), BF16),
        in_specs=[_rows(T, D), _rows(T, D, 0), _rows(T, D, 1), _rows(T, D)],
        out_specs=_rows(T, 3 * D), compiler_params=_cparams(),
    )(du, z, z, dsg)


ADAM_BLOCK_ELEMS = 128 * 1024


def _adam_tiles(R, C):
    if R * C <= ADAM_BLOCK_ELEMS:
        return R, C
    tr = R
    for cand in range(SUBLANES, R, SUBLANES):
        if R % cand == 0 and cand * C <= ADAM_BLOCK_ELEMS:
            tr = cand
    if tr < R:
        return tr, C
    tc = C
    for cand in range(LANES, C, LANES):
        if C % cand == 0 and R * cand <= ADAM_BLOCK_ELEMS:
            tc = cand
    return R, tc


def _adamw(g_parts, w, m, v, name):
    if not isinstance(g_parts, (list, tuple)):
        g_parts = [g_parts]
    ng = len(g_parts)
    _, R, C = g_parts[0].shape
    tr, tc = _adam_tiles(R, C)

    def body(*refs):
        g_refs = refs[:ng]
        w_ref, m_ref, v_ref, go_ref, d_ref, mo_ref, vo_ref = refs[ng:]
        g = None
        for g_ref in g_refs:
            for p in range(g_ref.shape[0]):
                part = g_ref[p].astype(F32)
                g = part if g is None else g + part
        mn = ADAM_B1 * m_ref[...] + (1.0 - ADAM_B1) * g
        vn = ADAM_B2 * v_ref[...] + (1.0 - ADAM_B2) * (g * g)
        m_hat = mn / (1.0 - ADAM_B1 ** ADAM_STEP)
        v_hat = vn / (1.0 - ADAM_B2 ** ADAM_STEP)
        go_ref[...] = g
        d_ref[...] = -ADAM_LR * (m_hat / (jnp.sqrt(v_hat) + ADAM_EPS) + ADAM_WD * w_ref[...])
        mo_ref[...] = mn
        vo_ref[...] = vn

    slab = jax.ShapeDtypeStruct((R, C), F32)
    blk = pl.BlockSpec((tr, tc), lambda i, j: (i, j))
    return pl.pallas_call(
        body, name=name, grid=(R // tr, C // tc),
        out_shape=(slab,) * 4,
        in_specs=[pl.BlockSpec((g.shape[0], tr, tc), lambda i, j: (0, i, j)) for g in g_parts] + [blk, blk, blk],
        out_specs=(blk,) * 4, compiler_params=_cparams(),
    )(*g_parts, w, m, v)


def _gather_cols(g, shape):
    nd = len(shape)
    t = jnp.moveaxis(g, 0, nd - 1)
    return t.reshape(tuple(shape[:-1]) + (N_DEV * shape[-1],))


def _scatter_cols(full, n):
    t = full.reshape(full.shape[:-1] + (N_DEV, n))
    return jnp.moveaxis(t, -2, 0)


def kernel(x, c, positions, ada_w, ada_b, pre_norm_g, post_norm_g, even_w_in, even_sc_conv_w, even_sc_conv_b, even_q_norm_g, even_kv_norm_g, even_w_uq, even_w_ukv, even_w_out, odd_w_in, odd_conv_w, odd_conv_b, odd_ln_g, odd_ln_b, odd_w_out, loss_target, m_ada_w, m_ada_b, m_pre_norm_g, m_post_norm_g, m_even_w_in, m_even_sc_conv_w, m_even_sc_conv_b, m_even_q_norm_g, m_even_kv_norm_g, m_even_w_uq, m_even_w_ukv, m_even_w_out, m_odd_w_in, m_odd_conv_w, m_odd_conv_b, m_odd_ln_g, m_odd_ln_b, m_odd_w_out, v_ada_w, v_ada_b, v_pre_norm_g, v_post_norm_g, v_even_w_in, v_even_sc_conv_w, v_even_sc_conv_b, v_even_q_norm_g, v_even_kv_norm_g, v_even_w_uq, v_even_w_ukv, v_even_w_out, v_odd_w_in, v_odd_conv_w, v_odd_conv_b, v_odd_ln_g, v_odd_ln_b, v_odd_w_out):
    S, D = x.shape[1], x.shape[2]
    L = ada_w.shape[0]
    NE, NO = even_w_in.shape[0], odd_w_in.shape[0]
    me = 4 * lax.axis_index("x") + 2 * lax.axis_index("y") + lax.axis_index("c")
    x0 = x[0]
    target = loss_target[0]

    small_parts = [c, even_sc_conv_w, odd_conv_w, odd_conv_b, odd_ln_g, odd_ln_b]
    small_shapes = [p.shape for p in small_parts]
    sg = _exchange([_pack(small_parts, F32, SUBLANES)], False, "gather_small")[0].reshape(N_DEV, -1)
    c_all, scw_g, ocw_g, ocb_g, olg_g, olb_g = _unpack(sg, small_shapes)
    c_all = c_all.reshape(N_DEV, D)
    sc_conv_w = _gather_cols(scw_g, even_sc_conv_w.shape)
    o_conv_w = _gather_cols(ocw_g, odd_conv_w.shape)
    o_conv_b = _gather_cols(ocb_g, odd_conv_b.shape)
    o_ln_g = _gather_cols(olg_g, odd_ln_g.shape)
    o_ln_b = _gather_cols(olb_g, odd_ln_b.shape)

    pad_q = HEAD_PAD - QK_NOPE - QK_ROPE
    w_local = [jnp.swapaxes(even_w_in, 1, 2).astype(BF16),
               jnp.pad(even_w_uq, ((0, 0), (0, 0), (0, pad_q))).astype(BF16),
               jnp.pad(even_w_ukv[..., :QK_NOPE], ((0, 0), (0, 0), (0, HEAD_PAD - QK_NOPE))).astype(BF16),
               even_w_ukv[..., QK_NOPE:].astype(BF16),
               even_w_out.astype(BF16), odd_w_in.astype(BF16), odd_w_out.astype(BF16)]
    n_ada = ada_w.shape[2]
    ada_b_cols = lax.dynamic_slice_in_dim(ada_b, me * n_ada, n_ada, axis=1).reshape(L, 1, n_ada)
    mod_slab = _ada_fwd(c_all, ada_w, ada_b_cols)
    mod_g = _exchange([_pack([mod_slab], F32, SUBLANES)], False, "gather_mod")[0].reshape(N_DEV, -1)
    mod_all = mod_g[:, :L * N_DEV * n_ada].reshape(N_DEV, L, N_DEV, n_ada)
    mod = lax.dynamic_index_in_dim(mod_all, me, axis=2, keepdims=False)
    mod = jnp.moveaxis(mod, 0, 1).reshape(L, 3 * D)
    shift, scale, gate = mod[:, :D], mod[:, D:2 * D], mod[:, 2 * D:]

    heads_to_cols = lambda g: jnp.moveaxis(g, 0, 1).reshape(g.shape[1], -1)
    w_handles = {}
    token = jnp.broadcast_to(jnp.minimum(jnp.abs(mod[0, 0]), 0.0), (SUBLANES, LANES))
    for layer in range(L):
        i = layer // 2
        groups = ({"in": [w_local[0][i]], "rest": [w[i] for w in w_local[1:5]]} if layer % 2 == 0
                  else {"all": [w[i] for w in w_local[5:]]})
        for key, mine in groups.items():
            mine = [w + token[0, 0].astype(BF16) for w in mine]
            w_handles[layer, key], token = _exchange_start(mine, False, f"gather_weights_start_l{layer}_{key}")
    w_token = token

    def arrived(layer, key, after):
        return _exchange_wait(w_handles[layer, key], False, after, f"gather_weights_wait_l{layer}_{key}")[1]

    e_w_in_k, e_w_q_k, e_w_kv_k, e_w_out, o_w_in, o_w_out = ([None] * NE, [None] * NE, [None] * NE, [None] * NE,
                                                             [None] * NO, [None] * NO)

    half = QK_ROPE // 2
    inv_freq = 1.0 / (ROPE_THETA ** (jnp.arange(0, QK_ROPE, 2, dtype=F32) / QK_ROPE))
    inv_lane = jnp.zeros((HEAD_PAD,), F32).at[QK_NOPE:QK_NOPE + QK_ROPE].set(jnp.concatenate([inv_freq, inv_freq]))
    tabs = _rope_tables(positions.astype(F32).reshape(S, 1), inv_lane.reshape(1, HEAD_PAD))
    del half

    row = lambda a: a.reshape(1, -1)
    scb = even_sc_conv_b
    KP3, KP31 = SUBLANES, 32

    saved = []
    xs = x0
    h = _pre_norm(xs, row(pre_norm_g[0]), row(scale[0]), row(shift[0]), "pre_norm_l0")
    for layer in range(L):
        i = layer // 2
        tag = f"l{layer}"
        first = w_token if layer == 0 else h
        if layer % 2 == 0:
            wt = arrived(layer, "in", first)[0].reshape(-1, D)
            e_w_in_k[i] = jnp.concatenate([wt[:2048], wt[2464:2976], wt[2048:2432], jnp.zeros((QK_NOPE, D), BF16),
                                           wt[2432:2464], jnp.zeros((pad_q, D), BF16)], axis=0)
            z = _matmul(h, e_w_in_k[i], "nt", BF16, f"w_in_{tag}", tn=1024)
            eq_g, ek_g, ev_g, eout_g = arrived(layer, "rest", z)
            e_w_q_k[i] = heads_to_cols(eq_g)
            e_w_kv_k[i] = jnp.concatenate([heads_to_cols(ek_g), heads_to_cols(ev_g)], axis=-1)
            e_w_out[i] = eout_g.reshape(-1, D)
            u, qn, kvn = _even_pre(z, row(even_q_norm_g[i]), row(even_kv_norm_g[i]), f"even_pre_{tag}")
            cw = jnp.pad(sc_conv_w[i], ((0, KP3 - SC_KERNEL), (0, 0)))
            cv = _conv_fwd(u, cw, row(scb[i]), SC_KERNEL, f"conv_{tag}")
            q, k, v, kT3, vT3 = _qkv_fwd_t(qn, kvn, z, tabs, e_w_q_k[i], e_w_kv_k[i], f"qkv_{tag}")
            o, lse = _attn_fwd_t(q, k, vT3, f"attn_{tag}")
            ycat = _even_post(z, cv, o, f"even_post_{tag}")
            y = _matmul(ycat, e_w_out[i], "nn", F32, f"w_out_{tag}", tn=1024)
            saved.append(dict(x=xs, h=h, z=z, u=u, qn=qn, kvn=kvn, cw=cw, cv=cv, q=q, k=k, v=v, kT3=kT3, o=o, lse=lse,
                              ycat=ycat, y=y))
        else:
            owin_g, oout_g = arrived(layer, "all", first)
            o_w_in[i], o_w_out[i] = heads_to_cols(owin_g), oout_g.reshape(-1, D)
            z = _matmul(h, o_w_in[i], "nn", BF16, f"w_in_{tag}", tn=1024)
            u = _odd_pre(z, f"odd_pre_{tag}")
            cw = jnp.pad(o_conv_w[i], ((0, KP31 - CONF_KERNEL), (0, 0)))
            cv = _conv_fwd(u, cw, row(o_conv_b[i]), CONF_KERNEL, f"conv_{tag}")
            yin = _odd_post(cv, z, row(o_ln_g[i]), row(o_ln_b[i]), f"odd_post_{tag}")
            y = _matmul(yin, o_w_out[i], "nn", F32, f"w_out_{tag}", tn=1024)
            saved.append(dict(x=xs, h=h, z=z, u=u, cw=cw, cv=cv, yin=yin, y=y))
        if layer < L - 1:
            xs, h = _post_pre_norm(xs, y, row(post_norm_g[layer]), row(gate[layer]), row(pre_norm_g[layer + 1]),
                                   row(scale[layer + 1]), row(shift[layer + 1]), f"post_pre_norm_{tag}")

    dx, dy, err_sq, dgate, g_post_last = _loss_post_norm_bwd(xs, y, row(post_norm_g[L - 1]), row(gate[L - 1]), target,
                                                             "loss_post_norm_bwd")
    loss = lax.psum(_scaled_total(err_sq, 0.5 / D, "loss_total")[0, 0], MESH_AXES)

    g_pre, g_post, dmod = [None] * L, [None] * L, [None] * L
    g_e_w_in, g_e_w_uq, g_e_w_ukv, g_e_w_out = [None] * NE, [None] * NE, [None] * NE, [None] * NE
    g_scw, g_scb, g_qg, g_kvg = [None] * NE, [None] * NE, [None] * NE, [None] * NE
    g_o_w_in, g_o_w_out, g_ocw, g_ocb, g_olg, g_olb = ([None] * NO for _ in range(6))
    sm_w = [even_sc_conv_w, odd_conv_w, odd_conv_b, odd_ln_g, odd_ln_b]
    sm_rows = _pack(sm_w, F32, SUBLANES).shape[0]

    def small_slab():
        full = [_scatter_cols(jnp.stack(g_scw), even_sc_conv_w.shape[-1]),
                _scatter_cols(jnp.stack(g_ocw), odd_conv_w.shape[-1]),
                _scatter_cols(jnp.concatenate(g_ocb, 0), odd_conv_b.shape[-1]),
                _scatter_cols(jnp.concatenate(g_olg, 0), odd_ln_g.shape[-1]),
                _scatter_cols(jnp.concatenate(g_olb, 0), odd_ln_b.shape[-1])]
        flat = jnp.concatenate([g.reshape(N_DEV, -1) for g in full], axis=1)
        return jnp.pad(flat, ((0, 0), (0, sm_rows * PACK_COLS - flat.shape[1]))).reshape(N_DEV, sm_rows, PACK_COLS)

    s_handles, own_parts = [None] * L, [None] * L
    bw_token = jnp.zeros((SUBLANES, LANES), F32)

    def start_scatter(layer, parts):
        if layer == 0:
            parts = parts + [small_slab()]
        own_parts[layer] = [lax.dynamic_slice_in_dim(g, me, 1, axis=0) for g in parts]
        s_handles[layer], token = _exchange_start([g.astype(BF16) for g in parts], True,
                                                  f"scatter_grads_start_l{layer}")
        return token

    for layer in reversed(range(L)):
        i = layer // 2
        tag = f"l{layer}"
        sv = saved[layer]
        if layer == L - 1:
            g_post[layer] = g_post_last
        if layer % 2 == 0:
            dyc = _matmul(dy, e_w_out[i], "nt", F32, f"d_ycat_{tag}", tn=1024)
            g_e_w_out[i] = _matmul(sv["ycat"], dy, "tn", F32, f"g_w_out_{tag}", tn=1024)
            dab, dag, dbg, dcv, do, delta = _even_bwd_gates(dyc, sv["z"], sv["cv"], sv["o"], f"even_gates_bwd_{tag}")
            du, dcw, g_scb[i] = _conv_bwd(dcv, sv["u"], sv["cw"], SC_KERNEL, f"conv_bwd_{tag}")
            g_scw[i] = dcw[:SC_KERNEL]
            dq, dk, dv = _attn_bwd_t(sv["q"], sv["k"], sv["v"], sv["kT3"], do, sv["lse"], delta, f"attn_bwd_{tag}")
            (dqp, dkvp, dcq, dckv, dkr, g_qg[i], g_kvg[i]) = _qkv_bwd(
                dq, dk, dv, sv["z"], tabs, e_w_q_k[i], e_w_kv_k[i],
                row(even_q_norm_g[i]), row(even_kv_norm_g[i]), f"qkv_bwd_{tag}")
            gq = _matmul(sv["qn"], dqp, "tn", F32, f"g_w_uq_{tag}", tn=1024)
            gkv = _matmul(sv["kvn"], dkvp, "tn", F32, f"g_w_ukv_{tag}")
            g_e_w_uq[i] = jnp.moveaxis(gq.reshape(QL, HEADS, HEAD_PAD)[..., :QK_NOPE + QK_ROPE], 1, 0)
            g_e_w_ukv[i] = jnp.moveaxis(jnp.concatenate(
                [gkv[:, :HEADS * HEAD_PAD].reshape(KVL, HEADS, HEAD_PAD)[..., :QK_NOPE],
                 gkv[:, HEADS * HEAD_PAD:].reshape(KVL, HEADS, V_HEAD)], axis=-1), 1, 0)
            g_e_w_out[i] = g_e_w_out[i].reshape(N_DEV, -1, D)
            dz = _even_dz(dab, du, sv["z"], dag, dbg, dcq, dckv, dkr, f"even_dz_{tag}")
            gt = _matmul(dz, sv["h"], "tn", F32, f"g_w_in_{tag}", tn=1024)
            g_e_w_in[i] = jnp.concatenate([gt[:2048], gt[2560:2944], gt[2944 + QK_NOPE:2944 + QK_NOPE + QK_ROPE],
                                           gt[2048:2560]], axis=0).reshape(N_DEV, -1, D)
            bw_token = start_scatter(layer, [g_e_w_in[i], g_e_w_uq[i], g_e_w_ukv[i], g_e_w_out[i]])
            dh = _matmul(dz, e_w_in_k[i], "nn", F32, f"d_h_{tag}", tn=1024)
        else:
            dyi = _matmul(dy, o_w_out[i], "nt", F32, f"d_yin_{tag}", tn=1024)
            g_o_w_out[i] = _matmul(sv["yin"], dy, "tn", F32, f"g_w_out_{tag}", tn=1024).reshape(N_DEV, -1, D)
            dcv, dsg, g_olg[i], g_olb[i] = _odd_bwd_norm(dyi, sv["cv"], sv["z"], row(o_ln_g[i]), row(o_ln_b[i]),
                                                         f"odd_norm_bwd_{tag}")
            du, dcw, g_ocb[i] = _conv_bwd(dcv, sv["u"], sv["cw"], CONF_KERNEL, f"conv_bwd_{tag}")
            g_ocw[i] = dcw[:CONF_KERNEL]
            dz = _odd_dz(du, sv["z"], dsg, f"odd_dz_{tag}")
            g_o_w_in[i] = jnp.moveaxis(_matmul(sv["h"], dz, "tn", F32, f"g_w_in_{tag}", tn=1024)
                                       .reshape(D, N_DEV, -1), 1, 0)
            bw_token = start_scatter(layer, [g_o_w_in[i], g_o_w_out[i]])
            dh = _matmul(dz, o_w_in[i], "nt", F32, f"d_h_{tag}", tn=1024)
        g_row = row(pre_norm_g[layer]) + bw_token[0, 0]
        if layer > 0:
            (dx, dy, dshift, dscale, g_pre[layer], dgate_prev, g_post[layer - 1]) = _pre_post_norm_bwd(
                dh, sv["x"], dx, g_row, row(scale[layer]), saved[layer - 1]["y"], row(post_norm_g[layer - 1]),
                row(gate[layer - 1]), f"pre_post_norm_bwd_{tag}")
        else:
            dx, dshift, dscale, g_pre[layer] = _pre_norm_bwd(dh, sv["x"], dx, g_row, row(scale[layer]),
                                                             f"pre_norm_bwd_{tag}")
            dgate_prev = None
        dmod[layer] = jnp.concatenate([dshift, dscale, dgate], axis=-1)
        dgate = dgate_prev
    grad_x = dx.reshape(1, S, D)

    rep_g = [jnp.concatenate(dmod, 0), jnp.concatenate(g_pre, 0), jnp.concatenate(g_post, 0),
             jnp.stack(g_scb), jnp.stack(g_qg), jnp.stack(g_kvg)]
    rep_w = [ada_b, pre_norm_g, post_norm_g, even_sc_conv_b, even_q_norm_g, even_kv_norm_g]
    rep_m = [m_ada_b, m_pre_norm_g, m_post_norm_g, m_even_sc_conv_b, m_even_q_norm_g, m_even_kv_norm_g]
    rep_v = [v_ada_b, v_pre_norm_g, v_post_norm_g, v_even_sc_conv_b, v_even_q_norm_g, v_even_kv_norm_g]
    rep_shapes = [w.shape for w in rep_w]
    rep_all = _exchange([_pack(rep_g, F32, SUBLANES)], False, "gather_small_grads")[0]
    rep_out = _adamw(rep_all, _pack(rep_w, F32, SUBLANES), _pack(rep_m, F32, SUBLANES), _pack(rep_v, F32, SUBLANES),
                     "adamw_replicated")
    rep_res = [_unpack(o.reshape(-1), rep_shapes) for o in rep_out]

    dmod_all = rep_all.reshape(N_DEV, -1)[:, :L * 3 * D].reshape(N_DEV, L, 3 * D)
    dmod_cols = jnp.moveaxis(lax.dynamic_slice_in_dim(dmod_all, me * n_ada, n_ada, axis=2), 0, 1)
    g_ada_w = _ada_bwd(c_all.T, dmod_cols)
    ada_out = _adamw(g_ada_w.reshape(1, -1, PACK_COLS), ada_w.reshape(-1, PACK_COLS),
                     m_ada_w.reshape(-1, PACK_COLS), v_ada_w.reshape(-1, PACK_COLS), "adamw_ada_w")
    ada_res = [o.reshape(ada_w.shape) for o in ada_out]

    sm_m = [m_even_sc_conv_w, m_odd_conv_w, m_odd_conv_b, m_odd_ln_g, m_odd_ln_b]
    sm_v = [v_even_sc_conv_w, v_odd_conv_w, v_odd_conv_b, v_odd_ln_g, v_odd_ln_b]
    sm_shapes = [w.shape for w in sm_w]
    state = {"even_w_in": (even_w_in, m_even_w_in, v_even_w_in), "even_w_uq": (even_w_uq, m_even_w_uq, v_even_w_uq),
             "even_w_ukv": (even_w_ukv, m_even_w_ukv, v_even_w_ukv), "even_w_out": (even_w_out, m_even_w_out, v_even_w_out),
             "odd_w_in": (odd_w_in, m_odd_w_in, v_odd_w_in), "odd_w_out": (odd_w_out, m_odd_w_out, v_odd_w_out)}
    big_res = {name: [[None] * len(state[name][0]) for _ in range(4)] for name in state}
    after = bw_token
    sm_res = None
    for layer in reversed(range(L)):
        i = layer // 2
        names = ["even_w_in", "even_w_uq", "even_w_ukv", "even_w_out"] if layer % 2 == 0 else ["odd_w_in", "odd_w_out"]
        _, landed = _exchange_wait(s_handles[layer], True, after, f"scatter_grads_wait_l{layer}")
        own = own_parts[layer]
        for a, name in enumerate(names):
            transposed = name == "even_w_in"
            wmv = [t[i].T if transposed else t[i] for t in state[name]]
            res = _adamw([own[a], landed[a]], *wmv, f"adamw_{name}_{i}")
            for kind in range(4):
                big_res[name][kind][i] = res[kind].T if transposed else res[kind]
            after = res[0]
        if layer == 0:
            sm_out = _adamw([own[-1], landed[-1]], _pack(sm_w, F32, SUBLANES), _pack(sm_m, F32, SUBLANES),
                            _pack(sm_v, F32, SUBLANES), "adamw_small_sharded")
            sm_res = [_unpack(o.reshape(-1), sm_shapes) for o in sm_out]
    sh_res = [dict(zip(["even_sc_conv_w", "odd_conv_w", "odd_conv_b", "odd_ln_g", "odd_ln_b"], sm_res[kind]))
              for kind in range(4)]
    for name in state:
        for kind in range(4):
            sh_res[kind][name] = jnp.stack(big_res[name][kind])

    order = ["ada_w", "ada_b", "pre_norm_g", "post_norm_g", "even_w_in", "even_sc_conv_w", "even_sc_conv_b",
             "even_q_norm_g", "even_kv_norm_g", "even_w_uq", "even_w_ukv", "even_w_out", "odd_w_in", "odd_conv_w",
             "odd_conv_b", "odd_ln_g", "odd_ln_b", "odd_w_out"]
    rep_names = ["ada_b", "pre_norm_g", "post_norm_g", "even_sc_conv_b", "even_q_norm_g", "even_kv_norm_g"]
    outs = [loss, grad_x]
    for kind in range(4):
        for name in order:
            if name == "ada_w":
                outs.append(ada_res[kind])
            elif name in rep_names:
                outs.append(rep_res[kind][rep_names.index(name)])
            else:
                outs.append(sh_res[kind][name])
    return tuple(outs)
```

```python
import functools
import math

import jax
import jax.numpy as jnp
from jax import lax
from jax.experimental import pallas as pl
from jax.experimental.pallas import tpu as pltpu

F32 = jnp.float32
BF16 = jnp.bfloat16
MESH_AXES = ("x", "y", "c")
N_DEV = 8
EPS = 1e-6
CHUNK = 64
HEADS = 8
QK_NOPE = 64
QK_ROPE = 32
V_HEAD = 64
HEAD_PAD = 128
ROPE_THETA = 10000.0
SC_KERNEL = 3
CONF_KERNEL = 31
LANES = 128
SUBLANES = 8
PACK_COLS = 1024
VMEM_LIMIT = 48 * 1024 * 1024
NEG = -1e30

ADAM_LR = 0.001
ADAM_B1 = 0.9
ADAM_B2 = 0.999
ADAM_EPS = 1e-08
ADAM_WD = 0.01
ADAM_STEP = 10


def _cparams():
    return pltpu.CompilerParams(vmem_limit_bytes=VMEM_LIMIT)


def _sigmoid(x):
    return 1.0 / (1.0 + jnp.exp(-x))


def _f32(ref):
    return ref[...].astype(F32)


def _silu(x):
    return x * _sigmoid(x)


def _dsilu(x):
    s = _sigmoid(x)
    return s * (1.0 + x * (1.0 - s))


def _rows(T, width, cb=0):
    return pl.BlockSpec((T, width), lambda i: (i, cb))


def _const(shape):
    nd = len(shape)
    return pl.BlockSpec(shape, lambda i: (0,) * nd)


def _row_tile(S):
    return min(256, S)


def _exchange(srcs, scatter, name):
    n = len(srcs)
    shapes = [tuple(s.shape[1:]) if scatter else tuple(s.shape) for s in srcs]

    def body(*refs):
        src_refs, out_refs = refs[:n], refs[n:2 * n]
        send_sems, recv_sems, local_sems = refs[2 * n:]
        x, y, c = lax.axis_index("x"), lax.axis_index("y"), lax.axis_index("c")
        me = 4 * x + 2 * y + c
        owns, copies = [], []
        for a in range(n):
            def piece(d, a=a):
                return src_refs[a].at[d] if scatter else src_refs[a]

            own = pltpu.make_async_copy(piece(me), out_refs[a].at[me], local_sems.at[a])
            own.start()
            owns.append(own)
            for k in range(1, N_DEV):
                px, py, pc = x ^ ((k >> 2) & 1), y ^ ((k >> 1) & 1), c ^ (k & 1)
                peer = 4 * px + 2 * py + pc
                sem = a * (N_DEV - 1) + k - 1
                cp = pltpu.make_async_remote_copy(
                    src_ref=piece(peer), dst_ref=out_refs[a].at[me],
                    send_sem=send_sems.at[sem], recv_sem=recv_sems.at[sem],
                    device_id=(px, py, pc), device_id_type=pl.DeviceIdType.MESH)
                cp.start()
                arrival = pltpu.make_async_remote_copy(
                    src_ref=piece(peer), dst_ref=out_refs[a].at[peer],
                    send_sem=send_sems.at[sem], recv_sem=recv_sems.at[sem],
                    device_id=(x, y, c), device_id_type=pl.DeviceIdType.MESH)
                copies.append((cp, arrival))
        for _, arrival in copies:
            arrival.wait_recv()
        for cp, _ in copies:
            cp.wait_send()
        for own in owns:
            own.wait()

    return pl.pallas_call(
        body, name=name,
        out_shape=tuple(jax.ShapeDtypeStruct((N_DEV,) + shp, s.dtype) for shp, s in zip(shapes, srcs)),
        in_specs=[pl.BlockSpec(memory_space=pl.ANY)] * n,
        out_specs=tuple(pl.BlockSpec(memory_space=pl.ANY) for _ in range(n)),
        scratch_shapes=[pltpu.SemaphoreType.DMA((n * (N_DEV - 1),)),
                        pltpu.SemaphoreType.DMA((n * (N_DEV - 1),)),
                        pltpu.SemaphoreType.DMA((n,))],
    )(*srcs)


_HBM = pl.BlockSpec(memory_space=pltpu.HBM)
_SEM = pl.BlockSpec(memory_space=pltpu.SEMAPHORE)


def _peer(k):
    x, y, c = lax.axis_index("x"), lax.axis_index("y"), lax.axis_index("c")
    return x ^ ((k >> 2) & 1), y ^ ((k >> 1) & 1), c ^ (k & 1)


def _exchange_start(srcs, scatter, name):
    n = len(srcs)
    shapes = [tuple(s.shape[1:]) if scatter else tuple(s.shape) for s in srcs]
    slots = N_DEV - 1 if scatter else N_DEV
    lands = [lax.empty((slots,) + shp, s.dtype) for shp, s in zip(shapes, srcs)]
    if not scatter:
        here = 4 * lax.axis_index("x") + 2 * lax.axis_index("y") + lax.axis_index("c")
        lands = [lax.dynamic_update_index_in_dim(l, s, here, 0) for l, s in zip(lands, srcs)]

    def body(*refs):
        src_refs, land_refs = refs[:n], refs[n:2 * n]
        send_sems, recv_sems = refs[2 * n], refs[2 * n + 1]
        token = refs[4 * n + 2]
        me = 4 * lax.axis_index("x") + 2 * lax.axis_index("y") + lax.axis_index("c")
        for a in range(n):
            for k in range(1, N_DEV):
                px, py, pc = _peer(k)
                peer = 4 * px + 2 * py + pc
                pltpu.make_async_remote_copy(
                    src_ref=src_refs[a].at[peer] if scatter else src_refs[a],
                    dst_ref=land_refs[a].at[k - 1] if scatter else land_refs[a].at[me],
                    send_sem=send_sems.at[a * (N_DEV - 1) + k - 1], recv_sem=recv_sems.at[a * (N_DEV - 1) + k - 1],
                    device_id=(px, py, pc), device_id_type=pl.DeviceIdType.MESH).start()
        token[...] = jnp.zeros_like(token)

    hbm = lambda arrs: [pltpu.HBM(a.shape, a.dtype) for a in arrs]
    out = pl.pallas_call(
        body, name=name,
        out_shape=(pltpu.SemaphoreType.DMA((n * (N_DEV - 1),)), pltpu.SemaphoreType.DMA((n * (N_DEV - 1),)),
                   *hbm(srcs), *hbm(lands), jax.ShapeDtypeStruct((SUBLANES, LANES), F32)),
        in_specs=[_HBM] * (2 * n),
        out_specs=(_SEM, _SEM, *([_HBM] * (2 * n)), pl.BlockSpec(memory_space=pltpu.VMEM)),
        input_output_aliases={a: 2 + a for a in range(2 * n)},
        compiler_params=pltpu.CompilerParams(has_side_effects=pltpu.SideEffectType.DATAFLOW_SIDE_EFFECTING),
    )(*[pltpu.with_memory_space_constraint(s, pltpu.HBM) for s in srcs],
      *[pltpu.with_memory_space_constraint(l, pltpu.HBM) for l in lands])
    return (out[0], out[1], list(out[2:2 + n]), list(out[2 + n:2 + 2 * n])), out[2 + 2 * n]


def _exchange_wait(handle, scatter, after, name):
    send_sems, recv_sems, srcs, lands = handle
    n = len(srcs)
    after = list(after) if isinstance(after, (list, tuple)) else [after]

    def body(*refs):
        src_refs, land_refs = refs[:n], refs[n:2 * n]
        send_sems, recv_sems = refs[2 * n], refs[2 * n + 1]
        for a in range(n):
            for k in range(1, N_DEV):
                px, py, pc = _peer(k)
                peer = 4 * px + 2 * py + pc
                cp = pltpu.make_async_remote_copy(
                    src_ref=src_refs[a].at[peer] if scatter else src_refs[a],
                    dst_ref=land_refs[a].at[k - 1] if scatter else land_refs[a].at[peer],
                    send_sem=send_sems.at[a * (N_DEV - 1) + k - 1], recv_sem=recv_sems.at[a * (N_DEV - 1) + k - 1],
                    device_id=(px, py, pc), device_id_type=pl.DeviceIdType.MESH)
                cp.wait_send()
                cp.wait_recv()

    out = pl.pallas_call(
        body, name=name,
        out_shape=tuple(pltpu.HBM(a.shape, a.dtype) for a in srcs + lands),
        in_specs=[_HBM] * (2 * n) + [_SEM, _SEM] + [pl.BlockSpec(memory_space=pl.ANY)] * len(after),
        out_specs=tuple([_HBM] * (2 * n)),
        input_output_aliases={a: a for a in range(2 * n)},
        compiler_params=pltpu.CompilerParams(has_side_effects=pltpu.SideEffectType.DATAFLOW_SIDE_EFFECTING),
    )(*srcs, *lands, send_sems, recv_sems, *after)
    return list(out[:n]), list(out[n:])


def _pack(parts, dtype, row_mult):
    flat = jnp.concatenate([p.reshape(-1).astype(dtype) for p in parts])
    n = flat.shape[0]
    rows = -(-n // PACK_COLS)
    rows = -(-rows // row_mult) * row_mult
    flat = jnp.pad(flat, (0, rows * PACK_COLS - n))
    return flat.reshape(rows, PACK_COLS)


def _unpack(flat, shapes):
    out, off = [], 0
    for shp in shapes:
        n = math.prod(shp)
        out.append(flat[..., off:off + n].reshape(flat.shape[:-1] + tuple(shp)))
        off += n
    return out


_DIMS = {"nn": (((1,), (0,)), ((), ())), "nt": (((1,), (1,)), ((), ())), "tn": (((0,), (0,)), ((), ()))}


def _matmul(a, b, mode, out_dtype, name, tm=512, tn=512, tk=None):
    if mode == "nn":
        (M, K), (_, N) = a.shape, b.shape
    elif mode == "nt":
        (M, K), (N, _) = a.shape, b.shape
    else:
        (K, M), (_, N) = a.shape, b.shape
    tm, tn = min(tm, M), min(tn, N)
    tk = K if tk is None else min(tk, K)
    nk = K // tk
    assert M % tm == 0 and N % tn == 0 and K % tk == 0, (name, a.shape, b.shape)

    def body(a_ref, b_ref, o_ref, *scratch):
        p = lax.dot_general(a_ref[...].astype(BF16), b_ref[...].astype(BF16), _DIMS[mode],
                            preferred_element_type=F32)
        if nk == 1:
            o_ref[...] = p.astype(out_dtype)
        else:
            acc = scratch[0]
            k = pl.program_id(2)

            @pl.when(k == 0)
            def _():
                acc[...] = p

            @pl.when(k > 0)
            def _():
                acc[...] += p

            @pl.when(k == nk - 1)
            def _():
                o_ref[...] = acc[...].astype(out_dtype)

    a_spec = (pl.BlockSpec((tk, tm), lambda i, j, k: (k, i)) if mode == "tn"
              else pl.BlockSpec((tm, tk), lambda i, j, k: (i, k)))
    b_spec = (pl.BlockSpec((tn, tk), lambda i, j, k: (j, k)) if mode == "nt"
              else pl.BlockSpec((tk, tn), lambda i, j, k: (k, j)))
    return pl.pallas_call(
        body, name=name, grid=(M // tm, N // tn, nk),
        out_shape=jax.ShapeDtypeStruct((M, N), out_dtype),
        in_specs=[a_spec, b_spec],
        out_specs=pl.BlockSpec((tm, tn), lambda i, j, k: (i, j)),
        scratch_shapes=[pltpu.VMEM((tm, tn), F32)] if nk > 1 else [],
        compiler_params=_cparams(),
    )(a, b)


def _ada_fwd(c_all, ada_w, ada_b_cols):
    L, D, n = ada_w.shape

    def body(c_ref, w_ref, b_ref, o_ref):
        act = _silu(c_ref[...]).astype(BF16)
        o_ref[0] = jnp.dot(act, w_ref[0].astype(BF16), preferred_element_type=F32) + b_ref[0]

    return pl.pallas_call(
        body, name="ada_fwd", grid=(L,),
        out_shape=jax.ShapeDtypeStruct((L, N_DEV, n), F32),
        in_specs=[pl.BlockSpec((N_DEV, D), lambda l: (0, 0)),
                  pl.BlockSpec((1, D, n), lambda l: (l, 0, 0)),
                  pl.BlockSpec((1, 1, n), lambda l: (l, 0, 0))],
        out_specs=pl.BlockSpec((1, N_DEV, n), lambda l: (l, 0, 0)),
        compiler_params=_cparams(),
    )(c_all, ada_w, ada_b_cols)


def _ada_bwd(c_all_t, dmod_cols):
    D = c_all_t.shape[0]
    L, _, n = dmod_cols.shape

    def body(c_ref, d_ref, o_ref):
        act = _silu(c_ref[...])
        dm = d_ref[0]
        acc = act[:, 0:1] * dm[0:1, :]
        for b in range(1, N_DEV):
            acc = acc + act[:, b:b + 1] * dm[b:b + 1, :]
        o_ref[0] = acc

    return pl.pallas_call(
        body, name="ada_bwd", grid=(L,),
        out_shape=jax.ShapeDtypeStruct((L, D, n), F32),
        in_specs=[pl.BlockSpec((D, N_DEV), lambda l: (0, 0)),
                  pl.BlockSpec((1, N_DEV, n), lambda l: (l, 0, 0))],
        out_specs=pl.BlockSpec((1, D, n), lambda l: (l, 0, 0)),
        compiler_params=_cparams(),
    )(c_all_t, dmod_cols)


def _rope_tables(pos_col, inv_lane):
    S = pos_col.shape[0]
    T = _row_tile(S)
    half = QK_ROPE // 2

    def body(p_ref, f_ref, c_ref, up_ref, dn_ref):
        ang = p_ref[...] * f_ref[...]
        lane = lax.broadcasted_iota(jnp.int32, ang.shape, 1)
        first = (lane >= QK_NOPE) & (lane < QK_NOPE + half)
        second = (lane >= QK_NOPE + half) & (lane < QK_NOPE + QK_ROPE)
        cs, sn = jnp.cos(ang), jnp.sin(ang)
        c_ref[...] = jnp.where(first | second, cs, 1.0)
        up_ref[...] = jnp.where(first, -sn, 0.0)
        dn_ref[...] = jnp.where(second, sn, 0.0)

    tab = jax.ShapeDtypeStruct((S, HEAD_PAD), F32)
    return pl.pallas_call(
        body, name="rope_tables", grid=(S // T,),
        out_shape=(tab, tab, tab),
        in_specs=[_rows(T, 1), _const((1, HEAD_PAD))],
        out_specs=(_rows(T, HEAD_PAD),) * 3,
        compiler_params=_cparams(),
    )(pos_col, inv_lane)


def _rope(blk, ct, ut, dt):
    half = QK_ROPE // 2
    up = pltpu.roll(blk, HEAD_PAD - half, 1)
    dn = pltpu.roll(blk, half, 1)
    return blk * ct + up * ut + dn * dt


def _rope_t(d, ct, ut, dt):
    half = QK_ROPE // 2
    return d * ct + pltpu.roll(d * ut, half, 1) + pltpu.roll(d * dt, HEAD_PAD - half, 1)


def _pre_norm(x, g, scale, shift, name):
    S, D = x.shape
    T = _row_tile(S)

    def body(x_ref, g_ref, sc_ref, sh_ref, h_ref):
        xv = x_ref[...]
        rstd = lax.rsqrt(jnp.mean(xv * xv, axis=-1, keepdims=True) + EPS)
        h_ref[...] = ((xv * rstd) * g_ref[...] * (1.0 + sc_ref[...]) + sh_ref[...]).astype(BF16)

    return pl.pallas_call(
        body, name=name, grid=(S // T,),
        out_shape=jax.ShapeDtypeStruct((S, D), BF16),
        in_specs=[_rows(T, D), _const((1, D)), _const((1, D)), _const((1, D))],
        out_specs=_rows(T, D), compiler_params=_cparams(),
    )(x, g, scale, shift)


def _post_norm(x, y, g, gate, name):
    S, D = x.shape
    T = _row_tile(S)

    def body(x_ref, y_ref, g_ref, gt_ref, o_ref):
        yv = y_ref[...]
        rstd = lax.rsqrt(jnp.mean(yv * yv, axis=-1, keepdims=True) + EPS)
        o_ref[...] = x_ref[...] + gt_ref[...] * ((yv * rstd) * g_ref[...])

    return pl.pallas_call(
        body, name=name, grid=(S // T,),
        out_shape=jax.ShapeDtypeStruct((S, D), F32),
        in_specs=[_rows(T, D), _rows(T, D), _const((1, D)), _const((1, D))],
        out_specs=_rows(T, D), compiler_params=_cparams(),
    )(x, y, g, gate)


def _fold8(v):
    T, C = v.shape
    return v.reshape(T // SUBLANES, SUBLANES, C).sum(axis=0)


def _col_sums(n_sums, body_fn, ins, in_specs, outs, out_specs, S, T, widths, name):
    n_in, n_out = len(ins), len(outs)
    nt = S // T

    def body(*refs):
        in_refs = refs[:n_in]
        out_refs = refs[n_in:n_in + n_out]
        sum_refs = refs[n_in + n_out:n_in + n_out + n_sums]
        accs = refs[n_in + n_out + n_sums:]
        i = pl.program_id(0)
        terms = body_fn(in_refs, out_refs)

        @pl.when(i == 0)
        def _():
            for acc, t in zip(accs, terms):
                acc[...] = _fold8(t)

        @pl.when(i > 0)
        def _():
            for acc, t in zip(accs, terms):
                acc[...] += _fold8(t)

        @pl.when(i == nt - 1)
        def _():
            for acc, s_ref in zip(accs, sum_refs):
                s_ref[...] = jnp.sum(acc[...], axis=0, keepdims=True)

    return pl.pallas_call(
        body, name=name, grid=(nt,),
        out_shape=tuple(outs) + tuple(jax.ShapeDtypeStruct((1, w), F32) for w in widths),
        in_specs=in_specs,
        out_specs=tuple(out_specs) + tuple(_const((1, w)) for w in widths),
        scratch_shapes=[pltpu.VMEM((SUBLANES, w), F32) for w in widths],
        compiler_params=_cparams(),
    )(*ins)


def _post_norm_bwd(dxo, y, g, gate, name):
    S, D = y.shape
    T = _row_tile(S)

    def fn(ins, outs):
        dxo_ref, y_ref, g_ref, gt_ref = ins
        yv, dv = y_ref[...], dxo_ref[...]
        rstd = lax.rsqrt(jnp.mean(yv * yv, axis=-1, keepdims=True) + EPS)
        yh = yv * rstd
        dn = dv * gt_ref[...]
        dyh = dn * g_ref[...]
        outs[0][...] = (rstd * (dyh - yh * jnp.mean(dyh * yh, axis=-1, keepdims=True))).astype(BF16)
        return [dv * (yh * g_ref[...]), dn * yh]

    return _col_sums(2, fn, [dxo, y, g, gate],
                     [_rows(T, D), _rows(T, D), _const((1, D)), _const((1, D))],
                     [jax.ShapeDtypeStruct((S, D), BF16)], [_rows(T, D)], S, T, [D, D], name)


def _pre_norm_bwd(dh, x, dxo, g, scale, name):
    S, D = x.shape
    T = _row_tile(S)

    def fn(ins, outs):
        dh_ref, x_ref, dxo_ref, g_ref, sc_ref = ins
        xv, dv = x_ref[...], dh_ref[...]
        rstd = lax.rsqrt(jnp.mean(xv * xv, axis=-1, keepdims=True) + EPS)
        xh = xv * rstd
        dr = dv * (1.0 + sc_ref[...])
        dxh = dr * g_ref[...]
        outs[0][...] = dxo_ref[...] + rstd * (dxh - xh * jnp.mean(dxh * xh, axis=-1, keepdims=True))
        return [dv, dv * (xh * g_ref[...]), dr * xh]

    return _col_sums(3, fn, [dh, x, dxo, g, scale],
                     [_rows(T, D), _rows(T, D), _rows(T, D), _const((1, D)), _const((1, D))],
                     [jax.ShapeDtypeStruct((S, D), F32)], [_rows(T, D)], S, T, [D, D, D], name)


def _post_pre_norm(x, y, g_post, gate, g_pre, scale, shift, name):
    S, D = x.shape
    T = _row_tile(S)

    def body(x_ref, y_ref, gp_ref, gt_ref, g_ref, sc_ref, sh_ref, xn_ref, h_ref):
        yv = y_ref[...]
        rstd_y = lax.rsqrt(jnp.mean(yv * yv, axis=-1, keepdims=True) + EPS)
        xn = x_ref[...] + gt_ref[...] * ((yv * rstd_y) * gp_ref[...])
        xn_ref[...] = xn
        rstd = lax.rsqrt(jnp.mean(xn * xn, axis=-1, keepdims=True) + EPS)
        h_ref[...] = ((xn * rstd) * g_ref[...] * (1.0 + sc_ref[...]) + sh_ref[...]).astype(BF16)

    return pl.pallas_call(
        body, name=name, grid=(S // T,),
        out_shape=(jax.ShapeDtypeStruct((S, D), F32), jax.ShapeDtypeStruct((S, D), BF16)),
        in_specs=[_rows(T, D), _rows(T, D)] + [_const((1, D))] * 5,
        out_specs=(_rows(T, D), _rows(T, D)), compiler_params=_cparams(),
    )(x, y, g_post, gate, g_pre, scale, shift)


def _pre_post_norm_bwd(dh, x, dxo, g_pre, scale, y_prev, g_post_prev, gate_prev, name):
    S, D = x.shape
    T = _row_tile(S)

    def fn(ins, outs):
        dh_ref, x_ref, dxo_ref, g_ref, sc_ref, y_ref, gp_ref, gt_ref = ins
        xv, dv = x_ref[...], dh_ref[...]
        rstd = lax.rsqrt(jnp.mean(xv * xv, axis=-1, keepdims=True) + EPS)
        xh = xv * rstd
        dr = dv * (1.0 + sc_ref[...])
        dxh = dr * g_ref[...]
        dx = dxo_ref[...] + rstd * (dxh - xh * jnp.mean(dxh * xh, axis=-1, keepdims=True))
        outs[0][...] = dx
        yv = y_ref[...]
        rstd_y = lax.rsqrt(jnp.mean(yv * yv, axis=-1, keepdims=True) + EPS)
        yh = yv * rstd_y
        dn = dx * gt_ref[...]
        dyh = dn * gp_ref[...]
        outs[1][...] = (rstd_y * (dyh - yh * jnp.mean(dyh * yh, axis=-1, keepdims=True))).astype(BF16)
        return [dv, dv * (xh * g_ref[...]), dr * xh, dx * (yh * gp_ref[...]), dn * yh]

    return _col_sums(5, fn, [dh, x, dxo, g_pre, scale, y_prev, g_post_prev, gate_prev],
                     [_rows(T, D), _rows(T, D), _rows(T, D), _const((1, D)), _const((1, D)),
                      _rows(T, D), _const((1, D)), _const((1, D))],
                     [jax.ShapeDtypeStruct((S, D), F32), jax.ShapeDtypeStruct((S, D), BF16)],
                     [_rows(T, D), _rows(T, D)], S, T, [D] * 5, name)


def _loss_post_norm_bwd(x, y, g_post, gate, target, name):
    S, D = x.shape
    T = _row_tile(S)

    def fn(ins, outs):
        x_ref, y_ref, gp_ref, gt_ref, t_ref = ins
        yv = y_ref[...]
        rstd_y = lax.rsqrt(jnp.mean(yv * yv, axis=-1, keepdims=True) + EPS)
        yh = yv * rstd_y
        e = x_ref[...] + gt_ref[...] * (yh * gp_ref[...]) - t_ref[...]
        dx = e * (1.0 / D)
        outs[0][...] = dx
        dn = dx * gt_ref[...]
        dyh = dn * gp_ref[...]
        outs[1][...] = (rstd_y * (dyh - yh * jnp.mean(dyh * yh, axis=-1, keepdims=True))).astype(BF16)
        return [e * e, dx * (yh * gp_ref[...]), dn * yh]

    return _col_sums(3, fn, [x, y, g_post, gate, target],
                     [_rows(T, D), _rows(T, D), _const((1, D)), _const((1, D)), _rows(T, D)],
                     [jax.ShapeDtypeStruct((S, D), F32), jax.ShapeDtypeStruct((S, D), BF16)],
                     [_rows(T, D), _rows(T, D)], S, T, [D] * 3, name)


def _scaled_total(v, coef, name):
    def body(v_ref, o_ref):
        o_ref[...] = jnp.broadcast_to(jnp.sum(v_ref[...], axis=1, keepdims=True) * coef, (1, LANES))

    return pl.pallas_call(body, name=name, out_shape=jax.ShapeDtypeStruct((1, LANES), F32))(v)


def _loss_head(x, target):
    S, D = x.shape
    T = _row_tile(S)
    nt = S // T

    def body(x_ref, t_ref, l_ref, dx_ref, acc):
        i = pl.program_id(0)
        e = x_ref[...] - t_ref[...]
        dx_ref[...] = e * (1.0 / D)
        part = _fold8(e * e)

        @pl.when(i == 0)
        def _():
            acc[...] = part

        @pl.when(i > 0)
        def _():
            acc[...] += part

        @pl.when(i == nt - 1)
        def _():
            tot = jnp.sum(jnp.sum(acc[...], axis=0, keepdims=True), axis=1, keepdims=True)
            l_ref[...] = jnp.broadcast_to(tot * (0.5 / D), (1, LANES))

    return pl.pallas_call(
        body, name="loss_head", grid=(nt,),
        out_shape=(jax.ShapeDtypeStruct((1, LANES), F32), jax.ShapeDtypeStruct((S, D), F32)),
        in_specs=[_rows(T, D), _rows(T, D)],
        out_specs=(_const((1, LANES)), _rows(T, D)),
        scratch_shapes=[pltpu.VMEM((SUBLANES, D), F32)],
        compiler_params=_cparams(),
    )(x, target)


CONV_ROWS = 64


def _conv_halo(K):
    return SUBLANES if K - 1 <= SUBLANES else 32


def _conv_fwd(u, w, b, K, name):
    S, C = u.shape
    KP = w.shape[0]
    T, HB, RS = min(512, S), _conv_halo(K), CONV_ROWS
    ratio = T // HB

    def body(u_ref, h_ref, w_ref, b_ref, o_ref, ext):
        i = pl.program_id(1)
        ext[0:HB, :] = jnp.where(i > 0, h_ref[...], 0.0)
        ext[HB:HB + T, :] = u_ref[...]
        for r0 in range(0, T, RS):
            acc = jnp.broadcast_to(b_ref[...], (RS, LANES))
            for k in range(K):
                off = HB - (K - 1) + k + r0
                acc = acc + w_ref[k:k + 1, :] * ext[off:off + RS, :]
            o_ref[r0:r0 + RS, :] = acc

    return pl.pallas_call(
        body, name=name, grid=(C // LANES, S // T),
        out_shape=jax.ShapeDtypeStruct((S, C), F32),
        in_specs=[pl.BlockSpec((T, LANES), lambda c, i: (i, c)),
                  pl.BlockSpec((HB, LANES), lambda c, i: (jnp.maximum(i * ratio - 1, 0), c)),
                  pl.BlockSpec((KP, LANES), lambda c, i: (0, c)),
                  pl.BlockSpec((1, LANES), lambda c, i: (0, c))],
        out_specs=pl.BlockSpec((T, LANES), lambda c, i: (i, c)),
        scratch_shapes=[pltpu.VMEM((HB + T, LANES), F32)],
        compiler_params=_cparams(),
    )(u, u, w, b)


def _conv_bwd(d, u, w, K, name):
    S, C = u.shape
    KP = w.shape[0]
    T, HB, RS = min(512, S), _conv_halo(K), CONV_ROWS
    ratio = T // HB
    nt = S // T
    last_halo = S // HB - 1

    def body(d_ref, dn_ref, u_ref, up_ref, w_ref, du_ref, dw_ref, db_ref, extd, extu, dws, dbs):
        i = pl.program_id(1)
        extd[0:T, :] = d_ref[...]
        extd[T:T + HB, :] = jnp.where(i < nt - 1, dn_ref[...], 0.0)
        extu[0:HB, :] = jnp.where(i > 0, up_ref[...], 0.0)
        extu[HB:HB + T, :] = u_ref[...]

        @pl.when(i == 0)
        def _():
            dws[...] = jnp.zeros_like(dws)
            dbs[...] = jnp.zeros_like(dbs)

        for r0 in range(0, T, RS):
            acc = jnp.zeros((RS, LANES), F32)
            for k in range(K):
                off = (K - 1 - k) + r0
                acc = acc + w_ref[k:k + 1, :] * extd[off:off + RS, :]
            du_ref[r0:r0 + RS, :] = acc
            dch = d_ref[r0:r0 + RS, :]
            dbs[...] += _fold8(dch)
            for k in range(K):
                off = HB - (K - 1) + k + r0
                dws[k * SUBLANES:(k + 1) * SUBLANES, :] += _fold8(dch * extu[off:off + RS, :])

        @pl.when(i == nt - 1)
        def _():
            dw_ref[...] = jnp.zeros_like(dw_ref)
            for k in range(K):
                dw_ref[k:k + 1, :] = jnp.sum(dws[k * SUBLANES:(k + 1) * SUBLANES, :], axis=0, keepdims=True)
            db_ref[...] = jnp.sum(dbs[...], axis=0, keepdims=True)

    return pl.pallas_call(
        body, name=name, grid=(C // LANES, nt),
        out_shape=(jax.ShapeDtypeStruct((S, C), F32), jax.ShapeDtypeStruct((KP, C), F32),
                   jax.ShapeDtypeStruct((1, C), F32)),
        in_specs=[pl.BlockSpec((T, LANES), lambda c, i: (i, c)),
                  pl.BlockSpec((HB, LANES), lambda c, i: (jnp.minimum((i + 1) * ratio, last_halo), c)),
                  pl.BlockSpec((T, LANES), lambda c, i: (i, c)),
                  pl.BlockSpec((HB, LANES), lambda c, i: (jnp.maximum(i * ratio - 1, 0), c)),
                  pl.BlockSpec((KP, LANES), lambda c, i: (0, c))],
        out_specs=(pl.BlockSpec((T, LANES), lambda c, i: (i, c)),
                   pl.BlockSpec((KP, LANES), lambda c, i: (0, c)),
                   pl.BlockSpec((1, LANES), lambda c, i: (0, c))),
        scratch_shapes=[pltpu.VMEM((T + HB, LANES), F32), pltpu.VMEM((HB + T, LANES), F32),
                        pltpu.VMEM((KP * SUBLANES, LANES), F32), pltpu.VMEM((SUBLANES, LANES), F32)],
        compiler_params=_cparams(),
    )(d, d, u, u, w)


SCW = 512
ZE = 3072
QL = 256
KVL = 128


def _rms_rows(x, g):
    rstd = lax.rsqrt(jnp.mean(x * x, axis=-1, keepdims=True) + EPS)
    return (x * rstd) * g


def _even_pre(z, qg, kvg, name):
    S = z.shape[0]
    T = _row_tile(S)

    def body(ac_ref, ax_ref, cq_ref, ckv_ref, qg_ref, kvg_ref, u_ref, qn_ref, kvn_ref):
        u_ref[...] = _f32(ac_ref) * _f32(ax_ref)
        qn_ref[...] = _rms_rows(_f32(cq_ref), qg_ref[...]).astype(BF16)
        kvn_ref[...] = _rms_rows(_f32(ckv_ref), kvg_ref[...]).astype(BF16)

    return pl.pallas_call(
        body, name=name, grid=(S // T,),
        out_shape=(jax.ShapeDtypeStruct((S, SCW), F32), jax.ShapeDtypeStruct((S, QL), BF16),
                   jax.ShapeDtypeStruct((S, KVL), BF16)),
        in_specs=[_rows(T, SCW, 1), _rows(T, SCW, 2), _rows(T, QL, 10), _rows(T, KVL, 22),
                  _const((1, QL)), _const((1, KVL))],
        out_specs=(_rows(T, SCW), _rows(T, QL), _rows(T, KVL)),
        compiler_params=_cparams(),
    )(z, z, z, z, qg, kvg)


def _qkv_fwd(qn, kvn, z, tabs, w_q, w_kv, name):
    S = qn.shape[0]
    T = _row_tile(S)
    HW = HEADS * HEAD_PAD
    scale = 1.0 / math.sqrt(QK_NOPE + QK_ROPE)

    def body(qn_ref, kvn_ref, kr_ref, ct_ref, ut_ref, dt_ref, wq_ref, wkv_ref, q_ref, k_ref, v_ref):
        ct, ut, dt = ct_ref[...], ut_ref[...], dt_ref[...]
        qa = jnp.dot(qn_ref[...], wq_ref[...], preferred_element_type=F32)
        kva = jnp.dot(kvn_ref[...], wkv_ref[...], preferred_element_type=F32)
        kr = _f32(kr_ref)
        for h in range(HEADS):
            sl = slice(h * HEAD_PAD, (h + 1) * HEAD_PAD)
            q_ref[:, sl] = (_rope(qa[:, sl], ct, ut, dt) * scale).astype(BF16)
            k_ref[:, sl] = _rope(kva[:, sl] + kr, ct, ut, dt).astype(BF16)
        v_ref[...] = kva[:, HW:].astype(BF16)

    return pl.pallas_call(
        body, name=name, grid=(S // T,),
        out_shape=(jax.ShapeDtypeStruct((S, HW), BF16), jax.ShapeDtypeStruct((S, HW), BF16),
                   jax.ShapeDtypeStruct((S, HEADS * V_HEAD), BF16)),
        in_specs=[_rows(T, QL), _rows(T, KVL), _rows(T, HEAD_PAD, 23),
                  _rows(T, HEAD_PAD), _rows(T, HEAD_PAD), _rows(T, HEAD_PAD),
                  _const(w_q.shape), _const(w_kv.shape)],
        out_specs=(_rows(T, HW), _rows(T, HW), _rows(T, HEADS * V_HEAD)),
        compiler_params=_cparams(),
    )(qn, kvn, z, *tabs, w_q, w_kv)


def _attn_tile(S):
    return min(256, S)


def _chunk_mask(TQ):
    r = lax.broadcasted_iota(jnp.int32, (TQ, TQ), 0) // CHUNK
    c = lax.broadcasted_iota(jnp.int32, (TQ, TQ), 1) // CHUNK
    return c <= r


_NT = (((1,), (1,)), ((), ()))
_TN = (((0,), (0,)), ((), ()))


def _attn_fwd(q, k, v, name):
    S = q.shape[0]
    TQ = _attn_tile(S)
    nq = S // TQ
    PW = 2 * HEAD_PAD

    def body(q_ref, k_ref, v_ref, o_ref, lse_ref, m_s, l_s, acc_s):
        i = pl.program_id(1)
        left = lax.broadcasted_iota(jnp.int32, (TQ, LANES), 1) < V_HEAD
        m_s[...] = jnp.full_like(m_s, NEG)
        l_s[...] = jnp.zeros_like(l_s)
        acc_s[...] = jnp.zeros_like(acc_s)
        qv = q_ref[...]

        def step(j, masked):
            r0 = pl.multiple_of(j * TQ, TQ)
            kb = k_ref[pl.ds(r0, TQ), :]
            vb = v_ref[pl.ds(r0, TQ), :]
            alphas, pvs = [], []
            for h in range(2):
                sl = slice(h * HEAD_PAD, (h + 1) * HEAD_PAD)
                s = lax.dot_general(qv[:, sl], kb[:, sl], _NT, preferred_element_type=F32)
                if masked:
                    s = jnp.where(_chunk_mask(TQ), s, NEG)
                m_prev = m_s[h]
                m_new = jnp.maximum(m_prev, jnp.max(s, axis=1, keepdims=True))
                alpha = jnp.exp(m_prev - m_new)
                p = jnp.exp(s - m_new[:, 0:1])
                l_s[h] = alpha * l_s[h] + jnp.sum(p, axis=1, keepdims=True)
                m_s[h] = m_new
                alphas.append(alpha)
                pvs.append(jnp.dot(p.astype(BF16), vb, preferred_element_type=F32))
            acc_s[...] = acc_s[...] * jnp.where(left, alphas[0], alphas[1]) + jnp.where(left, pvs[0], pvs[1])

        def loop_body(j, carry):
            step(j, False)
            return carry

        lax.fori_loop(0, i, loop_body, 0)
        step(i, True)
        o_ref[...] = acc_s[...] / jnp.where(left, l_s[0], l_s[1])
        lse_ref[...] = jnp.where(left, m_s[0] + jnp.log(l_s[0]), m_s[1] + jnp.log(l_s[1]))

    return pl.pallas_call(
        body, name=name, grid=(HEADS // 2, nq),
        out_shape=(jax.ShapeDtypeStruct((S, HEADS * V_HEAD), F32), jax.ShapeDtypeStruct((S, HEADS * V_HEAD), F32)),
        in_specs=[pl.BlockSpec((TQ, PW), lambda p, i: (i, p)),
                  pl.BlockSpec((S, PW), lambda p, i: (0, p)),
                  pl.BlockSpec((S, LANES), lambda p, i: (0, p))],
        out_specs=(pl.BlockSpec((TQ, LANES), lambda p, i: (i, p)),
                   pl.BlockSpec((TQ, LANES), lambda p, i: (i, p))),
        scratch_shapes=[pltpu.VMEM((2, TQ, LANES), F32), pltpu.VMEM((2, TQ, LANES), F32),
                        pltpu.VMEM((TQ, LANES), F32)],
        compiler_params=_cparams(),
    )(q, k, v)


def _attn_dq(q, k, v, do, lse, delta, name):
    S = q.shape[0]
    TQ = _attn_tile(S)
    nq = S // TQ
    PW = 2 * HEAD_PAD

    def body(q_ref, k_ref, v_ref, do_ref, lse_ref, dl_ref, dq_ref, acc_s):
        i = pl.program_id(1)
        left = lax.broadcasted_iota(jnp.int32, (TQ, LANES), 1) < V_HEAD
        acc_s[...] = jnp.zeros_like(acc_s)
        qv = q_ref[...]
        dov = do_ref[...]
        dos = [jnp.where(left, dov, jnp.zeros_like(dov)), jnp.where(left, jnp.zeros_like(dov), dov)]
        lses = [lse_ref[:, 0:1], lse_ref[:, V_HEAD:V_HEAD + 1]]
        dls = [dl_ref[:, 0:1], dl_ref[:, V_HEAD:V_HEAD + 1]]

        def step(j, masked):
            r0 = pl.multiple_of(j * TQ, TQ)
            kb = k_ref[pl.ds(r0, TQ), :]
            vb = v_ref[pl.ds(r0, TQ), :]
            for h in range(2):
                sl = slice(h * HEAD_PAD, (h + 1) * HEAD_PAD)
                s = lax.dot_general(qv[:, sl], kb[:, sl], _NT, preferred_element_type=F32)
                p = jnp.exp(s - lses[h])
                if masked:
                    p = jnp.where(_chunk_mask(TQ), p, 0.0)
                dp = lax.dot_general(dos[h], vb, _NT, preferred_element_type=F32)
                ds = (p * (dp - dls[h])).astype(BF16)
                acc_s[:, sl] += jnp.dot(ds, kb[:, sl], preferred_element_type=F32)

        def loop_body(j, carry):
            step(j, False)
            return carry

        lax.fori_loop(0, i, loop_body, 0)
        step(i, True)
        dq_ref[...] = acc_s[...]

    return pl.pallas_call(
        body, name=name, grid=(HEADS // 2, nq),
        out_shape=jax.ShapeDtypeStruct((S, HEADS * HEAD_PAD), F32),
        in_specs=[pl.BlockSpec((TQ, PW), lambda p, i: (i, p)),
                  pl.BlockSpec((S, PW), lambda p, i: (0, p)),
                  pl.BlockSpec((S, LANES), lambda p, i: (0, p)),
                  pl.BlockSpec((TQ, LANES), lambda p, i: (i, p)),
                  pl.BlockSpec((TQ, LANES), lambda p, i: (i, p)),
                  pl.BlockSpec((TQ, LANES), lambda p, i: (i, p))],
        out_specs=pl.BlockSpec((TQ, PW), lambda p, i: (i, p)),
        scratch_shapes=[pltpu.VMEM((TQ, PW), F32)],
        compiler_params=_cparams(),
    )(q, k, v, do, lse, delta)


def _attn_dkv(q, k, v, do, lse, delta, name):
    S = q.shape[0]
    TQ = _attn_tile(S)
    nq = S // TQ
    PW = 2 * HEAD_PAD

    def body(q_ref, k_ref, v_ref, do_ref, lse_ref, dl_ref, dk_ref, dv_ref, dk_s, dv_s):
        j = pl.program_id(1)
        left = lax.broadcasted_iota(jnp.int32, (TQ, LANES), 1) < V_HEAD
        dk_s[...] = jnp.zeros_like(dk_s)
        dv_s[...] = jnp.zeros_like(dv_s)
        kb = k_ref[...]
        vb = v_ref[...]

        def step(i, masked):
            r0 = pl.multiple_of(i * TQ, TQ)
            qb = q_ref[pl.ds(r0, TQ), :]
            dov = do_ref[pl.ds(r0, TQ), :]
            lse = lse_ref[pl.ds(r0, TQ), :]
            dl = dl_ref[pl.ds(r0, TQ), :]
            dos = [jnp.where(left, dov, jnp.zeros_like(dov)), jnp.where(left, jnp.zeros_like(dov), dov)]
            for h in range(2):
                sl = slice(h * HEAD_PAD, (h + 1) * HEAD_PAD)
                c0 = h * V_HEAD
                s = lax.dot_general(qb[:, sl], kb[:, sl], _NT, preferred_element_type=F32)
                p = jnp.exp(s - lse[:, c0:c0 + 1])
                if masked:
                    p = jnp.where(_chunk_mask(TQ), p, 0.0)
                dv_s[...] += lax.dot_general(p.astype(BF16), dos[h], _TN, preferred_element_type=F32)
                dp = lax.dot_general(dos[h], vb, _NT, preferred_element_type=F32)
                ds = (p * (dp - dl[:, c0:c0 + 1])).astype(BF16)
                dk_s[:, sl] += lax.dot_general(ds, qb[:, sl], _TN, preferred_element_type=F32)

        def loop_body(i, carry):
            step(i, False)
            return carry

        step(j, True)
        lax.fori_loop(j + 1, nq, loop_body, 0)
        dk_ref[...] = dk_s[...]
        dv_ref[...] = dv_s[...]

    return pl.pallas_call(
        body, name=name, grid=(HEADS // 2, nq),
        out_shape=(jax.ShapeDtypeStruct((S, HEADS * HEAD_PAD), F32), jax.ShapeDtypeStruct((S, HEADS * V_HEAD), F32)),
        in_specs=[pl.BlockSpec((S, PW), lambda p, j: (0, p)),
                  pl.BlockSpec((TQ, PW), lambda p, j: (j, p)),
                  pl.BlockSpec((TQ, LANES), lambda p, j: (j, p)),
                  pl.BlockSpec((S, LANES), lambda p, j: (0, p)),
                  pl.BlockSpec((S, LANES), lambda p, j: (0, p)),
                  pl.BlockSpec((S, LANES), lambda p, j: (0, p))],
        out_specs=(pl.BlockSpec((TQ, PW), lambda p, j: (j, p)),
                   pl.BlockSpec((TQ, LANES), lambda p, j: (j, p))),
        scratch_shapes=[pltpu.VMEM((TQ, PW), F32), pltpu.VMEM((TQ, LANES), F32)],
        compiler_params=_cparams(),
    )(q, k, v, do, lse, delta)


LOG2E = math.log2(math.e)
ATTN_FWD_HEADS = 8
ATTN_BWD_HEADS = 4


def _chunk_mask_t(T):
    key = lax.broadcasted_iota(jnp.int32, (T, T), 0) // CHUNK
    qry = lax.broadcasted_iota(jnp.int32, (T, T), 1) // CHUNK
    return key <= qry


def _qkv_fwd_t(qn, kvn, z, tabs, w_q, w_kv, name):
    S = qn.shape[0]
    T = _attn_tile(S)
    HW = HEADS * HEAD_PAD
    scale = LOG2E / math.sqrt(QK_NOPE + QK_ROPE)

    def body(qn_ref, kvn_ref, kr_ref, ct_ref, ut_ref, dt_ref, wq_ref, wkv_ref, q_ref, k_ref, v_ref, kt_ref, vt_ref):
        ct, ut, dt = ct_ref[...], ut_ref[...], dt_ref[...]
        qa = jnp.dot(qn_ref[...], wq_ref[...], preferred_element_type=F32)
        kva = jnp.dot(kvn_ref[...], wkv_ref[...], preferred_element_type=F32)
        kr = _f32(kr_ref)
        ones_row = (lax.broadcasted_iota(jnp.int32, (V_HEAD, T), 0) == 0).astype(F32)
        for h in range(HEADS):
            sl = slice(h * HEAD_PAD, (h + 1) * HEAD_PAD)
            q_ref[:, sl] = (_rope(qa[:, sl], ct, ut, dt) * scale).astype(BF16)
            kh = _rope(kva[:, sl] + kr, ct, ut, dt)
            k_ref[:, sl] = kh.astype(BF16)
            kt_ref[0, sl, :] = kh.T.astype(BF16)
        v_ref[...] = kva[:, HW:].astype(BF16)
        for p in range(HEADS // 2):
            vpt = kva[:, HW + p * LANES:HW + (p + 1) * LANES].T
            for h in range(2):
                r0 = (2 * p + h) * HEAD_PAD
                vt_ref[0, r0:r0 + V_HEAD, :] = vpt[h * V_HEAD:(h + 1) * V_HEAD, :].astype(BF16)
                vt_ref[0, r0 + V_HEAD:r0 + HEAD_PAD, :] = ones_row.astype(BF16)

    t3 = jax.ShapeDtypeStruct((S // T, HW, T), BF16)
    return pl.pallas_call(
        body, name=name, grid=(S // T,),
        out_shape=(jax.ShapeDtypeStruct((S, HW), BF16), jax.ShapeDtypeStruct((S, HW), BF16),
                   jax.ShapeDtypeStruct((S, HEADS * V_HEAD), BF16), t3, t3),
        in_specs=[_rows(T, QL), _rows(T, KVL), _rows(T, HEAD_PAD, 23),
                  _rows(T, HEAD_PAD), _rows(T, HEAD_PAD), _rows(T, HEAD_PAD),
                  _const(w_q.shape), _const(w_kv.shape)],
        out_specs=(_rows(T, HW), _rows(T, HW), _rows(T, HEADS * V_HEAD),
                   pl.BlockSpec((1, HW, T), lambda i: (i, 0, 0)), pl.BlockSpec((1, HW, T), lambda i: (i, 0, 0))),
        compiler_params=_cparams(),
    )(qn, kvn, z, *tabs, w_q, w_kv)


def _attn_fwd_t(q, k, vT3, name):
    S = q.shape[0]
    T = _attn_tile(S)
    nq = S // T
    NH = ATTN_FWD_HEADS
    PW = NH * HEAD_PAD

    def body(q_ref, k_ref, vt_ref, o_ref, lse_ref, m_s, acc_s):
        i = pl.program_id(1)
        m_s[...] = jnp.full_like(m_s, NEG)
        acc_s[...] = jnp.zeros_like(acc_s)
        qv = q_ref[...]

        def step(j, masked):
            kb = k_ref[pl.ds(pl.multiple_of(j * T, T), T), :]
            vt = vt_ref[j]
            heads = [slice(h * HEAD_PAD, (h + 1) * HEAD_PAD) for h in range(NH)]
            sts = [lax.dot_general(kb[:, sl], qv[:, sl], _NT, preferred_element_type=F32) for sl in heads]
            alphas, pvs = [], []
            for h, sl in enumerate(heads):
                st = jnp.where(_chunk_mask_t(T), sts[h], NEG) if masked else sts[h]
                m_prev = m_s[h]
                m_new = jnp.maximum(m_prev, jnp.max(st, axis=0, keepdims=True))
                alphas.append(jnp.exp2(m_prev[0:1] - m_new[0:1]))
                pt = jnp.exp2(st - m_new[0:1]).astype(BF16)
                m_s[h] = m_new
                pvs.append(jnp.dot(vt[sl, :], pt, preferred_element_type=F32))
            for h in range(NH):
                acc_s[h] = acc_s[h] * alphas[h] + pvs[h]

        def loop_body(j, carry):
            step(j, False)
            return carry

        lax.fori_loop(0, i, loop_body, 0)
        step(i, True)
        for g in range(NH // 2):
            outs = []
            for h in (2 * g, 2 * g + 1):
                acc = acc_s[h]
                l_row = acc[V_HEAD:V_HEAD + 1, :]
                outs.append(acc[0:V_HEAD, :] / l_row)
                lse_ref[0, h * SUBLANES:(h + 1) * SUBLANES, :] = m_s[h] + jnp.log2(l_row)
            o_ref[:, g * LANES:(g + 1) * LANES] = jnp.concatenate(outs, axis=0).T

    return pl.pallas_call(
        body, name=name, grid=(HEADS // NH, nq),
        out_shape=(jax.ShapeDtypeStruct((S, HEADS * V_HEAD), F32),
                   jax.ShapeDtypeStruct((nq, HEADS * SUBLANES, T), F32)),
        in_specs=[pl.BlockSpec((T, PW), lambda p, i: (i, p)),
                  pl.BlockSpec((S, PW), lambda p, i: (0, p)),
                  pl.BlockSpec((nq, PW, T), lambda p, i: (0, p, 0))],
        out_specs=(pl.BlockSpec((T, NH * V_HEAD), lambda p, i: (i, p)),
                   pl.BlockSpec((1, NH * SUBLANES, T), lambda p, i: (i, p, 0))),
        scratch_shapes=[pltpu.VMEM((NH, SUBLANES, T), F32), pltpu.VMEM((NH, HEAD_PAD, T), F32)],
        compiler_params=_cparams(),
    )(q, k, vT3)


def _attn_bwd_t(q, k, v, kT3, do, lse3, dl3, name):
    S = q.shape[0]
    T = _attn_tile(S)
    nq = S // T
    NH = ATTN_BWD_HEADS
    PW = NH * HEAD_PAD
    VW = NH * V_HEAD

    def body(q_ref, k_ref, v_ref, kt_ref, do_ref, lse_ref, dl_ref, dq_ref, dk_ref, dv_ref, dk_s, dv_s):
        j = pl.program_id(1)
        left = lax.broadcasted_iota(jnp.int32, (T, LANES), 1) < V_HEAD

        @pl.when(j == 0)
        def _():
            dq_ref[...] = jnp.zeros_like(dq_ref)

        dk_s[...] = jnp.zeros_like(dk_s)
        dv_s[...] = jnp.zeros_like(dv_s)
        kb = k_ref[...]
        vms = []
        for g in range(NH // 2):
            vb = v_ref[:, g * LANES:(g + 1) * LANES]
            vms += [jnp.where(left, vb, jnp.zeros_like(vb)), jnp.where(left, jnp.zeros_like(vb), vb)]
        kt = kt_ref[0]

        def step(i, masked):
            r0 = pl.multiple_of(i * T, T)
            qb = q_ref[pl.ds(r0, T), :]
            do_all = do_ref[pl.ds(r0, T), :]
            lse = lse_ref[i]
            dl = dl_ref[i]
            heads = [slice(h * HEAD_PAD, (h + 1) * HEAD_PAD) for h in range(NH)]
            dobs = [do_all[:, (h // 2) * LANES:(h // 2 + 1) * LANES] for h in range(NH)]
            sts = [lax.dot_general(kb[:, sl], qb[:, sl], _NT, preferred_element_type=F32) for sl in heads]
            dpts = [lax.dot_general(vms[h], dobs[h], _NT, preferred_element_type=F32) for h in range(NH)]
            res = []
            for h, sl in enumerate(heads):
                r8 = h * SUBLANES
                pt = jnp.exp2(sts[h] - lse[r8:r8 + 1, :])
                if masked:
                    pt = jnp.where(_chunk_mask_t(T), pt, 0.0)
                dst = (pt * (dpts[h] - dl[r8:r8 + 1, :])).astype(BF16)
                res.append((jnp.dot(pt.astype(BF16), dobs[h], preferred_element_type=F32),
                            jnp.dot(dst, qb[:, sl], preferred_element_type=F32),
                            jnp.dot(kt[sl, :], dst, preferred_element_type=F32)))
            for h, sl in enumerate(heads):
                dv_s[h] += res[h][0]
                dk_s[:, sl] += res[h][1]
                dq_ref[i, sl, :] += res[h][2]

        def loop_body(i, carry):
            step(i, False)
            return carry

        step(j, True)
        lax.fori_loop(j + 1, nq, loop_body, 0)
        dk_ref[...] = dk_s[...] * (1.0 / LOG2E)
        for g in range(NH // 2):
            dv_ref[:, g * LANES:(g + 1) * LANES] = jnp.where(left, dv_s[2 * g], dv_s[2 * g + 1])

    return pl.pallas_call(
        body, name=name, grid=(HEADS // NH, nq),
        out_shape=(jax.ShapeDtypeStruct((nq, HEADS * HEAD_PAD, T), F32),
                   jax.ShapeDtypeStruct((S, HEADS * HEAD_PAD), F32), jax.ShapeDtypeStruct((S, HEADS * V_HEAD), F32)),
        in_specs=[pl.BlockSpec((S, PW), lambda p, j: (0, p)),
                  pl.BlockSpec((T, PW), lambda p, j: (j, p)),
                  pl.BlockSpec((T, VW), lambda p, j: (j, p)),
                  pl.BlockSpec((1, PW, T), lambda p, j: (j, p, 0)),
                  pl.BlockSpec((S, VW), lambda p, j: (0, p)),
                  pl.BlockSpec((nq, NH * SUBLANES, T), lambda p, j: (0, p, 0)),
                  pl.BlockSpec((nq, NH * SUBLANES, T), lambda p, j: (0, p, 0))],
        out_specs=(pl.BlockSpec((nq, PW, T), lambda p, j: (0, p, 0)),
                   pl.BlockSpec((T, PW), lambda p, j: (j, p)),
                   pl.BlockSpec((T, VW), lambda p, j: (j, p))),
        scratch_shapes=[pltpu.VMEM((T, PW), F32), pltpu.VMEM((NH, T, LANES), F32)],
        compiler_params=_cparams(),
    )(q, k, v, kT3, do, lse3, dl3)


def _even_post(z, cv, o, name):
    S = z.shape[0]
    T = _row_tile(S)

    def body(ab_ref, ag_ref, bg_ref, cv_ref, o_ref, y_ref):
        y_ref[:, 0:SCW] = (_f32(ab_ref) * cv_ref[...] * _silu(_f32(ag_ref))).astype(BF16)
        y_ref[:, SCW:2 * SCW] = (o_ref[...] * _silu(_f32(bg_ref))).astype(BF16)

    return pl.pallas_call(
        body, name=name, grid=(S // T,),
        out_shape=jax.ShapeDtypeStruct((S, 2 * SCW), BF16),
        in_specs=[_rows(T, SCW, 0), _rows(T, SCW, 3), _rows(T, SCW, 4), _rows(T, SCW), _rows(T, SCW)],
        out_specs=_rows(T, 2 * SCW), compiler_params=_cparams(),
    )(z, z, z, cv, o)


def _even_bwd_gates(dyc, z, cv, o, name):
    S = z.shape[0]
    T = _row_tile(S)

    def body(dya_ref, dyb_ref, ab_ref, ag_ref, bg_ref, cv_ref, o_ref,
             dab_ref, dag_ref, dbg_ref, dcv_ref, do_ref, dl_ref):
        dya, ab, ag, cv = dya_ref[...], _f32(ab_ref), _f32(ag_ref), cv_ref[...]
        sg = _silu(ag)
        dab_ref[...] = (dya * cv * sg).astype(BF16)
        dcv_ref[...] = dya * ab * sg
        dag_ref[...] = (dya * ab * cv * _dsilu(ag)).astype(BF16)
        dyb, bg, ov = dyb_ref[...], _f32(bg_ref), o_ref[...]
        dov = dyb * _silu(bg)
        do_ref[...] = dov.astype(BF16)
        dbg_ref[...] = (dyb * ov * _dsilu(bg)).astype(BF16)
        prod = dov * ov
        left = lax.broadcasted_iota(jnp.int32, (T, LANES), 1) < V_HEAD
        for p in range(HEADS // 2):
            blk = prod[:, p * LANES:(p + 1) * LANES]
            s0 = jnp.sum(jnp.where(left, blk, 0.0), axis=1, keepdims=True)
            s1 = jnp.sum(jnp.where(left, 0.0, blk), axis=1, keepdims=True)
            dt = jnp.where(left, s0, s1).T
            dl_ref[0, 2 * p * SUBLANES:(2 * p + 1) * SUBLANES, :] = dt[0:SUBLANES, :]
            dl_ref[0, (2 * p + 1) * SUBLANES:(2 * p + 2) * SUBLANES, :] = dt[V_HEAD:V_HEAD + SUBLANES, :]

    assert T == _attn_tile(S)
    bf = jax.ShapeDtypeStruct((S, SCW), BF16)
    ff = jax.ShapeDtypeStruct((S, SCW), F32)
    return pl.pallas_call(
        body, name=name, grid=(S // T,),
        out_shape=(bf, bf, bf, ff, bf, jax.ShapeDtypeStruct((S // T, HEADS * SUBLANES, T), F32)),
        in_specs=[_rows(T, SCW, 0), _rows(T, SCW, 1), _rows(T, SCW, 0), _rows(T, SCW, 3), _rows(T, SCW, 4),
                  _rows(T, SCW), _rows(T, SCW)],
        out_specs=(_rows(T, SCW),) * 5 + (pl.BlockSpec((1, HEADS * SUBLANES, T), lambda i: (i, 0, 0)),),
        compiler_params=_cparams(),
    )(dyc, dyc, z, z, z, cv, o)


def _qkv_bwd(dq, dk, dv, z, tabs, w_q, w_kv, qg, kvg, name):
    S = dk.shape[0]
    T = _attn_tile(S)
    HW = HEADS * HEAD_PAD
    VW = HEADS * V_HEAD
    scale = 1.0 / math.sqrt(QK_NOPE + QK_ROPE)

    def fn(ins, outs):
        dq_ref, dk_ref, dv_ref, cq_ref, ckv_ref, ct_ref, ut_ref, dt_ref, wq_ref, wkv_ref, qg_ref, kvg_ref = ins
        dqp_ref, dkvp_ref, dcq_ref, dckv_ref, dkr_ref = outs
        ct, ut, dt = ct_ref[...], ut_ref[...], dt_ref[...]
        dkr = jnp.zeros((T, HEAD_PAD), F32)
        for h in range(HEADS):
            sl = slice(h * HEAD_PAD, (h + 1) * HEAD_PAD)
            dqp_ref[:, sl] = (_rope_t(dq_ref[0, sl, :].T, ct, ut, dt) * scale).astype(BF16)
            dkh = _rope_t(dk_ref[:, sl], ct, ut, dt)
            dkr = dkr + dkh
            dkvp_ref[:, sl] = dkh.astype(BF16)
        dkvp_ref[:, HW:] = dv_ref[...].astype(BF16)
        dkr_ref[...] = dkr.astype(BF16)
        sums = []
        for lat_ref, g_ref, dpre_ref, w_ref, dlat_ref in ((cq_ref, qg_ref, dqp_ref, wq_ref, dcq_ref),
                                                         (ckv_ref, kvg_ref, dkvp_ref, wkv_ref, dckv_ref)):
            dn = lax.dot_general(dpre_ref[...], w_ref[...], _NT, preferred_element_type=F32)
            xv = _f32(lat_ref)
            rstd = lax.rsqrt(jnp.mean(xv * xv, axis=-1, keepdims=True) + EPS)
            xh = xv * rstd
            dxh = dn * g_ref[...]
            dlat_ref[...] = (rstd * (dxh - xh * jnp.mean(dxh * xh, axis=-1, keepdims=True))).astype(BF16)
            sums.append(dn * xh)
        return sums

    return _col_sums(
        2, fn, [dq, dk, dv, z, z, *tabs, w_q, w_kv, qg, kvg],
        [pl.BlockSpec((1, HW, T), lambda i: (i, 0, 0)), _rows(T, HW), _rows(T, VW), _rows(T, QL, 10), _rows(T, KVL, 22),
         _rows(T, HEAD_PAD), _rows(T, HEAD_PAD), _rows(T, HEAD_PAD),
         _const(w_q.shape), _const(w_kv.shape), _const((1, QL)), _const((1, KVL))],
        [jax.ShapeDtypeStruct((S, HW), BF16), jax.ShapeDtypeStruct((S, HW + VW), BF16),
         jax.ShapeDtypeStruct((S, QL), BF16), jax.ShapeDtypeStruct((S, KVL), BF16),
         jax.ShapeDtypeStruct((S, HEAD_PAD), BF16)],
        [_rows(T, HW), _rows(T, HW + VW), _rows(T, QL), _rows(T, KVL), _rows(T, HEAD_PAD)],
        S, T, [QL, KVL], name)


def _even_dz(dab, du, z, dag, dbg, dcq, dckv, dkr, name):
    S = z.shape[0]
    T = _row_tile(S)

    def body(dab_ref, du_ref, ac_ref, ax_ref, dag_ref, dbg_ref, dcq_ref, dckv_ref, dkr_ref, dz_ref):
        duv = du_ref[...]
        dz_ref[:, 0:SCW] = dab_ref[...]
        dz_ref[:, SCW:2 * SCW] = (duv * _f32(ax_ref)).astype(BF16)
        dz_ref[:, 2 * SCW:3 * SCW] = (duv * _f32(ac_ref)).astype(BF16)
        dz_ref[:, 3 * SCW:4 * SCW] = dag_ref[...]
        dz_ref[:, 4 * SCW:5 * SCW] = dbg_ref[...]
        dz_ref[:, 5 * SCW:5 * SCW + QL] = dcq_ref[...]
        dz_ref[:, 5 * SCW + QL:5 * SCW + QL + KVL] = dckv_ref[...]
        dz_ref[:, 5 * SCW + QL + KVL:ZE] = dkr_ref[...]

    return pl.pallas_call(
        body, name=name, grid=(S // T,),
        out_shape=jax.ShapeDtypeStruct((S, ZE), BF16),
        in_specs=[_rows(T, SCW), _rows(T, SCW), _rows(T, SCW, 1), _rows(T, SCW, 2), _rows(T, SCW), _rows(T, SCW),
                  _rows(T, QL), _rows(T, KVL), _rows(T, HEAD_PAD)],
        out_specs=_rows(T, ZE), compiler_params=_cparams(),
    )(dab, du, z, z, dag, dbg, dcq, dckv, dkr)


def _odd_pre(z, name):
    S, D = z.shape[0], z.shape[1] // 3
    T = _row_tile(S)

    def body(val_ref, glu_ref, u_ref):
        u_ref[...] = _f32(val_ref) * _sigmoid(_f32(glu_ref))

    return pl.pallas_call(
        body, name=name, grid=(S // T,),
        out_shape=jax.ShapeDtypeStruct((S, D), F32),
        in_specs=[_rows(T, D, 0), _rows(T, D, 1)], out_specs=_rows(T, D),
        compiler_params=_cparams(),
    )(z, z)


def _layer_norm_stats(cv):
    mu = jnp.mean(cv, axis=-1, keepdims=True)
    cen = cv - mu
    rstd = lax.rsqrt(jnp.mean(cen * cen, axis=-1, keepdims=True) + EPS)
    return cen * rstd, rstd


def _odd_post(cv, z, ln_g, ln_b, name):
    S, D = cv.shape
    T = _row_tile(S)

    def body(cv_ref, sg_ref, g_ref, b_ref, y_ref):
        cvh, _ = _layer_norm_stats(cv_ref[...])
        y_ref[...] = (_silu(cvh * g_ref[...] + b_ref[...]) * _silu(_f32(sg_ref))).astype(BF16)

    return pl.pallas_call(
        body, name=name, grid=(S // T,),
        out_shape=jax.ShapeDtypeStruct((S, D), BF16),
        in_specs=[_rows(T, D), _rows(T, D, 2), _const((1, D)), _const((1, D))],
        out_specs=_rows(T, D), compiler_params=_cparams(),
    )(cv, z, ln_g, ln_b)


def _odd_bwd_norm(dyi, cv, z, ln_g, ln_b, name):
    S, D = cv.shape
    T = _row_tile(S)

    def fn(ins, outs):
        dy_ref, cv_ref, sg_ref, g_ref, b_ref = ins
        dcv_ref, dsg_ref = outs
        cvh, rstd = _layer_norm_stats(cv_ref[...])
        ln = cvh * g_ref[...] + b_ref[...]
        sgv, dy = _f32(sg_ref), dy_ref[...]
        dsg_ref[...] = (dy * _silu(ln) * _dsilu(sgv)).astype(BF16)
        dln = dy * _silu(sgv) * _dsilu(ln)
        dh = dln * g_ref[...]
        dcv_ref[...] = rstd * (dh - jnp.mean(dh, axis=-1, keepdims=True)
                               - cvh * jnp.mean(dh * cvh, axis=-1, keepdims=True))
        return [dln * cvh, dln]

    return _col_sums(2, fn, [dyi, cv, z, ln_g, ln_b],
                     [_rows(T, D), _rows(T, D), _rows(T, D, 2), _const((1, D)), _const((1, D))],
                     [jax.ShapeDtypeStruct((S, D), F32), jax.ShapeDtypeStruct((S, D), BF16)],
                     [_rows(T, D), _rows(T, D)], S, T, [D, D], name)


def _odd_dz(du, z, dsg, name):
    S, D = du.shape
    T = _row_tile(S)

    def body(du_ref, val_ref, glu_ref, dsg_ref, dz_ref):
        duv = du_ref[...]
        sig = _sigmoid(_f32(glu_ref))
        dz_ref[:, 0:D] = (duv * sig).astype(BF16)
        dz_ref[:, D:2 * D] = (duv * _f32(val_ref) * sig * (1.0 - sig)).astype(BF16)
        dz_ref[:, 2 * D:3 * D] = dsg_ref[...]

    return pl.pallas_call(
        body, name=name, grid=(S // T,),
        out_shape=jax.ShapeDtypeStruct((S, 3 * D), BF16),
        in_specs=[_rows(T, D), _rows(T, D, 0), _rows(T, D, 1), _rows(T, D)],
        out_specs=_rows(T, 3 * D), compiler_params=_cparams(),
    )(du, z, z, dsg)


ADAM_BLOCK_ELEMS = 128 * 1024


def _adam_tiles(R, C):
    if R * C <= ADAM_BLOCK_ELEMS:
        return R, C
    tr = R
    for cand in range(SUBLANES, R, SUBLANES):
        if R % cand == 0 and cand * C <= ADAM_BLOCK_ELEMS:
            tr = cand
    if tr < R:
        return tr, C
    tc = C
    for cand in range(LANES, C, LANES):
        if C % cand == 0 and R * cand <= ADAM_BLOCK_ELEMS:
            tc = cand
    return R, tc


def _adamw(g_parts, w, m, v, name):
    if not isinstance(g_parts, (list, tuple)):
        g_parts = [g_parts]
    ng = len(g_parts)
    _, R, C = g_parts[0].shape
    tr, tc = _adam_tiles(R, C)

    def body(*refs):
        g_refs = refs[:ng]
        w_ref, m_ref, v_ref, go_ref, d_ref, mo_ref, vo_ref = refs[ng:]
        g = None
        for g_ref in g_refs:
            for p in range(g_ref.shape[0]):
                part = g_ref[p].astype(F32)
                g = part if g is None else g + part
        mn = ADAM_B1 * m_ref[...] + (1.0 - ADAM_B1) * g
        vn = ADAM_B2 * v_ref[...] + (1.0 - ADAM_B2) * (g * g)
        m_hat = mn / (1.0 - ADAM_B1 ** ADAM_STEP)
        v_hat = vn / (1.0 - ADAM_B2 ** ADAM_STEP)
        go_ref[...] = g
        d_ref[...] = -ADAM_LR * (m_hat / (jnp.sqrt(v_hat) + ADAM_EPS) + ADAM_WD * w_ref[...])
        mo_ref[...] = mn
        vo_ref[...] = vn

    slab = jax.ShapeDtypeStruct((R, C), F32)
    blk = pl.BlockSpec((tr, tc), lambda i, j: (i, j))
    return pl.pallas_call(
        body, name=name, grid=(R // tr, C // tc),
        out_shape=(slab,) * 4,
        in_specs=[pl.BlockSpec((g.shape[0], tr, tc), lambda i, j: (0, i, j)) for g in g_parts] + [blk, blk, blk],
        out_specs=(blk,) * 4, compiler_params=_cparams(),
    )(*g_parts, w, m, v)


def _gather_cols(g, shape):
    nd = len(shape)
    t = jnp.moveaxis(g, 0, nd - 1)
    return t.reshape(tuple(shape[:-1]) + (N_DEV * shape[-1],))


def _scatter_cols(full, n):
    t = full.reshape(full.shape[:-1] + (N_DEV, n))
    return jnp.moveaxis(t, -2, 0)


def kernel(x, c, positions, ada_w, ada_b, pre_norm_g, post_norm_g, even_w_in, even_sc_conv_w, even_sc_conv_b, even_q_norm_g, even_kv_norm_g, even_w_uq, even_w_ukv, even_w_out, odd_w_in, odd_conv_w, odd_conv_b, odd_ln_g, odd_ln_b, odd_w_out, loss_target, m_ada_w, m_ada_b, m_pre_norm_g, m_post_norm_g, m_even_w_in, m_even_sc_conv_w, m_even_sc_conv_b, m_even_q_norm_g, m_even_kv_norm_g, m_even_w_uq, m_even_w_ukv, m_even_w_out, m_odd_w_in, m_odd_conv_w, m_odd_conv_b, m_odd_ln_g, m_odd_ln_b, m_odd_w_out, v_ada_w, v_ada_b, v_pre_norm_g, v_post_norm_g, v_even_w_in, v_even_sc_conv_w, v_even_sc_conv_b, v_even_q_norm_g, v_even_kv_norm_g, v_even_w_uq, v_even_w_ukv, v_even_w_out, v_odd_w_in, v_odd_conv_w, v_odd_conv_b, v_odd_ln_g, v_odd_ln_b, v_odd_w_out):
    S, D = x.shape[1], x.shape[2]
    L = ada_w.shape[0]
    NE, NO = even_w_in.shape[0], odd_w_in.shape[0]
    me = 4 * lax.axis_index("x") + 2 * lax.axis_index("y") + lax.axis_index("c")
    x0 = x[0]
    target = loss_target[0]

    small_parts = [c, even_sc_conv_w, odd_conv_w, odd_conv_b, odd_ln_g, odd_ln_b]
    small_shapes = [p.shape for p in small_parts]
    sg = _exchange([_pack(small_parts, F32, SUBLANES)], False, "gather_small")[0].reshape(N_DEV, -1)
    c_all, scw_g, ocw_g, ocb_g, olg_g, olb_g = _unpack(sg, small_shapes)
    c_all = c_all.reshape(N_DEV, D)
    sc_conv_w = _gather_cols(scw_g, even_sc_conv_w.shape)
    o_conv_w = _gather_cols(ocw_g, odd_conv_w.shape)
    o_conv_b = _gather_cols(ocb_g, odd_conv_b.shape)
    o_ln_g = _gather_cols(olg_g, odd_ln_g.shape)
    o_ln_b = _gather_cols(olb_g, odd_ln_b.shape)

    pad_q = HEAD_PAD - QK_NOPE - QK_ROPE
    w_local = [jnp.swapaxes(even_w_in, 1, 2).astype(BF16),
               jnp.pad(even_w_uq, ((0, 0), (0, 0), (0, pad_q))).astype(BF16),
               jnp.pad(even_w_ukv[..., :QK_NOPE], ((0, 0), (0, 0), (0, HEAD_PAD - QK_NOPE))).astype(BF16),
               even_w_ukv[..., QK_NOPE:].astype(BF16),
               even_w_out.astype(BF16), odd_w_in.astype(BF16), odd_w_out.astype(BF16)]
    n_ada = ada_w.shape[2]
    ada_b_cols = lax.dynamic_slice_in_dim(ada_b, me * n_ada, n_ada, axis=1).reshape(L, 1, n_ada)
    mod_slab = _ada_fwd(c_all, ada_w, ada_b_cols)
    mod_g = _exchange([_pack([mod_slab], F32, SUBLANES)], False, "gather_mod")[0].reshape(N_DEV, -1)
    mod_all = mod_g[:, :L * N_DEV * n_ada].reshape(N_DEV, L, N_DEV, n_ada)
    mod = lax.dynamic_index_in_dim(mod_all, me, axis=2, keepdims=False)
    mod = jnp.moveaxis(mod, 0, 1).reshape(L, 3 * D)
    shift, scale, gate = mod[:, :D], mod[:, D:2 * D], mod[:, 2 * D:]

    heads_to_cols = lambda g: jnp.moveaxis(g, 0, 1).reshape(g.shape[1], -1)
    w_handles = {}
    token = jnp.broadcast_to(jnp.minimum(jnp.abs(mod[0, 0]), 0.0), (SUBLANES, LANES))
    for layer in range(L):
        i = layer // 2
        groups = ({"in": [w_local[0][i]], "rest": [w[i] for w in w_local[1:5]]} if layer % 2 == 0
                  else {"all": [w[i] for w in w_local[5:]]})
        for key, mine in groups.items():
            mine = [w + token[0, 0].astype(BF16) for w in mine]
            w_handles[layer, key], token = _exchange_start(mine, False, f"gather_weights_start_l{layer}_{key}")
    w_token = token

    def arrived(layer, key, after):
        return _exchange_wait(w_handles[layer, key], False, after, f"gather_weights_wait_l{layer}_{key}")[1]

    e_w_in_k, e_w_q_k, e_w_kv_k, e_w_out, o_w_in, o_w_out = ([None] * NE, [None] * NE, [None] * NE, [None] * NE,
                                                             [None] * NO, [None] * NO)

    half = QK_ROPE // 2
    inv_freq = 1.0 / (ROPE_THETA ** (jnp.arange(0, QK_ROPE, 2, dtype=F32) / QK_ROPE))
    inv_lane = jnp.zeros((HEAD_PAD,), F32).at[QK_NOPE:QK_NOPE + QK_ROPE].set(jnp.concatenate([inv_freq, inv_freq]))
    tabs = _rope_tables(positions.astype(F32).reshape(S, 1), inv_lane.reshape(1, HEAD_PAD))
    del half

    row = lambda a: a.reshape(1, -1)
    scb = even_sc_conv_b
    KP3, KP31 = SUBLANES, 32

    saved = []
    xs = x0
    h = _pre_norm(xs, row(pre_norm_g[0]) + w_token[0, 0], row(scale[0]), row(shift[0]), "pre_norm_l0")
    for layer in range(L):
        i = layer // 2
        tag = f"l{layer}"
        first = [h, tabs[0]] if layer == 0 else h
        if layer % 2 == 0:
            wt = arrived(layer, "in", first)[0].reshape(-1, D)
            e_w_in_k[i] = jnp.concatenate([wt[:2048], wt[2464:2976], wt[2048:2432], jnp.zeros((QK_NOPE, D), BF16),
                                           wt[2432:2464], jnp.zeros((pad_q, D), BF16)], axis=0)
            z = _matmul(h, e_w_in_k[i], "nt", BF16, f"w_in_{tag}", tn=1024)
            eq_g, ek_g, ev_g, eout_g = arrived(layer, "rest", z)
            e_w_q_k[i] = heads_to_cols(eq_g)
            e_w_kv_k[i] = jnp.concatenate([heads_to_cols(ek_g), heads_to_cols(ev_g)], axis=-1)
            e_w_out[i] = eout_g.reshape(-1, D)
            u, qn, kvn = _even_pre(z, row(even_q_norm_g[i]), row(even_kv_norm_g[i]), f"even_pre_{tag}")
            cw = jnp.pad(sc_conv_w[i], ((0, KP3 - SC_KERNEL), (0, 0)))
            cv = _conv_fwd(u, cw, row(scb[i]), SC_KERNEL, f"conv_{tag}")
            q, k, v, kT3, vT3 = _qkv_fwd_t(qn, kvn, z, tabs, e_w_q_k[i], e_w_kv_k[i], f"qkv_{tag}")
            o, lse = _attn_fwd_t(q, k, vT3, f"attn_{tag}")
            ycat = _even_post(z, cv, o, f"even_post_{tag}")
            y = _matmul(ycat, e_w_out[i], "nn", F32, f"w_out_{tag}", tn=1024)
            saved.append(dict(x=xs, h=h, z=z, u=u, qn=qn, kvn=kvn, cw=cw, cv=cv, q=q, k=k, v=v, kT3=kT3, o=o, lse=lse,
                              ycat=ycat, y=y))
        else:
            owin_g, oout_g = arrived(layer, "all", first)
            o_w_in[i], o_w_out[i] = heads_to_cols(owin_g), oout_g.reshape(-1, D)
            z = _matmul(h, o_w_in[i], "nn", BF16, f"w_in_{tag}", tn=1024)
            u = _odd_pre(z, f"odd_pre_{tag}")
            cw = jnp.pad(o_conv_w[i], ((0, KP31 - CONF_KERNEL), (0, 0)))
            cv = _conv_fwd(u, cw, row(o_conv_b[i]), CONF_KERNEL, f"conv_{tag}")
            yin = _odd_post(cv, z, row(o_ln_g[i]), row(o_ln_b[i]), f"odd_post_{tag}")
            y = _matmul(yin, o_w_out[i], "nn", F32, f"w_out_{tag}", tn=1024)
            saved.append(dict(x=xs, h=h, z=z, u=u, cw=cw, cv=cv, yin=yin, y=y))
        if layer < L - 1:
            xs, h = _post_pre_norm(xs, y, row(post_norm_g[layer]), row(gate[layer]), row(pre_norm_g[layer + 1]),
                                   row(scale[layer + 1]), row(shift[layer + 1]), f"post_pre_norm_{tag}")

    dx, dy, err_sq, dgate, g_post_last = _loss_post_norm_bwd(xs, y, row(post_norm_g[L - 1]), row(gate[L - 1]), target,
                                                             "loss_post_norm_bwd")
    loss = lax.psum(_scaled_total(err_sq, 0.5 / D, "loss_total")[0, 0], MESH_AXES)

    g_pre, g_post, dmod = [None] * L, [None] * L, [None] * L
    g_e_w_in, g_e_w_uq, g_e_w_ukv, g_e_w_out = [None] * NE, [None] * NE, [None] * NE, [None] * NE
    g_scw, g_scb, g_qg, g_kvg = [None] * NE, [None] * NE, [None] * NE, [None] * NE
    g_o_w_in, g_o_w_out, g_ocw, g_ocb, g_olg, g_olb = ([None] * NO for _ in range(6))
    sm_w = [even_sc_conv_w, odd_conv_w, odd_conv_b, odd_ln_g, odd_ln_b]
    sm_rows = _pack(sm_w, F32, SUBLANES).shape[0]

    def small_slab():
        full = [_scatter_cols(jnp.stack(g_scw), even_sc_conv_w.shape[-1]),
                _scatter_cols(jnp.stack(g_ocw), odd_conv_w.shape[-1]),
                _scatter_cols(jnp.concatenate(g_ocb, 0), odd_conv_b.shape[-1]),
                _scatter_cols(jnp.concatenate(g_olg, 0), odd_ln_g.shape[-1]),
                _scatter_cols(jnp.concatenate(g_olb, 0), odd_ln_b.shape[-1])]
        flat = jnp.concatenate([g.reshape(N_DEV, -1) for g in full], axis=1)
        return jnp.pad(flat, ((0, 0), (0, sm_rows * PACK_COLS - flat.shape[1]))).reshape(N_DEV, sm_rows, PACK_COLS)

    s_handles, own_parts = [None] * L, [None] * L
    bw_token = jnp.zeros((SUBLANES, LANES), F32)

    def start_scatter(layer, parts):
        if layer == 0:
            parts = parts + [small_slab()]
        own_parts[layer] = [lax.dynamic_slice_in_dim(g, me, 1, axis=0) for g in parts]
        s_handles[layer], token = _exchange_start([g.astype(BF16) for g in parts], True,
                                                  f"scatter_grads_start_l{layer}")
        return token

    for layer in reversed(range(L)):
        i = layer // 2
        tag = f"l{layer}"
        sv = saved[layer]
        if layer == L - 1:
            g_post[layer] = g_post_last
        if layer % 2 == 0:
            dyc = _matmul(dy, e_w_out[i], "nt", F32, f"d_ycat_{tag}", tn=1024)
            g_e_w_out[i] = _matmul(sv["ycat"], dy, "tn", F32, f"g_w_out_{tag}", tn=1024)
            dab, dag, dbg, dcv, do, delta = _even_bwd_gates(dyc, sv["z"], sv["cv"], sv["o"], f"even_gates_bwd_{tag}")
            du, dcw, g_scb[i] = _conv_bwd(dcv, sv["u"], sv["cw"], SC_KERNEL, f"conv_bwd_{tag}")
            g_scw[i] = dcw[:SC_KERNEL]
            dq, dk, dv = _attn_bwd_t(sv["q"], sv["k"], sv["v"], sv["kT3"], do, sv["lse"], delta, f"attn_bwd_{tag}")
            (dqp, dkvp, dcq, dckv, dkr, g_qg[i], g_kvg[i]) = _qkv_bwd(
                dq, dk, dv, sv["z"], tabs, e_w_q_k[i], e_w_kv_k[i],
                row(even_q_norm_g[i]), row(even_kv_norm_g[i]), f"qkv_bwd_{tag}")
            gq = _matmul(sv["qn"], dqp, "tn", F32, f"g_w_uq_{tag}", tn=1024)
            gkv = _matmul(sv["kvn"], dkvp, "tn", F32, f"g_w_ukv_{tag}")
            g_e_w_uq[i] = jnp.moveaxis(gq.reshape(QL, HEADS, HEAD_PAD)[..., :QK_NOPE + QK_ROPE], 1, 0)
            g_e_w_ukv[i] = jnp.moveaxis(jnp.concatenate(
                [gkv[:, :HEADS * HEAD_PAD].reshape(KVL, HEADS, HEAD_PAD)[..., :QK_NOPE],
                 gkv[:, HEADS * HEAD_PAD:].reshape(KVL, HEADS, V_HEAD)], axis=-1), 1, 0)
            g_e_w_out[i] = g_e_w_out[i].reshape(N_DEV, -1, D)
            dz = _even_dz(dab, du, sv["z"], dag, dbg, dcq, dckv, dkr, f"even_dz_{tag}")
            gt = _matmul(dz, sv["h"], "tn", F32, f"g_w_in_{tag}", tn=1024)
            g_e_w_in[i] = jnp.concatenate([gt[:2048], gt[2560:2944], gt[2944 + QK_NOPE:2944 + QK_NOPE + QK_ROPE],
                                           gt[2048:2560]], axis=0).reshape(N_DEV, -1, D)
            bw_token = start_scatter(layer, [g_e_w_in[i], g_e_w_uq[i], g_e_w_ukv[i], g_e_w_out[i]])
            dh = _matmul(dz, e_w_in_k[i], "nn", F32, f"d_h_{tag}", tn=1024)
        else:
            dyi = _matmul(dy, o_w_out[i], "nt", F32, f"d_yin_{tag}", tn=1024)
            g_o_w_out[i] = _matmul(sv["yin"], dy, "tn", F32, f"g_w_out_{tag}", tn=1024).reshape(N_DEV, -1, D)
            dcv, dsg, g_olg[i], g_olb[i] = _odd_bwd_norm(dyi, sv["cv"], sv["z"], row(o_ln_g[i]), row(o_ln_b[i]),
                                                         f"odd_norm_bwd_{tag}")
            du, dcw, g_ocb[i] = _conv_bwd(dcv, sv["u"], sv["cw"], CONF_KERNEL, f"conv_bwd_{tag}")
            g_ocw[i] = dcw[:CONF_KERNEL]
            dz = _odd_dz(du, sv["z"], dsg, f"odd_dz_{tag}")
            g_o_w_in[i] = jnp.moveaxis(_matmul(sv["h"], dz, "tn", F32, f"g_w_in_{tag}", tn=1024)
                                       .reshape(D, N_DEV, -1), 1, 0)
            bw_token = start_scatter(layer, [g_o_w_in[i], g_o_w_out[i]])
            dh = _matmul(dz, o_w_in[i], "nt", F32, f"d_h_{tag}", tn=1024)
        g_row = row(pre_norm_g[layer]) + bw_token[0, 0]
        if layer > 0:
            (dx, dy, dshift, dscale, g_pre[layer], dgate_prev, g_post[layer - 1]) = _pre_post_norm_bwd(
                dh, sv["x"], dx, g_row, row(scale[layer]), saved[layer - 1]["y"], row(post_norm_g[layer - 1]),
                row(gate[layer - 1]), f"pre_post_norm_bwd_{tag}")
        else:
            dx, dshift, dscale, g_pre[layer] = _pre_norm_bwd(dh, sv["x"], dx, g_row, row(scale[layer]),
                                                             f"pre_norm_bwd_{tag}")
            dgate_prev = None
        dmod[layer] = jnp.concatenate([dshift, dscale, dgate], axis=-1)
        dgate = dgate_prev
    grad_x = dx.reshape(1, S, D)

    rep_g = [jnp.concatenate(dmod, 0), jnp.concatenate(g_pre, 0), jnp.concatenate(g_post, 0),
             jnp.stack(g_scb), jnp.stack(g_qg), jnp.stack(g_kvg)]
    rep_w = [ada_b, pre_norm_g, post_norm_g, even_sc_conv_b, even_q_norm_g, even_kv_norm_g]
    rep_m = [m_ada_b, m_pre_norm_g, m_post_norm_g, m_even_sc_conv_b, m_even_q_norm_g, m_even_kv_norm_g]
    rep_v = [v_ada_b, v_pre_norm_g, v_post_norm_g, v_even_sc_conv_b, v_even_q_norm_g, v_even_kv_norm_g]
    rep_shapes = [w.shape for w in rep_w]
    rep_all = _exchange([_pack(rep_g, F32, SUBLANES)], False, "gather_small_grads")[0]
    rep_out = _adamw(rep_all, _pack(rep_w, F32, SUBLANES), _pack(rep_m, F32, SUBLANES), _pack(rep_v, F32, SUBLANES),
                     "adamw_replicated")
    rep_res = [_unpack(o.reshape(-1), rep_shapes) for o in rep_out]

    dmod_all = rep_all.reshape(N_DEV, -1)[:, :L * 3 * D].reshape(N_DEV, L, 3 * D)
    dmod_cols = jnp.moveaxis(lax.dynamic_slice_in_dim(dmod_all, me * n_ada, n_ada, axis=2), 0, 1)
    g_ada_w = _ada_bwd(c_all.T, dmod_cols)
    ada_out = _adamw(g_ada_w.reshape(1, -1, PACK_COLS), ada_w.reshape(-1, PACK_COLS),
                     m_ada_w.reshape(-1, PACK_COLS), v_ada_w.reshape(-1, PACK_COLS), "adamw_ada_w")
    ada_res = [o.reshape(ada_w.shape) for o in ada_out]

    sm_m = [m_even_sc_conv_w, m_odd_conv_w, m_odd_conv_b, m_odd_ln_g, m_odd_ln_b]
    sm_v = [v_even_sc_conv_w, v_odd_conv_w, v_odd_conv_b, v_odd_ln_g, v_odd_ln_b]
    sm_shapes = [w.shape for w in sm_w]
    state = {"even_w_in": (even_w_in, m_even_w_in, v_even_w_in), "even_w_uq": (even_w_uq, m_even_w_uq, v_even_w_uq),
             "even_w_ukv": (even_w_ukv, m_even_w_ukv, v_even_w_ukv), "even_w_out": (even_w_out, m_even_w_out, v_even_w_out),
             "odd_w_in": (odd_w_in, m_odd_w_in, v_odd_w_in), "odd_w_out": (odd_w_out, m_odd_w_out, v_odd_w_out)}
    big_res = {name: [[None] * len(state[name][0]) for _ in range(4)] for name in state}
    after = [bw_token, grad_x, rep_out[0], ada_out[0]]
    sm_res = None
    for layer in reversed(range(L)):
        i = layer // 2
        names = ["even_w_in", "even_w_uq", "even_w_ukv", "even_w_out"] if layer % 2 == 0 else ["odd_w_in", "odd_w_out"]
        _, landed = _exchange_wait(s_handles[layer], True, after, f"scatter_grads_wait_l{layer}")
        own = own_parts[layer]
        for a, name in enumerate(names):
            transposed = name == "even_w_in"
            wmv = [t[i].T if transposed else t[i] for t in state[name]]
            res = _adamw([own[a], landed[a]], *wmv, f"adamw_{name}_{i}")
            for kind in range(4):
                big_res[name][kind][i] = res[kind].T if transposed else res[kind]
            after = res[0]
        if layer == 0:
            sm_out = _adamw([own[-1], landed[-1]], _pack(sm_w, F32, SUBLANES), _pack(sm_m, F32, SUBLANES),
                            _pack(sm_v, F32, SUBLANES), "adamw_small_sharded")
            sm_res = [_unpack(o.reshape(-1), sm_shapes) for o in sm_out]
    sh_res = [dict(zip(["even_sc_conv_w", "odd_conv_w", "odd_conv_b", "odd_ln_g", "odd_ln_b"], sm_res[kind]))
              for kind in range(4)]
    for name in state:
        for kind in range(4):
            sh_res[kind][name] = jnp.stack(big_res[name][kind])

    order = ["ada_w", "ada_b", "pre_norm_g", "post_norm_g", "even_w_in", "even_sc_conv_w", "even_sc_conv_b",
             "even_q_norm_g", "even_kv_norm_g", "even_w_uq", "even_w_ukv", "even_w_out", "odd_w_in", "odd_conv_w",
             "odd_conv_b", "odd_ln_g", "odd_ln_b", "odd_w_out"]
    rep_names = ["ada_b", "pre_norm_g", "post_norm_g", "even_sc_conv_b", "even_q_norm_g", "even_kv_norm_g"]
    outs = [loss, grad_x]
    for kind in range(4):
        for name in order:
            if name == "ada_w":
                outs.append(ada_res[kind])
            elif name in rep_names:
                outs.append(rep_res[kind][rep_names.index(name)])
            else:
                outs.append(sh_res[kind][name])
    return tuple(outs)
```

```python
import functools
import math

import jax
import jax.numpy as jnp
from jax import lax
from jax.experimental import pallas as pl
from jax.experimental.pallas import tpu as pltpu

F32 = jnp.float32
BF16 = jnp.bfloat16
MESH_AXES = ("x", "y", "c")
N_DEV = 8
EPS = 1e-6
CHUNK = 64
HEADS = 8
QK_NOPE = 64
QK_ROPE = 32
V_HEAD = 64
HEAD_PAD = 128
ROPE_THETA = 10000.0
SC_KERNEL = 3
CONF_KERNEL = 31
LANES = 128
SUBLANES = 8
PACK_COLS = 1024
VMEM_LIMIT = 48 * 1024 * 1024
NEG = -1e30

ADAM_LR = 0.001
ADAM_B1 = 0.9
ADAM_B2 = 0.999
ADAM_EPS = 1e-08
ADAM_WD = 0.01
ADAM_STEP = 10


def _cparams():
    return pltpu.CompilerParams(vmem_limit_bytes=VMEM_LIMIT)


def _sigmoid(x):
    return 1.0 / (1.0 + jnp.exp(-x))


def _f32(ref):
    return ref[...].astype(F32)


def _silu(x):
    return x * _sigmoid(x)


def _dsilu(x):
    s = _sigmoid(x)
    return s * (1.0 + x * (1.0 - s))


def _rows(T, width, cb=0):
    return pl.BlockSpec((T, width), lambda i: (i, cb))


def _const(shape):
    nd = len(shape)
    return pl.BlockSpec(shape, lambda i: (0,) * nd)


def _row_tile(S):
    return min(256, S)


def _exchange(srcs, scatter, name):
    n = len(srcs)
    shapes = [tuple(s.shape[1:]) if scatter else tuple(s.shape) for s in srcs]

    def body(*refs):
        src_refs, out_refs = refs[:n], refs[n:2 * n]
        send_sems, recv_sems, local_sems = refs[2 * n:]
        x, y, c = lax.axis_index("x"), lax.axis_index("y"), lax.axis_index("c")
        me = 4 * x + 2 * y + c
        owns, copies = [], []
        for a in range(n):
            def piece(d, a=a):
                return src_refs[a].at[d] if scatter else src_refs[a]

            own = pltpu.make_async_copy(piece(me), out_refs[a].at[me], local_sems.at[a])
            own.start()
            owns.append(own)
            for k in range(1, N_DEV):
                px, py, pc = x ^ ((k >> 2) & 1), y ^ ((k >> 1) & 1), c ^ (k & 1)
                peer = 4 * px + 2 * py + pc
                sem = a * (N_DEV - 1) + k - 1
                cp = pltpu.make_async_remote_copy(
                    src_ref=piece(peer), dst_ref=out_refs[a].at[me],
                    send_sem=send_sems.at[sem], recv_sem=recv_sems.at[sem],
                    device_id=(px, py, pc), device_id_type=pl.DeviceIdType.MESH)
                cp.start()
                arrival = pltpu.make_async_remote_copy(
                    src_ref=piece(peer), dst_ref=out_refs[a].at[peer],
                    send_sem=send_sems.at[sem], recv_sem=recv_sems.at[sem],
                    device_id=(x, y, c), device_id_type=pl.DeviceIdType.MESH)
                copies.append((cp, arrival))
        for _, arrival in copies:
            arrival.wait_recv()
        for cp, _ in copies:
            cp.wait_send()
        for own in owns:
            own.wait()

    return pl.pallas_call(
        body, name=name,
        out_shape=tuple(jax.ShapeDtypeStruct((N_DEV,) + shp, s.dtype) for shp, s in zip(shapes, srcs)),
        in_specs=[pl.BlockSpec(memory_space=pl.ANY)] * n,
        out_specs=tuple(pl.BlockSpec(memory_space=pl.ANY) for _ in range(n)),
        scratch_shapes=[pltpu.SemaphoreType.DMA((n * (N_DEV - 1),)),
                        pltpu.SemaphoreType.DMA((n * (N_DEV - 1),)),
                        pltpu.SemaphoreType.DMA((n,))],
    )(*srcs)


_HBM = pl.BlockSpec(memory_space=pltpu.HBM)
_SEM = pl.BlockSpec(memory_space=pltpu.SEMAPHORE)


def _peer(k):
    x, y, c = lax.axis_index("x"), lax.axis_index("y"), lax.axis_index("c")
    return x ^ ((k >> 2) & 1), y ^ ((k >> 1) & 1), c ^ (k & 1)


def _exchange_start(srcs, scatter, name):
    n = len(srcs)
    shapes = [tuple(s.shape[1:]) if scatter else tuple(s.shape) for s in srcs]
    slots = N_DEV - 1 if scatter else N_DEV
    lands = [lax.empty((slots,) + shp, s.dtype) for shp, s in zip(shapes, srcs)]
    if not scatter:
        here = 4 * lax.axis_index("x") + 2 * lax.axis_index("y") + lax.axis_index("c")
        lands = [lax.dynamic_update_index_in_dim(l, s, here, 0) for l, s in zip(lands, srcs)]

    def body(*refs):
        src_refs, land_refs = refs[:n], refs[n:2 * n]
        send_sems, recv_sems = refs[2 * n], refs[2 * n + 1]
        token = refs[4 * n + 2]
        me = 4 * lax.axis_index("x") + 2 * lax.axis_index("y") + lax.axis_index("c")
        for a in range(n):
            for k in range(1, N_DEV):
                px, py, pc = _peer(k)
                peer = 4 * px + 2 * py + pc
                pltpu.make_async_remote_copy(
                    src_ref=src_refs[a].at[peer] if scatter else src_refs[a],
                    dst_ref=land_refs[a].at[k - 1] if scatter else land_refs[a].at[me],
                    send_sem=send_sems.at[a * (N_DEV - 1) + k - 1], recv_sem=recv_sems.at[a * (N_DEV - 1) + k - 1],
                    device_id=(px, py, pc), device_id_type=pl.DeviceIdType.MESH).start()
        token[...] = jnp.zeros_like(token)

    hbm = lambda arrs: [pltpu.HBM(a.shape, a.dtype) for a in arrs]
    out = pl.pallas_call(
        body, name=name,
        out_shape=(pltpu.SemaphoreType.DMA((n * (N_DEV - 1),)), pltpu.SemaphoreType.DMA((n * (N_DEV - 1),)),
                   *hbm(srcs), *hbm(lands), jax.ShapeDtypeStruct((SUBLANES, LANES), F32)),
        in_specs=[_HBM] * (2 * n),
        out_specs=(_SEM, _SEM, *([_HBM] * (2 * n)), pl.BlockSpec(memory_space=pltpu.VMEM)),
        input_output_aliases={a: 2 + a for a in range(2 * n)},
        compiler_params=pltpu.CompilerParams(has_side_effects=pltpu.SideEffectType.DATAFLOW_SIDE_EFFECTING),
    )(*[pltpu.with_memory_space_constraint(s, pltpu.HBM) for s in srcs],
      *[pltpu.with_memory_space_constraint(l, pltpu.HBM) for l in lands])
    return (out[0], out[1], list(out[2:2 + n]), list(out[2 + n:2 + 2 * n])), out[2 + 2 * n]


def _exchange_wait(handle, scatter, after, name):
    send_sems, recv_sems, srcs, lands = handle
    n = len(srcs)
    after = list(after) if isinstance(after, (list, tuple)) else [after]

    def body(*refs):
        src_refs, land_refs = refs[:n], refs[n:2 * n]
        send_sems, recv_sems = refs[2 * n], refs[2 * n + 1]
        for a in range(n):
            for k in range(1, N_DEV):
                px, py, pc = _peer(k)
                peer = 4 * px + 2 * py + pc
                cp = pltpu.make_async_remote_copy(
                    src_ref=src_refs[a].at[peer] if scatter else src_refs[a],
                    dst_ref=land_refs[a].at[k - 1] if scatter else land_refs[a].at[peer],
                    send_sem=send_sems.at[a * (N_DEV - 1) + k - 1], recv_sem=recv_sems.at[a * (N_DEV - 1) + k - 1],
                    device_id=(px, py, pc), device_id_type=pl.DeviceIdType.MESH)
                cp.wait_send()
                cp.wait_recv()

    out = pl.pallas_call(
        body, name=name,
        out_shape=tuple(pltpu.HBM(a.shape, a.dtype) for a in srcs + lands),
        in_specs=[_HBM] * (2 * n) + [_SEM, _SEM] + [pl.BlockSpec(memory_space=pl.ANY)] * len(after),
        out_specs=tuple([_HBM] * (2 * n)),
        input_output_aliases={a: a for a in range(2 * n)},
        compiler_params=pltpu.CompilerParams(has_side_effects=pltpu.SideEffectType.DATAFLOW_SIDE_EFFECTING),
    )(*srcs, *lands, send_sems, recv_sems, *after)
    return list(out[:n]), list(out[n:])


def _pack(parts, dtype, row_mult):
    flat = jnp.concatenate([p.reshape(-1).astype(dtype) for p in parts])
    n = flat.shape[0]
    rows = -(-n // PACK_COLS)
    rows = -(-rows // row_mult) * row_mult
    flat = jnp.pad(flat, (0, rows * PACK_COLS - n))
    return flat.reshape(rows, PACK_COLS)


def _unpack(flat, shapes):
    out, off = [], 0
    for shp in shapes:
        n = math.prod(shp)
        out.append(flat[..., off:off + n].reshape(flat.shape[:-1] + tuple(shp)))
        off += n
    return out


_DIMS = {"nn": (((1,), (0,)), ((), ())), "nt": (((1,), (1,)), ((), ())), "tn": (((0,), (0,)), ((), ()))}


def _matmul(a, b, mode, out_dtype, name, tm=512, tn=512, tk=None):
    if mode == "nn":
        (M, K), (_, N) = a.shape, b.shape
    elif mode == "nt":
        (M, K), (N, _) = a.shape, b.shape
    else:
        (K, M), (_, N) = a.shape, b.shape
    tm, tn = min(tm, M), min(tn, N)
    tk = K if tk is None else min(tk, K)
    nk = K // tk
    assert M % tm == 0 and N % tn == 0 and K % tk == 0, (name, a.shape, b.shape)

    def body(a_ref, b_ref, o_ref, *scratch):
        p = lax.dot_general(a_ref[...].astype(BF16), b_ref[...].astype(BF16), _DIMS[mode],
                            preferred_element_type=F32)
        if nk == 1:
            o_ref[...] = p.astype(out_dtype)
        else:
            acc = scratch[0]
            k = pl.program_id(2)

            @pl.when(k == 0)
            def _():
                acc[...] = p

            @pl.when(k > 0)
            def _():
                acc[...] += p

            @pl.when(k == nk - 1)
            def _():
                o_ref[...] = acc[...].astype(out_dtype)

    a_spec = (pl.BlockSpec((tk, tm), lambda i, j, k: (k, i)) if mode == "tn"
              else pl.BlockSpec((tm, tk), lambda i, j, k: (i, k)))
    b_spec = (pl.BlockSpec((tn, tk), lambda i, j, k: (j, k)) if mode == "nt"
              else pl.BlockSpec((tk, tn), lambda i, j, k: (k, j)))
    return pl.pallas_call(
        body, name=name, grid=(M // tm, N // tn, nk),
        out_shape=jax.ShapeDtypeStruct((M, N), out_dtype),
        in_specs=[a_spec, b_spec],
        out_specs=pl.BlockSpec((tm, tn), lambda i, j, k: (i, j)),
        scratch_shapes=[pltpu.VMEM((tm, tn), F32)] if nk > 1 else [],
        compiler_params=_cparams(),
    )(a, b)


def _ada_fwd(c_all, ada_w, ada_b_cols):
    L, D, n = ada_w.shape

    def body(c_ref, w_ref, b_ref, o_ref):
        act = _silu(c_ref[...]).astype(BF16)
        o_ref[0] = jnp.dot(act, w_ref[0].astype(BF16), preferred_element_type=F32) + b_ref[0]

    return pl.pallas_call(
        body, name="ada_fwd", grid=(L,),
        out_shape=jax.ShapeDtypeStruct((L, N_DEV, n), F32),
        in_specs=[pl.BlockSpec((N_DEV, D), lambda l: (0, 0)),
                  pl.BlockSpec((1, D, n), lambda l: (l, 0, 0)),
                  pl.BlockSpec((1, 1, n), lambda l: (l, 0, 0))],
        out_specs=pl.BlockSpec((1, N_DEV, n), lambda l: (l, 0, 0)),
        compiler_params=_cparams(),
    )(c_all, ada_w, ada_b_cols)


def _ada_bwd(c_all_t, dmod_cols):
    D = c_all_t.shape[0]
    L, _, n = dmod_cols.shape

    def body(c_ref, d_ref, o_ref):
        act = _silu(c_ref[...])
        dm = d_ref[0]
        acc = act[:, 0:1] * dm[0:1, :]
        for b in range(1, N_DEV):
            acc = acc + act[:, b:b + 1] * dm[b:b + 1, :]
        o_ref[0] = acc

    return pl.pallas_call(
        body, name="ada_bwd", grid=(L,),
        out_shape=jax.ShapeDtypeStruct((L, D, n), F32),
        in_specs=[pl.BlockSpec((D, N_DEV), lambda l: (0, 0)),
                  pl.BlockSpec((1, N_DEV, n), lambda l: (l, 0, 0))],
        out_specs=pl.BlockSpec((1, D, n), lambda l: (l, 0, 0)),
        compiler_params=_cparams(),
    )(c_all_t, dmod_cols)


def _rope_tables(pos_col, inv_lane):
    S = pos_col.shape[0]
    T = _row_tile(S)
    half = QK_ROPE // 2

    def body(p_ref, f_ref, c_ref, up_ref, dn_ref):
        ang = p_ref[...] * f_ref[...]
        lane = lax.broadcasted_iota(jnp.int32, ang.shape, 1)
        first = (lane >= QK_NOPE) & (lane < QK_NOPE + half)
        second = (lane >= QK_NOPE + half) & (lane < QK_NOPE + QK_ROPE)
        cs, sn = jnp.cos(ang), jnp.sin(ang)
        c_ref[...] = jnp.where(first | second, cs, 1.0)
        up_ref[...] = jnp.where(first, -sn, 0.0)
        dn_ref[...] = jnp.where(second, sn, 0.0)

    tab = jax.ShapeDtypeStruct((S, HEAD_PAD), F32)
    return pl.pallas_call(
        body, name="rope_tables", grid=(S // T,),
        out_shape=(tab, tab, tab),
        in_specs=[_rows(T, 1), _const((1, HEAD_PAD))],
        out_specs=(_rows(T, HEAD_PAD),) * 3,
        compiler_params=_cparams(),
    )(pos_col, inv_lane)


def _rope(blk, ct, ut, dt):
    half = QK_ROPE // 2
    up = pltpu.roll(blk, HEAD_PAD - half, 1)
    dn = pltpu.roll(blk, half, 1)
    return blk * ct + up * ut + dn * dt


def _rope_t(d, ct, ut, dt):
    half = QK_ROPE // 2
    return d * ct + pltpu.roll(d * ut, half, 1) + pltpu.roll(d * dt, HEAD_PAD - half, 1)


def _pre_norm(x, g, scale, shift, name):
    S, D = x.shape
    T = _row_tile(S)

    def body(x_ref, g_ref, sc_ref, sh_ref, h_ref):
        xv = x_ref[...]
        rstd = lax.rsqrt(jnp.mean(xv * xv, axis=-1, keepdims=True) + EPS)
        h_ref[...] = ((xv * rstd) * g_ref[...] * (1.0 + sc_ref[...]) + sh_ref[...]).astype(BF16)

    return pl.pallas_call(
        body, name=name, grid=(S // T,),
        out_shape=jax.ShapeDtypeStruct((S, D), BF16),
        in_specs=[_rows(T, D), _const((1, D)), _const((1, D)), _const((1, D))],
        out_specs=_rows(T, D), compiler_params=_cparams(),
    )(x, g, scale, shift)


def _post_norm(x, y, g, gate, name):
    S, D = x.shape
    T = _row_tile(S)

    def body(x_ref, y_ref, g_ref, gt_ref, o_ref):
        yv = y_ref[...]
        rstd = lax.rsqrt(jnp.mean(yv * yv, axis=-1, keepdims=True) + EPS)
        o_ref[...] = x_ref[...] + gt_ref[...] * ((yv * rstd) * g_ref[...])

    return pl.pallas_call(
        body, name=name, grid=(S // T,),
        out_shape=jax.ShapeDtypeStruct((S, D), F32),
        in_specs=[_rows(T, D), _rows(T, D), _const((1, D)), _const((1, D))],
        out_specs=_rows(T, D), compiler_params=_cparams(),
    )(x, y, g, gate)


def _fold8(v):
    T, C = v.shape
    return v.reshape(T // SUBLANES, SUBLANES, C).sum(axis=0)


def _col_sums(n_sums, body_fn, ins, in_specs, outs, out_specs, S, T, widths, name):
    n_in, n_out = len(ins), len(outs)
    nt = S // T

    def body(*refs):
        in_refs = refs[:n_in]
        out_refs = refs[n_in:n_in + n_out]
        sum_refs = refs[n_in + n_out:n_in + n_out + n_sums]
        accs = refs[n_in + n_out + n_sums:]
        i = pl.program_id(0)
        terms = body_fn(in_refs, out_refs)

        @pl.when(i == 0)
        def _():
            for acc, t in zip(accs, terms):
                acc[...] = _fold8(t)

        @pl.when(i > 0)
        def _():
            for acc, t in zip(accs, terms):
                acc[...] += _fold8(t)

        @pl.when(i == nt - 1)
        def _():
            for acc, s_ref in zip(accs, sum_refs):
                s_ref[...] = jnp.sum(acc[...], axis=0, keepdims=True)

    return pl.pallas_call(
        body, name=name, grid=(nt,),
        out_shape=tuple(outs) + tuple(jax.ShapeDtypeStruct((1, w), F32) for w in widths),
        in_specs=in_specs,
        out_specs=tuple(out_specs) + tuple(_const((1, w)) for w in widths),
        scratch_shapes=[pltpu.VMEM((SUBLANES, w), F32) for w in widths],
        compiler_params=_cparams(),
    )(*ins)


def _post_norm_bwd(dxo, y, g, gate, name):
    S, D = y.shape
    T = _row_tile(S)

    def fn(ins, outs):
        dxo_ref, y_ref, g_ref, gt_ref = ins
        yv, dv = y_ref[...], dxo_ref[...]
        rstd = lax.rsqrt(jnp.mean(yv * yv, axis=-1, keepdims=True) + EPS)
        yh = yv * rstd
        dn = dv * gt_ref[...]
        dyh = dn * g_ref[...]
        outs[0][...] = (rstd * (dyh - yh * jnp.mean(dyh * yh, axis=-1, keepdims=True))).astype(BF16)
        return [dv * (yh * g_ref[...]), dn * yh]

    return _col_sums(2, fn, [dxo, y, g, gate],
                     [_rows(T, D), _rows(T, D), _const((1, D)), _const((1, D))],
                     [jax.ShapeDtypeStruct((S, D), BF16)], [_rows(T, D)], S, T, [D, D], name)


def _pre_norm_bwd(dh, x, dxo, g, scale, name):
    S, D = x.shape
    T = _row_tile(S)

    def fn(ins, outs):
        dh_ref, x_ref, dxo_ref, g_ref, sc_ref = ins
        xv, dv = x_ref[...], dh_ref[...]
        rstd = lax.rsqrt(jnp.mean(xv * xv, axis=-1, keepdims=True) + EPS)
        xh = xv * rstd
        dr = dv * (1.0 + sc_ref[...])
        dxh = dr * g_ref[...]
        outs[0][...] = dxo_ref[...] + rstd * (dxh - xh * jnp.mean(dxh * xh, axis=-1, keepdims=True))
        return [dv, dv * (xh * g_ref[...]), dr * xh]

    return _col_sums(3, fn, [dh, x, dxo, g, scale],
                     [_rows(T, D), _rows(T, D), _rows(T, D), _const((1, D)), _const((1, D))],
                     [jax.ShapeDtypeStruct((S, D), F32)], [_rows(T, D)], S, T, [D, D, D], name)


def _post_pre_norm(x, y, g_post, gate, g_pre, scale, shift, name):
    S, D = x.shape
    T = _row_tile(S)

    def body(x_ref, y_ref, gp_ref, gt_ref, g_ref, sc_ref, sh_ref, xn_ref, h_ref):
        yv = y_ref[...]
        rstd_y = lax.rsqrt(jnp.mean(yv * yv, axis=-1, keepdims=True) + EPS)
        xn = x_ref[...] + gt_ref[...] * ((yv * rstd_y) * gp_ref[...])
        xn_ref[...] = xn
        rstd = lax.rsqrt(jnp.mean(xn * xn, axis=-1, keepdims=True) + EPS)
        h_ref[...] = ((xn * rstd) * g_ref[...] * (1.0 + sc_ref[...]) + sh_ref[...]).astype(BF16)

    return pl.pallas_call(
        body, name=name, grid=(S // T,),
        out_shape=(jax.ShapeDtypeStruct((S, D), F32), jax.ShapeDtypeStruct((S, D), BF16)),
        in_specs=[_rows(T, D), _rows(T, D)] + [_const((1, D))] * 5,
        out_specs=(_rows(T, D), _rows(T, D)), compiler_params=_cparams(),
    )(x, y, g_post, gate, g_pre, scale, shift)


def _pre_post_norm_bwd(dh, x, dxo, g_pre, scale, y_prev, g_post_prev, gate_prev, name):
    S, D = x.shape
    T = _row_tile(S)

    def fn(ins, outs):
        dh_ref, x_ref, dxo_ref, g_ref, sc_ref, y_ref, gp_ref, gt_ref = ins
        xv, dv = x_ref[...], dh_ref[...]
        rstd = lax.rsqrt(jnp.mean(xv * xv, axis=-1, keepdims=True) + EPS)
        xh = xv * rstd
        dr = dv * (1.0 + sc_ref[...])
        dxh = dr * g_ref[...]
        dx = dxo_ref[...] + rstd * (dxh - xh * jnp.mean(dxh * xh, axis=-1, keepdims=True))
        outs[0][...] = dx
        yv = y_ref[...]
        rstd_y = lax.rsqrt(jnp.mean(yv * yv, axis=-1, keepdims=True) + EPS)
        yh = yv * rstd_y
        dn = dx * gt_ref[...]
        dyh = dn * gp_ref[...]
        outs[1][...] = (rstd_y * (dyh - yh * jnp.mean(dyh * yh, axis=-1, keepdims=True))).astype(BF16)
        return [dv, dv * (xh * g_ref[...]), dr * xh, dx * (yh * gp_ref[...]), dn * yh]

    return _col_sums(5, fn, [dh, x, dxo, g_pre, scale, y_prev, g_post_prev, gate_prev],
                     [_rows(T, D), _rows(T, D), _rows(T, D), _const((1, D)), _const((1, D)),
                      _rows(T, D), _const((1, D)), _const((1, D))],
                     [jax.ShapeDtypeStruct((S, D), F32), jax.ShapeDtypeStruct((S, D), BF16)],
                     [_rows(T, D), _rows(T, D)], S, T, [D] * 5, name)


def _loss_post_norm_bwd(x, y, g_post, gate, target, name):
    S, D = x.shape
    T = _row_tile(S)

    def fn(ins, outs):
        x_ref, y_ref, gp_ref, gt_ref, t_ref = ins
        yv = y_ref[...]
        rstd_y = lax.rsqrt(jnp.mean(yv * yv, axis=-1, keepdims=True) + EPS)
        yh = yv * rstd_y
        e = x_ref[...] + gt_ref[...] * (yh * gp_ref[...]) - t_ref[...]
        dx = e * (1.0 / D)
        outs[0][...] = dx
        dn = dx * gt_ref[...]
        dyh = dn * gp_ref[...]
        outs[1][...] = (rstd_y * (dyh - yh * jnp.mean(dyh * yh, axis=-1, keepdims=True))).astype(BF16)
        return [e * e, dx * (yh * gp_ref[...]), dn * yh]

    return _col_sums(3, fn, [x, y, g_post, gate, target],
                     [_rows(T, D), _rows(T, D), _const((1, D)), _const((1, D)), _rows(T, D)],
                     [jax.ShapeDtypeStruct((S, D), F32), jax.ShapeDtypeStruct((S, D), BF16)],
                     [_rows(T, D), _rows(T, D)], S, T, [D] * 3, name)


def _scaled_total(v, coef, name):
    def body(v_ref, o_ref):
        o_ref[...] = jnp.broadcast_to(jnp.sum(v_ref[...], axis=1, keepdims=True) * coef, (1, LANES))

    return pl.pallas_call(body, name=name, out_shape=jax.ShapeDtypeStruct((1, LANES), F32))(v)


def _loss_head(x, target):
    S, D = x.shape
    T = _row_tile(S)
    nt = S // T

    def body(x_ref, t_ref, l_ref, dx_ref, acc):
        i = pl.program_id(0)
        e = x_ref[...] - t_ref[...]
        dx_ref[...] = e * (1.0 / D)
        part = _fold8(e * e)

        @pl.when(i == 0)
        def _():
            acc[...] = part

        @pl.when(i > 0)
        def _():
            acc[...] += part

        @pl.when(i == nt - 1)
        def _():
            tot = jnp.sum(jnp.sum(acc[...], axis=0, keepdims=True), axis=1, keepdims=True)
            l_ref[...] = jnp.broadcast_to(tot * (0.5 / D), (1, LANES))

    return pl.pallas_call(
        body, name="loss_head", grid=(nt,),
        out_shape=(jax.ShapeDtypeStruct((1, LANES), F32), jax.ShapeDtypeStruct((S, D), F32)),
        in_specs=[_rows(T, D), _rows(T, D)],
        out_specs=(_const((1, LANES)), _rows(T, D)),
        scratch_shapes=[pltpu.VMEM((SUBLANES, D), F32)],
        compiler_params=_cparams(),
    )(x, target)


CONV_ROWS = 64


def _conv_halo(K):
    return SUBLANES if K - 1 <= SUBLANES else 32


def _conv_fwd(u, w, b, K, name):
    S, C = u.shape
    KP = w.shape[0]
    T, HB, RS = min(512, S), _conv_halo(K), CONV_ROWS
    ratio = T // HB

    def body(u_ref, h_ref, w_ref, b_ref, o_ref, ext):
        i = pl.program_id(1)
        ext[0:HB, :] = jnp.where(i > 0, h_ref[...], 0.0)
        ext[HB:HB + T, :] = u_ref[...]
        for r0 in range(0, T, RS):
            acc = jnp.broadcast_to(b_ref[...], (RS, LANES))
            for k in range(K):
                off = HB - (K - 1) + k + r0
                acc = acc + w_ref[k:k + 1, :] * ext[off:off + RS, :]
            o_ref[r0:r0 + RS, :] = acc

    return pl.pallas_call(
        body, name=name, grid=(C // LANES, S // T),
        out_shape=jax.ShapeDtypeStruct((S, C), F32),
        in_specs=[pl.BlockSpec((T, LANES), lambda c, i: (i, c)),
                  pl.BlockSpec((HB, LANES), lambda c, i: (jnp.maximum(i * ratio - 1, 0), c)),
                  pl.BlockSpec((KP, LANES), lambda c, i: (0, c)),
                  pl.BlockSpec((1, LANES), lambda c, i: (0, c))],
        out_specs=pl.BlockSpec((T, LANES), lambda c, i: (i, c)),
        scratch_shapes=[pltpu.VMEM((HB + T, LANES), F32)],
        compiler_params=_cparams(),
    )(u, u, w, b)


def _conv_bwd(d, u, w, K, name):
    S, C = u.shape
    KP = w.shape[0]
    T, HB, RS = min(512, S), _conv_halo(K), CONV_ROWS
    ratio = T // HB
    nt = S // T
    last_halo = S // HB - 1

    def body(d_ref, dn_ref, u_ref, up_ref, w_ref, du_ref, dw_ref, db_ref, extd, extu, dws, dbs):
        i = pl.program_id(1)
        extd[0:T, :] = d_ref[...]
        extd[T:T + HB, :] = jnp.where(i < nt - 1, dn_ref[...], 0.0)
        extu[0:HB, :] = jnp.where(i > 0, up_ref[...], 0.0)
        extu[HB:HB + T, :] = u_ref[...]

        @pl.when(i == 0)
        def _():
            dws[...] = jnp.zeros_like(dws)
            dbs[...] = jnp.zeros_like(dbs)

        for r0 in range(0, T, RS):
            acc = jnp.zeros((RS, LANES), F32)
            for k in range(K):
                off = (K - 1 - k) + r0
                acc = acc + w_ref[k:k + 1, :] * extd[off:off + RS, :]
            du_ref[r0:r0 + RS, :] = acc
            dch = d_ref[r0:r0 + RS, :]
            dbs[...] += _fold8(dch)
            for k in range(K):
                off = HB - (K - 1) + k + r0
                dws[k * SUBLANES:(k + 1) * SUBLANES, :] += _fold8(dch * extu[off:off + RS, :])

        @pl.when(i == nt - 1)
        def _():
            dw_ref[...] = jnp.zeros_like(dw_ref)
            for k in range(K):
                dw_ref[k:k + 1, :] = jnp.sum(dws[k * SUBLANES:(k + 1) * SUBLANES, :], axis=0, keepdims=True)
            db_ref[...] = jnp.sum(dbs[...], axis=0, keepdims=True)

    return pl.pallas_call(
        body, name=name, grid=(C // LANES, nt),
        out_shape=(jax.ShapeDtypeStruct((S, C), F32), jax.ShapeDtypeStruct((KP, C), F32),
                   jax.ShapeDtypeStruct((1, C), F32)),
        in_specs=[pl.BlockSpec((T, LANES), lambda c, i: (i, c)),
                  pl.BlockSpec((HB, LANES), lambda c, i: (jnp.minimum((i + 1) * ratio, last_halo), c)),
                  pl.BlockSpec((T, LANES), lambda c, i: (i, c)),
                  pl.BlockSpec((HB, LANES), lambda c, i: (jnp.maximum(i * ratio - 1, 0), c)),
                  pl.BlockSpec((KP, LANES), lambda c, i: (0, c))],
        out_specs=(pl.BlockSpec((T, LANES), lambda c, i: (i, c)),
                   pl.BlockSpec((KP, LANES), lambda c, i: (0, c)),
                   pl.BlockSpec((1, LANES), lambda c, i: (0, c))),
        scratch_shapes=[pltpu.VMEM((T + HB, LANES), F32), pltpu.VMEM((HB + T, LANES), F32),
                        pltpu.VMEM((KP * SUBLANES, LANES), F32), pltpu.VMEM((SUBLANES, LANES), F32)],
        compiler_params=_cparams(),
    )(d, d, u, u, w)


SCW = 512
ZE = 3072
QL = 256
KVL = 128


def _rms_rows(x, g):
    rstd = lax.rsqrt(jnp.mean(x * x, axis=-1, keepdims=True) + EPS)
    return (x * rstd) * g


def _even_pre(z, qg, kvg, name):
    S = z.shape[0]
    T = _row_tile(S)

    def body(ac_ref, ax_ref, cq_ref, ckv_ref, qg_ref, kvg_ref, u_ref, qn_ref, kvn_ref):
        u_ref[...] = _f32(ac_ref) * _f32(ax_ref)
        qn_ref[...] = _rms_rows(_f32(cq_ref), qg_ref[...]).astype(BF16)
        kvn_ref[...] = _rms_rows(_f32(ckv_ref), kvg_ref[...]).astype(BF16)

    return pl.pallas_call(
        body, name=name, grid=(S // T,),
        out_shape=(jax.ShapeDtypeStruct((S, SCW), F32), jax.ShapeDtypeStruct((S, QL), BF16),
                   jax.ShapeDtypeStruct((S, KVL), BF16)),
        in_specs=[_rows(T, SCW, 1), _rows(T, SCW, 2), _rows(T, QL, 10), _rows(T, KVL, 22),
                  _const((1, QL)), _const((1, KVL))],
        out_specs=(_rows(T, SCW), _rows(T, QL), _rows(T, KVL)),
        compiler_params=_cparams(),
    )(z, z, z, z, qg, kvg)


def _qkv_fwd(qn, kvn, z, tabs, w_q, w_kv, name):
    S = qn.shape[0]
    T = _row_tile(S)
    HW = HEADS * HEAD_PAD
    scale = 1.0 / math.sqrt(QK_NOPE + QK_ROPE)

    def body(qn_ref, kvn_ref, kr_ref, ct_ref, ut_ref, dt_ref, wq_ref, wkv_ref, q_ref, k_ref, v_ref):
        ct, ut, dt = ct_ref[...], ut_ref[...], dt_ref[...]
        qa = jnp.dot(qn_ref[...], wq_ref[...], preferred_element_type=F32)
        kva = jnp.dot(kvn_ref[...], wkv_ref[...], preferred_element_type=F32)
        kr = _f32(kr_ref)
        for h in range(HEADS):
            sl = slice(h * HEAD_PAD, (h + 1) * HEAD_PAD)
            q_ref[:, sl] = (_rope(qa[:, sl], ct, ut, dt) * scale).astype(BF16)
            k_ref[:, sl] = _rope(kva[:, sl] + kr, ct, ut, dt).astype(BF16)
        v_ref[...] = kva[:, HW:].astype(BF16)

    return pl.pallas_call(
        body, name=name, grid=(S // T,),
        out_shape=(jax.ShapeDtypeStruct((S, HW), BF16), jax.ShapeDtypeStruct((S, HW), BF16),
                   jax.ShapeDtypeStruct((S, HEADS * V_HEAD), BF16)),
        in_specs=[_rows(T, QL), _rows(T, KVL), _rows(T, HEAD_PAD, 23),
                  _rows(T, HEAD_PAD), _rows(T, HEAD_PAD), _rows(T, HEAD_PAD),
                  _const(w_q.shape), _const(w_kv.shape)],
        out_specs=(_rows(T, HW), _rows(T, HW), _rows(T, HEADS * V_HEAD)),
        compiler_params=_cparams(),
    )(qn, kvn, z, *tabs, w_q, w_kv)


def _attn_tile(S):
    return min(256, S)


def _chunk_mask(TQ):
    r = lax.broadcasted_iota(jnp.int32, (TQ, TQ), 0) // CHUNK
    c = lax.broadcasted_iota(jnp.int32, (TQ, TQ), 1) // CHUNK
    return c <= r


_NT = (((1,), (1,)), ((), ()))
_TN = (((0,), (0,)), ((), ()))


def _attn_fwd(q, k, v, name):
    S = q.shape[0]
    TQ = _attn_tile(S)
    nq = S // TQ
    PW = 2 * HEAD_PAD

    def body(q_ref, k_ref, v_ref, o_ref, lse_ref, m_s, l_s, acc_s):
        i = pl.program_id(1)
        left = lax.broadcasted_iota(jnp.int32, (TQ, LANES), 1) < V_HEAD
        m_s[...] = jnp.full_like(m_s, NEG)
        l_s[...] = jnp.zeros_like(l_s)
        acc_s[...] = jnp.zeros_like(acc_s)
        qv = q_ref[...]

        def step(j, masked):
            r0 = pl.multiple_of(j * TQ, TQ)
            kb = k_ref[pl.ds(r0, TQ), :]
            vb = v_ref[pl.ds(r0, TQ), :]
            alphas, pvs = [], []
            for h in range(2):
                sl = slice(h * HEAD_PAD, (h + 1) * HEAD_PAD)
                s = lax.dot_general(qv[:, sl], kb[:, sl], _NT, preferred_element_type=F32)
                if masked:
                    s = jnp.where(_chunk_mask(TQ), s, NEG)
                m_prev = m_s[h]
                m_new = jnp.maximum(m_prev, jnp.max(s, axis=1, keepdims=True))
                alpha = jnp.exp(m_prev - m_new)
                p = jnp.exp(s - m_new[:, 0:1])
                l_s[h] = alpha * l_s[h] + jnp.sum(p, axis=1, keepdims=True)
                m_s[h] = m_new
                alphas.append(alpha)
                pvs.append(jnp.dot(p.astype(BF16), vb, preferred_element_type=F32))
            acc_s[...] = acc_s[...] * jnp.where(left, alphas[0], alphas[1]) + jnp.where(left, pvs[0], pvs[1])

        def loop_body(j, carry):
            step(j, False)
            return carry

        lax.fori_loop(0, i, loop_body, 0)
        step(i, True)
        o_ref[...] = acc_s[...] / jnp.where(left, l_s[0], l_s[1])
        lse_ref[...] = jnp.where(left, m_s[0] + jnp.log(l_s[0]), m_s[1] + jnp.log(l_s[1]))

    return pl.pallas_call(
        body, name=name, grid=(HEADS // 2, nq),
        out_shape=(jax.ShapeDtypeStruct((S, HEADS * V_HEAD), F32), jax.ShapeDtypeStruct((S, HEADS * V_HEAD), F32)),
        in_specs=[pl.BlockSpec((TQ, PW), lambda p, i: (i, p)),
                  pl.BlockSpec((S, PW), lambda p, i: (0, p)),
                  pl.BlockSpec((S, LANES), lambda p, i: (0, p))],
        out_specs=(pl.BlockSpec((TQ, LANES), lambda p, i: (i, p)),
                   pl.BlockSpec((TQ, LANES), lambda p, i: (i, p))),
        scratch_shapes=[pltpu.VMEM((2, TQ, LANES), F32), pltpu.VMEM((2, TQ, LANES), F32),
                        pltpu.VMEM((TQ, LANES), F32)],
        compiler_params=_cparams(),
    )(q, k, v)


def _attn_dq(q, k, v, do, lse, delta, name):
    S = q.shape[0]
    TQ = _attn_tile(S)
    nq = S // TQ
    PW = 2 * HEAD_PAD

    def body(q_ref, k_ref, v_ref, do_ref, lse_ref, dl_ref, dq_ref, acc_s):
        i = pl.program_id(1)
        left = lax.broadcasted_iota(jnp.int32, (TQ, LANES), 1) < V_HEAD
        acc_s[...] = jnp.zeros_like(acc_s)
        qv = q_ref[...]
        dov = do_ref[...]
        dos = [jnp.where(left, dov, jnp.zeros_like(dov)), jnp.where(left, jnp.zeros_like(dov), dov)]
        lses = [lse_ref[:, 0:1], lse_ref[:, V_HEAD:V_HEAD + 1]]
        dls = [dl_ref[:, 0:1], dl_ref[:, V_HEAD:V_HEAD + 1]]

        def step(j, masked):
            r0 = pl.multiple_of(j * TQ, TQ)
            kb = k_ref[pl.ds(r0, TQ), :]
            vb = v_ref[pl.ds(r0, TQ), :]
            for h in range(2):
                sl = slice(h * HEAD_PAD, (h + 1) * HEAD_PAD)
                s = lax.dot_general(qv[:, sl], kb[:, sl], _NT, preferred_element_type=F32)
                p = jnp.exp(s - lses[h])
                if masked:
                    p = jnp.where(_chunk_mask(TQ), p, 0.0)
                dp = lax.dot_general(dos[h], vb, _NT, preferred_element_type=F32)
                ds = (p * (dp - dls[h])).astype(BF16)
                acc_s[:, sl] += jnp.dot(ds, kb[:, sl], preferred_element_type=F32)

        def loop_body(j, carry):
            step(j, False)
            return carry

        lax.fori_loop(0, i, loop_body, 0)
        step(i, True)
        dq_ref[...] = acc_s[...]

    return pl.pallas_call(
        body, name=name, grid=(HEADS // 2, nq),
        out_shape=jax.ShapeDtypeStruct((S, HEADS * HEAD_PAD), F32),
        in_specs=[pl.BlockSpec((TQ, PW), lambda p, i: (i, p)),
                  pl.BlockSpec((S, PW), lambda p, i: (0, p)),
                  pl.BlockSpec((S, LANES), lambda p, i: (0, p)),
                  pl.BlockSpec((TQ, LANES), lambda p, i: (i, p)),
                  pl.BlockSpec((TQ, LANES), lambda p, i: (i, p)),
                  pl.BlockSpec((TQ, LANES), lambda p, i: (i, p))],
        out_specs=pl.BlockSpec((TQ, PW), lambda p, i: (i, p)),
        scratch_shapes=[pltpu.VMEM((TQ, PW), F32)],
        compiler_params=_cparams(),
    )(q, k, v, do, lse, delta)


def _attn_dkv(q, k, v, do, lse, delta, name):
    S = q.shape[0]
    TQ = _attn_tile(S)
    nq = S // TQ
    PW = 2 * HEAD_PAD

    def body(q_ref, k_ref, v_ref, do_ref, lse_ref, dl_ref, dk_ref, dv_ref, dk_s, dv_s):
        j = pl.program_id(1)
        left = lax.broadcasted_iota(jnp.int32, (TQ, LANES), 1) < V_HEAD
        dk_s[...] = jnp.zeros_like(dk_s)
        dv_s[...] = jnp.zeros_like(dv_s)
        kb = k_ref[...]
        vb = v_ref[...]

        def step(i, masked):
            r0 = pl.multiple_of(i * TQ, TQ)
            qb = q_ref[pl.ds(r0, TQ), :]
            dov = do_ref[pl.ds(r0, TQ), :]
            lse = lse_ref[pl.ds(r0, TQ), :]
            dl = dl_ref[pl.ds(r0, TQ), :]
            dos = [jnp.where(left, dov, jnp.zeros_like(dov)), jnp.where(left, jnp.zeros_like(dov), dov)]
            for h in range(2):
                sl = slice(h * HEAD_PAD, (h + 1) * HEAD_PAD)
                c0 = h * V_HEAD
                s = lax.dot_general(qb[:, sl], kb[:, sl], _NT, preferred_element_type=F32)
                p = jnp.exp(s - lse[:, c0:c0 + 1])
                if masked:
                    p = jnp.where(_chunk_mask(TQ), p, 0.0)
                dv_s[...] += lax.dot_general(p.astype(BF16), dos[h], _TN, preferred_element_type=F32)
                dp = lax.dot_general(dos[h], vb, _NT, preferred_element_type=F32)
                ds = (p * (dp - dl[:, c0:c0 + 1])).astype(BF16)
                dk_s[:, sl] += lax.dot_general(ds, qb[:, sl], _TN, preferred_element_type=F32)

        def loop_body(i, carry):
            step(i, False)
            return carry

        step(j, True)
        lax.fori_loop(j + 1, nq, loop_body, 0)
        dk_ref[...] = dk_s[...]
        dv_ref[...] = dv_s[...]

    return pl.pallas_call(
        body, name=name, grid=(HEADS // 2, nq),
        out_shape=(jax.ShapeDtypeStruct((S, HEADS * HEAD_PAD), F32), jax.ShapeDtypeStruct((S, HEADS * V_HEAD), F32)),
        in_specs=[pl.BlockSpec((S, PW), lambda p, j: (0, p)),
                  pl.BlockSpec((TQ, PW), lambda p, j: (j, p)),
                  pl.BlockSpec((TQ, LANES), lambda p, j: (j, p)),
                  pl.BlockSpec((S, LANES), lambda p, j: (0, p)),
                  pl.BlockSpec((S, LANES), lambda p, j: (0, p)),
                  pl.BlockSpec((S, LANES), lambda p, j: (0, p))],
        out_specs=(pl.BlockSpec((TQ, PW), lambda p, j: (j, p)),
                   pl.BlockSpec((TQ, LANES), lambda p, j: (j, p))),
        scratch_shapes=[pltpu.VMEM((TQ, PW), F32), pltpu.VMEM((TQ, LANES), F32)],
        compiler_params=_cparams(),
    )(q, k, v, do, lse, delta)


LOG2E = math.log2(math.e)
ATTN_FWD_HEADS = 8
ATTN_BWD_HEADS = 4


def _chunk_mask_t(T):
    key = lax.broadcasted_iota(jnp.int32, (T, T), 0) // CHUNK
    qry = lax.broadcasted_iota(jnp.int32, (T, T), 1) // CHUNK
    return key <= qry


def _qkv_fwd_t(qn, kvn, z, tabs, w_q, w_kv, name):
    S = qn.shape[0]
    T = _attn_tile(S)
    HW = HEADS * HEAD_PAD
    scale = LOG2E / math.sqrt(QK_NOPE + QK_ROPE)

    def body(qn_ref, kvn_ref, kr_ref, ct_ref, ut_ref, dt_ref, wq_ref, wkv_ref, q_ref, k_ref, v_ref, kt_ref, vt_ref):
        ct, ut, dt = ct_ref[...], ut_ref[...], dt_ref[...]
        qa = jnp.dot(qn_ref[...], wq_ref[...], preferred_element_type=F32)
        kva = jnp.dot(kvn_ref[...], wkv_ref[...], preferred_element_type=F32)
        kr = _f32(kr_ref)
        ones_row = (lax.broadcasted_iota(jnp.int32, (V_HEAD, T), 0) == 0).astype(F32)
        for h in range(HEADS):
            sl = slice(h * HEAD_PAD, (h + 1) * HEAD_PAD)
            q_ref[:, sl] = (_rope(qa[:, sl], ct, ut, dt) * scale).astype(BF16)
            kh = _rope(kva[:, sl] + kr, ct, ut, dt)
            k_ref[:, sl] = kh.astype(BF16)
            kt_ref[0, sl, :] = kh.T.astype(BF16)
        v_ref[...] = kva[:, HW:].astype(BF16)
        for p in range(HEADS // 2):
            vpt = kva[:, HW + p * LANES:HW + (p + 1) * LANES].T
            for h in range(2):
                r0 = (2 * p + h) * HEAD_PAD
                vt_ref[0, r0:r0 + V_HEAD, :] = vpt[h * V_HEAD:(h + 1) * V_HEAD, :].astype(BF16)
                vt_ref[0, r0 + V_HEAD:r0 + HEAD_PAD, :] = ones_row.astype(BF16)

    t3 = jax.ShapeDtypeStruct((S // T, HW, T), BF16)
    return pl.pallas_call(
        body, name=name, grid=(S // T,),
        out_shape=(jax.ShapeDtypeStruct((S, HW), BF16), jax.ShapeDtypeStruct((S, HW), BF16),
                   jax.ShapeDtypeStruct((S, HEADS * V_HEAD), BF16), t3, t3),
        in_specs=[_rows(T, QL), _rows(T, KVL), _rows(T, HEAD_PAD, 23),
                  _rows(T, HEAD_PAD), _rows(T, HEAD_PAD), _rows(T, HEAD_PAD),
                  _const(w_q.shape), _const(w_kv.shape)],
        out_specs=(_rows(T, HW), _rows(T, HW), _rows(T, HEADS * V_HEAD),
                   pl.BlockSpec((1, HW, T), lambda i: (i, 0, 0)), pl.BlockSpec((1, HW, T), lambda i: (i, 0, 0))),
        compiler_params=_cparams(),
    )(qn, kvn, z, *tabs, w_q, w_kv)


def _attn_fwd_t(q, k, vT3, name):
    S = q.shape[0]
    T = _attn_tile(S)
    nq = S // T
    NH = ATTN_FWD_HEADS
    PW = NH * HEAD_PAD

    def body(q_ref, k_ref, vt_ref, o_ref, lse_ref, m_s, acc_s):
        i = pl.program_id(1)
        m_s[...] = jnp.full_like(m_s, NEG)
        acc_s[...] = jnp.zeros_like(acc_s)
        qv = q_ref[...]

        def step(j, masked):
            kb = k_ref[pl.ds(pl.multiple_of(j * T, T), T), :]
            vt = vt_ref[j]
            heads = [slice(h * HEAD_PAD, (h + 1) * HEAD_PAD) for h in range(NH)]
            sts = [lax.dot_general(kb[:, sl], qv[:, sl], _NT, preferred_element_type=F32) for sl in heads]
            alphas, pvs = [], []
            for h, sl in enumerate(heads):
                st = jnp.where(_chunk_mask_t(T), sts[h], NEG) if masked else sts[h]
                m_prev = m_s[h]
                m_new = jnp.maximum(m_prev, jnp.max(st, axis=0, keepdims=True))
                alphas.append(jnp.exp2(m_prev[0:1] - m_new[0:1]))
                pt = jnp.exp2(st - m_new[0:1]).astype(BF16)
                m_s[h] = m_new
                pvs.append(jnp.dot(vt[sl, :], pt, preferred_element_type=F32))
            for h in range(NH):
                acc_s[h] = acc_s[h] * alphas[h] + pvs[h]

        def loop_body(j, carry):
            step(j, False)
            return carry

        lax.fori_loop(0, i, loop_body, 0)
        step(i, True)
        for g in range(NH // 2):
            outs = []
            for h in (2 * g, 2 * g + 1):
                acc = acc_s[h]
                l_row = acc[V_HEAD:V_HEAD + 1, :]
                outs.append(acc[0:V_HEAD, :] / l_row)
                lse_ref[0, h * SUBLANES:(h + 1) * SUBLANES, :] = m_s[h] + jnp.log2(l_row)
            o_ref[:, g * LANES:(g + 1) * LANES] = jnp.concatenate(outs, axis=0).T

    return pl.pallas_call(
        body, name=name, grid=(HEADS // NH, nq),
        out_shape=(jax.ShapeDtypeStruct((S, HEADS * V_HEAD), F32),
                   jax.ShapeDtypeStruct((nq, HEADS * SUBLANES, T), F32)),
        in_specs=[pl.BlockSpec((T, PW), lambda p, i: (i, p)),
                  pl.BlockSpec((S, PW), lambda p, i: (0, p)),
                  pl.BlockSpec((nq, PW, T), lambda p, i: (0, p, 0))],
        out_specs=(pl.BlockSpec((T, NH * V_HEAD), lambda p, i: (i, p)),
                   pl.BlockSpec((1, NH * SUBLANES, T), lambda p, i: (i, p, 0))),
        scratch_shapes=[pltpu.VMEM((NH, SUBLANES, T), F32), pltpu.VMEM((NH, HEAD_PAD, T), F32)],
        compiler_params=_cparams(),
    )(q, k, vT3)


def _attn_bwd_t(q, k, v, kT3, do, lse3, dl3, name):
    S = q.shape[0]
    T = _attn_tile(S)
    nq = S // T
    NH = ATTN_BWD_HEADS
    PW = NH * HEAD_PAD
    VW = NH * V_HEAD

    def body(q_ref, k_ref, v_ref, kt_ref, do_ref, lse_ref, dl_ref, dq_ref, dk_ref, dv_ref, dk_s, dv_s):
        j = pl.program_id(1)
        left = lax.broadcasted_iota(jnp.int32, (T, LANES), 1) < V_HEAD

        @pl.when(j == 0)
        def _():
            dq_ref[...] = jnp.zeros_like(dq_ref)

        dk_s[...] = jnp.zeros_like(dk_s)
        dv_s[...] = jnp.zeros_like(dv_s)
        kb = k_ref[...]
        vms = []
        for g in range(NH // 2):
            vb = v_ref[:, g * LANES:(g + 1) * LANES]
            vms += [jnp.where(left, vb, jnp.zeros_like(vb)), jnp.where(left, jnp.zeros_like(vb), vb)]
        kt = kt_ref[0]

        def step(i, masked):
            r0 = pl.multiple_of(i * T, T)
            qb = q_ref[pl.ds(r0, T), :]
            do_all = do_ref[pl.ds(r0, T), :]
            lse = lse_ref[i]
            dl = dl_ref[i]
            heads = [slice(h * HEAD_PAD, (h + 1) * HEAD_PAD) for h in range(NH)]
            dobs = [do_all[:, (h // 2) * LANES:(h // 2 + 1) * LANES] for h in range(NH)]
            sts = [lax.dot_general(kb[:, sl], qb[:, sl], _NT, preferred_element_type=F32) for sl in heads]
            dpts = [lax.dot_general(vms[h], dobs[h], _NT, preferred_element_type=F32) for h in range(NH)]
            res = []
            for h, sl in enumerate(heads):
                r8 = h * SUBLANES
                pt = jnp.exp2(sts[h] - lse[r8:r8 + 1, :])
                if masked:
                    pt = jnp.where(_chunk_mask_t(T), pt, 0.0)
                dst = (pt * (dpts[h] - dl[r8:r8 + 1, :])).astype(BF16)
                res.append((jnp.dot(pt.astype(BF16), dobs[h], preferred_element_type=F32),
                            jnp.dot(dst, qb[:, sl], preferred_element_type=F32),
                            jnp.dot(kt[sl, :], dst, preferred_element_type=F32)))
            for h, sl in enumerate(heads):
                dv_s[h] += res[h][0]
                dk_s[:, sl] += res[h][1]
                dq_ref[i, sl, :] += res[h][2]

        def loop_body(i, carry):
            step(i, False)
            return carry

        step(j, True)
        lax.fori_loop(j + 1, nq, loop_body, 0)
        dk_ref[...] = dk_s[...] * (1.0 / LOG2E)
        for g in range(NH // 2):
            dv_ref[:, g * LANES:(g + 1) * LANES] = jnp.where(left, dv_s[2 * g], dv_s[2 * g + 1])

    return pl.pallas_call(
        body, name=name, grid=(HEADS // NH, nq),
        out_shape=(jax.ShapeDtypeStruct((nq, HEADS * HEAD_PAD, T), F32),
                   jax.ShapeDtypeStruct((S, HEADS * HEAD_PAD), F32), jax.ShapeDtypeStruct((S, HEADS * V_HEAD), F32)),
        in_specs=[pl.BlockSpec((S, PW), lambda p, j: (0, p)),
                  pl.BlockSpec((T, PW), lambda p, j: (j, p)),
                  pl.BlockSpec((T, VW), lambda p, j: (j, p)),
                  pl.BlockSpec((1, PW, T), lambda p, j: (j, p, 0)),
                  pl.BlockSpec((S, VW), lambda p, j: (0, p)),
                  pl.BlockSpec((nq, NH * SUBLANES, T), lambda p, j: (0, p, 0)),
                  pl.BlockSpec((nq, NH * SUBLANES, T), lambda p, j: (0, p, 0))],
        out_specs=(pl.BlockSpec((nq, PW, T), lambda p, j: (0, p, 0)),
                   pl.BlockSpec((T, PW), lambda p, j: (j, p)),
                   pl.BlockSpec((T, VW), lambda p, j: (j, p))),
        scratch_shapes=[pltpu.VMEM((T, PW), F32), pltpu.VMEM((NH, T, LANES), F32)],
        compiler_params=_cparams(),
    )(q, k, v, kT3, do, lse3, dl3)


def _even_post(z, cv, o, name):
    S = z.shape[0]
    T = _row_tile(S)

    def body(ab_ref, ag_ref, bg_ref, cv_ref, o_ref, y_ref):
        y_ref[:, 0:SCW] = (_f32(ab_ref) * cv_ref[...] * _silu(_f32(ag_ref))).astype(BF16)
        y_ref[:, SCW:2 * SCW] = (o_ref[...] * _silu(_f32(bg_ref))).astype(BF16)

    return pl.pallas_call(
        body, name=name, grid=(S // T,),
        out_shape=jax.ShapeDtypeStruct((S, 2 * SCW), BF16),
        in_specs=[_rows(T, SCW, 0), _rows(T, SCW, 3), _rows(T, SCW, 4), _rows(T, SCW), _rows(T, SCW)],
        out_specs=_rows(T, 2 * SCW), compiler_params=_cparams(),
    )(z, z, z, cv, o)


def _even_bwd_gates(dyc, z, cv, o, name):
    S = z.shape[0]
    T = _row_tile(S)

    def body(dya_ref, dyb_ref, ab_ref, ag_ref, bg_ref, cv_ref, o_ref,
             dab_ref, dag_ref, dbg_ref, dcv_ref, do_ref, dl_ref):
        dya, ab, ag, cv = dya_ref[...], _f32(ab_ref), _f32(ag_ref), cv_ref[...]
        sg = _silu(ag)
        dab_ref[...] = (dya * cv * sg).astype(BF16)
        dcv_ref[...] = dya * ab * sg
        dag_ref[...] = (dya * ab * cv * _dsilu(ag)).astype(BF16)
        dyb, bg, ov = dyb_ref[...], _f32(bg_ref), o_ref[...]
        dov = dyb * _silu(bg)
        do_ref[...] = dov.astype(BF16)
        dbg_ref[...] = (dyb * ov * _dsilu(bg)).astype(BF16)
        prod = dov * ov
        left = lax.broadcasted_iota(jnp.int32, (T, LANES), 1) < V_HEAD
        for p in range(HEADS // 2):
            blk = prod[:, p * LANES:(p + 1) * LANES]
            s0 = jnp.sum(jnp.where(left, blk, 0.0), axis=1, keepdims=True)
            s1 = jnp.sum(jnp.where(left, 0.0, blk), axis=1, keepdims=True)
            dt = jnp.where(left, s0, s1).T
            dl_ref[0, 2 * p * SUBLANES:(2 * p + 1) * SUBLANES, :] = dt[0:SUBLANES, :]
            dl_ref[0, (2 * p + 1) * SUBLANES:(2 * p + 2) * SUBLANES, :] = dt[V_HEAD:V_HEAD + SUBLANES, :]

    assert T == _attn_tile(S)
    bf = jax.ShapeDtypeStruct((S, SCW), BF16)
    ff = jax.ShapeDtypeStruct((S, SCW), F32)
    return pl.pallas_call(
        body, name=name, grid=(S // T,),
        out_shape=(bf, bf, bf, ff, bf, jax.ShapeDtypeStruct((S // T, HEADS * SUBLANES, T), F32)),
        in_specs=[_rows(T, SCW, 0), _rows(T, SCW, 1), _rows(T, SCW, 0), _rows(T, SCW, 3), _rows(T, SCW, 4),
                  _rows(T, SCW), _rows(T, SCW)],
        out_specs=(_rows(T, SCW),) * 5 + (pl.BlockSpec((1, HEADS * SUBLANES, T), lambda i: (i, 0, 0)),),
        compiler_params=_cparams(),
    )(dyc, dyc, z, z, z, cv, o)


def _qkv_bwd(dq, dk, dv, z, tabs, w_q, w_kv, qg, kvg, name):
    S = dk.shape[0]
    T = _attn_tile(S)
    HW = HEADS * HEAD_PAD
    VW = HEADS * V_HEAD
    scale = 1.0 / math.sqrt(QK_NOPE + QK_ROPE)

    def fn(ins, outs):
        dq_ref, dk_ref, dv_ref, cq_ref, ckv_ref, ct_ref, ut_ref, dt_ref, wq_ref, wkv_ref, qg_ref, kvg_ref = ins
        dqp_ref, dkvp_ref, dcq_ref, dckv_ref, dkr_ref = outs
        ct, ut, dt = ct_ref[...], ut_ref[...], dt_ref[...]
        dkr = jnp.zeros((T, HEAD_PAD), F32)
        for h in range(HEADS):
            sl = slice(h * HEAD_PAD, (h + 1) * HEAD_PAD)
            dqp_ref[:, sl] = (_rope_t(dq_ref[0, sl, :].T, ct, ut, dt) * scale).astype(BF16)
            dkh = _rope_t(dk_ref[:, sl], ct, ut, dt)
            dkr = dkr + dkh
            dkvp_ref[:, sl] = dkh.astype(BF16)
        dkvp_ref[:, HW:] = dv_ref[...].astype(BF16)
        dkr_ref[...] = dkr.astype(BF16)
        sums = []
        for lat_ref, g_ref, dpre_ref, w_ref, dlat_ref in ((cq_ref, qg_ref, dqp_ref, wq_ref, dcq_ref),
                                                         (ckv_ref, kvg_ref, dkvp_ref, wkv_ref, dckv_ref)):
            dn = lax.dot_general(dpre_ref[...], w_ref[...], _NT, preferred_element_type=F32)
            xv = _f32(lat_ref)
            rstd = lax.rsqrt(jnp.mean(xv * xv, axis=-1, keepdims=True) + EPS)
            xh = xv * rstd
            dxh = dn * g_ref[...]
            dlat_ref[...] = (rstd * (dxh - xh * jnp.mean(dxh * xh, axis=-1, keepdims=True))).astype(BF16)
            sums.append(dn * xh)
        return sums

    return _col_sums(
        2, fn, [dq, dk, dv, z, z, *tabs, w_q, w_kv, qg, kvg],
        [pl.BlockSpec((1, HW, T), lambda i: (i, 0, 0)), _rows(T, HW), _rows(T, VW), _rows(T, QL, 10), _rows(T, KVL, 22),
         _rows(T, HEAD_PAD), _rows(T, HEAD_PAD), _rows(T, HEAD_PAD),
         _const(w_q.shape), _const(w_kv.shape), _const((1, QL)), _const((1, KVL))],
        [jax.ShapeDtypeStruct((S, HW), BF16), jax.ShapeDtypeStruct((S, HW + VW), BF16),
         jax.ShapeDtypeStruct((S, QL), BF16), jax.ShapeDtypeStruct((S, KVL), BF16),
         jax.ShapeDtypeStruct((S, HEAD_PAD), BF16)],
        [_rows(T, HW), _rows(T, HW + VW), _rows(T, QL), _rows(T, KVL), _rows(T, HEAD_PAD)],
        S, T, [QL, KVL], name)


def _even_dz(dab, du, z, dag, dbg, dcq, dckv, dkr, name):
    S = z.shape[0]
    T = _row_tile(S)

    def body(dab_ref, du_ref, ac_ref, ax_ref, dag_ref, dbg_ref, dcq_ref, dckv_ref, dkr_ref, dz_ref):
        duv = du_ref[...]
        dz_ref[:, 0:SCW] = dab_ref[...]
        dz_ref[:, SCW:2 * SCW] = (duv * _f32(ax_ref)).astype(BF16)
        dz_ref[:, 2 * SCW:3 * SCW] = (duv * _f32(ac_ref)).astype(BF16)
        dz_ref[:, 3 * SCW:4 * SCW] = dag_ref[...]
        dz_ref[:, 4 * SCW:5 * SCW] = dbg_ref[...]
        dz_ref[:, 5 * SCW:5 * SCW + QL] = dcq_ref[...]
        dz_ref[:, 5 * SCW + QL:5 * SCW + QL + KVL] = dckv_ref[...]
        dz_ref[:, 5 * SCW + QL + KVL:ZE] = dkr_ref[...]

    return pl.pallas_call(
        body, name=name, grid=(S // T,),
        out_shape=jax.ShapeDtypeStruct((S, ZE), BF16),
        in_specs=[_rows(T, SCW), _rows(T, SCW), _rows(T, SCW, 1), _rows(T, SCW, 2), _rows(T, SCW), _rows(T, SCW),
                  _rows(T, QL), _rows(T, KVL), _rows(T, HEAD_PAD)],
        out_specs=_rows(T, ZE), compiler_params=_cparams(),
    )(dab, du, z, z, dag, dbg, dcq, dckv, dkr)


def _odd_pre(z, name):
    S, D = z.shape[0], z.shape[1] // 3
    T = _row_tile(S)

    def body(val_ref, glu_ref, u_ref):
        u_ref[...] = _f32(val_ref) * _sigmoid(_f32(glu_ref))

    return pl.pallas_call(
        body, name=name, grid=(S // T,),
        out_shape=jax.ShapeDtypeStruct((S, D), F32),
        in_specs=[_rows(T, D, 0), _rows(T, D, 1)], out_specs=_rows(T, D),
        compiler_params=_cparams(),
    )(z, z)


def _layer_norm_stats(cv):
    mu = jnp.mean(cv, axis=-1, keepdims=True)
    cen = cv - mu
    rstd = lax.rsqrt(jnp.mean(cen * cen, axis=-1, keepdims=True) + EPS)
    return cen * rstd, rstd


def _odd_post(cv, z, ln_g, ln_b, name):
    S, D = cv.shape
    T = _row_tile(S)

    def body(cv_ref, sg_ref, g_ref, b_ref, y_ref):
        cvh, _ = _layer_norm_stats(cv_ref[...])
        y_ref[...] = (_silu(cvh * g_ref[...] + b_ref[...]) * _silu(_f32(sg_ref))).astype(BF16)

    return pl.pallas_call(
        body, name=name, grid=(S // T,),
        out_shape=jax.ShapeDtypeStruct((S, D), BF16),
        in_specs=[_rows(T, D), _rows(T, D, 2), _const((1, D)), _const((1, D))],
        out_specs=_rows(T, D), compiler_params=_cparams(),
    )(cv, z, ln_g, ln_b)


def _odd_bwd_norm(dyi, cv, z, ln_g, ln_b, name):
    S, D = cv.shape
    T = _row_tile(S)

    def fn(ins, outs):
        dy_ref, cv_ref, sg_ref, g_ref, b_ref = ins
        dcv_ref, dsg_ref = outs
        cvh, rstd = _layer_norm_stats(cv_ref[...])
        ln = cvh * g_ref[...] + b_ref[...]
        sgv, dy = _f32(sg_ref), dy_ref[...]
        dsg_ref[...] = (dy * _silu(ln) * _dsilu(sgv)).astype(BF16)
        dln = dy * _silu(sgv) * _dsilu(ln)
        dh = dln * g_ref[...]
        dcv_ref[...] = rstd * (dh - jnp.mean(dh, axis=-1, keepdims=True)
                               - cvh * jnp.mean(dh * cvh, axis=-1, keepdims=True))
        return [dln * cvh, dln]

    return _col_sums(2, fn, [dyi, cv, z, ln_g, ln_b],
                     [_rows(T, D), _rows(T, D), _rows(T, D, 2), _const((1, D)), _const((1, D))],
                     [jax.ShapeDtypeStruct((S, D), F32), jax.ShapeDtypeStruct((S, D), BF16)],
                     [_rows(T, D), _rows(T, D)], S, T, [D, D], name)


def _odd_dz(du, z, dsg, name):
    S, D = du.shape
    T = _row_tile(S)

    def body(du_ref, val_ref, glu_ref, dsg_ref, dz_ref):
        duv = du_ref[...]
        sig = _sigmoid(_f32(glu_ref))
        dz_ref[:, 0:D] = (duv * sig).astype(BF16)
        dz_ref[:, D:2 * D] = (duv * _f32(val_ref) * sig * (1.0 - sig)).astype(BF16)
        dz_ref[:, 2 * D:3 * D] = dsg_ref[...]

    return pl.pallas_call(
        body, name=name, grid=(S // T,),
        out_shape=jax.ShapeDtypeStruct((S, 3 * D), BF16),
        in_specs=[_rows(T, D), _rows(T, D, 0), _rows(T, D, 1), _rows(T, D)],
        out_specs=_rows(T, 3 * D), compiler_params=_cparams(),
    )(du, z, z, dsg)


ADAM_BLOCK_ELEMS = 128 * 1024


def _adam_tiles(R, C):
    if R * C <= ADAM_BLOCK_ELEMS:
        return R, C
    tr = R
    for cand in range(SUBLANES, R, SUBLANES):
        if R % cand == 0 and cand * C <= ADAM_BLOCK_ELEMS:
            tr = cand
    if tr < R:
        return tr, C
    tc = C
    for cand in range(LANES, C, LANES):
        if C % cand == 0 and R * cand <= ADAM_BLOCK_ELEMS:
            tc = cand
    return R, tc


def _adamw(g_parts, w, m, v, name):
    if not isinstance(g_parts, (list, tuple)):
        g_parts = [g_parts]
    ng = len(g_parts)
    _, R, C = g_parts[0].shape
    tr, tc = _adam_tiles(R, C)

    def body(*refs):
        g_refs = refs[:ng]
        w_ref, m_ref, v_ref, go_ref, d_ref, mo_ref, vo_ref = refs[ng:]
        g = None
        for g_ref in g_refs:
            for p in range(g_ref.shape[0]):
                part = g_ref[p].astype(F32)
                g = part if g is None else g + part
        mn = ADAM_B1 * m_ref[...] + (1.0 - ADAM_B1) * g
        vn = ADAM_B2 * v_ref[...] + (1.0 - ADAM_B2) * (g * g)
        m_hat = mn / (1.0 - ADAM_B1 ** ADAM_STEP)
        v_hat = vn / (1.0 - ADAM_B2 ** ADAM_STEP)
        go_ref[...] = g
        d_ref[...] = -ADAM_LR * (m_hat / (jnp.sqrt(v_hat) + ADAM_EPS) + ADAM_WD * w_ref[...])
        mo_ref[...] = mn
        vo_ref[...] = vn

    slab = jax.ShapeDtypeStruct((R, C), F32)
    blk = pl.BlockSpec((tr, tc), lambda i, j: (i, j))
    return pl.pallas_call(
        body, name=name, grid=(R // tr, C // tc),
        out_shape=(slab,) * 4,
        in_specs=[pl.BlockSpec((g.shape[0], tr, tc), lambda i, j: (0, i, j)) for g in g_parts] + [blk, blk, blk],
        out_specs=(blk,) * 4, compiler_params=_cparams(),
    )(*g_parts, w, m, v)


def _gather_cols(g, shape):
    nd = len(shape)
    t = jnp.moveaxis(g, 0, nd - 1)
    return t.reshape(tuple(shape[:-1]) + (N_DEV * shape[-1],))


def _scatter_cols(full, n):
    t = full.reshape(full.shape[:-1] + (N_DEV, n))
    return jnp.moveaxis(t, -2, 0)


def kernel(x, c, positions, ada_w, ada_b, pre_norm_g, post_norm_g, even_w_in, even_sc_conv_w, even_sc_conv_b, even_q_norm_g, even_kv_norm_g, even_w_uq, even_w_ukv, even_w_out, odd_w_in, odd_conv_w, odd_conv_b, odd_ln_g, odd_ln_b, odd_w_out, loss_target, m_ada_w, m_ada_b, m_pre_norm_g, m_post_norm_g, m_even_w_in, m_even_sc_conv_w, m_even_sc_conv_b, m_even_q_norm_g, m_even_kv_norm_g, m_even_w_uq, m_even_w_ukv, m_even_w_out, m_odd_w_in, m_odd_conv_w, m_odd_conv_b, m_odd_ln_g, m_odd_ln_b, m_odd_w_out, v_ada_w, v_ada_b, v_pre_norm_g, v_post_norm_g, v_even_w_in, v_even_sc_conv_w, v_even_sc_conv_b, v_even_q_norm_g, v_even_kv_norm_g, v_even_w_uq, v_even_w_ukv, v_even_w_out, v_odd_w_in, v_odd_conv_w, v_odd_conv_b, v_odd_ln_g, v_odd_ln_b, v_odd_w_out):
    S, D = x.shape[1], x.shape[2]
    L = ada_w.shape[0]
    NE, NO = even_w_in.shape[0], odd_w_in.shape[0]
    me = 4 * lax.axis_index("x") + 2 * lax.axis_index("y") + lax.axis_index("c")
    x0 = x[0]
    target = loss_target[0]

    small_parts = [c, even_sc_conv_w, odd_conv_w, odd_conv_b, odd_ln_g, odd_ln_b]
    small_shapes = [p.shape for p in small_parts]
    sg = _exchange([_pack(small_parts, F32, SUBLANES)], False, "gather_small")[0].reshape(N_DEV, -1)
    c_all, scw_g, ocw_g, ocb_g, olg_g, olb_g = _unpack(sg, small_shapes)
    c_all = c_all.reshape(N_DEV, D)
    sc_conv_w = _gather_cols(scw_g, even_sc_conv_w.shape)
    o_conv_w = _gather_cols(ocw_g, odd_conv_w.shape)
    o_conv_b = _gather_cols(ocb_g, odd_conv_b.shape)
    o_ln_g = _gather_cols(olg_g, odd_ln_g.shape)
    o_ln_b = _gather_cols(olb_g, odd_ln_b.shape)

    pad_q = HEAD_PAD - QK_NOPE - QK_ROPE
    w_local = [jnp.swapaxes(even_w_in, 1, 2).astype(BF16),
               jnp.pad(even_w_uq, ((0, 0), (0, 0), (0, pad_q))).astype(BF16),
               jnp.pad(even_w_ukv[..., :QK_NOPE], ((0, 0), (0, 0), (0, HEAD_PAD - QK_NOPE))).astype(BF16),
               even_w_ukv[..., QK_NOPE:].astype(BF16),
               even_w_out.astype(BF16), odd_w_in.astype(BF16), odd_w_out.astype(BF16)]
    n_ada = ada_w.shape[2]
    ada_b_cols = lax.dynamic_slice_in_dim(ada_b, me * n_ada, n_ada, axis=1).reshape(L, 1, n_ada)
    mod_slab = _ada_fwd(c_all, ada_w, ada_b_cols)
    mod_g = _exchange([_pack([mod_slab], F32, SUBLANES)], False, "gather_mod")[0].reshape(N_DEV, -1)
    mod_all = mod_g[:, :L * N_DEV * n_ada].reshape(N_DEV, L, N_DEV, n_ada)
    mod = lax.dynamic_index_in_dim(mod_all, me, axis=2, keepdims=False)
    mod = jnp.moveaxis(mod, 0, 1).reshape(L, 3 * D)
    shift, scale, gate = mod[:, :D], mod[:, D:2 * D], mod[:, 2 * D:]

    heads_to_cols = lambda g: jnp.moveaxis(g, 0, 1).reshape(g.shape[1], -1)
    w_handles = {}
    token = jnp.broadcast_to(jnp.minimum(jnp.abs(mod[0, 0]), 0.0), (SUBLANES, LANES))
    for layer in range(L):
        i = layer // 2
        groups = ({"in": [w_local[0][i]], "rest": [w[i] for w in w_local[1:5]]} if layer % 2 == 0
                  else {"all": [w[i] for w in w_local[5:]]})
        for key, mine in groups.items():
            mine = [w + token[0, 0].astype(BF16) for w in mine]
            w_handles[layer, key], token = _exchange_start(mine, False, f"gather_weights_start_l{layer}_{key}")
    w_token = token

    def arrived(layer, key, after):
        return _exchange_wait(w_handles[layer, key], False, after, f"gather_weights_wait_l{layer}_{key}")[1]

    e_w_in_k, e_w_q_k, e_w_kv_k, e_w_out, o_w_in, o_w_out = ([None] * NE, [None] * NE, [None] * NE, [None] * NE,
                                                             [None] * NO, [None] * NO)

    half = QK_ROPE // 2
    inv_freq = 1.0 / (ROPE_THETA ** (jnp.arange(0, QK_ROPE, 2, dtype=F32) / QK_ROPE))
    inv_lane = jnp.zeros((HEAD_PAD,), F32).at[QK_NOPE:QK_NOPE + QK_ROPE].set(jnp.concatenate([inv_freq, inv_freq]))
    tabs = _rope_tables(positions.astype(F32).reshape(S, 1), inv_lane.reshape(1, HEAD_PAD))
    del half

    row = lambda a: a.reshape(1, -1)
    scb = even_sc_conv_b
    KP3, KP31 = SUBLANES, 32

    saved = []
    xs = x0
    h = _pre_norm(xs, row(pre_norm_g[0]) + w_token[0, 0], row(scale[0]), row(shift[0]), "pre_norm_l0")
    for layer in range(L):
        i = layer // 2
        tag = f"l{layer}"
        first = [h, tabs[0]] if layer == 0 else h
        if layer % 2 == 0:
            wt = arrived(layer, "in", first)[0].reshape(-1, D)
            e_w_in_k[i] = jnp.concatenate([wt[:2048], wt[2464:2976], wt[2048:2432], jnp.zeros((QK_NOPE, D), BF16),
                                           wt[2432:2464], jnp.zeros((pad_q, D), BF16)], axis=0)
            z = _matmul(h, e_w_in_k[i], "nt", BF16, f"w_in_{tag}", tn=1024)
            eq_g, ek_g, ev_g, eout_g = arrived(layer, "rest", z)
            e_w_q_k[i] = heads_to_cols(eq_g)
            e_w_kv_k[i] = jnp.concatenate([heads_to_cols(ek_g), heads_to_cols(ev_g)], axis=-1)
            e_w_out[i] = eout_g.reshape(-1, D)
            u, qn, kvn = _even_pre(z, row(even_q_norm_g[i]), row(even_kv_norm_g[i]), f"even_pre_{tag}")
            cw = jnp.pad(sc_conv_w[i], ((0, KP3 - SC_KERNEL), (0, 0)))
            cv = _conv_fwd(u, cw, row(scb[i]), SC_KERNEL, f"conv_{tag}")
            q, k, v, kT3, vT3 = _qkv_fwd_t(qn, kvn, z, tabs, e_w_q_k[i], e_w_kv_k[i], f"qkv_{tag}")
            o, lse = _attn_fwd_t(q, k, vT3, f"attn_{tag}")
            ycat = _even_post(z, cv, o, f"even_post_{tag}")
            y = _matmul(ycat, e_w_out[i], "nn", F32, f"w_out_{tag}", tn=1024)
            saved.append(dict(x=xs, h=h, z=z, u=u, qn=qn, kvn=kvn, cw=cw, cv=cv, q=q, k=k, v=v, kT3=kT3, o=o, lse=lse,
                              ycat=ycat, y=y))
        else:
            owin_g, oout_g = arrived(layer, "all", first)
            o_w_in[i], o_w_out[i] = heads_to_cols(owin_g), oout_g.reshape(-1, D)
            z = _matmul(h, o_w_in[i], "nn", BF16, f"w_in_{tag}", tn=1024)
            u = _odd_pre(z, f"odd_pre_{tag}")
            cw = jnp.pad(o_conv_w[i], ((0, KP31 - CONF_KERNEL), (0, 0)))
            cv = _conv_fwd(u, cw, row(o_conv_b[i]), CONF_KERNEL, f"conv_{tag}")
            yin = _odd_post(cv, z, row(o_ln_g[i]), row(o_ln_b[i]), f"odd_post_{tag}")
            y = _matmul(yin, o_w_out[i], "nn", F32, f"w_out_{tag}", tn=1024)
            saved.append(dict(x=xs, h=h, z=z, u=u, cw=cw, cv=cv, yin=yin, y=y))
        if layer < L - 1:
            xs, h = _post_pre_norm(xs, y, row(post_norm_g[layer]), row(gate[layer]), row(pre_norm_g[layer + 1]),
                                   row(scale[layer + 1]), row(shift[layer + 1]), f"post_pre_norm_{tag}")

    dx, dy, err_sq, dgate, g_post_last = _loss_post_norm_bwd(xs, y, row(post_norm_g[L - 1]), row(gate[L - 1]), target,
                                                             "loss_post_norm_bwd")
    loss = lax.psum(_scaled_total(err_sq, 0.5 / D, "loss_total")[0, 0], MESH_AXES)

    g_pre, g_post, dmod = [None] * L, [None] * L, [None] * L
    g_e_w_in, g_e_w_uq, g_e_w_ukv, g_e_w_out = [None] * NE, [None] * NE, [None] * NE, [None] * NE
    g_scw, g_scb, g_qg, g_kvg = [None] * NE, [None] * NE, [None] * NE, [None] * NE
    g_o_w_in, g_o_w_out, g_ocw, g_ocb, g_olg, g_olb = ([None] * NO for _ in range(6))
    sm_w = [even_sc_conv_w, odd_conv_w, odd_conv_b, odd_ln_g, odd_ln_b]
    sm_rows = _pack(sm_w, F32, SUBLANES).shape[0]

    def small_slab():
        full = [_scatter_cols(jnp.stack(g_scw), even_sc_conv_w.shape[-1]),
                _scatter_cols(jnp.stack(g_ocw), odd_conv_w.shape[-1]),
                _scatter_cols(jnp.concatenate(g_ocb, 0), odd_conv_b.shape[-1]),
                _scatter_cols(jnp.concatenate(g_olg, 0), odd_ln_g.shape[-1]),
                _scatter_cols(jnp.concatenate(g_olb, 0), odd_ln_b.shape[-1])]
        flat = jnp.concatenate([g.reshape(N_DEV, -1) for g in full], axis=1)
        return jnp.pad(flat, ((0, 0), (0, sm_rows * PACK_COLS - flat.shape[1]))).reshape(N_DEV, sm_rows, PACK_COLS)

    scatters = []
    bw_token = jnp.zeros((SUBLANES, LANES), F32)

    def start_scatter(tag, names, parts):
        own = [lax.dynamic_slice_in_dim(g, me, 1, axis=0) for g in parts]
        handle, token = _exchange_start([g.astype(BF16) for g in parts], True, f"scatter_grads_start_{tag}")
        scatters.append((tag, names, handle, own))
        return token

    for layer in reversed(range(L)):
        i = layer // 2
        tag = f"l{layer}"
        sv = saved[layer]
        if layer == L - 1:
            g_post[layer] = g_post_last
        if layer % 2 == 0:
            dyc = _matmul(dy, e_w_out[i], "nt", F32, f"d_ycat_{tag}", tn=1024)
            g_e_w_out[i] = _matmul(sv["ycat"], dy, "tn", F32, f"g_w_out_{tag}", tn=1024).reshape(N_DEV, -1, D)
            if layer == 0:
                bw_token = start_scatter("l0_out", [("even_w_out", i)], [g_e_w_out[i]])
            dab, dag, dbg, dcv, do, delta = _even_bwd_gates(dyc, sv["z"], sv["cv"], sv["o"], f"even_gates_bwd_{tag}")
            du, dcw, g_scb[i] = _conv_bwd(dcv, sv["u"], sv["cw"] + bw_token[0, 0], SC_KERNEL, f"conv_bwd_{tag}")
            g_scw[i] = dcw[:SC_KERNEL]
            dq, dk, dv = _attn_bwd_t(sv["q"], sv["k"], sv["v"], sv["kT3"], do, sv["lse"], delta, f"attn_bwd_{tag}")
            (dqp, dkvp, dcq, dckv, dkr, g_qg[i], g_kvg[i]) = _qkv_bwd(
                dq, dk, dv, sv["z"], tabs, e_w_q_k[i], e_w_kv_k[i],
                row(even_q_norm_g[i]), row(even_kv_norm_g[i]), f"qkv_bwd_{tag}")
            gq = _matmul(sv["qn"], dqp, "tn", F32, f"g_w_uq_{tag}", tn=1024)
            gkv = _matmul(sv["kvn"], dkvp, "tn", F32, f"g_w_ukv_{tag}")
            g_e_w_uq[i] = jnp.moveaxis(gq.reshape(QL, HEADS, HEAD_PAD)[..., :QK_NOPE + QK_ROPE], 1, 0)
            g_e_w_ukv[i] = jnp.moveaxis(jnp.concatenate(
                [gkv[:, :HEADS * HEAD_PAD].reshape(KVL, HEADS, HEAD_PAD)[..., :QK_NOPE],
                 gkv[:, HEADS * HEAD_PAD:].reshape(KVL, HEADS, V_HEAD)], axis=-1), 1, 0)
            dz = _even_dz(dab, du, sv["z"], dag, dbg, dcq, dckv, dkr, f"even_dz_{tag}")
            gt = _matmul(dz, sv["h"], "tn", F32, f"g_w_in_{tag}", tn=1024)
            g_e_w_in[i] = jnp.concatenate([gt[:2048], gt[2560:2944], gt[2944 + QK_NOPE:2944 + QK_NOPE + QK_ROPE],
                                           gt[2048:2560]], axis=0).reshape(N_DEV, -1, D)
            names = [("even_w_in", i), ("even_w_uq", i), ("even_w_ukv", i)]
            parts = [g_e_w_in[i], g_e_w_uq[i], g_e_w_ukv[i]]
            if layer == 0:
                names, parts = names + [("small", 0)], parts + [small_slab()]
            else:
                names, parts = names + [("even_w_out", i)], parts + [g_e_w_out[i]]
            bw_token = start_scatter(tag, names, parts)
            dh = _matmul(dz, e_w_in_k[i], "nn", F32, f"d_h_{tag}", tn=1024)
        else:
            dyi = _matmul(dy, o_w_out[i], "nt", F32, f"d_yin_{tag}", tn=1024)
            g_o_w_out[i] = _matmul(sv["yin"], dy, "tn", F32, f"g_w_out_{tag}", tn=1024).reshape(N_DEV, -1, D)
            dcv, dsg, g_olg[i], g_olb[i] = _odd_bwd_norm(dyi, sv["cv"], sv["z"], row(o_ln_g[i]), row(o_ln_b[i]),
                                                         f"odd_norm_bwd_{tag}")
            du, dcw, g_ocb[i] = _conv_bwd(dcv, sv["u"], sv["cw"], CONF_KERNEL, f"conv_bwd_{tag}")
            g_ocw[i] = dcw[:CONF_KERNEL]
            dz = _odd_dz(du, sv["z"], dsg, f"odd_dz_{tag}")
            g_o_w_in[i] = jnp.moveaxis(_matmul(sv["h"], dz, "tn", F32, f"g_w_in_{tag}", tn=1024)
                                       .reshape(D, N_DEV, -1), 1, 0)
            bw_token = start_scatter(tag, [("odd_w_in", i), ("odd_w_out", i)], [g_o_w_in[i], g_o_w_out[i]])
            dh = _matmul(dz, o_w_in[i], "nt", F32, f"d_h_{tag}", tn=1024)
        g_row = row(pre_norm_g[layer]) + bw_token[0, 0]
        if layer > 0:
            (dx, dy, dshift, dscale, g_pre[layer], dgate_prev, g_post[layer - 1]) = _pre_post_norm_bwd(
                dh, sv["x"], dx, g_row, row(scale[layer]), saved[layer - 1]["y"], row(post_norm_g[layer - 1]),
                row(gate[layer - 1]), f"pre_post_norm_bwd_{tag}")
        else:
            dx, dshift, dscale, g_pre[layer] = _pre_norm_bwd(dh, sv["x"], dx, g_row, row(scale[layer]),
                                                             f"pre_norm_bwd_{tag}")
            dgate_prev = None
        dmod[layer] = jnp.concatenate([dshift, dscale, dgate], axis=-1)
        dgate = dgate_prev
    grad_x = dx.reshape(1, S, D)

    rep_g = [jnp.concatenate(dmod, 0), jnp.concatenate(g_pre, 0), jnp.concatenate(g_post, 0),
             jnp.stack(g_scb), jnp.stack(g_qg), jnp.stack(g_kvg)]
    rep_w = [ada_b, pre_norm_g, post_norm_g, even_sc_conv_b, even_q_norm_g, even_kv_norm_g]
    rep_m = [m_ada_b, m_pre_norm_g, m_post_norm_g, m_even_sc_conv_b, m_even_q_norm_g, m_even_kv_norm_g]
    rep_v = [v_ada_b, v_pre_norm_g, v_post_norm_g, v_even_sc_conv_b, v_even_q_norm_g, v_even_kv_norm_g]
    rep_shapes = [w.shape for w in rep_w]
    rep_all = _exchange([_pack(rep_g, F32, SUBLANES)], False, "gather_small_grads")[0]
    rep_out = _adamw(rep_all, _pack(rep_w, F32, SUBLANES), _pack(rep_m, F32, SUBLANES), _pack(rep_v, F32, SUBLANES),
                     "adamw_replicated")
    rep_res = [_unpack(o.reshape(-1), rep_shapes) for o in rep_out]

    dmod_all = rep_all.reshape(N_DEV, -1)[:, :L * 3 * D].reshape(N_DEV, L, 3 * D)
    dmod_cols = jnp.moveaxis(lax.dynamic_slice_in_dim(dmod_all, me * n_ada, n_ada, axis=2), 0, 1)
    g_ada_w = _ada_bwd(c_all.T, dmod_cols)
    ada_out = _adamw(g_ada_w.reshape(1, -1, PACK_COLS), ada_w.reshape(-1, PACK_COLS),
                     m_ada_w.reshape(-1, PACK_COLS), v_ada_w.reshape(-1, PACK_COLS), "adamw_ada_w")
    ada_res = [o.reshape(ada_w.shape) for o in ada_out]

    sm_m = [m_even_sc_conv_w, m_odd_conv_w, m_odd_conv_b, m_odd_ln_g, m_odd_ln_b]
    sm_v = [v_even_sc_conv_w, v_odd_conv_w, v_odd_conv_b, v_odd_ln_g, v_odd_ln_b]
    sm_shapes = [w.shape for w in sm_w]
    state = {"even_w_in": (even_w_in, m_even_w_in, v_even_w_in), "even_w_uq": (even_w_uq, m_even_w_uq, v_even_w_uq),
             "even_w_ukv": (even_w_ukv, m_even_w_ukv, v_even_w_ukv), "even_w_out": (even_w_out, m_even_w_out, v_even_w_out),
             "odd_w_in": (odd_w_in, m_odd_w_in, v_odd_w_in), "odd_w_out": (odd_w_out, m_odd_w_out, v_odd_w_out)}
    big_res = {name: [[None] * len(state[name][0]) for _ in range(4)] for name in state}
    after = [bw_token, grad_x, rep_out[0], ada_out[0]]
    sm_res = None
    for tag, names, handle, own in scatters:
        _, landed = _exchange_wait(handle, True, after, f"scatter_grads_wait_{tag}")
        after = []
        for a, (name, i) in enumerate(names):
            if name == "small":
                sm_out = _adamw([own[a], landed[a]], _pack(sm_w, F32, SUBLANES), _pack(sm_m, F32, SUBLANES),
                                _pack(sm_v, F32, SUBLANES), "adamw_small_sharded")
                sm_res = [_unpack(o.reshape(-1), sm_shapes) for o in sm_out]
                continue
            transposed = name == "even_w_in"
            wmv = [t[i].T if transposed else t[i] for t in state[name]]
            res = _adamw([own[a], landed[a]], *wmv, f"adamw_{name}_{i}")
            for kind in range(4):
                big_res[name][kind][i] = res[kind].T if transposed else res[kind]
            after += [big_res[name][kind][i] for kind in range(4)]
    sh_res = [dict(zip(["even_sc_conv_w", "odd_conv_w", "odd_conv_b", "odd_ln_g", "odd_ln_b"], sm_res[kind]))
              for kind in range(4)]
    for name in state:
        for kind in range(4):
            sh_res[kind][name] = jnp.stack(big_res[name][kind])

    order = ["ada_w", "ada_b", "pre_norm_g", "post_norm_g", "even_w_in", "even_sc_conv_w", "even_sc_conv_b",
             "even_q_norm_g", "even_kv_norm_g", "even_w_uq", "even_w_ukv", "even_w_out", "odd_w_in", "odd_conv_w",
             "odd_conv_b", "odd_ln_g", "odd_ln_b", "odd_w_out"]
    rep_names = ["ada_b", "pre_norm_g", "post_norm_g", "even_sc_conv_b", "even_q_norm_g", "even_kv_norm_g"]
    outs = [loss, grad_x]
    for kind in range(4):
        for name in order:
            if name == "ada_w":
                outs.append(ada_res[kind])
            elif name in rep_names:
                outs.append(rep_res[kind][rep_names.index(name)])
            else:
                outs.append(sh_res[kind][name])
    return tuple(outs)
```

```python
import functools
import math

import jax
import jax.numpy as jnp
from jax import lax
from jax.experimental import pallas as pl
from jax.experimental.pallas import tpu as pltpu

F32 = jnp.float32
BF16 = jnp.bfloat16
MESH_AXES = ("x", "y", "c")
N_DEV = 8
EPS = 1e-6
CHUNK = 64
HEADS = 8
QK_NOPE = 64
QK_ROPE = 32
V_HEAD = 64
HEAD_PAD = 128
ROPE_THETA = 10000.0
SC_KERNEL = 3
CONF_KERNEL = 31
LANES = 128
SUBLANES = 8
PACK_COLS = 1024
VMEM_LIMIT = 48 * 1024 * 1024
NEG = -1e30

ADAM_LR = 0.001
ADAM_B1 = 0.9
ADAM_B2 = 0.999
ADAM_EPS = 1e-08
ADAM_WD = 0.01
ADAM_STEP = 10


def _cparams():
    return pltpu.CompilerParams(vmem_limit_bytes=VMEM_LIMIT)


def _sigmoid(x):
    return 1.0 / (1.0 + jnp.exp(-x))


def _f32(ref):
    return ref[...].astype(F32)


def _silu(x):
    return x * _sigmoid(x)


def _dsilu(x):
    s = _sigmoid(x)
    return s * (1.0 + x * (1.0 - s))


def _rows(T, width, cb=0):
    return pl.BlockSpec((T, width), lambda i: (i, cb))


def _const(shape):
    nd = len(shape)
    return pl.BlockSpec(shape, lambda i: (0,) * nd)


def _row_tile(S):
    return min(256, S)


def _exchange(srcs, scatter, name):
    n = len(srcs)
    shapes = [tuple(s.shape[1:]) if scatter else tuple(s.shape) for s in srcs]

    def body(*refs):
        src_refs, out_refs = refs[:n], refs[n:2 * n]
        send_sems, recv_sems, local_sems = refs[2 * n:]
        x, y, c = lax.axis_index("x"), lax.axis_index("y"), lax.axis_index("c")
        me = 4 * x + 2 * y + c
        owns, copies = [], []
        for a in range(n):
            def piece(d, a=a):
                return src_refs[a].at[d] if scatter else src_refs[a]

            own = pltpu.make_async_copy(piece(me), out_refs[a].at[me], local_sems.at[a])
            own.start()
            owns.append(own)
            for k in range(1, N_DEV):
                px, py, pc = x ^ ((k >> 2) & 1), y ^ ((k >> 1) & 1), c ^ (k & 1)
                peer = 4 * px + 2 * py + pc
                sem = a * (N_DEV - 1) + k - 1
                cp = pltpu.make_async_remote_copy(
                    src_ref=piece(peer), dst_ref=out_refs[a].at[me],
                    send_sem=send_sems.at[sem], recv_sem=recv_sems.at[sem],
                    device_id=(px, py, pc), device_id_type=pl.DeviceIdType.MESH)
                cp.start()
                arrival = pltpu.make_async_remote_copy(
                    src_ref=piece(peer), dst_ref=out_refs[a].at[peer],
                    send_sem=send_sems.at[sem], recv_sem=recv_sems.at[sem],
                    device_id=(x, y, c), device_id_type=pl.DeviceIdType.MESH)
                copies.append((cp, arrival))
        for _, arrival in copies:
            arrival.wait_recv()
        for cp, _ in copies:
            cp.wait_send()
        for own in owns:
            own.wait()

    return pl.pallas_call(
        body, name=name,
        out_shape=tuple(jax.ShapeDtypeStruct((N_DEV,) + shp, s.dtype) for shp, s in zip(shapes, srcs)),
        in_specs=[pl.BlockSpec(memory_space=pl.ANY)] * n,
        out_specs=tuple(pl.BlockSpec(memory_space=pl.ANY) for _ in range(n)),
        scratch_shapes=[pltpu.SemaphoreType.DMA((n * (N_DEV - 1),)),
                        pltpu.SemaphoreType.DMA((n * (N_DEV - 1),)),
                        pltpu.SemaphoreType.DMA((n,))],
    )(*srcs)


_HBM = pl.BlockSpec(memory_space=pltpu.HBM)
_SEM = pl.BlockSpec(memory_space=pltpu.SEMAPHORE)


def _peer(k):
    x, y, c = lax.axis_index("x"), lax.axis_index("y"), lax.axis_index("c")
    return x ^ ((k >> 2) & 1), y ^ ((k >> 1) & 1), c ^ (k & 1)


def _exchange_start(srcs, scatter, name):
    n = len(srcs)
    shapes = [tuple(s.shape[1:]) if scatter else tuple(s.shape) for s in srcs]
    slots = N_DEV - 1 if scatter else N_DEV
    lands = [lax.empty((slots,) + shp, s.dtype) for shp, s in zip(shapes, srcs)]
    if not scatter:
        here = 4 * lax.axis_index("x") + 2 * lax.axis_index("y") + lax.axis_index("c")
        lands = [lax.dynamic_update_index_in_dim(l, s, here, 0) for l, s in zip(lands, srcs)]

    def body(*refs):
        src_refs, land_refs = refs[:n], refs[n:2 * n]
        send_sems, recv_sems = refs[2 * n], refs[2 * n + 1]
        token = refs[4 * n + 2]
        me = 4 * lax.axis_index("x") + 2 * lax.axis_index("y") + lax.axis_index("c")
        for a in range(n):
            for k in range(1, N_DEV):
                px, py, pc = _peer(k)
                peer = 4 * px + 2 * py + pc
                pltpu.make_async_remote_copy(
                    src_ref=src_refs[a].at[peer] if scatter else src_refs[a],
                    dst_ref=land_refs[a].at[k - 1] if scatter else land_refs[a].at[me],
                    send_sem=send_sems.at[a * (N_DEV - 1) + k - 1], recv_sem=recv_sems.at[a * (N_DEV - 1) + k - 1],
                    device_id=(px, py, pc), device_id_type=pl.DeviceIdType.MESH).start()
        token[...] = jnp.zeros_like(token)

    hbm = lambda arrs: [pltpu.HBM(a.shape, a.dtype) for a in arrs]
    out = pl.pallas_call(
        body, name=name,
        out_shape=(pltpu.SemaphoreType.DMA((n * (N_DEV - 1),)), pltpu.SemaphoreType.DMA((n * (N_DEV - 1),)),
                   *hbm(srcs), *hbm(lands), jax.ShapeDtypeStruct((SUBLANES, LANES), F32)),
        in_specs=[_HBM] * (2 * n),
        out_specs=(_SEM, _SEM, *([_HBM] * (2 * n)), pl.BlockSpec(memory_space=pltpu.VMEM)),
        input_output_aliases={a: 2 + a for a in range(2 * n)},
        compiler_params=pltpu.CompilerParams(has_side_effects=pltpu.SideEffectType.DATAFLOW_SIDE_EFFECTING),
    )(*[pltpu.with_memory_space_constraint(s, pltpu.HBM) for s in srcs],
      *[pltpu.with_memory_space_constraint(l, pltpu.HBM) for l in lands])
    return (out[0], out[1], list(out[2:2 + n]), list(out[2 + n:2 + 2 * n])), out[2 + 2 * n]


def _exchange_wait(handle, scatter, after, name):
    send_sems, recv_sems, srcs, lands = handle
    n = len(srcs)
    after = list(after) if isinstance(after, (list, tuple)) else [after]

    def body(*refs):
        src_refs, land_refs = refs[:n], refs[n:2 * n]
        send_sems, recv_sems = refs[2 * n], refs[2 * n + 1]
        for a in range(n):
            for k in range(1, N_DEV):
                px, py, pc = _peer(k)
                peer = 4 * px + 2 * py + pc
                cp = pltpu.make_async_remote_copy(
                    src_ref=src_refs[a].at[peer] if scatter else src_refs[a],
                    dst_ref=land_refs[a].at[k - 1] if scatter else land_refs[a].at[peer],
                    send_sem=send_sems.at[a * (N_DEV - 1) + k - 1], recv_sem=recv_sems.at[a * (N_DEV - 1) + k - 1],
                    device_id=(px, py, pc), device_id_type=pl.DeviceIdType.MESH)
                cp.wait_send()
                cp.wait_recv()

    out = pl.pallas_call(
        body, name=name,
        out_shape=tuple(pltpu.HBM(a.shape, a.dtype) for a in srcs + lands),
        in_specs=[_HBM] * (2 * n) + [_SEM, _SEM] + [pl.BlockSpec(memory_space=pl.ANY)] * len(after),
        out_specs=tuple([_HBM] * (2 * n)),
        input_output_aliases={a: a for a in range(2 * n)},
        compiler_params=pltpu.CompilerParams(has_side_effects=pltpu.SideEffectType.DATAFLOW_SIDE_EFFECTING),
    )(*srcs, *lands, send_sems, recv_sems, *after)
    return list(out[:n]), list(out[n:])


def _pack(parts, dtype, row_mult):
    flat = jnp.concatenate([p.reshape(-1).astype(dtype) for p in parts])
    n = flat.shape[0]
    rows = -(-n // PACK_COLS)
    rows = -(-rows // row_mult) * row_mult
    flat = jnp.pad(flat, (0, rows * PACK_COLS - n))
    return flat.reshape(rows, PACK_COLS)


def _unpack(flat, shapes):
    out, off = [], 0
    for shp in shapes:
        n = math.prod(shp)
        out.append(flat[..., off:off + n].reshape(flat.shape[:-1] + tuple(shp)))
        off += n
    return out


_DIMS = {"nn": (((1,), (0,)), ((), ())), "nt": (((1,), (1,)), ((), ())), "tn": (((0,), (0,)), ((), ()))}


def _matmul(a, b, mode, out_dtype, name, tm=512, tn=512, tk=None, split_n=False):
    if mode == "nn":
        (M, K), (_, N) = a.shape, b.shape
    elif mode == "nt":
        (M, K), (N, _) = a.shape, b.shape
    else:
        (K, M), (_, N) = a.shape, b.shape
    tm, tn = min(tm, M), min(tn, N)
    tk = K if tk is None else min(tk, K)
    nk = K // tk
    assert M % tm == 0 and N % tn == 0 and K % tk == 0, (name, a.shape, b.shape)

    def body(a_ref, b_ref, o_ref, *scratch):
        p = lax.dot_general(a_ref[...].astype(BF16), b_ref[...].astype(BF16), _DIMS[mode],
                            preferred_element_type=F32)
        if split_n:
            o_ref[0] = p.astype(out_dtype)
        elif nk == 1:
            o_ref[...] = p.astype(out_dtype)
        else:
            acc = scratch[0]
            k = pl.program_id(2)

            @pl.when(k == 0)
            def _():
                acc[...] = p

            @pl.when(k > 0)
            def _():
                acc[...] += p

            @pl.when(k == nk - 1)
            def _():
                o_ref[...] = acc[...].astype(out_dtype)

    a_spec = (pl.BlockSpec((tk, tm), lambda i, j, k: (k, i)) if mode == "tn"
              else pl.BlockSpec((tm, tk), lambda i, j, k: (i, k)))
    b_spec = (pl.BlockSpec((tn, tk), lambda i, j, k: (j, k)) if mode == "nt"
              else pl.BlockSpec((tk, tn), lambda i, j, k: (k, j)))
    return pl.pallas_call(
        body, name=name, grid=(M // tm, N // tn, nk),
        out_shape=jax.ShapeDtypeStruct((N // tn, M, tn) if split_n else (M, N), out_dtype),
        in_specs=[a_spec, b_spec],
        out_specs=(pl.BlockSpec((1, tm, tn), lambda i, j, k: (j, i, 0)) if split_n
                   else pl.BlockSpec((tm, tn), lambda i, j, k: (i, j))),
        scratch_shapes=[pltpu.VMEM((tm, tn), F32)] if nk > 1 else [],
        compiler_params=_cparams(),
    )(a, b)


def _ada_fwd(c_all, ada_w, ada_b_cols):
    L, D, n = ada_w.shape

    def body(c_ref, w_ref, b_ref, o_ref):
        act = _silu(c_ref[...]).astype(BF16)
        o_ref[0] = jnp.dot(act, w_ref[0].astype(BF16), preferred_element_type=F32) + b_ref[0]

    return pl.pallas_call(
        body, name="ada_fwd", grid=(L,),
        out_shape=jax.ShapeDtypeStruct((L, N_DEV, n), F32),
        in_specs=[pl.BlockSpec((N_DEV, D), lambda l: (0, 0)),
                  pl.BlockSpec((1, D, n), lambda l: (l, 0, 0)),
                  pl.BlockSpec((1, 1, n), lambda l: (l, 0, 0))],
        out_specs=pl.BlockSpec((1, N_DEV, n), lambda l: (l, 0, 0)),
        compiler_params=_cparams(),
    )(c_all, ada_w, ada_b_cols)


def _ada_bwd(c_all_t, dmod_cols):
    D = c_all_t.shape[0]
    L, _, n = dmod_cols.shape

    def body(c_ref, d_ref, o_ref):
        act = _silu(c_ref[...])
        dm = d_ref[0]
        acc = act[:, 0:1] * dm[0:1, :]
        for b in range(1, N_DEV):
            acc = acc + act[:, b:b + 1] * dm[b:b + 1, :]
        o_ref[0] = acc

    return pl.pallas_call(
        body, name="ada_bwd", grid=(L,),
        out_shape=jax.ShapeDtypeStruct((L, D, n), F32),
        in_specs=[pl.BlockSpec((D, N_DEV), lambda l: (0, 0)),
                  pl.BlockSpec((1, N_DEV, n), lambda l: (l, 0, 0))],
        out_specs=pl.BlockSpec((1, D, n), lambda l: (l, 0, 0)),
        compiler_params=_cparams(),
    )(c_all_t, dmod_cols)


def _rope_tables(pos_col, inv_lane):
    S = pos_col.shape[0]
    T = _row_tile(S)
    half = QK_ROPE // 2

    def body(p_ref, f_ref, c_ref, up_ref, dn_ref):
        ang = p_ref[...] * f_ref[...]
        lane = lax.broadcasted_iota(jnp.int32, ang.shape, 1)
        first = (lane >= QK_NOPE) & (lane < QK_NOPE + half)
        second = (lane >= QK_NOPE + half) & (lane < QK_NOPE + QK_ROPE)
        cs, sn = jnp.cos(ang), jnp.sin(ang)
        c_ref[...] = jnp.where(first | second, cs, 1.0)
        up_ref[...] = jnp.where(first, -sn, 0.0)
        dn_ref[...] = jnp.where(second, sn, 0.0)

    tab = jax.ShapeDtypeStruct((S, HEAD_PAD), F32)
    return pl.pallas_call(
        body, name="rope_tables", grid=(S // T,),
        out_shape=(tab, tab, tab),
        in_specs=[_rows(T, 1), _const((1, HEAD_PAD))],
        out_specs=(_rows(T, HEAD_PAD),) * 3,
        compiler_params=_cparams(),
    )(pos_col, inv_lane)


def _rope(blk, ct, ut, dt):
    half = QK_ROPE // 2
    up = pltpu.roll(blk, HEAD_PAD - half, 1)
    dn = pltpu.roll(blk, half, 1)
    return blk * ct + up * ut + dn * dt


def _rope_t(d, ct, ut, dt):
    half = QK_ROPE // 2
    return d * ct + pltpu.roll(d * ut, half, 1) + pltpu.roll(d * dt, HEAD_PAD - half, 1)


def _pre_norm(x, g, scale, shift, name):
    S, D = x.shape
    T = _row_tile(S)

    def body(x_ref, g_ref, sc_ref, sh_ref, h_ref):
        xv = x_ref[...]
        rstd = lax.rsqrt(jnp.mean(xv * xv, axis=-1, keepdims=True) + EPS)
        h_ref[...] = ((xv * rstd) * g_ref[...] * (1.0 + sc_ref[...]) + sh_ref[...]).astype(BF16)

    return pl.pallas_call(
        body, name=name, grid=(S // T,),
        out_shape=jax.ShapeDtypeStruct((S, D), BF16),
        in_specs=[_rows(T, D), _const((1, D)), _const((1, D)), _const((1, D))],
        out_specs=_rows(T, D), compiler_params=_cparams(),
    )(x, g, scale, shift)


def _post_norm(x, y, g, gate, name):
    S, D = x.shape
    T = _row_tile(S)

    def body(x_ref, y_ref, g_ref, gt_ref, o_ref):
        yv = y_ref[...]
        rstd = lax.rsqrt(jnp.mean(yv * yv, axis=-1, keepdims=True) + EPS)
        o_ref[...] = x_ref[...] + gt_ref[...] * ((yv * rstd) * g_ref[...])

    return pl.pallas_call(
        body, name=name, grid=(S // T,),
        out_shape=jax.ShapeDtypeStruct((S, D), F32),
        in_specs=[_rows(T, D), _rows(T, D), _const((1, D)), _const((1, D))],
        out_specs=_rows(T, D), compiler_params=_cparams(),
    )(x, y, g, gate)


def _fold8(v):
    T, C = v.shape
    return v.reshape(T // SUBLANES, SUBLANES, C).sum(axis=0)


def _col_sums(n_sums, body_fn, ins, in_specs, outs, out_specs, S, T, widths, name):
    n_in, n_out = len(ins), len(outs)
    nt = S // T

    def body(*refs):
        in_refs = refs[:n_in]
        out_refs = refs[n_in:n_in + n_out]
        sum_refs = refs[n_in + n_out:n_in + n_out + n_sums]
        accs = refs[n_in + n_out + n_sums:]
        i = pl.program_id(0)
        terms = body_fn(in_refs, out_refs)

        @pl.when(i == 0)
        def _():
            for acc, t in zip(accs, terms):
                acc[...] = _fold8(t)

        @pl.when(i > 0)
        def _():
            for acc, t in zip(accs, terms):
                acc[...] += _fold8(t)

        @pl.when(i == nt - 1)
        def _():
            for acc, s_ref in zip(accs, sum_refs):
                s_ref[...] = jnp.sum(acc[...], axis=0, keepdims=True)

    return pl.pallas_call(
        body, name=name, grid=(nt,),
        out_shape=tuple(outs) + tuple(jax.ShapeDtypeStruct((1, w), F32) for w in widths),
        in_specs=in_specs,
        out_specs=tuple(out_specs) + tuple(_const((1, w)) for w in widths),
        scratch_shapes=[pltpu.VMEM((SUBLANES, w), F32) for w in widths],
        compiler_params=_cparams(),
    )(*ins)


def _post_norm_bwd(dxo, y, g, gate, name):
    S, D = y.shape
    T = _row_tile(S)

    def fn(ins, outs):
        dxo_ref, y_ref, g_ref, gt_ref = ins
        yv, dv = y_ref[...], dxo_ref[...]
        rstd = lax.rsqrt(jnp.mean(yv * yv, axis=-1, keepdims=True) + EPS)
        yh = yv * rstd
        dn = dv * gt_ref[...]
        dyh = dn * g_ref[...]
        outs[0][...] = (rstd * (dyh - yh * jnp.mean(dyh * yh, axis=-1, keepdims=True))).astype(BF16)
        return [dv * (yh * g_ref[...]), dn * yh]

    return _col_sums(2, fn, [dxo, y, g, gate],
                     [_rows(T, D), _rows(T, D), _const((1, D)), _const((1, D))],
                     [jax.ShapeDtypeStruct((S, D), BF16)], [_rows(T, D)], S, T, [D, D], name)


def _pre_norm_bwd(dh, x, dxo, g, scale, name):
    S, D = x.shape
    T = _row_tile(S)

    def fn(ins, outs):
        dh_ref, x_ref, dxo_ref, g_ref, sc_ref = ins
        xv, dv = x_ref[...], dh_ref[...]
        rstd = lax.rsqrt(jnp.mean(xv * xv, axis=-1, keepdims=True) + EPS)
        xh = xv * rstd
        dr = dv * (1.0 + sc_ref[...])
        dxh = dr * g_ref[...]
        outs[0][...] = dxo_ref[...] + rstd * (dxh - xh * jnp.mean(dxh * xh, axis=-1, keepdims=True))
        return [dv, dv * (xh * g_ref[...]), dr * xh]

    return _col_sums(3, fn, [dh, x, dxo, g, scale],
                     [_rows(T, D), _rows(T, D), _rows(T, D), _const((1, D)), _const((1, D))],
                     [jax.ShapeDtypeStruct((S, D), F32)], [_rows(T, D)], S, T, [D, D, D], name)


def _post_pre_norm(x, y, g_post, gate, g_pre, scale, shift, name):
    S, D = x.shape
    T = _row_tile(S)

    def body(x_ref, y_ref, gp_ref, gt_ref, g_ref, sc_ref, sh_ref, xn_ref, h_ref):
        yv = y_ref[...]
        rstd_y = lax.rsqrt(jnp.mean(yv * yv, axis=-1, keepdims=True) + EPS)
        xn = x_ref[...] + gt_ref[...] * ((yv * rstd_y) * gp_ref[...])
        xn_ref[...] = xn
        rstd = lax.rsqrt(jnp.mean(xn * xn, axis=-1, keepdims=True) + EPS)
        h_ref[...] = ((xn * rstd) * g_ref[...] * (1.0 + sc_ref[...]) + sh_ref[...]).astype(BF16)

    return pl.pallas_call(
        body, name=name, grid=(S // T,),
        out_shape=(jax.ShapeDtypeStruct((S, D), F32), jax.ShapeDtypeStruct((S, D), BF16)),
        in_specs=[_rows(T, D), _rows(T, D)] + [_const((1, D))] * 5,
        out_specs=(_rows(T, D), _rows(T, D)), compiler_params=_cparams(),
    )(x, y, g_post, gate, g_pre, scale, shift)


def _pre_post_norm_bwd(dh, x, dxo, g_pre, scale, y_prev, g_post_prev, gate_prev, name):
    S, D = x.shape
    T = _row_tile(S)

    def fn(ins, outs):
        dh_ref, x_ref, dxo_ref, g_ref, sc_ref, y_ref, gp_ref, gt_ref = ins
        xv, dv = x_ref[...], dh_ref[...]
        rstd = lax.rsqrt(jnp.mean(xv * xv, axis=-1, keepdims=True) + EPS)
        xh = xv * rstd
        dr = dv * (1.0 + sc_ref[...])
        dxh = dr * g_ref[...]
        dx = dxo_ref[...] + rstd * (dxh - xh * jnp.mean(dxh * xh, axis=-1, keepdims=True))
        outs[0][...] = dx
        yv = y_ref[...]
        rstd_y = lax.rsqrt(jnp.mean(yv * yv, axis=-1, keepdims=True) + EPS)
        yh = yv * rstd_y
        dn = dx * gt_ref[...]
        dyh = dn * gp_ref[...]
        outs[1][...] = (rstd_y * (dyh - yh * jnp.mean(dyh * yh, axis=-1, keepdims=True))).astype(BF16)
        return [dv, dv * (xh * g_ref[...]), dr * xh, dx * (yh * gp_ref[...]), dn * yh]

    return _col_sums(5, fn, [dh, x, dxo, g_pre, scale, y_prev, g_post_prev, gate_prev],
                     [_rows(T, D), _rows(T, D), _rows(T, D), _const((1, D)), _const((1, D)),
                      _rows(T, D), _const((1, D)), _const((1, D))],
                     [jax.ShapeDtypeStruct((S, D), F32), jax.ShapeDtypeStruct((S, D), BF16)],
                     [_rows(T, D), _rows(T, D)], S, T, [D] * 5, name)


def _loss_post_norm_bwd(x, y, g_post, gate, target, name):
    S, D = x.shape
    T = _row_tile(S)

    def fn(ins, outs):
        x_ref, y_ref, gp_ref, gt_ref, t_ref = ins
        yv = y_ref[...]
        rstd_y = lax.rsqrt(jnp.mean(yv * yv, axis=-1, keepdims=True) + EPS)
        yh = yv * rstd_y
        e = x_ref[...] + gt_ref[...] * (yh * gp_ref[...]) - t_ref[...]
        dx = e * (1.0 / D)
        outs[0][...] = dx
        dn = dx * gt_ref[...]
        dyh = dn * gp_ref[...]
        outs[1][...] = (rstd_y * (dyh - yh * jnp.mean(dyh * yh, axis=-1, keepdims=True))).astype(BF16)
        return [e * e, dx * (yh * gp_ref[...]), dn * yh]

    return _col_sums(3, fn, [x, y, g_post, gate, target],
                     [_rows(T, D), _rows(T, D), _const((1, D)), _const((1, D)), _rows(T, D)],
                     [jax.ShapeDtypeStruct((S, D), F32), jax.ShapeDtypeStruct((S, D), BF16)],
                     [_rows(T, D), _rows(T, D)], S, T, [D] * 3, name)


def _scaled_total(v, coef, name):
    def body(v_ref, o_ref):
        o_ref[...] = jnp.broadcast_to(jnp.sum(v_ref[...], axis=1, keepdims=True) * coef, (1, LANES))

    return pl.pallas_call(body, name=name, out_shape=jax.ShapeDtypeStruct((1, LANES), F32))(v)


def _loss_head(x, target):
    S, D = x.shape
    T = _row_tile(S)
    nt = S // T

    def body(x_ref, t_ref, l_ref, dx_ref, acc):
        i = pl.program_id(0)
        e = x_ref[...] - t_ref[...]
        dx_ref[...] = e * (1.0 / D)
        part = _fold8(e * e)

        @pl.when(i == 0)
        def _():
            acc[...] = part

        @pl.when(i > 0)
        def _():
            acc[...] += part

        @pl.when(i == nt - 1)
        def _():
            tot = jnp.sum(jnp.sum(acc[...], axis=0, keepdims=True), axis=1, keepdims=True)
            l_ref[...] = jnp.broadcast_to(tot * (0.5 / D), (1, LANES))

    return pl.pallas_call(
        body, name="loss_head", grid=(nt,),
        out_shape=(jax.ShapeDtypeStruct((1, LANES), F32), jax.ShapeDtypeStruct((S, D), F32)),
        in_specs=[_rows(T, D), _rows(T, D)],
        out_specs=(_const((1, LANES)), _rows(T, D)),
        scratch_shapes=[pltpu.VMEM((SUBLANES, D), F32)],
        compiler_params=_cparams(),
    )(x, target)


CONV_ROWS = 64


def _conv_halo(K):
    return SUBLANES if K - 1 <= SUBLANES else 32


def _conv_fwd(u, w, b, K, name):
    S, C = u.shape
    KP = w.shape[0]
    T, HB, RS = min(512, S), _conv_halo(K), CONV_ROWS
    ratio = T // HB

    def body(u_ref, h_ref, w_ref, b_ref, o_ref, ext):
        i = pl.program_id(1)
        ext[0:HB, :] = jnp.where(i > 0, h_ref[...], 0.0)
        ext[HB:HB + T, :] = u_ref[...]
        for r0 in range(0, T, RS):
            acc = jnp.broadcast_to(b_ref[...], (RS, LANES))
            for k in range(K):
                off = HB - (K - 1) + k + r0
                acc = acc + w_ref[k:k + 1, :] * ext[off:off + RS, :]
            o_ref[r0:r0 + RS, :] = acc

    return pl.pallas_call(
        body, name=name, grid=(C // LANES, S // T),
        out_shape=jax.ShapeDtypeStruct((S, C), F32),
        in_specs=[pl.BlockSpec((T, LANES), lambda c, i: (i, c)),
                  pl.BlockSpec((HB, LANES), lambda c, i: (jnp.maximum(i * ratio - 1, 0), c)),
                  pl.BlockSpec((KP, LANES), lambda c, i: (0, c)),
                  pl.BlockSpec((1, LANES), lambda c, i: (0, c))],
        out_specs=pl.BlockSpec((T, LANES), lambda c, i: (i, c)),
        scratch_shapes=[pltpu.VMEM((HB + T, LANES), F32)],
        compiler_params=_cparams(),
    )(u, u, w, b)


def _conv_bwd(d, u, w, K, name):
    S, C = u.shape
    KP = w.shape[0]
    T, HB, RS = min(512, S), _conv_halo(K), CONV_ROWS
    ratio = T // HB
    nt = S // T
    last_halo = S // HB - 1

    def body(d_ref, dn_ref, u_ref, up_ref, w_ref, du_ref, dw_ref, db_ref, extd, extu, dws, dbs):
        i = pl.program_id(1)
        extd[0:T, :] = d_ref[...]
        extd[T:T + HB, :] = jnp.where(i < nt - 1, dn_ref[...], 0.0)
        extu[0:HB, :] = jnp.where(i > 0, up_ref[...], 0.0)
        extu[HB:HB + T, :] = u_ref[...]

        @pl.when(i == 0)
        def _():
            dws[...] = jnp.zeros_like(dws)
            dbs[...] = jnp.zeros_like(dbs)

        for r0 in range(0, T, RS):
            acc = jnp.zeros((RS, LANES), F32)
            for k in range(K):
                off = (K - 1 - k) + r0
                acc = acc + w_ref[k:k + 1, :] * extd[off:off + RS, :]
            du_ref[r0:r0 + RS, :] = acc
            dch = d_ref[r0:r0 + RS, :]
            dbs[...] += _fold8(dch)
            for k in range(K):
                off = HB - (K - 1) + k + r0
                dws[k * SUBLANES:(k + 1) * SUBLANES, :] += _fold8(dch * extu[off:off + RS, :])

        @pl.when(i == nt - 1)
        def _():
            dw_ref[...] = jnp.zeros_like(dw_ref)
            for k in range(K):
                dw_ref[k:k + 1, :] = jnp.sum(dws[k * SUBLANES:(k + 1) * SUBLANES, :], axis=0, keepdims=True)
            db_ref[...] = jnp.sum(dbs[...], axis=0, keepdims=True)

    return pl.pallas_call(
        body, name=name, grid=(C // LANES, nt),
        out_shape=(jax.ShapeDtypeStruct((S, C), F32), jax.ShapeDtypeStruct((KP, C), F32),
                   jax.ShapeDtypeStruct((1, C), F32)),
        in_specs=[pl.BlockSpec((T, LANES), lambda c, i: (i, c)),
                  pl.BlockSpec((HB, LANES), lambda c, i: (jnp.minimum((i + 1) * ratio, last_halo), c)),
                  pl.BlockSpec((T, LANES), lambda c, i: (i, c)),
                  pl.BlockSpec((HB, LANES), lambda c, i: (jnp.maximum(i * ratio - 1, 0), c)),
                  pl.BlockSpec((KP, LANES), lambda c, i: (0, c))],
        out_specs=(pl.BlockSpec((T, LANES), lambda c, i: (i, c)),
                   pl.BlockSpec((KP, LANES), lambda c, i: (0, c)),
                   pl.BlockSpec((1, LANES), lambda c, i: (0, c))),
        scratch_shapes=[pltpu.VMEM((T + HB, LANES), F32), pltpu.VMEM((HB + T, LANES), F32),
                        pltpu.VMEM((KP * SUBLANES, LANES), F32), pltpu.VMEM((SUBLANES, LANES), F32)],
        compiler_params=_cparams(),
    )(d, d, u, u, w)


SCW = 512
ZE = 3072
QL = 256
KVL = 128


def _rms_rows(x, g):
    rstd = lax.rsqrt(jnp.mean(x * x, axis=-1, keepdims=True) + EPS)
    return (x * rstd) * g


def _even_pre(z, qg, kvg, name):
    S = z.shape[0]
    T = _row_tile(S)

    def body(ac_ref, ax_ref, cq_ref, ckv_ref, qg_ref, kvg_ref, u_ref, qn_ref, kvn_ref):
        u_ref[...] = _f32(ac_ref) * _f32(ax_ref)
        qn_ref[...] = _rms_rows(_f32(cq_ref), qg_ref[...]).astype(BF16)
        kvn_ref[...] = _rms_rows(_f32(ckv_ref), kvg_ref[...]).astype(BF16)

    return pl.pallas_call(
        body, name=name, grid=(S // T,),
        out_shape=(jax.ShapeDtypeStruct((S, SCW), F32), jax.ShapeDtypeStruct((S, QL), BF16),
                   jax.ShapeDtypeStruct((S, KVL), BF16)),
        in_specs=[_rows(T, SCW, 1), _rows(T, SCW, 2), _rows(T, QL, 10), _rows(T, KVL, 22),
                  _const((1, QL)), _const((1, KVL))],
        out_specs=(_rows(T, SCW), _rows(T, QL), _rows(T, KVL)),
        compiler_params=_cparams(),
    )(z, z, z, z, qg, kvg)


def _qkv_fwd(qn, kvn, z, tabs, w_q, w_kv, name):
    S = qn.shape[0]
    T = _row_tile(S)
    HW = HEADS * HEAD_PAD
    scale = 1.0 / math.sqrt(QK_NOPE + QK_ROPE)

    def body(qn_ref, kvn_ref, kr_ref, ct_ref, ut_ref, dt_ref, wq_ref, wkv_ref, q_ref, k_ref, v_ref):
        ct, ut, dt = ct_ref[...], ut_ref[...], dt_ref[...]
        qa = jnp.dot(qn_ref[...], wq_ref[...], preferred_element_type=F32)
        kva = jnp.dot(kvn_ref[...], wkv_ref[...], preferred_element_type=F32)
        kr = _f32(kr_ref)
        for h in range(HEADS):
            sl = slice(h * HEAD_PAD, (h + 1) * HEAD_PAD)
            q_ref[:, sl] = (_rope(qa[:, sl], ct, ut, dt) * scale).astype(BF16)
            k_ref[:, sl] = _rope(kva[:, sl] + kr, ct, ut, dt).astype(BF16)
        v_ref[...] = kva[:, HW:].astype(BF16)

    return pl.pallas_call(
        body, name=name, grid=(S // T,),
        out_shape=(jax.ShapeDtypeStruct((S, HW), BF16), jax.ShapeDtypeStruct((S, HW), BF16),
                   jax.ShapeDtypeStruct((S, HEADS * V_HEAD), BF16)),
        in_specs=[_rows(T, QL), _rows(T, KVL), _rows(T, HEAD_PAD, 23),
                  _rows(T, HEAD_PAD), _rows(T, HEAD_PAD), _rows(T, HEAD_PAD),
                  _const(w_q.shape), _const(w_kv.shape)],
        out_specs=(_rows(T, HW), _rows(T, HW), _rows(T, HEADS * V_HEAD)),
        compiler_params=_cparams(),
    )(qn, kvn, z, *tabs, w_q, w_kv)


def _attn_tile(S):
    return min(256, S)


def _chunk_mask(TQ):
    r = lax.broadcasted_iota(jnp.int32, (TQ, TQ), 0) // CHUNK
    c = lax.broadcasted_iota(jnp.int32, (TQ, TQ), 1) // CHUNK
    return c <= r


_NT = (((1,), (1,)), ((), ()))
_TN = (((0,), (0,)), ((), ()))


def _attn_fwd(q, k, v, name):
    S = q.shape[0]
    TQ = _attn_tile(S)
    nq = S // TQ
    PW = 2 * HEAD_PAD

    def body(q_ref, k_ref, v_ref, o_ref, lse_ref, m_s, l_s, acc_s):
        i = pl.program_id(1)
        left = lax.broadcasted_iota(jnp.int32, (TQ, LANES), 1) < V_HEAD
        m_s[...] = jnp.full_like(m_s, NEG)
        l_s[...] = jnp.zeros_like(l_s)
        acc_s[...] = jnp.zeros_like(acc_s)
        qv = q_ref[...]

        def step(j, masked):
            r0 = pl.multiple_of(j * TQ, TQ)
            kb = k_ref[pl.ds(r0, TQ), :]
            vb = v_ref[pl.ds(r0, TQ), :]
            alphas, pvs = [], []
            for h in range(2):
                sl = slice(h * HEAD_PAD, (h + 1) * HEAD_PAD)
                s = lax.dot_general(qv[:, sl], kb[:, sl], _NT, preferred_element_type=F32)
                if masked:
                    s = jnp.where(_chunk_mask(TQ), s, NEG)
                m_prev = m_s[h]
                m_new = jnp.maximum(m_prev, jnp.max(s, axis=1, keepdims=True))
                alpha = jnp.exp(m_prev - m_new)
                p = jnp.exp(s - m_new[:, 0:1])
                l_s[h] = alpha * l_s[h] + jnp.sum(p, axis=1, keepdims=True)
                m_s[h] = m_new
                alphas.append(alpha)
                pvs.append(jnp.dot(p.astype(BF16), vb, preferred_element_type=F32))
            acc_s[...] = acc_s[...] * jnp.where(left, alphas[0], alphas[1]) + jnp.where(left, pvs[0], pvs[1])

        def loop_body(j, carry):
            step(j, False)
            return carry

        lax.fori_loop(0, i, loop_body, 0)
        step(i, True)
        o_ref[...] = acc_s[...] / jnp.where(left, l_s[0], l_s[1])
        lse_ref[...] = jnp.where(left, m_s[0] + jnp.log(l_s[0]), m_s[1] + jnp.log(l_s[1]))

    return pl.pallas_call(
        body, name=name, grid=(HEADS // 2, nq),
        out_shape=(jax.ShapeDtypeStruct((S, HEADS * V_HEAD), F32), jax.ShapeDtypeStruct((S, HEADS * V_HEAD), F32)),
        in_specs=[pl.BlockSpec((TQ, PW), lambda p, i: (i, p)),
                  pl.BlockSpec((S, PW), lambda p, i: (0, p)),
                  pl.BlockSpec((S, LANES), lambda p, i: (0, p))],
        out_specs=(pl.BlockSpec((TQ, LANES), lambda p, i: (i, p)),
                   pl.BlockSpec((TQ, LANES), lambda p, i: (i, p))),
        scratch_shapes=[pltpu.VMEM((2, TQ, LANES), F32), pltpu.VMEM((2, TQ, LANES), F32),
                        pltpu.VMEM((TQ, LANES), F32)],
        compiler_params=_cparams(),
    )(q, k, v)


def _attn_dq(q, k, v, do, lse, delta, name):
    S = q.shape[0]
    TQ = _attn_tile(S)
    nq = S // TQ
    PW = 2 * HEAD_PAD

    def body(q_ref, k_ref, v_ref, do_ref, lse_ref, dl_ref, dq_ref, acc_s):
        i = pl.program_id(1)
        left = lax.broadcasted_iota(jnp.int32, (TQ, LANES), 1) < V_HEAD
        acc_s[...] = jnp.zeros_like(acc_s)
        qv = q_ref[...]
        dov = do_ref[...]
        dos = [jnp.where(left, dov, jnp.zeros_like(dov)), jnp.where(left, jnp.zeros_like(dov), dov)]
        lses = [lse_ref[:, 0:1], lse_ref[:, V_HEAD:V_HEAD + 1]]
        dls = [dl_ref[:, 0:1], dl_ref[:, V_HEAD:V_HEAD + 1]]

        def step(j, masked):
            r0 = pl.multiple_of(j * TQ, TQ)
            kb = k_ref[pl.ds(r0, TQ), :]
            vb = v_ref[pl.ds(r0, TQ), :]
            for h in range(2):
                sl = slice(h * HEAD_PAD, (h + 1) * HEAD_PAD)
                s = lax.dot_general(qv[:, sl], kb[:, sl], _NT, preferred_element_type=F32)
                p = jnp.exp(s - lses[h])
                if masked:
                    p = jnp.where(_chunk_mask(TQ), p, 0.0)
                dp = lax.dot_general(dos[h], vb, _NT, preferred_element_type=F32)
                ds = (p * (dp - dls[h])).astype(BF16)
                acc_s[:, sl] += jnp.dot(ds, kb[:, sl], preferred_element_type=F32)

        def loop_body(j, carry):
            step(j, False)
            return carry

        lax.fori_loop(0, i, loop_body, 0)
        step(i, True)
        dq_ref[...] = acc_s[...]

    return pl.pallas_call(
        body, name=name, grid=(HEADS // 2, nq),
        out_shape=jax.ShapeDtypeStruct((S, HEADS * HEAD_PAD), F32),
        in_specs=[pl.BlockSpec((TQ, PW), lambda p, i: (i, p)),
                  pl.BlockSpec((S, PW), lambda p, i: (0, p)),
                  pl.BlockSpec((S, LANES), lambda p, i: (0, p)),
                  pl.BlockSpec((TQ, LANES), lambda p, i: (i, p)),
                  pl.BlockSpec((TQ, LANES), lambda p, i: (i, p)),
                  pl.BlockSpec((TQ, LANES), lambda p, i: (i, p))],
        out_specs=pl.BlockSpec((TQ, PW), lambda p, i: (i, p)),
        scratch_shapes=[pltpu.VMEM((TQ, PW), F32)],
        compiler_params=_cparams(),
    )(q, k, v, do, lse, delta)


def _attn_dkv(q, k, v, do, lse, delta, name):
    S = q.shape[0]
    TQ = _attn_tile(S)
    nq = S // TQ
    PW = 2 * HEAD_PAD

    def body(q_ref, k_ref, v_ref, do_ref, lse_ref, dl_ref, dk_ref, dv_ref, dk_s, dv_s):
        j = pl.program_id(1)
        left = lax.broadcasted_iota(jnp.int32, (TQ, LANES), 1) < V_HEAD
        dk_s[...] = jnp.zeros_like(dk_s)
        dv_s[...] = jnp.zeros_like(dv_s)
        kb = k_ref[...]
        vb = v_ref[...]

        def step(i, masked):
            r0 = pl.multiple_of(i * TQ, TQ)
            qb = q_ref[pl.ds(r0, TQ), :]
            dov = do_ref[pl.ds(r0, TQ), :]
            lse = lse_ref[pl.ds(r0, TQ), :]
            dl = dl_ref[pl.ds(r0, TQ), :]
            dos = [jnp.where(left, dov, jnp.zeros_like(dov)), jnp.where(left, jnp.zeros_like(dov), dov)]
            for h in range(2):
                sl = slice(h * HEAD_PAD, (h + 1) * HEAD_PAD)
                c0 = h * V_HEAD
                s = lax.dot_general(qb[:, sl], kb[:, sl], _NT, preferred_element_type=F32)
                p = jnp.exp(s - lse[:, c0:c0 + 1])
                if masked:
                    p = jnp.where(_chunk_mask(TQ), p, 0.0)
                dv_s[...] += lax.dot_general(p.astype(BF16), dos[h], _TN, preferred_element_type=F32)
                dp = lax.dot_general(dos[h], vb, _NT, preferred_element_type=F32)
                ds = (p * (dp - dl[:, c0:c0 + 1])).astype(BF16)
                dk_s[:, sl] += lax.dot_general(ds, qb[:, sl], _TN, preferred_element_type=F32)

        def loop_body(i, carry):
            step(i, False)
            return carry

        step(j, True)
        lax.fori_loop(j + 1, nq, loop_body, 0)
        dk_ref[...] = dk_s[...]
        dv_ref[...] = dv_s[...]

    return pl.pallas_call(
        body, name=name, grid=(HEADS // 2, nq),
        out_shape=(jax.ShapeDtypeStruct((S, HEADS * HEAD_PAD), F32), jax.ShapeDtypeStruct((S, HEADS * V_HEAD), F32)),
        in_specs=[pl.BlockSpec((S, PW), lambda p, j: (0, p)),
                  pl.BlockSpec((TQ, PW), lambda p, j: (j, p)),
                  pl.BlockSpec((TQ, LANES), lambda p, j: (j, p)),
                  pl.BlockSpec((S, LANES), lambda p, j: (0, p)),
                  pl.BlockSpec((S, LANES), lambda p, j: (0, p)),
                  pl.BlockSpec((S, LANES), lambda p, j: (0, p))],
        out_specs=(pl.BlockSpec((TQ, PW), lambda p, j: (j, p)),
                   pl.BlockSpec((TQ, LANES), lambda p, j: (j, p))),
        scratch_shapes=[pltpu.VMEM((TQ, PW), F32), pltpu.VMEM((TQ, LANES), F32)],
        compiler_params=_cparams(),
    )(q, k, v, do, lse, delta)


LOG2E = math.log2(math.e)
ATTN_FWD_HEADS = 8
ATTN_BWD_HEADS = 4


def _chunk_mask_t(T):
    key = lax.broadcasted_iota(jnp.int32, (T, T), 0) // CHUNK
    qry = lax.broadcasted_iota(jnp.int32, (T, T), 1) // CHUNK
    return key <= qry


def _qkv_fwd_t(qn, kvn, z, tabs, w_q, w_kv, name):
    S = qn.shape[0]
    T = _attn_tile(S)
    HW = HEADS * HEAD_PAD
    scale = LOG2E / math.sqrt(QK_NOPE + QK_ROPE)

    def body(qn_ref, kvn_ref, kr_ref, ct_ref, ut_ref, dt_ref, wq_ref, wkv_ref, q_ref, k_ref, v_ref, kt_ref, vt_ref):
        ct, ut, dt = ct_ref[...], ut_ref[...], dt_ref[...]
        qa = jnp.dot(qn_ref[...], wq_ref[...], preferred_element_type=F32)
        kva = jnp.dot(kvn_ref[...], wkv_ref[...], preferred_element_type=F32)
        kr = _f32(kr_ref)
        ones_row = (lax.broadcasted_iota(jnp.int32, (V_HEAD, T), 0) == 0).astype(F32)
        for h in range(HEADS):
            sl = slice(h * HEAD_PAD, (h + 1) * HEAD_PAD)
            q_ref[:, sl] = (_rope(qa[:, sl], ct, ut, dt) * scale).astype(BF16)
            kh = _rope(kva[:, sl] + kr, ct, ut, dt)
            k_ref[:, sl] = kh.astype(BF16)
            kt_ref[0, sl, :] = kh.T.astype(BF16)
        v_ref[...] = kva[:, HW:].astype(BF16)
        for p in range(HEADS // 2):
            vpt = kva[:, HW + p * LANES:HW + (p + 1) * LANES].T
            for h in range(2):
                r0 = (2 * p + h) * HEAD_PAD
                vt_ref[0, r0:r0 + V_HEAD, :] = vpt[h * V_HEAD:(h + 1) * V_HEAD, :].astype(BF16)
                vt_ref[0, r0 + V_HEAD:r0 + HEAD_PAD, :] = ones_row.astype(BF16)

    t3 = jax.ShapeDtypeStruct((S // T, HW, T), BF16)
    return pl.pallas_call(
        body, name=name, grid=(S // T,),
        out_shape=(jax.ShapeDtypeStruct((S, HW), BF16), jax.ShapeDtypeStruct((S, HW), BF16),
                   jax.ShapeDtypeStruct((S, HEADS * V_HEAD), BF16), t3, t3),
        in_specs=[_rows(T, QL), _rows(T, KVL), _rows(T, HEAD_PAD, 23),
                  _rows(T, HEAD_PAD), _rows(T, HEAD_PAD), _rows(T, HEAD_PAD),
                  _const(w_q.shape), _const(w_kv.shape)],
        out_specs=(_rows(T, HW), _rows(T, HW), _rows(T, HEADS * V_HEAD),
                   pl.BlockSpec((1, HW, T), lambda i: (i, 0, 0)), pl.BlockSpec((1, HW, T), lambda i: (i, 0, 0))),
        compiler_params=_cparams(),
    )(qn, kvn, z, *tabs, w_q, w_kv)


def _attn_fwd_t(q, k, vT3, name):
    S = q.shape[0]
    T = _attn_tile(S)
    nq = S // T
    NH = ATTN_FWD_HEADS
    PW = NH * HEAD_PAD

    def body(q_ref, k_ref, vt_ref, o_ref, lse_ref, m_s, acc_s):
        i = pl.program_id(1)
        m_s[...] = jnp.full_like(m_s, NEG)
        acc_s[...] = jnp.zeros_like(acc_s)
        qv = q_ref[...]

        def step(j, masked):
            kb = k_ref[pl.ds(pl.multiple_of(j * T, T), T), :]
            vt = vt_ref[j]
            heads = [slice(h * HEAD_PAD, (h + 1) * HEAD_PAD) for h in range(NH)]
            sts = [lax.dot_general(kb[:, sl], qv[:, sl], _NT, preferred_element_type=F32) for sl in heads]
            alphas, pvs = [], []
            for h, sl in enumerate(heads):
                st = jnp.where(_chunk_mask_t(T), sts[h], NEG) if masked else sts[h]
                m_prev = m_s[h]
                m_new = jnp.maximum(m_prev, jnp.max(st, axis=0, keepdims=True))
                alphas.append(jnp.exp2(m_prev[0:1] - m_new[0:1]))
                pt = jnp.exp2(st - m_new[0:1]).astype(BF16)
                m_s[h] = m_new
                pvs.append(jnp.dot(vt[sl, :], pt, preferred_element_type=F32))
            for h in range(NH):
                acc_s[h] = acc_s[h] * alphas[h] + pvs[h]

        def loop_body(j, carry):
            step(j, False)
            return carry

        lax.fori_loop(0, i, loop_body, 0)
        step(i, True)
        for g in range(NH // 2):
            outs = []
            for h in (2 * g, 2 * g + 1):
                acc = acc_s[h]
                l_row = acc[V_HEAD:V_HEAD + 1, :]
                outs.append(acc[0:V_HEAD, :] / l_row)
                lse_ref[0, h * SUBLANES:(h + 1) * SUBLANES, :] = m_s[h] + jnp.log2(l_row)
            o_ref[:, g * LANES:(g + 1) * LANES] = jnp.concatenate(outs, axis=0).T

    return pl.pallas_call(
        body, name=name, grid=(HEADS // NH, nq),
        out_shape=(jax.ShapeDtypeStruct((S, HEADS * V_HEAD), F32),
                   jax.ShapeDtypeStruct((nq, HEADS * SUBLANES, T), F32)),
        in_specs=[pl.BlockSpec((T, PW), lambda p, i: (i, p)),
                  pl.BlockSpec((S, PW), lambda p, i: (0, p)),
                  pl.BlockSpec((nq, PW, T), lambda p, i: (0, p, 0))],
        out_specs=(pl.BlockSpec((T, NH * V_HEAD), lambda p, i: (i, p)),
                   pl.BlockSpec((1, NH * SUBLANES, T), lambda p, i: (i, p, 0))),
        scratch_shapes=[pltpu.VMEM((NH, SUBLANES, T), F32), pltpu.VMEM((NH, HEAD_PAD, T), F32)],
        compiler_params=_cparams(),
    )(q, k, vT3)


def _attn_bwd_t(q, k, v, kT3, do, lse3, dl3, name):
    S = q.shape[0]
    T = _attn_tile(S)
    nq = S // T
    NH = ATTN_BWD_HEADS
    PW = NH * HEAD_PAD
    VW = NH * V_HEAD

    def body(q_ref, k_ref, v_ref, kt_ref, do_ref, lse_ref, dl_ref, dq_ref, dk_ref, dv_ref, dk_s, dv_s):
        j = pl.program_id(1)
        left = lax.broadcasted_iota(jnp.int32, (T, LANES), 1) < V_HEAD

        @pl.when(j == 0)
        def _():
            dq_ref[...] = jnp.zeros_like(dq_ref)

        dk_s[...] = jnp.zeros_like(dk_s)
        dv_s[...] = jnp.zeros_like(dv_s)
        kb = k_ref[...]
        vms = []
        for g in range(NH // 2):
            vb = v_ref[:, g * LANES:(g + 1) * LANES]
            vms += [jnp.where(left, vb, jnp.zeros_like(vb)), jnp.where(left, jnp.zeros_like(vb), vb)]
        kt = kt_ref[0]

        def step(i, masked):
            r0 = pl.multiple_of(i * T, T)
            qb = q_ref[pl.ds(r0, T), :]
            do_all = do_ref[pl.ds(r0, T), :]
            lse = lse_ref[i]
            dl = dl_ref[i]
            heads = [slice(h * HEAD_PAD, (h + 1) * HEAD_PAD) for h in range(NH)]
            dobs = [do_all[:, (h // 2) * LANES:(h // 2 + 1) * LANES] for h in range(NH)]
            sts = [lax.dot_general(kb[:, sl], qb[:, sl], _NT, preferred_element_type=F32) for sl in heads]
            dpts = [lax.dot_general(vms[h], dobs[h], _NT, preferred_element_type=F32) for h in range(NH)]
            res = []
            for h, sl in enumerate(heads):
                r8 = h * SUBLANES
                pt = jnp.exp2(sts[h] - lse[r8:r8 + 1, :])
                if masked:
                    pt = jnp.where(_chunk_mask_t(T), pt, 0.0)
                dst = (pt * (dpts[h] - dl[r8:r8 + 1, :])).astype(BF16)
                res.append((jnp.dot(pt.astype(BF16), dobs[h], preferred_element_type=F32),
                            jnp.dot(dst, qb[:, sl], preferred_element_type=F32),
                            jnp.dot(kt[sl, :], dst, preferred_element_type=F32)))
            for h, sl in enumerate(heads):
                dv_s[h] += res[h][0]
                dk_s[:, sl] += res[h][1]
                dq_ref[i, sl, :] += res[h][2]

        def loop_body(i, carry):
            step(i, False)
            return carry

        step(j, True)
        lax.fori_loop(j + 1, nq, loop_body, 0)
        dk_ref[...] = dk_s[...] * (1.0 / LOG2E)
        for g in range(NH // 2):
            dv_ref[:, g * LANES:(g + 1) * LANES] = jnp.where(left, dv_s[2 * g], dv_s[2 * g + 1])

    return pl.pallas_call(
        body, name=name, grid=(HEADS // NH, nq),
        out_shape=(jax.ShapeDtypeStruct((nq, HEADS * HEAD_PAD, T), F32),
                   jax.ShapeDtypeStruct((S, HEADS * HEAD_PAD), F32), jax.ShapeDtypeStruct((S, HEADS * V_HEAD), F32)),
        in_specs=[pl.BlockSpec((S, PW), lambda p, j: (0, p)),
                  pl.BlockSpec((T, PW), lambda p, j: (j, p)),
                  pl.BlockSpec((T, VW), lambda p, j: (j, p)),
                  pl.BlockSpec((1, PW, T), lambda p, j: (j, p, 0)),
                  pl.BlockSpec((S, VW), lambda p, j: (0, p)),
                  pl.BlockSpec((nq, NH * SUBLANES, T), lambda p, j: (0, p, 0)),
                  pl.BlockSpec((nq, NH * SUBLANES, T), lambda p, j: (0, p, 0))],
        out_specs=(pl.BlockSpec((nq, PW, T), lambda p, j: (0, p, 0)),
                   pl.BlockSpec((T, PW), lambda p, j: (j, p)),
                   pl.BlockSpec((T, VW), lambda p, j: (j, p))),
        scratch_shapes=[pltpu.VMEM((T, PW), F32), pltpu.VMEM((NH, T, LANES), F32)],
        compiler_params=_cparams(),
    )(q, k, v, kT3, do, lse3, dl3)


def _even_post(z, cv, o, name):
    S = z.shape[0]
    T = _row_tile(S)

    def body(ab_ref, ag_ref, bg_ref, cv_ref, o_ref, y_ref):
        y_ref[:, 0:SCW] = (_f32(ab_ref) * cv_ref[...] * _silu(_f32(ag_ref))).astype(BF16)
        y_ref[:, SCW:2 * SCW] = (o_ref[...] * _silu(_f32(bg_ref))).astype(BF16)

    return pl.pallas_call(
        body, name=name, grid=(S // T,),
        out_shape=jax.ShapeDtypeStruct((S, 2 * SCW), BF16),
        in_specs=[_rows(T, SCW, 0), _rows(T, SCW, 3), _rows(T, SCW, 4), _rows(T, SCW), _rows(T, SCW)],
        out_specs=_rows(T, 2 * SCW), compiler_params=_cparams(),
    )(z, z, z, cv, o)


def _even_bwd_gates(dyc, z, cv, o, name):
    S = z.shape[0]
    T = _row_tile(S)

    def body(dya_ref, dyb_ref, ab_ref, ag_ref, bg_ref, cv_ref, o_ref,
             dab_ref, dag_ref, dbg_ref, dcv_ref, do_ref, dl_ref):
        dya, ab, ag, cv = dya_ref[...], _f32(ab_ref), _f32(ag_ref), cv_ref[...]
        sg = _silu(ag)
        dab_ref[...] = (dya * cv * sg).astype(BF16)
        dcv_ref[...] = dya * ab * sg
        dag_ref[...] = (dya * ab * cv * _dsilu(ag)).astype(BF16)
        dyb, bg, ov = dyb_ref[...], _f32(bg_ref), o_ref[...]
        dov = dyb * _silu(bg)
        do_ref[...] = dov.astype(BF16)
        dbg_ref[...] = (dyb * ov * _dsilu(bg)).astype(BF16)
        prod = dov * ov
        left = lax.broadcasted_iota(jnp.int32, (T, LANES), 1) < V_HEAD
        for p in range(HEADS // 2):
            blk = prod[:, p * LANES:(p + 1) * LANES]
            s0 = jnp.sum(jnp.where(left, blk, 0.0), axis=1, keepdims=True)
            s1 = jnp.sum(jnp.where(left, 0.0, blk), axis=1, keepdims=True)
            dt = jnp.where(left, s0, s1).T
            dl_ref[0, 2 * p * SUBLANES:(2 * p + 1) * SUBLANES, :] = dt[0:SUBLANES, :]
            dl_ref[0, (2 * p + 1) * SUBLANES:(2 * p + 2) * SUBLANES, :] = dt[V_HEAD:V_HEAD + SUBLANES, :]

    assert T == _attn_tile(S)
    bf = jax.ShapeDtypeStruct((S, SCW), BF16)
    ff = jax.ShapeDtypeStruct((S, SCW), F32)
    return pl.pallas_call(
        body, name=name, grid=(S // T,),
        out_shape=(bf, bf, bf, ff, bf, jax.ShapeDtypeStruct((S // T, HEADS * SUBLANES, T), F32)),
        in_specs=[_rows(T, SCW, 0), _rows(T, SCW, 1), _rows(T, SCW, 0), _rows(T, SCW, 3), _rows(T, SCW, 4),
                  _rows(T, SCW), _rows(T, SCW)],
        out_specs=(_rows(T, SCW),) * 5 + (pl.BlockSpec((1, HEADS * SUBLANES, T), lambda i: (i, 0, 0)),),
        compiler_params=_cparams(),
    )(dyc, dyc, z, z, z, cv, o)


def _qkv_bwd(dq, dk, dv, z, tabs, w_q, w_kv, qg, kvg, name):
    S = dk.shape[0]
    T = _attn_tile(S)
    HW = HEADS * HEAD_PAD
    VW = HEADS * V_HEAD
    scale = 1.0 / math.sqrt(QK_NOPE + QK_ROPE)

    def fn(ins, outs):
        dq_ref, dk_ref, dv_ref, cq_ref, ckv_ref, ct_ref, ut_ref, dt_ref, wq_ref, wkv_ref, qg_ref, kvg_ref = ins
        dqp_ref, dkvp_ref, dcq_ref, dckv_ref, dkr_ref = outs
        ct, ut, dt = ct_ref[...], ut_ref[...], dt_ref[...]
        dkr = jnp.zeros((T, HEAD_PAD), F32)
        for h in range(HEADS):
            sl = slice(h * HEAD_PAD, (h + 1) * HEAD_PAD)
            dqp_ref[:, sl] = (_rope_t(dq_ref[0, sl, :].T, ct, ut, dt) * scale).astype(BF16)
            dkh = _rope_t(dk_ref[:, sl], ct, ut, dt)
            dkr = dkr + dkh
            dkvp_ref[:, sl] = dkh.astype(BF16)
        dkvp_ref[:, HW:] = dv_ref[...].astype(BF16)
        dkr_ref[...] = dkr.astype(BF16)
        sums = []
        for lat_ref, g_ref, dpre_ref, w_ref, dlat_ref in ((cq_ref, qg_ref, dqp_ref, wq_ref, dcq_ref),
                                                         (ckv_ref, kvg_ref, dkvp_ref, wkv_ref, dckv_ref)):
            dn = lax.dot_general(dpre_ref[...], w_ref[...], _NT, preferred_element_type=F32)
            xv = _f32(lat_ref)
            rstd = lax.rsqrt(jnp.mean(xv * xv, axis=-1, keepdims=True) + EPS)
            xh = xv * rstd
            dxh = dn * g_ref[...]
            dlat_ref[...] = (rstd * (dxh - xh * jnp.mean(dxh * xh, axis=-1, keepdims=True))).astype(BF16)
            sums.append(dn * xh)
        return sums

    return _col_sums(
        2, fn, [dq, dk, dv, z, z, *tabs, w_q, w_kv, qg, kvg],
        [pl.BlockSpec((1, HW, T), lambda i: (i, 0, 0)), _rows(T, HW), _rows(T, VW), _rows(T, QL, 10), _rows(T, KVL, 22),
         _rows(T, HEAD_PAD), _rows(T, HEAD_PAD), _rows(T, HEAD_PAD),
         _const(w_q.shape), _const(w_kv.shape), _const((1, QL)), _const((1, KVL))],
        [jax.ShapeDtypeStruct((S, HW), BF16), jax.ShapeDtypeStruct((S, HW + VW), BF16),
         jax.ShapeDtypeStruct((S, QL), BF16), jax.ShapeDtypeStruct((S, KVL), BF16),
         jax.ShapeDtypeStruct((S, HEAD_PAD), BF16)],
        [_rows(T, HW), _rows(T, HW + VW), _rows(T, QL), _rows(T, KVL), _rows(T, HEAD_PAD)],
        S, T, [QL, KVL], name)


def _even_dz(dab, du, z, dag, dbg, dcq, dckv, dkr, name):
    S = z.shape[0]
    T = _row_tile(S)

    def body(dab_ref, du_ref, ac_ref, ax_ref, dag_ref, dbg_ref, dcq_ref, dckv_ref, dkr_ref, dz_ref):
        duv = du_ref[...]
        dz_ref[:, 0:SCW] = dab_ref[...]
        dz_ref[:, SCW:2 * SCW] = (duv * _f32(ax_ref)).astype(BF16)
        dz_ref[:, 2 * SCW:3 * SCW] = (duv * _f32(ac_ref)).astype(BF16)
        dz_ref[:, 3 * SCW:4 * SCW] = dag_ref[...]
        dz_ref[:, 4 * SCW:5 * SCW] = dbg_ref[...]
        dz_ref[:, 5 * SCW:5 * SCW + QL] = dcq_ref[...]
        dz_ref[:, 5 * SCW + QL:5 * SCW + QL + KVL] = dckv_ref[...]
        dz_ref[:, 5 * SCW + QL + KVL:ZE] = dkr_ref[...]

    return pl.pallas_call(
        body, name=name, grid=(S // T,),
        out_shape=jax.ShapeDtypeStruct((S, ZE), BF16),
        in_specs=[_rows(T, SCW), _rows(T, SCW), _rows(T, SCW, 1), _rows(T, SCW, 2), _rows(T, SCW), _rows(T, SCW),
                  _rows(T, QL), _rows(T, KVL), _rows(T, HEAD_PAD)],
        out_specs=_rows(T, ZE), compiler_params=_cparams(),
    )(dab, du, z, z, dag, dbg, dcq, dckv, dkr)


def _odd_pre(z, name):
    S, D = z.shape[0], z.shape[1] // 3
    T = _row_tile(S)

    def body(val_ref, glu_ref, u_ref):
        u_ref[...] = _f32(val_ref) * _sigmoid(_f32(glu_ref))

    return pl.pallas_call(
        body, name=name, grid=(S // T,),
        out_shape=jax.ShapeDtypeStruct((S, D), F32),
        in_specs=[_rows(T, D, 0), _rows(T, D, 1)], out_specs=_rows(T, D),
        compiler_params=_cparams(),
    )(z, z)


def _layer_norm_stats(cv):
    mu = jnp.mean(cv, axis=-1, keepdims=True)
    cen = cv - mu
    rstd = lax.rsqrt(jnp.mean(cen * cen, axis=-1, keepdims=True) + EPS)
    return cen * rstd, rstd


def _odd_post(cv, z, ln_g, ln_b, name):
    S, D = cv.shape
    T = _row_tile(S)

    def body(cv_ref, sg_ref, g_ref, b_ref, y_ref):
        cvh, _ = _layer_norm_stats(cv_ref[...])
        y_ref[...] = (_silu(cvh * g_ref[...] + b_ref[...]) * _silu(_f32(sg_ref))).astype(BF16)

    return pl.pallas_call(
        body, name=name, grid=(S // T,),
        out_shape=jax.ShapeDtypeStruct((S, D), BF16),
        in_specs=[_rows(T, D), _rows(T, D, 2), _const((1, D)), _const((1, D))],
        out_specs=_rows(T, D), compiler_params=_cparams(),
    )(cv, z, ln_g, ln_b)


def _odd_bwd_norm(dyi, cv, z, ln_g, ln_b, name):
    S, D = cv.shape
    T = _row_tile(S)

    def fn(ins, outs):
        dy_ref, cv_ref, sg_ref, g_ref, b_ref = ins
        dcv_ref, dsg_ref = outs
        cvh, rstd = _layer_norm_stats(cv_ref[...])
        ln = cvh * g_ref[...] + b_ref[...]
        sgv, dy = _f32(sg_ref), dy_ref[...]
        dsg_ref[...] = (dy * _silu(ln) * _dsilu(sgv)).astype(BF16)
        dln = dy * _silu(sgv) * _dsilu(ln)
        dh = dln * g_ref[...]
        dcv_ref[...] = rstd * (dh - jnp.mean(dh, axis=-1, keepdims=True)
                               - cvh * jnp.mean(dh * cvh, axis=-1, keepdims=True))
        return [dln * cvh, dln]

    return _col_sums(2, fn, [dyi, cv, z, ln_g, ln_b],
                     [_rows(T, D), _rows(T, D), _rows(T, D, 2), _const((1, D)), _const((1, D))],
                     [jax.ShapeDtypeStruct((S, D), F32), jax.ShapeDtypeStruct((S, D), BF16)],
                     [_rows(T, D), _rows(T, D)], S, T, [D, D], name)


def _odd_dz(du, z, dsg, name):
    S, D = du.shape
    T = _row_tile(S)

    def body(du_ref, val_ref, glu_ref, dsg_ref, dz_ref):
        duv = du_ref[...]
        sig = _sigmoid(_f32(glu_ref))
        dz_ref[:, 0:D] = (duv * sig).astype(BF16)
        dz_ref[:, D:2 * D] = (duv * _f32(val_ref) * sig * (1.0 - sig)).astype(BF16)
        dz_ref[:, 2 * D:3 * D] = dsg_ref[...]

    return pl.pallas_call(
        body, name=name, grid=(S // T,),
        out_shape=jax.ShapeDtypeStruct((S, 3 * D), BF16),
        in_specs=[_rows(T, D), _rows(T, D, 0), _rows(T, D, 1), _rows(T, D)],
        out_specs=_rows(T, 3 * D), compiler_params=_cparams(),
    )(du, z, z, dsg)


ADAM_BLOCK_ELEMS = 128 * 1024


def _adam_tiles(R, C):
    if R * C <= ADAM_BLOCK_ELEMS:
        return R, C
    tr = R
    for cand in range(SUBLANES, R, SUBLANES):
        if R % cand == 0 and cand * C <= ADAM_BLOCK_ELEMS:
            tr = cand
    if tr < R:
        return tr, C
    tc = C
    for cand in range(LANES, C, LANES):
        if C % cand == 0 and R * cand <= ADAM_BLOCK_ELEMS:
            tc = cand
    return R, tc


def _adamw(g_parts, w, m, v, name):
    if not isinstance(g_parts, (list, tuple)):
        g_parts = [g_parts]
    ng = len(g_parts)
    _, R, C = g_parts[0].shape
    tr, tc = _adam_tiles(R, C)

    def body(*refs):
        g_refs = refs[:ng]
        w_ref, m_ref, v_ref, go_ref, d_ref, mo_ref, vo_ref = refs[ng:]
        g = None
        for g_ref in g_refs:
            for p in range(g_ref.shape[0]):
                part = g_ref[p].astype(F32)
                g = part if g is None else g + part
        mn = ADAM_B1 * m_ref[...] + (1.0 - ADAM_B1) * g
        vn = ADAM_B2 * v_ref[...] + (1.0 - ADAM_B2) * (g * g)
        m_hat = mn / (1.0 - ADAM_B1 ** ADAM_STEP)
        v_hat = vn / (1.0 - ADAM_B2 ** ADAM_STEP)
        go_ref[...] = g
        d_ref[...] = -ADAM_LR * (m_hat / (jnp.sqrt(v_hat) + ADAM_EPS) + ADAM_WD * w_ref[...])
        mo_ref[...] = mn
        vo_ref[...] = vn

    slab = jax.ShapeDtypeStruct((R, C), F32)
    blk = pl.BlockSpec((tr, tc), lambda i, j: (i, j))
    return pl.pallas_call(
        body, name=name, grid=(R // tr, C // tc),
        out_shape=(slab,) * 4,
        in_specs=[pl.BlockSpec((g.shape[0], tr, tc), lambda i, j: (0, i, j)) for g in g_parts] + [blk, blk, blk],
        out_specs=(blk,) * 4, compiler_params=_cparams(),
    )(*g_parts, w, m, v)


def _gather_cols(g, shape):
    nd = len(shape)
    t = jnp.moveaxis(g, 0, nd - 1)
    return t.reshape(tuple(shape[:-1]) + (N_DEV * shape[-1],))


def _scatter_cols(full, n):
    t = full.reshape(full.shape[:-1] + (N_DEV, n))
    return jnp.moveaxis(t, -2, 0)


def kernel(x, c, positions, ada_w, ada_b, pre_norm_g, post_norm_g, even_w_in, even_sc_conv_w, even_sc_conv_b, even_q_norm_g, even_kv_norm_g, even_w_uq, even_w_ukv, even_w_out, odd_w_in, odd_conv_w, odd_conv_b, odd_ln_g, odd_ln_b, odd_w_out, loss_target, m_ada_w, m_ada_b, m_pre_norm_g, m_post_norm_g, m_even_w_in, m_even_sc_conv_w, m_even_sc_conv_b, m_even_q_norm_g, m_even_kv_norm_g, m_even_w_uq, m_even_w_ukv, m_even_w_out, m_odd_w_in, m_odd_conv_w, m_odd_conv_b, m_odd_ln_g, m_odd_ln_b, m_odd_w_out, v_ada_w, v_ada_b, v_pre_norm_g, v_post_norm_g, v_even_w_in, v_even_sc_conv_w, v_even_sc_conv_b, v_even_q_norm_g, v_even_kv_norm_g, v_even_w_uq, v_even_w_ukv, v_even_w_out, v_odd_w_in, v_odd_conv_w, v_odd_conv_b, v_odd_ln_g, v_odd_ln_b, v_odd_w_out):
    S, D = x.shape[1], x.shape[2]
    L = ada_w.shape[0]
    NE, NO = even_w_in.shape[0], odd_w_in.shape[0]
    me = 4 * lax.axis_index("x") + 2 * lax.axis_index("y") + lax.axis_index("c")
    x0 = x[0]
    target = loss_target[0]

    small_parts = [c, even_sc_conv_w, odd_conv_w, odd_conv_b, odd_ln_g, odd_ln_b]
    small_shapes = [p.shape for p in small_parts]
    sg = _exchange([_pack(small_parts, F32, SUBLANES)], False, "gather_small")[0].reshape(N_DEV, -1)
    c_all, scw_g, ocw_g, ocb_g, olg_g, olb_g = _unpack(sg, small_shapes)
    c_all = c_all.reshape(N_DEV, D)
    sc_conv_w = _gather_cols(scw_g, even_sc_conv_w.shape)
    o_conv_w = _gather_cols(ocw_g, odd_conv_w.shape)
    o_conv_b = _gather_cols(ocb_g, odd_conv_b.shape)
    o_ln_g = _gather_cols(olg_g, odd_ln_g.shape)
    o_ln_b = _gather_cols(olb_g, odd_ln_b.shape)

    pad_q = HEAD_PAD - QK_NOPE - QK_ROPE
    w_local = [jnp.swapaxes(even_w_in, 1, 2).astype(BF16),
               jnp.pad(even_w_uq, ((0, 0), (0, 0), (0, pad_q))).astype(BF16),
               jnp.pad(even_w_ukv[..., :QK_NOPE], ((0, 0), (0, 0), (0, HEAD_PAD - QK_NOPE))).astype(BF16),
               even_w_ukv[..., QK_NOPE:].astype(BF16),
               even_w_out.astype(BF16), odd_w_in.astype(BF16), odd_w_out.astype(BF16)]
    n_ada = ada_w.shape[2]
    ada_b_cols = lax.dynamic_slice_in_dim(ada_b, me * n_ada, n_ada, axis=1).reshape(L, 1, n_ada)
    mod_slab = _ada_fwd(c_all, ada_w, ada_b_cols)
    mod_g = _exchange([_pack([mod_slab], F32, SUBLANES)], False, "gather_mod")[0].reshape(N_DEV, -1)
    mod_all = mod_g[:, :L * N_DEV * n_ada].reshape(N_DEV, L, N_DEV, n_ada)
    mod = lax.dynamic_index_in_dim(mod_all, me, axis=2, keepdims=False)
    mod = jnp.moveaxis(mod, 0, 1).reshape(L, 3 * D)
    shift, scale, gate = mod[:, :D], mod[:, D:2 * D], mod[:, 2 * D:]

    heads_to_cols = lambda g: jnp.moveaxis(g, 0, 1).reshape(g.shape[1], -1)
    w_handles = {}
    token = jnp.broadcast_to(jnp.minimum(jnp.abs(mod[0, 0]), 0.0), (SUBLANES, LANES))
    for layer in range(L):
        i = layer // 2
        groups = ({"in": [w_local[0][i]], "rest": [w[i] for w in w_local[1:5]]} if layer % 2 == 0
                  else {"all": [w[i] for w in w_local[5:]]})
        for key, mine in groups.items():
            mine = [w + token[0, 0].astype(BF16) for w in mine]
            w_handles[layer, key], token = _exchange_start(mine, False, f"gather_weights_start_l{layer}_{key}")
    w_token = token

    def arrived(layer, key, after):
        return _exchange_wait(w_handles[layer, key], False, after, f"gather_weights_wait_l{layer}_{key}")[1]

    e_w_in_k, e_w_q_k, e_w_kv_k, e_w_out, o_w_in, o_w_out = ([None] * NE, [None] * NE, [None] * NE, [None] * NE,
                                                             [None] * NO, [None] * NO)

    half = QK_ROPE // 2
    inv_freq = 1.0 / (ROPE_THETA ** (jnp.arange(0, QK_ROPE, 2, dtype=F32) / QK_ROPE))
    inv_lane = jnp.zeros((HEAD_PAD,), F32).at[QK_NOPE:QK_NOPE + QK_ROPE].set(jnp.concatenate([inv_freq, inv_freq]))
    tabs = _rope_tables(positions.astype(F32).reshape(S, 1), inv_lane.reshape(1, HEAD_PAD))
    del half

    row = lambda a: a.reshape(1, -1)
    scb = even_sc_conv_b
    KP3, KP31 = SUBLANES, 32

    saved = []
    xs = x0
    h = _pre_norm(xs, row(pre_norm_g[0]) + w_token[0, 0], row(scale[0]), row(shift[0]), "pre_norm_l0")
    for layer in range(L):
        i = layer // 2
        tag = f"l{layer}"
        first = [h, tabs[0]] if layer == 0 else h
        if layer % 2 == 0:
            wt = arrived(layer, "in", first)[0].reshape(-1, D)
            e_w_in_k[i] = jnp.concatenate([wt[:2048], wt[2464:2976], wt[2048:2432], jnp.zeros((QK_NOPE, D), BF16),
                                           wt[2432:2464], jnp.zeros((pad_q, D), BF16)], axis=0)
            z = _matmul(h, e_w_in_k[i], "nt", BF16, f"w_in_{tag}", tn=1024)
            eq_g, ek_g, ev_g, eout_g = arrived(layer, "rest", z)
            e_w_q_k[i] = heads_to_cols(eq_g)
            e_w_kv_k[i] = jnp.concatenate([heads_to_cols(ek_g), heads_to_cols(ev_g)], axis=-1)
            e_w_out[i] = eout_g.reshape(-1, D)
            u, qn, kvn = _even_pre(z, row(even_q_norm_g[i]), row(even_kv_norm_g[i]), f"even_pre_{tag}")
            cw = jnp.pad(sc_conv_w[i], ((0, KP3 - SC_KERNEL), (0, 0)))
            cv = _conv_fwd(u, cw, row(scb[i]), SC_KERNEL, f"conv_{tag}")
            q, k, v, kT3, vT3 = _qkv_fwd_t(qn, kvn, z, tabs, e_w_q_k[i], e_w_kv_k[i], f"qkv_{tag}")
            o, lse = _attn_fwd_t(q, k, vT3, f"attn_{tag}")
            ycat = _even_post(z, cv, o, f"even_post_{tag}")
            y = _matmul(ycat, e_w_out[i], "nn", F32, f"w_out_{tag}", tn=1024)
            saved.append(dict(x=xs, h=h, z=z, u=u, qn=qn, kvn=kvn, cw=cw, cv=cv, q=q, k=k, v=v, kT3=kT3, o=o, lse=lse,
                              ycat=ycat, y=y))
        else:
            owin_g, oout_g = arrived(layer, "all", first)
            o_w_in[i], o_w_out[i] = heads_to_cols(owin_g), oout_g.reshape(-1, D)
            z = _matmul(h, o_w_in[i], "nn", BF16, f"w_in_{tag}", tn=1024)
            u = _odd_pre(z, f"odd_pre_{tag}")
            cw = jnp.pad(o_conv_w[i], ((0, KP31 - CONF_KERNEL), (0, 0)))
            cv = _conv_fwd(u, cw, row(o_conv_b[i]), CONF_KERNEL, f"conv_{tag}")
            yin = _odd_post(cv, z, row(o_ln_g[i]), row(o_ln_b[i]), f"odd_post_{tag}")
            y = _matmul(yin, o_w_out[i], "nn", F32, f"w_out_{tag}", tn=1024)
            saved.append(dict(x=xs, h=h, z=z, u=u, cw=cw, cv=cv, yin=yin, y=y))
        if layer < L - 1:
            xs, h = _post_pre_norm(xs, y, row(post_norm_g[layer]), row(gate[layer]), row(pre_norm_g[layer + 1]),
                                   row(scale[layer + 1]), row(shift[layer + 1]), f"post_pre_norm_{tag}")

    dx, dy, err_sq, dgate, g_post_last = _loss_post_norm_bwd(xs, y, row(post_norm_g[L - 1]), row(gate[L - 1]), target,
                                                             "loss_post_norm_bwd")
    loss = lax.psum(_scaled_total(err_sq, 0.5 / D, "loss_total")[0, 0], MESH_AXES)

    g_pre, g_post, dmod = [None] * L, [None] * L, [None] * L
    g_e_w_in, g_e_w_uq, g_e_w_ukv, g_e_w_out = [None] * NE, [None] * NE, [None] * NE, [None] * NE
    g_scw, g_scb, g_qg, g_kvg = [None] * NE, [None] * NE, [None] * NE, [None] * NE
    g_o_w_in, g_o_w_out, g_ocw, g_ocb, g_olg, g_olb = ([None] * NO for _ in range(6))
    sm_w = [even_sc_conv_w, odd_conv_w, odd_conv_b, odd_ln_g, odd_ln_b]
    sm_rows = _pack(sm_w, F32, SUBLANES).shape[0]

    def small_slab():
        full = [_scatter_cols(jnp.stack(g_scw), even_sc_conv_w.shape[-1]),
                _scatter_cols(jnp.stack(g_ocw), odd_conv_w.shape[-1]),
                _scatter_cols(jnp.concatenate(g_ocb, 0), odd_conv_b.shape[-1]),
                _scatter_cols(jnp.concatenate(g_olg, 0), odd_ln_g.shape[-1]),
                _scatter_cols(jnp.concatenate(g_olb, 0), odd_ln_b.shape[-1])]
        flat = jnp.concatenate([g.reshape(N_DEV, -1) for g in full], axis=1)
        return jnp.pad(flat, ((0, 0), (0, sm_rows * PACK_COLS - flat.shape[1]))).reshape(N_DEV, sm_rows, PACK_COLS)

    scatters = []
    bw_token = jnp.zeros((SUBLANES, LANES), F32)

    def start_scatter(tag, names, parts):
        own = [lax.dynamic_slice_in_dim(g, me, 1, axis=0) for g in parts]
        handle, token = _exchange_start([g.astype(BF16) for g in parts], True, f"scatter_grads_start_{tag}")
        scatters.append((tag, names, handle, own))
        return token

    for layer in reversed(range(L)):
        i = layer // 2
        tag = f"l{layer}"
        sv = saved[layer]
        if layer == L - 1:
            g_post[layer] = g_post_last
        if layer % 2 == 0:
            dyc = _matmul(dy, e_w_out[i], "nt", F32, f"d_ycat_{tag}", tn=1024)
            g_e_w_out[i] = _matmul(sv["ycat"], dy, "tn", BF16, f"g_w_out_{tag}", tn=1024).reshape(N_DEV, -1, D)
            if layer == 0:
                bw_token = start_scatter("l0_out", [("even_w_out", i)], [g_e_w_out[i]])
            dab, dag, dbg, dcv, do, delta = _even_bwd_gates(dyc, sv["z"], sv["cv"], sv["o"], f"even_gates_bwd_{tag}")
            du, dcw, g_scb[i] = _conv_bwd(dcv, sv["u"], sv["cw"] + bw_token[0, 0], SC_KERNEL, f"conv_bwd_{tag}")
            g_scw[i] = dcw[:SC_KERNEL]
            dq, dk, dv = _attn_bwd_t(sv["q"], sv["k"], sv["v"], sv["kT3"], do, sv["lse"], delta, f"attn_bwd_{tag}")
            (dqp, dkvp, dcq, dckv, dkr, g_qg[i], g_kvg[i]) = _qkv_bwd(
                dq, dk, dv, sv["z"], tabs, e_w_q_k[i], e_w_kv_k[i],
                row(even_q_norm_g[i]), row(even_kv_norm_g[i]), f"qkv_bwd_{tag}")
            gq = _matmul(sv["qn"], dqp, "tn", BF16, f"g_w_uq_{tag}", tn=1024)
            gkv = _matmul(sv["kvn"], dkvp, "tn", BF16, f"g_w_ukv_{tag}")
            g_e_w_uq[i] = jnp.moveaxis(gq.reshape(QL, HEADS, HEAD_PAD)[..., :QK_NOPE + QK_ROPE], 1, 0)
            g_e_w_ukv[i] = jnp.moveaxis(jnp.concatenate(
                [gkv[:, :HEADS * HEAD_PAD].reshape(KVL, HEADS, HEAD_PAD)[..., :QK_NOPE],
                 gkv[:, HEADS * HEAD_PAD:].reshape(KVL, HEADS, V_HEAD)], axis=-1), 1, 0)
            dz = _even_dz(dab, du, sv["z"], dag, dbg, dcq, dckv, dkr, f"even_dz_{tag}")
            gt = _matmul(dz, sv["h"], "tn", BF16, f"g_w_in_{tag}", tn=1024)
            g_e_w_in[i] = jnp.concatenate([gt[:2048], gt[2560:2944], gt[2944 + QK_NOPE:2944 + QK_NOPE + QK_ROPE],
                                           gt[2048:2560]], axis=0).reshape(N_DEV, -1, D)
            names = [("even_w_in", i), ("even_w_uq", i), ("even_w_ukv", i)]
            parts = [g_e_w_in[i], g_e_w_uq[i], g_e_w_ukv[i]]
            if layer == 0:
                names, parts = names + [("small", 0)], parts + [small_slab()]
            else:
                names, parts = names + [("even_w_out", i)], parts + [g_e_w_out[i]]
            bw_token = start_scatter(tag, names, parts)
            w_dh = e_w_in_k[i] + bw_token[0, 0].astype(BF16) if layer == 0 else e_w_in_k[i]
            dh = _matmul(dz, w_dh, "nn", F32, f"d_h_{tag}", tn=1024)
        else:
            dyi = _matmul(dy, o_w_out[i], "nt", F32, f"d_yin_{tag}", tn=1024)
            g_o_w_out[i] = _matmul(sv["yin"], dy, "tn", BF16, f"g_w_out_{tag}", tn=1024).reshape(N_DEV, -1, D)
            dcv, dsg, g_olg[i], g_olb[i] = _odd_bwd_norm(dyi, sv["cv"], sv["z"], row(o_ln_g[i]), row(o_ln_b[i]),
                                                         f"odd_norm_bwd_{tag}")
            du, dcw, g_ocb[i] = _conv_bwd(dcv, sv["u"], sv["cw"], CONF_KERNEL, f"conv_bwd_{tag}")
            g_ocw[i] = dcw[:CONF_KERNEL]
            dz = _odd_dz(du, sv["z"], dsg, f"odd_dz_{tag}")
            g_o_w_in[i] = _matmul(sv["h"], dz, "tn", BF16, f"g_w_in_{tag}", tn=odd_w_in.shape[-1],
                                  split_n=True)
            bw_token = start_scatter(tag, [("odd_w_in", i), ("odd_w_out", i)], [g_o_w_in[i], g_o_w_out[i]])
            dh = _matmul(dz, o_w_in[i], "nt", F32, f"d_h_{tag}", tn=1024)
        g_row = row(pre_norm_g[layer]) + bw_token[0, 0]
        if layer > 0:
            (dx, dy, dshift, dscale, g_pre[layer], dgate_prev, g_post[layer - 1]) = _pre_post_norm_bwd(
                dh, sv["x"], dx, g_row, row(scale[layer]), saved[layer - 1]["y"], row(post_norm_g[layer - 1]),
                row(gate[layer - 1]), f"pre_post_norm_bwd_{tag}")
        else:
            dx, dshift, dscale, g_pre[layer] = _pre_norm_bwd(dh, sv["x"], dx, g_row, row(scale[layer]),
                                                             f"pre_norm_bwd_{tag}")
            dgate_prev = None
        dmod[layer] = jnp.concatenate([dshift, dscale, dgate], axis=-1)
        dgate = dgate_prev
    grad_x = dx.reshape(1, S, D)

    rep_g = [jnp.concatenate(dmod, 0), jnp.concatenate(g_pre, 0), jnp.concatenate(g_post, 0),
             jnp.stack(g_scb), jnp.stack(g_qg), jnp.stack(g_kvg)]
    rep_w = [ada_b, pre_norm_g, post_norm_g, even_sc_conv_b, even_q_norm_g, even_kv_norm_g]
    rep_m = [m_ada_b, m_pre_norm_g, m_post_norm_g, m_even_sc_conv_b, m_even_q_norm_g, m_even_kv_norm_g]
    rep_v = [v_ada_b, v_pre_norm_g, v_post_norm_g, v_even_sc_conv_b, v_even_q_norm_g, v_even_kv_norm_g]
    rep_shapes = [w.shape for w in rep_w]
    rep_all = _exchange([_pack(rep_g, F32, SUBLANES)], False, "gather_small_grads")[0]
    rep_out = _adamw(rep_all, _pack(rep_w, F32, SUBLANES), _pack(rep_m, F32, SUBLANES), _pack(rep_v, F32, SUBLANES),
                     "adamw_replicated")
    rep_res = [_unpack(o.reshape(-1), rep_shapes) for o in rep_out]

    dmod_all = rep_all.reshape(N_DEV, -1)[:, :L * 3 * D].reshape(N_DEV, L, 3 * D)
    dmod_cols = jnp.moveaxis(lax.dynamic_slice_in_dim(dmod_all, me * n_ada, n_ada, axis=2), 0, 1)
    g_ada_w = _ada_bwd(c_all.T, dmod_cols)
    ada_out = _adamw(g_ada_w.reshape(1, -1, PACK_COLS), ada_w.reshape(-1, PACK_COLS),
                     m_ada_w.reshape(-1, PACK_COLS), v_ada_w.reshape(-1, PACK_COLS), "adamw_ada_w")
    ada_res = [o.reshape(ada_w.shape) for o in ada_out]

    sm_m = [m_even_sc_conv_w, m_odd_conv_w, m_odd_conv_b, m_odd_ln_g, m_odd_ln_b]
    sm_v = [v_even_sc_conv_w, v_odd_conv_w, v_odd_conv_b, v_odd_ln_g, v_odd_ln_b]
    sm_shapes = [w.shape for w in sm_w]
    state = {"even_w_in": (even_w_in, m_even_w_in, v_even_w_in), "even_w_uq": (even_w_uq, m_even_w_uq, v_even_w_uq),
             "even_w_ukv": (even_w_ukv, m_even_w_ukv, v_even_w_ukv), "even_w_out": (even_w_out, m_even_w_out, v_even_w_out),
             "odd_w_in": (odd_w_in, m_odd_w_in, v_odd_w_in), "odd_w_out": (odd_w_out, m_odd_w_out, v_odd_w_out)}
    big_res = {name: [[None] * len(state[name][0]) for _ in range(4)] for name in state}
    after = [bw_token, grad_x, rep_out[0], ada_out[0]]
    sm_res = None
    for tag, names, handle, own in scatters:
        _, landed = _exchange_wait(handle, True, after, f"scatter_grads_wait_{tag}")
        after = []
        for a, (name, i) in enumerate(names):
            if name == "small":
                sm_out = _adamw([own[a], landed[a]], _pack(sm_w, F32, SUBLANES), _pack(sm_m, F32, SUBLANES),
                                _pack(sm_v, F32, SUBLANES), "adamw_small_sharded")
                sm_res = [_unpack(o.reshape(-1), sm_shapes) for o in sm_out]
                continue
            transposed = name == "even_w_in"
            wmv = [t[i].T if transposed else t[i] for t in state[name]]
            res = _adamw([own[a], landed[a]], *wmv, f"adamw_{name}_{i}")
            for kind in range(4):
                big_res[name][kind][i] = res[kind].T if transposed else res[kind]
            after += [big_res[name][kind][i] for kind in range(4)]
    sh_res = [dict(zip(["even_sc_conv_w", "odd_conv_w", "odd_conv_b", "odd_ln_g", "odd_ln_b"], sm_res[kind]))
              for kind in range(4)]
    for name in state:
        for kind in range(4):
            sh_res[kind][name] = jnp.stack(big_res[name][kind])

    order = ["ada_w", "ada_b", "pre_norm_g", "post_norm_g", "even_w_in", "even_sc_conv_w", "even_sc_conv_b",
             "even_q_norm_g", "even_kv_norm_g", "even_w_uq", "even_w_ukv", "even_w_out", "odd_w_in", "odd_conv_w",
             "odd_conv_b", "odd_ln_g", "odd_ln_b", "odd_w_out"]
    rep_names = ["ada_b", "pre_norm_g", "post_norm_g", "even_sc_conv_b", "even_q_norm_g", "even_kv_norm_g"]
    outs = [loss, grad_x]
    for kind in range(4):
        for name in order:
            if name == "ada_w":
                outs.append(ada_res[kind])
            elif name in rep_names:
                outs.append(rep_res[kind][rep_names.index(name)])
            else:
                outs.append(sh_res[kind][name])
    return tuple(outs)
```

```python
import functools
import math

import jax
import jax.numpy as jnp
from jax import lax
from jax.experimental import pallas as pl
from jax.experimental.pallas import tpu as pltpu

F32 = jnp.float32
BF16 = jnp.bfloat16
MESH_AXES = ("x", "y", "c")
N_DEV = 8
EPS = 1e-6
CHUNK = 64
HEADS = 8
QK_NOPE = 64
QK_ROPE = 32
V_HEAD = 64
HEAD_PAD = 128
ROPE_THETA = 10000.0
SC_KERNEL = 3
CONF_KERNEL = 31
LANES = 128
SUBLANES = 8
PACK_COLS = 1024
VMEM_LIMIT = 48 * 1024 * 1024
NEG = -1e30

ADAM_LR = 0.001
ADAM_B1 = 0.9
ADAM_B2 = 0.999
ADAM_EPS = 1e-08
ADAM_WD = 0.01
ADAM_STEP = 10


def _cparams():
    return pltpu.CompilerParams(vmem_limit_bytes=VMEM_LIMIT)


def _sigmoid(x):
    return 1.0 / (1.0 + jnp.exp(-x))


def _f32(ref):
    return ref[...].astype(F32)


def _silu(x):
    return x * _sigmoid(x)


def _dsilu(x):
    s = _sigmoid(x)
    return s * (1.0 + x * (1.0 - s))


def _rows(T, width, cb=0):
    return pl.BlockSpec((T, width), lambda i: (i, cb))


def _const(shape):
    nd = len(shape)
    return pl.BlockSpec(shape, lambda i: (0,) * nd)


def _row_tile(S):
    return min(256, S)


def _exchange(srcs, scatter, name):
    n = len(srcs)
    shapes = [tuple(s.shape[1:]) if scatter else tuple(s.shape) for s in srcs]

    def body(*refs):
        src_refs, out_refs = refs[:n], refs[n:2 * n]
        send_sems, recv_sems, local_sems = refs[2 * n:]
        x, y, c = lax.axis_index("x"), lax.axis_index("y"), lax.axis_index("c")
        me = 4 * x + 2 * y + c
        owns, copies = [], []
        for a in range(n):
            def piece(d, a=a):
                return src_refs[a].at[d] if scatter else src_refs[a]

            own = pltpu.make_async_copy(piece(me), out_refs[a].at[me], local_sems.at[a])
            own.start()
            owns.append(own)
            for k in range(1, N_DEV):
                px, py, pc = x ^ ((k >> 2) & 1), y ^ ((k >> 1) & 1), c ^ (k & 1)
                peer = 4 * px + 2 * py + pc
                sem = a * (N_DEV - 1) + k - 1
                cp = pltpu.make_async_remote_copy(
                    src_ref=piece(peer), dst_ref=out_refs[a].at[me],
                    send_sem=send_sems.at[sem], recv_sem=recv_sems.at[sem],
                    device_id=(px, py, pc), device_id_type=pl.DeviceIdType.MESH)
                cp.start()
                arrival = pltpu.make_async_remote_copy(
                    src_ref=piece(peer), dst_ref=out_refs[a].at[peer],
                    send_sem=send_sems.at[sem], recv_sem=recv_sems.at[sem],
                    device_id=(x, y, c), device_id_type=pl.DeviceIdType.MESH)
                copies.append((cp, arrival))
        for _, arrival in copies:
            arrival.wait_recv()
        for cp, _ in copies:
            cp.wait_send()
        for own in owns:
            own.wait()

    return pl.pallas_call(
        body, name=name,
        out_shape=tuple(jax.ShapeDtypeStruct((N_DEV,) + shp, s.dtype) for shp, s in zip(shapes, srcs)),
        in_specs=[pl.BlockSpec(memory_space=pl.ANY)] * n,
        out_specs=tuple(pl.BlockSpec(memory_space=pl.ANY) for _ in range(n)),
        scratch_shapes=[pltpu.SemaphoreType.DMA((n * (N_DEV - 1),)),
                        pltpu.SemaphoreType.DMA((n * (N_DEV - 1),)),
                        pltpu.SemaphoreType.DMA((n,))],
    )(*srcs)


_HBM = pl.BlockSpec(memory_space=pltpu.HBM)
_SEM = pl.BlockSpec(memory_space=pltpu.SEMAPHORE)


def _peer(k):
    x, y, c = lax.axis_index("x"), lax.axis_index("y"), lax.axis_index("c")
    return x ^ ((k >> 2) & 1), y ^ ((k >> 1) & 1), c ^ (k & 1)


def _exchange_start(srcs, scatter, name):
    n = len(srcs)
    shapes = [tuple(s.shape[1:]) if scatter else tuple(s.shape) for s in srcs]
    slots = N_DEV - 1 if scatter else N_DEV
    lands = [lax.empty((slots,) + shp, s.dtype) for shp, s in zip(shapes, srcs)]
    if not scatter:
        here = 4 * lax.axis_index("x") + 2 * lax.axis_index("y") + lax.axis_index("c")
        lands = [lax.dynamic_update_index_in_dim(l, s, here, 0) for l, s in zip(lands, srcs)]

    def body(*refs):
        src_refs, land_refs = refs[:n], refs[n:2 * n]
        send_sems, recv_sems = refs[2 * n], refs[2 * n + 1]
        token = refs[4 * n + 2]
        me = 4 * lax.axis_index("x") + 2 * lax.axis_index("y") + lax.axis_index("c")
        for a in range(n):
            for k in range(1, N_DEV):
                px, py, pc = _peer(k)
                peer = 4 * px + 2 * py + pc
                pltpu.make_async_remote_copy(
                    src_ref=src_refs[a].at[peer] if scatter else src_refs[a],
                    dst_ref=land_refs[a].at[k - 1] if scatter else land_refs[a].at[me],
                    send_sem=send_sems.at[a * (N_DEV - 1) + k - 1], recv_sem=recv_sems.at[a * (N_DEV - 1) + k - 1],
                    device_id=(px, py, pc), device_id_type=pl.DeviceIdType.MESH).start()
        token[...] = jnp.zeros_like(token)

    hbm = lambda arrs: [pltpu.HBM(a.shape, a.dtype) for a in arrs]
    out = pl.pallas_call(
        body, name=name,
        out_shape=(pltpu.SemaphoreType.DMA((n * (N_DEV - 1),)), pltpu.SemaphoreType.DMA((n * (N_DEV - 1),)),
                   *hbm(srcs), *hbm(lands), jax.ShapeDtypeStruct((SUBLANES, LANES), F32)),
        in_specs=[_HBM] * (2 * n),
        out_specs=(_SEM, _SEM, *([_HBM] * (2 * n)), pl.BlockSpec(memory_space=pltpu.VMEM)),
        input_output_aliases={a: 2 + a for a in range(2 * n)},
        compiler_params=pltpu.CompilerParams(has_side_effects=pltpu.SideEffectType.DATAFLOW_SIDE_EFFECTING),
    )(*[pltpu.with_memory_space_constraint(s, pltpu.HBM) for s in srcs],
      *[pltpu.with_memory_space_constraint(l, pltpu.HBM) for l in lands])
    return (out[0], out[1], list(out[2:2 + n]), list(out[2 + n:2 + 2 * n])), out[2 + 2 * n]


def _exchange_wait(handle, scatter, after, name):
    send_sems, recv_sems, srcs, lands = handle
    n = len(srcs)
    after = list(after) if isinstance(after, (list, tuple)) else [after]

    def body(*refs):
        src_refs, land_refs = refs[:n], refs[n:2 * n]
        send_sems, recv_sems = refs[2 * n], refs[2 * n + 1]
        for a in range(n):
            for k in range(1, N_DEV):
                px, py, pc = _peer(k)
                peer = 4 * px + 2 * py + pc
                cp = pltpu.make_async_remote_copy(
                    src_ref=src_refs[a].at[peer] if scatter else src_refs[a],
                    dst_ref=land_refs[a].at[k - 1] if scatter else land_refs[a].at[peer],
                    send_sem=send_sems.at[a * (N_DEV - 1) + k - 1], recv_sem=recv_sems.at[a * (N_DEV - 1) + k - 1],
                    device_id=(px, py, pc), device_id_type=pl.DeviceIdType.MESH)
                cp.wait_send()
                cp.wait_recv()

    out = pl.pallas_call(
        body, name=name,
        out_shape=tuple(pltpu.HBM(a.shape, a.dtype) for a in srcs + lands),
        in_specs=[_HBM] * (2 * n) + [_SEM, _SEM] + [pl.BlockSpec(memory_space=pl.ANY)] * len(after),
        out_specs=tuple([_HBM] * (2 * n)),
        input_output_aliases={a: a for a in range(2 * n)},
        compiler_params=pltpu.CompilerParams(has_side_effects=pltpu.SideEffectType.DATAFLOW_SIDE_EFFECTING),
    )(*srcs, *lands, send_sems, recv_sems, *after)
    return list(out[:n]), list(out[n:])


def _pack(parts, dtype, row_mult):
    flat = jnp.concatenate([p.reshape(-1).astype(dtype) for p in parts])
    n = flat.shape[0]
    rows = -(-n // PACK_COLS)
    rows = -(-rows // row_mult) * row_mult
    flat = jnp.pad(flat, (0, rows * PACK_COLS - n))
    return flat.reshape(rows, PACK_COLS)


def _unpack(flat, shapes):
    out, off = [], 0
    for shp in shapes:
        n = math.prod(shp)
        out.append(flat[..., off:off + n].reshape(flat.shape[:-1] + tuple(shp)))
        off += n
    return out


_DIMS = {"nn": (((1,), (0,)), ((), ())), "nt": (((1,), (1,)), ((), ())), "tn": (((0,), (0,)), ((), ()))}


def _matmul(a, b, mode, out_dtype, name, tm=512, tn=512, tk=None, split_n=False):
    if mode == "nn":
        (M, K), (_, N) = a.shape, b.shape
    elif mode == "nt":
        (M, K), (N, _) = a.shape, b.shape
    else:
        (K, M), (_, N) = a.shape, b.shape
    tm, tn = min(tm, M), min(tn, N)
    tk = K if tk is None else min(tk, K)
    nk = K // tk
    assert M % tm == 0 and N % tn == 0 and K % tk == 0, (name, a.shape, b.shape)

    def body(a_ref, b_ref, o_ref, *scratch):
        p = lax.dot_general(a_ref[...].astype(BF16), b_ref[...].astype(BF16), _DIMS[mode],
                            preferred_element_type=F32)
        if split_n:
            o_ref[0] = p.astype(out_dtype)
        elif nk == 1:
            o_ref[...] = p.astype(out_dtype)
        else:
            acc = scratch[0]
            k = pl.program_id(2)

            @pl.when(k == 0)
            def _():
                acc[...] = p

            @pl.when(k > 0)
            def _():
                acc[...] += p

            @pl.when(k == nk - 1)
            def _():
                o_ref[...] = acc[...].astype(out_dtype)

    a_spec = (pl.BlockSpec((tk, tm), lambda i, j, k: (k, i)) if mode == "tn"
              else pl.BlockSpec((tm, tk), lambda i, j, k: (i, k)))
    b_spec = (pl.BlockSpec((tn, tk), lambda i, j, k: (j, k)) if mode == "nt"
              else pl.BlockSpec((tk, tn), lambda i, j, k: (k, j)))
    return pl.pallas_call(
        body, name=name, grid=(M // tm, N // tn, nk),
        out_shape=jax.ShapeDtypeStruct((N // tn, M, tn) if split_n else (M, N), out_dtype),
        in_specs=[a_spec, b_spec],
        out_specs=(pl.BlockSpec((1, tm, tn), lambda i, j, k: (j, i, 0)) if split_n
                   else pl.BlockSpec((tm, tn), lambda i, j, k: (i, j))),
        scratch_shapes=[pltpu.VMEM((tm, tn), F32)] if nk > 1 else [],
        compiler_params=_cparams(),
    )(a, b)


def _ada_fwd(c_all, ada_w, ada_b_cols):
    L, D, n = ada_w.shape

    def body(c_ref, w_ref, b_ref, o_ref):
        act = _silu(c_ref[...]).astype(BF16)
        o_ref[0] = jnp.dot(act, w_ref[0].astype(BF16), preferred_element_type=F32) + b_ref[0]

    return pl.pallas_call(
        body, name="ada_fwd", grid=(L,),
        out_shape=jax.ShapeDtypeStruct((L, N_DEV, n), F32),
        in_specs=[pl.BlockSpec((N_DEV, D), lambda l: (0, 0)),
                  pl.BlockSpec((1, D, n), lambda l: (l, 0, 0)),
                  pl.BlockSpec((1, 1, n), lambda l: (l, 0, 0))],
        out_specs=pl.BlockSpec((1, N_DEV, n), lambda l: (l, 0, 0)),
        compiler_params=_cparams(),
    )(c_all, ada_w, ada_b_cols)


def _ada_bwd(c_all_t, dmod_cols):
    D = c_all_t.shape[0]
    L, _, n = dmod_cols.shape

    def body(c_ref, d_ref, o_ref):
        act = _silu(c_ref[...])
        dm = d_ref[0]
        acc = act[:, 0:1] * dm[0:1, :]
        for b in range(1, N_DEV):
            acc = acc + act[:, b:b + 1] * dm[b:b + 1, :]
        o_ref[0] = acc

    return pl.pallas_call(
        body, name="ada_bwd", grid=(L,),
        out_shape=jax.ShapeDtypeStruct((L, D, n), F32),
        in_specs=[pl.BlockSpec((D, N_DEV), lambda l: (0, 0)),
                  pl.BlockSpec((1, N_DEV, n), lambda l: (l, 0, 0))],
        out_specs=pl.BlockSpec((1, D, n), lambda l: (l, 0, 0)),
        compiler_params=_cparams(),
    )(c_all_t, dmod_cols)


def _rope_tables(pos_col, inv_lane):
    S = pos_col.shape[0]
    T = _row_tile(S)
    half = QK_ROPE // 2

    def body(p_ref, f_ref, c_ref, up_ref, dn_ref):
        ang = p_ref[...] * f_ref[...]
        lane = lax.broadcasted_iota(jnp.int32, ang.shape, 1)
        first = (lane >= QK_NOPE) & (lane < QK_NOPE + half)
        second = (lane >= QK_NOPE + half) & (lane < QK_NOPE + QK_ROPE)
        cs, sn = jnp.cos(ang), jnp.sin(ang)
        c_ref[...] = jnp.where(first | second, cs, 1.0)
        up_ref[...] = jnp.where(first, -sn, 0.0)
        dn_ref[...] = jnp.where(second, sn, 0.0)

    tab = jax.ShapeDtypeStruct((S, HEAD_PAD), F32)
    return pl.pallas_call(
        body, name="rope_tables", grid=(S // T,),
        out_shape=(tab, tab, tab),
        in_specs=[_rows(T, 1), _const((1, HEAD_PAD))],
        out_specs=(_rows(T, HEAD_PAD),) * 3,
        compiler_params=_cparams(),
    )(pos_col, inv_lane)


def _rope(blk, ct, ut, dt):
    half = QK_ROPE // 2
    up = pltpu.roll(blk, HEAD_PAD - half, 1)
    dn = pltpu.roll(blk, half, 1)
    return blk * ct + up * ut + dn * dt


def _rope_t(d, ct, ut, dt):
    half = QK_ROPE // 2
    return d * ct + pltpu.roll(d * ut, half, 1) + pltpu.roll(d * dt, HEAD_PAD - half, 1)


def _pre_norm(x, g, scale, shift, name):
    S, D = x.shape
    T = _row_tile(S)

    def body(x_ref, g_ref, sc_ref, sh_ref, h_ref):
        xv = x_ref[...]
        rstd = lax.rsqrt(jnp.mean(xv * xv, axis=-1, keepdims=True) + EPS)
        h_ref[...] = ((xv * rstd) * g_ref[...] * (1.0 + sc_ref[...]) + sh_ref[...]).astype(BF16)

    return pl.pallas_call(
        body, name=name, grid=(S // T,),
        out_shape=jax.ShapeDtypeStruct((S, D), BF16),
        in_specs=[_rows(T, D), _const((1, D)), _const((1, D)), _const((1, D))],
        out_specs=_rows(T, D), compiler_params=_cparams(),
    )(x, g, scale, shift)


def _post_norm(x, y, g, gate, name):
    S, D = x.shape
    T = _row_tile(S)

    def body(x_ref, y_ref, g_ref, gt_ref, o_ref):
        yv = y_ref[...]
        rstd = lax.rsqrt(jnp.mean(yv * yv, axis=-1, keepdims=True) + EPS)
        o_ref[...] = x_ref[...] + gt_ref[...] * ((yv * rstd) * g_ref[...])

    return pl.pallas_call(
        body, name=name, grid=(S // T,),
        out_shape=jax.ShapeDtypeStruct((S, D), F32),
        in_specs=[_rows(T, D), _rows(T, D), _const((1, D)), _const((1, D))],
        out_specs=_rows(T, D), compiler_params=_cparams(),
    )(x, y, g, gate)


def _fold8(v):
    T, C = v.shape
    return v.reshape(T // SUBLANES, SUBLANES, C).sum(axis=0)


def _col_sums(n_sums, body_fn, ins, in_specs, outs, out_specs, S, T, widths, name):
    n_in, n_out = len(ins), len(outs)
    nt = S // T

    def body(*refs):
        in_refs = refs[:n_in]
        out_refs = refs[n_in:n_in + n_out]
        sum_refs = refs[n_in + n_out:n_in + n_out + n_sums]
        accs = refs[n_in + n_out + n_sums:]
        i = pl.program_id(0)
        terms = body_fn(in_refs, out_refs)

        @pl.when(i == 0)
        def _():
            for acc, t in zip(accs, terms):
                acc[...] = _fold8(t)

        @pl.when(i > 0)
        def _():
            for acc, t in zip(accs, terms):
                acc[...] += _fold8(t)

        @pl.when(i == nt - 1)
        def _():
            for acc, s_ref in zip(accs, sum_refs):
                s_ref[...] = jnp.sum(acc[...], axis=0, keepdims=True)

    return pl.pallas_call(
        body, name=name, grid=(nt,),
        out_shape=tuple(outs) + tuple(jax.ShapeDtypeStruct((1, w), F32) for w in widths),
        in_specs=in_specs,
        out_specs=tuple(out_specs) + tuple(_const((1, w)) for w in widths),
        scratch_shapes=[pltpu.VMEM((SUBLANES, w), F32) for w in widths],
        compiler_params=_cparams(),
    )(*ins)


def _post_norm_bwd(dxo, y, g, gate, name):
    S, D = y.shape
    T = _row_tile(S)

    def fn(ins, outs):
        dxo_ref, y_ref, g_ref, gt_ref = ins
        yv, dv = y_ref[...], dxo_ref[...]
        rstd = lax.rsqrt(jnp.mean(yv * yv, axis=-1, keepdims=True) + EPS)
        yh = yv * rstd
        dn = dv * gt_ref[...]
        dyh = dn * g_ref[...]
        outs[0][...] = (rstd * (dyh - yh * jnp.mean(dyh * yh, axis=-1, keepdims=True))).astype(BF16)
        return [dv * (yh * g_ref[...]), dn * yh]

    return _col_sums(2, fn, [dxo, y, g, gate],
                     [_rows(T, D), _rows(T, D), _const((1, D)), _const((1, D))],
                     [jax.ShapeDtypeStruct((S, D), BF16)], [_rows(T, D)], S, T, [D, D], name)


def _pre_norm_bwd(dh, x, dxo, g, scale, name):
    S, D = x.shape
    T = _row_tile(S)

    def fn(ins, outs):
        dh_ref, x_ref, dxo_ref, g_ref, sc_ref = ins
        xv, dv = x_ref[...], dh_ref[...]
        rstd = lax.rsqrt(jnp.mean(xv * xv, axis=-1, keepdims=True) + EPS)
        xh = xv * rstd
        dr = dv * (1.0 + sc_ref[...])
        dxh = dr * g_ref[...]
        outs[0][...] = dxo_ref[...] + rstd * (dxh - xh * jnp.mean(dxh * xh, axis=-1, keepdims=True))
        return [dv, dv * (xh * g_ref[...]), dr * xh]

    return _col_sums(3, fn, [dh, x, dxo, g, scale],
                     [_rows(T, D), _rows(T, D), _rows(T, D), _const((1, D)), _const((1, D))],
                     [jax.ShapeDtypeStruct((S, D), F32)], [_rows(T, D)], S, T, [D, D, D], name)


def _post_pre_norm(x, y, g_post, gate, g_pre, scale, shift, name):
    S, D = x.shape
    T = _row_tile(S)

    def body(x_ref, y_ref, gp_ref, gt_ref, g_ref, sc_ref, sh_ref, xn_ref, h_ref):
        yv = y_ref[...]
        rstd_y = lax.rsqrt(jnp.mean(yv * yv, axis=-1, keepdims=True) + EPS)
        xn = x_ref[...] + gt_ref[...] * ((yv * rstd_y) * gp_ref[...])
        xn_ref[...] = xn
        rstd = lax.rsqrt(jnp.mean(xn * xn, axis=-1, keepdims=True) + EPS)
        h_ref[...] = ((xn * rstd) * g_ref[...] * (1.0 + sc_ref[...]) + sh_ref[...]).astype(BF16)

    return pl.pallas_call(
        body, name=name, grid=(S // T,),
        out_shape=(jax.ShapeDtypeStruct((S, D), F32), jax.ShapeDtypeStruct((S, D), BF16)),
        in_specs=[_rows(T, D), _rows(T, D)] + [_const((1, D))] * 5,
        out_specs=(_rows(T, D), _rows(T, D)), compiler_params=_cparams(),
    )(x, y, g_post, gate, g_pre, scale, shift)


def _pre_post_norm_bwd(dh, x, dxo, g_pre, scale, y_prev, g_post_prev, gate_prev, name):
    S, D = x.shape
    T = _row_tile(S)

    def fn(ins, outs):
        dh_ref, x_ref, dxo_ref, g_ref, sc_ref, y_ref, gp_ref, gt_ref = ins
        xv, dv = x_ref[...], dh_ref[...]
        rstd = lax.rsqrt(jnp.mean(xv * xv, axis=-1, keepdims=True) + EPS)
        xh = xv * rstd
        dr = dv * (1.0 + sc_ref[...])
        dxh = dr * g_ref[...]
        dx = dxo_ref[...] + rstd * (dxh - xh * jnp.mean(dxh * xh, axis=-1, keepdims=True))
        outs[0][...] = dx
        yv = y_ref[...]
        rstd_y = lax.rsqrt(jnp.mean(yv * yv, axis=-1, keepdims=True) + EPS)
        yh = yv * rstd_y
        dn = dx * gt_ref[...]
        dyh = dn * gp_ref[...]
        outs[1][...] = (rstd_y * (dyh - yh * jnp.mean(dyh * yh, axis=-1, keepdims=True))).astype(BF16)
        return [dv, dv * (xh * g_ref[...]), dr * xh, dx * (yh * gp_ref[...]), dn * yh]

    return _col_sums(5, fn, [dh, x, dxo, g_pre, scale, y_prev, g_post_prev, gate_prev],
                     [_rows(T, D), _rows(T, D), _rows(T, D), _const((1, D)), _const((1, D)),
                      _rows(T, D), _const((1, D)), _const((1, D))],
                     [jax.ShapeDtypeStruct((S, D), F32), jax.ShapeDtypeStruct((S, D), BF16)],
                     [_rows(T, D), _rows(T, D)], S, T, [D] * 5, name)


def _loss_post_norm_bwd(x, y, g_post, gate, target, name):
    S, D = x.shape
    T = _row_tile(S)

    def fn(ins, outs):
        x_ref, y_ref, gp_ref, gt_ref, t_ref = ins
        yv = y_ref[...]
        rstd_y = lax.rsqrt(jnp.mean(yv * yv, axis=-1, keepdims=True) + EPS)
        yh = yv * rstd_y
        e = x_ref[...] + gt_ref[...] * (yh * gp_ref[...]) - t_ref[...]
        dx = e * (1.0 / D)
        outs[0][...] = dx
        dn = dx * gt_ref[...]
        dyh = dn * gp_ref[...]
        outs[1][...] = (rstd_y * (dyh - yh * jnp.mean(dyh * yh, axis=-1, keepdims=True))).astype(BF16)
        return [e * e, dx * (yh * gp_ref[...]), dn * yh]

    return _col_sums(3, fn, [x, y, g_post, gate, target],
                     [_rows(T, D), _rows(T, D), _const((1, D)), _const((1, D)), _rows(T, D)],
                     [jax.ShapeDtypeStruct((S, D), F32), jax.ShapeDtypeStruct((S, D), BF16)],
                     [_rows(T, D), _rows(T, D)], S, T, [D] * 3, name)


def _scaled_total(v, coef, name):
    def body(v_ref, o_ref):
        o_ref[...] = jnp.broadcast_to(jnp.sum(v_ref[...], axis=1, keepdims=True) * coef, (1, LANES))

    return pl.pallas_call(body, name=name, out_shape=jax.ShapeDtypeStruct((1, LANES), F32))(v)


def _loss_head(x, target):
    S, D = x.shape
    T = _row_tile(S)
    nt = S // T

    def body(x_ref, t_ref, l_ref, dx_ref, acc):
        i = pl.program_id(0)
        e = x_ref[...] - t_ref[...]
        dx_ref[...] = e * (1.0 / D)
        part = _fold8(e * e)

        @pl.when(i == 0)
        def _():
            acc[...] = part

        @pl.when(i > 0)
        def _():
            acc[...] += part

        @pl.when(i == nt - 1)
        def _():
            tot = jnp.sum(jnp.sum(acc[...], axis=0, keepdims=True), axis=1, keepdims=True)
            l_ref[...] = jnp.broadcast_to(tot * (0.5 / D), (1, LANES))

    return pl.pallas_call(
        body, name="loss_head", grid=(nt,),
        out_shape=(jax.ShapeDtypeStruct((1, LANES), F32), jax.ShapeDtypeStruct((S, D), F32)),
        in_specs=[_rows(T, D), _rows(T, D)],
        out_specs=(_const((1, LANES)), _rows(T, D)),
        scratch_shapes=[pltpu.VMEM((SUBLANES, D), F32)],
        compiler_params=_cparams(),
    )(x, target)


CONV_ROWS = 64


def _conv_halo(K):
    return SUBLANES if K - 1 <= SUBLANES else 32


def _conv_fwd(u, w, b, K, name):
    S, C = u.shape
    KP = w.shape[0]
    T, HB, RS = min(512, S), _conv_halo(K), CONV_ROWS
    ratio = T // HB

    def body(u_ref, h_ref, w_ref, b_ref, o_ref, ext):
        i = pl.program_id(1)
        ext[0:HB, :] = jnp.where(i > 0, h_ref[...], 0.0)
        ext[HB:HB + T, :] = u_ref[...]
        for r0 in range(0, T, RS):
            acc = jnp.broadcast_to(b_ref[...], (RS, LANES))
            for k in range(K):
                off = HB - (K - 1) + k + r0
                acc = acc + w_ref[k:k + 1, :] * ext[off:off + RS, :]
            o_ref[r0:r0 + RS, :] = acc

    return pl.pallas_call(
        body, name=name, grid=(C // LANES, S // T),
        out_shape=jax.ShapeDtypeStruct((S, C), F32),
        in_specs=[pl.BlockSpec((T, LANES), lambda c, i: (i, c)),
                  pl.BlockSpec((HB, LANES), lambda c, i: (jnp.maximum(i * ratio - 1, 0), c)),
                  pl.BlockSpec((KP, LANES), lambda c, i: (0, c)),
                  pl.BlockSpec((1, LANES), lambda c, i: (0, c))],
        out_specs=pl.BlockSpec((T, LANES), lambda c, i: (i, c)),
        scratch_shapes=[pltpu.VMEM((HB + T, LANES), F32)],
        compiler_params=_cparams(),
    )(u, u, w, b)


def _conv_bwd(d, u, w, K, name):
    S, C = u.shape
    KP = w.shape[0]
    T, HB, RS = min(512, S), _conv_halo(K), CONV_ROWS
    ratio = T // HB
    nt = S // T
    last_halo = S // HB - 1

    def body(d_ref, dn_ref, u_ref, up_ref, w_ref, du_ref, dw_ref, db_ref, extd, extu, dws, dbs):
        i = pl.program_id(1)
        extd[0:T, :] = d_ref[...]
        extd[T:T + HB, :] = jnp.where(i < nt - 1, dn_ref[...], 0.0)
        extu[0:HB, :] = jnp.where(i > 0, up_ref[...], 0.0)
        extu[HB:HB + T, :] = u_ref[...]

        @pl.when(i == 0)
        def _():
            dws[...] = jnp.zeros_like(dws)
            dbs[...] = jnp.zeros_like(dbs)

        for r0 in range(0, T, RS):
            acc = jnp.zeros((RS, LANES), F32)
            for k in range(K):
                off = (K - 1 - k) + r0
                acc = acc + w_ref[k:k + 1, :] * extd[off:off + RS, :]
            du_ref[r0:r0 + RS, :] = acc
            dch = d_ref[r0:r0 + RS, :]
            dbs[...] += _fold8(dch)
            for k in range(K):
                off = HB - (K - 1) + k + r0
                dws[k * SUBLANES:(k + 1) * SUBLANES, :] += _fold8(dch * extu[off:off + RS, :])

        @pl.when(i == nt - 1)
        def _():
            dw_ref[...] = jnp.zeros_like(dw_ref)
            for k in range(K):
                dw_ref[k:k + 1, :] = jnp.sum(dws[k * SUBLANES:(k + 1) * SUBLANES, :], axis=0, keepdims=True)
            db_ref[...] = jnp.sum(dbs[...], axis=0, keepdims=True)

    return pl.pallas_call(
        body, name=name, grid=(C // LANES, nt),
        out_shape=(jax.ShapeDtypeStruct((S, C), F32), jax.ShapeDtypeStruct((KP, C), F32),
                   jax.ShapeDtypeStruct((1, C), F32)),
        in_specs=[pl.BlockSpec((T, LANES), lambda c, i: (i, c)),
                  pl.BlockSpec((HB, LANES), lambda c, i: (jnp.minimum((i + 1) * ratio, last_halo), c)),
                  pl.BlockSpec((T, LANES), lambda c, i: (i, c)),
                  pl.BlockSpec((HB, LANES), lambda c, i: (jnp.maximum(i * ratio - 1, 0), c)),
                  pl.BlockSpec((KP, LANES), lambda c, i: (0, c))],
        out_specs=(pl.BlockSpec((T, LANES), lambda c, i: (i, c)),
                   pl.BlockSpec((KP, LANES), lambda c, i: (0, c)),
                   pl.BlockSpec((1, LANES), lambda c, i: (0, c))),
        scratch_shapes=[pltpu.VMEM((T + HB, LANES), F32), pltpu.VMEM((HB + T, LANES), F32),
                        pltpu.VMEM((KP * SUBLANES, LANES), F32), pltpu.VMEM((SUBLANES, LANES), F32)],
        compiler_params=_cparams(),
    )(d, d, u, u, w)


SCW = 512
ZE = 3072
QL = 256
KVL = 128


def _rms_rows(x, g):
    rstd = lax.rsqrt(jnp.mean(x * x, axis=-1, keepdims=True) + EPS)
    return (x * rstd) * g


def _even_pre(z, qg, kvg, name):
    S = z.shape[0]
    T = _row_tile(S)

    def body(ac_ref, ax_ref, cq_ref, ckv_ref, qg_ref, kvg_ref, u_ref, qn_ref, kvn_ref):
        u_ref[...] = _f32(ac_ref) * _f32(ax_ref)
        qn_ref[...] = _rms_rows(_f32(cq_ref), qg_ref[...]).astype(BF16)
        kvn_ref[...] = _rms_rows(_f32(ckv_ref), kvg_ref[...]).astype(BF16)

    return pl.pallas_call(
        body, name=name, grid=(S // T,),
        out_shape=(jax.ShapeDtypeStruct((S, SCW), F32), jax.ShapeDtypeStruct((S, QL), BF16),
                   jax.ShapeDtypeStruct((S, KVL), BF16)),
        in_specs=[_rows(T, SCW, 1), _rows(T, SCW, 2), _rows(T, QL, 10), _rows(T, KVL, 22),
                  _const((1, QL)), _const((1, KVL))],
        out_specs=(_rows(T, SCW), _rows(T, QL), _rows(T, KVL)),
        compiler_params=_cparams(),
    )(z, z, z, z, qg, kvg)


def _qkv_fwd(qn, kvn, z, tabs, w_q, w_kv, name):
    S = qn.shape[0]
    T = _row_tile(S)
    HW = HEADS * HEAD_PAD
    scale = 1.0 / math.sqrt(QK_NOPE + QK_ROPE)

    def body(qn_ref, kvn_ref, kr_ref, ct_ref, ut_ref, dt_ref, wq_ref, wkv_ref, q_ref, k_ref, v_ref):
        ct, ut, dt = ct_ref[...], ut_ref[...], dt_ref[...]
        qa = jnp.dot(qn_ref[...], wq_ref[...], preferred_element_type=F32)
        kva = jnp.dot(kvn_ref[...], wkv_ref[...], preferred_element_type=F32)
        kr = _f32(kr_ref)
        for h in range(HEADS):
            sl = slice(h * HEAD_PAD, (h + 1) * HEAD_PAD)
            q_ref[:, sl] = (_rope(qa[:, sl], ct, ut, dt) * scale).astype(BF16)
            k_ref[:, sl] = _rope(kva[:, sl] + kr, ct, ut, dt).astype(BF16)
        v_ref[...] = kva[:, HW:].astype(BF16)

    return pl.pallas_call(
        body, name=name, grid=(S // T,),
        out_shape=(jax.ShapeDtypeStruct((S, HW), BF16), jax.ShapeDtypeStruct((S, HW), BF16),
                   jax.ShapeDtypeStruct((S, HEADS * V_HEAD), BF16)),
        in_specs=[_rows(T, QL), _rows(T, KVL), _rows(T, HEAD_PAD, 23),
                  _rows(T, HEAD_PAD), _rows(T, HEAD_PAD), _rows(T, HEAD_PAD),
                  _const(w_q.shape), _const(w_kv.shape)],
        out_specs=(_rows(T, HW), _rows(T, HW), _rows(T, HEADS * V_HEAD)),
        compiler_params=_cparams(),
    )(qn, kvn, z, *tabs, w_q, w_kv)


def _attn_tile(S):
    return min(256, S)


def _chunk_mask(TQ):
    r = lax.broadcasted_iota(jnp.int32, (TQ, TQ), 0) // CHUNK
    c = lax.broadcasted_iota(jnp.int32, (TQ, TQ), 1) // CHUNK
    return c <= r


_NT = (((1,), (1,)), ((), ()))
_TN = (((0,), (0,)), ((), ()))


def _attn_fwd(q, k, v, name):
    S = q.shape[0]
    TQ = _attn_tile(S)
    nq = S // TQ
    PW = 2 * HEAD_PAD

    def body(q_ref, k_ref, v_ref, o_ref, lse_ref, m_s, l_s, acc_s):
        i = pl.program_id(1)
        left = lax.broadcasted_iota(jnp.int32, (TQ, LANES), 1) < V_HEAD
        m_s[...] = jnp.full_like(m_s, NEG)
        l_s[...] = jnp.zeros_like(l_s)
        acc_s[...] = jnp.zeros_like(acc_s)
        qv = q_ref[...]

        def step(j, masked):
            r0 = pl.multiple_of(j * TQ, TQ)
            kb = k_ref[pl.ds(r0, TQ), :]
            vb = v_ref[pl.ds(r0, TQ), :]
            alphas, pvs = [], []
            for h in range(2):
                sl = slice(h * HEAD_PAD, (h + 1) * HEAD_PAD)
                s = lax.dot_general(qv[:, sl], kb[:, sl], _NT, preferred_element_type=F32)
                if masked:
                    s = jnp.where(_chunk_mask(TQ), s, NEG)
                m_prev = m_s[h]
                m_new = jnp.maximum(m_prev, jnp.max(s, axis=1, keepdims=True))
                alpha = jnp.exp(m_prev - m_new)
                p = jnp.exp(s - m_new[:, 0:1])
                l_s[h] = alpha * l_s[h] + jnp.sum(p, axis=1, keepdims=True)
                m_s[h] = m_new
                alphas.append(alpha)
                pvs.append(jnp.dot(p.astype(BF16), vb, preferred_element_type=F32))
            acc_s[...] = acc_s[...] * jnp.where(left, alphas[0], alphas[1]) + jnp.where(left, pvs[0], pvs[1])

        def loop_body(j, carry):
            step(j, False)
            return carry

        lax.fori_loop(0, i, loop_body, 0)
        step(i, True)
        o_ref[...] = acc_s[...] / jnp.where(left, l_s[0], l_s[1])
        lse_ref[...] = jnp.where(left, m_s[0] + jnp.log(l_s[0]), m_s[1] + jnp.log(l_s[1]))

    return pl.pallas_call(
        body, name=name, grid=(HEADS // 2, nq),
        out_shape=(jax.ShapeDtypeStruct((S, HEADS * V_HEAD), F32), jax.ShapeDtypeStruct((S, HEADS * V_HEAD), F32)),
        in_specs=[pl.BlockSpec((TQ, PW), lambda p, i: (i, p)),
                  pl.BlockSpec((S, PW), lambda p, i: (0, p)),
                  pl.BlockSpec((S, LANES), lambda p, i: (0, p))],
        out_specs=(pl.BlockSpec((TQ, LANES), lambda p, i: (i, p)),
                   pl.BlockSpec((TQ, LANES), lambda p, i: (i, p))),
        scratch_shapes=[pltpu.VMEM((2, TQ, LANES), F32), pltpu.VMEM((2, TQ, LANES), F32),
                        pltpu.VMEM((TQ, LANES), F32)],
        compiler_params=_cparams(),
    )(q, k, v)


def _attn_dq(q, k, v, do, lse, delta, name):
    S = q.shape[0]
    TQ = _attn_tile(S)
    nq = S // TQ
    PW = 2 * HEAD_PAD

    def body(q_ref, k_ref, v_ref, do_ref, lse_ref, dl_ref, dq_ref, acc_s):
        i = pl.program_id(1)
        left = lax.broadcasted_iota(jnp.int32, (TQ, LANES), 1) < V_HEAD
        acc_s[...] = jnp.zeros_like(acc_s)
        qv = q_ref[...]
        dov = do_ref[...]
        dos = [jnp.where(left, dov, jnp.zeros_like(dov)), jnp.where(left, jnp.zeros_like(dov), dov)]
        lses = [lse_ref[:, 0:1], lse_ref[:, V_HEAD:V_HEAD + 1]]
        dls = [dl_ref[:, 0:1], dl_ref[:, V_HEAD:V_HEAD + 1]]

        def step(j, masked):
            r0 = pl.multiple_of(j * TQ, TQ)
            kb = k_ref[pl.ds(r0, TQ), :]
            vb = v_ref[pl.ds(r0, TQ), :]
            for h in range(2):
                sl = slice(h * HEAD_PAD, (h + 1) * HEAD_PAD)
                s = lax.dot_general(qv[:, sl], kb[:, sl], _NT, preferred_element_type=F32)
                p = jnp.exp(s - lses[h])
                if masked:
                    p = jnp.where(_chunk_mask(TQ), p, 0.0)
                dp = lax.dot_general(dos[h], vb, _NT, preferred_element_type=F32)
                ds = (p * (dp - dls[h])).astype(BF16)
                acc_s[:, sl] += jnp.dot(ds, kb[:, sl], preferred_element_type=F32)

        def loop_body(j, carry):
            step(j, False)
            return carry

        lax.fori_loop(0, i, loop_body, 0)
        step(i, True)
        dq_ref[...] = acc_s[...]

    return pl.pallas_call(
        body, name=name, grid=(HEADS // 2, nq),
        out_shape=jax.ShapeDtypeStruct((S, HEADS * HEAD_PAD), F32),
        in_specs=[pl.BlockSpec((TQ, PW), lambda p, i: (i, p)),
                  pl.BlockSpec((S, PW), lambda p, i: (0, p)),
                  pl.BlockSpec((S, LANES), lambda p, i: (0, p)),
                  pl.BlockSpec((TQ, LANES), lambda p, i: (i, p)),
                  pl.BlockSpec((TQ, LANES), lambda p, i: (i, p)),
                  pl.BlockSpec((TQ, LANES), lambda p, i: (i, p))],
        out_specs=pl.BlockSpec((TQ, PW), lambda p, i: (i, p)),
        scratch_shapes=[pltpu.VMEM((TQ, PW), F32)],
        compiler_params=_cparams(),
    )(q, k, v, do, lse, delta)


def _attn_dkv(q, k, v, do, lse, delta, name):
    S = q.shape[0]
    TQ = _attn_tile(S)
    nq = S // TQ
    PW = 2 * HEAD_PAD

    def body(q_ref, k_ref, v_ref, do_ref, lse_ref, dl_ref, dk_ref, dv_ref, dk_s, dv_s):
        j = pl.program_id(1)
        left = lax.broadcasted_iota(jnp.int32, (TQ, LANES), 1) < V_HEAD
        dk_s[...] = jnp.zeros_like(dk_s)
        dv_s[...] = jnp.zeros_like(dv_s)
        kb = k_ref[...]
        vb = v_ref[...]

        def step(i, masked):
            r0 = pl.multiple_of(i * TQ, TQ)
            qb = q_ref[pl.ds(r0, TQ), :]
            dov = do_ref[pl.ds(r0, TQ), :]
            lse = lse_ref[pl.ds(r0, TQ), :]
            dl = dl_ref[pl.ds(r0, TQ), :]
            dos = [jnp.where(left, dov, jnp.zeros_like(dov)), jnp.where(left, jnp.zeros_like(dov), dov)]
            for h in range(2):
                sl = slice(h * HEAD_PAD, (h + 1) * HEAD_PAD)
                c0 = h * V_HEAD
                s = lax.dot_general(qb[:, sl], kb[:, sl], _NT, preferred_element_type=F32)
                p = jnp.exp(s - lse[:, c0:c0 + 1])
                if masked:
                    p = jnp.where(_chunk_mask(TQ), p, 0.0)
                dv_s[...] += lax.dot_general(p.astype(BF16), dos[h], _TN, preferred_element_type=F32)
                dp = lax.dot_general(dos[h], vb, _NT, preferred_element_type=F32)
                ds = (p * (dp - dl[:, c0:c0 + 1])).astype(BF16)
                dk_s[:, sl] += lax.dot_general(ds, qb[:, sl], _TN, preferred_element_type=F32)

        def loop_body(i, carry):
            step(i, False)
            return carry

        step(j, True)
        lax.fori_loop(j + 1, nq, loop_body, 0)
        dk_ref[...] = dk_s[...]
        dv_ref[...] = dv_s[...]

    return pl.pallas_call(
        body, name=name, grid=(HEADS // 2, nq),
        out_shape=(jax.ShapeDtypeStruct((S, HEADS * HEAD_PAD), F32), jax.ShapeDtypeStruct((S, HEADS * V_HEAD), F32)),
        in_specs=[pl.BlockSpec((S, PW), lambda p, j: (0, p)),
                  pl.BlockSpec((TQ, PW), lambda p, j: (j, p)),
                  pl.BlockSpec((TQ, LANES), lambda p, j: (j, p)),
                  pl.BlockSpec((S, LANES), lambda p, j: (0, p)),
                  pl.BlockSpec((S, LANES), lambda p, j: (0, p)),
                  pl.BlockSpec((S, LANES), lambda p, j: (0, p))],
        out_specs=(pl.BlockSpec((TQ, PW), lambda p, j: (j, p)),
                   pl.BlockSpec((TQ, LANES), lambda p, j: (j, p))),
        scratch_shapes=[pltpu.VMEM((TQ, PW), F32), pltpu.VMEM((TQ, LANES), F32)],
        compiler_params=_cparams(),
    )(q, k, v, do, lse, delta)


LOG2E = math.log2(math.e)
ATTN_FWD_HEADS = 8
ATTN_BWD_HEADS = 4


def _chunk_mask_t(T):
    key = lax.broadcasted_iota(jnp.int32, (T, T), 0) // CHUNK
    qry = lax.broadcasted_iota(jnp.int32, (T, T), 1) // CHUNK
    return key <= qry


def _qkv_fwd_t(qn, kvn, z, tabs, w_q, w_kv, name):
    S = qn.shape[0]
    T = _attn_tile(S)
    HW = HEADS * HEAD_PAD
    scale = LOG2E / math.sqrt(QK_NOPE + QK_ROPE)

    def body(qn_ref, kvn_ref, kr_ref, ct_ref, ut_ref, dt_ref, wq_ref, wkv_ref, q_ref, k_ref, v_ref, kt_ref, vt_ref):
        ct, ut, dt = ct_ref[...], ut_ref[...], dt_ref[...]
        qa = jnp.dot(qn_ref[...], wq_ref[...], preferred_element_type=F32)
        kva = jnp.dot(kvn_ref[...], wkv_ref[...], preferred_element_type=F32)
        kr = _f32(kr_ref)
        ones_row = (lax.broadcasted_iota(jnp.int32, (V_HEAD, T), 0) == 0).astype(F32)
        for h in range(HEADS):
            sl = slice(h * HEAD_PAD, (h + 1) * HEAD_PAD)
            q_ref[:, sl] = (_rope(qa[:, sl], ct, ut, dt) * scale).astype(BF16)
            kh = _rope(kva[:, sl] + kr, ct, ut, dt)
            k_ref[:, sl] = kh.astype(BF16)
            kt_ref[0, sl, :] = kh.T.astype(BF16)
        v_ref[...] = kva[:, HW:].astype(BF16)
        for p in range(HEADS // 2):
            vpt = kva[:, HW + p * LANES:HW + (p + 1) * LANES].T
            for h in range(2):
                r0 = (2 * p + h) * HEAD_PAD
                vt_ref[0, r0:r0 + V_HEAD, :] = vpt[h * V_HEAD:(h + 1) * V_HEAD, :].astype(BF16)
                vt_ref[0, r0 + V_HEAD:r0 + HEAD_PAD, :] = ones_row.astype(BF16)

    t3 = jax.ShapeDtypeStruct((S // T, HW, T), BF16)
    return pl.pallas_call(
        body, name=name, grid=(S // T,),
        out_shape=(jax.ShapeDtypeStruct((S, HW), BF16), jax.ShapeDtypeStruct((S, HW), BF16),
                   jax.ShapeDtypeStruct((S, HEADS * V_HEAD), BF16), t3, t3),
        in_specs=[_rows(T, QL), _rows(T, KVL), _rows(T, HEAD_PAD, 23),
                  _rows(T, HEAD_PAD), _rows(T, HEAD_PAD), _rows(T, HEAD_PAD),
                  _const(w_q.shape), _const(w_kv.shape)],
        out_specs=(_rows(T, HW), _rows(T, HW), _rows(T, HEADS * V_HEAD),
                   pl.BlockSpec((1, HW, T), lambda i: (i, 0, 0)), pl.BlockSpec((1, HW, T), lambda i: (i, 0, 0))),
        compiler_params=_cparams(),
    )(qn, kvn, z, *tabs, w_q, w_kv)


def _attn_fwd_t(q, k, vT3, name):
    S = q.shape[0]
    T = _attn_tile(S)
    nq = S // T
    NH = ATTN_FWD_HEADS
    PW = NH * HEAD_PAD

    def body(q_ref, k_ref, vt_ref, o_ref, lse_ref, m_s, acc_s):
        i = pl.program_id(1)
        m_s[...] = jnp.full_like(m_s, NEG)
        acc_s[...] = jnp.zeros_like(acc_s)
        qv = q_ref[...]

        def step(j, masked):
            kb = k_ref[pl.ds(pl.multiple_of(j * T, T), T), :]
            vt = vt_ref[j]
            heads = [slice(h * HEAD_PAD, (h + 1) * HEAD_PAD) for h in range(NH)]
            sts = [lax.dot_general(kb[:, sl], qv[:, sl], _NT, preferred_element_type=F32) for sl in heads]
            alphas, pvs = [], []
            for h, sl in enumerate(heads):
                st = jnp.where(_chunk_mask_t(T), sts[h], NEG) if masked else sts[h]
                m_prev = m_s[h]
                m_new = jnp.maximum(m_prev, jnp.max(st, axis=0, keepdims=True))
                alphas.append(jnp.exp2(m_prev[0:1] - m_new[0:1]))
                pt = jnp.exp2(st - m_new[0:1]).astype(BF16)
                m_s[h] = m_new
                pvs.append(jnp.dot(vt[sl, :], pt, preferred_element_type=F32))
            for h in range(NH):
                acc_s[h] = acc_s[h] * alphas[h] + pvs[h]

        def loop_body(j, carry):
            step(j, False)
            return carry

        lax.fori_loop(0, i, loop_body, 0)
        step(i, True)
        for g in range(NH // 2):
            outs = []
            for h in (2 * g, 2 * g + 1):
                acc = acc_s[h]
                l_row = acc[V_HEAD:V_HEAD + 1, :]
                outs.append(acc[0:V_HEAD, :] / l_row)
                lse_ref[0, h * SUBLANES:(h + 1) * SUBLANES, :] = m_s[h] + jnp.log2(l_row)
            o_ref[:, g * LANES:(g + 1) * LANES] = jnp.concatenate(outs, axis=0).T

    return pl.pallas_call(
        body, name=name, grid=(HEADS // NH, nq),
        out_shape=(jax.ShapeDtypeStruct((S, HEADS * V_HEAD), F32),
                   jax.ShapeDtypeStruct((nq, HEADS * SUBLANES, T), F32)),
        in_specs=[pl.BlockSpec((T, PW), lambda p, i: (i, p)),
                  pl.BlockSpec((S, PW), lambda p, i: (0, p)),
                  pl.BlockSpec((nq, PW, T), lambda p, i: (0, p, 0))],
        out_specs=(pl.BlockSpec((T, NH * V_HEAD), lambda p, i: (i, p)),
                   pl.BlockSpec((1, NH * SUBLANES, T), lambda p, i: (i, p, 0))),
        scratch_shapes=[pltpu.VMEM((NH, SUBLANES, T), F32), pltpu.VMEM((NH, HEAD_PAD, T), F32)],
        compiler_params=_cparams(),
    )(q, k, vT3)


def _attn_bwd_t(q, k, v, kT3, do, lse3, dl3, name):
    S = q.shape[0]
    T = _attn_tile(S)
    nq = S // T
    NH = ATTN_BWD_HEADS
    PW = NH * HEAD_PAD
    VW = NH * V_HEAD

    def body(q_ref, k_ref, v_ref, kt_ref, do_ref, lse_ref, dl_ref, dq_ref, dk_ref, dv_ref, dk_s, dv_s):
        j = pl.program_id(1)
        left = lax.broadcasted_iota(jnp.int32, (T, LANES), 1) < V_HEAD

        @pl.when(j == 0)
        def _():
            dq_ref[...] = jnp.zeros_like(dq_ref)

        dk_s[...] = jnp.zeros_like(dk_s)
        dv_s[...] = jnp.zeros_like(dv_s)
        kb = k_ref[...]
        vms = []
        for g in range(NH // 2):
            vb = v_ref[:, g * LANES:(g + 1) * LANES]
            vms += [jnp.where(left, vb, jnp.zeros_like(vb)), jnp.where(left, jnp.zeros_like(vb), vb)]
        kt = kt_ref[0]

        def step(i, masked):
            r0 = pl.multiple_of(i * T, T)
            qb = q_ref[pl.ds(r0, T), :]
            do_all = do_ref[pl.ds(r0, T), :]
            lse = lse_ref[i]
            dl = dl_ref[i]
            heads = [slice(h * HEAD_PAD, (h + 1) * HEAD_PAD) for h in range(NH)]
            dobs = [do_all[:, (h // 2) * LANES:(h // 2 + 1) * LANES] for h in range(NH)]
            sts = [lax.dot_general(kb[:, sl], qb[:, sl], _NT, preferred_element_type=F32) for sl in heads]
            dpts = [lax.dot_general(vms[h], dobs[h], _NT, preferred_element_type=F32) for h in range(NH)]
            res = []
            for h, sl in enumerate(heads):
                r8 = h * SUBLANES
                pt = jnp.exp2(sts[h] - lse[r8:r8 + 1, :])
                if masked:
                    pt = jnp.where(_chunk_mask_t(T), pt, 0.0)
                dst = (pt * (dpts[h] - dl[r8:r8 + 1, :])).astype(BF16)
                res.append((jnp.dot(pt.astype(BF16), dobs[h], preferred_element_type=F32),
                            jnp.dot(dst, qb[:, sl], preferred_element_type=F32),
                            jnp.dot(kt[sl, :], dst, preferred_element_type=F32)))
            for h, sl in enumerate(heads):
                dv_s[h] += res[h][0]
                dk_s[:, sl] += res[h][1]
                dq_ref[i, sl, :] += res[h][2]

        def loop_body(i, carry):
            step(i, False)
            return carry

        step(j, True)
        lax.fori_loop(j + 1, nq, loop_body, 0)
        dk_ref[...] = dk_s[...] * (1.0 / LOG2E)
        for g in range(NH // 2):
            dv_ref[:, g * LANES:(g + 1) * LANES] = jnp.where(left, dv_s[2 * g], dv_s[2 * g + 1])

    return pl.pallas_call(
        body, name=name, grid=(HEADS // NH, nq),
        out_shape=(jax.ShapeDtypeStruct((nq, HEADS * HEAD_PAD, T), F32),
                   jax.ShapeDtypeStruct((S, HEADS * HEAD_PAD), F32), jax.ShapeDtypeStruct((S, HEADS * V_HEAD), F32)),
        in_specs=[pl.BlockSpec((S, PW), lambda p, j: (0, p)),
                  pl.BlockSpec((T, PW), lambda p, j: (j, p)),
                  pl.BlockSpec((T, VW), lambda p, j: (j, p)),
                  pl.BlockSpec((1, PW, T), lambda p, j: (j, p, 0)),
                  pl.BlockSpec((S, VW), lambda p, j: (0, p)),
                  pl.BlockSpec((nq, NH * SUBLANES, T), lambda p, j: (0, p, 0)),
                  pl.BlockSpec((nq, NH * SUBLANES, T), lambda p, j: (0, p, 0))],
        out_specs=(pl.BlockSpec((nq, PW, T), lambda p, j: (0, p, 0)),
                   pl.BlockSpec((T, PW), lambda p, j: (j, p)),
                   pl.BlockSpec((T, VW), lambda p, j: (j, p))),
        scratch_shapes=[pltpu.VMEM((T, PW), F32), pltpu.VMEM((NH, T, LANES), F32)],
        compiler_params=_cparams(),
    )(q, k, v, kT3, do, lse3, dl3)


def _even_post(z, cv, o, name):
    S = z.shape[0]
    T = _row_tile(S)

    def body(ab_ref, ag_ref, bg_ref, cv_ref, o_ref, y_ref):
        y_ref[:, 0:SCW] = (_f32(ab_ref) * cv_ref[...] * _silu(_f32(ag_ref))).astype(BF16)
        y_ref[:, SCW:2 * SCW] = (o_ref[...] * _silu(_f32(bg_ref))).astype(BF16)

    return pl.pallas_call(
        body, name=name, grid=(S // T,),
        out_shape=jax.ShapeDtypeStruct((S, 2 * SCW), BF16),
        in_specs=[_rows(T, SCW, 0), _rows(T, SCW, 3), _rows(T, SCW, 4), _rows(T, SCW), _rows(T, SCW)],
        out_specs=_rows(T, 2 * SCW), compiler_params=_cparams(),
    )(z, z, z, cv, o)


def _even_bwd_gates(dyc, z, cv, o, name):
    S = z.shape[0]
    T = _row_tile(S)

    def body(dya_ref, dyb_ref, ab_ref, ag_ref, bg_ref, cv_ref, o_ref,
             dab_ref, dag_ref, dbg_ref, dcv_ref, do_ref, dl_ref):
        dya, ab, ag, cv = dya_ref[...], _f32(ab_ref), _f32(ag_ref), cv_ref[...]
        sg = _silu(ag)
        dab_ref[...] = (dya * cv * sg).astype(BF16)
        dcv_ref[...] = dya * ab * sg
        dag_ref[...] = (dya * ab * cv * _dsilu(ag)).astype(BF16)
        dyb, bg, ov = dyb_ref[...], _f32(bg_ref), o_ref[...]
        dov = dyb * _silu(bg)
        do_ref[...] = dov.astype(BF16)
        dbg_ref[...] = (dyb * ov * _dsilu(bg)).astype(BF16)
        prod = dov * ov
        left = lax.broadcasted_iota(jnp.int32, (T, LANES), 1) < V_HEAD
        for p in range(HEADS // 2):
            blk = prod[:, p * LANES:(p + 1) * LANES]
            s0 = jnp.sum(jnp.where(left, blk, 0.0), axis=1, keepdims=True)
            s1 = jnp.sum(jnp.where(left, 0.0, blk), axis=1, keepdims=True)
            dt = jnp.where(left, s0, s1).T
            dl_ref[0, 2 * p * SUBLANES:(2 * p + 1) * SUBLANES, :] = dt[0:SUBLANES, :]
            dl_ref[0, (2 * p + 1) * SUBLANES:(2 * p + 2) * SUBLANES, :] = dt[V_HEAD:V_HEAD + SUBLANES, :]

    assert T == _attn_tile(S)
    bf = jax.ShapeDtypeStruct((S, SCW), BF16)
    ff = jax.ShapeDtypeStruct((S, SCW), F32)
    return pl.pallas_call(
        body, name=name, grid=(S // T,),
        out_shape=(bf, bf, bf, ff, bf, jax.ShapeDtypeStruct((S // T, HEADS * SUBLANES, T), F32)),
        in_specs=[_rows(T, SCW, 0), _rows(T, SCW, 1), _rows(T, SCW, 0), _rows(T, SCW, 3), _rows(T, SCW, 4),
                  _rows(T, SCW), _rows(T, SCW)],
        out_specs=(_rows(T, SCW),) * 5 + (pl.BlockSpec((1, HEADS * SUBLANES, T), lambda i: (i, 0, 0)),),
        compiler_params=_cparams(),
    )(dyc, dyc, z, z, z, cv, o)


def _qkv_bwd(dq, dk, dv, z, tabs, w_q, w_kv, qg, kvg, name):
    S = dk.shape[0]
    T = _attn_tile(S)
    HW = HEADS * HEAD_PAD
    VW = HEADS * V_HEAD
    scale = 1.0 / math.sqrt(QK_NOPE + QK_ROPE)

    def fn(ins, outs):
        dq_ref, dk_ref, dv_ref, cq_ref, ckv_ref, ct_ref, ut_ref, dt_ref, wq_ref, wkv_ref, qg_ref, kvg_ref = ins
        dqp_ref, dkvp_ref, dcq_ref, dckv_ref, dkr_ref = outs
        ct, ut, dt = ct_ref[...], ut_ref[...], dt_ref[...]
        dkr = jnp.zeros((T, HEAD_PAD), F32)
        for h in range(HEADS):
            sl = slice(h * HEAD_PAD, (h + 1) * HEAD_PAD)
            dqp_ref[:, sl] = (_rope_t(dq_ref[0, sl, :].T, ct, ut, dt) * scale).astype(BF16)
            dkh = _rope_t(dk_ref[:, sl], ct, ut, dt)
            dkr = dkr + dkh
            dkvp_ref[:, sl] = dkh.astype(BF16)
        dkvp_ref[:, HW:] = dv_ref[...].astype(BF16)
        dkr_ref[...] = dkr.astype(BF16)
        sums = []
        for lat_ref, g_ref, dpre_ref, w_ref, dlat_ref in ((cq_ref, qg_ref, dqp_ref, wq_ref, dcq_ref),
                                                         (ckv_ref, kvg_ref, dkvp_ref, wkv_ref, dckv_ref)):
            dn = lax.dot_general(dpre_ref[...], w_ref[...], _NT, preferred_element_type=F32)
            xv = _f32(lat_ref)
            rstd = lax.rsqrt(jnp.mean(xv * xv, axis=-1, keepdims=True) + EPS)
            xh = xv * rstd
            dxh = dn * g_ref[...]
            dlat_ref[...] = (rstd * (dxh - xh * jnp.mean(dxh * xh, axis=-1, keepdims=True))).astype(BF16)
            sums.append(dn * xh)
        return sums

    return _col_sums(
        2, fn, [dq, dk, dv, z, z, *tabs, w_q, w_kv, qg, kvg],
        [pl.BlockSpec((1, HW, T), lambda i: (i, 0, 0)), _rows(T, HW), _rows(T, VW), _rows(T, QL, 10), _rows(T, KVL, 22),
         _rows(T, HEAD_PAD), _rows(T, HEAD_PAD), _rows(T, HEAD_PAD),
         _const(w_q.shape), _const(w_kv.shape), _const((1, QL)), _const((1, KVL))],
        [jax.ShapeDtypeStruct((S, HW), BF16), jax.ShapeDtypeStruct((S, HW + VW), BF16),
         jax.ShapeDtypeStruct((S, QL), BF16), jax.ShapeDtypeStruct((S, KVL), BF16),
         jax.ShapeDtypeStruct((S, HEAD_PAD), BF16)],
        [_rows(T, HW), _rows(T, HW + VW), _rows(T, QL), _rows(T, KVL), _rows(T, HEAD_PAD)],
        S, T, [QL, KVL], name)


def _even_dz(dab, du, z, dag, dbg, dcq, dckv, dkr, name):
    S = z.shape[0]
    T = _row_tile(S)

    def body(dab_ref, du_ref, ac_ref, ax_ref, dag_ref, dbg_ref, dcq_ref, dckv_ref, dkr_ref, dz_ref):
        duv = du_ref[...]
        dz_ref[:, 0:SCW] = dab_ref[...]
        dz_ref[:, SCW:2 * SCW] = (duv * _f32(ax_ref)).astype(BF16)
        dz_ref[:, 2 * SCW:3 * SCW] = (duv * _f32(ac_ref)).astype(BF16)
        dz_ref[:, 3 * SCW:4 * SCW] = dag_ref[...]
        dz_ref[:, 4 * SCW:5 * SCW] = dbg_ref[...]
        dz_ref[:, 5 * SCW:5 * SCW + QL] = dcq_ref[...]
        dz_ref[:, 5 * SCW + QL:5 * SCW + QL + KVL] = dckv_ref[...]
        dz_ref[:, 5 * SCW + QL + KVL:ZE] = dkr_ref[...]

    return pl.pallas_call(
        body, name=name, grid=(S // T,),
        out_shape=jax.ShapeDtypeStruct((S, ZE), BF16),
        in_specs=[_rows(T, SCW), _rows(T, SCW), _rows(T, SCW, 1), _rows(T, SCW, 2), _rows(T, SCW), _rows(T, SCW),
                  _rows(T, QL), _rows(T, KVL), _rows(T, HEAD_PAD)],
        out_specs=_rows(T, ZE), compiler_params=_cparams(),
    )(dab, du, z, z, dag, dbg, dcq, dckv, dkr)


def _odd_pre(z, name):
    S, D = z.shape[0], z.shape[1] // 3
    T = _row_tile(S)

    def body(val_ref, glu_ref, u_ref):
        u_ref[...] = _f32(val_ref) * _sigmoid(_f32(glu_ref))

    return pl.pallas_call(
        body, name=name, grid=(S // T,),
        out_shape=jax.ShapeDtypeStruct((S, D), F32),
        in_specs=[_rows(T, D, 0), _rows(T, D, 1)], out_specs=_rows(T, D),
        compiler_params=_cparams(),
    )(z, z)


def _layer_norm_stats(cv):
    mu = jnp.mean(cv, axis=-1, keepdims=True)
    cen = cv - mu
    rstd = lax.rsqrt(jnp.mean(cen * cen, axis=-1, keepdims=True) + EPS)
    return cen * rstd, rstd


def _odd_post(cv, z, ln_g, ln_b, name):
    S, D = cv.shape
    T = _row_tile(S)

    def body(cv_ref, sg_ref, g_ref, b_ref, y_ref):
        cvh, _ = _layer_norm_stats(cv_ref[...])
        y_ref[...] = (_silu(cvh * g_ref[...] + b_ref[...]) * _silu(_f32(sg_ref))).astype(BF16)

    return pl.pallas_call(
        body, name=name, grid=(S // T,),
        out_shape=jax.ShapeDtypeStruct((S, D), BF16),
        in_specs=[_rows(T, D), _rows(T, D, 2), _const((1, D)), _const((1, D))],
        out_specs=_rows(T, D), compiler_params=_cparams(),
    )(cv, z, ln_g, ln_b)


def _odd_bwd_norm(dyi, cv, z, ln_g, ln_b, name):
    S, D = cv.shape
    T = _row_tile(S)

    def fn(ins, outs):
        dy_ref, cv_ref, sg_ref, g_ref, b_ref = ins
        dcv_ref, dsg_ref = outs
        cvh, rstd = _layer_norm_stats(cv_ref[...])
        ln = cvh * g_ref[...] + b_ref[...]
        sgv, dy = _f32(sg_ref), dy_ref[...]
        dsg_ref[...] = (dy * _silu(ln) * _dsilu(sgv)).astype(BF16)
        dln = dy * _silu(sgv) * _dsilu(ln)
        dh = dln * g_ref[...]
        dcv_ref[...] = rstd * (dh - jnp.mean(dh, axis=-1, keepdims=True)
                               - cvh * jnp.mean(dh * cvh, axis=-1, keepdims=True))
        return [dln * cvh, dln]

    return _col_sums(2, fn, [dyi, cv, z, ln_g, ln_b],
                     [_rows(T, D), _rows(T, D), _rows(T, D, 2), _const((1, D)), _const((1, D))],
                     [jax.ShapeDtypeStruct((S, D), F32), jax.ShapeDtypeStruct((S, D), BF16)],
                     [_rows(T, D), _rows(T, D)], S, T, [D, D], name)


def _odd_dz(du, z, dsg, name):
    S, D = du.shape
    T = _row_tile(S)

    def body(du_ref, val_ref, glu_ref, dsg_ref, dz_ref):
        duv = du_ref[...]
        sig = _sigmoid(_f32(glu_ref))
        dz_ref[:, 0:D] = (duv * sig).astype(BF16)
        dz_ref[:, D:2 * D] = (duv * _f32(val_ref) * sig * (1.0 - sig)).astype(BF16)
        dz_ref[:, 2 * D:3 * D] = dsg_ref[...]

    return pl.pallas_call(
        body, name=name, grid=(S // T,),
        out_shape=jax.ShapeDtypeStruct((S, 3 * D), BF16),
        in_specs=[_rows(T, D), _rows(T, D, 0), _rows(T, D, 1), _rows(T, D)],
        out_specs=_rows(T, 3 * D), compiler_params=_cparams(),
    )(du, z, z, dsg)


ADAM_BLOCK_ELEMS = 128 * 1024


def _adam_tiles(R, C):
    if R * C <= ADAM_BLOCK_ELEMS:
        return R, C
    tr = R
    for cand in range(SUBLANES, R, SUBLANES):
        if R % cand == 0 and cand * C <= ADAM_BLOCK_ELEMS:
            tr = cand
    if tr < R:
        return tr, C
    tc = C
    for cand in range(LANES, C, LANES):
        if C % cand == 0 and R * cand <= ADAM_BLOCK_ELEMS:
            tc = cand
    return R, tc


def _adamw(g_parts, w, m, v, name):
    if not isinstance(g_parts, (list, tuple)):
        g_parts = [g_parts]
    ng = len(g_parts)
    _, R, C = g_parts[0].shape
    tr, tc = _adam_tiles(R, C)

    def body(*refs):
        g_refs = refs[:ng]
        w_ref, m_ref, v_ref, go_ref, d_ref, mo_ref, vo_ref = refs[ng:]
        g = None
        for g_ref in g_refs:
            for p in range(g_ref.shape[0]):
                part = g_ref[p].astype(F32)
                g = part if g is None else g + part
        mn = ADAM_B1 * m_ref[...] + (1.0 - ADAM_B1) * g
        vn = ADAM_B2 * v_ref[...] + (1.0 - ADAM_B2) * (g * g)
        m_hat = mn / (1.0 - ADAM_B1 ** ADAM_STEP)
        v_hat = vn / (1.0 - ADAM_B2 ** ADAM_STEP)
        go_ref[...] = g
        d_ref[...] = -ADAM_LR * (m_hat / (jnp.sqrt(v_hat) + ADAM_EPS) + ADAM_WD * w_ref[...])
        mo_ref[...] = mn
        vo_ref[...] = vn

    slab = jax.ShapeDtypeStruct((R, C), F32)
    blk = pl.BlockSpec((tr, tc), lambda i, j: (i, j))
    return pl.pallas_call(
        body, name=name, grid=(R // tr, C // tc),
        out_shape=(slab,) * 4,
        in_specs=[pl.BlockSpec((g.shape[0], tr, tc), lambda i, j: (0, i, j)) for g in g_parts] + [blk, blk, blk],
        out_specs=(blk,) * 4, compiler_params=_cparams(),
    )(*g_parts, w, m, v)


def _gather_cols(g, shape):
    nd = len(shape)
    t = jnp.moveaxis(g, 0, nd - 1)
    return t.reshape(tuple(shape[:-1]) + (N_DEV * shape[-1],))


def _scatter_cols(full, n):
    t = full.reshape(full.shape[:-1] + (N_DEV, n))
    return jnp.moveaxis(t, -2, 0)


def kernel(x, c, positions, ada_w, ada_b, pre_norm_g, post_norm_g, even_w_in, even_sc_conv_w, even_sc_conv_b, even_q_norm_g, even_kv_norm_g, even_w_uq, even_w_ukv, even_w_out, odd_w_in, odd_conv_w, odd_conv_b, odd_ln_g, odd_ln_b, odd_w_out, loss_target, m_ada_w, m_ada_b, m_pre_norm_g, m_post_norm_g, m_even_w_in, m_even_sc_conv_w, m_even_sc_conv_b, m_even_q_norm_g, m_even_kv_norm_g, m_even_w_uq, m_even_w_ukv, m_even_w_out, m_odd_w_in, m_odd_conv_w, m_odd_conv_b, m_odd_ln_g, m_odd_ln_b, m_odd_w_out, v_ada_w, v_ada_b, v_pre_norm_g, v_post_norm_g, v_even_w_in, v_even_sc_conv_w, v_even_sc_conv_b, v_even_q_norm_g, v_even_kv_norm_g, v_even_w_uq, v_even_w_ukv, v_even_w_out, v_odd_w_in, v_odd_conv_w, v_odd_conv_b, v_odd_ln_g, v_odd_ln_b, v_odd_w_out):
    S, D = x.shape[1], x.shape[2]
    L = ada_w.shape[0]
    NE, NO = even_w_in.shape[0], odd_w_in.shape[0]
    me = 4 * lax.axis_index("x") + 2 * lax.axis_index("y") + lax.axis_index("c")
    x0 = x[0]
    target = loss_target[0]

    small_parts = [c, even_sc_conv_w, odd_conv_w, odd_conv_b, odd_ln_g, odd_ln_b]
    small_shapes = [p.shape for p in small_parts]
    sg = _exchange([_pack(small_parts, F32, SUBLANES)], False, "gather_small")[0].reshape(N_DEV, -1)
    c_all, scw_g, ocw_g, ocb_g, olg_g, olb_g = _unpack(sg, small_shapes)
    c_all = c_all.reshape(N_DEV, D)
    sc_conv_w = _gather_cols(scw_g, even_sc_conv_w.shape)
    o_conv_w = _gather_cols(ocw_g, odd_conv_w.shape)
    o_conv_b = _gather_cols(ocb_g, odd_conv_b.shape)
    o_ln_g = _gather_cols(olg_g, odd_ln_g.shape)
    o_ln_b = _gather_cols(olb_g, odd_ln_b.shape)

    pad_q = HEAD_PAD - QK_NOPE - QK_ROPE
    w_local = [jnp.swapaxes(even_w_in, 1, 2).astype(BF16),
               jnp.pad(even_w_uq, ((0, 0), (0, 0), (0, pad_q))).astype(BF16),
               jnp.pad(even_w_ukv[..., :QK_NOPE], ((0, 0), (0, 0), (0, HEAD_PAD - QK_NOPE))).astype(BF16),
               even_w_ukv[..., QK_NOPE:].astype(BF16),
               even_w_out.astype(BF16), odd_w_in.astype(BF16), odd_w_out.astype(BF16)]
    n_ada = ada_w.shape[2]
    ada_b_cols = lax.dynamic_slice_in_dim(ada_b, me * n_ada, n_ada, axis=1).reshape(L, 1, n_ada)
    mod_slab = _ada_fwd(c_all, ada_w, ada_b_cols)
    mod_g = _exchange([_pack([mod_slab], F32, SUBLANES)], False, "gather_mod")[0].reshape(N_DEV, -1)
    mod_all = mod_g[:, :L * N_DEV * n_ada].reshape(N_DEV, L, N_DEV, n_ada)
    mod = lax.dynamic_index_in_dim(mod_all, me, axis=2, keepdims=False)
    mod = jnp.moveaxis(mod, 0, 1).reshape(L, 3 * D)
    shift, scale, gate = mod[:, :D], mod[:, D:2 * D], mod[:, 2 * D:]

    heads_to_cols = lambda g: jnp.moveaxis(g, 0, 1).reshape(g.shape[1], -1)
    w_handles = {}
    token = jnp.broadcast_to(jnp.minimum(jnp.abs(mod[0, 0]), 0.0), (SUBLANES, LANES))
    for layer in range(L):
        i = layer // 2
        groups = ({"in": [w_local[0][i]], "rest": [w[i] for w in w_local[1:5]]} if layer % 2 == 0
                  else {"all": [w[i] for w in w_local[5:]]})
        for key, mine in groups.items():
            mine = [w + token[0, 0].astype(BF16) for w in mine]
            w_handles[layer, key], token = _exchange_start(mine, False, f"gather_weights_start_l{layer}_{key}")
    w_token = token

    def arrived(layer, key, after):
        return _exchange_wait(w_handles[layer, key], False, after, f"gather_weights_wait_l{layer}_{key}")[1]

    e_w_in_k, e_w_q_k, e_w_kv_k, e_w_out, o_w_in, o_w_out = ([None] * NE, [None] * NE, [None] * NE, [None] * NE,
                                                             [None] * NO, [None] * NO)

    half = QK_ROPE // 2
    inv_freq = 1.0 / (ROPE_THETA ** (jnp.arange(0, QK_ROPE, 2, dtype=F32) / QK_ROPE))
    inv_lane = jnp.zeros((HEAD_PAD,), F32).at[QK_NOPE:QK_NOPE + QK_ROPE].set(jnp.concatenate([inv_freq, inv_freq]))
    tabs = _rope_tables(positions.astype(F32).reshape(S, 1), inv_lane.reshape(1, HEAD_PAD))
    del half

    row = lambda a: a.reshape(1, -1)
    scb = even_sc_conv_b
    KP3, KP31 = SUBLANES, 32

    saved = []
    xs = x0
    h = _pre_norm(xs, row(pre_norm_g[0]) + w_token[0, 0], row(scale[0]), row(shift[0]), "pre_norm_l0")
    for layer in range(L):
        i = layer // 2
        tag = f"l{layer}"
        first = [h, tabs[0]] if layer == 0 else h
        if layer % 2 == 0:
            wt = arrived(layer, "in", first)[0].reshape(-1, D)
            e_w_in_k[i] = jnp.concatenate([wt[:2048], wt[2464:2976], wt[2048:2432], jnp.zeros((QK_NOPE, D), BF16),
                                           wt[2432:2464], jnp.zeros((pad_q, D), BF16)], axis=0)
            z = _matmul(h, e_w_in_k[i], "nt", BF16, f"w_in_{tag}", tn=ZE)
            eq_g, ek_g, ev_g, eout_g = arrived(layer, "rest", z)
            e_w_q_k[i] = heads_to_cols(eq_g)
            e_w_kv_k[i] = jnp.concatenate([heads_to_cols(ek_g), heads_to_cols(ev_g)], axis=-1)
            e_w_out[i] = eout_g.reshape(-1, D)
            u, qn, kvn = _even_pre(z, row(even_q_norm_g[i]), row(even_kv_norm_g[i]), f"even_pre_{tag}")
            cw = jnp.pad(sc_conv_w[i], ((0, KP3 - SC_KERNEL), (0, 0)))
            cv = _conv_fwd(u, cw, row(scb[i]), SC_KERNEL, f"conv_{tag}")
            q, k, v, kT3, vT3 = _qkv_fwd_t(qn, kvn, z, tabs, e_w_q_k[i], e_w_kv_k[i], f"qkv_{tag}")
            o, lse = _attn_fwd_t(q, k, vT3, f"attn_{tag}")
            ycat = _even_post(z, cv, o, f"even_post_{tag}")
            y = _matmul(ycat, e_w_out[i], "nn", F32, f"w_out_{tag}", tn=1024)
            saved.append(dict(x=xs, h=h, z=z, u=u, qn=qn, kvn=kvn, cw=cw, cv=cv, q=q, k=k, v=v, kT3=kT3, o=o, lse=lse,
                              ycat=ycat, y=y))
        else:
            owin_g, oout_g = arrived(layer, "all", first)
            o_w_in[i], o_w_out[i] = heads_to_cols(owin_g), oout_g.reshape(-1, D)
            z = _matmul(h, o_w_in[i], "nn", BF16, f"w_in_{tag}", tn=3 * D)
            u = _odd_pre(z, f"odd_pre_{tag}")
            cw = jnp.pad(o_conv_w[i], ((0, KP31 - CONF_KERNEL), (0, 0)))
            cv = _conv_fwd(u, cw, row(o_conv_b[i]), CONF_KERNEL, f"conv_{tag}")
            yin = _odd_post(cv, z, row(o_ln_g[i]), row(o_ln_b[i]), f"odd_post_{tag}")
            y = _matmul(yin, o_w_out[i], "nn", F32, f"w_out_{tag}", tn=1024)
            saved.append(dict(x=xs, h=h, z=z, u=u, cw=cw, cv=cv, yin=yin, y=y))
        if layer < L - 1:
            xs, h = _post_pre_norm(xs, y, row(post_norm_g[layer]), row(gate[layer]), row(pre_norm_g[layer + 1]),
                                   row(scale[layer + 1]), row(shift[layer + 1]), f"post_pre_norm_{tag}")

    dx, dy, err_sq, dgate, g_post_last = _loss_post_norm_bwd(xs, y, row(post_norm_g[L - 1]), row(gate[L - 1]), target,
                                                             "loss_post_norm_bwd")
    loss = lax.psum(_scaled_total(err_sq, 0.5 / D, "loss_total")[0, 0], MESH_AXES)

    g_pre, g_post, dmod = [None] * L, [None] * L, [None] * L
    g_e_w_in, g_e_w_uq, g_e_w_ukv, g_e_w_out = [None] * NE, [None] * NE, [None] * NE, [None] * NE
    g_scw, g_scb, g_qg, g_kvg = [None] * NE, [None] * NE, [None] * NE, [None] * NE
    g_o_w_in, g_o_w_out, g_ocw, g_ocb, g_olg, g_olb = ([None] * NO for _ in range(6))
    sm_w = [even_sc_conv_w, odd_conv_w, odd_conv_b, odd_ln_g, odd_ln_b]
    sm_rows = _pack(sm_w, F32, SUBLANES).shape[0]

    def small_slab():
        full = [_scatter_cols(jnp.stack(g_scw), even_sc_conv_w.shape[-1]),
                _scatter_cols(jnp.stack(g_ocw), odd_conv_w.shape[-1]),
                _scatter_cols(jnp.concatenate(g_ocb, 0), odd_conv_b.shape[-1]),
                _scatter_cols(jnp.concatenate(g_olg, 0), odd_ln_g.shape[-1]),
                _scatter_cols(jnp.concatenate(g_olb, 0), odd_ln_b.shape[-1])]
        flat = jnp.concatenate([g.reshape(N_DEV, -1) for g in full], axis=1)
        return jnp.pad(flat, ((0, 0), (0, sm_rows * PACK_COLS - flat.shape[1]))).reshape(N_DEV, sm_rows, PACK_COLS)

    scatters = []
    bw_token = jnp.zeros((SUBLANES, LANES), F32)

    def start_scatter(tag, names, parts):
        own = [lax.dynamic_slice_in_dim(g, me, 1, axis=0) for g in parts]
        handle, token = _exchange_start([g.astype(BF16) for g in parts], True, f"scatter_grads_start_{tag}")
        scatters.append((tag, names, handle, own))
        return token

    for layer in reversed(range(L)):
        i = layer // 2
        tag = f"l{layer}"
        sv = saved[layer]
        if layer == L - 1:
            g_post[layer] = g_post_last
        if layer % 2 == 0:
            dyc = _matmul(dy, e_w_out[i], "nt", F32, f"d_ycat_{tag}", tn=1024)
            g_e_w_out[i] = _matmul(sv["ycat"], dy, "tn", BF16, f"g_w_out_{tag}", tn=1024).reshape(N_DEV, -1, D)
            if layer == 0:
                bw_token = start_scatter("l0_out", [("even_w_out", i)], [g_e_w_out[i]])
            dab, dag, dbg, dcv, do, delta = _even_bwd_gates(dyc, sv["z"], sv["cv"], sv["o"], f"even_gates_bwd_{tag}")
            du, dcw, g_scb[i] = _conv_bwd(dcv, sv["u"], sv["cw"] + bw_token[0, 0], SC_KERNEL, f"conv_bwd_{tag}")
            g_scw[i] = dcw[:SC_KERNEL]
            dq, dk, dv = _attn_bwd_t(sv["q"], sv["k"], sv["v"], sv["kT3"], do, sv["lse"], delta, f"attn_bwd_{tag}")
            (dqp, dkvp, dcq, dckv, dkr, g_qg[i], g_kvg[i]) = _qkv_bwd(
                dq, dk, dv, sv["z"], tabs, e_w_q_k[i], e_w_kv_k[i],
                row(even_q_norm_g[i]), row(even_kv_norm_g[i]), f"qkv_bwd_{tag}")
            gq = _matmul(sv["qn"], dqp, "tn", BF16, f"g_w_uq_{tag}", tn=1024)
            gkv = _matmul(sv["kvn"], dkvp, "tn", BF16, f"g_w_ukv_{tag}")
            g_e_w_uq[i] = jnp.moveaxis(gq.reshape(QL, HEADS, HEAD_PAD)[..., :QK_NOPE + QK_ROPE], 1, 0)
            g_e_w_ukv[i] = jnp.moveaxis(jnp.concatenate(
                [gkv[:, :HEADS * HEAD_PAD].reshape(KVL, HEADS, HEAD_PAD)[..., :QK_NOPE],
                 gkv[:, HEADS * HEAD_PAD:].reshape(KVL, HEADS, V_HEAD)], axis=-1), 1, 0)
            dz = _even_dz(dab, du, sv["z"], dag, dbg, dcq, dckv, dkr, f"even_dz_{tag}")
            gt = _matmul(dz, sv["h"], "tn", BF16, f"g_w_in_{tag}", tn=1024)
            g_e_w_in[i] = jnp.concatenate([gt[:2048], gt[2560:2944], gt[2944 + QK_NOPE:2944 + QK_NOPE + QK_ROPE],
                                           gt[2048:2560]], axis=0).reshape(N_DEV, -1, D)
            names = [("even_w_in", i), ("even_w_uq", i), ("even_w_ukv", i)]
            parts = [g_e_w_in[i], g_e_w_uq[i], g_e_w_ukv[i]]
            if layer == 0:
                names, parts = names + [("small", 0)], parts + [small_slab()]
            else:
                names, parts = names + [("even_w_out", i)], parts + [g_e_w_out[i]]
            bw_token = start_scatter(tag, names, parts)
            w_dh = e_w_in_k[i] + bw_token[0, 0].astype(BF16) if layer == 0 else e_w_in_k[i]
            dh = _matmul(dz, w_dh, "nn", F32, f"d_h_{tag}", tn=1024)
        else:
            dyi = _matmul(dy, o_w_out[i], "nt", F32, f"d_yin_{tag}", tn=1024)
            g_o_w_out[i] = _matmul(sv["yin"], dy, "tn", BF16, f"g_w_out_{tag}", tn=1024).reshape(N_DEV, -1, D)
            dcv, dsg, g_olg[i], g_olb[i] = _odd_bwd_norm(dyi, sv["cv"], sv["z"], row(o_ln_g[i]), row(o_ln_b[i]),
                                                         f"odd_norm_bwd_{tag}")
            du, dcw, g_ocb[i] = _conv_bwd(dcv, sv["u"], sv["cw"], CONF_KERNEL, f"conv_bwd_{tag}")
            g_ocw[i] = dcw[:CONF_KERNEL]
            dz = _odd_dz(du, sv["z"], dsg, f"odd_dz_{tag}")
            g_o_w_in[i] = _matmul(sv["h"], dz, "tn", BF16, f"g_w_in_{tag}", tn=odd_w_in.shape[-1],
                                  split_n=True)
            bw_token = start_scatter(tag, [("odd_w_in", i), ("odd_w_out", i)], [g_o_w_in[i], g_o_w_out[i]])
            dh = _matmul(dz, o_w_in[i], "nt", F32, f"d_h_{tag}", tn=1024)
        g_row = row(pre_norm_g[layer]) + bw_token[0, 0]
        if layer > 0:
            (dx, dy, dshift, dscale, g_pre[layer], dgate_prev, g_post[layer - 1]) = _pre_post_norm_bwd(
                dh, sv["x"], dx, g_row, row(scale[layer]), saved[layer - 1]["y"], row(post_norm_g[layer - 1]),
                row(gate[layer - 1]), f"pre_post_norm_bwd_{tag}")
        else:
            dx, dshift, dscale, g_pre[layer] = _pre_norm_bwd(dh, sv["x"], dx, g_row, row(scale[layer]),
                                                             f"pre_norm_bwd_{tag}")
            dgate_prev = None
        dmod[layer] = jnp.concatenate([dshift, dscale, dgate], axis=-1)
        dgate = dgate_prev
    grad_x = dx.reshape(1, S, D)

    rep_g = [jnp.concatenate(dmod, 0), jnp.concatenate(g_pre, 0), jnp.concatenate(g_post, 0),
             jnp.stack(g_scb), jnp.stack(g_qg), jnp.stack(g_kvg)]
    rep_w = [ada_b, pre_norm_g, post_norm_g, even_sc_conv_b, even_q_norm_g, even_kv_norm_g]
    rep_m = [m_ada_b, m_pre_norm_g, m_post_norm_g, m_even_sc_conv_b, m_even_q_norm_g, m_even_kv_norm_g]
    rep_v = [v_ada_b, v_pre_norm_g, v_post_norm_g, v_even_sc_conv_b, v_even_q_norm_g, v_even_kv_norm_g]
    rep_shapes = [w.shape for w in rep_w]
    rep_all = _exchange([_pack(rep_g, F32, SUBLANES)], False, "gather_small_grads")[0]
    rep_out = _adamw(rep_all, _pack(rep_w, F32, SUBLANES), _pack(rep_m, F32, SUBLANES), _pack(rep_v, F32, SUBLANES),
                     "adamw_replicated")
    rep_res = [_unpack(o.reshape(-1), rep_shapes) for o in rep_out]

    dmod_all = rep_all.reshape(N_DEV, -1)[:, :L * 3 * D].reshape(N_DEV, L, 3 * D)
    dmod_cols = jnp.moveaxis(lax.dynamic_slice_in_dim(dmod_all, me * n_ada, n_ada, axis=2), 0, 1)
    g_ada_w = _ada_bwd(c_all.T, dmod_cols)
    ada_out = _adamw(g_ada_w.reshape(1, -1, n_ada), ada_w.reshape(-1, n_ada),
                     m_ada_w.reshape(-1, n_ada), v_ada_w.reshape(-1, n_ada), "adamw_ada_w")
    ada_res = [o.reshape(ada_w.shape) for o in ada_out]

    sm_m = [m_even_sc_conv_w, m_odd_conv_w, m_odd_conv_b, m_odd_ln_g, m_odd_ln_b]
    sm_v = [v_even_sc_conv_w, v_odd_conv_w, v_odd_conv_b, v_odd_ln_g, v_odd_ln_b]
    sm_shapes = [w.shape for w in sm_w]
    state = {"even_w_in": (even_w_in, m_even_w_in, v_even_w_in), "even_w_uq": (even_w_uq, m_even_w_uq, v_even_w_uq),
             "even_w_ukv": (even_w_ukv, m_even_w_ukv, v_even_w_ukv), "even_w_out": (even_w_out, m_even_w_out, v_even_w_out),
             "odd_w_in": (odd_w_in, m_odd_w_in, v_odd_w_in), "odd_w_out": (odd_w_out, m_odd_w_out, v_odd_w_out)}
    big_res = {name: [[None] * len(state[name][0]) for _ in range(4)] for name in state}
    after = [bw_token, grad_x, rep_out[0], ada_out[0]]
    sm_res = None
    for tag, names, handle, own in scatters:
        _, landed = _exchange_wait(handle, True, after, f"scatter_grads_wait_{tag}")
        after = []
        for a, (name, i) in enumerate(names):
            if name == "small":
                sm_out = _adamw([own[a], landed[a]], _pack(sm_w, F32, SUBLANES), _pack(sm_m, F32, SUBLANES),
                                _pack(sm_v, F32, SUBLANES), "adamw_small_sharded")
                sm_res = [_unpack(o.reshape(-1), sm_shapes) for o in sm_out]
                continue
            transposed = name == "even_w_in"
            wmv = [t[i].T if transposed else t[i] for t in state[name]]
            res = _adamw([own[a], landed[a]], *wmv, f"adamw_{name}_{i}")
            for kind in range(4):
                big_res[name][kind][i] = res[kind].T if transposed else res[kind]
            after += [big_res[name][kind][i] for kind in range(4)]
    sh_res = [dict(zip(["even_sc_conv_w", "odd_conv_w", "odd_conv_b", "odd_ln_g", "odd_ln_b"], sm_res[kind]))
              for kind in range(4)]
    for name in state:
        for kind in range(4):
            sh_res[kind][name] = jnp.stack(big_res[name][kind])

    order = ["ada_w", "ada_b", "pre_norm_g", "post_norm_g", "even_w_in", "even_sc_conv_w", "even_sc_conv_b",
             "even_q_norm_g", "even_kv_norm_g", "even_w_uq", "even_w_ukv", "even_w_out", "odd_w_in", "odd_conv_w",
             "odd_conv_b", "odd_ln_g", "odd_ln_b", "odd_w_out"]
    rep_names = ["ada_b", "pre_norm_g", "post_norm_g", "even_sc_conv_b", "even_q_norm_g", "even_kv_norm_g"]
    outs = [loss, grad_x]
    for kind in range(4):
        for name in order:
            if name == "ada_w":
                outs.append(ada_res[kind])
            elif name in rep_names:
                outs.append(rep_res[kind][rep_names.index(name)])
            else:
                outs.append(sh_res[kind][name])
    return tuple(outs)
```

```python
import functools
import math

import jax
import jax.numpy as jnp
from jax import lax
from jax.experimental import pallas as pl
from jax.experimental.pallas import tpu as pltpu

F32 = jnp.float32
BF16 = jnp.bfloat16
MESH_AXES = ("x", "y", "c")
N_DEV = 8
EPS = 1e-6
CHUNK = 64
HEADS = 8
QK_NOPE = 64
QK_ROPE = 32
V_HEAD = 64
HEAD_PAD = 128
ROPE_THETA = 10000.0
SC_KERNEL = 3
CONF_KERNEL = 31
LANES = 128
SUBLANES = 8
PACK_COLS = 1024
VMEM_LIMIT = 48 * 1024 * 1024
NEG = -1e30

ADAM_LR = 0.001
ADAM_B1 = 0.9
ADAM_B2 = 0.999
ADAM_EPS = 1e-08
ADAM_WD = 0.01
ADAM_STEP = 10


def _cparams():
    return pltpu.CompilerParams(vmem_limit_bytes=VMEM_LIMIT)


def _sigmoid(x):
    return 1.0 / (1.0 + jnp.exp(-x))


def _f32(ref):
    return ref[...].astype(F32)


def _silu(x):
    return x * _sigmoid(x)


def _dsilu(x):
    s = _sigmoid(x)
    return s * (1.0 + x * (1.0 - s))


def _rows(T, width, cb=0):
    return pl.BlockSpec((T, width), lambda i: (i, cb))


def _const(shape):
    nd = len(shape)
    return pl.BlockSpec(shape, lambda i: (0,) * nd)


def _row_tile(S):
    return min(256, S)


def _exchange(srcs, scatter, name):
    n = len(srcs)
    shapes = [tuple(s.shape[1:]) if scatter else tuple(s.shape) for s in srcs]

    def body(*refs):
        src_refs, out_refs = refs[:n], refs[n:2 * n]
        send_sems, recv_sems, local_sems = refs[2 * n:]
        x, y, c = lax.axis_index("x"), lax.axis_index("y"), lax.axis_index("c")
        me = 4 * x + 2 * y + c
        owns, copies = [], []
        for a in range(n):
            def piece(d, a=a):
                return src_refs[a].at[d] if scatter else src_refs[a]

            own = pltpu.make_async_copy(piece(me), out_refs[a].at[me], local_sems.at[a])
            own.start()
            owns.append(own)
            for k in range(1, N_DEV):
                px, py, pc = x ^ ((k >> 2) & 1), y ^ ((k >> 1) & 1), c ^ (k & 1)
                peer = 4 * px + 2 * py + pc
                sem = a * (N_DEV - 1) + k - 1
                cp = pltpu.make_async_remote_copy(
                    src_ref=piece(peer), dst_ref=out_refs[a].at[me],
                    send_sem=send_sems.at[sem], recv_sem=recv_sems.at[sem],
                    device_id=(px, py, pc), device_id_type=pl.DeviceIdType.MESH)
                cp.start()
                arrival = pltpu.make_async_remote_copy(
                    src_ref=piece(peer), dst_ref=out_refs[a].at[peer],
                    send_sem=send_sems.at[sem], recv_sem=recv_sems.at[sem],
                    device_id=(x, y, c), device_id_type=pl.DeviceIdType.MESH)
                copies.append((cp, arrival))
        for _, arrival in copies:
            arrival.wait_recv()
        for cp, _ in copies:
            cp.wait_send()
        for own in owns:
            own.wait()

    return pl.pallas_call(
        body, name=name,
        out_shape=tuple(jax.ShapeDtypeStruct((N_DEV,) + shp, s.dtype) for shp, s in zip(shapes, srcs)),
        in_specs=[pl.BlockSpec(memory_space=pl.ANY)] * n,
        out_specs=tuple(pl.BlockSpec(memory_space=pl.ANY) for _ in range(n)),
        scratch_shapes=[pltpu.SemaphoreType.DMA((n * (N_DEV - 1),)),
                        pltpu.SemaphoreType.DMA((n * (N_DEV - 1),)),
                        pltpu.SemaphoreType.DMA((n,))],
    )(*srcs)


_HBM = pl.BlockSpec(memory_space=pltpu.HBM)
_SEM = pl.BlockSpec(memory_space=pltpu.SEMAPHORE)


def _peer(k):
    x, y, c = lax.axis_index("x"), lax.axis_index("y"), lax.axis_index("c")
    return x ^ ((k >> 2) & 1), y ^ ((k >> 1) & 1), c ^ (k & 1)


def _exchange_start(srcs, scatter, name):
    n = len(srcs)
    shapes = [tuple(s.shape[1:]) if scatter else tuple(s.shape) for s in srcs]
    slots = N_DEV - 1 if scatter else N_DEV
    lands = [lax.empty((slots,) + shp, s.dtype) for shp, s in zip(shapes, srcs)]
    if not scatter:
        here = 4 * lax.axis_index("x") + 2 * lax.axis_index("y") + lax.axis_index("c")
        lands = [lax.dynamic_update_index_in_dim(l, s, here, 0) for l, s in zip(lands, srcs)]

    def body(*refs):
        src_refs, land_refs = refs[:n], refs[n:2 * n]
        send_sems, recv_sems = refs[2 * n], refs[2 * n + 1]
        token = refs[4 * n + 2]
        me = 4 * lax.axis_index("x") + 2 * lax.axis_index("y") + lax.axis_index("c")
        for a in range(n):
            for k in range(1, N_DEV):
                px, py, pc = _peer(k)
                peer = 4 * px + 2 * py + pc
                pltpu.make_async_remote_copy(
                    src_ref=src_refs[a].at[peer] if scatter else src_refs[a],
                    dst_ref=land_refs[a].at[k - 1] if scatter else land_refs[a].at[me],
                    send_sem=send_sems.at[a * (N_DEV - 1) + k - 1], recv_sem=recv_sems.at[a * (N_DEV - 1) + k - 1],
                    device_id=(px, py, pc), device_id_type=pl.DeviceIdType.MESH).start()
        token[...] = jnp.zeros_like(token)

    hbm = lambda arrs: [pltpu.HBM(a.shape, a.dtype) for a in arrs]
    out = pl.pallas_call(
        body, name=name,
        out_shape=(pltpu.SemaphoreType.DMA((n * (N_DEV - 1),)), pltpu.SemaphoreType.DMA((n * (N_DEV - 1),)),
                   *hbm(srcs), *hbm(lands), jax.ShapeDtypeStruct((SUBLANES, LANES), F32)),
        in_specs=[_HBM] * (2 * n),
        out_specs=(_SEM, _SEM, *([_HBM] * (2 * n)), pl.BlockSpec(memory_space=pltpu.VMEM)),
        input_output_aliases={a: 2 + a for a in range(2 * n)},
        compiler_params=pltpu.CompilerParams(has_side_effects=pltpu.SideEffectType.DATAFLOW_SIDE_EFFECTING),
    )(*[pltpu.with_memory_space_constraint(s, pltpu.HBM) for s in srcs],
      *[pltpu.with_memory_space_constraint(l, pltpu.HBM) for l in lands])
    return (out[0], out[1], list(out[2:2 + n]), list(out[2 + n:2 + 2 * n])), out[2 + 2 * n]


def _exchange_wait(handle, scatter, after, name):
    send_sems, recv_sems, srcs, lands = handle
    n = len(srcs)
    after = list(after) if isinstance(after, (list, tuple)) else [after]

    def body(*refs):
        src_refs, land_refs = refs[:n], refs[n:2 * n]
        send_sems, recv_sems = refs[2 * n], refs[2 * n + 1]
        for a in range(n):
            for k in range(1, N_DEV):
                px, py, pc = _peer(k)
                peer = 4 * px + 2 * py + pc
                cp = pltpu.make_async_remote_copy(
                    src_ref=src_refs[a].at[peer] if scatter else src_refs[a],
                    dst_ref=land_refs[a].at[k - 1] if scatter else land_refs[a].at[peer],
                    send_sem=send_sems.at[a * (N_DEV - 1) + k - 1], recv_sem=recv_sems.at[a * (N_DEV - 1) + k - 1],
                    device_id=(px, py, pc), device_id_type=pl.DeviceIdType.MESH)
                cp.wait_send()
                cp.wait_recv()

    out = pl.pallas_call(
        body, name=name,
        out_shape=tuple(pltpu.HBM(a.shape, a.dtype) for a in srcs + lands),
        in_specs=[_HBM] * (2 * n) + [_SEM, _SEM] + [pl.BlockSpec(memory_space=pl.ANY)] * len(after),
        out_specs=tuple([_HBM] * (2 * n)),
        input_output_aliases={a: a for a in range(2 * n)},
        compiler_params=pltpu.CompilerParams(has_side_effects=pltpu.SideEffectType.DATAFLOW_SIDE_EFFECTING),
    )(*srcs, *lands, send_sems, recv_sems, *after)
    return list(out[:n]), list(out[n:])


def _pack(parts, dtype, row_mult):
    flat = jnp.concatenate([p.reshape(-1).astype(dtype) for p in parts])
    n = flat.shape[0]
    rows = -(-n // PACK_COLS)
    rows = -(-rows // row_mult) * row_mult
    flat = jnp.pad(flat, (0, rows * PACK_COLS - n))
    return flat.reshape(rows, PACK_COLS)


def _unpack(flat, shapes):
    out, off = [], 0
    for shp in shapes:
        n = math.prod(shp)
        out.append(flat[..., off:off + n].reshape(flat.shape[:-1] + tuple(shp)))
        off += n
    return out


_DIMS = {"nn": (((1,), (0,)), ((), ())), "nt": (((1,), (1,)), ((), ())), "tn": (((0,), (0,)), ((), ()))}


def _matmul(a, b, mode, out_dtype, name, tm=512, tn=512, tk=None, split_n=False):
    if mode == "nn":
        (M, K), (_, N) = a.shape, b.shape
    elif mode == "nt":
        (M, K), (N, _) = a.shape, b.shape
    else:
        (K, M), (_, N) = a.shape, b.shape
    tm, tn = min(tm, M), min(tn, N)
    tk = K if tk is None else min(tk, K)
    nk = K // tk
    assert M % tm == 0 and N % tn == 0 and K % tk == 0, (name, a.shape, b.shape)

    def body(a_ref, b_ref, o_ref, *scratch):
        p = lax.dot_general(a_ref[...].astype(BF16), b_ref[...].astype(BF16), _DIMS[mode],
                            preferred_element_type=F32)
        if split_n:
            o_ref[0] = p.astype(out_dtype)
        elif nk == 1:
            o_ref[...] = p.astype(out_dtype)
        else:
            acc = scratch[0]
            k = pl.program_id(2)

            @pl.when(k == 0)
            def _():
                acc[...] = p

            @pl.when(k > 0)
            def _():
                acc[...] += p

            @pl.when(k == nk - 1)
            def _():
                o_ref[...] = acc[...].astype(out_dtype)

    a_spec = (pl.BlockSpec((tk, tm), lambda i, j, k: (k, i)) if mode == "tn"
              else pl.BlockSpec((tm, tk), lambda i, j, k: (i, k)))
    b_spec = (pl.BlockSpec((tn, tk), lambda i, j, k: (j, k)) if mode == "nt"
              else pl.BlockSpec((tk, tn), lambda i, j, k: (k, j)))
    return pl.pallas_call(
        body, name=name, grid=(M // tm, N // tn, nk),
        out_shape=jax.ShapeDtypeStruct((N // tn, M, tn) if split_n else (M, N), out_dtype),
        in_specs=[a_spec, b_spec],
        out_specs=(pl.BlockSpec((1, tm, tn), lambda i, j, k: (j, i, 0)) if split_n
                   else pl.BlockSpec((tm, tn), lambda i, j, k: (i, j))),
        scratch_shapes=[pltpu.VMEM((tm, tn), F32)] if nk > 1 else [],
        compiler_params=_cparams(),
    )(a, b)


def _ada_fwd(c_all, ada_w, ada_b_cols):
    L, D, n = ada_w.shape

    def body(c_ref, w_ref, b_ref, o_ref):
        act = _silu(c_ref[...]).astype(BF16)
        o_ref[0] = jnp.dot(act, w_ref[0].astype(BF16), preferred_element_type=F32) + b_ref[0]

    return pl.pallas_call(
        body, name="ada_fwd", grid=(L,),
        out_shape=jax.ShapeDtypeStruct((L, N_DEV, n), F32),
        in_specs=[pl.BlockSpec((N_DEV, D), lambda l: (0, 0)),
                  pl.BlockSpec((1, D, n), lambda l: (l, 0, 0)),
                  pl.BlockSpec((1, 1, n), lambda l: (l, 0, 0))],
        out_specs=pl.BlockSpec((1, N_DEV, n), lambda l: (l, 0, 0)),
        compiler_params=_cparams(),
    )(c_all, ada_w, ada_b_cols)


def _ada_bwd(c_all_t, dmod_cols):
    D = c_all_t.shape[0]
    L, _, n = dmod_cols.shape

    def body(c_ref, d_ref, o_ref):
        act = _silu(c_ref[...])
        dm = d_ref[0]
        acc = act[:, 0:1] * dm[0:1, :]
        for b in range(1, N_DEV):
            acc = acc + act[:, b:b + 1] * dm[b:b + 1, :]
        o_ref[0] = acc

    return pl.pallas_call(
        body, name="ada_bwd", grid=(L,),
        out_shape=jax.ShapeDtypeStruct((L, D, n), F32),
        in_specs=[pl.BlockSpec((D, N_DEV), lambda l: (0, 0)),
                  pl.BlockSpec((1, N_DEV, n), lambda l: (l, 0, 0))],
        out_specs=pl.BlockSpec((1, D, n), lambda l: (l, 0, 0)),
        compiler_params=_cparams(),
    )(c_all_t, dmod_cols)


def _rope_tables(pos_col, inv_lane):
    S = pos_col.shape[0]
    T = _row_tile(S)
    half = QK_ROPE // 2

    def body(p_ref, f_ref, c_ref, up_ref, dn_ref):
        ang = p_ref[...] * f_ref[...]
        lane = lax.broadcasted_iota(jnp.int32, ang.shape, 1)
        first = (lane >= QK_NOPE) & (lane < QK_NOPE + half)
        second = (lane >= QK_NOPE + half) & (lane < QK_NOPE + QK_ROPE)
        cs, sn = jnp.cos(ang), jnp.sin(ang)
        c_ref[...] = jnp.where(first | second, cs, 1.0)
        up_ref[...] = jnp.where(first, -sn, 0.0)
        dn_ref[...] = jnp.where(second, sn, 0.0)

    tab = jax.ShapeDtypeStruct((S, HEAD_PAD), F32)
    return pl.pallas_call(
        body, name="rope_tables", grid=(S // T,),
        out_shape=(tab, tab, tab),
        in_specs=[_rows(T, 1), _const((1, HEAD_PAD))],
        out_specs=(_rows(T, HEAD_PAD),) * 3,
        compiler_params=_cparams(),
    )(pos_col, inv_lane)


def _rope(blk, ct, ut, dt):
    half = QK_ROPE // 2
    up = pltpu.roll(blk, HEAD_PAD - half, 1)
    dn = pltpu.roll(blk, half, 1)
    return blk * ct + up * ut + dn * dt


def _rope_t(d, ct, ut, dt):
    half = QK_ROPE // 2
    return d * ct + pltpu.roll(d * ut, half, 1) + pltpu.roll(d * dt, HEAD_PAD - half, 1)


def _pre_norm(x, g, scale, shift, name):
    S, D = x.shape
    T = _row_tile(S)

    def body(x_ref, g_ref, sc_ref, sh_ref, h_ref):
        xv = x_ref[...]
        rstd = lax.rsqrt(jnp.mean(xv * xv, axis=-1, keepdims=True) + EPS)
        h_ref[...] = ((xv * rstd) * g_ref[...] * (1.0 + sc_ref[...]) + sh_ref[...]).astype(BF16)

    return pl.pallas_call(
        body, name=name, grid=(S // T,),
        out_shape=jax.ShapeDtypeStruct((S, D), BF16),
        in_specs=[_rows(T, D), _const((1, D)), _const((1, D)), _const((1, D))],
        out_specs=_rows(T, D), compiler_params=_cparams(),
    )(x, g, scale, shift)


def _post_norm(x, y, g, gate, name):
    S, D = x.shape
    T = _row_tile(S)

    def body(x_ref, y_ref, g_ref, gt_ref, o_ref):
        yv = y_ref[...]
        rstd = lax.rsqrt(jnp.mean(yv * yv, axis=-1, keepdims=True) + EPS)
        o_ref[...] = x_ref[...] + gt_ref[...] * ((yv * rstd) * g_ref[...])

    return pl.pallas_call(
        body, name=name, grid=(S // T,),
        out_shape=jax.ShapeDtypeStruct((S, D), F32),
        in_specs=[_rows(T, D), _rows(T, D), _const((1, D)), _const((1, D))],
        out_specs=_rows(T, D), compiler_params=_cparams(),
    )(x, y, g, gate)


def _fold8(v):
    T, C = v.shape
    return v.reshape(T // SUBLANES, SUBLANES, C).sum(axis=0)


def _col_sums(n_sums, body_fn, ins, in_specs, outs, out_specs, S, T, widths, name):
    n_in, n_out = len(ins), len(outs)
    nt = S // T

    def body(*refs):
        in_refs = refs[:n_in]
        out_refs = refs[n_in:n_in + n_out]
        sum_refs = refs[n_in + n_out:n_in + n_out + n_sums]
        accs = refs[n_in + n_out + n_sums:]
        i = pl.program_id(0)
        terms = body_fn(in_refs, out_refs)

        @pl.when(i == 0)
        def _():
            for acc, t in zip(accs, terms):
                acc[...] = _fold8(t)

        @pl.when(i > 0)
        def _():
            for acc, t in zip(accs, terms):
                acc[...] += _fold8(t)

        @pl.when(i == nt - 1)
        def _():
            for acc, s_ref in zip(accs, sum_refs):
                s_ref[...] = jnp.sum(acc[...], axis=0, keepdims=True)

    return pl.pallas_call(
        body, name=name, grid=(nt,),
        out_shape=tuple(outs) + tuple(jax.ShapeDtypeStruct((1, w), F32) for w in widths),
        in_specs=in_specs,
        out_specs=tuple(out_specs) + tuple(_const((1, w)) for w in widths),
        scratch_shapes=[pltpu.VMEM((SUBLANES, w), F32) for w in widths],
        compiler_params=_cparams(),
    )(*ins)


def _post_norm_bwd(dxo, y, g, gate, name):
    S, D = y.shape
    T = _row_tile(S)

    def fn(ins, outs):
        dxo_ref, y_ref, g_ref, gt_ref = ins
        yv, dv = y_ref[...], dxo_ref[...]
        rstd = lax.rsqrt(jnp.mean(yv * yv, axis=-1, keepdims=True) + EPS)
        yh = yv * rstd
        dn = dv * gt_ref[...]
        dyh = dn * g_ref[...]
        outs[0][...] = (rstd * (dyh - yh * jnp.mean(dyh * yh, axis=-1, keepdims=True))).astype(BF16)
        return [dv * (yh * g_ref[...]), dn * yh]

    return _col_sums(2, fn, [dxo, y, g, gate],
                     [_rows(T, D), _rows(T, D), _const((1, D)), _const((1, D))],
                     [jax.ShapeDtypeStruct((S, D), BF16)], [_rows(T, D)], S, T, [D, D], name)


def _pre_norm_bwd(dh, x, dxo, g, scale, name):
    S, D = x.shape
    T = _row_tile(S)

    def fn(ins, outs):
        dh_ref, x_ref, dxo_ref, g_ref, sc_ref = ins
        xv, dv = x_ref[...], dh_ref[...]
        rstd = lax.rsqrt(jnp.mean(xv * xv, axis=-1, keepdims=True) + EPS)
        xh = xv * rstd
        dr = dv * (1.0 + sc_ref[...])
        dxh = dr * g_ref[...]
        outs[0][...] = dxo_ref[...] + rstd * (dxh - xh * jnp.mean(dxh * xh, axis=-1, keepdims=True))
        return [dv, dv * (xh * g_ref[...]), dr * xh]

    return _col_sums(3, fn, [dh, x, dxo, g, scale],
                     [_rows(T, D), _rows(T, D), _rows(T, D), _const((1, D)), _const((1, D))],
                     [jax.ShapeDtypeStruct((S, D), F32)], [_rows(T, D)], S, T, [D, D, D], name)


def _post_pre_norm(x, y, g_post, gate, g_pre, scale, shift, name):
    S, D = x.shape
    T = _row_tile(S)

    def body(x_ref, y_ref, gp_ref, gt_ref, g_ref, sc_ref, sh_ref, xn_ref, h_ref):
        yv = y_ref[...]
        rstd_y = lax.rsqrt(jnp.mean(yv * yv, axis=-1, keepdims=True) + EPS)
        xn = x_ref[...] + gt_ref[...] * ((yv * rstd_y) * gp_ref[...])
        xn_ref[...] = xn
        rstd = lax.rsqrt(jnp.mean(xn * xn, axis=-1, keepdims=True) + EPS)
        h_ref[...] = ((xn * rstd) * g_ref[...] * (1.0 + sc_ref[...]) + sh_ref[...]).astype(BF16)

    return pl.pallas_call(
        body, name=name, grid=(S // T,),
        out_shape=(jax.ShapeDtypeStruct((S, D), F32), jax.ShapeDtypeStruct((S, D), BF16)),
        in_specs=[_rows(T, D), _rows(T, D)] + [_const((1, D))] * 5,
        out_specs=(_rows(T, D), _rows(T, D)), compiler_params=_cparams(),
    )(x, y, g_post, gate, g_pre, scale, shift)


def _pre_post_norm_bwd(dh, x, dxo, g_pre, scale, y_prev, g_post_prev, gate_prev, name):
    S, D = x.shape
    T = _row_tile(S)

    def fn(ins, outs):
        dh_ref, x_ref, dxo_ref, g_ref, sc_ref, y_ref, gp_ref, gt_ref = ins
        xv, dv = x_ref[...], dh_ref[...]
        rstd = lax.rsqrt(jnp.mean(xv * xv, axis=-1, keepdims=True) + EPS)
        xh = xv * rstd
        dr = dv * (1.0 + sc_ref[...])
        dxh = dr * g_ref[...]
        dx = dxo_ref[...] + rstd * (dxh - xh * jnp.mean(dxh * xh, axis=-1, keepdims=True))
        outs[0][...] = dx
        yv = y_ref[...]
        rstd_y = lax.rsqrt(jnp.mean(yv * yv, axis=-1, keepdims=True) + EPS)
        yh = yv * rstd_y
        dn = dx * gt_ref[...]
        dyh = dn * gp_ref[...]
        outs[1][...] = (rstd_y * (dyh - yh * jnp.mean(dyh * yh, axis=-1, keepdims=True))).astype(BF16)
        return [dv, dv * (xh * g_ref[...]), dr * xh, dx * (yh * gp_ref[...]), dn * yh]

    return _col_sums(5, fn, [dh, x, dxo, g_pre, scale, y_prev, g_post_prev, gate_prev],
                     [_rows(T, D), _rows(T, D), _rows(T, D), _const((1, D)), _const((1, D)),
                      _rows(T, D), _const((1, D)), _const((1, D))],
                     [jax.ShapeDtypeStruct((S, D), F32), jax.ShapeDtypeStruct((S, D), BF16)],
                     [_rows(T, D), _rows(T, D)], S, T, [D] * 5, name)


def _loss_post_norm_bwd(x, y, g_post, gate, target, name):
    S, D = x.shape
    T = _row_tile(S)

    def fn(ins, outs):
        x_ref, y_ref, gp_ref, gt_ref, t_ref = ins
        yv = y_ref[...]
        rstd_y = lax.rsqrt(jnp.mean(yv * yv, axis=-1, keepdims=True) + EPS)
        yh = yv * rstd_y
        e = x_ref[...] + gt_ref[...] * (yh * gp_ref[...]) - t_ref[...]
        dx = e * (1.0 / D)
        outs[0][...] = dx
        dn = dx * gt_ref[...]
        dyh = dn * gp_ref[...]
        outs[1][...] = (rstd_y * (dyh - yh * jnp.mean(dyh * yh, axis=-1, keepdims=True))).astype(BF16)
        return [e * e, dx * (yh * gp_ref[...]), dn * yh]

    return _col_sums(3, fn, [x, y, g_post, gate, target],
                     [_rows(T, D), _rows(T, D), _const((1, D)), _const((1, D)), _rows(T, D)],
                     [jax.ShapeDtypeStruct((S, D), F32), jax.ShapeDtypeStruct((S, D), BF16)],
                     [_rows(T, D), _rows(T, D)], S, T, [D] * 3, name)


def _scaled_total(v, coef, name):
    def body(v_ref, o_ref):
        o_ref[...] = jnp.broadcast_to(jnp.sum(v_ref[...], axis=1, keepdims=True) * coef, (1, LANES))

    return pl.pallas_call(body, name=name, out_shape=jax.ShapeDtypeStruct((1, LANES), F32))(v)


def _loss_head(x, target):
    S, D = x.shape
    T = _row_tile(S)
    nt = S // T

    def body(x_ref, t_ref, l_ref, dx_ref, acc):
        i = pl.program_id(0)
        e = x_ref[...] - t_ref[...]
        dx_ref[...] = e * (1.0 / D)
        part = _fold8(e * e)

        @pl.when(i == 0)
        def _():
            acc[...] = part

        @pl.when(i > 0)
        def _():
            acc[...] += part

        @pl.when(i == nt - 1)
        def _():
            tot = jnp.sum(jnp.sum(acc[...], axis=0, keepdims=True), axis=1, keepdims=True)
            l_ref[...] = jnp.broadcast_to(tot * (0.5 / D), (1, LANES))

    return pl.pallas_call(
        body, name="loss_head", grid=(nt,),
        out_shape=(jax.ShapeDtypeStruct((1, LANES), F32), jax.ShapeDtypeStruct((S, D), F32)),
        in_specs=[_rows(T, D), _rows(T, D)],
        out_specs=(_const((1, LANES)), _rows(T, D)),
        scratch_shapes=[pltpu.VMEM((SUBLANES, D), F32)],
        compiler_params=_cparams(),
    )(x, target)


CONV_ROWS = 64


def _conv_halo(K):
    return SUBLANES if K - 1 <= SUBLANES else 32


def _conv_fwd(u, w, b, K, name):
    S, C = u.shape
    KP = w.shape[0]
    T, HB, RS = min(512, S), _conv_halo(K), CONV_ROWS
    ratio = T // HB

    def body(u_ref, h_ref, w_ref, b_ref, o_ref, ext):
        i = pl.program_id(1)
        ext[0:HB, :] = jnp.where(i > 0, h_ref[...], 0.0)
        ext[HB:HB + T, :] = u_ref[...]
        for r0 in range(0, T, RS):
            acc = jnp.broadcast_to(b_ref[...], (RS, LANES))
            for k in range(K):
                off = HB - (K - 1) + k + r0
                acc = acc + w_ref[k:k + 1, :] * ext[off:off + RS, :]
            o_ref[r0:r0 + RS, :] = acc

    return pl.pallas_call(
        body, name=name, grid=(C // LANES, S // T),
        out_shape=jax.ShapeDtypeStruct((S, C), F32),
        in_specs=[pl.BlockSpec((T, LANES), lambda c, i: (i, c)),
                  pl.BlockSpec((HB, LANES), lambda c, i: (jnp.maximum(i * ratio - 1, 0), c)),
                  pl.BlockSpec((KP, LANES), lambda c, i: (0, c)),
                  pl.BlockSpec((1, LANES), lambda c, i: (0, c))],
        out_specs=pl.BlockSpec((T, LANES), lambda c, i: (i, c)),
        scratch_shapes=[pltpu.VMEM((HB + T, LANES), F32)],
        compiler_params=_cparams(),
    )(u, u, w, b)


def _conv_bwd(d, u, w, K, name):
    S, C = u.shape
    KP = w.shape[0]
    T, HB, RS = min(512, S), _conv_halo(K), CONV_ROWS
    ratio = T // HB
    nt = S // T
    last_halo = S // HB - 1

    def body(d_ref, dn_ref, u_ref, up_ref, w_ref, du_ref, dw_ref, db_ref, extd, extu, dws, dbs):
        i = pl.program_id(1)
        extd[0:T, :] = d_ref[...]
        extd[T:T + HB, :] = jnp.where(i < nt - 1, dn_ref[...], 0.0)
        extu[0:HB, :] = jnp.where(i > 0, up_ref[...], 0.0)
        extu[HB:HB + T, :] = u_ref[...]

        @pl.when(i == 0)
        def _():
            dws[...] = jnp.zeros_like(dws)
            dbs[...] = jnp.zeros_like(dbs)

        for r0 in range(0, T, RS):
            acc = jnp.zeros((RS, LANES), F32)
            for k in range(K):
                off = (K - 1 - k) + r0
                acc = acc + w_ref[k:k + 1, :] * extd[off:off + RS, :]
            du_ref[r0:r0 + RS, :] = acc
            dch = d_ref[r0:r0 + RS, :]
            dbs[...] += _fold8(dch)
            for k in range(K):
                off = HB - (K - 1) + k + r0
                dws[k * SUBLANES:(k + 1) * SUBLANES, :] += _fold8(dch * extu[off:off + RS, :])

        @pl.when(i == nt - 1)
        def _():
            dw_ref[...] = jnp.zeros_like(dw_ref)
            for k in range(K):
                dw_ref[k:k + 1, :] = jnp.sum(dws[k * SUBLANES:(k + 1) * SUBLANES, :], axis=0, keepdims=True)
            db_ref[...] = jnp.sum(dbs[...], axis=0, keepdims=True)

    return pl.pallas_call(
        body, name=name, grid=(C // LANES, nt),
        out_shape=(jax.ShapeDtypeStruct((S, C), F32), jax.ShapeDtypeStruct((KP, C), F32),
                   jax.ShapeDtypeStruct((1, C), F32)),
        in_specs=[pl.BlockSpec((T, LANES), lambda c, i: (i, c)),
                  pl.BlockSpec((HB, LANES), lambda c, i: (jnp.minimum((i + 1) * ratio, last_halo), c)),
                  pl.BlockSpec((T, LANES), lambda c, i: (i, c)),
                  pl.BlockSpec((HB, LANES), lambda c, i: (jnp.maximum(i * ratio - 1, 0), c)),
                  pl.BlockSpec((KP, LANES), lambda c, i: (0, c))],
        out_specs=(pl.BlockSpec((T, LANES), lambda c, i: (i, c)),
                   pl.BlockSpec((KP, LANES), lambda c, i: (0, c)),
                   pl.BlockSpec((1, LANES), lambda c, i: (0, c))),
        scratch_shapes=[pltpu.VMEM((T + HB, LANES), F32), pltpu.VMEM((HB + T, LANES), F32),
                        pltpu.VMEM((KP * SUBLANES, LANES), F32), pltpu.VMEM((SUBLANES, LANES), F32)],
        compiler_params=_cparams(),
    )(d, d, u, u, w)


SCW = 512
ZE = 3072
QL = 256
KVL = 128


def _rms_rows(x, g):
    rstd = lax.rsqrt(jnp.mean(x * x, axis=-1, keepdims=True) + EPS)
    return (x * rstd) * g


def _even_pre(z, qg, kvg, name):
    S = z.shape[0]
    T = _row_tile(S)

    def body(ac_ref, ax_ref, cq_ref, ckv_ref, qg_ref, kvg_ref, u_ref, qn_ref, kvn_ref):
        u_ref[...] = _f32(ac_ref) * _f32(ax_ref)
        qn_ref[...] = _rms_rows(_f32(cq_ref), qg_ref[...]).astype(BF16)
        kvn_ref[...] = _rms_rows(_f32(ckv_ref), kvg_ref[...]).astype(BF16)

    return pl.pallas_call(
        body, name=name, grid=(S // T,),
        out_shape=(jax.ShapeDtypeStruct((S, SCW), F32), jax.ShapeDtypeStruct((S, QL), BF16),
                   jax.ShapeDtypeStruct((S, KVL), BF16)),
        in_specs=[_rows(T, SCW, 1), _rows(T, SCW, 2), _rows(T, QL, 10), _rows(T, KVL, 22),
                  _const((1, QL)), _const((1, KVL))],
        out_specs=(_rows(T, SCW), _rows(T, QL), _rows(T, KVL)),
        compiler_params=_cparams(),
    )(z, z, z, z, qg, kvg)


def _qkv_fwd(qn, kvn, z, tabs, w_q, w_kv, name):
    S = qn.shape[0]
    T = _row_tile(S)
    HW = HEADS * HEAD_PAD
    scale = 1.0 / math.sqrt(QK_NOPE + QK_ROPE)

    def body(qn_ref, kvn_ref, kr_ref, ct_ref, ut_ref, dt_ref, wq_ref, wkv_ref, q_ref, k_ref, v_ref):
        ct, ut, dt = ct_ref[...], ut_ref[...], dt_ref[...]
        qa = jnp.dot(qn_ref[...], wq_ref[...], preferred_element_type=F32)
        kva = jnp.dot(kvn_ref[...], wkv_ref[...], preferred_element_type=F32)
        kr = _f32(kr_ref)
        for h in range(HEADS):
            sl = slice(h * HEAD_PAD, (h + 1) * HEAD_PAD)
            q_ref[:, sl] = (_rope(qa[:, sl], ct, ut, dt) * scale).astype(BF16)
            k_ref[:, sl] = _rope(kva[:, sl] + kr, ct, ut, dt).astype(BF16)
        v_ref[...] = kva[:, HW:].astype(BF16)

    return pl.pallas_call(
        body, name=name, grid=(S // T,),
        out_shape=(jax.ShapeDtypeStruct((S, HW), BF16), jax.ShapeDtypeStruct((S, HW), BF16),
                   jax.ShapeDtypeStruct((S, HEADS * V_HEAD), BF16)),
        in_specs=[_rows(T, QL), _rows(T, KVL), _rows(T, HEAD_PAD, 23),
                  _rows(T, HEAD_PAD), _rows(T, HEAD_PAD), _rows(T, HEAD_PAD),
                  _const(w_q.shape), _const(w_kv.shape)],
        out_specs=(_rows(T, HW), _rows(T, HW), _rows(T, HEADS * V_HEAD)),
        compiler_params=_cparams(),
    )(qn, kvn, z, *tabs, w_q, w_kv)


def _attn_tile(S):
    return min(256, S)


def _chunk_mask(TQ):
    r = lax.broadcasted_iota(jnp.int32, (TQ, TQ), 0) // CHUNK
    c = lax.broadcasted_iota(jnp.int32, (TQ, TQ), 1) // CHUNK
    return c <= r


_NT = (((1,), (1,)), ((), ()))
_TN = (((0,), (0,)), ((), ()))


def _attn_fwd(q, k, v, name):
    S = q.shape[0]
    TQ = _attn_tile(S)
    nq = S // TQ
    PW = 2 * HEAD_PAD

    def body(q_ref, k_ref, v_ref, o_ref, lse_ref, m_s, l_s, acc_s):
        i = pl.program_id(1)
        left = lax.broadcasted_iota(jnp.int32, (TQ, LANES), 1) < V_HEAD
        m_s[...] = jnp.full_like(m_s, NEG)
        l_s[...] = jnp.zeros_like(l_s)
        acc_s[...] = jnp.zeros_like(acc_s)
        qv = q_ref[...]

        def step(j, masked):
            r0 = pl.multiple_of(j * TQ, TQ)
            kb = k_ref[pl.ds(r0, TQ), :]
            vb = v_ref[pl.ds(r0, TQ), :]
            alphas, pvs = [], []
            for h in range(2):
                sl = slice(h * HEAD_PAD, (h + 1) * HEAD_PAD)
                s = lax.dot_general(qv[:, sl], kb[:, sl], _NT, preferred_element_type=F32)
                if masked:
                    s = jnp.where(_chunk_mask(TQ), s, NEG)
                m_prev = m_s[h]
                m_new = jnp.maximum(m_prev, jnp.max(s, axis=1, keepdims=True))
                alpha = jnp.exp(m_prev - m_new)
                p = jnp.exp(s - m_new[:, 0:1])
                l_s[h] = alpha * l_s[h] + jnp.sum(p, axis=1, keepdims=True)
                m_s[h] = m_new
                alphas.append(alpha)
                pvs.append(jnp.dot(p.astype(BF16), vb, preferred_element_type=F32))
            acc_s[...] = acc_s[...] * jnp.where(left, alphas[0], alphas[1]) + jnp.where(left, pvs[0], pvs[1])

        def loop_body(j, carry):
            step(j, False)
            return carry

        lax.fori_loop(0, i, loop_body, 0)
        step(i, True)
        o_ref[...] = acc_s[...] / jnp.where(left, l_s[0], l_s[1])
        lse_ref[...] = jnp.where(left, m_s[0] + jnp.log(l_s[0]), m_s[1] + jnp.log(l_s[1]))

    return pl.pallas_call(
        body, name=name, grid=(HEADS // 2, nq),
        out_shape=(jax.ShapeDtypeStruct((S, HEADS * V_HEAD), F32), jax.ShapeDtypeStruct((S, HEADS * V_HEAD), F32)),
        in_specs=[pl.BlockSpec((TQ, PW), lambda p, i: (i, p)),
                  pl.BlockSpec((S, PW), lambda p, i: (0, p)),
                  pl.BlockSpec((S, LANES), lambda p, i: (0, p))],
        out_specs=(pl.BlockSpec((TQ, LANES), lambda p, i: (i, p)),
                   pl.BlockSpec((TQ, LANES), lambda p, i: (i, p))),
        scratch_shapes=[pltpu.VMEM((2, TQ, LANES), F32), pltpu.VMEM((2, TQ, LANES), F32),
                        pltpu.VMEM((TQ, LANES), F32)],
        compiler_params=_cparams(),
    )(q, k, v)


def _attn_dq(q, k, v, do, lse, delta, name):
    S = q.shape[0]
    TQ = _attn_tile(S)
    nq = S // TQ
    PW = 2 * HEAD_PAD

    def body(q_ref, k_ref, v_ref, do_ref, lse_ref, dl_ref, dq_ref, acc_s):
        i = pl.program_id(1)
        left = lax.broadcasted_iota(jnp.int32, (TQ, LANES), 1) < V_HEAD
        acc_s[...] = jnp.zeros_like(acc_s)
        qv = q_ref[...]
        dov = do_ref[...]
        dos = [jnp.where(left, dov, jnp.zeros_like(dov)), jnp.where(left, jnp.zeros_like(dov), dov)]
        lses = [lse_ref[:, 0:1], lse_ref[:, V_HEAD:V_HEAD + 1]]
        dls = [dl_ref[:, 0:1], dl_ref[:, V_HEAD:V_HEAD + 1]]

        def step(j, masked):
            r0 = pl.multiple_of(j * TQ, TQ)
            kb = k_ref[pl.ds(r0, TQ), :]
            vb = v_ref[pl.ds(r0, TQ), :]
            for h in range(2):
                sl = slice(h * HEAD_PAD, (h + 1) * HEAD_PAD)
                s = lax.dot_general(qv[:, sl], kb[:, sl], _NT, preferred_element_type=F32)
                p = jnp.exp(s - lses[h])
                if masked:
                    p = jnp.where(_chunk_mask(TQ), p, 0.0)
                dp = lax.dot_general(dos[h], vb, _NT, preferred_element_type=F32)
                ds = (p * (dp - dls[h])).astype(BF16)
                acc_s[:, sl] += jnp.dot(ds, kb[:, sl], preferred_element_type=F32)

        def loop_body(j, carry):
            step(j, False)
            return carry

        lax.fori_loop(0, i, loop_body, 0)
        step(i, True)
        dq_ref[...] = acc_s[...]

    return pl.pallas_call(
        body, name=name, grid=(HEADS // 2, nq),
        out_shape=jax.ShapeDtypeStruct((S, HEADS * HEAD_PAD), F32),
        in_specs=[pl.BlockSpec((TQ, PW), lambda p, i: (i, p)),
                  pl.BlockSpec((S, PW), lambda p, i: (0, p)),
                  pl.BlockSpec((S, LANES), lambda p, i: (0, p)),
                  pl.BlockSpec((TQ, LANES), lambda p, i: (i, p)),
                  pl.BlockSpec((TQ, LANES), lambda p, i: (i, p)),
                  pl.BlockSpec((TQ, LANES), lambda p, i: (i, p))],
        out_specs=pl.BlockSpec((TQ, PW), lambda p, i: (i, p)),
        scratch_shapes=[pltpu.VMEM((TQ, PW), F32)],
        compiler_params=_cparams(),
    )(q, k, v, do, lse, delta)


def _attn_dkv(q, k, v, do, lse, delta, name):
    S = q.shape[0]
    TQ = _attn_tile(S)
    nq = S // TQ
    PW = 2 * HEAD_PAD

    def body(q_ref, k_ref, v_ref, do_ref, lse_ref, dl_ref, dk_ref, dv_ref, dk_s, dv_s):
        j = pl.program_id(1)
        left = lax.broadcasted_iota(jnp.int32, (TQ, LANES), 1) < V_HEAD
        dk_s[...] = jnp.zeros_like(dk_s)
        dv_s[...] = jnp.zeros_like(dv_s)
        kb = k_ref[...]
        vb = v_ref[...]

        def step(i, masked):
            r0 = pl.multiple_of(i * TQ, TQ)
            qb = q_ref[pl.ds(r0, TQ), :]
            dov = do_ref[pl.ds(r0, TQ), :]
            lse = lse_ref[pl.ds(r0, TQ), :]
            dl = dl_ref[pl.ds(r0, TQ), :]
            dos = [jnp.where(left, dov, jnp.zeros_like(dov)), jnp.where(left, jnp.zeros_like(dov), dov)]
            for h in range(2):
                sl = slice(h * HEAD_PAD, (h + 1) * HEAD_PAD)
                c0 = h * V_HEAD
                s = lax.dot_general(qb[:, sl], kb[:, sl], _NT, preferred_element_type=F32)
                p = jnp.exp(s - lse[:, c0:c0 + 1])
                if masked:
                    p = jnp.where(_chunk_mask(TQ), p, 0.0)
                dv_s[...] += lax.dot_general(p.astype(BF16), dos[h], _TN, preferred_element_type=F32)
                dp = lax.dot_general(dos[h], vb, _NT, preferred_element_type=F32)
                ds = (p * (dp - dl[:, c0:c0 + 1])).astype(BF16)
                dk_s[:, sl] += lax.dot_general(ds, qb[:, sl], _TN, preferred_element_type=F32)

        def loop_body(i, carry):
            step(i, False)
            return carry

        step(j, True)
        lax.fori_loop(j + 1, nq, loop_body, 0)
        dk_ref[...] = dk_s[...]
        dv_ref[...] = dv_s[...]

    return pl.pallas_call(
        body, name=name, grid=(HEADS // 2, nq),
        out_shape=(jax.ShapeDtypeStruct((S, HEADS * HEAD_PAD), F32), jax.ShapeDtypeStruct((S, HEADS * V_HEAD), F32)),
        in_specs=[pl.BlockSpec((S, PW), lambda p, j: (0, p)),
                  pl.BlockSpec((TQ, PW), lambda p, j: (j, p)),
                  pl.BlockSpec((TQ, LANES), lambda p, j: (j, p)),
                  pl.BlockSpec((S, LANES), lambda p, j: (0, p)),
                  pl.BlockSpec((S, LANES), lambda p, j: (0, p)),
                  pl.BlockSpec((S, LANES), lambda p, j: (0, p))],
        out_specs=(pl.BlockSpec((TQ, PW), lambda p, j: (j, p)),
                   pl.BlockSpec((TQ, LANES), lambda p, j: (j, p))),
        scratch_shapes=[pltpu.VMEM((TQ, PW), F32), pltpu.VMEM((TQ, LANES), F32)],
        compiler_params=_cparams(),
    )(q, k, v, do, lse, delta)


LOG2E = math.log2(math.e)
ATTN_FWD_HEADS = 8
ATTN_BWD_HEADS = 4


def _chunk_mask_t(T):
    key = lax.broadcasted_iota(jnp.int32, (T, T), 0) // CHUNK
    qry = lax.broadcasted_iota(jnp.int32, (T, T), 1) // CHUNK
    return key <= qry


W_IN_ROWS = 512


def _w_in_even(h, w_t, qg, kvg, name):
    S, D = h.shape
    tm = min(W_IN_ROWS, S)

    def body(h_ref, w_ref, qg_ref, kvg_ref, z_ref, u_ref, qn_ref, kvn_ref):
        p = lax.dot_general(h_ref[...], w_ref[...], _NT, preferred_element_type=F32)
        z_ref[...] = p.astype(BF16)
        u_ref[...] = p[:, SCW:2 * SCW] * p[:, 2 * SCW:3 * SCW]
        qn_ref[...] = _rms_rows(p[:, 5 * SCW:5 * SCW + QL], qg_ref[...]).astype(BF16)
        kvn_ref[...] = _rms_rows(p[:, 5 * SCW + QL:5 * SCW + QL + KVL], kvg_ref[...]).astype(BF16)

    return pl.pallas_call(
        body, name=name, grid=(S // tm,),
        out_shape=(jax.ShapeDtypeStruct((S, ZE), BF16), jax.ShapeDtypeStruct((S, SCW), F32),
                   jax.ShapeDtypeStruct((S, QL), BF16), jax.ShapeDtypeStruct((S, KVL), BF16)),
        in_specs=[_rows(tm, D), _const(w_t.shape), _const((1, QL)), _const((1, KVL))],
        out_specs=(_rows(tm, ZE), _rows(tm, SCW), _rows(tm, QL), _rows(tm, KVL)),
        compiler_params=_cparams(),
    )(h, w_t, qg, kvg)


def _w_in_odd(h, w, name):
    S, D = h.shape
    tm = min(W_IN_ROWS, S)

    def body(h_ref, w_ref, z_ref, u_ref):
        p = jnp.dot(h_ref[...], w_ref[...], preferred_element_type=F32)
        z_ref[...] = p.astype(BF16)
        u_ref[...] = p[:, 0:D] * _sigmoid(p[:, D:2 * D])

    return pl.pallas_call(
        body, name=name, grid=(S // tm,),
        out_shape=(jax.ShapeDtypeStruct((S, 3 * D), BF16), jax.ShapeDtypeStruct((S, D), F32)),
        in_specs=[_rows(tm, D), _const(w.shape)],
        out_specs=(_rows(tm, 3 * D), _rows(tm, D)),
        compiler_params=_cparams(),
    )(h, w)


def _qkv_fwd_t(qn, kvn, z, tabs, w_q, w_kv, name):
    S = qn.shape[0]
    T = _attn_tile(S)
    HW = HEADS * HEAD_PAD
    scale = LOG2E / math.sqrt(QK_NOPE + QK_ROPE)

    def body(qn_ref, kvn_ref, kr_ref, ct_ref, ut_ref, dt_ref, wq_ref, wkv_ref, q_ref, k_ref, v_ref, kt_ref, vt_ref):
        ct, ut, dt = ct_ref[...], ut_ref[...], dt_ref[...]
        qa = jnp.dot(qn_ref[...], wq_ref[...], preferred_element_type=F32)
        kva = jnp.dot(kvn_ref[...], wkv_ref[...], preferred_element_type=F32)
        kr = _f32(kr_ref)
        ones_row = (lax.broadcasted_iota(jnp.int32, (V_HEAD, T), 0) == 0).astype(F32)
        for h in range(HEADS):
            sl = slice(h * HEAD_PAD, (h + 1) * HEAD_PAD)
            q_ref[:, sl] = (_rope(qa[:, sl], ct, ut, dt) * scale).astype(BF16)
            kh = _rope(kva[:, sl] + kr, ct, ut, dt)
            k_ref[:, sl] = kh.astype(BF16)
            kt_ref[0, sl, :] = kh.T.astype(BF16)
        v_ref[...] = kva[:, HW:].astype(BF16)
        for p in range(HEADS // 2):
            vpt = kva[:, HW + p * LANES:HW + (p + 1) * LANES].T
            for h in range(2):
                r0 = (2 * p + h) * HEAD_PAD
                vt_ref[0, r0:r0 + V_HEAD, :] = vpt[h * V_HEAD:(h + 1) * V_HEAD, :].astype(BF16)
                vt_ref[0, r0 + V_HEAD:r0 + HEAD_PAD, :] = ones_row.astype(BF16)

    t3 = jax.ShapeDtypeStruct((S // T, HW, T), BF16)
    return pl.pallas_call(
        body, name=name, grid=(S // T,),
        out_shape=(jax.ShapeDtypeStruct((S, HW), BF16), jax.ShapeDtypeStruct((S, HW), BF16),
                   jax.ShapeDtypeStruct((S, HEADS * V_HEAD), BF16), t3, t3),
        in_specs=[_rows(T, QL), _rows(T, KVL), _rows(T, HEAD_PAD, 23),
                  _rows(T, HEAD_PAD), _rows(T, HEAD_PAD), _rows(T, HEAD_PAD),
                  _const(w_q.shape), _const(w_kv.shape)],
        out_specs=(_rows(T, HW), _rows(T, HW), _rows(T, HEADS * V_HEAD),
                   pl.BlockSpec((1, HW, T), lambda i: (i, 0, 0)), pl.BlockSpec((1, HW, T), lambda i: (i, 0, 0))),
        compiler_params=_cparams(),
    )(qn, kvn, z, *tabs, w_q, w_kv)


def _attn_fwd_t(q, k, vT3, name):
    S = q.shape[0]
    T = _attn_tile(S)
    nq = S // T
    NH = ATTN_FWD_HEADS
    PW = NH * HEAD_PAD

    def body(q_ref, k_ref, vt_ref, o_ref, lse_ref, m_s, acc_s):
        i = pl.program_id(1)
        m_s[...] = jnp.full_like(m_s, NEG)
        acc_s[...] = jnp.zeros_like(acc_s)
        qv = q_ref[...]

        def step(j, masked):
            kb = k_ref[pl.ds(pl.multiple_of(j * T, T), T), :]
            vt = vt_ref[j]
            heads = [slice(h * HEAD_PAD, (h + 1) * HEAD_PAD) for h in range(NH)]
            sts = [lax.dot_general(kb[:, sl], qv[:, sl], _NT, preferred_element_type=F32) for sl in heads]
            alphas, pvs = [], []
            for h, sl in enumerate(heads):
                st = jnp.where(_chunk_mask_t(T), sts[h], NEG) if masked else sts[h]
                m_prev = m_s[h]
                m_new = jnp.maximum(m_prev, jnp.max(st, axis=0, keepdims=True))
                alphas.append(jnp.exp2(m_prev[0:1] - m_new[0:1]))
                pt = jnp.exp2(st - m_new[0:1]).astype(BF16)
                m_s[h] = m_new
                pvs.append(jnp.dot(vt[sl, :], pt, preferred_element_type=F32))
            for h in range(NH):
                acc_s[h] = acc_s[h] * alphas[h] + pvs[h]

        def loop_body(j, carry):
            step(j, False)
            return carry

        lax.fori_loop(0, i, loop_body, 0)
        step(i, True)
        for g in range(NH // 2):
            outs = []
            for h in (2 * g, 2 * g + 1):
                acc = acc_s[h]
                l_row = acc[V_HEAD:V_HEAD + 1, :]
                outs.append(acc[0:V_HEAD, :] / l_row)
                lse_ref[0, h * SUBLANES:(h + 1) * SUBLANES, :] = m_s[h] + jnp.log2(l_row)
            o_ref[:, g * LANES:(g + 1) * LANES] = jnp.concatenate(outs, axis=0).T

    return pl.pallas_call(
        body, name=name, grid=(HEADS // NH, nq),
        out_shape=(jax.ShapeDtypeStruct((S, HEADS * V_HEAD), F32),
                   jax.ShapeDtypeStruct((nq, HEADS * SUBLANES, T), F32)),
        in_specs=[pl.BlockSpec((T, PW), lambda p, i: (i, p)),
                  pl.BlockSpec((S, PW), lambda p, i: (0, p)),
                  pl.BlockSpec((nq, PW, T), lambda p, i: (0, p, 0))],
        out_specs=(pl.BlockSpec((T, NH * V_HEAD), lambda p, i: (i, p)),
                   pl.BlockSpec((1, NH * SUBLANES, T), lambda p, i: (i, p, 0))),
        scratch_shapes=[pltpu.VMEM((NH, SUBLANES, T), F32), pltpu.VMEM((NH, HEAD_PAD, T), F32)],
        compiler_params=_cparams(),
    )(q, k, vT3)


def _attn_bwd_t(q, k, v, kT3, do, lse3, dl3, name):
    S = q.shape[0]
    T = _attn_tile(S)
    nq = S // T
    NH = ATTN_BWD_HEADS
    PW = NH * HEAD_PAD
    VW = NH * V_HEAD

    def body(q_ref, k_ref, v_ref, kt_ref, do_ref, lse_ref, dl_ref, dq_ref, dk_ref, dv_ref, dk_s, dv_s):
        j = pl.program_id(1)
        left = lax.broadcasted_iota(jnp.int32, (T, LANES), 1) < V_HEAD

        @pl.when(j == 0)
        def _():
            dq_ref[...] = jnp.zeros_like(dq_ref)

        dk_s[...] = jnp.zeros_like(dk_s)
        dv_s[...] = jnp.zeros_like(dv_s)
        kb = k_ref[...]
        vms = []
        for g in range(NH // 2):
            vb = v_ref[:, g * LANES:(g + 1) * LANES]
            vms += [jnp.where(left, vb, jnp.zeros_like(vb)), jnp.where(left, jnp.zeros_like(vb), vb)]
        kt = kt_ref[0]

        def step(i, masked):
            r0 = pl.multiple_of(i * T, T)
            qb = q_ref[pl.ds(r0, T), :]
            do_all = do_ref[pl.ds(r0, T), :]
            lse = lse_ref[i]
            dl = dl_ref[i]
            heads = [slice(h * HEAD_PAD, (h + 1) * HEAD_PAD) for h in range(NH)]
            dobs = [do_all[:, (h // 2) * LANES:(h // 2 + 1) * LANES] for h in range(NH)]
            sts = [lax.dot_general(kb[:, sl], qb[:, sl], _NT, preferred_element_type=F32) for sl in heads]
            dpts = [lax.dot_general(vms[h], dobs[h], _NT, preferred_element_type=F32) for h in range(NH)]
            res = []
            for h, sl in enumerate(heads):
                r8 = h * SUBLANES
                pt = jnp.exp2(sts[h] - lse[r8:r8 + 1, :])
                if masked:
                    pt = jnp.where(_chunk_mask_t(T), pt, 0.0)
                dst = (pt * (dpts[h] - dl[r8:r8 + 1, :])).astype(BF16)
                res.append((jnp.dot(pt.astype(BF16), dobs[h], preferred_element_type=F32),
                            jnp.dot(dst, qb[:, sl], preferred_element_type=F32),
                            jnp.dot(kt[sl, :], dst, preferred_element_type=F32)))
            for h, sl in enumerate(heads):
                dv_s[h] += res[h][0]
                dk_s[:, sl] += res[h][1]
                dq_ref[i, sl, :] += res[h][2]

        def loop_body(i, carry):
            step(i, False)
            return carry

        step(j, True)
        lax.fori_loop(j + 1, nq, loop_body, 0)
        dk_ref[...] = dk_s[...] * (1.0 / LOG2E)
        for g in range(NH // 2):
            dv_ref[:, g * LANES:(g + 1) * LANES] = jnp.where(left, dv_s[2 * g], dv_s[2 * g + 1])

    return pl.pallas_call(
        body, name=name, grid=(HEADS // NH, nq),
        out_shape=(jax.ShapeDtypeStruct((nq, HEADS * HEAD_PAD, T), F32),
                   jax.ShapeDtypeStruct((S, HEADS * HEAD_PAD), F32), jax.ShapeDtypeStruct((S, HEADS * V_HEAD), F32)),
        in_specs=[pl.BlockSpec((S, PW), lambda p, j: (0, p)),
                  pl.BlockSpec((T, PW), lambda p, j: (j, p)),
                  pl.BlockSpec((T, VW), lambda p, j: (j, p)),
                  pl.BlockSpec((1, PW, T), lambda p, j: (j, p, 0)),
                  pl.BlockSpec((S, VW), lambda p, j: (0, p)),
                  pl.BlockSpec((nq, NH * SUBLANES, T), lambda p, j: (0, p, 0)),
                  pl.BlockSpec((nq, NH * SUBLANES, T), lambda p, j: (0, p, 0))],
        out_specs=(pl.BlockSpec((nq, PW, T), lambda p, j: (0, p, 0)),
                   pl.BlockSpec((T, PW), lambda p, j: (j, p)),
                   pl.BlockSpec((T, VW), lambda p, j: (j, p))),
        scratch_shapes=[pltpu.VMEM((T, PW), F32), pltpu.VMEM((NH, T, LANES), F32)],
        compiler_params=_cparams(),
    )(q, k, v, kT3, do, lse3, dl3)


def _even_post(z, cv, o, name):
    S = z.shape[0]
    T = _row_tile(S)

    def body(ab_ref, ag_ref, bg_ref, cv_ref, o_ref, y_ref):
        y_ref[:, 0:SCW] = (_f32(ab_ref) * cv_ref[...] * _silu(_f32(ag_ref))).astype(BF16)
        y_ref[:, SCW:2 * SCW] = (o_ref[...] * _silu(_f32(bg_ref))).astype(BF16)

    return pl.pallas_call(
        body, name=name, grid=(S // T,),
        out_shape=jax.ShapeDtypeStruct((S, 2 * SCW), BF16),
        in_specs=[_rows(T, SCW, 0), _rows(T, SCW, 3), _rows(T, SCW, 4), _rows(T, SCW), _rows(T, SCW)],
        out_specs=_rows(T, 2 * SCW), compiler_params=_cparams(),
    )(z, z, z, cv, o)


def _even_bwd_gates(dyc, z, cv, o, name):
    S = z.shape[0]
    T = _row_tile(S)

    def body(dya_ref, dyb_ref, ab_ref, ag_ref, bg_ref, cv_ref, o_ref,
             dab_ref, dag_ref, dbg_ref, dcv_ref, do_ref, dl_ref):
        dya, ab, ag, cv = dya_ref[...], _f32(ab_ref), _f32(ag_ref), cv_ref[...]
        sg = _silu(ag)
        dab_ref[...] = (dya * cv * sg).astype(BF16)
        dcv_ref[...] = dya * ab * sg
        dag_ref[...] = (dya * ab * cv * _dsilu(ag)).astype(BF16)
        dyb, bg, ov = dyb_ref[...], _f32(bg_ref), o_ref[...]
        dov = dyb * _silu(bg)
        do_ref[...] = dov.astype(BF16)
        dbg_ref[...] = (dyb * ov * _dsilu(bg)).astype(BF16)
        prod = dov * ov
        left = lax.broadcasted_iota(jnp.int32, (T, LANES), 1) < V_HEAD
        for p in range(HEADS // 2):
            blk = prod[:, p * LANES:(p + 1) * LANES]
            s0 = jnp.sum(jnp.where(left, blk, 0.0), axis=1, keepdims=True)
            s1 = jnp.sum(jnp.where(left, 0.0, blk), axis=1, keepdims=True)
            dt = jnp.where(left, s0, s1).T
            dl_ref[0, 2 * p * SUBLANES:(2 * p + 1) * SUBLANES, :] = dt[0:SUBLANES, :]
            dl_ref[0, (2 * p + 1) * SUBLANES:(2 * p + 2) * SUBLANES, :] = dt[V_HEAD:V_HEAD + SUBLANES, :]

    assert T == _attn_tile(S)
    bf = jax.ShapeDtypeStruct((S, SCW), BF16)
    ff = jax.ShapeDtypeStruct((S, SCW), F32)
    return pl.pallas_call(
        body, name=name, grid=(S // T,),
        out_shape=(bf, bf, bf, ff, bf, jax.ShapeDtypeStruct((S // T, HEADS * SUBLANES, T), F32)),
        in_specs=[_rows(T, SCW, 0), _rows(T, SCW, 1), _rows(T, SCW, 0), _rows(T, SCW, 3), _rows(T, SCW, 4),
                  _rows(T, SCW), _rows(T, SCW)],
        out_specs=(_rows(T, SCW),) * 5 + (pl.BlockSpec((1, HEADS * SUBLANES, T), lambda i: (i, 0, 0)),),
        compiler_params=_cparams(),
    )(dyc, dyc, z, z, z, cv, o)


def _qkv_bwd(dq, dk, dv, z, tabs, w_q, w_kv, qg, kvg, name):
    S = dk.shape[0]
    T = _attn_tile(S)
    HW = HEADS * HEAD_PAD
    VW = HEADS * V_HEAD
    scale = 1.0 / math.sqrt(QK_NOPE + QK_ROPE)

    def fn(ins, outs):
        dq_ref, dk_ref, dv_ref, cq_ref, ckv_ref, ct_ref, ut_ref, dt_ref, wq_ref, wkv_ref, qg_ref, kvg_ref = ins
        dqp_ref, dkvp_ref, dcq_ref, dckv_ref, dkr_ref = outs
        ct, ut, dt = ct_ref[...], ut_ref[...], dt_ref[...]
        dkr = jnp.zeros((T, HEAD_PAD), F32)
        for h in range(HEADS):
            sl = slice(h * HEAD_PAD, (h + 1) * HEAD_PAD)
            dqp_ref[:, sl] = (_rope_t(dq_ref[0, sl, :].T, ct, ut, dt) * scale).astype(BF16)
            dkh = _rope_t(dk_ref[:, sl], ct, ut, dt)
            dkr = dkr + dkh
            dkvp_ref[:, sl] = dkh.astype(BF16)
        dkvp_ref[:, HW:] = dv_ref[...].astype(BF16)
        dkr_ref[...] = dkr.astype(BF16)
        sums = []
        for lat_ref, g_ref, dpre_ref, w_ref, dlat_ref in ((cq_ref, qg_ref, dqp_ref, wq_ref, dcq_ref),
                                                         (ckv_ref, kvg_ref, dkvp_ref, wkv_ref, dckv_ref)):
            dn = lax.dot_general(dpre_ref[...], w_ref[...], _NT, preferred_element_type=F32)
            xv = _f32(lat_ref)
            rstd = lax.rsqrt(jnp.mean(xv * xv, axis=-1, keepdims=True) + EPS)
            xh = xv * rstd
            dxh = dn * g_ref[...]
            dlat_ref[...] = (rstd * (dxh - xh * jnp.mean(dxh * xh, axis=-1, keepdims=True))).astype(BF16)
            sums.append(dn * xh)
        return sums

    return _col_sums(
        2, fn, [dq, dk, dv, z, z, *tabs, w_q, w_kv, qg, kvg],
        [pl.BlockSpec((1, HW, T), lambda i: (i, 0, 0)), _rows(T, HW), _rows(T, VW), _rows(T, QL, 10), _rows(T, KVL, 22),
         _rows(T, HEAD_PAD), _rows(T, HEAD_PAD), _rows(T, HEAD_PAD),
         _const(w_q.shape), _const(w_kv.shape), _const((1, QL)), _const((1, KVL))],
        [jax.ShapeDtypeStruct((S, HW), BF16), jax.ShapeDtypeStruct((S, HW + VW), BF16),
         jax.ShapeDtypeStruct((S, QL), BF16), jax.ShapeDtypeStruct((S, KVL), BF16),
         jax.ShapeDtypeStruct((S, HEAD_PAD), BF16)],
        [_rows(T, HW), _rows(T, HW + VW), _rows(T, QL), _rows(T, KVL), _rows(T, HEAD_PAD)],
        S, T, [QL, KVL], name)


def _even_dz(dab, du, z, dag, dbg, dcq, dckv, dkr, name):
    S = z.shape[0]
    T = _row_tile(S)

    def body(dab_ref, du_ref, ac_ref, ax_ref, dag_ref, dbg_ref, dcq_ref, dckv_ref, dkr_ref, dz_ref):
        duv = du_ref[...]
        dz_ref[:, 0:SCW] = dab_ref[...]
        dz_ref[:, SCW:2 * SCW] = (duv * _f32(ax_ref)).astype(BF16)
        dz_ref[:, 2 * SCW:3 * SCW] = (duv * _f32(ac_ref)).astype(BF16)
        dz_ref[:, 3 * SCW:4 * SCW] = dag_ref[...]
        dz_ref[:, 4 * SCW:5 * SCW] = dbg_ref[...]
        dz_ref[:, 5 * SCW:5 * SCW + QL] = dcq_ref[...]
        dz_ref[:, 5 * SCW + QL:5 * SCW + QL + KVL] = dckv_ref[...]
        dz_ref[:, 5 * SCW + QL + KVL:ZE] = dkr_ref[...]

    return pl.pallas_call(
        body, name=name, grid=(S // T,),
        out_shape=jax.ShapeDtypeStruct((S, ZE), BF16),
        in_specs=[_rows(T, SCW), _rows(T, SCW), _rows(T, SCW, 1), _rows(T, SCW, 2), _rows(T, SCW), _rows(T, SCW),
                  _rows(T, QL), _rows(T, KVL), _rows(T, HEAD_PAD)],
        out_specs=_rows(T, ZE), compiler_params=_cparams(),
    )(dab, du, z, z, dag, dbg, dcq, dckv, dkr)


def _odd_pre(z, name):
    S, D = z.shape[0], z.shape[1] // 3
    T = _row_tile(S)

    def body(val_ref, glu_ref, u_ref):
        u_ref[...] = _f32(val_ref) * _sigmoid(_f32(glu_ref))

    return pl.pallas_call(
        body, name=name, grid=(S // T,),
        out_shape=jax.ShapeDtypeStruct((S, D), F32),
        in_specs=[_rows(T, D, 0), _rows(T, D, 1)], out_specs=_rows(T, D),
        compiler_params=_cparams(),
    )(z, z)


def _layer_norm_stats(cv):
    mu = jnp.mean(cv, axis=-1, keepdims=True)
    cen = cv - mu
    rstd = lax.rsqrt(jnp.mean(cen * cen, axis=-1, keepdims=True) + EPS)
    return cen * rstd, rstd


def _odd_post(cv, z, ln_g, ln_b, name):
    S, D = cv.shape
    T = _row_tile(S)

    def body(cv_ref, sg_ref, g_ref, b_ref, y_ref):
        cvh, _ = _layer_norm_stats(cv_ref[...])
        y_ref[...] = (_silu(cvh * g_ref[...] + b_ref[...]) * _silu(_f32(sg_ref))).astype(BF16)

    return pl.pallas_call(
        body, name=name, grid=(S // T,),
        out_shape=jax.ShapeDtypeStruct((S, D), BF16),
        in_specs=[_rows(T, D), _rows(T, D, 2), _const((1, D)), _const((1, D))],
        out_specs=_rows(T, D), compiler_params=_cparams(),
    )(cv, z, ln_g, ln_b)


def _odd_bwd_norm(dyi, cv, z, ln_g, ln_b, name):
    S, D = cv.shape
    T = _row_tile(S)

    def fn(ins, outs):
        dy_ref, cv_ref, sg_ref, g_ref, b_ref = ins
        dcv_ref, dsg_ref = outs
        cvh, rstd = _layer_norm_stats(cv_ref[...])
        ln = cvh * g_ref[...] + b_ref[...]
        sgv, dy = _f32(sg_ref), dy_ref[...]
        dsg_ref[...] = (dy * _silu(ln) * _dsilu(sgv)).astype(BF16)
        dln = dy * _silu(sgv) * _dsilu(ln)
        dh = dln * g_ref[...]
        dcv_ref[...] = rstd * (dh - jnp.mean(dh, axis=-1, keepdims=True)
                               - cvh * jnp.mean(dh * cvh, axis=-1, keepdims=True))
        return [dln * cvh, dln]

    return _col_sums(2, fn, [dyi, cv, z, ln_g, ln_b],
                     [_rows(T, D), _rows(T, D), _rows(T, D, 2), _const((1, D)), _const((1, D))],
                     [jax.ShapeDtypeStruct((S, D), F32), jax.ShapeDtypeStruct((S, D), BF16)],
                     [_rows(T, D), _rows(T, D)], S, T, [D, D], name)


def _odd_dz(du, z, dsg, name):
    S, D = du.shape
    T = _row_tile(S)

    def body(du_ref, val_ref, glu_ref, dsg_ref, dz_ref):
        duv = du_ref[...]
        sig = _sigmoid(_f32(glu_ref))
        dz_ref[:, 0:D] = (duv * sig).astype(BF16)
        dz_ref[:, D:2 * D] = (duv * _f32(val_ref) * sig * (1.0 - sig)).astype(BF16)
        dz_ref[:, 2 * D:3 * D] = dsg_ref[...]

    return pl.pallas_call(
        body, name=name, grid=(S // T,),
        out_shape=jax.ShapeDtypeStruct((S, 3 * D), BF16),
        in_specs=[_rows(T, D), _rows(T, D, 0), _rows(T, D, 1), _rows(T, D)],
        out_specs=_rows(T, 3 * D), compiler_params=_cparams(),
    )(du, z, z, dsg)


ADAM_BLOCK_ELEMS = 128 * 1024


def _adam_tiles(R, C):
    if R * C <= ADAM_BLOCK_ELEMS:
        return R, C
    tr = R
    for cand in range(SUBLANES, R, SUBLANES):
        if R % cand == 0 and cand * C <= ADAM_BLOCK_ELEMS:
            tr = cand
    if tr < R:
        return tr, C
    tc = C
    for cand in range(LANES, C, LANES):
        if C % cand == 0 and R * cand <= ADAM_BLOCK_ELEMS:
            tc = cand
    return R, tc


def _adamw(g_parts, w, m, v, name):
    if not isinstance(g_parts, (list, tuple)):
        g_parts = [g_parts]
    ng = len(g_parts)
    _, R, C = g_parts[0].shape
    tr, tc = _adam_tiles(R, C)

    def body(*refs):
        g_refs = refs[:ng]
        w_ref, m_ref, v_ref, go_ref, d_ref, mo_ref, vo_ref = refs[ng:]
        g = None
        for g_ref in g_refs:
            for p in range(g_ref.shape[0]):
                part = g_ref[p].astype(F32)
                g = part if g is None else g + part
        mn = ADAM_B1 * m_ref[...] + (1.0 - ADAM_B1) * g
        vn = ADAM_B2 * v_ref[...] + (1.0 - ADAM_B2) * (g * g)
        m_hat = mn / (1.0 - ADAM_B1 ** ADAM_STEP)
        v_hat = vn / (1.0 - ADAM_B2 ** ADAM_STEP)
        go_ref[...] = g
        d_ref[...] = -ADAM_LR * (m_hat / (jnp.sqrt(v_hat) + ADAM_EPS) + ADAM_WD * w_ref[...])
        mo_ref[...] = mn
        vo_ref[...] = vn

    slab = jax.ShapeDtypeStruct((R, C), F32)
    blk = pl.BlockSpec((tr, tc), lambda i, j: (i, j))
    return pl.pallas_call(
        body, name=name, grid=(R // tr, C // tc),
        out_shape=(slab,) * 4,
        in_specs=[pl.BlockSpec((g.shape[0], tr, tc), lambda i, j: (0, i, j)) for g in g_parts] + [blk, blk, blk],
        out_specs=(blk,) * 4, compiler_params=_cparams(),
    )(*g_parts, w, m, v)


def _adamw_slab(g_parts, w, m, v, layer, prev, name):
    ng = len(g_parts)
    NL, R, C = w.shape
    tr, tc = _adam_tiles(R, C)

    def body(*refs):
        g_refs = refs[:ng]
        w_ref, m_ref, v_ref = refs[ng:ng + 3]
        go_ref, d_ref, mo_ref, vo_ref = refs[-4:]
        g = None
        for g_ref in g_refs:
            for p in range(g_ref.shape[0]):
                part = g_ref[p].astype(F32)
                g = part if g is None else g + part
        mn = ADAM_B1 * m_ref[0] + (1.0 - ADAM_B1) * g
        vn = ADAM_B2 * v_ref[0] + (1.0 - ADAM_B2) * (g * g)
        m_hat = mn / (1.0 - ADAM_B1 ** ADAM_STEP)
        v_hat = vn / (1.0 - ADAM_B2 ** ADAM_STEP)
        go_ref[0] = g
        d_ref[0] = -ADAM_LR * (m_hat / (jnp.sqrt(v_hat) + ADAM_EPS) + ADAM_WD * w_ref[0])
        mo_ref[0] = mn
        vo_ref[0] = vn

    blk = pl.BlockSpec((1, tr, tc), lambda i, j: (layer, i, j))
    n_in = ng + 3
    prev = list(prev) if prev is not None else []
    return pl.pallas_call(
        body, name=name, grid=(R // tr, C // tc),
        out_shape=(jax.ShapeDtypeStruct((NL, R, C), F32),) * 4,
        in_specs=([pl.BlockSpec((g.shape[0], tr, tc), lambda i, j: (0, i, j)) for g in g_parts] + [blk, blk, blk]
                  + [pl.BlockSpec(memory_space=pl.ANY)] * len(prev)),
        out_specs=(blk,) * 4,
        input_output_aliases={n_in + k: k for k in range(len(prev))},
        compiler_params=_cparams(),
    )(*g_parts, w, m, v, *prev)


def _gather_cols(g, shape):
    nd = len(shape)
    t = jnp.moveaxis(g, 0, nd - 1)
    return t.reshape(tuple(shape[:-1]) + (N_DEV * shape[-1],))


def _scatter_cols(full, n):
    t = full.reshape(full.shape[:-1] + (N_DEV, n))
    return jnp.moveaxis(t, -2, 0)


def kernel(x, c, positions, ada_w, ada_b, pre_norm_g, post_norm_g, even_w_in, even_sc_conv_w, even_sc_conv_b, even_q_norm_g, even_kv_norm_g, even_w_uq, even_w_ukv, even_w_out, odd_w_in, odd_conv_w, odd_conv_b, odd_ln_g, odd_ln_b, odd_w_out, loss_target, m_ada_w, m_ada_b, m_pre_norm_g, m_post_norm_g, m_even_w_in, m_even_sc_conv_w, m_even_sc_conv_b, m_even_q_norm_g, m_even_kv_norm_g, m_even_w_uq, m_even_w_ukv, m_even_w_out, m_odd_w_in, m_odd_conv_w, m_odd_conv_b, m_odd_ln_g, m_odd_ln_b, m_odd_w_out, v_ada_w, v_ada_b, v_pre_norm_g, v_post_norm_g, v_even_w_in, v_even_sc_conv_w, v_even_sc_conv_b, v_even_q_norm_g, v_even_kv_norm_g, v_even_w_uq, v_even_w_ukv, v_even_w_out, v_odd_w_in, v_odd_conv_w, v_odd_conv_b, v_odd_ln_g, v_odd_ln_b, v_odd_w_out):
    S, D = x.shape[1], x.shape[2]
    L = ada_w.shape[0]
    NE, NO = even_w_in.shape[0], odd_w_in.shape[0]
    me = 4 * lax.axis_index("x") + 2 * lax.axis_index("y") + lax.axis_index("c")
    x0 = x[0]
    target = loss_target[0]

    small_parts = [c, even_sc_conv_w, odd_conv_w, odd_conv_b, odd_ln_g, odd_ln_b]
    small_shapes = [p.shape for p in small_parts]
    sg = _exchange([_pack(small_parts, F32, SUBLANES)], False, "gather_small")[0].reshape(N_DEV, -1)
    c_all, scw_g, ocw_g, ocb_g, olg_g, olb_g = _unpack(sg, small_shapes)
    c_all = c_all.reshape(N_DEV, D)
    sc_conv_w = _gather_cols(scw_g, even_sc_conv_w.shape)
    o_conv_w = _gather_cols(ocw_g, odd_conv_w.shape)
    o_conv_b = _gather_cols(ocb_g, odd_conv_b.shape)
    o_ln_g = _gather_cols(olg_g, odd_ln_g.shape)
    o_ln_b = _gather_cols(olb_g, odd_ln_b.shape)

    pad_q = HEAD_PAD - QK_NOPE - QK_ROPE
    w_local = [jnp.swapaxes(even_w_in, 1, 2).astype(BF16),
               jnp.pad(even_w_uq, ((0, 0), (0, 0), (0, pad_q))).astype(BF16),
               jnp.pad(even_w_ukv[..., :QK_NOPE], ((0, 0), (0, 0), (0, HEAD_PAD - QK_NOPE))).astype(BF16),
               even_w_ukv[..., QK_NOPE:].astype(BF16),
               even_w_out.astype(BF16), odd_w_in.astype(BF16), odd_w_out.astype(BF16)]
    n_ada = ada_w.shape[2]
    ada_b_cols = lax.dynamic_slice_in_dim(ada_b, me * n_ada, n_ada, axis=1).reshape(L, 1, n_ada)
    mod_slab = _ada_fwd(c_all, ada_w, ada_b_cols)
    mod_g = _exchange([_pack([mod_slab], F32, SUBLANES)], False, "gather_mod")[0].reshape(N_DEV, -1)
    mod_all = mod_g[:, :L * N_DEV * n_ada].reshape(N_DEV, L, N_DEV, n_ada)
    mod = lax.dynamic_index_in_dim(mod_all, me, axis=2, keepdims=False)
    mod = jnp.moveaxis(mod, 0, 1).reshape(L, 3 * D)
    shift, scale, gate = mod[:, :D], mod[:, D:2 * D], mod[:, 2 * D:]

    heads_to_cols = lambda g: jnp.moveaxis(g, 0, 1).reshape(g.shape[1], -1)
    w_handles = {}
    token = jnp.broadcast_to(jnp.minimum(jnp.abs(mod[0, 0]), 0.0), (SUBLANES, LANES))
    for layer in range(L):
        i = layer // 2
        groups = ({"in": [w_local[0][i]], "rest": [w[i] for w in w_local[1:5]]} if layer % 2 == 0
                  else {"all": [w[i] for w in w_local[5:]]})
        for key, mine in groups.items():
            mine = [w + token[0, 0].astype(BF16) for w in mine]
            w_handles[layer, key], token = _exchange_start(mine, False, f"gather_weights_start_l{layer}_{key}")
    w_token = token

    def arrived(layer, key, after):
        return _exchange_wait(w_handles[layer, key], False, after, f"gather_weights_wait_l{layer}_{key}")[1]

    e_w_in_k, e_w_q_k, e_w_kv_k, e_w_out, o_w_in, o_w_out = ([None] * NE, [None] * NE, [None] * NE, [None] * NE,
                                                             [None] * NO, [None] * NO)

    half = QK_ROPE // 2
    inv_freq = 1.0 / (ROPE_THETA ** (jnp.arange(0, QK_ROPE, 2, dtype=F32) / QK_ROPE))
    inv_lane = jnp.zeros((HEAD_PAD,), F32).at[QK_NOPE:QK_NOPE + QK_ROPE].set(jnp.concatenate([inv_freq, inv_freq]))
    tabs = _rope_tables(positions.astype(F32).reshape(S, 1), inv_lane.reshape(1, HEAD_PAD))
    del half

    row = lambda a: a.reshape(1, -1)
    scb = even_sc_conv_b
    KP3, KP31 = SUBLANES, 32

    saved = []
    xs = x0
    h = _pre_norm(xs, row(pre_norm_g[0]) + w_token[0, 0], row(scale[0]), row(shift[0]), "pre_norm_l0")
    for layer in range(L):
        i = layer // 2
        tag = f"l{layer}"
        first = [h, tabs[0]] if layer == 0 else h
        if layer % 2 == 0:
            wt = arrived(layer, "in", first)[0].reshape(-1, D)
            e_w_in_k[i] = jnp.concatenate([wt[:2048], wt[2464:2976], wt[2048:2432], jnp.zeros((QK_NOPE, D), BF16),
                                           wt[2432:2464], jnp.zeros((pad_q, D), BF16)], axis=0)
            z, u, qn, kvn = _w_in_even(h, e_w_in_k[i], row(even_q_norm_g[i]), row(even_kv_norm_g[i]), f"w_in_{tag}")
            eq_g, ek_g, ev_g, eout_g = arrived(layer, "rest", z)
            e_w_q_k[i] = heads_to_cols(eq_g)
            e_w_kv_k[i] = jnp.concatenate([heads_to_cols(ek_g), heads_to_cols(ev_g)], axis=-1)
            e_w_out[i] = eout_g.reshape(-1, D)
            cw = jnp.pad(sc_conv_w[i], ((0, KP3 - SC_KERNEL), (0, 0)))
            cv = _conv_fwd(u, cw, row(scb[i]), SC_KERNEL, f"conv_{tag}")
            q, k, v, kT3, vT3 = _qkv_fwd_t(qn, kvn, z, tabs, e_w_q_k[i], e_w_kv_k[i], f"qkv_{tag}")
            o, lse = _attn_fwd_t(q, k, vT3, f"attn_{tag}")
            ycat = _even_post(z, cv, o, f"even_post_{tag}")
            y = _matmul(ycat, e_w_out[i], "nn", F32, f"w_out_{tag}", tn=1024)
            saved.append(dict(x=xs, h=h, z=z, u=u, qn=qn, kvn=kvn, cw=cw, cv=cv, q=q, k=k, v=v, kT3=kT3, o=o, lse=lse,
                              ycat=ycat, y=y))
        else:
            owin_g, oout_g = arrived(layer, "all", first)
            o_w_in[i], o_w_out[i] = heads_to_cols(owin_g), oout_g.reshape(-1, D)
            z, u = _w_in_odd(h, o_w_in[i], f"w_in_{tag}")
            cw = jnp.pad(o_conv_w[i], ((0, KP31 - CONF_KERNEL), (0, 0)))
            cv = _conv_fwd(u, cw, row(o_conv_b[i]), CONF_KERNEL, f"conv_{tag}")
            yin = _odd_post(cv, z, row(o_ln_g[i]), row(o_ln_b[i]), f"odd_post_{tag}")
            y = _matmul(yin, o_w_out[i], "nn", F32, f"w_out_{tag}", tn=1024)
            saved.append(dict(x=xs, h=h, z=z, u=u, cw=cw, cv=cv, yin=yin, y=y))
        if layer < L - 1:
            xs, h = _post_pre_norm(xs, y, row(post_norm_g[layer]), row(gate[layer]), row(pre_norm_g[layer + 1]),
                                   row(scale[layer + 1]), row(shift[layer + 1]), f"post_pre_norm_{tag}")

    dx, dy, err_sq, dgate, g_post_last = _loss_post_norm_bwd(xs, y, row(post_norm_g[L - 1]), row(gate[L - 1]), target,
                                                             "loss_post_norm_bwd")
    loss = lax.psum(_scaled_total(err_sq, 0.5 / D, "loss_total")[0, 0], MESH_AXES)

    g_pre, g_post, dmod = [None] * L, [None] * L, [None] * L
    g_e_w_in, g_e_w_uq, g_e_w_ukv, g_e_w_out = [None] * NE, [None] * NE, [None] * NE, [None] * NE
    g_scw, g_scb, g_qg, g_kvg = [None] * NE, [None] * NE, [None] * NE, [None] * NE
    g_o_w_in, g_o_w_out, g_ocw, g_ocb, g_olg, g_olb = ([None] * NO for _ in range(6))
    sm_w = [even_sc_conv_w, odd_conv_w, odd_conv_b, odd_ln_g, odd_ln_b]
    sm_rows = _pack(sm_w, F32, SUBLANES).shape[0]

    def small_slab():
        full = [_scatter_cols(jnp.stack(g_scw), even_sc_conv_w.shape[-1]),
                _scatter_cols(jnp.stack(g_ocw), odd_conv_w.shape[-1]),
                _scatter_cols(jnp.concatenate(g_ocb, 0), odd_conv_b.shape[-1]),
                _scatter_cols(jnp.concatenate(g_olg, 0), odd_ln_g.shape[-1]),
                _scatter_cols(jnp.concatenate(g_olb, 0), odd_ln_b.shape[-1])]
        flat = jnp.concatenate([g.reshape(N_DEV, -1) for g in full], axis=1)
        return jnp.pad(flat, ((0, 0), (0, sm_rows * PACK_COLS - flat.shape[1]))).reshape(N_DEV, sm_rows, PACK_COLS)

    scatters = []
    bw_token = jnp.zeros((SUBLANES, LANES), F32)

    def start_scatter(tag, names, parts):
        own = [lax.dynamic_slice_in_dim(g, me, 1, axis=0) for g in parts]
        handle, token = _exchange_start([g.astype(BF16) for g in parts], True, f"scatter_grads_start_{tag}")
        scatters.append((tag, names, handle, own))
        return token

    for layer in reversed(range(L)):
        i = layer // 2
        tag = f"l{layer}"
        sv = saved[layer]
        if layer == L - 1:
            g_post[layer] = g_post_last
        if layer % 2 == 0:
            dyc = _matmul(dy, e_w_out[i], "nt", F32, f"d_ycat_{tag}", tn=1024)
            g_e_w_out[i] = _matmul(sv["ycat"], dy, "tn", BF16, f"g_w_out_{tag}", tn=1024).reshape(N_DEV, -1, D)
            if layer == 0:
                bw_token = start_scatter("l0_out", [("even_w_out", i)], [g_e_w_out[i]])
            dab, dag, dbg, dcv, do, delta = _even_bwd_gates(dyc, sv["z"], sv["cv"], sv["o"], f"even_gates_bwd_{tag}")
            du, dcw, g_scb[i] = _conv_bwd(dcv, sv["u"], sv["cw"] + bw_token[0, 0], SC_KERNEL, f"conv_bwd_{tag}")
            g_scw[i] = dcw[:SC_KERNEL]
            dq, dk, dv = _attn_bwd_t(sv["q"], sv["k"], sv["v"], sv["kT3"], do, sv["lse"], delta, f"attn_bwd_{tag}")
            (dqp, dkvp, dcq, dckv, dkr, g_qg[i], g_kvg[i]) = _qkv_bwd(
                dq, dk, dv, sv["z"], tabs, e_w_q_k[i], e_w_kv_k[i],
                row(even_q_norm_g[i]), row(even_kv_norm_g[i]), f"qkv_bwd_{tag}")
            gq = _matmul(sv["qn"], dqp, "tn", BF16, f"g_w_uq_{tag}", tn=1024)
            gkv = _matmul(sv["kvn"], dkvp, "tn", BF16, f"g_w_ukv_{tag}")
            g_e_w_uq[i] = jnp.moveaxis(gq.reshape(QL, HEADS, HEAD_PAD)[..., :QK_NOPE + QK_ROPE], 1, 0)
            g_e_w_ukv[i] = jnp.moveaxis(jnp.concatenate(
                [gkv[:, :HEADS * HEAD_PAD].reshape(KVL, HEADS, HEAD_PAD)[..., :QK_NOPE],
                 gkv[:, HEADS * HEAD_PAD:].reshape(KVL, HEADS, V_HEAD)], axis=-1), 1, 0)
            dz = _even_dz(dab, du, sv["z"], dag, dbg, dcq, dckv, dkr, f"even_dz_{tag}")
            gt = _matmul(dz, sv["h"], "tn", BF16, f"g_w_in_{tag}", tm=1024, tn=1024)
            g_e_w_in[i] = jnp.concatenate([gt[:2048], gt[2560:2944], gt[2944 + QK_NOPE:2944 + QK_NOPE + QK_ROPE],
                                           gt[2048:2560]], axis=0).reshape(N_DEV, -1, D)
            names = [("even_w_in", i), ("even_w_uq", i), ("even_w_ukv", i)]
            parts = [g_e_w_in[i], g_e_w_uq[i], g_e_w_ukv[i]]
            if layer == 0:
                names, parts = names + [("small", 0)], parts + [small_slab()]
            else:
                names, parts = names + [("even_w_out", i)], parts + [g_e_w_out[i]]
            bw_token = start_scatter(tag, names, parts)
            w_dh = e_w_in_k[i] + bw_token[0, 0].astype(BF16) if layer == 0 else e_w_in_k[i]
            dh = _matmul(dz, w_dh, "nn", F32, f"d_h_{tag}", tn=1024)
        else:
            dyi = _matmul(dy, o_w_out[i], "nt", F32, f"d_yin_{tag}", tn=1024)
            g_o_w_out[i] = _matmul(sv["yin"], dy, "tn", BF16, f"g_w_out_{tag}", tn=1024).reshape(N_DEV, -1, D)
            dcv, dsg, g_olg[i], g_olb[i] = _odd_bwd_norm(dyi, sv["cv"], sv["z"], row(o_ln_g[i]), row(o_ln_b[i]),
                                                         f"odd_norm_bwd_{tag}")
            du, dcw, g_ocb[i] = _conv_bwd(dcv, sv["u"], sv["cw"], CONF_KERNEL, f"conv_bwd_{tag}")
            g_ocw[i] = dcw[:CONF_KERNEL]
            dz = _odd_dz(du, sv["z"], dsg, f"odd_dz_{tag}")
            g_o_w_in[i] = _matmul(sv["h"], dz, "tn", BF16, f"g_w_in_{tag}", tm=1024, tn=odd_w_in.shape[-1],
                                  split_n=True)
            bw_token = start_scatter(tag, [("odd_w_in", i), ("odd_w_out", i)], [g_o_w_in[i], g_o_w_out[i]])
            dh = _matmul(dz, o_w_in[i], "nt", F32, f"d_h_{tag}", tn=1024)
        g_row = row(pre_norm_g[layer]) + bw_token[0, 0]
        if layer > 0:
            (dx, dy, dshift, dscale, g_pre[layer], dgate_prev, g_post[layer - 1]) = _pre_post_norm_bwd(
                dh, sv["x"], dx, g_row, row(scale[layer]), saved[layer - 1]["y"], row(post_norm_g[layer - 1]),
                row(gate[layer - 1]), f"pre_post_norm_bwd_{tag}")
        else:
            dx, dshift, dscale, g_pre[layer] = _pre_norm_bwd(dh, sv["x"], dx, g_row, row(scale[layer]),
                                                             f"pre_norm_bwd_{tag}")
            dgate_prev = None
        dmod[layer] = jnp.concatenate([dshift, dscale, dgate], axis=-1)
        dgate = dgate_prev
    grad_x = dx.reshape(1, S, D)

    rep_g = [jnp.concatenate(dmod, 0), jnp.concatenate(g_pre, 0), jnp.concatenate(g_post, 0),
             jnp.stack(g_scb), jnp.stack(g_qg), jnp.stack(g_kvg)]
    rep_w = [ada_b, pre_norm_g, post_norm_g, even_sc_conv_b, even_q_norm_g, even_kv_norm_g]
    rep_m = [m_ada_b, m_pre_norm_g, m_post_norm_g, m_even_sc_conv_b, m_even_q_norm_g, m_even_kv_norm_g]
    rep_v = [v_ada_b, v_pre_norm_g, v_post_norm_g, v_even_sc_conv_b, v_even_q_norm_g, v_even_kv_norm_g]
    rep_shapes = [w.shape for w in rep_w]
    rep_all = _exchange([_pack(rep_g, F32, SUBLANES)], False, "gather_small_grads")[0]
    rep_out = _adamw(rep_all, _pack(rep_w, F32, SUBLANES), _pack(rep_m, F32, SUBLANES), _pack(rep_v, F32, SUBLANES),
                     "adamw_replicated")
    rep_res = [_unpack(o.reshape(-1), rep_shapes) for o in rep_out]

    dmod_all = rep_all.reshape(N_DEV, -1)[:, :L * 3 * D].reshape(N_DEV, L, 3 * D)
    dmod_cols = jnp.moveaxis(lax.dynamic_slice_in_dim(dmod_all, me * n_ada, n_ada, axis=2), 0, 1)
    g_ada_w = _ada_bwd(c_all.T, dmod_cols)
    ada_out = _adamw(g_ada_w.reshape(1, -1, n_ada), ada_w.reshape(-1, n_ada),
                     m_ada_w.reshape(-1, n_ada), v_ada_w.reshape(-1, n_ada), "adamw_ada_w")
    ada_res = [o.reshape(ada_w.shape) for o in ada_out]

    sm_m = [m_even_sc_conv_w, m_odd_conv_w, m_odd_conv_b, m_odd_ln_g, m_odd_ln_b]
    sm_v = [v_even_sc_conv_w, v_odd_conv_w, v_odd_conv_b, v_odd_ln_g, v_odd_ln_b]
    sm_shapes = [w.shape for w in sm_w]
    state = {"even_w_in": (even_w_in, m_even_w_in, v_even_w_in), "even_w_uq": (even_w_uq, m_even_w_uq, v_even_w_uq),
             "even_w_ukv": (even_w_ukv, m_even_w_ukv, v_even_w_ukv), "even_w_out": (even_w_out, m_even_w_out, v_even_w_out),
             "odd_w_in": (odd_w_in, m_odd_w_in, v_odd_w_in), "odd_w_out": (odd_w_out, m_odd_w_out, v_odd_w_out)}
    state["even_w_in"] = tuple(jnp.swapaxes(t, 1, 2) for t in state["even_w_in"])
    big_res = {name: None for name in state}
    after = [bw_token, grad_x, rep_out[0], ada_out[0]]
    sm_res = None
    for tag, names, handle, own in scatters:
        _, landed = _exchange_wait(handle, True, after, f"scatter_grads_wait_{tag}")
        after = []
        for a, (name, i) in enumerate(names):
            if name == "small":
                sm_out = _adamw([own[a], landed[a]], _pack(sm_w, F32, SUBLANES), _pack(sm_m, F32, SUBLANES),
                                _pack(sm_v, F32, SUBLANES), "adamw_small_sharded")
                sm_res = [_unpack(o.reshape(-1), sm_shapes) for o in sm_out]
                continue
            big_res[name] = _adamw_slab([own[a], landed[a]], *state[name], i, big_res[name], f"adamw_{name}_{i}")
            after += list(big_res[name])
    sh_res = [dict(zip(["even_sc_conv_w", "odd_conv_w", "odd_conv_b", "odd_ln_g", "odd_ln_b"], sm_res[kind]))
              for kind in range(4)]
    for name in state:
        for kind in range(4):
            res = big_res[name][kind]
            sh_res[kind][name] = jnp.swapaxes(res, 1, 2) if name == "even_w_in" else res

    order = ["ada_w", "ada_b", "pre_norm_g", "post_norm_g", "even_w_in", "even_sc_conv_w", "even_sc_conv_b",
             "even_q_norm_g", "even_kv_norm_g", "even_w_uq", "even_w_ukv", "even_w_out", "odd_w_in", "odd_conv_w",
             "odd_conv_b", "odd_ln_g", "odd_ln_b", "odd_w_out"]
    rep_names = ["ada_b", "pre_norm_g", "post_norm_g", "even_sc_conv_b", "even_q_norm_g", "even_kv_norm_g"]
    outs = [loss, grad_x]
    for kind in range(4):
        for name in order:
            if name == "ada_w":
                outs.append(ada_res[kind])
            elif name in rep_names:
                outs.append(rep_res[kind][rep_names.index(name)])
            else:
                outs.append(sh_res[kind][name])
    return tuple(outs)
```

```python
import functools
import math

import jax
import jax.numpy as jnp
from jax import lax
from jax.experimental import pallas as pl
from jax.experimental.pallas import tpu as pltpu

F32 = jnp.float32
BF16 = jnp.bfloat16
MESH_AXES = ("x", "y", "c")
N_DEV = 8
EPS = 1e-6
CHUNK = 64
HEADS = 8
QK_NOPE = 64
QK_ROPE = 32
V_HEAD = 64
HEAD_PAD = 128
ROPE_THETA = 10000.0
SC_KERNEL = 3
CONF_KERNEL = 31
LANES = 128
SUBLANES = 8
PACK_COLS = 1024
VMEM_LIMIT = 48 * 1024 * 1024
NEG = -1e30

ADAM_LR = 0.001
ADAM_B1 = 0.9
ADAM_B2 = 0.999
ADAM_EPS = 1e-08
ADAM_WD = 0.01
ADAM_STEP = 10


def _cparams():
    return pltpu.CompilerParams(vmem_limit_bytes=VMEM_LIMIT)


def _sigmoid(x):
    return 1.0 / (1.0 + jnp.exp(-x))


def _f32(ref):
    return ref[...].astype(F32)


def _silu(x):
    return x * _sigmoid(x)


def _dsilu(x):
    s = _sigmoid(x)
    return s * (1.0 + x * (1.0 - s))


def _rows(T, width, cb=0):
    return pl.BlockSpec((T, width), lambda i: (i, cb))


def _const(shape):
    nd = len(shape)
    return pl.BlockSpec(shape, lambda i: (0,) * nd)


def _row_tile(S):
    return min(256, S)


def _exchange(srcs, scatter, name):
    n = len(srcs)
    shapes = [tuple(s.shape[1:]) if scatter else tuple(s.shape) for s in srcs]

    def body(*refs):
        src_refs, out_refs = refs[:n], refs[n:2 * n]
        send_sems, recv_sems, local_sems = refs[2 * n:]
        x, y, c = lax.axis_index("x"), lax.axis_index("y"), lax.axis_index("c")
        me = 4 * x + 2 * y + c
        owns, copies = [], []
        for a in range(n):
            def piece(d, a=a):
                return src_refs[a].at[d] if scatter else src_refs[a]

            own = pltpu.make_async_copy(piece(me), out_refs[a].at[me], local_sems.at[a])
            own.start()
            owns.append(own)
            for k in range(1, N_DEV):
                px, py, pc = x ^ ((k >> 2) & 1), y ^ ((k >> 1) & 1), c ^ (k & 1)
                peer = 4 * px + 2 * py + pc
                sem = a * (N_DEV - 1) + k - 1
                cp = pltpu.make_async_remote_copy(
                    src_ref=piece(peer), dst_ref=out_refs[a].at[me],
                    send_sem=send_sems.at[sem], recv_sem=recv_sems.at[sem],
                    device_id=(px, py, pc), device_id_type=pl.DeviceIdType.MESH)
                cp.start()
                arrival = pltpu.make_async_remote_copy(
                    src_ref=piece(peer), dst_ref=out_refs[a].at[peer],
                    send_sem=send_sems.at[sem], recv_sem=recv_sems.at[sem],
                    device_id=(x, y, c), device_id_type=pl.DeviceIdType.MESH)
                copies.append((cp, arrival))
        for _, arrival in copies:
            arrival.wait_recv()
        for cp, _ in copies:
            cp.wait_send()
        for own in owns:
            own.wait()

    return pl.pallas_call(
        body, name=name,
        out_shape=tuple(jax.ShapeDtypeStruct((N_DEV,) + shp, s.dtype) for shp, s in zip(shapes, srcs)),
        in_specs=[pl.BlockSpec(memory_space=pl.ANY)] * n,
        out_specs=tuple(pl.BlockSpec(memory_space=pl.ANY) for _ in range(n)),
        scratch_shapes=[pltpu.SemaphoreType.DMA((n * (N_DEV - 1),)),
                        pltpu.SemaphoreType.DMA((n * (N_DEV - 1),)),
                        pltpu.SemaphoreType.DMA((n,))],
    )(*srcs)


_HBM = pl.BlockSpec(memory_space=pltpu.HBM)
_SEM = pl.BlockSpec(memory_space=pltpu.SEMAPHORE)


def _peer(k):
    x, y, c = lax.axis_index("x"), lax.axis_index("y"), lax.axis_index("c")
    return x ^ ((k >> 2) & 1), y ^ ((k >> 1) & 1), c ^ (k & 1)


def _exchange_start(srcs, scatter, name):
    n = len(srcs)
    shapes = [tuple(s.shape[1:]) if scatter else tuple(s.shape) for s in srcs]
    slots = N_DEV - 1 if scatter else N_DEV
    lands = [lax.empty((slots,) + shp, s.dtype) for shp, s in zip(shapes, srcs)]
    if not scatter:
        here = 4 * lax.axis_index("x") + 2 * lax.axis_index("y") + lax.axis_index("c")
        lands = [lax.dynamic_update_index_in_dim(l, s, here, 0) for l, s in zip(lands, srcs)]

    def body(*refs):
        src_refs, land_refs = refs[:n], refs[n:2 * n]
        send_sems, recv_sems = refs[2 * n], refs[2 * n + 1]
        token = refs[4 * n + 2]
        me = 4 * lax.axis_index("x") + 2 * lax.axis_index("y") + lax.axis_index("c")
        for a in range(n):
            for k in range(1, N_DEV):
                px, py, pc = _peer(k)
                peer = 4 * px + 2 * py + pc
                pltpu.make_async_remote_copy(
                    src_ref=src_refs[a].at[peer] if scatter else src_refs[a],
                    dst_ref=land_refs[a].at[k - 1] if scatter else land_refs[a].at[me],
                    send_sem=send_sems.at[a * (N_DEV - 1) + k - 1], recv_sem=recv_sems.at[a * (N_DEV - 1) + k - 1],
                    device_id=(px, py, pc), device_id_type=pl.DeviceIdType.MESH).start()
        token[...] = jnp.zeros_like(token)

    hbm = lambda arrs: [pltpu.HBM(a.shape, a.dtype) for a in arrs]
    out = pl.pallas_call(
        body, name=name,
        out_shape=(pltpu.SemaphoreType.DMA((n * (N_DEV - 1),)), pltpu.SemaphoreType.DMA((n * (N_DEV - 1),)),
                   *hbm(srcs), *hbm(lands), jax.ShapeDtypeStruct((SUBLANES, LANES), F32)),
        in_specs=[_HBM] * (2 * n),
        out_specs=(_SEM, _SEM, *([_HBM] * (2 * n)), pl.BlockSpec(memory_space=pltpu.VMEM)),
        input_output_aliases={a: 2 + a for a in range(2 * n)},
        compiler_params=pltpu.CompilerParams(has_side_effects=pltpu.SideEffectType.DATAFLOW_SIDE_EFFECTING),
    )(*[pltpu.with_memory_space_constraint(s, pltpu.HBM) for s in srcs],
      *[pltpu.with_memory_space_constraint(l, pltpu.HBM) for l in lands])
    return (out[0], out[1], list(out[2:2 + n]), list(out[2 + n:2 + 2 * n])), out[2 + 2 * n]


def _exchange_wait(handle, scatter, after, name):
    send_sems, recv_sems, srcs, lands = handle
    n = len(srcs)
    after = list(after) if isinstance(after, (list, tuple)) else [after]

    def body(*refs):
        src_refs, land_refs = refs[:n], refs[n:2 * n]
        send_sems, recv_sems = refs[2 * n], refs[2 * n + 1]
        for a in range(n):
            for k in range(1, N_DEV):
                px, py, pc = _peer(k)
                peer = 4 * px + 2 * py + pc
                cp = pltpu.make_async_remote_copy(
                    src_ref=src_refs[a].at[peer] if scatter else src_refs[a],
                    dst_ref=land_refs[a].at[k - 1] if scatter else land_refs[a].at[peer],
                    send_sem=send_sems.at[a * (N_DEV - 1) + k - 1], recv_sem=recv_sems.at[a * (N_DEV - 1) + k - 1],
                    device_id=(px, py, pc), device_id_type=pl.DeviceIdType.MESH)
                cp.wait_send()
                cp.wait_recv()

    out = pl.pallas_call(
        body, name=name,
        out_shape=tuple(pltpu.HBM(a.shape, a.dtype) for a in srcs + lands),
        in_specs=[_HBM] * (2 * n) + [_SEM, _SEM] + [pl.BlockSpec(memory_space=pl.ANY)] * len(after),
        out_specs=tuple([_HBM] * (2 * n)),
        input_output_aliases={a: a for a in range(2 * n)},
        compiler_params=pltpu.CompilerParams(has_side_effects=pltpu.SideEffectType.DATAFLOW_SIDE_EFFECTING),
    )(*srcs, *lands, send_sems, recv_sems, *after)
    return list(out[:n]), list(out[n:])


def _pack(parts, dtype, row_mult):
    flat = jnp.concatenate([p.reshape(-1).astype(dtype) for p in parts])
    n = flat.shape[0]
    rows = -(-n // PACK_COLS)
    rows = -(-rows // row_mult) * row_mult
    flat = jnp.pad(flat, (0, rows * PACK_COLS - n))
    return flat.reshape(rows, PACK_COLS)


def _unpack(flat, shapes):
    out, off = [], 0
    for shp in shapes:
        n = math.prod(shp)
        out.append(flat[..., off:off + n].reshape(flat.shape[:-1] + tuple(shp)))
        off += n
    return out


_DIMS = {"nn": (((1,), (0,)), ((), ())), "nt": (((1,), (1,)), ((), ())), "tn": (((0,), (0,)), ((), ()))}


def _matmul(a, b, mode, out_dtype, name, tm=512, tn=512, tk=None, split_n=False):
    if mode == "nn":
        (M, K), (_, N) = a.shape, b.shape
    elif mode == "nt":
        (M, K), (N, _) = a.shape, b.shape
    else:
        (K, M), (_, N) = a.shape, b.shape
    tm, tn = min(tm, M), min(tn, N)
    tk = K if tk is None else min(tk, K)
    nk = K // tk
    assert M % tm == 0 and N % tn == 0 and K % tk == 0, (name, a.shape, b.shape)

    def body(a_ref, b_ref, o_ref, *scratch):
        p = lax.dot_general(a_ref[...].astype(BF16), b_ref[...].astype(BF16), _DIMS[mode],
                            preferred_element_type=F32)
        if split_n:
            o_ref[0] = p.astype(out_dtype)
        elif nk == 1:
            o_ref[...] = p.astype(out_dtype)
        else:
            acc = scratch[0]
            k = pl.program_id(2)

            @pl.when(k == 0)
            def _():
                acc[...] = p

            @pl.when(k > 0)
            def _():
                acc[...] += p

            @pl.when(k == nk - 1)
            def _():
                o_ref[...] = acc[...].astype(out_dtype)

    a_spec = (pl.BlockSpec((tk, tm), lambda i, j, k: (k, i)) if mode == "tn"
              else pl.BlockSpec((tm, tk), lambda i, j, k: (i, k)))
    b_spec = (pl.BlockSpec((tn, tk), lambda i, j, k: (j, k)) if mode == "nt"
              else pl.BlockSpec((tk, tn), lambda i, j, k: (k, j)))
    return pl.pallas_call(
        body, name=name, grid=(M // tm, N // tn, nk),
        out_shape=jax.ShapeDtypeStruct((N // tn, M, tn) if split_n else (M, N), out_dtype),
        in_specs=[a_spec, b_spec],
        out_specs=(pl.BlockSpec((1, tm, tn), lambda i, j, k: (j, i, 0)) if split_n
                   else pl.BlockSpec((tm, tn), lambda i, j, k: (i, j))),
        scratch_shapes=[pltpu.VMEM((tm, tn), F32)] if nk > 1 else [],
        compiler_params=_cparams(),
    )(a, b)


def _ada_fwd(c_all, ada_w, ada_b_cols):
    L, D, n = ada_w.shape

    def body(c_ref, w_ref, b_ref, o_ref):
        act = _silu(c_ref[...]).astype(BF16)
        o_ref[0] = jnp.dot(act, w_ref[0].astype(BF16), preferred_element_type=F32) + b_ref[0]

    return pl.pallas_call(
        body, name="ada_fwd", grid=(L,),
        out_shape=jax.ShapeDtypeStruct((L, N_DEV, n), F32),
        in_specs=[pl.BlockSpec((N_DEV, D), lambda l: (0, 0)),
                  pl.BlockSpec((1, D, n), lambda l: (l, 0, 0)),
                  pl.BlockSpec((1, 1, n), lambda l: (l, 0, 0))],
        out_specs=pl.BlockSpec((1, N_DEV, n), lambda l: (l, 0, 0)),
        compiler_params=_cparams(),
    )(c_all, ada_w, ada_b_cols)


def _ada_bwd(c_all_t, dmod_cols):
    D = c_all_t.shape[0]
    L, _, n = dmod_cols.shape

    def body(c_ref, d_ref, o_ref):
        act = _silu(c_ref[...])
        dm = d_ref[0]
        acc = act[:, 0:1] * dm[0:1, :]
        for b in range(1, N_DEV):
            acc = acc + act[:, b:b + 1] * dm[b:b + 1, :]
        o_ref[0] = acc

    return pl.pallas_call(
        body, name="ada_bwd", grid=(L,),
        out_shape=jax.ShapeDtypeStruct((L, D, n), F32),
        in_specs=[pl.BlockSpec((D, N_DEV), lambda l: (0, 0)),
                  pl.BlockSpec((1, N_DEV, n), lambda l: (l, 0, 0))],
        out_specs=pl.BlockSpec((1, D, n), lambda l: (l, 0, 0)),
        compiler_params=_cparams(),
    )(c_all_t, dmod_cols)


def _rope_tables(pos_col, inv_lane):
    S = pos_col.shape[0]
    T = _row_tile(S)
    half = QK_ROPE // 2

    def body(p_ref, f_ref, c_ref, up_ref, dn_ref):
        ang = p_ref[...] * f_ref[...]
        lane = lax.broadcasted_iota(jnp.int32, ang.shape, 1)
        first = (lane >= QK_NOPE) & (lane < QK_NOPE + half)
        second = (lane >= QK_NOPE + half) & (lane < QK_NOPE + QK_ROPE)
        cs, sn = jnp.cos(ang), jnp.sin(ang)
        c_ref[...] = jnp.where(first | second, cs, 1.0)
        up_ref[...] = jnp.where(first, -sn, 0.0)
        dn_ref[...] = jnp.where(second, sn, 0.0)

    tab = jax.ShapeDtypeStruct((S, HEAD_PAD), F32)
    return pl.pallas_call(
        body, name="rope_tables", grid=(S // T,),
        out_shape=(tab, tab, tab),
        in_specs=[_rows(T, 1), _const((1, HEAD_PAD))],
        out_specs=(_rows(T, HEAD_PAD),) * 3,
        compiler_params=_cparams(),
    )(pos_col, inv_lane)


def _rope(blk, ct, ut, dt):
    half = QK_ROPE // 2
    up = pltpu.roll(blk, HEAD_PAD - half, 1)
    dn = pltpu.roll(blk, half, 1)
    return blk * ct + up * ut + dn * dt


def _rope_t(d, ct, ut, dt):
    half = QK_ROPE // 2
    return d * ct + pltpu.roll(d * ut, half, 1) + pltpu.roll(d * dt, HEAD_PAD - half, 1)


def _row_operands(rows, after=None):
    ops, specs = [], []
    for a in rows:
        if isinstance(a, tuple):
            table, r = a
            ops.append(table)
            specs.append(pl.BlockSpec((None, 1, table.shape[-1]), lambda *_, r=r: (r, 0, 0)))
        else:
            ops.append(a)
            specs.append(pl.BlockSpec(a.shape, lambda *_, nd=a.ndim: (0,) * nd))
    if after is not None:
        ops.append(after)
        specs.append(pl.BlockSpec(memory_space=pl.ANY))
    return ops, specs


def _pre_norm(x, g, scale, shift, name, after=None):
    S, D = x.shape
    T = _row_tile(S)
    row_ops, row_specs = _row_operands([g, scale, shift], after)

    def body(x_ref, g_ref, sc_ref, sh_ref, *rest):
        h_ref = rest[-1]
        xv = x_ref[...]
        rstd = lax.rsqrt(jnp.mean(xv * xv, axis=-1, keepdims=True) + EPS)
        h_ref[...] = ((xv * rstd) * g_ref[...] * (1.0 + sc_ref[...]) + sh_ref[...]).astype(BF16)

    return pl.pallas_call(
        body, name=name, grid=(S // T,),
        out_shape=jax.ShapeDtypeStruct((S, D), BF16),
        in_specs=[_rows(T, D)] + row_specs,
        out_specs=_rows(T, D), compiler_params=_cparams(),
    )(x, *row_ops)


def _post_norm(x, y, g, gate, name):
    S, D = x.shape
    T = _row_tile(S)

    def body(x_ref, y_ref, g_ref, gt_ref, o_ref):
        yv = y_ref[...]
        rstd = lax.rsqrt(jnp.mean(yv * yv, axis=-1, keepdims=True) + EPS)
        o_ref[...] = x_ref[...] + gt_ref[...] * ((yv * rstd) * g_ref[...])

    return pl.pallas_call(
        body, name=name, grid=(S // T,),
        out_shape=jax.ShapeDtypeStruct((S, D), F32),
        in_specs=[_rows(T, D), _rows(T, D), _const((1, D)), _const((1, D))],
        out_specs=_rows(T, D), compiler_params=_cparams(),
    )(x, y, g, gate)


def _fold8(v):
    T, C = v.shape
    return v.reshape(T // SUBLANES, SUBLANES, C).sum(axis=0)


def _col_sums(n_sums, body_fn, ins, in_specs, outs, out_specs, S, T, widths, name):
    n_in, n_out = len(ins), len(outs)
    nt = S // T

    def body(*refs):
        in_refs = refs[:n_in]
        out_refs = refs[n_in:n_in + n_out]
        sum_refs = refs[n_in + n_out:n_in + n_out + n_sums]
        accs = refs[n_in + n_out + n_sums:]
        i = pl.program_id(0)
        terms = body_fn(in_refs, out_refs)

        @pl.when(i == 0)
        def _():
            for acc, t in zip(accs, terms):
                acc[...] = _fold8(t)

        @pl.when(i > 0)
        def _():
            for acc, t in zip(accs, terms):
                acc[...] += _fold8(t)

        @pl.when(i == nt - 1)
        def _():
            for acc, s_ref in zip(accs, sum_refs):
                s_ref[...] = jnp.sum(acc[...], axis=0, keepdims=True)

    return pl.pallas_call(
        body, name=name, grid=(nt,),
        out_shape=tuple(outs) + tuple(jax.ShapeDtypeStruct((1, w), F32) for w in widths),
        in_specs=in_specs,
        out_specs=tuple(out_specs) + tuple(_const((1, w)) for w in widths),
        scratch_shapes=[pltpu.VMEM((SUBLANES, w), F32) for w in widths],
        compiler_params=_cparams(),
    )(*ins)


def _post_norm_bwd(dxo, y, g, gate, name):
    S, D = y.shape
    T = _row_tile(S)

    def fn(ins, outs):
        dxo_ref, y_ref, g_ref, gt_ref = ins
        yv, dv = y_ref[...], dxo_ref[...]
        rstd = lax.rsqrt(jnp.mean(yv * yv, axis=-1, keepdims=True) + EPS)
        yh = yv * rstd
        dn = dv * gt_ref[...]
        dyh = dn * g_ref[...]
        outs[0][...] = (rstd * (dyh - yh * jnp.mean(dyh * yh, axis=-1, keepdims=True))).astype(BF16)
        return [dv * (yh * g_ref[...]), dn * yh]

    return _col_sums(2, fn, [dxo, y, g, gate],
                     [_rows(T, D), _rows(T, D), _const((1, D)), _const((1, D))],
                     [jax.ShapeDtypeStruct((S, D), BF16)], [_rows(T, D)], S, T, [D, D], name)


def _pre_norm_bwd(dh, x, dxo, g, scale, name, after=None):
    S, D = x.shape
    T = _row_tile(S)
    row_ops, row_specs = _row_operands([g, scale], after)

    def fn(ins, outs):
        dh_ref, x_ref, dxo_ref, g_ref, sc_ref = ins[:5]
        xv, dv = x_ref[...], dh_ref[...]
        rstd = lax.rsqrt(jnp.mean(xv * xv, axis=-1, keepdims=True) + EPS)
        xh = xv * rstd
        dr = dv * (1.0 + sc_ref[...])
        dxh = dr * g_ref[...]
        outs[0][...] = dxo_ref[...] + rstd * (dxh - xh * jnp.mean(dxh * xh, axis=-1, keepdims=True))
        return [dv, dv * (xh * g_ref[...]), dr * xh]

    return _col_sums(3, fn, [dh, x, dxo] + row_ops,
                     [_rows(T, D), _rows(T, D), _rows(T, D)] + row_specs,
                     [jax.ShapeDtypeStruct((S, D), F32)], [_rows(T, D)], S, T, [D, D, D], name)


def _post_pre_norm(x, y, g_post, gate, g_pre, scale, shift, name):
    S, D = x.shape
    T = _row_tile(S)
    row_ops, row_specs = _row_operands([g_post, gate, g_pre, scale, shift])

    def body(x_ref, y_ref, gp_ref, gt_ref, g_ref, sc_ref, sh_ref, xn_ref, h_ref):
        yv = y_ref[...]
        rstd_y = lax.rsqrt(jnp.mean(yv * yv, axis=-1, keepdims=True) + EPS)
        xn = x_ref[...] + gt_ref[...] * ((yv * rstd_y) * gp_ref[...])
        xn_ref[...] = xn
        rstd = lax.rsqrt(jnp.mean(xn * xn, axis=-1, keepdims=True) + EPS)
        h_ref[...] = ((xn * rstd) * g_ref[...] * (1.0 + sc_ref[...]) + sh_ref[...]).astype(BF16)

    return pl.pallas_call(
        body, name=name, grid=(S // T,),
        out_shape=(jax.ShapeDtypeStruct((S, D), F32), jax.ShapeDtypeStruct((S, D), BF16)),
        in_specs=[_rows(T, D), _rows(T, D)] + row_specs,
        out_specs=(_rows(T, D), _rows(T, D)), compiler_params=_cparams(),
    )(x, y, *row_ops)


def _pre_post_norm_bwd(dh, x, dxo, g_pre, scale, y_prev, g_post_prev, gate_prev, name, after=None):
    S, D = x.shape
    T = _row_tile(S)
    row_ops, row_specs = _row_operands([g_pre, scale, g_post_prev, gate_prev], after)

    def fn(ins, outs):
        dh_ref, x_ref, dxo_ref, y_ref, g_ref, sc_ref, gp_ref, gt_ref = ins[:8]
        xv, dv = x_ref[...], dh_ref[...]
        rstd = lax.rsqrt(jnp.mean(xv * xv, axis=-1, keepdims=True) + EPS)
        xh = xv * rstd
        dr = dv * (1.0 + sc_ref[...])
        dxh = dr * g_ref[...]
        dx = dxo_ref[...] + rstd * (dxh - xh * jnp.mean(dxh * xh, axis=-1, keepdims=True))
        outs[0][...] = dx
        yv = y_ref[...]
        rstd_y = lax.rsqrt(jnp.mean(yv * yv, axis=-1, keepdims=True) + EPS)
        yh = yv * rstd_y
        dn = dx * gt_ref[...]
        dyh = dn * gp_ref[...]
        outs[1][...] = (rstd_y * (dyh - yh * jnp.mean(dyh * yh, axis=-1, keepdims=True))).astype(BF16)
        return [dv, dv * (xh * g_ref[...]), dr * xh, dx * (yh * gp_ref[...]), dn * yh]

    return _col_sums(5, fn, [dh, x, dxo, y_prev] + row_ops,
                     [_rows(T, D), _rows(T, D), _rows(T, D), _rows(T, D)] + row_specs,
                     [jax.ShapeDtypeStruct((S, D), F32), jax.ShapeDtypeStruct((S, D), BF16)],
                     [_rows(T, D), _rows(T, D)], S, T, [D] * 5, name)


def _loss_post_norm_bwd(x, y, g_post, gate, target, name):
    S, D = x.shape
    T = _row_tile(S)
    row_ops, row_specs = _row_operands([g_post, gate])

    def fn(ins, outs):
        x_ref, y_ref, t_ref, gp_ref, gt_ref = ins
        yv = y_ref[...]
        rstd_y = lax.rsqrt(jnp.mean(yv * yv, axis=-1, keepdims=True) + EPS)
        yh = yv * rstd_y
        e = x_ref[...] + gt_ref[...] * (yh * gp_ref[...]) - t_ref[...]
        dx = e * (1.0 / D)
        outs[0][...] = dx
        dn = dx * gt_ref[...]
        dyh = dn * gp_ref[...]
        outs[1][...] = (rstd_y * (dyh - yh * jnp.mean(dyh * yh, axis=-1, keepdims=True))).astype(BF16)
        return [e * e, dx * (yh * gp_ref[...]), dn * yh]

    return _col_sums(3, fn, [x, y, target] + row_ops,
                     [_rows(T, D), _rows(T, D), _rows(T, D)] + row_specs,
                     [jax.ShapeDtypeStruct((S, D), F32), jax.ShapeDtypeStruct((S, D), BF16)],
                     [_rows(T, D), _rows(T, D)], S, T, [D] * 3, name)


def _scaled_total(v, coef, name):
    def body(v_ref, o_ref):
        o_ref[...] = jnp.broadcast_to(jnp.sum(v_ref[...], axis=1, keepdims=True) * coef, (1, LANES))

    return pl.pallas_call(body, name=name, out_shape=jax.ShapeDtypeStruct((1, LANES), F32))(v)


def _loss_head(x, target):
    S, D = x.shape
    T = _row_tile(S)
    nt = S // T

    def body(x_ref, t_ref, l_ref, dx_ref, acc):
        i = pl.program_id(0)
        e = x_ref[...] - t_ref[...]
        dx_ref[...] = e * (1.0 / D)
        part = _fold8(e * e)

        @pl.when(i == 0)
        def _():
            acc[...] = part

        @pl.when(i > 0)
        def _():
            acc[...] += part

        @pl.when(i == nt - 1)
        def _():
            tot = jnp.sum(jnp.sum(acc[...], axis=0, keepdims=True), axis=1, keepdims=True)
            l_ref[...] = jnp.broadcast_to(tot * (0.5 / D), (1, LANES))

    return pl.pallas_call(
        body, name="loss_head", grid=(nt,),
        out_shape=(jax.ShapeDtypeStruct((1, LANES), F32), jax.ShapeDtypeStruct((S, D), F32)),
        in_specs=[_rows(T, D), _rows(T, D)],
        out_specs=(_const((1, LANES)), _rows(T, D)),
        scratch_shapes=[pltpu.VMEM((SUBLANES, D), F32)],
        compiler_params=_cparams(),
    )(x, target)


CONV_ROWS = 64


def _conv_halo(K):
    return SUBLANES if K - 1 <= SUBLANES else 32


def _conv_fwd(u, w, b, K, name):
    S, C = u.shape
    KP = w.shape[0]
    T, HB, RS = min(512, S), _conv_halo(K), CONV_ROWS
    ratio = T // HB

    def body(u_ref, h_ref, w_ref, b_ref, o_ref, ext):
        i = pl.program_id(1)
        ext[0:HB, :] = jnp.where(i > 0, h_ref[...], 0.0)
        ext[HB:HB + T, :] = u_ref[...]
        for r0 in range(0, T, RS):
            acc = jnp.broadcast_to(b_ref[...], (RS, LANES))
            for k in range(K):
                off = HB - (K - 1) + k + r0
                acc = acc + w_ref[k:k + 1, :] * ext[off:off + RS, :]
            o_ref[r0:r0 + RS, :] = acc

    return pl.pallas_call(
        body, name=name, grid=(C // LANES, S // T),
        out_shape=jax.ShapeDtypeStruct((S, C), F32),
        in_specs=[pl.BlockSpec((T, LANES), lambda c, i: (i, c)),
                  pl.BlockSpec((HB, LANES), lambda c, i: (jnp.maximum(i * ratio - 1, 0), c)),
                  pl.BlockSpec((KP, LANES), lambda c, i: (0, c)),
                  pl.BlockSpec((1, LANES), lambda c, i: (0, c))],
        out_specs=pl.BlockSpec((T, LANES), lambda c, i: (i, c)),
        scratch_shapes=[pltpu.VMEM((HB + T, LANES), F32)],
        compiler_params=_cparams(),
    )(u, u, w, b)


def _conv_bwd(d, u, w, K, name):
    S, C = u.shape
    KP = w.shape[0]
    T, HB, RS = min(512, S), _conv_halo(K), CONV_ROWS
    ratio = T // HB
    nt = S // T
    last_halo = S // HB - 1

    def body(d_ref, dn_ref, u_ref, up_ref, w_ref, du_ref, dw_ref, db_ref, extd, extu, dws, dbs):
        i = pl.program_id(1)
        extd[0:T, :] = d_ref[...]
        extd[T:T + HB, :] = jnp.where(i < nt - 1, dn_ref[...], 0.0)
        extu[0:HB, :] = jnp.where(i > 0, up_ref[...], 0.0)
        extu[HB:HB + T, :] = u_ref[...]

        @pl.when(i == 0)
        def _():
            dws[...] = jnp.zeros_like(dws)
            dbs[...] = jnp.zeros_like(dbs)

        for r0 in range(0, T, RS):
            acc = jnp.zeros((RS, LANES), F32)
            for k in range(K):
                off = (K - 1 - k) + r0
                acc = acc + w_ref[k:k + 1, :] * extd[off:off + RS, :]
            du_ref[r0:r0 + RS, :] = acc
            dch = d_ref[r0:r0 + RS, :]
            dbs[...] += _fold8(dch)
            for k in range(K):
                off = HB - (K - 1) + k + r0
                dws[k * SUBLANES:(k + 1) * SUBLANES, :] += _fold8(dch * extu[off:off + RS, :])

        @pl.when(i == nt - 1)
        def _():
            dw_ref[...] = jnp.zeros_like(dw_ref)
            for k in range(K):
                dw_ref[k:k + 1, :] = jnp.sum(dws[k * SUBLANES:(k + 1) * SUBLANES, :], axis=0, keepdims=True)
            db_ref[...] = jnp.sum(dbs[...], axis=0, keepdims=True)

    return pl.pallas_call(
        body, name=name, grid=(C // LANES, nt),
        out_shape=(jax.ShapeDtypeStruct((S, C), F32), jax.ShapeDtypeStruct((KP, C), F32),
                   jax.ShapeDtypeStruct((1, C), F32)),
        in_specs=[pl.BlockSpec((T, LANES), lambda c, i: (i, c)),
                  pl.BlockSpec((HB, LANES), lambda c, i: (jnp.minimum((i + 1) * ratio, last_halo), c)),
                  pl.BlockSpec((T, LANES), lambda c, i: (i, c)),
                  pl.BlockSpec((HB, LANES), lambda c, i: (jnp.maximum(i * ratio - 1, 0), c)),
                  pl.BlockSpec((KP, LANES), lambda c, i: (0, c))],
        out_specs=(pl.BlockSpec((T, LANES), lambda c, i: (i, c)),
                   pl.BlockSpec((KP, LANES), lambda c, i: (0, c)),
                   pl.BlockSpec((1, LANES), lambda c, i: (0, c))),
        scratch_shapes=[pltpu.VMEM((T + HB, LANES), F32), pltpu.VMEM((HB + T, LANES), F32),
                        pltpu.VMEM((KP * SUBLANES, LANES), F32), pltpu.VMEM((SUBLANES, LANES), F32)],
        compiler_params=_cparams(),
    )(d, d, u, u, w)


SCW = 512
ZE = 3072
QL = 256
KVL = 128


def _rms_rows(x, g):
    rstd = lax.rsqrt(jnp.mean(x * x, axis=-1, keepdims=True) + EPS)
    return (x * rstd) * g


def _even_pre(z, qg, kvg, name):
    S = z.shape[0]
    T = _row_tile(S)

    def body(ac_ref, ax_ref, cq_ref, ckv_ref, qg_ref, kvg_ref, u_ref, qn_ref, kvn_ref):
        u_ref[...] = _f32(ac_ref) * _f32(ax_ref)
        qn_ref[...] = _rms_rows(_f32(cq_ref), qg_ref[...]).astype(BF16)
        kvn_ref[...] = _rms_rows(_f32(ckv_ref), kvg_ref[...]).astype(BF16)

    return pl.pallas_call(
        body, name=name, grid=(S // T,),
        out_shape=(jax.ShapeDtypeStruct((S, SCW), F32), jax.ShapeDtypeStruct((S, QL), BF16),
                   jax.ShapeDtypeStruct((S, KVL), BF16)),
        in_specs=[_rows(T, SCW, 1), _rows(T, SCW, 2), _rows(T, QL, 10), _rows(T, KVL, 22),
                  _const((1, QL)), _const((1, KVL))],
        out_specs=(_rows(T, SCW), _rows(T, QL), _rows(T, KVL)),
        compiler_params=_cparams(),
    )(z, z, z, z, qg, kvg)


def _qkv_fwd(qn, kvn, z, tabs, w_q, w_kv, name):
    S = qn.shape[0]
    T = _row_tile(S)
    HW = HEADS * HEAD_PAD
    scale = 1.0 / math.sqrt(QK_NOPE + QK_ROPE)

    def body(qn_ref, kvn_ref, kr_ref, ct_ref, ut_ref, dt_ref, wq_ref, wkv_ref, q_ref, k_ref, v_ref):
        ct, ut, dt = ct_ref[...], ut_ref[...], dt_ref[...]
        qa = jnp.dot(qn_ref[...], wq_ref[...], preferred_element_type=F32)
        kva = jnp.dot(kvn_ref[...], wkv_ref[...], preferred_element_type=F32)
        kr = _f32(kr_ref)
        for h in range(HEADS):
            sl = slice(h * HEAD_PAD, (h + 1) * HEAD_PAD)
            q_ref[:, sl] = (_rope(qa[:, sl], ct, ut, dt) * scale).astype(BF16)
            k_ref[:, sl] = _rope(kva[:, sl] + kr, ct, ut, dt).astype(BF16)
        v_ref[...] = kva[:, HW:].astype(BF16)

    return pl.pallas_call(
        body, name=name, grid=(S // T,),
        out_shape=(jax.ShapeDtypeStruct((S, HW), BF16), jax.ShapeDtypeStruct((S, HW), BF16),
                   jax.ShapeDtypeStruct((S, HEADS * V_HEAD), BF16)),
        in_specs=[_rows(T, QL), _rows(T, KVL), _rows(T, HEAD_PAD, 23),
                  _rows(T, HEAD_PAD), _rows(T, HEAD_PAD), _rows(T, HEAD_PAD),
                  _const(w_q.shape), _const(w_kv.shape)],
        out_specs=(_rows(T, HW), _rows(T, HW), _rows(T, HEADS * V_HEAD)),
        compiler_params=_cparams(),
    )(qn, kvn, z, *tabs, w_q, w_kv)


def _attn_tile(S):
    return min(256, S)


def _chunk_mask(TQ):
    r = lax.broadcasted_iota(jnp.int32, (TQ, TQ), 0) // CHUNK
    c = lax.broadcasted_iota(jnp.int32, (TQ, TQ), 1) // CHUNK
    return c <= r


_NT = (((1,), (1,)), ((), ()))
_TN = (((0,), (0,)), ((), ()))


def _attn_fwd(q, k, v, name):
    S = q.shape[0]
    TQ = _attn_tile(S)
    nq = S // TQ
    PW = 2 * HEAD_PAD

    def body(q_ref, k_ref, v_ref, o_ref, lse_ref, m_s, l_s, acc_s):
        i = pl.program_id(1)
        left = lax.broadcasted_iota(jnp.int32, (TQ, LANES), 1) < V_HEAD
        m_s[...] = jnp.full_like(m_s, NEG)
        l_s[...] = jnp.zeros_like(l_s)
        acc_s[...] = jnp.zeros_like(acc_s)
        qv = q_ref[...]

        def step(j, masked):
            r0 = pl.multiple_of(j * TQ, TQ)
            kb = k_ref[pl.ds(r0, TQ), :]
            vb = v_ref[pl.ds(r0, TQ), :]
            alphas, pvs = [], []
            for h in range(2):
                sl = slice(h * HEAD_PAD, (h + 1) * HEAD_PAD)
                s = lax.dot_general(qv[:, sl], kb[:, sl], _NT, preferred_element_type=F32)
                if masked:
                    s = jnp.where(_chunk_mask(TQ), s, NEG)
                m_prev = m_s[h]
                m_new = jnp.maximum(m_prev, jnp.max(s, axis=1, keepdims=True))
                alpha = jnp.exp(m_prev - m_new)
                p = jnp.exp(s - m_new[:, 0:1])
                l_s[h] = alpha * l_s[h] + jnp.sum(p, axis=1, keepdims=True)
                m_s[h] = m_new
                alphas.append(alpha)
                pvs.append(jnp.dot(p.astype(BF16), vb, preferred_element_type=F32))
            acc_s[...] = acc_s[...] * jnp.where(left, alphas[0], alphas[1]) + jnp.where(left, pvs[0], pvs[1])

        def loop_body(j, carry):
            step(j, False)
            return carry

        lax.fori_loop(0, i, loop_body, 0)
        step(i, True)
        o_ref[...] = acc_s[...] / jnp.where(left, l_s[0], l_s[1])
        lse_ref[...] = jnp.where(left, m_s[0] + jnp.log(l_s[0]), m_s[1] + jnp.log(l_s[1]))

    return pl.pallas_call(
        body, name=name, grid=(HEADS // 2, nq),
        out_shape=(jax.ShapeDtypeStruct((S, HEADS * V_HEAD), F32), jax.ShapeDtypeStruct((S, HEADS * V_HEAD), F32)),
        in_specs=[pl.BlockSpec((TQ, PW), lambda p, i: (i, p)),
                  pl.BlockSpec((S, PW), lambda p, i: (0, p)),
                  pl.BlockSpec((S, LANES), lambda p, i: (0, p))],
        out_specs=(pl.BlockSpec((TQ, LANES), lambda p, i: (i, p)),
                   pl.BlockSpec((TQ, LANES), lambda p, i: (i, p))),
        scratch_shapes=[pltpu.VMEM((2, TQ, LANES), F32), pltpu.VMEM((2, TQ, LANES), F32),
                        pltpu.VMEM((TQ, LANES), F32)],
        compiler_params=_cparams(),
    )(q, k, v)


def _attn_dq(q, k, v, do, lse, delta, name):
    S = q.shape[0]
    TQ = _attn_tile(S)
    nq = S // TQ
    PW = 2 * HEAD_PAD

    def body(q_ref, k_ref, v_ref, do_ref, lse_ref, dl_ref, dq_ref, acc_s):
        i = pl.program_id(1)
        left = lax.broadcasted_iota(jnp.int32, (TQ, LANES), 1) < V_HEAD
        acc_s[...] = jnp.zeros_like(acc_s)
        qv = q_ref[...]
        dov = do_ref[...]
        dos = [jnp.where(left, dov, jnp.zeros_like(dov)), jnp.where(left, jnp.zeros_like(dov), dov)]
        lses = [lse_ref[:, 0:1], lse_ref[:, V_HEAD:V_HEAD + 1]]
        dls = [dl_ref[:, 0:1], dl_ref[:, V_HEAD:V_HEAD + 1]]

        def step(j, masked):
            r0 = pl.multiple_of(j * TQ, TQ)
            kb = k_ref[pl.ds(r0, TQ), :]
            vb = v_ref[pl.ds(r0, TQ), :]
            for h in range(2):
                sl = slice(h * HEAD_PAD, (h + 1) * HEAD_PAD)
                s = lax.dot_general(qv[:, sl], kb[:, sl], _NT, preferred_element_type=F32)
                p = jnp.exp(s - lses[h])
                if masked:
                    p = jnp.where(_chunk_mask(TQ), p, 0.0)
                dp = lax.dot_general(dos[h], vb, _NT, preferred_element_type=F32)
                ds = (p * (dp - dls[h])).astype(BF16)
                acc_s[:, sl] += jnp.dot(ds, kb[:, sl], preferred_element_type=F32)

        def loop_body(j, carry):
            step(j, False)
            return carry

        lax.fori_loop(0, i, loop_body, 0)
        step(i, True)
        dq_ref[...] = acc_s[...]

    return pl.pallas_call(
        body, name=name, grid=(HEADS // 2, nq),
        out_shape=jax.ShapeDtypeStruct((S, HEADS * HEAD_PAD), F32),
        in_specs=[pl.BlockSpec((TQ, PW), lambda p, i: (i, p)),
                  pl.BlockSpec((S, PW), lambda p, i: (0, p)),
                  pl.BlockSpec((S, LANES), lambda p, i: (0, p)),
                  pl.BlockSpec((TQ, LANES), lambda p, i: (i, p)),
                  pl.BlockSpec((TQ, LANES), lambda p, i: (i, p)),
                  pl.BlockSpec((TQ, LANES), lambda p, i: (i, p))],
        out_specs=pl.BlockSpec((TQ, PW), lambda p, i: (i, p)),
        scratch_shapes=[pltpu.VMEM((TQ, PW), F32)],
        compiler_params=_cparams(),
    )(q, k, v, do, lse, delta)


def _attn_dkv(q, k, v, do, lse, delta, name):
    S = q.shape[0]
    TQ = _attn_tile(S)
    nq = S // TQ
    PW = 2 * HEAD_PAD

    def body(q_ref, k_ref, v_ref, do_ref, lse_ref, dl_ref, dk_ref, dv_ref, dk_s, dv_s):
        j = pl.program_id(1)
        left = lax.broadcasted_iota(jnp.int32, (TQ, LANES), 1) < V_HEAD
        dk_s[...] = jnp.zeros_like(dk_s)
        dv_s[...] = jnp.zeros_like(dv_s)
        kb = k_ref[...]
        vb = v_ref[...]

        def step(i, masked):
            r0 = pl.multiple_of(i * TQ, TQ)
            qb = q_ref[pl.ds(r0, TQ), :]
            dov = do_ref[pl.ds(r0, TQ), :]
            lse = lse_ref[pl.ds(r0, TQ), :]
            dl = dl_ref[pl.ds(r0, TQ), :]
            dos = [jnp.where(left, dov, jnp.zeros_like(dov)), jnp.where(left, jnp.zeros_like(dov), dov)]
            for h in range(2):
                sl = slice(h * HEAD_PAD, (h + 1) * HEAD_PAD)
                c0 = h * V_HEAD
                s = lax.dot_general(qb[:, sl], kb[:, sl], _NT, preferred_element_type=F32)
                p = jnp.exp(s - lse[:, c0:c0 + 1])
                if masked:
                    p = jnp.where(_chunk_mask(TQ), p, 0.0)
                dv_s[...] += lax.dot_general(p.astype(BF16), dos[h], _TN, preferred_element_type=F32)
                dp = lax.dot_general(dos[h], vb, _NT, preferred_element_type=F32)
                ds = (p * (dp - dl[:, c0:c0 + 1])).astype(BF16)
                dk_s[:, sl] += lax.dot_general(ds, qb[:, sl], _TN, preferred_element_type=F32)

        def loop_body(i, carry):
            step(i, False)
            return carry

        step(j, True)
        lax.fori_loop(j + 1, nq, loop_body, 0)
        dk_ref[...] = dk_s[...]
        dv_ref[...] = dv_s[...]

    return pl.pallas_call(
        body, name=name, grid=(HEADS // 2, nq),
        out_shape=(jax.ShapeDtypeStruct((S, HEADS * HEAD_PAD), F32), jax.ShapeDtypeStruct((S, HEADS * V_HEAD), F32)),
        in_specs=[pl.BlockSpec((S, PW), lambda p, j: (0, p)),
                  pl.BlockSpec((TQ, PW), lambda p, j: (j, p)),
                  pl.BlockSpec((TQ, LANES), lambda p, j: (j, p)),
                  pl.BlockSpec((S, LANES), lambda p, j: (0, p)),
                  pl.BlockSpec((S, LANES), lambda p, j: (0, p)),
                  pl.BlockSpec((S, LANES), lambda p, j: (0, p))],
        out_specs=(pl.BlockSpec((TQ, PW), lambda p, j: (j, p)),
                   pl.BlockSpec((TQ, LANES), lambda p, j: (j, p))),
        scratch_shapes=[pltpu.VMEM((TQ, PW), F32), pltpu.VMEM((TQ, LANES), F32)],
        compiler_params=_cparams(),
    )(q, k, v, do, lse, delta)


LOG2E = math.log2(math.e)
ATTN_FWD_HEADS = 8
ATTN_BWD_HEADS = 4


def _chunk_mask_t(T):
    key = lax.broadcasted_iota(jnp.int32, (T, T), 0) // CHUNK
    qry = lax.broadcasted_iota(jnp.int32, (T, T), 1) // CHUNK
    return key <= qry


W_IN_ROWS = 512


def _w_in_even(h, w_t, qg, kvg, name):
    S, D = h.shape
    tm = min(W_IN_ROWS, S)

    def body(h_ref, w_ref, qg_ref, kvg_ref, z_ref, u_ref, qn_ref, kvn_ref):
        p = lax.dot_general(h_ref[...], w_ref[...], _NT, preferred_element_type=F32)
        z_ref[...] = p.astype(BF16)
        u_ref[...] = p[:, SCW:2 * SCW] * p[:, 2 * SCW:3 * SCW]
        qn_ref[...] = _rms_rows(p[:, 5 * SCW:5 * SCW + QL], qg_ref[...]).astype(BF16)
        kvn_ref[...] = _rms_rows(p[:, 5 * SCW + QL:5 * SCW + QL + KVL], kvg_ref[...]).astype(BF16)

    return pl.pallas_call(
        body, name=name, grid=(S // tm,),
        out_shape=(jax.ShapeDtypeStruct((S, ZE), BF16), jax.ShapeDtypeStruct((S, SCW), F32),
                   jax.ShapeDtypeStruct((S, QL), BF16), jax.ShapeDtypeStruct((S, KVL), BF16)),
        in_specs=[_rows(tm, D), _const(w_t.shape), _const((1, QL)), _const((1, KVL))],
        out_specs=(_rows(tm, ZE), _rows(tm, SCW), _rows(tm, QL), _rows(tm, KVL)),
        compiler_params=_cparams(),
    )(h, w_t, qg, kvg)


def _w_in_odd(h, w, name):
    S, D = h.shape
    tm = min(W_IN_ROWS, S)

    def body(h_ref, w_ref, z_ref, u_ref):
        p = jnp.dot(h_ref[...], w_ref[...], preferred_element_type=F32)
        z_ref[...] = p.astype(BF16)
        u_ref[...] = p[:, 0:D] * _sigmoid(p[:, D:2 * D])

    return pl.pallas_call(
        body, name=name, grid=(S // tm,),
        out_shape=(jax.ShapeDtypeStruct((S, 3 * D), BF16), jax.ShapeDtypeStruct((S, D), F32)),
        in_specs=[_rows(tm, D), _const(w.shape)],
        out_specs=(_rows(tm, 3 * D), _rows(tm, D)),
        compiler_params=_cparams(),
    )(h, w)


def _qkv_fwd_t(qn, kvn, z, tabs, w_q, w_kv, name):
    S = qn.shape[0]
    T = _attn_tile(S)
    HW = HEADS * HEAD_PAD
    scale = LOG2E / math.sqrt(QK_NOPE + QK_ROPE)

    def body(qn_ref, kvn_ref, kr_ref, ct_ref, ut_ref, dt_ref, wq_ref, wkv_ref, q_ref, k_ref, v_ref, kt_ref, vt_ref):
        ct, ut, dt = ct_ref[...], ut_ref[...], dt_ref[...]
        qa = jnp.dot(qn_ref[...], wq_ref[...], preferred_element_type=F32)
        kva = jnp.dot(kvn_ref[...], wkv_ref[...], preferred_element_type=F32)
        kr = _f32(kr_ref)
        ones_row = (lax.broadcasted_iota(jnp.int32, (V_HEAD, T), 0) == 0).astype(F32)
        for h in range(HEADS):
            sl = slice(h * HEAD_PAD, (h + 1) * HEAD_PAD)
            q_ref[:, sl] = (_rope(qa[:, sl], ct, ut, dt) * scale).astype(BF16)
            kh = _rope(kva[:, sl] + kr, ct, ut, dt)
            k_ref[:, sl] = kh.astype(BF16)
            kt_ref[0, sl, :] = kh.T.astype(BF16)
        v_ref[...] = kva[:, HW:].astype(BF16)
        for p in range(HEADS // 2):
            vpt = kva[:, HW + p * LANES:HW + (p + 1) * LANES].T
            for h in range(2):
                r0 = (2 * p + h) * HEAD_PAD
                vt_ref[0, r0:r0 + V_HEAD, :] = vpt[h * V_HEAD:(h + 1) * V_HEAD, :].astype(BF16)
                vt_ref[0, r0 + V_HEAD:r0 + HEAD_PAD, :] = ones_row.astype(BF16)

    t3 = jax.ShapeDtypeStruct((S // T, HW, T), BF16)
    return pl.pallas_call(
        body, name=name, grid=(S // T,),
        out_shape=(jax.ShapeDtypeStruct((S, HW), BF16), jax.ShapeDtypeStruct((S, HW), BF16),
                   jax.ShapeDtypeStruct((S, HEADS * V_HEAD), BF16), t3, t3),
        in_specs=[_rows(T, QL), _rows(T, KVL), _rows(T, HEAD_PAD, 23),
                  _rows(T, HEAD_PAD), _rows(T, HEAD_PAD), _rows(T, HEAD_PAD),
                  _const(w_q.shape), _const(w_kv.shape)],
        out_specs=(_rows(T, HW), _rows(T, HW), _rows(T, HEADS * V_HEAD),
                   pl.BlockSpec((1, HW, T), lambda i: (i, 0, 0)), pl.BlockSpec((1, HW, T), lambda i: (i, 0, 0))),
        compiler_params=_cparams(),
    )(qn, kvn, z, *tabs, w_q, w_kv)


def _attn_fwd_t(q, k, vT3, name):
    S = q.shape[0]
    T = _attn_tile(S)
    nq = S // T
    NH = ATTN_FWD_HEADS
    PW = NH * HEAD_PAD

    def body(q_ref, k_ref, vt_ref, o_ref, lse_ref, m_s, acc_s):
        i = pl.program_id(1)
        m_s[...] = jnp.full_like(m_s, NEG)
        acc_s[...] = jnp.zeros_like(acc_s)
        qv = q_ref[...]

        def step(j, masked):
            kb = k_ref[pl.ds(pl.multiple_of(j * T, T), T), :]
            vt = vt_ref[j]
            heads = [slice(h * HEAD_PAD, (h + 1) * HEAD_PAD) for h in range(NH)]
            sts = [lax.dot_general(kb[:, sl], qv[:, sl], _NT, preferred_element_type=F32) for sl in heads]
            alphas, pvs = [], []
            for h, sl in enumerate(heads):
                st = jnp.where(_chunk_mask_t(T), sts[h], NEG) if masked else sts[h]
                m_prev = m_s[h]
                m_new = jnp.maximum(m_prev, jnp.max(st, axis=0, keepdims=True))
                alphas.append(jnp.exp2(m_prev[0:1] - m_new[0:1]))
                pt = jnp.exp2(st - m_new[0:1]).astype(BF16)
                m_s[h] = m_new
                pvs.append(jnp.dot(vt[sl, :], pt, preferred_element_type=F32))
            for h in range(NH):
                acc_s[h] = acc_s[h] * alphas[h] + pvs[h]

        def loop_body(j, carry):
            step(j, False)
            return carry

        lax.fori_loop(0, i, loop_body, 0)
        step(i, True)
        for g in range(NH // 2):
            outs = []
            for h in (2 * g, 2 * g + 1):
                acc = acc_s[h]
                l_row = acc[V_HEAD:V_HEAD + 1, :]
                outs.append(acc[0:V_HEAD, :] / l_row)
                lse_ref[0, h * SUBLANES:(h + 1) * SUBLANES, :] = m_s[h] + jnp.log2(l_row)
            o_ref[:, g * LANES:(g + 1) * LANES] = jnp.concatenate(outs, axis=0).T

    return pl.pallas_call(
        body, name=name, grid=(HEADS // NH, nq),
        out_shape=(jax.ShapeDtypeStruct((S, HEADS * V_HEAD), F32),
                   jax.ShapeDtypeStruct((nq, HEADS * SUBLANES, T), F32)),
        in_specs=[pl.BlockSpec((T, PW), lambda p, i: (i, p)),
                  pl.BlockSpec((S, PW), lambda p, i: (0, p)),
                  pl.BlockSpec((nq, PW, T), lambda p, i: (0, p, 0))],
        out_specs=(pl.BlockSpec((T, NH * V_HEAD), lambda p, i: (i, p)),
                   pl.BlockSpec((1, NH * SUBLANES, T), lambda p, i: (i, p, 0))),
        scratch_shapes=[pltpu.VMEM((NH, SUBLANES, T), F32), pltpu.VMEM((NH, HEAD_PAD, T), F32)],
        compiler_params=_cparams(),
    )(q, k, vT3)


def _attn_bwd_t(q, k, v, kT3, do, lse3, dl3, name):
    S = q.shape[0]
    T = _attn_tile(S)
    nq = S // T
    NH = ATTN_BWD_HEADS
    PW = NH * HEAD_PAD
    VW = NH * V_HEAD

    def body(q_ref, k_ref, v_ref, kt_ref, do_ref, lse_ref, dl_ref, dq_ref, dk_ref, dv_ref, dk_s, dv_s):
        j = pl.program_id(1)
        left = lax.broadcasted_iota(jnp.int32, (T, LANES), 1) < V_HEAD

        @pl.when(j == 0)
        def _():
            dq_ref[...] = jnp.zeros_like(dq_ref)

        dk_s[...] = jnp.zeros_like(dk_s)
        dv_s[...] = jnp.zeros_like(dv_s)
        kb = k_ref[...]
        vms = []
        for g in range(NH // 2):
            vb = v_ref[:, g * LANES:(g + 1) * LANES]
            vms += [jnp.where(left, vb, jnp.zeros_like(vb)), jnp.where(left, jnp.zeros_like(vb), vb)]
        kt = kt_ref[0]

        def step(i, masked):
            r0 = pl.multiple_of(i * T, T)
            qb = q_ref[pl.ds(r0, T), :]
            do_all = do_ref[pl.ds(r0, T), :]
            lse = lse_ref[i]
            dl = dl_ref[i]
            heads = [slice(h * HEAD_PAD, (h + 1) * HEAD_PAD) for h in range(NH)]
            dobs = [do_all[:, (h // 2) * LANES:(h // 2 + 1) * LANES] for h in range(NH)]
            sts = [lax.dot_general(kb[:, sl], qb[:, sl], _NT, preferred_element_type=F32) for sl in heads]
            dpts = [lax.dot_general(vms[h], dobs[h], _NT, preferred_element_type=F32) for h in range(NH)]
            res = []
            for h, sl in enumerate(heads):
                r8 = h * SUBLANES
                pt = jnp.exp2(sts[h] - lse[r8:r8 + 1, :])
                if masked:
                    pt = jnp.where(_chunk_mask_t(T), pt, 0.0)
                dst = (pt * (dpts[h] - dl[r8:r8 + 1, :])).astype(BF16)
                res.append((jnp.dot(pt.astype(BF16), dobs[h], preferred_element_type=F32),
                            jnp.dot(dst, qb[:, sl], preferred_element_type=F32),
                            jnp.dot(kt[sl, :], dst, preferred_element_type=F32)))
            for h, sl in enumerate(heads):
                dv_s[h] += res[h][0]
                dk_s[:, sl] += res[h][1]
                dq_ref[i, sl, :] += res[h][2]

        def loop_body(i, carry):
            step(i, False)
            return carry

        step(j, True)
        lax.fori_loop(j + 1, nq, loop_body, 0)
        dk_ref[...] = dk_s[...] * (1.0 / LOG2E)
        for g in range(NH // 2):
            dv_ref[:, g * LANES:(g + 1) * LANES] = jnp.where(left, dv_s[2 * g], dv_s[2 * g + 1])

    return pl.pallas_call(
        body, name=name, grid=(HEADS // NH, nq),
        out_shape=(jax.ShapeDtypeStruct((nq, HEADS * HEAD_PAD, T), F32),
                   jax.ShapeDtypeStruct((S, HEADS * HEAD_PAD), F32), jax.ShapeDtypeStruct((S, HEADS * V_HEAD), F32)),
        in_specs=[pl.BlockSpec((S, PW), lambda p, j: (0, p)),
                  pl.BlockSpec((T, PW), lambda p, j: (j, p)),
                  pl.BlockSpec((T, VW), lambda p, j: (j, p)),
                  pl.BlockSpec((1, PW, T), lambda p, j: (j, p, 0)),
                  pl.BlockSpec((S, VW), lambda p, j: (0, p)),
                  pl.BlockSpec((nq, NH * SUBLANES, T), lambda p, j: (0, p, 0)),
                  pl.BlockSpec((nq, NH * SUBLANES, T), lambda p, j: (0, p, 0))],
        out_specs=(pl.BlockSpec((nq, PW, T), lambda p, j: (0, p, 0)),
                   pl.BlockSpec((T, PW), lambda p, j: (j, p)),
                   pl.BlockSpec((T, VW), lambda p, j: (j, p))),
        scratch_shapes=[pltpu.VMEM((T, PW), F32), pltpu.VMEM((NH, T, LANES), F32)],
        compiler_params=_cparams(),
    )(q, k, v, kT3, do, lse3, dl3)


def _even_post(z, cv, o, name):
    S = z.shape[0]
    T = _row_tile(S)

    def body(ab_ref, ag_ref, bg_ref, cv_ref, o_ref, y_ref):
        y_ref[:, 0:SCW] = (_f32(ab_ref) * cv_ref[...] * _silu(_f32(ag_ref))).astype(BF16)
        y_ref[:, SCW:2 * SCW] = (o_ref[...] * _silu(_f32(bg_ref))).astype(BF16)

    return pl.pallas_call(
        body, name=name, grid=(S // T,),
        out_shape=jax.ShapeDtypeStruct((S, 2 * SCW), BF16),
        in_specs=[_rows(T, SCW, 0), _rows(T, SCW, 3), _rows(T, SCW, 4), _rows(T, SCW), _rows(T, SCW)],
        out_specs=_rows(T, 2 * SCW), compiler_params=_cparams(),
    )(z, z, z, cv, o)


def _even_bwd_gates(dyc, z, cv, o, name):
    S = z.shape[0]
    T = _row_tile(S)

    def body(dya_ref, dyb_ref, ab_ref, ag_ref, bg_ref, cv_ref, o_ref,
             dab_ref, dag_ref, dbg_ref, dcv_ref, do_ref, dl_ref):
        dya, ab, ag, cv = dya_ref[...], _f32(ab_ref), _f32(ag_ref), cv_ref[...]
        sg = _silu(ag)
        dab_ref[...] = (dya * cv * sg).astype(BF16)
        dcv_ref[...] = dya * ab * sg
        dag_ref[...] = (dya * ab * cv * _dsilu(ag)).astype(BF16)
        dyb, bg, ov = dyb_ref[...], _f32(bg_ref), o_ref[...]
        dov = dyb * _silu(bg)
        do_ref[...] = dov.astype(BF16)
        dbg_ref[...] = (dyb * ov * _dsilu(bg)).astype(BF16)
        prod = dov * ov
        left = lax.broadcasted_iota(jnp.int32, (T, LANES), 1) < V_HEAD
        for p in range(HEADS // 2):
            blk = prod[:, p * LANES:(p + 1) * LANES]
            s0 = jnp.sum(jnp.where(left, blk, 0.0), axis=1, keepdims=True)
            s1 = jnp.sum(jnp.where(left, 0.0, blk), axis=1, keepdims=True)
            dt = jnp.where(left, s0, s1).T
            dl_ref[0, 2 * p * SUBLANES:(2 * p + 1) * SUBLANES, :] = dt[0:SUBLANES, :]
            dl_ref[0, (2 * p + 1) * SUBLANES:(2 * p + 2) * SUBLANES, :] = dt[V_HEAD:V_HEAD + SUBLANES, :]

    assert T == _attn_tile(S)
    bf = jax.ShapeDtypeStruct((S, SCW), BF16)
    ff = jax.ShapeDtypeStruct((S, SCW), F32)
    return pl.pallas_call(
        body, name=name, grid=(S // T,),
        out_shape=(bf, bf, bf, ff, bf, jax.ShapeDtypeStruct((S // T, HEADS * SUBLANES, T), F32)),
        in_specs=[_rows(T, SCW, 0), _rows(T, SCW, 1), _rows(T, SCW, 0), _rows(T, SCW, 3), _rows(T, SCW, 4),
                  _rows(T, SCW), _rows(T, SCW)],
        out_specs=(_rows(T, SCW),) * 5 + (pl.BlockSpec((1, HEADS * SUBLANES, T), lambda i: (i, 0, 0)),),
        compiler_params=_cparams(),
    )(dyc, dyc, z, z, z, cv, o)


def _qkv_bwd(dq, dk, dv, z, tabs, w_q, w_kv, qg, kvg, name):
    S = dk.shape[0]
    T = _attn_tile(S)
    HW = HEADS * HEAD_PAD
    VW = HEADS * V_HEAD
    scale = 1.0 / math.sqrt(QK_NOPE + QK_ROPE)

    def fn(ins, outs):
        dq_ref, dk_ref, dv_ref, cq_ref, ckv_ref, ct_ref, ut_ref, dt_ref, wq_ref, wkv_ref, qg_ref, kvg_ref = ins
        dqp_ref, dkvp_ref, dcq_ref, dckv_ref, dkr_ref = outs
        ct, ut, dt = ct_ref[...], ut_ref[...], dt_ref[...]
        dkr = jnp.zeros((T, HEAD_PAD), F32)
        for h in range(HEADS):
            sl = slice(h * HEAD_PAD, (h + 1) * HEAD_PAD)
            dqp_ref[:, sl] = (_rope_t(dq_ref[0, sl, :].T, ct, ut, dt) * scale).astype(BF16)
            dkh = _rope_t(dk_ref[:, sl], ct, ut, dt)
            dkr = dkr + dkh
            dkvp_ref[:, sl] = dkh.astype(BF16)
        dkvp_ref[:, HW:] = dv_ref[...].astype(BF16)
        dkr_ref[...] = dkr.astype(BF16)
        sums = []
        for lat_ref, g_ref, dpre_ref, w_ref, dlat_ref in ((cq_ref, qg_ref, dqp_ref, wq_ref, dcq_ref),
                                                         (ckv_ref, kvg_ref, dkvp_ref, wkv_ref, dckv_ref)):
            dn = lax.dot_general(dpre_ref[...], w_ref[...], _NT, preferred_element_type=F32)
            xv = _f32(lat_ref)
            rstd = lax.rsqrt(jnp.mean(xv * xv, axis=-1, keepdims=True) + EPS)
            xh = xv * rstd
            dxh = dn * g_ref[...]
            dlat_ref[...] = (rstd * (dxh - xh * jnp.mean(dxh * xh, axis=-1, keepdims=True))).astype(BF16)
            sums.append(dn * xh)
        return sums

    return _col_sums(
        2, fn, [dq, dk, dv, z, z, *tabs, w_q, w_kv, qg, kvg],
        [pl.BlockSpec((1, HW, T), lambda i: (i, 0, 0)), _rows(T, HW), _rows(T, VW), _rows(T, QL, 10), _rows(T, KVL, 22),
         _rows(T, HEAD_PAD), _rows(T, HEAD_PAD), _rows(T, HEAD_PAD),
         _const(w_q.shape), _const(w_kv.shape), _const((1, QL)), _const((1, KVL))],
        [jax.ShapeDtypeStruct((S, HW), BF16), jax.ShapeDtypeStruct((S, HW + VW), BF16),
         jax.ShapeDtypeStruct((S, QL), BF16), jax.ShapeDtypeStruct((S, KVL), BF16),
         jax.ShapeDtypeStruct((S, HEAD_PAD), BF16)],
        [_rows(T, HW), _rows(T, HW + VW), _rows(T, QL), _rows(T, KVL), _rows(T, HEAD_PAD)],
        S, T, [QL, KVL], name)


def _even_dz(dab, du, z, dag, dbg, dcq, dckv, dkr, name):
    S = z.shape[0]
    T = _row_tile(S)

    def body(dab_ref, du_ref, ac_ref, ax_ref, dag_ref, dbg_ref, dcq_ref, dckv_ref, dkr_ref, dz_ref):
        duv = du_ref[...]
        dz_ref[:, 0:SCW] = dab_ref[...]
        dz_ref[:, SCW:2 * SCW] = (duv * _f32(ax_ref)).astype(BF16)
        dz_ref[:, 2 * SCW:3 * SCW] = (duv * _f32(ac_ref)).astype(BF16)
        dz_ref[:, 3 * SCW:4 * SCW] = dag_ref[...]
        dz_ref[:, 4 * SCW:5 * SCW] = dbg_ref[...]
        dz_ref[:, 5 * SCW:5 * SCW + QL] = dcq_ref[...]
        dz_ref[:, 5 * SCW + QL:5 * SCW + QL + KVL] = dckv_ref[...]
        dz_ref[:, 5 * SCW + QL + KVL:ZE] = dkr_ref[...]

    return pl.pallas_call(
        body, name=name, grid=(S // T,),
        out_shape=jax.ShapeDtypeStruct((S, ZE), BF16),
        in_specs=[_rows(T, SCW), _rows(T, SCW), _rows(T, SCW, 1), _rows(T, SCW, 2), _rows(T, SCW), _rows(T, SCW),
                  _rows(T, QL), _rows(T, KVL), _rows(T, HEAD_PAD)],
        out_specs=_rows(T, ZE), compiler_params=_cparams(),
    )(dab, du, z, z, dag, dbg, dcq, dckv, dkr)


def _odd_pre(z, name):
    S, D = z.shape[0], z.shape[1] // 3
    T = _row_tile(S)

    def body(val_ref, glu_ref, u_ref):
        u_ref[...] = _f32(val_ref) * _sigmoid(_f32(glu_ref))

    return pl.pallas_call(
        body, name=name, grid=(S // T,),
        out_shape=jax.ShapeDtypeStruct((S, D), F32),
        in_specs=[_rows(T, D, 0), _rows(T, D, 1)], out_specs=_rows(T, D),
        compiler_params=_cparams(),
    )(z, z)


def _layer_norm_stats(cv):
    mu = jnp.mean(cv, axis=-1, keepdims=True)
    cen = cv - mu
    rstd = lax.rsqrt(jnp.mean(cen * cen, axis=-1, keepdims=True) + EPS)
    return cen * rstd, rstd


def _odd_post(cv, z, ln_g, ln_b, name):
    S, D = cv.shape
    T = _row_tile(S)

    def body(cv_ref, sg_ref, g_ref, b_ref, y_ref):
        cvh, _ = _layer_norm_stats(cv_ref[...])
        y_ref[...] = (_silu(cvh * g_ref[...] + b_ref[...]) * _silu(_f32(sg_ref))).astype(BF16)

    return pl.pallas_call(
        body, name=name, grid=(S // T,),
        out_shape=jax.ShapeDtypeStruct((S, D), BF16),
        in_specs=[_rows(T, D), _rows(T, D, 2), _const((1, D)), _const((1, D))],
        out_specs=_rows(T, D), compiler_params=_cparams(),
    )(cv, z, ln_g, ln_b)


def _odd_bwd_norm(dyi, cv, z, ln_g, ln_b, name):
    S, D = cv.shape
    T = _row_tile(S)

    def fn(ins, outs):
        dy_ref, cv_ref, sg_ref, g_ref, b_ref = ins
        dcv_ref, dsg_ref = outs
        cvh, rstd = _layer_norm_stats(cv_ref[...])
        ln = cvh * g_ref[...] + b_ref[...]
        sgv, dy = _f32(sg_ref), dy_ref[...]
        dsg_ref[...] = (dy * _silu(ln) * _dsilu(sgv)).astype(BF16)
        dln = dy * _silu(sgv) * _dsilu(ln)
        dh = dln * g_ref[...]
        dcv_ref[...] = rstd * (dh - jnp.mean(dh, axis=-1, keepdims=True)
                               - cvh * jnp.mean(dh * cvh, axis=-1, keepdims=True))
        return [dln * cvh, dln]

    return _col_sums(2, fn, [dyi, cv, z, ln_g, ln_b],
                     [_rows(T, D), _rows(T, D), _rows(T, D, 2), _const((1, D)), _const((1, D))],
                     [jax.ShapeDtypeStruct((S, D), F32), jax.ShapeDtypeStruct((S, D), BF16)],
                     [_rows(T, D), _rows(T, D)], S, T, [D, D], name)


def _odd_dz(du, z, dsg, name):
    S, D = du.shape
    T = _row_tile(S)

    def body(du_ref, val_ref, glu_ref, dsg_ref, dz_ref):
        duv = du_ref[...]
        sig = _sigmoid(_f32(glu_ref))
        dz_ref[:, 0:D] = (duv * sig).astype(BF16)
        dz_ref[:, D:2 * D] = (duv * _f32(val_ref) * sig * (1.0 - sig)).astype(BF16)
        dz_ref[:, 2 * D:3 * D] = dsg_ref[...]

    return pl.pallas_call(
        body, name=name, grid=(S // T,),
        out_shape=jax.ShapeDtypeStruct((S, 3 * D), BF16),
        in_specs=[_rows(T, D), _rows(T, D, 0), _rows(T, D, 1), _rows(T, D)],
        out_specs=_rows(T, 3 * D), compiler_params=_cparams(),
    )(du, z, z, dsg)


ADAM_BLOCK_ELEMS = 128 * 1024


def _adam_tiles(R, C):
    if R * C <= ADAM_BLOCK_ELEMS:
        return R, C
    tr = R
    for cand in range(SUBLANES, R, SUBLANES):
        if R % cand == 0 and cand * C <= ADAM_BLOCK_ELEMS:
            tr = cand
    if tr < R:
        return tr, C
    tc = C
    for cand in range(LANES, C, LANES):
        if C % cand == 0 and R * cand <= ADAM_BLOCK_ELEMS:
            tc = cand
    return R, tc


def _adamw(g_parts, w, m, v, name):
    if not isinstance(g_parts, (list, tuple)):
        g_parts = [g_parts]
    ng = len(g_parts)
    _, R, C = g_parts[0].shape
    tr, tc = _adam_tiles(R, C)

    def body(*refs):
        g_refs = refs[:ng]
        w_ref, m_ref, v_ref, go_ref, d_ref, mo_ref, vo_ref = refs[ng:]
        g = None
        for g_ref in g_refs:
            for p in range(g_ref.shape[0]):
                part = g_ref[p].astype(F32)
                g = part if g is None else g + part
        mn = ADAM_B1 * m_ref[...] + (1.0 - ADAM_B1) * g
        vn = ADAM_B2 * v_ref[...] + (1.0 - ADAM_B2) * (g * g)
        m_hat = mn / (1.0 - ADAM_B1 ** ADAM_STEP)
        v_hat = vn / (1.0 - ADAM_B2 ** ADAM_STEP)
        go_ref[...] = g
        d_ref[...] = -ADAM_LR * (m_hat / (jnp.sqrt(v_hat) + ADAM_EPS) + ADAM_WD * w_ref[...])
        mo_ref[...] = mn
        vo_ref[...] = vn

    slab = jax.ShapeDtypeStruct((R, C), F32)
    blk = pl.BlockSpec((tr, tc), lambda i, j: (i, j))
    return pl.pallas_call(
        body, name=name, grid=(R // tr, C // tc),
        out_shape=(slab,) * 4,
        in_specs=[pl.BlockSpec((g.shape[0], tr, tc), lambda i, j: (0, i, j)) for g in g_parts] + [blk, blk, blk],
        out_specs=(blk,) * 4, compiler_params=_cparams(),
    )(*g_parts, w, m, v)


def _adamw_slab(g_parts, w, m, v, layer, prev, name):
    ng = len(g_parts)
    NL, R, C = w.shape
    tr, tc = _adam_tiles(R, C)

    def body(*refs):
        g_refs = refs[:ng]
        w_ref, m_ref, v_ref = refs[ng:ng + 3]
        go_ref, d_ref, mo_ref, vo_ref = refs[-4:]
        g = None
        for g_ref in g_refs:
            for p in range(g_ref.shape[0]):
                part = g_ref[p].astype(F32)
                g = part if g is None else g + part
        mn = ADAM_B1 * m_ref[0] + (1.0 - ADAM_B1) * g
        vn = ADAM_B2 * v_ref[0] + (1.0 - ADAM_B2) * (g * g)
        m_hat = mn / (1.0 - ADAM_B1 ** ADAM_STEP)
        v_hat = vn / (1.0 - ADAM_B2 ** ADAM_STEP)
        go_ref[0] = g
        d_ref[0] = -ADAM_LR * (m_hat / (jnp.sqrt(v_hat) + ADAM_EPS) + ADAM_WD * w_ref[0])
        mo_ref[0] = mn
        vo_ref[0] = vn

    blk = pl.BlockSpec((1, tr, tc), lambda i, j: (layer, i, j))
    n_in = ng + 3
    prev = list(prev) if prev is not None else []
    return pl.pallas_call(
        body, name=name, grid=(R // tr, C // tc),
        out_shape=(jax.ShapeDtypeStruct((NL, R, C), F32),) * 4,
        in_specs=([pl.BlockSpec((g.shape[0], tr, tc), lambda i, j: (0, i, j)) for g in g_parts] + [blk, blk, blk]
                  + [pl.BlockSpec(memory_space=pl.ANY)] * len(prev)),
        out_specs=(blk,) * 4,
        input_output_aliases={n_in + k: k for k in range(len(prev))},
        compiler_params=_cparams(),
    )(*g_parts, w, m, v, *prev)


def _gather_cols(g, shape):
    nd = len(shape)
    t = jnp.moveaxis(g, 0, nd - 1)
    return t.reshape(tuple(shape[:-1]) + (N_DEV * shape[-1],))


def _scatter_cols(full, n):
    t = full.reshape(full.shape[:-1] + (N_DEV, n))
    return jnp.moveaxis(t, -2, 0)


def kernel(x, c, positions, ada_w, ada_b, pre_norm_g, post_norm_g, even_w_in, even_sc_conv_w, even_sc_conv_b, even_q_norm_g, even_kv_norm_g, even_w_uq, even_w_ukv, even_w_out, odd_w_in, odd_conv_w, odd_conv_b, odd_ln_g, odd_ln_b, odd_w_out, loss_target, m_ada_w, m_ada_b, m_pre_norm_g, m_post_norm_g, m_even_w_in, m_even_sc_conv_w, m_even_sc_conv_b, m_even_q_norm_g, m_even_kv_norm_g, m_even_w_uq, m_even_w_ukv, m_even_w_out, m_odd_w_in, m_odd_conv_w, m_odd_conv_b, m_odd_ln_g, m_odd_ln_b, m_odd_w_out, v_ada_w, v_ada_b, v_pre_norm_g, v_post_norm_g, v_even_w_in, v_even_sc_conv_w, v_even_sc_conv_b, v_even_q_norm_g, v_even_kv_norm_g, v_even_w_uq, v_even_w_ukv, v_even_w_out, v_odd_w_in, v_odd_conv_w, v_odd_conv_b, v_odd_ln_g, v_odd_ln_b, v_odd_w_out):
    S, D = x.shape[1], x.shape[2]
    L = ada_w.shape[0]
    NE, NO = even_w_in.shape[0], odd_w_in.shape[0]
    me = 4 * lax.axis_index("x") + 2 * lax.axis_index("y") + lax.axis_index("c")
    x0 = x[0]
    target = loss_target[0]

    small_parts = [c, even_sc_conv_w, odd_conv_w, odd_conv_b, odd_ln_g, odd_ln_b]
    small_shapes = [p.shape for p in small_parts]
    sg = _exchange([_pack(small_parts, F32, SUBLANES)], False, "gather_small")[0].reshape(N_DEV, -1)
    c_all, scw_g, ocw_g, ocb_g, olg_g, olb_g = _unpack(sg, small_shapes)
    c_all = c_all.reshape(N_DEV, D)
    sc_conv_w = _gather_cols(scw_g, even_sc_conv_w.shape)
    o_conv_w = _gather_cols(ocw_g, odd_conv_w.shape)
    o_conv_b = _gather_cols(ocb_g, odd_conv_b.shape)
    o_ln_g = _gather_cols(olg_g, odd_ln_g.shape)
    o_ln_b = _gather_cols(olb_g, odd_ln_b.shape)

    pad_q = HEAD_PAD - QK_NOPE - QK_ROPE
    w_local = [jnp.swapaxes(even_w_in, 1, 2).astype(BF16),
               jnp.pad(even_w_uq, ((0, 0), (0, 0), (0, pad_q))).astype(BF16),
               jnp.pad(even_w_ukv[..., :QK_NOPE], ((0, 0), (0, 0), (0, HEAD_PAD - QK_NOPE))).astype(BF16),
               even_w_ukv[..., QK_NOPE:].astype(BF16),
               even_w_out.astype(BF16), odd_w_in.astype(BF16), odd_w_out.astype(BF16)]
    n_ada = ada_w.shape[2]
    ada_b_cols = lax.dynamic_slice_in_dim(ada_b, me * n_ada, n_ada, axis=1).reshape(L, 1, n_ada)
    mod_slab = _ada_fwd(c_all, ada_w, ada_b_cols)
    mod_g = _exchange([_pack([mod_slab], F32, SUBLANES)], False, "gather_mod")[0].reshape(N_DEV, -1)
    mod_all = mod_g[:, :L * N_DEV * n_ada].reshape(N_DEV, L, N_DEV, n_ada)
    mod = lax.dynamic_index_in_dim(mod_all, me, axis=2, keepdims=False)
    mod = jnp.moveaxis(mod, 0, 1).reshape(L, 3 * D)
    shift, scale, gate = mod[:, :D], mod[:, D:2 * D], mod[:, 2 * D:]

    heads_to_cols = lambda g: jnp.moveaxis(g, 0, 1).reshape(g.shape[1], -1)
    w_handles = {}
    token = jnp.broadcast_to(jnp.minimum(jnp.abs(mod[0, 0]), 0.0), (SUBLANES, LANES))
    for layer in range(L):
        i = layer // 2
        groups = ({"in": [w_local[0][i]], "rest": [w[i] for w in w_local[1:5]]} if layer % 2 == 0
                  else {"all": [w[i] for w in w_local[5:]]})
        for key, mine in groups.items():
            mine = [w + token[0, 0].astype(BF16) for w in mine]
            w_handles[layer, key], token = _exchange_start(mine, False, f"gather_weights_start_l{layer}_{key}")
    w_token = token

    def arrived(layer, key, after):
        return _exchange_wait(w_handles[layer, key], False, after, f"gather_weights_wait_l{layer}_{key}")[1]

    e_w_in_k, e_w_q_k, e_w_kv_k, e_w_out, o_w_in, o_w_out = ([None] * NE, [None] * NE, [None] * NE, [None] * NE,
                                                             [None] * NO, [None] * NO)

    half = QK_ROPE // 2
    inv_freq = 1.0 / (ROPE_THETA ** (jnp.arange(0, QK_ROPE, 2, dtype=F32) / QK_ROPE))
    inv_lane = jnp.zeros((HEAD_PAD,), F32).at[QK_NOPE:QK_NOPE + QK_ROPE].set(jnp.concatenate([inv_freq, inv_freq]))
    tabs = _rope_tables(positions.astype(F32).reshape(S, 1), inv_lane.reshape(1, HEAD_PAD))
    del half

    row = lambda a: a.reshape(1, -1)
    scb = even_sc_conv_b
    KP3, KP31 = SUBLANES, 32

    saved = []
    xs = x0
    lv = jnp.stack([pre_norm_g, post_norm_g, scale, shift, gate], axis=1).reshape(L * 5, 1, D)
    PRE_G, POST_G, SCALE, SHIFT, GATE = range(5)
    vec = lambda layer, k: (lv, layer * 5 + k)
    h = _pre_norm(xs, vec(0, PRE_G), vec(0, SCALE), vec(0, SHIFT), "pre_norm_l0", after=w_token)
    for layer in range(L):
        i = layer // 2
        tag = f"l{layer}"
        first = [h, tabs[0]] if layer == 0 else h
        if layer % 2 == 0:
            wt = arrived(layer, "in", first)[0].reshape(-1, D)
            e_w_in_k[i] = jnp.concatenate([wt[:2048], wt[2464:2976], wt[2048:2432], jnp.zeros((QK_NOPE, D), BF16),
                                           wt[2432:2464], jnp.zeros((pad_q, D), BF16)], axis=0)
            z, u, qn, kvn = _w_in_even(h, e_w_in_k[i], row(even_q_norm_g[i]), row(even_kv_norm_g[i]), f"w_in_{tag}")
            eq_g, ek_g, ev_g, eout_g = arrived(layer, "rest", z)
            e_w_q_k[i] = heads_to_cols(eq_g)
            e_w_kv_k[i] = jnp.concatenate([heads_to_cols(ek_g), heads_to_cols(ev_g)], axis=-1)
            e_w_out[i] = eout_g.reshape(-1, D)
            cw = jnp.pad(sc_conv_w[i], ((0, KP3 - SC_KERNEL), (0, 0)))
            cv = _conv_fwd(u, cw, row(scb[i]), SC_KERNEL, f"conv_{tag}")
            q, k, v, kT3, vT3 = _qkv_fwd_t(qn, kvn, z, tabs, e_w_q_k[i], e_w_kv_k[i], f"qkv_{tag}")
            o, lse = _attn_fwd_t(q, k, vT3, f"attn_{tag}")
            ycat = _even_post(z, cv, o, f"even_post_{tag}")
            y = _matmul(ycat, e_w_out[i], "nn", F32, f"w_out_{tag}", tn=1024)
            saved.append(dict(x=xs, h=h, z=z, u=u, qn=qn, kvn=kvn, cw=cw, cv=cv, q=q, k=k, v=v, kT3=kT3, o=o, lse=lse,
                              ycat=ycat, y=y))
        else:
            owin_g, oout_g = arrived(layer, "all", first)
            o_w_in[i], o_w_out[i] = heads_to_cols(owin_g), oout_g.reshape(-1, D)
            z, u = _w_in_odd(h, o_w_in[i], f"w_in_{tag}")
            cw = jnp.pad(o_conv_w[i], ((0, KP31 - CONF_KERNEL), (0, 0)))
            cv = _conv_fwd(u, cw, row(o_conv_b[i]), CONF_KERNEL, f"conv_{tag}")
            yin = _odd_post(cv, z, row(o_ln_g[i]), row(o_ln_b[i]), f"odd_post_{tag}")
            y = _matmul(yin, o_w_out[i], "nn", F32, f"w_out_{tag}", tn=1024)
            saved.append(dict(x=xs, h=h, z=z, u=u, cw=cw, cv=cv, yin=yin, y=y))
        if layer < L - 1:
            xs, h = _post_pre_norm(xs, y, vec(layer, POST_G), vec(layer, GATE), vec(layer + 1, PRE_G),
                                   vec(layer + 1, SCALE), vec(layer + 1, SHIFT), f"post_pre_norm_{tag}")

    dx, dy, err_sq, dgate, g_post_last = _loss_post_norm_bwd(xs, y, vec(L - 1, POST_G), vec(L - 1, GATE), target,
                                                             "loss_post_norm_bwd")
    loss = lax.psum(_scaled_total(err_sq, 0.5 / D, "loss_total")[0, 0], MESH_AXES)

    g_pre, g_post, dmod = [None] * L, [None] * L, [None] * L
    g_e_w_in, g_e_w_uq, g_e_w_ukv, g_e_w_out = [None] * NE, [None] * NE, [None] * NE, [None] * NE
    g_scw, g_scb, g_qg, g_kvg = [None] * NE, [None] * NE, [None] * NE, [None] * NE
    g_o_w_in, g_o_w_out, g_ocw, g_ocb, g_olg, g_olb = ([None] * NO for _ in range(6))
    sm_w = [even_sc_conv_w, odd_conv_w, odd_conv_b, odd_ln_g, odd_ln_b]
    sm_rows = _pack(sm_w, F32, SUBLANES).shape[0]

    def small_slab():
        full = [_scatter_cols(jnp.stack(g_scw), even_sc_conv_w.shape[-1]),
                _scatter_cols(jnp.stack(g_ocw), odd_conv_w.shape[-1]),
                _scatter_cols(jnp.concatenate(g_ocb, 0), odd_conv_b.shape[-1]),
                _scatter_cols(jnp.concatenate(g_olg, 0), odd_ln_g.shape[-1]),
                _scatter_cols(jnp.concatenate(g_olb, 0), odd_ln_b.shape[-1])]
        flat = jnp.concatenate([g.reshape(N_DEV, -1) for g in full], axis=1)
        return jnp.pad(flat, ((0, 0), (0, sm_rows * PACK_COLS - flat.shape[1]))).reshape(N_DEV, sm_rows, PACK_COLS)

    scatters = []
    bw_token = jnp.zeros((SUBLANES, LANES), F32)

    def start_scatter(tag, names, parts):
        own = [lax.dynamic_slice_in_dim(g, me, 1, axis=0) for g in parts]
        handle, token = _exchange_start([g.astype(BF16) for g in parts], True, f"scatter_grads_start_{tag}")
        scatters.append((tag, names, handle, own))
        return token

    for layer in reversed(range(L)):
        i = layer // 2
        tag = f"l{layer}"
        sv = saved[layer]
        if layer == L - 1:
            g_post[layer] = g_post_last
        if layer % 2 == 0:
            dyc = _matmul(dy, e_w_out[i], "nt", F32, f"d_ycat_{tag}", tn=1024)
            g_e_w_out[i] = _matmul(sv["ycat"], dy, "tn", BF16, f"g_w_out_{tag}", tn=1024).reshape(N_DEV, -1, D)
            if layer == 0:
                bw_token = start_scatter("l0_out", [("even_w_out", i)], [g_e_w_out[i]])
            dab, dag, dbg, dcv, do, delta = _even_bwd_gates(dyc, sv["z"], sv["cv"], sv["o"], f"even_gates_bwd_{tag}")
            du, dcw, g_scb[i] = _conv_bwd(dcv, sv["u"], sv["cw"] + bw_token[0, 0], SC_KERNEL, f"conv_bwd_{tag}")
            g_scw[i] = dcw[:SC_KERNEL]
            dq, dk, dv = _attn_bwd_t(sv["q"], sv["k"], sv["v"], sv["kT3"], do, sv["lse"], delta, f"attn_bwd_{tag}")
            (dqp, dkvp, dcq, dckv, dkr, g_qg[i], g_kvg[i]) = _qkv_bwd(
                dq, dk, dv, sv["z"], tabs, e_w_q_k[i], e_w_kv_k[i],
                row(even_q_norm_g[i]), row(even_kv_norm_g[i]), f"qkv_bwd_{tag}")
            gq = _matmul(sv["qn"], dqp, "tn", BF16, f"g_w_uq_{tag}", tn=1024)
            gkv = _matmul(sv["kvn"], dkvp, "tn", BF16, f"g_w_ukv_{tag}")
            g_e_w_uq[i] = jnp.moveaxis(gq.reshape(QL, HEADS, HEAD_PAD)[..., :QK_NOPE + QK_ROPE], 1, 0)
            g_e_w_ukv[i] = jnp.moveaxis(jnp.concatenate(
                [gkv[:, :HEADS * HEAD_PAD].reshape(KVL, HEADS, HEAD_PAD)[..., :QK_NOPE],
                 gkv[:, HEADS * HEAD_PAD:].reshape(KVL, HEADS, V_HEAD)], axis=-1), 1, 0)
            dz = _even_dz(dab, du, sv["z"], dag, dbg, dcq, dckv, dkr, f"even_dz_{tag}")
            gt = _matmul(dz, sv["h"], "tn", BF16, f"g_w_in_{tag}", tm=1024, tn=1024)
            g_e_w_in[i] = jnp.concatenate([gt[:2048], gt[2560:2944], gt[2944 + QK_NOPE:2944 + QK_NOPE + QK_ROPE],
                                           gt[2048:2560]], axis=0).reshape(N_DEV, -1, D)
            names = [("even_w_in", i), ("even_w_uq", i), ("even_w_ukv", i)]
            parts = [g_e_w_in[i], g_e_w_uq[i], g_e_w_ukv[i]]
            if layer == 0:
                names, parts = names + [("small", 0)], parts + [small_slab()]
            else:
                names, parts = names + [("even_w_out", i)], parts + [g_e_w_out[i]]
            bw_token = start_scatter(tag, names, parts)
            w_dh = e_w_in_k[i] + bw_token[0, 0].astype(BF16) if layer == 0 else e_w_in_k[i]
            dh = _matmul(dz, w_dh, "nn", F32, f"d_h_{tag}", tn=1024)
        else:
            dyi = _matmul(dy, o_w_out[i], "nt", F32, f"d_yin_{tag}", tn=1024)
            g_o_w_out[i] = _matmul(sv["yin"], dy, "tn", BF16, f"g_w_out_{tag}", tn=1024).reshape(N_DEV, -1, D)
            dcv, dsg, g_olg[i], g_olb[i] = _odd_bwd_norm(dyi, sv["cv"], sv["z"], row(o_ln_g[i]), row(o_ln_b[i]),
                                                         f"odd_norm_bwd_{tag}")
            du, dcw, g_ocb[i] = _conv_bwd(dcv, sv["u"], sv["cw"], CONF_KERNEL, f"conv_bwd_{tag}")
            g_ocw[i] = dcw[:CONF_KERNEL]
            dz = _odd_dz(du, sv["z"], dsg, f"odd_dz_{tag}")
            g_o_w_in[i] = _matmul(sv["h"], dz, "tn", BF16, f"g_w_in_{tag}", tm=1024, tn=odd_w_in.shape[-1],
                                  split_n=True)
            bw_token = start_scatter(tag, [("odd_w_in", i), ("odd_w_out", i)], [g_o_w_in[i], g_o_w_out[i]])
            dh = _matmul(dz, o_w_in[i], "nt", F32, f"d_h_{tag}", tn=1024)
        if layer > 0:
            (dx, dy, dshift, dscale, g_pre[layer], dgate_prev, g_post[layer - 1]) = _pre_post_norm_bwd(
                dh, sv["x"], dx, vec(layer, PRE_G), vec(layer, SCALE), saved[layer - 1]["y"], vec(layer - 1, POST_G),
                vec(layer - 1, GATE), f"pre_post_norm_bwd_{tag}", after=bw_token)
        else:
            dx, dshift, dscale, g_pre[layer] = _pre_norm_bwd(dh, sv["x"], dx, vec(layer, PRE_G), vec(layer, SCALE),
                                                             f"pre_norm_bwd_{tag}", after=bw_token)
            dgate_prev = None
        dmod[layer] = jnp.concatenate([dshift, dscale, dgate], axis=-1)
        dgate = dgate_prev
    grad_x = dx.reshape(1, S, D)

    rep_g = [jnp.concatenate(dmod, 0), jnp.concatenate(g_pre, 0), jnp.concatenate(g_post, 0),
             jnp.stack(g_scb), jnp.stack(g_qg), jnp.stack(g_kvg)]
    rep_w = [ada_b, pre_norm_g, post_norm_g, even_sc_conv_b, even_q_norm_g, even_kv_norm_g]
    rep_m = [m_ada_b, m_pre_norm_g, m_post_norm_g, m_even_sc_conv_b, m_even_q_norm_g, m_even_kv_norm_g]
    rep_v = [v_ada_b, v_pre_norm_g, v_post_norm_g, v_even_sc_conv_b, v_even_q_norm_g, v_even_kv_norm_g]
    rep_shapes = [w.shape for w in rep_w]
    rep_all = _exchange([_pack(rep_g, F32, SUBLANES)], False, "gather_small_grads")[0]
    rep_out = _adamw(rep_all, _pack(rep_w, F32, SUBLANES), _pack(rep_m, F32, SUBLANES), _pack(rep_v, F32, SUBLANES),
                     "adamw_replicated")
    rep_res = [_unpack(o.reshape(-1), rep_shapes) for o in rep_out]

    dmod_all = rep_all.reshape(N_DEV, -1)[:, :L * 3 * D].reshape(N_DEV, L, 3 * D)
    dmod_cols = jnp.moveaxis(lax.dynamic_slice_in_dim(dmod_all, me * n_ada, n_ada, axis=2), 0, 1)
    g_ada_w = _ada_bwd(c_all.T, dmod_cols)
    ada_out = _adamw(g_ada_w.reshape(1, -1, n_ada), ada_w.reshape(-1, n_ada),
                     m_ada_w.reshape(-1, n_ada), v_ada_w.reshape(-1, n_ada), "adamw_ada_w")
    ada_res = [o.reshape(ada_w.shape) for o in ada_out]

    sm_m = [m_even_sc_conv_w, m_odd_conv_w, m_odd_conv_b, m_odd_ln_g, m_odd_ln_b]
    sm_v = [v_even_sc_conv_w, v_odd_conv_w, v_odd_conv_b, v_odd_ln_g, v_odd_ln_b]
    sm_shapes = [w.shape for w in sm_w]
    state = {"even_w_in": (even_w_in, m_even_w_in, v_even_w_in), "even_w_uq": (even_w_uq, m_even_w_uq, v_even_w_uq),
             "even_w_ukv": (even_w_ukv, m_even_w_ukv, v_even_w_ukv), "even_w_out": (even_w_out, m_even_w_out, v_even_w_out),
             "odd_w_in": (odd_w_in, m_odd_w_in, v_odd_w_in), "odd_w_out": (odd_w_out, m_odd_w_out, v_odd_w_out)}
    state["even_w_in"] = tuple(jnp.swapaxes(t, 1, 2) for t in state["even_w_in"])
    big_res = {name: None for name in state}
    after = [bw_token, grad_x, rep_out[0], ada_out[0]]
    sm_res = None
    for tag, names, handle, own in scatters:
        _, landed = _exchange_wait(handle, True, after, f"scatter_grads_wait_{tag}")
        after = []
        for a, (name, i) in enumerate(names):
            if name == "small":
                sm_out = _adamw([own[a], landed[a]], _pack(sm_w, F32, SUBLANES), _pack(sm_m, F32, SUBLANES),
                                _pack(sm_v, F32, SUBLANES), "adamw_small_sharded")
                sm_res = [_unpack(o.reshape(-1), sm_shapes) for o in sm_out]
                continue
            big_res[name] = _adamw_slab([own[a], landed[a]], *state[name], i, big_res[name], f"adamw_{name}_{i}")
            after += list(big_res[name])
    sh_res = [dict(zip(["even_sc_conv_w", "odd_conv_w", "odd_conv_b", "odd_ln_g", "odd_ln_b"], sm_res[kind]))
              for kind in range(4)]
    for name in state:
        for kind in range(4):
            res = big_res[name][kind]
            sh_res[kind][name] = jnp.swapaxes(res, 1, 2) if name == "even_w_in" else res

    order = ["ada_w", "ada_b", "pre_norm_g", "post_norm_g", "even_w_in", "even_sc_conv_w", "even_sc_conv_b",
             "even_q_norm_g", "even_kv_norm_g", "even_w_uq", "even_w_ukv", "even_w_out", "odd_w_in", "odd_conv_w",
             "odd_conv_b", "odd_ln_g", "odd_ln_b", "odd_w_out"]
    rep_names = ["ada_b", "pre_norm_g", "post_norm_g", "even_sc_conv_b", "even_q_norm_g", "even_kv_norm_g"]
    outs = [loss, grad_x]
    for kind in range(4):
        for name in order:
            if name == "ada_w":
                outs.append(ada_res[kind])
            elif name in rep_names:
                outs.append(rep_res[kind][rep_names.index(name)])
            else:
                outs.append(sh_res[kind][name])
    return tuple(outs)
```

```python
import math

import jax
import jax.numpy as jnp
from jax import lax
from jax.experimental import pallas as pl
from jax.experimental.pallas import tpu as pltpu

F32 = jnp.float32
BF16 = jnp.bfloat16
MESH_AXES = ("x", "y", "c")
N_DEV = 8
EPS = 1e-6
CHUNK = 64
HEADS = 8
QK_NOPE = 64
QK_ROPE = 32
V_HEAD = 64
HEAD_PAD = 128
ROPE_THETA = 10000.0
SC_KERNEL = 3
CONF_KERNEL = 31
LANES = 128
SUBLANES = 8
PACK_COLS = 1024
VMEM_LIMIT = 48 * 1024 * 1024
NEG = -1e30

ADAM_LR = 0.001
ADAM_B1 = 0.9
ADAM_B2 = 0.999
ADAM_EPS = 1e-08
ADAM_WD = 0.01
ADAM_STEP = 10


def _cparams():
    return pltpu.CompilerParams(vmem_limit_bytes=VMEM_LIMIT)


def _sigmoid(x):
    return 1.0 / (1.0 + jnp.exp(-x))


def _f32(ref):
    return ref[...].astype(F32)


def _silu(x):
    return x * _sigmoid(x)


def _dsilu(x):
    s = _sigmoid(x)
    return s * (1.0 + x * (1.0 - s))


def _rows(T, width, cb=0):
    return pl.BlockSpec((T, width), lambda i: (i, cb))


def _const(shape):
    nd = len(shape)
    return pl.BlockSpec(shape, lambda i: (0,) * nd)


def _wide_tile(S):
    return min(512, S)


def _row_tile(S):
    return min(256, S)


def _exchange(srcs, scatter, name):
    n = len(srcs)
    shapes = [tuple(s.shape[1:]) if scatter else tuple(s.shape) for s in srcs]

    def body(*refs):
        src_refs, out_refs = refs[:n], refs[n:2 * n]
        send_sems, recv_sems, local_sems = refs[2 * n:]
        x, y, c = lax.axis_index("x"), lax.axis_index("y"), lax.axis_index("c")
        me = 4 * x + 2 * y + c
        owns, copies = [], []
        for a in range(n):
            def piece(d, a=a):
                return src_refs[a].at[d] if scatter else src_refs[a]

            own = pltpu.make_async_copy(piece(me), out_refs[a].at[me], local_sems.at[a])
            own.start()
            owns.append(own)
            for k in range(1, N_DEV):
                px, py, pc = x ^ ((k >> 2) & 1), y ^ ((k >> 1) & 1), c ^ (k & 1)
                peer = 4 * px + 2 * py + pc
                sem = a * (N_DEV - 1) + k - 1
                cp = pltpu.make_async_remote_copy(
                    src_ref=piece(peer), dst_ref=out_refs[a].at[me],
                    send_sem=send_sems.at[sem], recv_sem=recv_sems.at[sem],
                    device_id=(px, py, pc), device_id_type=pl.DeviceIdType.MESH)
                cp.start()
                arrival = pltpu.make_async_remote_copy(
                    src_ref=piece(peer), dst_ref=out_refs[a].at[peer],
                    send_sem=send_sems.at[sem], recv_sem=recv_sems.at[sem],
                    device_id=(x, y, c), device_id_type=pl.DeviceIdType.MESH)
                copies.append((cp, arrival))
        for _, arrival in copies:
            arrival.wait_recv()
        for cp, _ in copies:
            cp.wait_send()
        for own in owns:
            own.wait()

    return pl.pallas_call(
        body, name=name,
        out_shape=tuple(jax.ShapeDtypeStruct((N_DEV,) + shp, s.dtype) for shp, s in zip(shapes, srcs)),
        in_specs=[pl.BlockSpec(memory_space=pl.ANY)] * n,
        out_specs=tuple(pl.BlockSpec(memory_space=pl.ANY) for _ in range(n)),
        scratch_shapes=[pltpu.SemaphoreType.DMA((n * (N_DEV - 1),)),
                        pltpu.SemaphoreType.DMA((n * (N_DEV - 1),)),
                        pltpu.SemaphoreType.DMA((n,))],
    )(*srcs)


_HBM = pl.BlockSpec(memory_space=pltpu.HBM)
_SEM = pl.BlockSpec(memory_space=pltpu.SEMAPHORE)


def _peer(k):
    x, y, c = lax.axis_index("x"), lax.axis_index("y"), lax.axis_index("c")
    return x ^ ((k >> 2) & 1), y ^ ((k >> 1) & 1), c ^ (k & 1)


def _exchange_start(srcs, scatter, name):
    n = len(srcs)
    shapes = [tuple(s.shape[1:]) if scatter else tuple(s.shape) for s in srcs]
    slots = N_DEV - 1 if scatter else N_DEV
    lands = [lax.empty((slots,) + shp, s.dtype) for shp, s in zip(shapes, srcs)]
    if not scatter:
        here = 4 * lax.axis_index("x") + 2 * lax.axis_index("y") + lax.axis_index("c")
        lands = [lax.dynamic_update_index_in_dim(l, s, here, 0) for l, s in zip(lands, srcs)]

    def body(*refs):
        src_refs, land_refs = refs[:n], refs[n:2 * n]
        send_sems, recv_sems = refs[2 * n], refs[2 * n + 1]
        token = refs[4 * n + 2]
        me = 4 * lax.axis_index("x") + 2 * lax.axis_index("y") + lax.axis_index("c")
        for a in range(n):
            for k in range(1, N_DEV):
                px, py, pc = _peer(k)
                peer = 4 * px + 2 * py + pc
                pltpu.make_async_remote_copy(
                    src_ref=src_refs[a].at[peer] if scatter else src_refs[a],
                    dst_ref=land_refs[a].at[k - 1] if scatter else land_refs[a].at[me],
                    send_sem=send_sems.at[a * (N_DEV - 1) + k - 1], recv_sem=recv_sems.at[a * (N_DEV - 1) + k - 1],
                    device_id=(px, py, pc), device_id_type=pl.DeviceIdType.MESH).start()
        token[...] = jnp.zeros_like(token)

    hbm = lambda arrs: [pltpu.HBM(a.shape, a.dtype) for a in arrs]
    out = pl.pallas_call(
        body, name=name,
        out_shape=(pltpu.SemaphoreType.DMA((n * (N_DEV - 1),)), pltpu.SemaphoreType.DMA((n * (N_DEV - 1),)),
                   *hbm(srcs), *hbm(lands), jax.ShapeDtypeStruct((SUBLANES, LANES), F32)),
        in_specs=[_HBM] * (2 * n),
        out_specs=(_SEM, _SEM, *([_HBM] * (2 * n)), pl.BlockSpec(memory_space=pltpu.VMEM)),
        input_output_aliases={a: 2 + a for a in range(2 * n)},
        compiler_params=pltpu.CompilerParams(has_side_effects=pltpu.SideEffectType.DATAFLOW_SIDE_EFFECTING),
    )(*[pltpu.with_memory_space_constraint(s, pltpu.HBM) for s in srcs],
      *[pltpu.with_memory_space_constraint(l, pltpu.HBM) for l in lands])
    return (out[0], out[1], list(out[2:2 + n]), list(out[2 + n:2 + 2 * n])), out[2 + 2 * n]


def _exchange_wait(handle, scatter, after, name):
    send_sems, recv_sems, srcs, lands = handle
    n = len(srcs)
    after = list(after) if isinstance(after, (list, tuple)) else [after]

    def body(*refs):
        src_refs, land_refs = refs[:n], refs[n:2 * n]
        send_sems, recv_sems = refs[2 * n], refs[2 * n + 1]
        for a in range(n):
            for k in range(1, N_DEV):
                px, py, pc = _peer(k)
                peer = 4 * px + 2 * py + pc
                cp = pltpu.make_async_remote_copy(
                    src_ref=src_refs[a].at[peer] if scatter else src_refs[a],
                    dst_ref=land_refs[a].at[k - 1] if scatter else land_refs[a].at[peer],
                    send_sem=send_sems.at[a * (N_DEV - 1) + k - 1], recv_sem=recv_sems.at[a * (N_DEV - 1) + k - 1],
                    device_id=(px, py, pc), device_id_type=pl.DeviceIdType.MESH)
                cp.wait_send()
                cp.wait_recv()

    out = pl.pallas_call(
        body, name=name,
        out_shape=tuple(pltpu.HBM(a.shape, a.dtype) for a in srcs + lands),
        in_specs=[_HBM] * (2 * n) + [_SEM, _SEM] + [pl.BlockSpec(memory_space=pl.ANY)] * len(after),
        out_specs=tuple([_HBM] * (2 * n)),
        input_output_aliases={a: a for a in range(2 * n)},
        compiler_params=pltpu.CompilerParams(has_side_effects=pltpu.SideEffectType.DATAFLOW_SIDE_EFFECTING),
    )(*srcs, *lands, send_sems, recv_sems, *after)
    return list(out[:n]), list(out[n:])


def _pack(parts, dtype, row_mult):
    flat = jnp.concatenate([p.reshape(-1).astype(dtype) for p in parts])
    n = flat.shape[0]
    rows = -(-n // PACK_COLS)
    rows = -(-rows // row_mult) * row_mult
    flat = jnp.pad(flat, (0, rows * PACK_COLS - n))
    return flat.reshape(rows, PACK_COLS)


def _unpack(flat, shapes):
    out, off = [], 0
    for shp in shapes:
        n = math.prod(shp)
        out.append(flat[..., off:off + n].reshape(flat.shape[:-1] + tuple(shp)))
        off += n
    return out


_DIMS = {"nn": (((1,), (0,)), ((), ())), "nt": (((1,), (1,)), ((), ())), "tn": (((0,), (0,)), ((), ()))}


def _matmul(a, b, mode, out_dtype, name, tm=512, tn=512, tk=None, split_n=False):
    if mode == "nn":
        (M, K), (_, N) = a.shape, b.shape
    elif mode == "nt":
        (M, K), (N, _) = a.shape, b.shape
    else:
        (K, M), (_, N) = a.shape, b.shape
    tm, tn = min(tm, M), min(tn, N)
    tk = K if tk is None else min(tk, K)
    nk = K // tk
    assert M % tm == 0 and N % tn == 0 and K % tk == 0, (name, a.shape, b.shape)

    def body(a_ref, b_ref, o_ref, *scratch):
        p = lax.dot_general(a_ref[...].astype(BF16), b_ref[...].astype(BF16), _DIMS[mode],
                            preferred_element_type=F32)
        if split_n:
            o_ref[0] = p.astype(out_dtype)
        elif nk == 1:
            o_ref[...] = p.astype(out_dtype)
        else:
            acc = scratch[0]
            k = pl.program_id(2)

            @pl.when(k == 0)
            def _():
                acc[...] = p

            @pl.when(k > 0)
            def _():
                acc[...] += p

            @pl.when(k == nk - 1)
            def _():
                o_ref[...] = acc[...].astype(out_dtype)

    a_spec = (pl.BlockSpec((tk, tm), lambda i, j, k: (k, i)) if mode == "tn"
              else pl.BlockSpec((tm, tk), lambda i, j, k: (i, k)))
    b_spec = (pl.BlockSpec((tn, tk), lambda i, j, k: (j, k)) if mode == "nt"
              else pl.BlockSpec((tk, tn), lambda i, j, k: (k, j)))
    return pl.pallas_call(
        body, name=name, grid=(M // tm, N // tn, nk),
        out_shape=jax.ShapeDtypeStruct((N // tn, M, tn) if split_n else (M, N), out_dtype),
        in_specs=[a_spec, b_spec],
        out_specs=(pl.BlockSpec((1, tm, tn), lambda i, j, k: (j, i, 0)) if split_n
                   else pl.BlockSpec((tm, tn), lambda i, j, k: (i, j))),
        scratch_shapes=[pltpu.VMEM((tm, tn), F32)] if nk > 1 else [],
        compiler_params=_cparams(),
    )(a, b)


def _ada_fwd(c_all, ada_w, ada_b_cols):
    L, D, n = ada_w.shape

    def body(c_ref, w_ref, b_ref, o_ref):
        act = _silu(c_ref[...]).astype(BF16)
        o_ref[0] = jnp.dot(act, w_ref[0].astype(BF16), preferred_element_type=F32) + b_ref[0]

    return pl.pallas_call(
        body, name="ada_fwd", grid=(L,),
        out_shape=jax.ShapeDtypeStruct((L, N_DEV, n), F32),
        in_specs=[pl.BlockSpec((N_DEV, D), lambda l: (0, 0)),
                  pl.BlockSpec((1, D, n), lambda l: (l, 0, 0)),
                  pl.BlockSpec((1, 1, n), lambda l: (l, 0, 0))],
        out_specs=pl.BlockSpec((1, N_DEV, n), lambda l: (l, 0, 0)),
        compiler_params=_cparams(),
    )(c_all, ada_w, ada_b_cols)


def _ada_bwd(c_all_t, dmod_cols):
    D = c_all_t.shape[0]
    L, _, n = dmod_cols.shape

    def body(c_ref, d_ref, o_ref):
        act = _silu(c_ref[...])
        dm = d_ref[0]
        acc = act[:, 0:1] * dm[0:1, :]
        for b in range(1, N_DEV):
            acc = acc + act[:, b:b + 1] * dm[b:b + 1, :]
        o_ref[0] = acc

    return pl.pallas_call(
        body, name="ada_bwd", grid=(L,),
        out_shape=jax.ShapeDtypeStruct((L, D, n), F32),
        in_specs=[pl.BlockSpec((D, N_DEV), lambda l: (0, 0)),
                  pl.BlockSpec((1, N_DEV, n), lambda l: (l, 0, 0))],
        out_specs=pl.BlockSpec((1, D, n), lambda l: (l, 0, 0)),
        compiler_params=_cparams(),
    )(c_all_t, dmod_cols)


def _rope_tables(pos_col, inv_lane):
    S = pos_col.shape[0]
    T = _row_tile(S)
    half = QK_ROPE // 2

    def body(p_ref, f_ref, c_ref, up_ref, dn_ref):
        ang = p_ref[...] * f_ref[...]
        lane = lax.broadcasted_iota(jnp.int32, ang.shape, 1)
        first = (lane >= QK_NOPE) & (lane < QK_NOPE + half)
        second = (lane >= QK_NOPE + half) & (lane < QK_NOPE + QK_ROPE)
        cs, sn = jnp.cos(ang), jnp.sin(ang)
        c_ref[...] = jnp.where(first | second, cs, 1.0)
        up_ref[...] = jnp.where(first, -sn, 0.0)
        dn_ref[...] = jnp.where(second, sn, 0.0)

    tab = jax.ShapeDtypeStruct((S, HEAD_PAD), F32)
    return pl.pallas_call(
        body, name="rope_tables", grid=(S // T,),
        out_shape=(tab, tab, tab),
        in_specs=[_rows(T, 1), _const((1, HEAD_PAD))],
        out_specs=(_rows(T, HEAD_PAD),) * 3,
        compiler_params=_cparams(),
    )(pos_col, inv_lane)


def _rope(blk, ct, ut, dt):
    half = QK_ROPE // 2
    up = pltpu.roll(blk, HEAD_PAD - half, 1)
    dn = pltpu.roll(blk, half, 1)
    return blk * ct + up * ut + dn * dt


def _rope_t(d, ct, ut, dt):
    half = QK_ROPE // 2
    return d * ct + pltpu.roll(d * ut, half, 1) + pltpu.roll(d * dt, HEAD_PAD - half, 1)


def _row_operands(rows, after=None):
    ops, specs = [], []
    for a in rows:
        if isinstance(a, tuple):
            table, r = a
            ops.append(table)
            specs.append(pl.BlockSpec((None, 1, table.shape[-1]), lambda *_, r=r: (r, 0, 0)))
        else:
            ops.append(a)
            specs.append(pl.BlockSpec(a.shape, lambda *_, nd=a.ndim: (0,) * nd))
    if after is not None:
        ops.append(after)
        specs.append(pl.BlockSpec(memory_space=pl.ANY))
    return ops, specs


def _pre_norm(x, g, scale, shift, name, after=None):
    S, D = x.shape
    T = _row_tile(S)
    row_ops, row_specs = _row_operands([g, scale, shift], after)

    def body(x_ref, g_ref, sc_ref, sh_ref, *rest):
        h_ref = rest[-1]
        xv = x_ref[...]
        rstd = lax.rsqrt(jnp.mean(xv * xv, axis=-1, keepdims=True) + EPS)
        h_ref[...] = ((xv * rstd) * g_ref[...] * (1.0 + sc_ref[...]) + sh_ref[...]).astype(BF16)

    return pl.pallas_call(
        body, name=name, grid=(S // T,),
        out_shape=jax.ShapeDtypeStruct((S, D), BF16),
        in_specs=[_rows(T, D)] + row_specs,
        out_specs=_rows(T, D), compiler_params=_cparams(),
    )(x, *row_ops)


def _fold8(v):
    T, C = v.shape
    return v.reshape(T // SUBLANES, SUBLANES, C).sum(axis=0)


def _col_sums(n_sums, body_fn, ins, in_specs, outs, out_specs, S, T, widths, name):
    n_in, n_out = len(ins), len(outs)
    nt = S // T

    def body(*refs):
        in_refs = refs[:n_in]
        out_refs = refs[n_in:n_in + n_out]
        sum_refs = refs[n_in + n_out:n_in + n_out + n_sums]
        accs = refs[n_in + n_out + n_sums:]
        i = pl.program_id(0)
        terms = body_fn(in_refs, out_refs)

        @pl.when(i == 0)
        def _():
            for acc, t in zip(accs, terms):
                acc[...] = _fold8(t)

        @pl.when(i > 0)
        def _():
            for acc, t in zip(accs, terms):
                acc[...] += _fold8(t)

        @pl.when(i == nt - 1)
        def _():
            for acc, s_ref in zip(accs, sum_refs):
                s_ref[...] = jnp.sum(acc[...], axis=0, keepdims=True)

    return pl.pallas_call(
        body, name=name, grid=(nt,),
        out_shape=tuple(outs) + tuple(jax.ShapeDtypeStruct((1, w), F32) for w in widths),
        in_specs=in_specs,
        out_specs=tuple(out_specs) + tuple(_const((1, w)) for w in widths),
        scratch_shapes=[pltpu.VMEM((SUBLANES, w), F32) for w in widths],
        compiler_params=_cparams(),
    )(*ins)


def _pre_norm_bwd(dh, x, dxo, g, scale, name, after=None):
    S, D = x.shape
    T = _row_tile(S)
    row_ops, row_specs = _row_operands([g, scale], after)

    def fn(ins, outs):
        dh_ref, x_ref, dxo_ref, g_ref, sc_ref = ins[:5]
        xv, dv = x_ref[...], dh_ref[...]
        rstd = lax.rsqrt(jnp.mean(xv * xv, axis=-1, keepdims=True) + EPS)
        xh = xv * rstd
        dr = dv * (1.0 + sc_ref[...])
        dxh = dr * g_ref[...]
        outs[0][...] = dxo_ref[...] + rstd * (dxh - xh * jnp.mean(dxh * xh, axis=-1, keepdims=True))
        return [dv, dv * (xh * g_ref[...]), dr * xh]

    return _col_sums(3, fn, [dh, x, dxo] + row_ops,
                     [_rows(T, D), _rows(T, D), _rows(T, D)] + row_specs,
                     [jax.ShapeDtypeStruct((S, D), F32)], [_rows(T, D)], S, T, [D, D, D], name)


def _post_pre_norm(x, y, g_post, gate, g_pre, scale, shift, name):
    S, D = x.shape
    T = _row_tile(S)
    row_ops, row_specs = _row_operands([g_post, gate, g_pre, scale, shift])

    def body(x_ref, y_ref, gp_ref, gt_ref, g_ref, sc_ref, sh_ref, xn_ref, h_ref):
        yv = y_ref[...]
        rstd_y = lax.rsqrt(jnp.mean(yv * yv, axis=-1, keepdims=True) + EPS)
        xn = x_ref[...] + gt_ref[...] * ((yv * rstd_y) * gp_ref[...])
        xn_ref[...] = xn
        rstd = lax.rsqrt(jnp.mean(xn * xn, axis=-1, keepdims=True) + EPS)
        h_ref[...] = ((xn * rstd) * g_ref[...] * (1.0 + sc_ref[...]) + sh_ref[...]).astype(BF16)

    return pl.pallas_call(
        body, name=name, grid=(S // T,),
        out_shape=(jax.ShapeDtypeStruct((S, D), F32), jax.ShapeDtypeStruct((S, D), BF16)),
        in_specs=[_rows(T, D), _rows(T, D)] + row_specs,
        out_specs=(_rows(T, D), _rows(T, D)), compiler_params=_cparams(),
    )(x, y, *row_ops)


def _pre_post_norm_bwd(dh, x, dxo, g_pre, scale, y_prev, g_post_prev, gate_prev, name, after=None):
    S, D = x.shape
    T = _row_tile(S)
    row_ops, row_specs = _row_operands([g_pre, scale, g_post_prev, gate_prev], after)

    def fn(ins, outs):
        dh_ref, x_ref, dxo_ref, y_ref, g_ref, sc_ref, gp_ref, gt_ref = ins[:8]
        xv, dv = x_ref[...], dh_ref[...]
        rstd = lax.rsqrt(jnp.mean(xv * xv, axis=-1, keepdims=True) + EPS)
        xh = xv * rstd
        dr = dv * (1.0 + sc_ref[...])
        dxh = dr * g_ref[...]
        dx = dxo_ref[...] + rstd * (dxh - xh * jnp.mean(dxh * xh, axis=-1, keepdims=True))
        outs[0][...] = dx
        yv = y_ref[...]
        rstd_y = lax.rsqrt(jnp.mean(yv * yv, axis=-1, keepdims=True) + EPS)
        yh = yv * rstd_y
        dn = dx * gt_ref[...]
        dyh = dn * gp_ref[...]
        outs[1][...] = (rstd_y * (dyh - yh * jnp.mean(dyh * yh, axis=-1, keepdims=True))).astype(BF16)
        return [dv, dv * (xh * g_ref[...]), dr * xh, dx * (yh * gp_ref[...]), dn * yh]

    return _col_sums(5, fn, [dh, x, dxo, y_prev] + row_ops,
                     [_rows(T, D), _rows(T, D), _rows(T, D), _rows(T, D)] + row_specs,
                     [jax.ShapeDtypeStruct((S, D), F32), jax.ShapeDtypeStruct((S, D), BF16)],
                     [_rows(T, D), _rows(T, D)], S, T, [D] * 5, name)


def _loss_post_norm_bwd(x, y, g_post, gate, target, name):
    S, D = x.shape
    T = _row_tile(S)
    row_ops, row_specs = _row_operands([g_post, gate])

    def fn(ins, outs):
        x_ref, y_ref, t_ref, gp_ref, gt_ref = ins
        yv = y_ref[...]
        rstd_y = lax.rsqrt(jnp.mean(yv * yv, axis=-1, keepdims=True) + EPS)
        yh = yv * rstd_y
        e = x_ref[...] + gt_ref[...] * (yh * gp_ref[...]) - t_ref[...]
        dx = e * (1.0 / D)
        outs[0][...] = dx
        dn = dx * gt_ref[...]
        dyh = dn * gp_ref[...]
        outs[1][...] = (rstd_y * (dyh - yh * jnp.mean(dyh * yh, axis=-1, keepdims=True))).astype(BF16)
        return [e * e, dx * (yh * gp_ref[...]), dn * yh]

    return _col_sums(3, fn, [x, y, target] + row_ops,
                     [_rows(T, D), _rows(T, D), _rows(T, D)] + row_specs,
                     [jax.ShapeDtypeStruct((S, D), F32), jax.ShapeDtypeStruct((S, D), BF16)],
                     [_rows(T, D), _rows(T, D)], S, T, [D] * 3, name)


def _scaled_total(v, coef, name):
    def body(v_ref, o_ref):
        o_ref[...] = jnp.broadcast_to(jnp.sum(v_ref[...], axis=1, keepdims=True) * coef, (1, LANES))

    return pl.pallas_call(body, name=name, out_shape=jax.ShapeDtypeStruct((1, LANES), F32))(v)


CONV_ROWS = 64


def _conv_halo(K):
    return SUBLANES if K - 1 <= SUBLANES else 32


def _conv_fwd(u, w, b, K, name):
    S, C = u.shape
    KP = w.shape[0]
    T, HB, RS = min(512, S), _conv_halo(K), CONV_ROWS
    ratio = T // HB

    def body(u_ref, h_ref, w_ref, b_ref, o_ref, ext):
        i = pl.program_id(1)
        ext[0:HB, :] = jnp.where(i > 0, h_ref[...], 0.0)
        ext[HB:HB + T, :] = u_ref[...]
        for r0 in range(0, T, RS):
            acc = jnp.broadcast_to(b_ref[...], (RS, LANES))
            for k in range(K):
                off = HB - (K - 1) + k + r0
                acc = acc + w_ref[k:k + 1, :] * ext[off:off + RS, :]
            o_ref[r0:r0 + RS, :] = acc

    return pl.pallas_call(
        body, name=name, grid=(C // LANES, S // T),
        out_shape=jax.ShapeDtypeStruct((S, C), F32),
        in_specs=[pl.BlockSpec((T, LANES), lambda c, i: (i, c)),
                  pl.BlockSpec((HB, LANES), lambda c, i: (jnp.maximum(i * ratio - 1, 0), c)),
                  pl.BlockSpec((KP, LANES), lambda c, i: (0, c)),
                  pl.BlockSpec((1, LANES), lambda c, i: (0, c))],
        out_specs=pl.BlockSpec((T, LANES), lambda c, i: (i, c)),
        scratch_shapes=[pltpu.VMEM((HB + T, LANES), F32)],
        compiler_params=_cparams(),
    )(u, u, w, b)


def _conv_bwd(d, u, w, K, name):
    S, C = u.shape
    KP = w.shape[0]
    T, HB, RS = min(512, S), _conv_halo(K), CONV_ROWS
    ratio = T // HB
    nt = S // T
    last_halo = S // HB - 1

    def body(d_ref, dn_ref, u_ref, up_ref, w_ref, du_ref, dw_ref, db_ref, extd, extu, dws, dbs):
        i = pl.program_id(1)
        extd[0:T, :] = d_ref[...]
        extd[T:T + HB, :] = jnp.where(i < nt - 1, dn_ref[...], 0.0)
        extu[0:HB, :] = jnp.where(i > 0, up_ref[...], 0.0)
        extu[HB:HB + T, :] = u_ref[...]

        @pl.when(i == 0)
        def _():
            dws[...] = jnp.zeros_like(dws)
            dbs[...] = jnp.zeros_like(dbs)

        for r0 in range(0, T, RS):
            acc = jnp.zeros((RS, LANES), F32)
            for k in range(K):
                off = (K - 1 - k) + r0
                acc = acc + w_ref[k:k + 1, :] * extd[off:off + RS, :]
            du_ref[r0:r0 + RS, :] = acc
            dch = d_ref[r0:r0 + RS, :]
            dbs[...] += _fold8(dch)
            for k in range(K):
                off = HB - (K - 1) + k + r0
                dws[k * SUBLANES:(k + 1) * SUBLANES, :] += _fold8(dch * extu[off:off + RS, :])

        @pl.when(i == nt - 1)
        def _():
            dw_ref[...] = jnp.zeros_like(dw_ref)
            for k in range(K):
                dw_ref[k:k + 1, :] = jnp.sum(dws[k * SUBLANES:(k + 1) * SUBLANES, :], axis=0, keepdims=True)
            db_ref[...] = jnp.sum(dbs[...], axis=0, keepdims=True)

    return pl.pallas_call(
        body, name=name, grid=(C // LANES, nt),
        out_shape=(jax.ShapeDtypeStruct((S, C), F32), jax.ShapeDtypeStruct((KP, C), F32),
                   jax.ShapeDtypeStruct((1, C), F32)),
        in_specs=[pl.BlockSpec((T, LANES), lambda c, i: (i, c)),
                  pl.BlockSpec((HB, LANES), lambda c, i: (jnp.minimum((i + 1) * ratio, last_halo), c)),
                  pl.BlockSpec((T, LANES), lambda c, i: (i, c)),
                  pl.BlockSpec((HB, LANES), lambda c, i: (jnp.maximum(i * ratio - 1, 0), c)),
                  pl.BlockSpec((KP, LANES), lambda c, i: (0, c))],
        out_specs=(pl.BlockSpec((T, LANES), lambda c, i: (i, c)),
                   pl.BlockSpec((KP, LANES), lambda c, i: (0, c)),
                   pl.BlockSpec((1, LANES), lambda c, i: (0, c))),
        scratch_shapes=[pltpu.VMEM((T + HB, LANES), F32), pltpu.VMEM((HB + T, LANES), F32),
                        pltpu.VMEM((KP * SUBLANES, LANES), F32), pltpu.VMEM((SUBLANES, LANES), F32)],
        compiler_params=_cparams(),
    )(d, d, u, u, w)


SCW = 512
ZE = 3072
QL = 256
KVL = 128


def _rms_rows(x, g):
    rstd = lax.rsqrt(jnp.mean(x * x, axis=-1, keepdims=True) + EPS)
    return (x * rstd) * g


def _attn_tile(S):
    return min(256, S)


_NT = (((1,), (1,)), ((), ()))


LOG2E = math.log2(math.e)
ATTN_FWD_HEADS = 8
ATTN_BWD_HEADS = 4


def _chunk_mask_t(T):
    key = lax.broadcasted_iota(jnp.int32, (T, T), 0) // CHUNK
    qry = lax.broadcasted_iota(jnp.int32, (T, T), 1) // CHUNK
    return key <= qry


W_IN_ROWS = 512


def _w_in_even(h, w_t, qg, kvg, name):
    S, D = h.shape
    tm = min(W_IN_ROWS, S)

    def body(h_ref, w_ref, qg_ref, kvg_ref, z_ref, u_ref, qn_ref, kvn_ref):
        p = lax.dot_general(h_ref[...], w_ref[...], _NT, preferred_element_type=F32)
        z_ref[...] = p.astype(BF16)
        u_ref[...] = p[:, SCW:2 * SCW] * p[:, 2 * SCW:3 * SCW]
        qn_ref[...] = _rms_rows(p[:, 5 * SCW:5 * SCW + QL], qg_ref[...]).astype(BF16)
        kvn_ref[...] = _rms_rows(p[:, 5 * SCW + QL:5 * SCW + QL + KVL], kvg_ref[...]).astype(BF16)

    return pl.pallas_call(
        body, name=name, grid=(S // tm,),
        out_shape=(jax.ShapeDtypeStruct((S, ZE), BF16), jax.ShapeDtypeStruct((S, SCW), F32),
                   jax.ShapeDtypeStruct((S, QL), BF16), jax.ShapeDtypeStruct((S, KVL), BF16)),
        in_specs=[_rows(tm, D), _const(w_t.shape), _const((1, QL)), _const((1, KVL))],
        out_specs=(_rows(tm, ZE), _rows(tm, SCW), _rows(tm, QL), _rows(tm, KVL)),
        compiler_params=_cparams(),
    )(h, w_t, qg, kvg)


def _w_in_odd(h, w, name):
    S, D = h.shape
    tm = min(W_IN_ROWS, S)

    def body(h_ref, w_ref, z_ref, u_ref):
        p = jnp.dot(h_ref[...], w_ref[...], preferred_element_type=F32)
        z_ref[...] = p.astype(BF16)
        u_ref[...] = p[:, 0:D] * _sigmoid(p[:, D:2 * D])

    return pl.pallas_call(
        body, name=name, grid=(S // tm,),
        out_shape=(jax.ShapeDtypeStruct((S, 3 * D), BF16), jax.ShapeDtypeStruct((S, D), F32)),
        in_specs=[_rows(tm, D), _const(w.shape)],
        out_specs=(_rows(tm, 3 * D), _rows(tm, D)),
        compiler_params=_cparams(),
    )(h, w)


def _qkv_fwd_t(qn, kvn, z, tabs, w_q, w_kv, name):
    S = qn.shape[0]
    T = _attn_tile(S)
    HW = HEADS * HEAD_PAD
    scale = LOG2E / math.sqrt(QK_NOPE + QK_ROPE)

    def body(qn_ref, kvn_ref, kr_ref, ct_ref, ut_ref, dt_ref, wq_ref, wkv_ref, q_ref, k_ref, v_ref, kt_ref, vt_ref):
        ct, ut, dt = ct_ref[...], ut_ref[...], dt_ref[...]
        qa = jnp.dot(qn_ref[...], wq_ref[...], preferred_element_type=F32)
        kva = jnp.dot(kvn_ref[...], wkv_ref[...], preferred_element_type=F32)
        kr = _f32(kr_ref)
        ones_row = (lax.broadcasted_iota(jnp.int32, (V_HEAD, T), 0) == 0).astype(F32)
        for h in range(HEADS):
            sl = slice(h * HEAD_PAD, (h + 1) * HEAD_PAD)
            q_ref[:, sl] = (_rope(qa[:, sl], ct, ut, dt) * scale).astype(BF16)
            kh = _rope(kva[:, sl] + kr, ct, ut, dt)
            k_ref[:, sl] = kh.astype(BF16)
            kt_ref[0, sl, :] = kh.T.astype(BF16)
        v_ref[...] = kva[:, HW:].astype(BF16)
        for p in range(HEADS // 2):
            vpt = kva[:, HW + p * LANES:HW + (p + 1) * LANES].T
            for h in range(2):
                r0 = (2 * p + h) * HEAD_PAD
                vt_ref[0, r0:r0 + V_HEAD, :] = vpt[h * V_HEAD:(h + 1) * V_HEAD, :].astype(BF16)
                vt_ref[0, r0 + V_HEAD:r0 + HEAD_PAD, :] = ones_row.astype(BF16)

    t3 = jax.ShapeDtypeStruct((S // T, HW, T), BF16)
    return pl.pallas_call(
        body, name=name, grid=(S // T,),
        out_shape=(jax.ShapeDtypeStruct((S, HW), BF16), jax.ShapeDtypeStruct((S, HW), BF16),
                   jax.ShapeDtypeStruct((S, HEADS * V_HEAD), BF16), t3, t3),
        in_specs=[_rows(T, QL), _rows(T, KVL), _rows(T, HEAD_PAD, 23),
                  _rows(T, HEAD_PAD), _rows(T, HEAD_PAD), _rows(T, HEAD_PAD),
                  _const(w_q.shape), _const(w_kv.shape)],
        out_specs=(_rows(T, HW), _rows(T, HW), _rows(T, HEADS * V_HEAD),
                   pl.BlockSpec((1, HW, T), lambda i: (i, 0, 0)), pl.BlockSpec((1, HW, T), lambda i: (i, 0, 0))),
        compiler_params=_cparams(),
    )(qn, kvn, z, *tabs, w_q, w_kv)


def _attn_fwd_t(q, k, vT3, name):
    S = q.shape[0]
    T = _attn_tile(S)
    nq = S // T
    NH = ATTN_FWD_HEADS
    PW = NH * HEAD_PAD

    def body(q_ref, k_ref, vt_ref, o_ref, lse_ref, m_s, acc_s):
        i = pl.program_id(1)
        m_s[...] = jnp.full_like(m_s, NEG)
        acc_s[...] = jnp.zeros_like(acc_s)
        qv = q_ref[...]

        def step(j, masked):
            kb = k_ref[pl.ds(pl.multiple_of(j * T, T), T), :]
            vt = vt_ref[j]
            heads = [slice(h * HEAD_PAD, (h + 1) * HEAD_PAD) for h in range(NH)]
            sts = [lax.dot_general(kb[:, sl], qv[:, sl], _NT, preferred_element_type=F32) for sl in heads]
            alphas, pvs = [], []
            for h, sl in enumerate(heads):
                st = jnp.where(_chunk_mask_t(T), sts[h], NEG) if masked else sts[h]
                m_prev = m_s[h]
                m_new = jnp.maximum(m_prev, jnp.max(st, axis=0, keepdims=True))
                alphas.append(jnp.exp2(m_prev[0:1] - m_new[0:1]))
                pt = jnp.exp2(st - m_new[0:1]).astype(BF16)
                m_s[h] = m_new
                pvs.append(jnp.dot(vt[sl, :], pt, preferred_element_type=F32))
            for h in range(NH):
                acc_s[h] = acc_s[h] * alphas[h] + pvs[h]

        def loop_body(j, carry):
            step(j, False)
            return carry

        lax.fori_loop(0, i, loop_body, 0)
        step(i, True)
        for g in range(NH // 2):
            outs = []
            for h in (2 * g, 2 * g + 1):
                acc = acc_s[h]
                l_row = acc[V_HEAD:V_HEAD + 1, :]
                outs.append(acc[0:V_HEAD, :] / l_row)
                lse_ref[0, h * SUBLANES:(h + 1) * SUBLANES, :] = m_s[h] + jnp.log2(l_row)
            o_ref[:, g * LANES:(g + 1) * LANES] = jnp.concatenate(outs, axis=0).T

    return pl.pallas_call(
        body, name=name, grid=(HEADS // NH, nq),
        out_shape=(jax.ShapeDtypeStruct((S, HEADS * V_HEAD), F32),
                   jax.ShapeDtypeStruct((nq, HEADS * SUBLANES, T), F32)),
        in_specs=[pl.BlockSpec((T, PW), lambda p, i: (i, p)),
                  pl.BlockSpec((S, PW), lambda p, i: (0, p)),
                  pl.BlockSpec((nq, PW, T), lambda p, i: (0, p, 0))],
        out_specs=(pl.BlockSpec((T, NH * V_HEAD), lambda p, i: (i, p)),
                   pl.BlockSpec((1, NH * SUBLANES, T), lambda p, i: (i, p, 0))),
        scratch_shapes=[pltpu.VMEM((NH, SUBLANES, T), F32), pltpu.VMEM((NH, HEAD_PAD, T), F32)],
        compiler_params=_cparams(),
    )(q, k, vT3)


def _attn_bwd_t(q, k, v, kT3, do, lse3, dl3, name):
    S = q.shape[0]
    T = _attn_tile(S)
    nq = S // T
    NH = ATTN_BWD_HEADS
    PW = NH * HEAD_PAD
    VW = NH * V_HEAD

    def body(q_ref, k_ref, v_ref, kt_ref, do_ref, lse_ref, dl_ref, dq_ref, dk_ref, dv_ref, dk_s, dv_s):
        j = pl.program_id(1)
        left = lax.broadcasted_iota(jnp.int32, (T, LANES), 1) < V_HEAD

        @pl.when(j == 0)
        def _():
            dq_ref[...] = jnp.zeros_like(dq_ref)

        dk_s[...] = jnp.zeros_like(dk_s)
        dv_s[...] = jnp.zeros_like(dv_s)
        kb = k_ref[...]
        vms = []
        for g in range(NH // 2):
            vb = v_ref[:, g * LANES:(g + 1) * LANES]
            vms += [jnp.where(left, vb, jnp.zeros_like(vb)), jnp.where(left, jnp.zeros_like(vb), vb)]
        kt = kt_ref[0]

        def step(i, masked):
            r0 = pl.multiple_of(i * T, T)
            qb = q_ref[pl.ds(r0, T), :]
            do_all = do_ref[pl.ds(r0, T), :]
            lse = lse_ref[i]
            dl = dl_ref[i]
            heads = [slice(h * HEAD_PAD, (h + 1) * HEAD_PAD) for h in range(NH)]
            dobs = [do_all[:, (h // 2) * LANES:(h // 2 + 1) * LANES] for h in range(NH)]
            sts = [lax.dot_general(kb[:, sl], qb[:, sl], _NT, preferred_element_type=F32) for sl in heads]
            dpts = [lax.dot_general(vms[h], dobs[h], _NT, preferred_element_type=F32) for h in range(NH)]
            res = []
            for h, sl in enumerate(heads):
                r8 = h * SUBLANES
                pt = jnp.exp2(sts[h] - lse[r8:r8 + 1, :])
                if masked:
                    pt = jnp.where(_chunk_mask_t(T), pt, 0.0)
                dst = (pt * (dpts[h] - dl[r8:r8 + 1, :])).astype(BF16)
                res.append((jnp.dot(pt.astype(BF16), dobs[h], preferred_element_type=F32),
                            jnp.dot(dst, qb[:, sl], preferred_element_type=F32),
                            jnp.dot(kt[sl, :], dst, preferred_element_type=F32)))
            for h, sl in enumerate(heads):
                dv_s[h] += res[h][0]
                dk_s[:, sl] += res[h][1]
                dq_ref[i, sl, :] += res[h][2]

        def loop_body(i, carry):
            step(i, False)
            return carry

        step(j, True)
        lax.fori_loop(j + 1, nq, loop_body, 0)
        dk_ref[...] = dk_s[...] * (1.0 / LOG2E)
        for g in range(NH // 2):
            dv_ref[:, g * LANES:(g + 1) * LANES] = jnp.where(left, dv_s[2 * g], dv_s[2 * g + 1])

    return pl.pallas_call(
        body, name=name, grid=(HEADS // NH, nq),
        out_shape=(jax.ShapeDtypeStruct((nq, HEADS * HEAD_PAD, T), F32),
                   jax.ShapeDtypeStruct((S, HEADS * HEAD_PAD), F32), jax.ShapeDtypeStruct((S, HEADS * V_HEAD), F32)),
        in_specs=[pl.BlockSpec((S, PW), lambda p, j: (0, p)),
                  pl.BlockSpec((T, PW), lambda p, j: (j, p)),
                  pl.BlockSpec((T, VW), lambda p, j: (j, p)),
                  pl.BlockSpec((1, PW, T), lambda p, j: (j, p, 0)),
                  pl.BlockSpec((S, VW), lambda p, j: (0, p)),
                  pl.BlockSpec((nq, NH * SUBLANES, T), lambda p, j: (0, p, 0)),
                  pl.BlockSpec((nq, NH * SUBLANES, T), lambda p, j: (0, p, 0))],
        out_specs=(pl.BlockSpec((nq, PW, T), lambda p, j: (0, p, 0)),
                   pl.BlockSpec((T, PW), lambda p, j: (j, p)),
                   pl.BlockSpec((T, VW), lambda p, j: (j, p))),
        scratch_shapes=[pltpu.VMEM((T, PW), F32), pltpu.VMEM((NH, T, LANES), F32)],
        compiler_params=_cparams(),
    )(q, k, v, kT3, do, lse3, dl3)


def _even_post(z, cv, o, name):
    S = z.shape[0]
    T = _wide_tile(S)

    def body(ab_ref, ag_ref, bg_ref, cv_ref, o_ref, y_ref):
        y_ref[:, 0:SCW] = (_f32(ab_ref) * cv_ref[...] * _silu(_f32(ag_ref))).astype(BF16)
        y_ref[:, SCW:2 * SCW] = (o_ref[...] * _silu(_f32(bg_ref))).astype(BF16)

    return pl.pallas_call(
        body, name=name, grid=(S // T,),
        out_shape=jax.ShapeDtypeStruct((S, 2 * SCW), BF16),
        in_specs=[_rows(T, SCW, 0), _rows(T, SCW, 3), _rows(T, SCW, 4), _rows(T, SCW), _rows(T, SCW)],
        out_specs=_rows(T, 2 * SCW), compiler_params=_cparams(),
    )(z, z, z, cv, o)


def _even_bwd_gates(dyc, z, cv, o, name):
    S = z.shape[0]
    T = _row_tile(S)

    def body(dya_ref, dyb_ref, ab_ref, ag_ref, bg_ref, cv_ref, o_ref,
             dab_ref, dag_ref, dbg_ref, dcv_ref, do_ref, dl_ref):
        dya, ab, ag, cv = dya_ref[...], _f32(ab_ref), _f32(ag_ref), cv_ref[...]
        sg = _silu(ag)
        dab_ref[...] = (dya * cv * sg).astype(BF16)
        dcv_ref[...] = dya * ab * sg
        dag_ref[...] = (dya * ab * cv * _dsilu(ag)).astype(BF16)
        dyb, bg, ov = dyb_ref[...], _f32(bg_ref), o_ref[...]
        dov = dyb * _silu(bg)
        do_ref[...] = dov.astype(BF16)
        dbg_ref[...] = (dyb * ov * _dsilu(bg)).astype(BF16)
        prod = dov * ov
        left = lax.broadcasted_iota(jnp.int32, (T, LANES), 1) < V_HEAD
        for p in range(HEADS // 2):
            blk = prod[:, p * LANES:(p + 1) * LANES]
            s0 = jnp.sum(jnp.where(left, blk, 0.0), axis=1, keepdims=True)
            s1 = jnp.sum(jnp.where(left, 0.0, blk), axis=1, keepdims=True)
            dt = jnp.where(left, s0, s1).T
            dl_ref[0, 2 * p * SUBLANES:(2 * p + 1) * SUBLANES, :] = dt[0:SUBLANES, :]
            dl_ref[0, (2 * p + 1) * SUBLANES:(2 * p + 2) * SUBLANES, :] = dt[V_HEAD:V_HEAD + SUBLANES, :]

    assert T == _attn_tile(S)
    bf = jax.ShapeDtypeStruct((S, SCW), BF16)
    ff = jax.ShapeDtypeStruct((S, SCW), F32)
    return pl.pallas_call(
        body, name=name, grid=(S // T,),
        out_shape=(bf, bf, bf, ff, bf, jax.ShapeDtypeStruct((S // T, HEADS * SUBLANES, T), F32)),
        in_specs=[_rows(T, SCW, 0), _rows(T, SCW, 1), _rows(T, SCW, 0), _rows(T, SCW, 3), _rows(T, SCW, 4),
                  _rows(T, SCW), _rows(T, SCW)],
        out_specs=(_rows(T, SCW),) * 5 + (pl.BlockSpec((1, HEADS * SUBLANES, T), lambda i: (i, 0, 0)),),
        compiler_params=_cparams(),
    )(dyc, dyc, z, z, z, cv, o)


def _qkv_bwd(dq, dk, dv, z, tabs, w_q, w_kv, qg, kvg, name):
    S = dk.shape[0]
    T = _attn_tile(S)
    HW = HEADS * HEAD_PAD
    VW = HEADS * V_HEAD
    scale = 1.0 / math.sqrt(QK_NOPE + QK_ROPE)

    def fn(ins, outs):
        dq_ref, dk_ref, dv_ref, cq_ref, ckv_ref, ct_ref, ut_ref, dt_ref, wq_ref, wkv_ref, qg_ref, kvg_ref = ins
        dqp_ref, dkvp_ref, dcq_ref, dckv_ref, dkr_ref = outs
        ct, ut, dt = ct_ref[...], ut_ref[...], dt_ref[...]
        dkr = jnp.zeros((T, HEAD_PAD), F32)
        for h in range(HEADS):
            sl = slice(h * HEAD_PAD, (h + 1) * HEAD_PAD)
            dqp_ref[:, sl] = (_rope_t(dq_ref[0, sl, :].T, ct, ut, dt) * scale).astype(BF16)
            dkh = _rope_t(dk_ref[:, sl], ct, ut, dt)
            dkr = dkr + dkh
            dkvp_ref[:, sl] = dkh.astype(BF16)
        dkvp_ref[:, HW:] = dv_ref[...].astype(BF16)
        dkr_ref[...] = dkr.astype(BF16)
        sums = []
        for lat_ref, g_ref, dpre_ref, w_ref, dlat_ref in ((cq_ref, qg_ref, dqp_ref, wq_ref, dcq_ref),
                                                         (ckv_ref, kvg_ref, dkvp_ref, wkv_ref, dckv_ref)):
            dn = lax.dot_general(dpre_ref[...], w_ref[...], _NT, preferred_element_type=F32)
            xv = _f32(lat_ref)
            rstd = lax.rsqrt(jnp.mean(xv * xv, axis=-1, keepdims=True) + EPS)
            xh = xv * rstd
            dxh = dn * g_ref[...]
            dlat_ref[...] = (rstd * (dxh - xh * jnp.mean(dxh * xh, axis=-1, keepdims=True))).astype(BF16)
            sums.append(dn * xh)
        return sums

    return _col_sums(
        2, fn, [dq, dk, dv, z, z, *tabs, w_q, w_kv, qg, kvg],
        [pl.BlockSpec((1, HW, T), lambda i: (i, 0, 0)), _rows(T, HW), _rows(T, VW), _rows(T, QL, 10), _rows(T, KVL, 22),
         _rows(T, HEAD_PAD), _rows(T, HEAD_PAD), _rows(T, HEAD_PAD),
         _const(w_q.shape), _const(w_kv.shape), _const((1, QL)), _const((1, KVL))],
        [jax.ShapeDtypeStruct((S, HW), BF16), jax.ShapeDtypeStruct((S, HW + VW), BF16),
         jax.ShapeDtypeStruct((S, QL), BF16), jax.ShapeDtypeStruct((S, KVL), BF16),
         jax.ShapeDtypeStruct((S, HEAD_PAD), BF16)],
        [_rows(T, HW), _rows(T, HW + VW), _rows(T, QL), _rows(T, KVL), _rows(T, HEAD_PAD)],
        S, T, [QL, KVL], name)


def _even_dz(dab, du, z, dag, dbg, dcq, dckv, dkr, name):
    S = z.shape[0]
    T = _wide_tile(S)

    def body(dab_ref, du_ref, ac_ref, ax_ref, dag_ref, dbg_ref, dcq_ref, dckv_ref, dkr_ref, dz_ref):
        duv = du_ref[...]
        dz_ref[:, 0:SCW] = dab_ref[...]
        dz_ref[:, SCW:2 * SCW] = (duv * _f32(ax_ref)).astype(BF16)
        dz_ref[:, 2 * SCW:3 * SCW] = (duv * _f32(ac_ref)).astype(BF16)
        dz_ref[:, 3 * SCW:4 * SCW] = dag_ref[...]
        dz_ref[:, 4 * SCW:5 * SCW] = dbg_ref[...]
        dz_ref[:, 5 * SCW:5 * SCW + QL] = dcq_ref[...]
        dz_ref[:, 5 * SCW + QL:5 * SCW + QL + KVL] = dckv_ref[...]
        dz_ref[:, 5 * SCW + QL + KVL:ZE] = dkr_ref[...]

    return pl.pallas_call(
        body, name=name, grid=(S // T,),
        out_shape=jax.ShapeDtypeStruct((S, ZE), BF16),
        in_specs=[_rows(T, SCW), _rows(T, SCW), _rows(T, SCW, 1), _rows(T, SCW, 2), _rows(T, SCW), _rows(T, SCW),
                  _rows(T, QL), _rows(T, KVL), _rows(T, HEAD_PAD)],
        out_specs=_rows(T, ZE), compiler_params=_cparams(),
    )(dab, du, z, z, dag, dbg, dcq, dckv, dkr)


def _layer_norm_stats(cv):
    mu = jnp.mean(cv, axis=-1, keepdims=True)
    cen = cv - mu
    rstd = lax.rsqrt(jnp.mean(cen * cen, axis=-1, keepdims=True) + EPS)
    return cen * rstd, rstd


def _odd_post(cv, z, ln_g, ln_b, name):
    S, D = cv.shape
    T = _wide_tile(S)

    def body(cv_ref, sg_ref, g_ref, b_ref, y_ref):
        cvh, _ = _layer_norm_stats(cv_ref[...])
        y_ref[...] = (_silu(cvh * g_ref[...] + b_ref[...]) * _silu(_f32(sg_ref))).astype(BF16)

    return pl.pallas_call(
        body, name=name, grid=(S // T,),
        out_shape=jax.ShapeDtypeStruct((S, D), BF16),
        in_specs=[_rows(T, D), _rows(T, D, 2), _const((1, D)), _const((1, D))],
        out_specs=_rows(T, D), compiler_params=_cparams(),
    )(cv, z, ln_g, ln_b)


def _odd_bwd_norm(dyi, cv, z, ln_g, ln_b, name):
    S, D = cv.shape
    T = _row_tile(S)

    def fn(ins, outs):
        dy_ref, cv_ref, sg_ref, g_ref, b_ref = ins
        dcv_ref, dsg_ref = outs
        cvh, rstd = _layer_norm_stats(cv_ref[...])
        ln = cvh * g_ref[...] + b_ref[...]
        sgv, dy = _f32(sg_ref), dy_ref[...]
        dsg_ref[...] = (dy * _silu(ln) * _dsilu(sgv)).astype(BF16)
        dln = dy * _silu(sgv) * _dsilu(ln)
        dh = dln * g_ref[...]
        dcv_ref[...] = rstd * (dh - jnp.mean(dh, axis=-1, keepdims=True)
                               - cvh * jnp.mean(dh * cvh, axis=-1, keepdims=True))
        return [dln * cvh, dln]

    return _col_sums(2, fn, [dyi, cv, z, ln_g, ln_b],
                     [_rows(T, D), _rows(T, D), _rows(T, D, 2), _const((1, D)), _const((1, D))],
                     [jax.ShapeDtypeStruct((S, D), F32), jax.ShapeDtypeStruct((S, D), BF16)],
                     [_rows(T, D), _rows(T, D)], S, T, [D, D], name)


def _odd_dz(du, z, dsg, name):
    S, D = du.shape
    T = _wide_tile(S)

    def body(du_ref, val_ref, glu_ref, dsg_ref, dz_ref):
        duv = du_ref[...]
        sig = _sigmoid(_f32(glu_ref))
        dz_ref[:, 0:D] = (duv * sig).astype(BF16)
        dz_ref[:, D:2 * D] = (duv * _f32(val_ref) * sig * (1.0 - sig)).astype(BF16)
        dz_ref[:, 2 * D:3 * D] = dsg_ref[...]

    return pl.pallas_call(
        body, name=name, grid=(S // T,),
        out_shape=jax.ShapeDtypeStruct((S, 3 * D), BF16),
        in_specs=[_rows(T, D), _rows(T, D, 0), _rows(T, D, 1), _rows(T, D)],
        out_specs=_rows(T, 3 * D), compiler_params=_cparams(),
    )(du, z, z, dsg)


ADAM_BLOCK_ELEMS = 128 * 1024


def _adam_tiles(R, C):
    if R * C <= ADAM_BLOCK_ELEMS:
        return R, C
    tr = R
    for cand in range(SUBLANES, R, SUBLANES):
        if R % cand == 0 and cand * C <= ADAM_BLOCK_ELEMS:
            tr = cand
    if tr < R:
        return tr, C
    tc = C
    for cand in range(LANES, C, LANES):
        if C % cand == 0 and R * cand <= ADAM_BLOCK_ELEMS:
            tc = cand
    return R, tc


def _adamw(g_parts, w, m, v, name):
    if not isinstance(g_parts, (list, tuple)):
        g_parts = [g_parts]
    ng = len(g_parts)
    _, R, C = g_parts[0].shape
    tr, tc = _adam_tiles(R, C)

    def body(*refs):
        g_refs = refs[:ng]
        w_ref, m_ref, v_ref, go_ref, d_ref, mo_ref, vo_ref = refs[ng:]
        g = None
        for g_ref in g_refs:
            for p in range(g_ref.shape[0]):
                part = g_ref[p].astype(F32)
                g = part if g is None else g + part
        mn = ADAM_B1 * m_ref[...] + (1.0 - ADAM_B1) * g
        vn = ADAM_B2 * v_ref[...] + (1.0 - ADAM_B2) * (g * g)
        m_hat = mn / (1.0 - ADAM_B1 ** ADAM_STEP)
        v_hat = vn / (1.0 - ADAM_B2 ** ADAM_STEP)
        go_ref[...] = g
        d_ref[...] = -ADAM_LR * (m_hat / (jnp.sqrt(v_hat) + ADAM_EPS) + ADAM_WD * w_ref[...])
        mo_ref[...] = mn
        vo_ref[...] = vn

    slab = jax.ShapeDtypeStruct((R, C), F32)
    blk = pl.BlockSpec((tr, tc), lambda i, j: (i, j))
    return pl.pallas_call(
        body, name=name, grid=(R // tr, C // tc),
        out_shape=(slab,) * 4,
        in_specs=[pl.BlockSpec((g.shape[0], tr, tc), lambda i, j: (0, i, j)) for g in g_parts] + [blk, blk, blk],
        out_specs=(blk,) * 4, compiler_params=_cparams(),
    )(*g_parts, w, m, v)


def _adamw_slab(g_parts, w, m, v, layer, prev, name):
    ng = len(g_parts)
    NL, R, C = w.shape
    tr, tc = _adam_tiles(R, C)

    def body(*refs):
        g_refs = refs[:ng]
        w_ref, m_ref, v_ref = refs[ng:ng + 3]
        go_ref, d_ref, mo_ref, vo_ref = refs[-4:]
        g = None
        for g_ref in g_refs:
            for p in range(g_ref.shape[0]):
                part = g_ref[p].astype(F32)
                g = part if g is None else g + part
        mn = ADAM_B1 * m_ref[0] + (1.0 - ADAM_B1) * g
        vn = ADAM_B2 * v_ref[0] + (1.0 - ADAM_B2) * (g * g)
        m_hat = mn / (1.0 - ADAM_B1 ** ADAM_STEP)
        v_hat = vn / (1.0 - ADAM_B2 ** ADAM_STEP)
        go_ref[0] = g
        d_ref[0] = -ADAM_LR * (m_hat / (jnp.sqrt(v_hat) + ADAM_EPS) + ADAM_WD * w_ref[0])
        mo_ref[0] = mn
        vo_ref[0] = vn

    blk = pl.BlockSpec((1, tr, tc), lambda i, j: (layer, i, j))
    n_in = ng + 3
    prev = list(prev) if prev is not None else []
    return pl.pallas_call(
        body, name=name, grid=(R // tr, C // tc),
        out_shape=(jax.ShapeDtypeStruct((NL, R, C), F32),) * 4,
        in_specs=([pl.BlockSpec((g.shape[0], tr, tc), lambda i, j: (0, i, j)) for g in g_parts] + [blk, blk, blk]
                  + [pl.BlockSpec(memory_space=pl.ANY)] * len(prev)),
        out_specs=(blk,) * 4,
        input_output_aliases={n_in + k: k for k in range(len(prev))},
        compiler_params=_cparams(),
    )(*g_parts, w, m, v, *prev)


def _gather_cols(g, shape):
    nd = len(shape)
    t = jnp.moveaxis(g, 0, nd - 1)
    return t.reshape(tuple(shape[:-1]) + (N_DEV * shape[-1],))


def _scatter_cols(full, n):
    t = full.reshape(full.shape[:-1] + (N_DEV, n))
    return jnp.moveaxis(t, -2, 0)


def kernel(x, c, positions, ada_w, ada_b, pre_norm_g, post_norm_g, even_w_in, even_sc_conv_w, even_sc_conv_b, even_q_norm_g, even_kv_norm_g, even_w_uq, even_w_ukv, even_w_out, odd_w_in, odd_conv_w, odd_conv_b, odd_ln_g, odd_ln_b, odd_w_out, loss_target, m_ada_w, m_ada_b, m_pre_norm_g, m_post_norm_g, m_even_w_in, m_even_sc_conv_w, m_even_sc_conv_b, m_even_q_norm_g, m_even_kv_norm_g, m_even_w_uq, m_even_w_ukv, m_even_w_out, m_odd_w_in, m_odd_conv_w, m_odd_conv_b, m_odd_ln_g, m_odd_ln_b, m_odd_w_out, v_ada_w, v_ada_b, v_pre_norm_g, v_post_norm_g, v_even_w_in, v_even_sc_conv_w, v_even_sc_conv_b, v_even_q_norm_g, v_even_kv_norm_g, v_even_w_uq, v_even_w_ukv, v_even_w_out, v_odd_w_in, v_odd_conv_w, v_odd_conv_b, v_odd_ln_g, v_odd_ln_b, v_odd_w_out):
    S, D = x.shape[1], x.shape[2]
    L = ada_w.shape[0]
    NE, NO = even_w_in.shape[0], odd_w_in.shape[0]
    me = 4 * lax.axis_index("x") + 2 * lax.axis_index("y") + lax.axis_index("c")
    x0 = x[0]
    target = loss_target[0]

    small_parts = [c, even_sc_conv_w, odd_conv_w, odd_conv_b, odd_ln_g, odd_ln_b]
    small_shapes = [p.shape for p in small_parts]
    sg = _exchange([_pack(small_parts, F32, SUBLANES)], False, "gather_small")[0].reshape(N_DEV, -1)
    c_all, scw_g, ocw_g, ocb_g, olg_g, olb_g = _unpack(sg, small_shapes)
    c_all = c_all.reshape(N_DEV, D)
    sc_conv_w = _gather_cols(scw_g, even_sc_conv_w.shape)
    o_conv_w = _gather_cols(ocw_g, odd_conv_w.shape)
    o_conv_b = _gather_cols(ocb_g, odd_conv_b.shape)
    o_ln_g = _gather_cols(olg_g, odd_ln_g.shape)
    o_ln_b = _gather_cols(olb_g, odd_ln_b.shape)

    pad_q = HEAD_PAD - QK_NOPE - QK_ROPE
    w_local = [jnp.swapaxes(even_w_in, 1, 2).astype(BF16),
               jnp.pad(even_w_uq, ((0, 0), (0, 0), (0, pad_q))).astype(BF16),
               jnp.pad(even_w_ukv[..., :QK_NOPE], ((0, 0), (0, 0), (0, HEAD_PAD - QK_NOPE))).astype(BF16),
               even_w_ukv[..., QK_NOPE:].astype(BF16),
               even_w_out.astype(BF16), odd_w_in.astype(BF16), odd_w_out.astype(BF16)]
    n_ada = ada_w.shape[2]
    ada_b_cols = lax.dynamic_slice_in_dim(ada_b, me * n_ada, n_ada, axis=1).reshape(L, 1, n_ada)
    mod_slab = _ada_fwd(c_all, ada_w, ada_b_cols)
    mod_g = _exchange([_pack([mod_slab], F32, SUBLANES)], False, "gather_mod")[0].reshape(N_DEV, -1)
    mod_all = mod_g[:, :L * N_DEV * n_ada].reshape(N_DEV, L, N_DEV, n_ada)
    mod = lax.dynamic_index_in_dim(mod_all, me, axis=2, keepdims=False)
    mod = jnp.moveaxis(mod, 0, 1).reshape(L, 3 * D)
    shift, scale, gate = mod[:, :D], mod[:, D:2 * D], mod[:, 2 * D:]

    heads_to_cols = lambda g: jnp.moveaxis(g, 0, 1).reshape(g.shape[1], -1)
    w_handles = {}
    token = jnp.broadcast_to(jnp.minimum(jnp.abs(mod[0, 0]), 0.0), (SUBLANES, LANES))
    for layer in range(L):
        i = layer // 2
        groups = ({"in": [w_local[0][i]], "rest": [w[i] for w in w_local[1:5]]} if layer % 2 == 0
                  else {"all": [w[i] for w in w_local[5:]]})
        for key, mine in groups.items():
            mine = [w + token[0, 0].astype(BF16) for w in mine]
            w_handles[layer, key], token = _exchange_start(mine, False, f"gather_weights_start_l{layer}_{key}")
    w_token = token

    def arrived(layer, key, after):
        return _exchange_wait(w_handles[layer, key], False, after, f"gather_weights_wait_l{layer}_{key}")[1]

    e_w_in_k, e_w_q_k, e_w_kv_k, e_w_out, o_w_in, o_w_out = ([None] * NE, [None] * NE, [None] * NE, [None] * NE,
                                                             [None] * NO, [None] * NO)

    inv_freq = 1.0 / (ROPE_THETA ** (jnp.arange(0, QK_ROPE, 2, dtype=F32) / QK_ROPE))
    inv_lane = jnp.zeros((HEAD_PAD,), F32).at[QK_NOPE:QK_NOPE + QK_ROPE].set(jnp.concatenate([inv_freq, inv_freq]))
    tabs = _rope_tables(positions.astype(F32).reshape(S, 1), inv_lane.reshape(1, HEAD_PAD))

    row = lambda a: a.reshape(1, -1)
    scb = even_sc_conv_b
    KP3, KP31 = SUBLANES, 32

    saved = []
    xs = x0
    lv = jnp.stack([pre_norm_g, post_norm_g, scale, shift, gate], axis=1).reshape(L * 5, 1, D)
    PRE_G, POST_G, SCALE, SHIFT, GATE = range(5)
    vec = lambda layer, k: (lv, layer * 5 + k)
    h = _pre_norm(xs, vec(0, PRE_G), vec(0, SCALE), vec(0, SHIFT), "pre_norm_l0", after=w_token)
    for layer in range(L):
        i = layer // 2
        tag = f"l{layer}"
        first = [h, tabs[0]] if layer == 0 else h
        if layer % 2 == 0:
            wt = arrived(layer, "in", first)[0].reshape(-1, D)
            e_w_in_k[i] = jnp.concatenate([wt[:2048], wt[2464:2976], wt[2048:2432], jnp.zeros((QK_NOPE, D), BF16),
                                           wt[2432:2464], jnp.zeros((pad_q, D), BF16)], axis=0)
            z, u, qn, kvn = _w_in_even(h, e_w_in_k[i], row(even_q_norm_g[i]), row(even_kv_norm_g[i]), f"w_in_{tag}")
            eq_g, ek_g, ev_g, eout_g = arrived(layer, "rest", z)
            e_w_q_k[i] = heads_to_cols(eq_g)
            e_w_kv_k[i] = jnp.concatenate([heads_to_cols(ek_g), heads_to_cols(ev_g)], axis=-1)
            e_w_out[i] = eout_g.reshape(-1, D)
            cw = jnp.pad(sc_conv_w[i], ((0, KP3 - SC_KERNEL), (0, 0)))
            cv = _conv_fwd(u, cw, row(scb[i]), SC_KERNEL, f"conv_{tag}")
            q, k, v, kT3, vT3 = _qkv_fwd_t(qn, kvn, z, tabs, e_w_q_k[i], e_w_kv_k[i], f"qkv_{tag}")
            o, lse = _attn_fwd_t(q, k, vT3, f"attn_{tag}")
            ycat = _even_post(z, cv, o, f"even_post_{tag}")
            y = _matmul(ycat, e_w_out[i], "nn", F32, f"w_out_{tag}", tn=1024)
            saved.append(dict(x=xs, h=h, z=z, u=u, qn=qn, kvn=kvn, cw=cw, cv=cv, q=q, k=k, v=v, kT3=kT3, o=o, lse=lse,
                              ycat=ycat, y=y))
        else:
            owin_g, oout_g = arrived(layer, "all", first)
            o_w_in[i], o_w_out[i] = heads_to_cols(owin_g), oout_g.reshape(-1, D)
            z, u = _w_in_odd(h, o_w_in[i], f"w_in_{tag}")
            cw = jnp.pad(o_conv_w[i], ((0, KP31 - CONF_KERNEL), (0, 0)))
            cv = _conv_fwd(u, cw, row(o_conv_b[i]), CONF_KERNEL, f"conv_{tag}")
            yin = _odd_post(cv, z, row(o_ln_g[i]), row(o_ln_b[i]), f"odd_post_{tag}")
            y = _matmul(yin, o_w_out[i], "nn", F32, f"w_out_{tag}", tn=1024)
            saved.append(dict(x=xs, h=h, z=z, u=u, cw=cw, cv=cv, yin=yin, y=y))
        if layer < L - 1:
            xs, h = _post_pre_norm(xs, y, vec(layer, POST_G), vec(layer, GATE), vec(layer + 1, PRE_G),
                                   vec(layer + 1, SCALE), vec(layer + 1, SHIFT), f"post_pre_norm_{tag}")

    dx, dy, err_sq, dgate, g_post_last = _loss_post_norm_bwd(xs, y, vec(L - 1, POST_G), vec(L - 1, GATE), target,
                                                             "loss_post_norm_bwd")
    loss = lax.psum(_scaled_total(err_sq, 0.5 / D, "loss_total")[0, 0], MESH_AXES)

    g_pre, g_post, dmod = [None] * L, [None] * L, [None] * L
    g_e_w_in, g_e_w_uq, g_e_w_ukv, g_e_w_out = [None] * NE, [None] * NE, [None] * NE, [None] * NE
    g_scw, g_scb, g_qg, g_kvg = [None] * NE, [None] * NE, [None] * NE, [None] * NE
    g_o_w_in, g_o_w_out, g_ocw, g_ocb, g_olg, g_olb = ([None] * NO for _ in range(6))
    sm_w = [even_sc_conv_w, odd_conv_w, odd_conv_b, odd_ln_g, odd_ln_b]
    sm_rows = _pack(sm_w, F32, SUBLANES).shape[0]

    def small_slab():
        full = [_scatter_cols(jnp.stack(g_scw), even_sc_conv_w.shape[-1]),
                _scatter_cols(jnp.stack(g_ocw), odd_conv_w.shape[-1]),
                _scatter_cols(jnp.concatenate(g_ocb, 0), odd_conv_b.shape[-1]),
                _scatter_cols(jnp.concatenate(g_olg, 0), odd_ln_g.shape[-1]),
                _scatter_cols(jnp.concatenate(g_olb, 0), odd_ln_b.shape[-1])]
        flat = jnp.concatenate([g.reshape(N_DEV, -1) for g in full], axis=1)
        return jnp.pad(flat, ((0, 0), (0, sm_rows * PACK_COLS - flat.shape[1]))).reshape(N_DEV, sm_rows, PACK_COLS)

    scatters = []
    bw_token = jnp.zeros((SUBLANES, LANES), F32)

    def start_scatter(tag, names, parts):
        own = [lax.dynamic_slice_in_dim(g, me, 1, axis=0) for g in parts]
        handle, token = _exchange_start([g.astype(BF16) for g in parts], True, f"scatter_grads_start_{tag}")
        scatters.append((tag, names, handle, own))
        return token

    for layer in reversed(range(L)):
        i = layer // 2
        tag = f"l{layer}"
        sv = saved[layer]
        if layer == L - 1:
            g_post[layer] = g_post_last
        if layer % 2 == 0:
            dyc = _matmul(dy, e_w_out[i], "nt", F32, f"d_ycat_{tag}", tn=1024)
            g_e_w_out[i] = _matmul(sv["ycat"], dy, "tn", BF16, f"g_w_out_{tag}", tn=1024).reshape(N_DEV, -1, D)
            if layer == 0:
                bw_token = start_scatter("l0_out", [("even_w_out", i)], [g_e_w_out[i]])
            dab, dag, dbg, dcv, do, delta = _even_bwd_gates(dyc, sv["z"], sv["cv"], sv["o"], f"even_gates_bwd_{tag}")
            du, dcw, g_scb[i] = _conv_bwd(dcv, sv["u"], sv["cw"] + bw_token[0, 0], SC_KERNEL, f"conv_bwd_{tag}")
            g_scw[i] = dcw[:SC_KERNEL]
            dq, dk, dv = _attn_bwd_t(sv["q"], sv["k"], sv["v"], sv["kT3"], do, sv["lse"], delta, f"attn_bwd_{tag}")
            (dqp, dkvp, dcq, dckv, dkr, g_qg[i], g_kvg[i]) = _qkv_bwd(
                dq, dk, dv, sv["z"], tabs, e_w_q_k[i], e_w_kv_k[i],
                row(even_q_norm_g[i]), row(even_kv_norm_g[i]), f"qkv_bwd_{tag}")
            gq = _matmul(sv["qn"], dqp, "tn", BF16, f"g_w_uq_{tag}", tn=1024)
            gkv = _matmul(sv["kvn"], dkvp, "tn", BF16, f"g_w_ukv_{tag}")
            g_e_w_uq[i] = jnp.moveaxis(gq.reshape(QL, HEADS, HEAD_PAD)[..., :QK_NOPE + QK_ROPE], 1, 0)
            g_e_w_ukv[i] = jnp.moveaxis(jnp.concatenate(
                [gkv[:, :HEADS * HEAD_PAD].reshape(KVL, HEADS, HEAD_PAD)[..., :QK_NOPE],
                 gkv[:, HEADS * HEAD_PAD:].reshape(KVL, HEADS, V_HEAD)], axis=-1), 1, 0)
            dz = _even_dz(dab, du, sv["z"], dag, dbg, dcq, dckv, dkr, f"even_dz_{tag}")
            gt = _matmul(dz, sv["h"], "tn", BF16, f"g_w_in_{tag}", tm=1024, tn=1024)
            g_e_w_in[i] = jnp.concatenate([gt[:2048], gt[2560:2944], gt[2944 + QK_NOPE:2944 + QK_NOPE + QK_ROPE],
                                           gt[2048:2560]], axis=0).reshape(N_DEV, -1, D)
            names = [("even_w_in", i), ("even_w_uq", i), ("even_w_ukv", i)]
            parts = [g_e_w_in[i], g_e_w_uq[i], g_e_w_ukv[i]]
            if layer == 0:
                names, parts = names + [("small", 0)], parts + [small_slab()]
            else:
                names, parts = names + [("even_w_out", i)], parts + [g_e_w_out[i]]
            bw_token = start_scatter(tag, names, parts)
            w_dh = e_w_in_k[i] + bw_token[0, 0].astype(BF16) if layer == 0 else e_w_in_k[i]
            dh = _matmul(dz, w_dh, "nn", F32, f"d_h_{tag}", tn=1024)
        else:
            dyi = _matmul(dy, o_w_out[i], "nt", F32, f"d_yin_{tag}", tn=1024)
            g_o_w_out[i] = _matmul(sv["yin"], dy, "tn", BF16, f"g_w_out_{tag}", tn=1024).reshape(N_DEV, -1, D)
            dcv, dsg, g_olg[i], g_olb[i] = _odd_bwd_norm(dyi, sv["cv"], sv["z"], row(o_ln_g[i]), row(o_ln_b[i]),
                                                         f"odd_norm_bwd_{tag}")
            du, dcw, g_ocb[i] = _conv_bwd(dcv, sv["u"], sv["cw"], CONF_KERNEL, f"conv_bwd_{tag}")
            g_ocw[i] = dcw[:CONF_KERNEL]
            dz = _odd_dz(du, sv["z"], dsg, f"odd_dz_{tag}")
            g_o_w_in[i] = _matmul(sv["h"], dz, "tn", BF16, f"g_w_in_{tag}", tm=1024, tn=odd_w_in.shape[-1],
                                  split_n=True)
            bw_token = start_scatter(tag, [("odd_w_in", i), ("odd_w_out", i)], [g_o_w_in[i], g_o_w_out[i]])
            dh = _matmul(dz, o_w_in[i], "nt", F32, f"d_h_{tag}", tn=1024)
        if layer > 0:
            (dx, dy, dshift, dscale, g_pre[layer], dgate_prev, g_post[layer - 1]) = _pre_post_norm_bwd(
                dh, sv["x"], dx, vec(layer, PRE_G), vec(layer, SCALE), saved[layer - 1]["y"], vec(layer - 1, POST_G),
                vec(layer - 1, GATE), f"pre_post_norm_bwd_{tag}", after=bw_token)
        else:
            dx, dshift, dscale, g_pre[layer] = _pre_norm_bwd(dh, sv["x"], dx, vec(layer, PRE_G), vec(layer, SCALE),
                                                             f"pre_norm_bwd_{tag}", after=bw_token)
            dgate_prev = None
        dmod[layer] = jnp.concatenate([dshift, dscale, dgate], axis=-1)
        dgate = dgate_prev
    grad_x = dx.reshape(1, S, D)

    rep_g = [jnp.concatenate(dmod, 0), jnp.concatenate(g_pre, 0), jnp.concatenate(g_post, 0),
             jnp.stack(g_scb), jnp.stack(g_qg), jnp.stack(g_kvg)]
    rep_w = [ada_b, pre_norm_g, post_norm_g, even_sc_conv_b, even_q_norm_g, even_kv_norm_g]
    rep_m = [m_ada_b, m_pre_norm_g, m_post_norm_g, m_even_sc_conv_b, m_even_q_norm_g, m_even_kv_norm_g]
    rep_v = [v_ada_b, v_pre_norm_g, v_post_norm_g, v_even_sc_conv_b, v_even_q_norm_g, v_even_kv_norm_g]
    rep_shapes = [w.shape for w in rep_w]
    rep_all = _exchange([_pack(rep_g, F32, SUBLANES)], False, "gather_small_grads")[0]
    rep_out = _adamw(rep_all, _pack(rep_w, F32, SUBLANES), _pack(rep_m, F32, SUBLANES), _pack(rep_v, F32, SUBLANES),
                     "adamw_replicated")
    rep_res = [_unpack(o.reshape(-1), rep_shapes) for o in rep_out]

    dmod_all = rep_all.reshape(N_DEV, -1)[:, :L * 3 * D].reshape(N_DEV, L, 3 * D)
    dmod_cols = jnp.moveaxis(lax.dynamic_slice_in_dim(dmod_all, me * n_ada, n_ada, axis=2), 0, 1)
    g_ada_w = _ada_bwd(c_all.T, dmod_cols)
    ada_out = _adamw(g_ada_w.reshape(1, -1, n_ada), ada_w.reshape(-1, n_ada),
                     m_ada_w.reshape(-1, n_ada), v_ada_w.reshape(-1, n_ada), "adamw_ada_w")
    ada_res = [o.reshape(ada_w.shape) for o in ada_out]

    sm_m = [m_even_sc_conv_w, m_odd_conv_w, m_odd_conv_b, m_odd_ln_g, m_odd_ln_b]
    sm_v = [v_even_sc_conv_w, v_odd_conv_w, v_odd_conv_b, v_odd_ln_g, v_odd_ln_b]
    sm_shapes = [w.shape for w in sm_w]
    state = {"even_w_in": (even_w_in, m_even_w_in, v_even_w_in), "even_w_uq": (even_w_uq, m_even_w_uq, v_even_w_uq),
             "even_w_ukv": (even_w_ukv, m_even_w_ukv, v_even_w_ukv), "even_w_out": (even_w_out, m_even_w_out, v_even_w_out),
             "odd_w_in": (odd_w_in, m_odd_w_in, v_odd_w_in), "odd_w_out": (odd_w_out, m_odd_w_out, v_odd_w_out)}
    state["even_w_in"] = tuple(jnp.swapaxes(t, 1, 2) for t in state["even_w_in"])
    big_res = {name: None for name in state}
    after = [bw_token, grad_x, rep_out[0], ada_out[0]]
    sm_res = None
    for tag, names, handle, own in scatters:
        _, landed = _exchange_wait(handle, True, after, f"scatter_grads_wait_{tag}")
        after = []
        for a, (name, i) in enumerate(names):
            if name == "small":
                sm_out = _adamw([own[a], landed[a]], _pack(sm_w, F32, SUBLANES), _pack(sm_m, F32, SUBLANES),
                                _pack(sm_v, F32, SUBLANES), "adamw_small_sharded")
                sm_res = [_unpack(o.reshape(-1), sm_shapes) for o in sm_out]
                continue
            big_res[name] = _adamw_slab([own[a], landed[a]], *state[name], i, big_res[name], f"adamw_{name}_{i}")
            after += list(big_res[name])
    sh_res = [dict(zip(["even_sc_conv_w", "odd_conv_w", "odd_conv_b", "odd_ln_g", "odd_ln_b"], sm_res[kind]))
              for kind in range(4)]
    for name in state:
        for kind in range(4):
            res = big_res[name][kind]
            sh_res[kind][name] = jnp.swapaxes(res, 1, 2) if name == "even_w_in" else res

    order = ["ada_w", "ada_b", "pre_norm_g", "post_norm_g", "even_w_in", "even_sc_conv_w", "even_sc_conv_b",
             "even_q_norm_g", "even_kv_norm_g", "even_w_uq", "even_w_ukv", "even_w_out", "odd_w_in", "odd_conv_w",
             "odd_conv_b", "odd_ln_g", "odd_ln_b", "odd_w_out"]
    rep_names = ["ada_b", "pre_norm_g", "post_norm_g", "even_sc_conv_b", "even_q_norm_g", "even_kv_norm_g"]
    outs = [loss, grad_x]
    for kind in range(4):
        for name in order:
            if name == "ada_w":
                outs.append(ada_res[kind])
            elif name in rep_names:
                outs.append(rep_res[kind][rep_names.index(name)])
            else:
                outs.append(sh_res[kind][name])
    return tuple(outs)
```

```python
import math

import jax
import jax.numpy as jnp
from jax import lax
from jax.experimental import pallas as pl
from jax.experimental.pallas import tpu as pltpu

F32 = jnp.float32
BF16 = jnp.bfloat16
MESH_AXES = ("x", "y", "c")
N_DEV = 8
EPS = 1e-6
CHUNK = 64
HEADS = 8
QK_NOPE = 64
QK_ROPE = 32
V_HEAD = 64
HEAD_PAD = 128
ROPE_THETA = 10000.0
SC_KERNEL = 3
CONF_KERNEL = 31
LANES = 128
SUBLANES = 8
PACK_COLS = 1024
VMEM_LIMIT = 48 * 1024 * 1024
NEG = -1e30

ADAM_LR = 0.001
ADAM_B1 = 0.9
ADAM_B2 = 0.999
ADAM_EPS = 1e-08
ADAM_WD = 0.01
ADAM_STEP = 10


def _cparams():
    return pltpu.CompilerParams(vmem_limit_bytes=VMEM_LIMIT)


def _sigmoid(x):
    return 1.0 / (1.0 + jnp.exp(-x))


def _f32(ref):
    return ref[...].astype(F32)


def _silu(x):
    return x * _sigmoid(x)


def _dsilu(x):
    s = _sigmoid(x)
    return s * (1.0 + x * (1.0 - s))


def _rows(T, width, cb=0):
    return pl.BlockSpec((T, width), lambda i: (i, cb))


def _const(shape):
    nd = len(shape)
    return pl.BlockSpec(shape, lambda i: (0,) * nd)


def _wide_tile(S):
    return min(512, S)


def _row_tile(S):
    return min(256, S)


def _exchange(srcs, scatter, name):
    n = len(srcs)
    shapes = [tuple(s.shape[1:]) if scatter else tuple(s.shape) for s in srcs]

    def body(*refs):
        src_refs, out_refs = refs[:n], refs[n:2 * n]
        send_sems, recv_sems, local_sems = refs[2 * n:]
        x, y, c = lax.axis_index("x"), lax.axis_index("y"), lax.axis_index("c")
        me = 4 * x + 2 * y + c
        owns, copies = [], []
        for a in range(n):
            def piece(d, a=a):
                return src_refs[a].at[d] if scatter else src_refs[a]

            own = pltpu.make_async_copy(piece(me), out_refs[a].at[me], local_sems.at[a])
            own.start()
            owns.append(own)
            for k in range(1, N_DEV):
                px, py, pc = x ^ ((k >> 2) & 1), y ^ ((k >> 1) & 1), c ^ (k & 1)
                peer = 4 * px + 2 * py + pc
                sem = a * (N_DEV - 1) + k - 1
                cp = pltpu.make_async_remote_copy(
                    src_ref=piece(peer), dst_ref=out_refs[a].at[me],
                    send_sem=send_sems.at[sem], recv_sem=recv_sems.at[sem],
                    device_id=(px, py, pc), device_id_type=pl.DeviceIdType.MESH)
                cp.start()
                arrival = pltpu.make_async_remote_copy(
                    src_ref=piece(peer), dst_ref=out_refs[a].at[peer],
                    send_sem=send_sems.at[sem], recv_sem=recv_sems.at[sem],
                    device_id=(x, y, c), device_id_type=pl.DeviceIdType.MESH)
                copies.append((cp, arrival))
        for _, arrival in copies:
            arrival.wait_recv()
        for cp, _ in copies:
            cp.wait_send()
        for own in owns:
            own.wait()

    return pl.pallas_call(
        body, name=name,
        out_shape=tuple(jax.ShapeDtypeStruct((N_DEV,) + shp, s.dtype) for shp, s in zip(shapes, srcs)),
        in_specs=[pl.BlockSpec(memory_space=pl.ANY)] * n,
        out_specs=tuple(pl.BlockSpec(memory_space=pl.ANY) for _ in range(n)),
        scratch_shapes=[pltpu.SemaphoreType.DMA((n * (N_DEV - 1),)),
                        pltpu.SemaphoreType.DMA((n * (N_DEV - 1),)),
                        pltpu.SemaphoreType.DMA((n,))],
    )(*srcs)


_HBM = pl.BlockSpec(memory_space=pltpu.HBM)
_SEM = pl.BlockSpec(memory_space=pltpu.SEMAPHORE)


def _peer(k):
    x, y, c = lax.axis_index("x"), lax.axis_index("y"), lax.axis_index("c")
    return x ^ ((k >> 2) & 1), y ^ ((k >> 1) & 1), c ^ (k & 1)


def _exchange_start(srcs, scatter, name):
    n = len(srcs)
    shapes = [tuple(s.shape[1:]) if scatter else tuple(s.shape) for s in srcs]
    slots = N_DEV - 1 if scatter else N_DEV
    lands = [lax.empty((slots,) + shp, s.dtype) for shp, s in zip(shapes, srcs)]
    if not scatter:
        here = 4 * lax.axis_index("x") + 2 * lax.axis_index("y") + lax.axis_index("c")
        lands = [lax.dynamic_update_index_in_dim(l, s, here, 0) for l, s in zip(lands, srcs)]

    def body(*refs):
        src_refs, land_refs = refs[:n], refs[n:2 * n]
        send_sems, recv_sems = refs[2 * n], refs[2 * n + 1]
        token = refs[4 * n + 2]
        me = 4 * lax.axis_index("x") + 2 * lax.axis_index("y") + lax.axis_index("c")
        for a in range(n):
            for k in range(1, N_DEV):
                px, py, pc = _peer(k)
                peer = 4 * px + 2 * py + pc
                pltpu.make_async_remote_copy(
                    src_ref=src_refs[a].at[peer] if scatter else src_refs[a],
                    dst_ref=land_refs[a].at[k - 1] if scatter else land_refs[a].at[me],
                    send_sem=send_sems.at[a * (N_DEV - 1) + k - 1], recv_sem=recv_sems.at[a * (N_DEV - 1) + k - 1],
                    device_id=(px, py, pc), device_id_type=pl.DeviceIdType.MESH).start()
        token[...] = jnp.zeros_like(token)

    hbm = lambda arrs: [pltpu.HBM(a.shape, a.dtype) for a in arrs]
    out = pl.pallas_call(
        body, name=name,
        out_shape=(pltpu.SemaphoreType.DMA((n * (N_DEV - 1),)), pltpu.SemaphoreType.DMA((n * (N_DEV - 1),)),
                   *hbm(srcs), *hbm(lands), jax.ShapeDtypeStruct((SUBLANES, LANES), F32)),
        in_specs=[_HBM] * (2 * n),
        out_specs=(_SEM, _SEM, *([_HBM] * (2 * n)), pl.BlockSpec(memory_space=pltpu.VMEM)),
        input_output_aliases={a: 2 + a for a in range(2 * n)},
        compiler_params=pltpu.CompilerParams(has_side_effects=pltpu.SideEffectType.DATAFLOW_SIDE_EFFECTING),
    )(*[pltpu.with_memory_space_constraint(s, pltpu.HBM) for s in srcs],
      *[pltpu.with_memory_space_constraint(l, pltpu.HBM) for l in lands])
    return (out[0], out[1], list(out[2:2 + n]), list(out[2 + n:2 + 2 * n])), out[2 + 2 * n]


def _exchange_wait(handle, scatter, after, name):
    send_sems, recv_sems, srcs, lands = handle
    n = len(srcs)
    after = list(after) if isinstance(after, (list, tuple)) else [after]

    def body(*refs):
        src_refs, land_refs = refs[:n], refs[n:2 * n]
        send_sems, recv_sems = refs[2 * n], refs[2 * n + 1]
        for a in range(n):
            for k in range(1, N_DEV):
                px, py, pc = _peer(k)
                peer = 4 * px + 2 * py + pc
                cp = pltpu.make_async_remote_copy(
                    src_ref=src_refs[a].at[peer] if scatter else src_refs[a],
                    dst_ref=land_refs[a].at[k - 1] if scatter else land_refs[a].at[peer],
                    send_sem=send_sems.at[a * (N_DEV - 1) + k - 1], recv_sem=recv_sems.at[a * (N_DEV - 1) + k - 1],
                    device_id=(px, py, pc), device_id_type=pl.DeviceIdType.MESH)
                cp.wait_send()
                cp.wait_recv()

    out = pl.pallas_call(
        body, name=name,
        out_shape=tuple(pltpu.HBM(a.shape, a.dtype) for a in srcs + lands),
        in_specs=[_HBM] * (2 * n) + [_SEM, _SEM] + [pl.BlockSpec(memory_space=pl.ANY)] * len(after),
        out_specs=tuple([_HBM] * (2 * n)),
        input_output_aliases={a: a for a in range(2 * n)},
        compiler_params=pltpu.CompilerParams(has_side_effects=pltpu.SideEffectType.DATAFLOW_SIDE_EFFECTING),
    )(*srcs, *lands, send_sems, recv_sems, *after)
    return list(out[:n]), list(out[n:])


def _pack(parts, dtype, row_mult):
    flat = jnp.concatenate([p.reshape(-1).astype(dtype) for p in parts])
    n = flat.shape[0]
    rows = -(-n // PACK_COLS)
    rows = -(-rows // row_mult) * row_mult
    flat = jnp.pad(flat, (0, rows * PACK_COLS - n))
    return flat.reshape(rows, PACK_COLS)


def _unpack(flat, shapes):
    out, off = [], 0
    for shp in shapes:
        n = math.prod(shp)
        out.append(flat[..., off:off + n].reshape(flat.shape[:-1] + tuple(shp)))
        off += n
    return out


_DIMS = {"nn": (((1,), (0,)), ((), ())), "nt": (((1,), (1,)), ((), ())), "tn": (((0,), (0,)), ((), ()))}


def _matmul(a, b, mode, out_dtype, name, tm=512, tn=512, tk=None, split_n=False):
    if mode == "nn":
        (M, K), (_, N) = a.shape, b.shape
    elif mode == "nt":
        (M, K), (N, _) = a.shape, b.shape
    else:
        (K, M), (_, N) = a.shape, b.shape
    tm, tn = min(tm, M), min(tn, N)
    tk = K if tk is None else min(tk, K)
    nk = K // tk
    assert M % tm == 0 and N % tn == 0 and K % tk == 0, (name, a.shape, b.shape)

    def body(a_ref, b_ref, o_ref, *scratch):
        p = lax.dot_general(a_ref[...].astype(BF16), b_ref[...].astype(BF16), _DIMS[mode],
                            preferred_element_type=F32)
        if split_n:
            o_ref[0] = p.astype(out_dtype)
        elif nk == 1:
            o_ref[...] = p.astype(out_dtype)
        else:
            acc = scratch[0]
            k = pl.program_id(2)

            @pl.when(k == 0)
            def _():
                acc[...] = p

            @pl.when(k > 0)
            def _():
                acc[...] += p

            @pl.when(k == nk - 1)
            def _():
                o_ref[...] = acc[...].astype(out_dtype)

    a_spec = (pl.BlockSpec((tk, tm), lambda i, j, k: (k, i)) if mode == "tn"
              else pl.BlockSpec((tm, tk), lambda i, j, k: (i, k)))
    b_spec = (pl.BlockSpec((tn, tk), lambda i, j, k: (j, k)) if mode == "nt"
              else pl.BlockSpec((tk, tn), lambda i, j, k: (k, j)))
    return pl.pallas_call(
        body, name=name, grid=(M // tm, N // tn, nk),
        out_shape=jax.ShapeDtypeStruct((N // tn, M, tn) if split_n else (M, N), out_dtype),
        in_specs=[a_spec, b_spec],
        out_specs=(pl.BlockSpec((1, tm, tn), lambda i, j, k: (j, i, 0)) if split_n
                   else pl.BlockSpec((tm, tn), lambda i, j, k: (i, j))),
        scratch_shapes=[pltpu.VMEM((tm, tn), F32)] if nk > 1 else [],
        compiler_params=_cparams(),
    )(a, b)


def _ada_fwd(c_all, ada_w, ada_b_cols):
    L, D, n = ada_w.shape

    def body(c_ref, w_ref, b_ref, o_ref):
        act = _silu(c_ref[...]).astype(BF16)
        o_ref[0] = jnp.dot(act, w_ref[0].astype(BF16), preferred_element_type=F32) + b_ref[0]

    return pl.pallas_call(
        body, name="ada_fwd", grid=(L,),
        out_shape=jax.ShapeDtypeStruct((L, N_DEV, n), F32),
        in_specs=[pl.BlockSpec((N_DEV, D), lambda l: (0, 0)),
                  pl.BlockSpec((1, D, n), lambda l: (l, 0, 0)),
                  pl.BlockSpec((1, 1, n), lambda l: (l, 0, 0))],
        out_specs=pl.BlockSpec((1, N_DEV, n), lambda l: (l, 0, 0)),
        compiler_params=_cparams(),
    )(c_all, ada_w, ada_b_cols)


def _ada_bwd(c_all_t, dmod_cols):
    D = c_all_t.shape[0]
    L, _, n = dmod_cols.shape

    def body(c_ref, d_ref, o_ref):
        act = _silu(c_ref[...])
        dm = d_ref[0]
        acc = act[:, 0:1] * dm[0:1, :]
        for b in range(1, N_DEV):
            acc = acc + act[:, b:b + 1] * dm[b:b + 1, :]
        o_ref[0] = acc

    return pl.pallas_call(
        body, name="ada_bwd", grid=(L,),
        out_shape=jax.ShapeDtypeStruct((L, D, n), F32),
        in_specs=[pl.BlockSpec((D, N_DEV), lambda l: (0, 0)),
                  pl.BlockSpec((1, N_DEV, n), lambda l: (l, 0, 0))],
        out_specs=pl.BlockSpec((1, D, n), lambda l: (l, 0, 0)),
        compiler_params=_cparams(),
    )(c_all_t, dmod_cols)


def _rope_tables(pos_col, inv_lane):
    S = pos_col.shape[0]
    T = _row_tile(S)
    half = QK_ROPE // 2

    def body(p_ref, f_ref, c_ref, up_ref, dn_ref):
        ang = p_ref[...] * f_ref[...]
        lane = lax.broadcasted_iota(jnp.int32, ang.shape, 1)
        first = (lane >= QK_NOPE) & (lane < QK_NOPE + half)
        second = (lane >= QK_NOPE + half) & (lane < QK_NOPE + QK_ROPE)
        cs, sn = jnp.cos(ang), jnp.sin(ang)
        c_ref[...] = jnp.where(first | second, cs, 1.0)
        up_ref[...] = jnp.where(first, -sn, 0.0)
        dn_ref[...] = jnp.where(second, sn, 0.0)

    tab = jax.ShapeDtypeStruct((S, HEAD_PAD), F32)
    return pl.pallas_call(
        body, name="rope_tables", grid=(S // T,),
        out_shape=(tab, tab, tab),
        in_specs=[_rows(T, 1), _const((1, HEAD_PAD))],
        out_specs=(_rows(T, HEAD_PAD),) * 3,
        compiler_params=_cparams(),
    )(pos_col, inv_lane)


def _rope(blk, ct, ut, dt):
    half = QK_ROPE // 2
    up = pltpu.roll(blk, HEAD_PAD - half, 1)
    dn = pltpu.roll(blk, half, 1)
    return blk * ct + up * ut + dn * dt


def _rope_t(d, ct, ut, dt):
    half = QK_ROPE // 2
    return d * ct + pltpu.roll(d * ut, half, 1) + pltpu.roll(d * dt, HEAD_PAD - half, 1)


def _row_operands(rows, after=None):
    ops, specs = [], []
    for a in rows:
        if isinstance(a, tuple):
            table, r = a
            ops.append(table)
            specs.append(pl.BlockSpec((None, 1, table.shape[-1]), lambda *_, r=r: (r, 0, 0)))
        else:
            ops.append(a)
            specs.append(pl.BlockSpec(a.shape, lambda *_, nd=a.ndim: (0,) * nd))
    if after is not None:
        ops.append(after)
        specs.append(pl.BlockSpec(memory_space=pl.ANY))
    return ops, specs


def _pre_norm(x, g, scale, shift, name, after=None):
    S, D = x.shape
    T = _row_tile(S)
    row_ops, row_specs = _row_operands([g, scale, shift], after)

    def body(x_ref, g_ref, sc_ref, sh_ref, *rest):
        h_ref = rest[-1]
        xv = x_ref[...]
        rstd = lax.rsqrt(jnp.mean(xv * xv, axis=-1, keepdims=True) + EPS)
        h_ref[...] = ((xv * rstd) * g_ref[...] * (1.0 + sc_ref[...]) + sh_ref[...]).astype(BF16)

    return pl.pallas_call(
        body, name=name, grid=(S // T,),
        out_shape=jax.ShapeDtypeStruct((S, D), BF16),
        in_specs=[_rows(T, D)] + row_specs,
        out_specs=_rows(T, D), compiler_params=_cparams(),
    )(x, *row_ops)


def _fold8(v):
    T, C = v.shape
    return v.reshape(T // SUBLANES, SUBLANES, C).sum(axis=0)


def _col_sums(n_sums, body_fn, ins, in_specs, outs, out_specs, S, T, widths, name):
    n_in, n_out = len(ins), len(outs)
    nt = S // T

    def body(*refs):
        in_refs = refs[:n_in]
        out_refs = refs[n_in:n_in + n_out]
        sum_refs = refs[n_in + n_out:n_in + n_out + n_sums]
        accs = refs[n_in + n_out + n_sums:]
        i = pl.program_id(0)
        terms = body_fn(in_refs, out_refs)

        @pl.when(i == 0)
        def _():
            for acc, t in zip(accs, terms):
                acc[...] = _fold8(t)

        @pl.when(i > 0)
        def _():
            for acc, t in zip(accs, terms):
                acc[...] += _fold8(t)

        @pl.when(i == nt - 1)
        def _():
            for acc, s_ref in zip(accs, sum_refs):
                s_ref[...] = jnp.sum(acc[...], axis=0, keepdims=True)

    return pl.pallas_call(
        body, name=name, grid=(nt,),
        out_shape=tuple(outs) + tuple(jax.ShapeDtypeStruct((1, w), F32) for w in widths),
        in_specs=in_specs,
        out_specs=tuple(out_specs) + tuple(_const((1, w)) for w in widths),
        scratch_shapes=[pltpu.VMEM((SUBLANES, w), F32) for w in widths],
        compiler_params=_cparams(),
    )(*ins)


def _pre_norm_bwd(dh, x, dxo, g, scale, name, after=None):
    S, D = x.shape
    T = _row_tile(S)
    row_ops, row_specs = _row_operands([g, scale], after)

    def fn(ins, outs):
        dh_ref, x_ref, dxo_ref, g_ref, sc_ref = ins[:5]
        xv, dv = x_ref[...], _f32(dh_ref)
        rstd = lax.rsqrt(jnp.mean(xv * xv, axis=-1, keepdims=True) + EPS)
        xh = xv * rstd
        dr = dv * (1.0 + sc_ref[...])
        dxh = dr * g_ref[...]
        outs[0][...] = dxo_ref[...] + rstd * (dxh - xh * jnp.mean(dxh * xh, axis=-1, keepdims=True))
        return [dv, dv * (xh * g_ref[...]), dr * xh]

    return _col_sums(3, fn, [dh, x, dxo] + row_ops,
                     [_rows(T, D), _rows(T, D), _rows(T, D)] + row_specs,
                     [jax.ShapeDtypeStruct((S, D), F32)], [_rows(T, D)], S, T, [D, D, D], name)


def _post_pre_norm(x, y, g_post, gate, g_pre, scale, shift, name):
    S, D = x.shape
    T = _row_tile(S)
    row_ops, row_specs = _row_operands([g_post, gate, g_pre, scale, shift])

    def body(x_ref, y_ref, gp_ref, gt_ref, g_ref, sc_ref, sh_ref, xn_ref, h_ref):
        yv = _f32(y_ref)
        rstd_y = lax.rsqrt(jnp.mean(yv * yv, axis=-1, keepdims=True) + EPS)
        xn = x_ref[...] + gt_ref[...] * ((yv * rstd_y) * gp_ref[...])
        xn_ref[...] = xn
        rstd = lax.rsqrt(jnp.mean(xn * xn, axis=-1, keepdims=True) + EPS)
        h_ref[...] = ((xn * rstd) * g_ref[...] * (1.0 + sc_ref[...]) + sh_ref[...]).astype(BF16)

    return pl.pallas_call(
        body, name=name, grid=(S // T,),
        out_shape=(jax.ShapeDtypeStruct((S, D), F32), jax.ShapeDtypeStruct((S, D), BF16)),
        in_specs=[_rows(T, D), _rows(T, D)] + row_specs,
        out_specs=(_rows(T, D), _rows(T, D)), compiler_params=_cparams(),
    )(x, y, *row_ops)


def _pre_post_norm_bwd(dh, x, dxo, g_pre, scale, y_prev, g_post_prev, gate_prev, name, after=None):
    S, D = x.shape
    T = _row_tile(S)
    row_ops, row_specs = _row_operands([g_pre, scale, g_post_prev, gate_prev], after)

    def fn(ins, outs):
        dh_ref, x_ref, dxo_ref, y_ref, g_ref, sc_ref, gp_ref, gt_ref = ins[:8]
        xv, dv = x_ref[...], _f32(dh_ref)
        rstd = lax.rsqrt(jnp.mean(xv * xv, axis=-1, keepdims=True) + EPS)
        xh = xv * rstd
        dr = dv * (1.0 + sc_ref[...])
        dxh = dr * g_ref[...]
        dx = dxo_ref[...] + rstd * (dxh - xh * jnp.mean(dxh * xh, axis=-1, keepdims=True))
        outs[0][...] = dx
        yv = _f32(y_ref)
        rstd_y = lax.rsqrt(jnp.mean(yv * yv, axis=-1, keepdims=True) + EPS)
        yh = yv * rstd_y
        dn = dx * gt_ref[...]
        dyh = dn * gp_ref[...]
        outs[1][...] = (rstd_y * (dyh - yh * jnp.mean(dyh * yh, axis=-1, keepdims=True))).astype(BF16)
        return [dv, dv * (xh * g_ref[...]), dr * xh, dx * (yh * gp_ref[...]), dn * yh]

    return _col_sums(5, fn, [dh, x, dxo, y_prev] + row_ops,
                     [_rows(T, D), _rows(T, D), _rows(T, D), _rows(T, D)] + row_specs,
                     [jax.ShapeDtypeStruct((S, D), F32), jax.ShapeDtypeStruct((S, D), BF16)],
                     [_rows(T, D), _rows(T, D)], S, T, [D] * 5, name)


def _loss_post_norm_bwd(x, y, g_post, gate, target, name):
    S, D = x.shape
    T = _row_tile(S)
    row_ops, row_specs = _row_operands([g_post, gate])

    def fn(ins, outs):
        x_ref, y_ref, t_ref, gp_ref, gt_ref = ins
        yv = _f32(y_ref)
        rstd_y = lax.rsqrt(jnp.mean(yv * yv, axis=-1, keepdims=True) + EPS)
        yh = yv * rstd_y
        e = x_ref[...] + gt_ref[...] * (yh * gp_ref[...]) - t_ref[...]
        dx = e * (1.0 / D)
        outs[0][...] = dx
        dn = dx * gt_ref[...]
        dyh = dn * gp_ref[...]
        outs[1][...] = (rstd_y * (dyh - yh * jnp.mean(dyh * yh, axis=-1, keepdims=True))).astype(BF16)
        return [e * e, dx * (yh * gp_ref[...]), dn * yh]

    return _col_sums(3, fn, [x, y, target] + row_ops,
                     [_rows(T, D), _rows(T, D), _rows(T, D)] + row_specs,
                     [jax.ShapeDtypeStruct((S, D), F32), jax.ShapeDtypeStruct((S, D), BF16)],
                     [_rows(T, D), _rows(T, D)], S, T, [D] * 3, name)


def _scaled_total(v, coef, name):
    def body(v_ref, o_ref):
        o_ref[...] = jnp.broadcast_to(jnp.sum(v_ref[...], axis=1, keepdims=True) * coef, (1, LANES))

    return pl.pallas_call(body, name=name, out_shape=jax.ShapeDtypeStruct((1, LANES), F32))(v)


CONV_ROWS = 64


def _conv_halo(K):
    return SUBLANES if K - 1 <= SUBLANES else 32


def _conv_fwd(u, w, b, K, name):
    S, C = u.shape
    KP = w.shape[0]
    T, HB, RS = min(512, S), _conv_halo(K), CONV_ROWS
    ratio = T // HB

    def body(u_ref, h_ref, w_ref, b_ref, o_ref, ext):
        i = pl.program_id(1)
        ext[0:HB, :] = jnp.where(i > 0, h_ref[...], 0.0)
        ext[HB:HB + T, :] = u_ref[...]
        for r0 in range(0, T, RS):
            acc = jnp.broadcast_to(b_ref[...], (RS, LANES))
            for k in range(K):
                off = HB - (K - 1) + k + r0
                acc = acc + w_ref[k:k + 1, :] * ext[off:off + RS, :]
            o_ref[r0:r0 + RS, :] = acc

    return pl.pallas_call(
        body, name=name, grid=(C // LANES, S // T),
        out_shape=jax.ShapeDtypeStruct((S, C), F32),
        in_specs=[pl.BlockSpec((T, LANES), lambda c, i: (i, c)),
                  pl.BlockSpec((HB, LANES), lambda c, i: (jnp.maximum(i * ratio - 1, 0), c)),
                  pl.BlockSpec((KP, LANES), lambda c, i: (0, c)),
                  pl.BlockSpec((1, LANES), lambda c, i: (0, c))],
        out_specs=pl.BlockSpec((T, LANES), lambda c, i: (i, c)),
        scratch_shapes=[pltpu.VMEM((HB + T, LANES), F32)],
        compiler_params=_cparams(),
    )(u, u, w, b)


def _conv_bwd(d, u, w, K, name):
    S, C = u.shape
    KP = w.shape[0]
    T, HB, RS = min(512, S), _conv_halo(K), CONV_ROWS
    ratio = T // HB
    nt = S // T
    last_halo = S // HB - 1

    def body(d_ref, dn_ref, u_ref, up_ref, w_ref, du_ref, dw_ref, db_ref, extd, extu, dws, dbs):
        i = pl.program_id(1)
        extd[0:T, :] = d_ref[...]
        extd[T:T + HB, :] = jnp.where(i < nt - 1, dn_ref[...], 0.0)
        extu[0:HB, :] = jnp.where(i > 0, up_ref[...], 0.0)
        extu[HB:HB + T, :] = u_ref[...]

        @pl.when(i == 0)
        def _():
            dws[...] = jnp.zeros_like(dws)
            dbs[...] = jnp.zeros_like(dbs)

        for r0 in range(0, T, RS):
            acc = jnp.zeros((RS, LANES), F32)
            for k in range(K):
                off = (K - 1 - k) + r0
                acc = acc + w_ref[k:k + 1, :] * extd[off:off + RS, :]
            du_ref[r0:r0 + RS, :] = acc
            dch = d_ref[r0:r0 + RS, :]
            dbs[...] += _fold8(dch)
            for k in range(K):
                off = HB - (K - 1) + k + r0
                dws[k * SUBLANES:(k + 1) * SUBLANES, :] += _fold8(dch * extu[off:off + RS, :])

        @pl.when(i == nt - 1)
        def _():
            dw_ref[...] = jnp.zeros_like(dw_ref)
            for k in range(K):
                dw_ref[k:k + 1, :] = jnp.sum(dws[k * SUBLANES:(k + 1) * SUBLANES, :], axis=0, keepdims=True)
            db_ref[...] = jnp.sum(dbs[...], axis=0, keepdims=True)

    return pl.pallas_call(
        body, name=name, grid=(C // LANES, nt),
        out_shape=(jax.ShapeDtypeStruct((S, C), F32), jax.ShapeDtypeStruct((KP, C), F32),
                   jax.ShapeDtypeStruct((1, C), F32)),
        in_specs=[pl.BlockSpec((T, LANES), lambda c, i: (i, c)),
                  pl.BlockSpec((HB, LANES), lambda c, i: (jnp.minimum((i + 1) * ratio, last_halo), c)),
                  pl.BlockSpec((T, LANES), lambda c, i: (i, c)),
                  pl.BlockSpec((HB, LANES), lambda c, i: (jnp.maximum(i * ratio - 1, 0), c)),
                  pl.BlockSpec((KP, LANES), lambda c, i: (0, c))],
        out_specs=(pl.BlockSpec((T, LANES), lambda c, i: (i, c)),
                   pl.BlockSpec((KP, LANES), lambda c, i: (0, c)),
                   pl.BlockSpec((1, LANES), lambda c, i: (0, c))),
        scratch_shapes=[pltpu.VMEM((T + HB, LANES), F32), pltpu.VMEM((HB + T, LANES), F32),
                        pltpu.VMEM((KP * SUBLANES, LANES), F32), pltpu.VMEM((SUBLANES, LANES), F32)],
        compiler_params=_cparams(),
    )(d, d, u, u, w)


SCW = 512
ZE = 3072
QL = 256
KVL = 128


def _rms_rows(x, g):
    rstd = lax.rsqrt(jnp.mean(x * x, axis=-1, keepdims=True) + EPS)
    return (x * rstd) * g


def _attn_tile(S):
    return min(256, S)


_NT = (((1,), (1,)), ((), ()))


LOG2E = math.log2(math.e)
ATTN_FWD_HEADS = 8
ATTN_BWD_HEADS = 4


def _chunk_mask_t(T):
    key = lax.broadcasted_iota(jnp.int32, (T, T), 0) // CHUNK
    qry = lax.broadcasted_iota(jnp.int32, (T, T), 1) // CHUNK
    return key <= qry


W_IN_ROWS = 512


def _w_in_even(h, w_t, qg, kvg, name):
    S, D = h.shape
    tm = min(W_IN_ROWS, S)

    def body(h_ref, w_ref, qg_ref, kvg_ref, z_ref, u_ref, qn_ref, kvn_ref):
        p = lax.dot_general(h_ref[...], w_ref[...], _NT, preferred_element_type=F32)
        z_ref[...] = p.astype(BF16)
        u_ref[...] = p[:, SCW:2 * SCW] * p[:, 2 * SCW:3 * SCW]
        qn_ref[...] = _rms_rows(p[:, 5 * SCW:5 * SCW + QL], qg_ref[...]).astype(BF16)
        kvn_ref[...] = _rms_rows(p[:, 5 * SCW + QL:5 * SCW + QL + KVL], kvg_ref[...]).astype(BF16)

    return pl.pallas_call(
        body, name=name, grid=(S // tm,),
        out_shape=(jax.ShapeDtypeStruct((S, ZE), BF16), jax.ShapeDtypeStruct((S, SCW), F32),
                   jax.ShapeDtypeStruct((S, QL), BF16), jax.ShapeDtypeStruct((S, KVL), BF16)),
        in_specs=[_rows(tm, D), _const(w_t.shape), _const((1, QL)), _const((1, KVL))],
        out_specs=(_rows(tm, ZE), _rows(tm, SCW), _rows(tm, QL), _rows(tm, KVL)),
        compiler_params=_cparams(),
    )(h, w_t, qg, kvg)


def _w_in_odd(h, w, name):
    S, D = h.shape
    tm = min(W_IN_ROWS, S)

    def body(h_ref, w_ref, z_ref, u_ref):
        p = jnp.dot(h_ref[...], w_ref[...], preferred_element_type=F32)
        z_ref[...] = p.astype(BF16)
        u_ref[...] = p[:, 0:D] * _sigmoid(p[:, D:2 * D])

    return pl.pallas_call(
        body, name=name, grid=(S // tm,),
        out_shape=(jax.ShapeDtypeStruct((S, 3 * D), BF16), jax.ShapeDtypeStruct((S, D), F32)),
        in_specs=[_rows(tm, D), _const(w.shape)],
        out_specs=(_rows(tm, 3 * D), _rows(tm, D)),
        compiler_params=_cparams(),
    )(h, w)


def _qkv_fwd_t(qn, kvn, z, tabs, w_q, w_kv, name):
    S = qn.shape[0]
    T = _attn_tile(S)
    HW = HEADS * HEAD_PAD
    scale = LOG2E / math.sqrt(QK_NOPE + QK_ROPE)

    def body(qn_ref, kvn_ref, kr_ref, ct_ref, ut_ref, dt_ref, wq_ref, wkv_ref, q_ref, k_ref, v_ref, kt_ref, vt_ref):
        ct, ut, dt = ct_ref[...], ut_ref[...], dt_ref[...]
        qa = jnp.dot(qn_ref[...], wq_ref[...], preferred_element_type=F32)
        kva = jnp.dot(kvn_ref[...], wkv_ref[...], preferred_element_type=F32)
        kr = _f32(kr_ref)
        ones_row = (lax.broadcasted_iota(jnp.int32, (V_HEAD, T), 0) == 0).astype(F32)
        for h in range(HEADS):
            sl = slice(h * HEAD_PAD, (h + 1) * HEAD_PAD)
            q_ref[:, sl] = (_rope(qa[:, sl], ct, ut, dt) * scale).astype(BF16)
            kh = _rope(kva[:, sl] + kr, ct, ut, dt)
            k_ref[:, sl] = kh.astype(BF16)
            kt_ref[0, sl, :] = kh.T.astype(BF16)
        v_ref[...] = kva[:, HW:].astype(BF16)
        for p in range(HEADS // 2):
            vpt = kva[:, HW + p * LANES:HW + (p + 1) * LANES].T
            for h in range(2):
                r0 = (2 * p + h) * HEAD_PAD
                vt_ref[0, r0:r0 + V_HEAD, :] = vpt[h * V_HEAD:(h + 1) * V_HEAD, :].astype(BF16)
                vt_ref[0, r0 + V_HEAD:r0 + HEAD_PAD, :] = ones_row.astype(BF16)

    t3 = jax.ShapeDtypeStruct((S // T, HW, T), BF16)
    return pl.pallas_call(
        body, name=name, grid=(S // T,),
        out_shape=(jax.ShapeDtypeStruct((S, HW), BF16), jax.ShapeDtypeStruct((S, HW), BF16),
                   jax.ShapeDtypeStruct((S, HEADS * V_HEAD), BF16), t3, t3),
        in_specs=[_rows(T, QL), _rows(T, KVL), _rows(T, HEAD_PAD, 23),
                  _rows(T, HEAD_PAD), _rows(T, HEAD_PAD), _rows(T, HEAD_PAD),
                  _const(w_q.shape), _const(w_kv.shape)],
        out_specs=(_rows(T, HW), _rows(T, HW), _rows(T, HEADS * V_HEAD),
                   pl.BlockSpec((1, HW, T), lambda i: (i, 0, 0)), pl.BlockSpec((1, HW, T), lambda i: (i, 0, 0))),
        compiler_params=_cparams(),
    )(qn, kvn, z, *tabs, w_q, w_kv)


def _attn_fwd_t(q, k, vT3, name):
    S = q.shape[0]
    T = _attn_tile(S)
    nq = S // T
    NH = ATTN_FWD_HEADS
    PW = NH * HEAD_PAD

    def body(q_ref, k_ref, vt_ref, o_ref, lse_ref, m_s, acc_s):
        i = pl.program_id(1)
        m_s[...] = jnp.full_like(m_s, NEG)
        acc_s[...] = jnp.zeros_like(acc_s)
        qv = q_ref[...]

        def step(j, masked):
            kb = k_ref[pl.ds(pl.multiple_of(j * T, T), T), :]
            vt = vt_ref[j]
            heads = [slice(h * HEAD_PAD, (h + 1) * HEAD_PAD) for h in range(NH)]
            sts = [lax.dot_general(kb[:, sl], qv[:, sl], _NT, preferred_element_type=F32) for sl in heads]
            alphas, pvs = [], []
            for h, sl in enumerate(heads):
                st = jnp.where(_chunk_mask_t(T), sts[h], NEG) if masked else sts[h]
                m_prev = m_s[h]
                m_new = jnp.maximum(m_prev, jnp.max(st, axis=0, keepdims=True))
                alphas.append(jnp.exp2(m_prev[0:1] - m_new[0:1]))
                pt = jnp.exp2(st - m_new[0:1]).astype(BF16)
                m_s[h] = m_new
                pvs.append(jnp.dot(vt[sl, :], pt, preferred_element_type=F32))
            for h in range(NH):
                acc_s[h] = acc_s[h] * alphas[h] + pvs[h]

        def loop_body(j, carry):
            step(j, False)
            return carry

        lax.fori_loop(0, i, loop_body, 0)
        step(i, True)
        for g in range(NH // 2):
            outs = []
            for h in (2 * g, 2 * g + 1):
                acc = acc_s[h]
                l_row = acc[V_HEAD:V_HEAD + 1, :]
                outs.append(acc[0:V_HEAD, :] / l_row)
                lse_ref[0, h * SUBLANES:(h + 1) * SUBLANES, :] = m_s[h] + jnp.log2(l_row)
            o_ref[:, g * LANES:(g + 1) * LANES] = jnp.concatenate(outs, axis=0).T

    return pl.pallas_call(
        body, name=name, grid=(HEADS // NH, nq),
        out_shape=(jax.ShapeDtypeStruct((S, HEADS * V_HEAD), F32),
                   jax.ShapeDtypeStruct((nq, HEADS * SUBLANES, T), F32)),
        in_specs=[pl.BlockSpec((T, PW), lambda p, i: (i, p)),
                  pl.BlockSpec((S, PW), lambda p, i: (0, p)),
                  pl.BlockSpec((nq, PW, T), lambda p, i: (0, p, 0))],
        out_specs=(pl.BlockSpec((T, NH * V_HEAD), lambda p, i: (i, p)),
                   pl.BlockSpec((1, NH * SUBLANES, T), lambda p, i: (i, p, 0))),
        scratch_shapes=[pltpu.VMEM((NH, SUBLANES, T), F32), pltpu.VMEM((NH, HEAD_PAD, T), F32)],
        compiler_params=_cparams(),
    )(q, k, vT3)


def _attn_bwd_t(q, k, v, kT3, do, lse3, dl3, name):
    S = q.shape[0]
    T = _attn_tile(S)
    nq = S // T
    NH = ATTN_BWD_HEADS
    PW = NH * HEAD_PAD
    VW = NH * V_HEAD

    def body(q_ref, k_ref, v_ref, kt_ref, do_ref, lse_ref, dl_ref, dq_ref, dk_ref, dv_ref, dk_s, dv_s):
        j = pl.program_id(1)
        left = lax.broadcasted_iota(jnp.int32, (T, LANES), 1) < V_HEAD

        @pl.when(j == 0)
        def _():
            dq_ref[...] = jnp.zeros_like(dq_ref)

        dk_s[...] = jnp.zeros_like(dk_s)
        dv_s[...] = jnp.zeros_like(dv_s)
        kb = k_ref[...]
        vms = []
        for g in range(NH // 2):
            vb = v_ref[:, g * LANES:(g + 1) * LANES]
            vms += [jnp.where(left, vb, jnp.zeros_like(vb)), jnp.where(left, jnp.zeros_like(vb), vb)]
        kt = kt_ref[0]

        def step(i, masked):
            r0 = pl.multiple_of(i * T, T)
            qb = q_ref[pl.ds(r0, T), :]
            do_all = do_ref[pl.ds(r0, T), :]
            lse = lse_ref[i]
            dl = dl_ref[i]
            heads = [slice(h * HEAD_PAD, (h + 1) * HEAD_PAD) for h in range(NH)]
            dobs = [do_all[:, (h // 2) * LANES:(h // 2 + 1) * LANES] for h in range(NH)]
            sts = [lax.dot_general(kb[:, sl], qb[:, sl], _NT, preferred_element_type=F32) for sl in heads]
            dpts = [lax.dot_general(vms[h], dobs[h], _NT, preferred_element_type=F32) for h in range(NH)]
            res = []
            for h, sl in enumerate(heads):
                r8 = h * SUBLANES
                pt = jnp.exp2(sts[h] - lse[r8:r8 + 1, :])
                if masked:
                    pt = jnp.where(_chunk_mask_t(T), pt, 0.0)
                dst = (pt * (dpts[h] - dl[r8:r8 + 1, :])).astype(BF16)
                res.append((jnp.dot(pt.astype(BF16), dobs[h], preferred_element_type=F32),
                            jnp.dot(dst, qb[:, sl], preferred_element_type=F32),
                            jnp.dot(kt[sl, :], dst, preferred_element_type=F32)))
            for h, sl in enumerate(heads):
                dv_s[h] += res[h][0]
                dk_s[:, sl] += res[h][1]
                dq_ref[i, sl, :] += res[h][2]

        def loop_body(i, carry):
            step(i, False)
            return carry

        step(j, True)
        lax.fori_loop(j + 1, nq, loop_body, 0)
        dk_ref[...] = dk_s[...] * (1.0 / LOG2E)
        for g in range(NH // 2):
            dv_ref[:, g * LANES:(g + 1) * LANES] = jnp.where(left, dv_s[2 * g], dv_s[2 * g + 1])

    return pl.pallas_call(
        body, name=name, grid=(HEADS // NH, nq),
        out_shape=(jax.ShapeDtypeStruct((nq, HEADS * HEAD_PAD, T), F32),
                   jax.ShapeDtypeStruct((S, HEADS * HEAD_PAD), F32), jax.ShapeDtypeStruct((S, HEADS * V_HEAD), F32)),
        in_specs=[pl.BlockSpec((S, PW), lambda p, j: (0, p)),
                  pl.BlockSpec((T, PW), lambda p, j: (j, p)),
                  pl.BlockSpec((T, VW), lambda p, j: (j, p)),
                  pl.BlockSpec((1, PW, T), lambda p, j: (j, p, 0)),
                  pl.BlockSpec((S, VW), lambda p, j: (0, p)),
                  pl.BlockSpec((nq, NH * SUBLANES, T), lambda p, j: (0, p, 0)),
                  pl.BlockSpec((nq, NH * SUBLANES, T), lambda p, j: (0, p, 0))],
        out_specs=(pl.BlockSpec((nq, PW, T), lambda p, j: (0, p, 0)),
                   pl.BlockSpec((T, PW), lambda p, j: (j, p)),
                   pl.BlockSpec((T, VW), lambda p, j: (j, p))),
        scratch_shapes=[pltpu.VMEM((T, PW), F32), pltpu.VMEM((NH, T, LANES), F32)],
        compiler_params=_cparams(),
    )(q, k, v, kT3, do, lse3, dl3)


def _even_post(z, cv, o, name):
    S = z.shape[0]
    T = _wide_tile(S)

    def body(ab_ref, ag_ref, bg_ref, cv_ref, o_ref, y_ref):
        y_ref[:, 0:SCW] = (_f32(ab_ref) * cv_ref[...] * _silu(_f32(ag_ref))).astype(BF16)
        y_ref[:, SCW:2 * SCW] = (o_ref[...] * _silu(_f32(bg_ref))).astype(BF16)

    return pl.pallas_call(
        body, name=name, grid=(S // T,),
        out_shape=jax.ShapeDtypeStruct((S, 2 * SCW), BF16),
        in_specs=[_rows(T, SCW, 0), _rows(T, SCW, 3), _rows(T, SCW, 4), _rows(T, SCW), _rows(T, SCW)],
        out_specs=_rows(T, 2 * SCW), compiler_params=_cparams(),
    )(z, z, z, cv, o)


def _even_bwd_gates(dyc, z, cv, o, name):
    S = z.shape[0]
    T = _row_tile(S)

    def body(dya_ref, dyb_ref, ab_ref, ag_ref, bg_ref, cv_ref, o_ref,
             dab_ref, dag_ref, dbg_ref, dcv_ref, do_ref, dl_ref):
        dya, ab, ag, cv = _f32(dya_ref), _f32(ab_ref), _f32(ag_ref), cv_ref[...]
        sg = _silu(ag)
        dab_ref[...] = (dya * cv * sg).astype(BF16)
        dcv_ref[...] = dya * ab * sg
        dag_ref[...] = (dya * ab * cv * _dsilu(ag)).astype(BF16)
        dyb, bg, ov = _f32(dyb_ref), _f32(bg_ref), o_ref[...]
        dov = dyb * _silu(bg)
        do_ref[...] = dov.astype(BF16)
        dbg_ref[...] = (dyb * ov * _dsilu(bg)).astype(BF16)
        prod = dov * ov
        left = lax.broadcasted_iota(jnp.int32, (T, LANES), 1) < V_HEAD
        for p in range(HEADS // 2):
            blk = prod[:, p * LANES:(p + 1) * LANES]
            s0 = jnp.sum(jnp.where(left, blk, 0.0), axis=1, keepdims=True)
            s1 = jnp.sum(jnp.where(left, 0.0, blk), axis=1, keepdims=True)
            dt = jnp.where(left, s0, s1).T
            dl_ref[0, 2 * p * SUBLANES:(2 * p + 1) * SUBLANES, :] = dt[0:SUBLANES, :]
            dl_ref[0, (2 * p + 1) * SUBLANES:(2 * p + 2) * SUBLANES, :] = dt[V_HEAD:V_HEAD + SUBLANES, :]

    assert T == _attn_tile(S)
    bf = jax.ShapeDtypeStruct((S, SCW), BF16)
    ff = jax.ShapeDtypeStruct((S, SCW), F32)
    return pl.pallas_call(
        body, name=name, grid=(S // T,),
        out_shape=(bf, bf, bf, ff, bf, jax.ShapeDtypeStruct((S // T, HEADS * SUBLANES, T), F32)),
        in_specs=[_rows(T, SCW, 0), _rows(T, SCW, 1), _rows(T, SCW, 0), _rows(T, SCW, 3), _rows(T, SCW, 4),
                  _rows(T, SCW), _rows(T, SCW)],
        out_specs=(_rows(T, SCW),) * 5 + (pl.BlockSpec((1, HEADS * SUBLANES, T), lambda i: (i, 0, 0)),),
        compiler_params=_cparams(),
    )(dyc, dyc, z, z, z, cv, o)


def _qkv_bwd(dq, dk, dv, z, tabs, w_q, w_kv, qg, kvg, name):
    S = dk.shape[0]
    T = _attn_tile(S)
    HW = HEADS * HEAD_PAD
    VW = HEADS * V_HEAD
    scale = 1.0 / math.sqrt(QK_NOPE + QK_ROPE)

    def fn(ins, outs):
        dq_ref, dk_ref, dv_ref, cq_ref, ckv_ref, ct_ref, ut_ref, dt_ref, wq_ref, wkv_ref, qg_ref, kvg_ref = ins
        dqp_ref, dkvp_ref, dcq_ref, dckv_ref, dkr_ref = outs
        ct, ut, dt = ct_ref[...], ut_ref[...], dt_ref[...]
        dkr = jnp.zeros((T, HEAD_PAD), F32)
        for h in range(HEADS):
            sl = slice(h * HEAD_PAD, (h + 1) * HEAD_PAD)
            dqp_ref[:, sl] = (_rope_t(dq_ref[0, sl, :].T, ct, ut, dt) * scale).astype(BF16)
            dkh = _rope_t(dk_ref[:, sl], ct, ut, dt)
            dkr = dkr + dkh
            dkvp_ref[:, sl] = dkh.astype(BF16)
        dkvp_ref[:, HW:] = dv_ref[...].astype(BF16)
        dkr_ref[...] = dkr.astype(BF16)
        sums = []
        for lat_ref, g_ref, dpre_ref, w_ref, dlat_ref in ((cq_ref, qg_ref, dqp_ref, wq_ref, dcq_ref),
                                                         (ckv_ref, kvg_ref, dkvp_ref, wkv_ref, dckv_ref)):
            dn = lax.dot_general(dpre_ref[...], w_ref[...], _NT, preferred_element_type=F32)
            xv = _f32(lat_ref)
            rstd = lax.rsqrt(jnp.mean(xv * xv, axis=-1, keepdims=True) + EPS)
            xh = xv * rstd
            dxh = dn * g_ref[...]
            dlat_ref[...] = (rstd * (dxh - xh * jnp.mean(dxh * xh, axis=-1, keepdims=True))).astype(BF16)
            sums.append(dn * xh)
        return sums

    return _col_sums(
        2, fn, [dq, dk, dv, z, z, *tabs, w_q, w_kv, qg, kvg],
        [pl.BlockSpec((1, HW, T), lambda i: (i, 0, 0)), _rows(T, HW), _rows(T, VW), _rows(T, QL, 10), _rows(T, KVL, 22),
         _rows(T, HEAD_PAD), _rows(T, HEAD_PAD), _rows(T, HEAD_PAD),
         _const(w_q.shape), _const(w_kv.shape), _const((1, QL)), _const((1, KVL))],
        [jax.ShapeDtypeStruct((S, HW), BF16), jax.ShapeDtypeStruct((S, HW + VW), BF16),
         jax.ShapeDtypeStruct((S, QL), BF16), jax.ShapeDtypeStruct((S, KVL), BF16),
         jax.ShapeDtypeStruct((S, HEAD_PAD), BF16)],
        [_rows(T, HW), _rows(T, HW + VW), _rows(T, QL), _rows(T, KVL), _rows(T, HEAD_PAD)],
        S, T, [QL, KVL], name)


def _even_dz(dab, du, z, dag, dbg, dcq, dckv, dkr, name):
    S = z.shape[0]
    T = _wide_tile(S)

    def body(dab_ref, du_ref, ac_ref, ax_ref, dag_ref, dbg_ref, dcq_ref, dckv_ref, dkr_ref, dz_ref):
        duv = du_ref[...]
        dz_ref[:, 0:SCW] = dab_ref[...]
        dz_ref[:, SCW:2 * SCW] = (duv * _f32(ax_ref)).astype(BF16)
        dz_ref[:, 2 * SCW:3 * SCW] = (duv * _f32(ac_ref)).astype(BF16)
        dz_ref[:, 3 * SCW:4 * SCW] = dag_ref[...]
        dz_ref[:, 4 * SCW:5 * SCW] = dbg_ref[...]
        dz_ref[:, 5 * SCW:5 * SCW + QL] = dcq_ref[...]
        dz_ref[:, 5 * SCW + QL:5 * SCW + QL + KVL] = dckv_ref[...]
        dz_ref[:, 5 * SCW + QL + KVL:ZE] = dkr_ref[...]

    return pl.pallas_call(
        body, name=name, grid=(S // T,),
        out_shape=jax.ShapeDtypeStruct((S, ZE), BF16),
        in_specs=[_rows(T, SCW), _rows(T, SCW), _rows(T, SCW, 1), _rows(T, SCW, 2), _rows(T, SCW), _rows(T, SCW),
                  _rows(T, QL), _rows(T, KVL), _rows(T, HEAD_PAD)],
        out_specs=_rows(T, ZE), compiler_params=_cparams(),
    )(dab, du, z, z, dag, dbg, dcq, dckv, dkr)


def _layer_norm_stats(cv):
    mu = jnp.mean(cv, axis=-1, keepdims=True)
    cen = cv - mu
    rstd = lax.rsqrt(jnp.mean(cen * cen, axis=-1, keepdims=True) + EPS)
    return cen * rstd, rstd


def _odd_post(cv, z, ln_g, ln_b, name):
    S, D = cv.shape
    T = _wide_tile(S)

    def body(cv_ref, sg_ref, g_ref, b_ref, y_ref):
        cvh, _ = _layer_norm_stats(cv_ref[...])
        y_ref[...] = (_silu(cvh * g_ref[...] + b_ref[...]) * _silu(_f32(sg_ref))).astype(BF16)

    return pl.pallas_call(
        body, name=name, grid=(S // T,),
        out_shape=jax.ShapeDtypeStruct((S, D), BF16),
        in_specs=[_rows(T, D), _rows(T, D, 2), _const((1, D)), _const((1, D))],
        out_specs=_rows(T, D), compiler_params=_cparams(),
    )(cv, z, ln_g, ln_b)


def _odd_bwd_norm(dyi, cv, z, ln_g, ln_b, name):
    S, D = cv.shape
    T = _row_tile(S)

    def fn(ins, outs):
        dy_ref, cv_ref, sg_ref, g_ref, b_ref = ins
        dcv_ref, dsg_ref = outs
        cvh, rstd = _layer_norm_stats(cv_ref[...])
        ln = cvh * g_ref[...] + b_ref[...]
        sgv, dy = _f32(sg_ref), _f32(dy_ref)
        dsg_ref[...] = (dy * _silu(ln) * _dsilu(sgv)).astype(BF16)
        dln = dy * _silu(sgv) * _dsilu(ln)
        dh = dln * g_ref[...]
        dcv_ref[...] = rstd * (dh - jnp.mean(dh, axis=-1, keepdims=True)
                               - cvh * jnp.mean(dh * cvh, axis=-1, keepdims=True))
        return [dln * cvh, dln]

    return _col_sums(2, fn, [dyi, cv, z, ln_g, ln_b],
                     [_rows(T, D), _rows(T, D), _rows(T, D, 2), _const((1, D)), _const((1, D))],
                     [jax.ShapeDtypeStruct((S, D), F32), jax.ShapeDtypeStruct((S, D), BF16)],
                     [_rows(T, D), _rows(T, D)], S, T, [D, D], name)


def _odd_dz(du, z, dsg, name):
    S, D = du.shape
    T = _wide_tile(S)

    def body(du_ref, val_ref, glu_ref, dsg_ref, dz_ref):
        duv = du_ref[...]
        sig = _sigmoid(_f32(glu_ref))
        dz_ref[:, 0:D] = (duv * sig).astype(BF16)
        dz_ref[:, D:2 * D] = (duv * _f32(val_ref) * sig * (1.0 - sig)).astype(BF16)
        dz_ref[:, 2 * D:3 * D] = dsg_ref[...]

    return pl.pallas_call(
        body, name=name, grid=(S // T,),
        out_shape=jax.ShapeDtypeStruct((S, 3 * D), BF16),
        in_specs=[_rows(T, D), _rows(T, D, 0), _rows(T, D, 1), _rows(T, D)],
        out_specs=_rows(T, 3 * D), compiler_params=_cparams(),
    )(du, z, z, dsg)


ADAM_BLOCK_ELEMS = 128 * 1024


def _adam_tiles(R, C):
    if R * C <= ADAM_BLOCK_ELEMS:
        return R, C
    tr = R
    for cand in range(SUBLANES, R, SUBLANES):
        if R % cand == 0 and cand * C <= ADAM_BLOCK_ELEMS:
            tr = cand
    if tr < R:
        return tr, C
    tc = C
    for cand in range(LANES, C, LANES):
        if C % cand == 0 and R * cand <= ADAM_BLOCK_ELEMS:
            tc = cand
    return R, tc


def _adamw(g_parts, w, m, v, name):
    if not isinstance(g_parts, (list, tuple)):
        g_parts = [g_parts]
    ng = len(g_parts)
    _, R, C = g_parts[0].shape
    tr, tc = _adam_tiles(R, C)

    def body(*refs):
        g_refs = refs[:ng]
        w_ref, m_ref, v_ref, go_ref, d_ref, mo_ref, vo_ref = refs[ng:]
        g = None
        for g_ref in g_refs:
            for p in range(g_ref.shape[0]):
                part = g_ref[p].astype(F32)
                g = part if g is None else g + part
        mn = ADAM_B1 * m_ref[...] + (1.0 - ADAM_B1) * g
        vn = ADAM_B2 * v_ref[...] + (1.0 - ADAM_B2) * (g * g)
        m_hat = mn / (1.0 - ADAM_B1 ** ADAM_STEP)
        v_hat = vn / (1.0 - ADAM_B2 ** ADAM_STEP)
        go_ref[...] = g
        d_ref[...] = -ADAM_LR * (m_hat / (jnp.sqrt(v_hat) + ADAM_EPS) + ADAM_WD * w_ref[...])
        mo_ref[...] = mn
        vo_ref[...] = vn

    slab = jax.ShapeDtypeStruct((R, C), F32)
    blk = pl.BlockSpec((tr, tc), lambda i, j: (i, j))
    return pl.pallas_call(
        body, name=name, grid=(R // tr, C // tc),
        out_shape=(slab,) * 4,
        in_specs=[pl.BlockSpec((g.shape[0], tr, tc), lambda i, j: (0, i, j)) for g in g_parts] + [blk, blk, blk],
        out_specs=(blk,) * 4, compiler_params=_cparams(),
    )(*g_parts, w, m, v)


def _adamw_slab(g_parts, w, m, v, layer, prev, name):
    ng = len(g_parts)
    NL, R, C = w.shape
    tr, tc = _adam_tiles(R, C)

    def body(*refs):
        g_refs = refs[:ng]
        w_ref, m_ref, v_ref = refs[ng:ng + 3]
        go_ref, d_ref, mo_ref, vo_ref = refs[-4:]
        g = None
        for g_ref in g_refs:
            for p in range(g_ref.shape[0]):
                part = g_ref[p].astype(F32)
                g = part if g is None else g + part
        mn = ADAM_B1 * m_ref[0] + (1.0 - ADAM_B1) * g
        vn = ADAM_B2 * v_ref[0] + (1.0 - ADAM_B2) * (g * g)
        m_hat = mn / (1.0 - ADAM_B1 ** ADAM_STEP)
        v_hat = vn / (1.0 - ADAM_B2 ** ADAM_STEP)
        go_ref[0] = g
        d_ref[0] = -ADAM_LR * (m_hat / (jnp.sqrt(v_hat) + ADAM_EPS) + ADAM_WD * w_ref[0])
        mo_ref[0] = mn
        vo_ref[0] = vn

    blk = pl.BlockSpec((1, tr, tc), lambda i, j: (layer, i, j))
    n_in = ng + 3
    prev = list(prev) if prev is not None else []
    return pl.pallas_call(
        body, name=name, grid=(R // tr, C // tc),
        out_shape=(jax.ShapeDtypeStruct((NL, R, C), F32),) * 4,
        in_specs=([pl.BlockSpec((g.shape[0], tr, tc), lambda i, j: (0, i, j)) for g in g_parts] + [blk, blk, blk]
                  + [pl.BlockSpec(memory_space=pl.ANY)] * len(prev)),
        out_specs=(blk,) * 4,
        input_output_aliases={n_in + k: k for k in range(len(prev))},
        compiler_params=_cparams(),
    )(*g_parts, w, m, v, *prev)


def _gather_cols(g, shape):
    nd = len(shape)
    t = jnp.moveaxis(g, 0, nd - 1)
    return t.reshape(tuple(shape[:-1]) + (N_DEV * shape[-1],))


def _scatter_cols(full, n):
    t = full.reshape(full.shape[:-1] + (N_DEV, n))
    return jnp.moveaxis(t, -2, 0)


def kernel(x, c, positions, ada_w, ada_b, pre_norm_g, post_norm_g, even_w_in, even_sc_conv_w, even_sc_conv_b, even_q_norm_g, even_kv_norm_g, even_w_uq, even_w_ukv, even_w_out, odd_w_in, odd_conv_w, odd_conv_b, odd_ln_g, odd_ln_b, odd_w_out, loss_target, m_ada_w, m_ada_b, m_pre_norm_g, m_post_norm_g, m_even_w_in, m_even_sc_conv_w, m_even_sc_conv_b, m_even_q_norm_g, m_even_kv_norm_g, m_even_w_uq, m_even_w_ukv, m_even_w_out, m_odd_w_in, m_odd_conv_w, m_odd_conv_b, m_odd_ln_g, m_odd_ln_b, m_odd_w_out, v_ada_w, v_ada_b, v_pre_norm_g, v_post_norm_g, v_even_w_in, v_even_sc_conv_w, v_even_sc_conv_b, v_even_q_norm_g, v_even_kv_norm_g, v_even_w_uq, v_even_w_ukv, v_even_w_out, v_odd_w_in, v_odd_conv_w, v_odd_conv_b, v_odd_ln_g, v_odd_ln_b, v_odd_w_out):
    S, D = x.shape[1], x.shape[2]
    L = ada_w.shape[0]
    NE, NO = even_w_in.shape[0], odd_w_in.shape[0]
    me = 4 * lax.axis_index("x") + 2 * lax.axis_index("y") + lax.axis_index("c")
    x0 = x[0]
    target = loss_target[0]

    small_parts = [c, even_sc_conv_w, odd_conv_w, odd_conv_b, odd_ln_g, odd_ln_b]
    small_shapes = [p.shape for p in small_parts]
    sg = _exchange([_pack(small_parts, F32, SUBLANES)], False, "gather_small")[0].reshape(N_DEV, -1)
    c_all, scw_g, ocw_g, ocb_g, olg_g, olb_g = _unpack(sg, small_shapes)
    c_all = c_all.reshape(N_DEV, D)
    sc_conv_w = _gather_cols(scw_g, even_sc_conv_w.shape)
    o_conv_w = _gather_cols(ocw_g, odd_conv_w.shape)
    o_conv_b = _gather_cols(ocb_g, odd_conv_b.shape)
    o_ln_g = _gather_cols(olg_g, odd_ln_g.shape)
    o_ln_b = _gather_cols(olb_g, odd_ln_b.shape)

    pad_q = HEAD_PAD - QK_NOPE - QK_ROPE
    w_local = [jnp.swapaxes(even_w_in, 1, 2).astype(BF16),
               jnp.pad(even_w_uq, ((0, 0), (0, 0), (0, pad_q))).astype(BF16),
               jnp.pad(even_w_ukv[..., :QK_NOPE], ((0, 0), (0, 0), (0, HEAD_PAD - QK_NOPE))).astype(BF16),
               even_w_ukv[..., QK_NOPE:].astype(BF16),
               even_w_out.astype(BF16), odd_w_in.astype(BF16), odd_w_out.astype(BF16)]
    n_ada = ada_w.shape[2]
    ada_b_cols = lax.dynamic_slice_in_dim(ada_b, me * n_ada, n_ada, axis=1).reshape(L, 1, n_ada)
    mod_slab = _ada_fwd(c_all, ada_w, ada_b_cols)
    mod_g = _exchange([_pack([mod_slab], F32, SUBLANES)], False, "gather_mod")[0].reshape(N_DEV, -1)
    mod_all = mod_g[:, :L * N_DEV * n_ada].reshape(N_DEV, L, N_DEV, n_ada)
    mod = lax.dynamic_index_in_dim(mod_all, me, axis=2, keepdims=False)
    mod = jnp.moveaxis(mod, 0, 1).reshape(L, 3 * D)
    shift, scale, gate = mod[:, :D], mod[:, D:2 * D], mod[:, 2 * D:]

    heads_to_cols = lambda g: jnp.moveaxis(g, 0, 1).reshape(g.shape[1], -1)
    w_handles = {}
    token = jnp.broadcast_to(jnp.minimum(jnp.abs(mod[0, 0]), 0.0), (SUBLANES, LANES))
    for layer in range(L):
        i = layer // 2
        groups = ({"in": [w_local[0][i]], "rest": [w[i] for w in w_local[1:5]]} if layer % 2 == 0
                  else {"all": [w[i] for w in w_local[5:]]})
        for key, mine in groups.items():
            mine = [w + token[0, 0].astype(BF16) for w in mine]
            w_handles[layer, key], token = _exchange_start(mine, False, f"gather_weights_start_l{layer}_{key}")
    w_token = token

    def arrived(layer, key, after):
        return _exchange_wait(w_handles[layer, key], False, after, f"gather_weights_wait_l{layer}_{key}")[1]

    e_w_in_k, e_w_q_k, e_w_kv_k, e_w_out, o_w_in, o_w_out = ([None] * NE, [None] * NE, [None] * NE, [None] * NE,
                                                             [None] * NO, [None] * NO)

    inv_freq = 1.0 / (ROPE_THETA ** (jnp.arange(0, QK_ROPE, 2, dtype=F32) / QK_ROPE))
    inv_lane = jnp.zeros((HEAD_PAD,), F32).at[QK_NOPE:QK_NOPE + QK_ROPE].set(jnp.concatenate([inv_freq, inv_freq]))
    tabs = _rope_tables(positions.astype(F32).reshape(S, 1), inv_lane.reshape(1, HEAD_PAD))

    row = lambda a: a.reshape(1, -1)
    scb = even_sc_conv_b
    KP3, KP31 = SUBLANES, 32

    saved = []
    xs = x0
    lv = jnp.stack([pre_norm_g, post_norm_g, scale, shift, gate], axis=1).reshape(L * 5, 1, D)
    PRE_G, POST_G, SCALE, SHIFT, GATE = range(5)
    vec = lambda layer, k: (lv, layer * 5 + k)
    h = _pre_norm(xs, vec(0, PRE_G), vec(0, SCALE), vec(0, SHIFT), "pre_norm_l0", after=w_token)
    for layer in range(L):
        i = layer // 2
        tag = f"l{layer}"
        first = [h, tabs[0]] if layer == 0 else h
        if layer % 2 == 0:
            wt = arrived(layer, "in", first)[0].reshape(-1, D)
            e_w_in_k[i] = jnp.concatenate([wt[:2048], wt[2464:2976], wt[2048:2432], jnp.zeros((QK_NOPE, D), BF16),
                                           wt[2432:2464], jnp.zeros((pad_q, D), BF16)], axis=0)
            z, u, qn, kvn = _w_in_even(h, e_w_in_k[i], row(even_q_norm_g[i]), row(even_kv_norm_g[i]), f"w_in_{tag}")
            eq_g, ek_g, ev_g, eout_g = arrived(layer, "rest", z)
            e_w_q_k[i] = heads_to_cols(eq_g)
            e_w_kv_k[i] = jnp.concatenate([heads_to_cols(ek_g), heads_to_cols(ev_g)], axis=-1)
            e_w_out[i] = eout_g.reshape(-1, D)
            cw = jnp.pad(sc_conv_w[i], ((0, KP3 - SC_KERNEL), (0, 0)))
            cv = _conv_fwd(u, cw, row(scb[i]), SC_KERNEL, f"conv_{tag}")
            q, k, v, kT3, vT3 = _qkv_fwd_t(qn, kvn, z, tabs, e_w_q_k[i], e_w_kv_k[i], f"qkv_{tag}")
            o, lse = _attn_fwd_t(q, k, vT3, f"attn_{tag}")
            ycat = _even_post(z, cv, o, f"even_post_{tag}")
            y = _matmul(ycat, e_w_out[i], "nn", BF16, f"w_out_{tag}", tn=1024)
            saved.append(dict(x=xs, h=h, z=z, u=u, qn=qn, kvn=kvn, cw=cw, cv=cv, q=q, k=k, v=v, kT3=kT3, o=o, lse=lse,
                              ycat=ycat, y=y))
        else:
            owin_g, oout_g = arrived(layer, "all", first)
            o_w_in[i], o_w_out[i] = heads_to_cols(owin_g), oout_g.reshape(-1, D)
            z, u = _w_in_odd(h, o_w_in[i], f"w_in_{tag}")
            cw = jnp.pad(o_conv_w[i], ((0, KP31 - CONF_KERNEL), (0, 0)))
            cv = _conv_fwd(u, cw, row(o_conv_b[i]), CONF_KERNEL, f"conv_{tag}")
            yin = _odd_post(cv, z, row(o_ln_g[i]), row(o_ln_b[i]), f"odd_post_{tag}")
            y = _matmul(yin, o_w_out[i], "nn", BF16, f"w_out_{tag}", tn=1024)
            saved.append(dict(x=xs, h=h, z=z, u=u, cw=cw, cv=cv, yin=yin, y=y))
        if layer < L - 1:
            xs, h = _post_pre_norm(xs, y, vec(layer, POST_G), vec(layer, GATE), vec(layer + 1, PRE_G),
                                   vec(layer + 1, SCALE), vec(layer + 1, SHIFT), f"post_pre_norm_{tag}")

    dx, dy, err_sq, dgate, g_post_last = _loss_post_norm_bwd(xs, y, vec(L - 1, POST_G), vec(L - 1, GATE), target,
                                                             "loss_post_norm_bwd")
    loss = lax.psum(_scaled_total(err_sq, 0.5 / D, "loss_total")[0, 0], MESH_AXES)

    g_pre, g_post, dmod = [None] * L, [None] * L, [None] * L
    g_e_w_in, g_e_w_uq, g_e_w_ukv, g_e_w_out = [None] * NE, [None] * NE, [None] * NE, [None] * NE
    g_scw, g_scb, g_qg, g_kvg = [None] * NE, [None] * NE, [None] * NE, [None] * NE
    g_o_w_in, g_o_w_out, g_ocw, g_ocb, g_olg, g_olb = ([None] * NO for _ in range(6))
    sm_w = [even_sc_conv_w, odd_conv_w, odd_conv_b, odd_ln_g, odd_ln_b]
    sm_rows = _pack(sm_w, F32, SUBLANES).shape[0]

    def small_slab():
        full = [_scatter_cols(jnp.stack(g_scw), even_sc_conv_w.shape[-1]),
                _scatter_cols(jnp.stack(g_ocw), odd_conv_w.shape[-1]),
                _scatter_cols(jnp.concatenate(g_ocb, 0), odd_conv_b.shape[-1]),
                _scatter_cols(jnp.concatenate(g_olg, 0), odd_ln_g.shape[-1]),
                _scatter_cols(jnp.concatenate(g_olb, 0), odd_ln_b.shape[-1])]
        flat = jnp.concatenate([g.reshape(N_DEV, -1) for g in full], axis=1)
        return jnp.pad(flat, ((0, 0), (0, sm_rows * PACK_COLS - flat.shape[1]))).reshape(N_DEV, sm_rows, PACK_COLS)

    scatters = []
    bw_token = jnp.zeros((SUBLANES, LANES), F32)

    def start_scatter(tag, names, parts):
        own = [lax.dynamic_slice_in_dim(g, me, 1, axis=0) for g in parts]
        handle, token = _exchange_start([g.astype(BF16) for g in parts], True, f"scatter_grads_start_{tag}")
        scatters.append((tag, names, handle, own))
        return token

    for layer in reversed(range(L)):
        i = layer // 2
        tag = f"l{layer}"
        sv = saved[layer]
        if layer == L - 1:
            g_post[layer] = g_post_last
        if layer % 2 == 0:
            dyc = _matmul(dy, e_w_out[i], "nt", BF16, f"d_ycat_{tag}", tn=1024)
            g_e_w_out[i] = _matmul(sv["ycat"], dy, "tn", BF16, f"g_w_out_{tag}", tn=1024).reshape(N_DEV, -1, D)
            if layer == 0:
                bw_token = start_scatter("l0_out", [("even_w_out", i)], [g_e_w_out[i]])
            dab, dag, dbg, dcv, do, delta = _even_bwd_gates(dyc, sv["z"], sv["cv"], sv["o"], f"even_gates_bwd_{tag}")
            du, dcw, g_scb[i] = _conv_bwd(dcv, sv["u"], sv["cw"] + bw_token[0, 0], SC_KERNEL, f"conv_bwd_{tag}")
            g_scw[i] = dcw[:SC_KERNEL]
            dq, dk, dv = _attn_bwd_t(sv["q"], sv["k"], sv["v"], sv["kT3"], do, sv["lse"], delta, f"attn_bwd_{tag}")
            (dqp, dkvp, dcq, dckv, dkr, g_qg[i], g_kvg[i]) = _qkv_bwd(
                dq, dk, dv, sv["z"], tabs, e_w_q_k[i], e_w_kv_k[i],
                row(even_q_norm_g[i]), row(even_kv_norm_g[i]), f"qkv_bwd_{tag}")
            gq = _matmul(sv["qn"], dqp, "tn", BF16, f"g_w_uq_{tag}", tn=1024)
            gkv = _matmul(sv["kvn"], dkvp, "tn", BF16, f"g_w_ukv_{tag}")
            g_e_w_uq[i] = jnp.moveaxis(gq.reshape(QL, HEADS, HEAD_PAD)[..., :QK_NOPE + QK_ROPE], 1, 0)
            g_e_w_ukv[i] = jnp.moveaxis(jnp.concatenate(
                [gkv[:, :HEADS * HEAD_PAD].reshape(KVL, HEADS, HEAD_PAD)[..., :QK_NOPE],
                 gkv[:, HEADS * HEAD_PAD:].reshape(KVL, HEADS, V_HEAD)], axis=-1), 1, 0)
            dz = _even_dz(dab, du, sv["z"], dag, dbg, dcq, dckv, dkr, f"even_dz_{tag}")
            gt = _matmul(dz, sv["h"], "tn", BF16, f"g_w_in_{tag}", tm=1024, tn=1024)
            g_e_w_in[i] = jnp.concatenate([gt[:2048], gt[2560:2944], gt[2944 + QK_NOPE:2944 + QK_NOPE + QK_ROPE],
                                           gt[2048:2560]], axis=0).reshape(N_DEV, -1, D)
            names = [("even_w_in", i), ("even_w_uq", i), ("even_w_ukv", i)]
            parts = [g_e_w_in[i], g_e_w_uq[i], g_e_w_ukv[i]]
            if layer == 0:
                names, parts = names + [("small", 0)], parts + [small_slab()]
            else:
                names, parts = names + [("even_w_out", i)], parts + [g_e_w_out[i]]
            bw_token = start_scatter(tag, names, parts)
            w_dh = e_w_in_k[i] + bw_token[0, 0].astype(BF16) if layer == 0 else e_w_in_k[i]
            dh = _matmul(dz, w_dh, "nn", BF16, f"d_h_{tag}", tn=1024)
        else:
            dyi = _matmul(dy, o_w_out[i], "nt", BF16, f"d_yin_{tag}", tn=1024)
            g_o_w_out[i] = _matmul(sv["yin"], dy, "tn", BF16, f"g_w_out_{tag}", tn=1024).reshape(N_DEV, -1, D)
            dcv, dsg, g_olg[i], g_olb[i] = _odd_bwd_norm(dyi, sv["cv"], sv["z"], row(o_ln_g[i]), row(o_ln_b[i]),
                                                         f"odd_norm_bwd_{tag}")
            du, dcw, g_ocb[i] = _conv_bwd(dcv, sv["u"], sv["cw"], CONF_KERNEL, f"conv_bwd_{tag}")
            g_ocw[i] = dcw[:CONF_KERNEL]
            dz = _odd_dz(du, sv["z"], dsg, f"odd_dz_{tag}")
            g_o_w_in[i] = _matmul(sv["h"], dz, "tn", BF16, f"g_w_in_{tag}", tm=1024, tn=odd_w_in.shape[-1],
                                  split_n=True)
            bw_token = start_scatter(tag, [("odd_w_in", i), ("odd_w_out", i)], [g_o_w_in[i], g_o_w_out[i]])
            dh = _matmul(dz, o_w_in[i], "nt", BF16, f"d_h_{tag}", tn=1024)
        if layer > 0:
            (dx, dy, dshift, dscale, g_pre[layer], dgate_prev, g_post[layer - 1]) = _pre_post_norm_bwd(
                dh, sv["x"], dx, vec(layer, PRE_G), vec(layer, SCALE), saved[layer - 1]["y"], vec(layer - 1, POST_G),
                vec(layer - 1, GATE), f"pre_post_norm_bwd_{tag}", after=bw_token)
        else:
            dx, dshift, dscale, g_pre[layer] = _pre_norm_bwd(dh, sv["x"], dx, vec(layer, PRE_G), vec(layer, SCALE),
                                                             f"pre_norm_bwd_{tag}", after=bw_token)
            dgate_prev = None
        dmod[layer] = jnp.concatenate([dshift, dscale, dgate], axis=-1)
        dgate = dgate_prev
    grad_x = dx.reshape(1, S, D)

    rep_g = [jnp.concatenate(dmod, 0), jnp.concatenate(g_pre, 0), jnp.concatenate(g_post, 0),
             jnp.stack(g_scb), jnp.stack(g_qg), jnp.stack(g_kvg)]
    rep_w = [ada_b, pre_norm_g, post_norm_g, even_sc_conv_b, even_q_norm_g, even_kv_norm_g]
    rep_m = [m_ada_b, m_pre_norm_g, m_post_norm_g, m_even_sc_conv_b, m_even_q_norm_g, m_even_kv_norm_g]
    rep_v = [v_ada_b, v_pre_norm_g, v_post_norm_g, v_even_sc_conv_b, v_even_q_norm_g, v_even_kv_norm_g]
    rep_shapes = [w.shape for w in rep_w]
    rep_all = _exchange([_pack(rep_g, F32, SUBLANES)], False, "gather_small_grads")[0]
    rep_out = _adamw(rep_all, _pack(rep_w, F32, SUBLANES), _pack(rep_m, F32, SUBLANES), _pack(rep_v, F32, SUBLANES),
                     "adamw_replicated")
    rep_res = [_unpack(o.reshape(-1), rep_shapes) for o in rep_out]

    dmod_all = rep_all.reshape(N_DEV, -1)[:, :L * 3 * D].reshape(N_DEV, L, 3 * D)
    dmod_cols = jnp.moveaxis(lax.dynamic_slice_in_dim(dmod_all, me * n_ada, n_ada, axis=2), 0, 1)
    g_ada_w = _ada_bwd(c_all.T, dmod_cols)
    ada_out = _adamw(g_ada_w.reshape(1, -1, n_ada), ada_w.reshape(-1, n_ada),
                     m_ada_w.reshape(-1, n_ada), v_ada_w.reshape(-1, n_ada), "adamw_ada_w")
    ada_res = [o.reshape(ada_w.shape) for o in ada_out]

    sm_m = [m_even_sc_conv_w, m_odd_conv_w, m_odd_conv_b, m_odd_ln_g, m_odd_ln_b]
    sm_v = [v_even_sc_conv_w, v_odd_conv_w, v_odd_conv_b, v_odd_ln_g, v_odd_ln_b]
    sm_shapes = [w.shape for w in sm_w]
    state = {"even_w_in": (even_w_in, m_even_w_in, v_even_w_in), "even_w_uq": (even_w_uq, m_even_w_uq, v_even_w_uq),
             "even_w_ukv": (even_w_ukv, m_even_w_ukv, v_even_w_ukv), "even_w_out": (even_w_out, m_even_w_out, v_even_w_out),
             "odd_w_in": (odd_w_in, m_odd_w_in, v_odd_w_in), "odd_w_out": (odd_w_out, m_odd_w_out, v_odd_w_out)}
    state["even_w_in"] = tuple(jnp.swapaxes(t, 1, 2) for t in state["even_w_in"])
    big_res = {name: None for name in state}
    after = [bw_token, grad_x, rep_out[0], ada_out[0]]
    sm_res = None
    for tag, names, handle, own in scatters:
        _, landed = _exchange_wait(handle, True, after, f"scatter_grads_wait_{tag}")
        after = []
        for a, (name, i) in enumerate(names):
            if name == "small":
                sm_out = _adamw([own[a], landed[a]], _pack(sm_w, F32, SUBLANES), _pack(sm_m, F32, SUBLANES),
                                _pack(sm_v, F32, SUBLANES), "adamw_small_sharded")
                sm_res = [_unpack(o.reshape(-1), sm_shapes) for o in sm_out]
                continue
            big_res[name] = _adamw_slab([own[a], landed[a]], *state[name], i, big_res[name], f"adamw_{name}_{i}")
            after += list(big_res[name])
    sh_res = [dict(zip(["even_sc_conv_w", "odd_conv_w", "odd_conv_b", "odd_ln_g", "odd_ln_b"], sm_res[kind]))
              for kind in range(4)]
    for name in state:
        for kind in range(4):
            res = big_res[name][kind]
            sh_res[kind][name] = jnp.swapaxes(res, 1, 2) if name == "even_w_in" else res

    order = ["ada_w", "ada_b", "pre_norm_g", "post_norm_g", "even_w_in", "even_sc_conv_w", "even_sc_conv_b",
             "even_q_norm_g", "even_kv_norm_g", "even_w_uq", "even_w_ukv", "even_w_out", "odd_w_in", "odd_conv_w",
             "odd_conv_b", "odd_ln_g", "odd_ln_b", "odd_w_out"]
    rep_names = ["ada_b", "pre_norm_g", "post_norm_g", "even_sc_conv_b", "even_q_norm_g", "even_kv_norm_g"]
    outs = [loss, grad_x]
    for kind in range(4):
        for name in order:
            if name == "ada_w":
                outs.append(ada_res[kind])
            elif name in rep_names:
                outs.append(rep_res[kind][rep_names.index(name)])
            else:
                outs.append(sh_res[kind][name])
    return tuple(outs)
```

```python
import math

import jax
import jax.numpy as jnp
from jax import lax
from jax.experimental import pallas as pl
from jax.experimental.pallas import tpu as pltpu

F32 = jnp.float32
BF16 = jnp.bfloat16
MESH_AXES = ("x", "y", "c")
N_DEV = 8
EPS = 1e-6
CHUNK = 64
HEADS = 8
QK_NOPE = 64
QK_ROPE = 32
V_HEAD = 64
HEAD_PAD = 128
ROPE_THETA = 10000.0
SC_KERNEL = 3
CONF_KERNEL = 31
LANES = 128
SUBLANES = 8
PACK_COLS = 1024
VMEM_LIMIT = 48 * 1024 * 1024
NEG = -1e30

ADAM_LR = 0.001
ADAM_B1 = 0.9
ADAM_B2 = 0.999
ADAM_EPS = 1e-08
ADAM_WD = 0.01
ADAM_STEP = 10


def _cparams():
    return pltpu.CompilerParams(vmem_limit_bytes=VMEM_LIMIT)


def _sigmoid(x):
    return 1.0 / (1.0 + jnp.exp(-x))


def _f32(ref):
    return ref[...].astype(F32)


def _silu(x):
    return x * _sigmoid(x)


def _dsilu(x):
    s = _sigmoid(x)
    return s * (1.0 + x * (1.0 - s))


def _rows(T, width, cb=0):
    return pl.BlockSpec((T, width), lambda i: (i, cb))


def _const(shape):
    nd = len(shape)
    return pl.BlockSpec(shape, lambda i: (0,) * nd)


def _wide_tile(S):
    return min(512, S)


def _row_tile(S):
    return min(512, S)


def _exchange(srcs, scatter, name):
    n = len(srcs)
    shapes = [tuple(s.shape[1:]) if scatter else tuple(s.shape) for s in srcs]

    def body(*refs):
        src_refs, out_refs = refs[:n], refs[n:2 * n]
        send_sems, recv_sems, local_sems = refs[2 * n:]
        x, y, c = lax.axis_index("x"), lax.axis_index("y"), lax.axis_index("c")
        me = 4 * x + 2 * y + c
        owns, copies = [], []
        for a in range(n):
            def piece(d, a=a):
                return src_refs[a].at[d] if scatter else src_refs[a]

            own = pltpu.make_async_copy(piece(me), out_refs[a].at[me], local_sems.at[a])
            own.start()
            owns.append(own)
            for k in range(1, N_DEV):
                px, py, pc = x ^ ((k >> 2) & 1), y ^ ((k >> 1) & 1), c ^ (k & 1)
                peer = 4 * px + 2 * py + pc
                sem = a * (N_DEV - 1) + k - 1
                cp = pltpu.make_async_remote_copy(
                    src_ref=piece(peer), dst_ref=out_refs[a].at[me],
                    send_sem=send_sems.at[sem], recv_sem=recv_sems.at[sem],
                    device_id=(px, py, pc), device_id_type=pl.DeviceIdType.MESH)
                cp.start()
                arrival = pltpu.make_async_remote_copy(
                    src_ref=piece(peer), dst_ref=out_refs[a].at[peer],
                    send_sem=send_sems.at[sem], recv_sem=recv_sems.at[sem],
                    device_id=(x, y, c), device_id_type=pl.DeviceIdType.MESH)
                copies.append((cp, arrival))
        for _, arrival in copies:
            arrival.wait_recv()
        for cp, _ in copies:
            cp.wait_send()
        for own in owns:
            own.wait()

    return pl.pallas_call(
        body, name=name,
        out_shape=tuple(jax.ShapeDtypeStruct((N_DEV,) + shp, s.dtype) for shp, s in zip(shapes, srcs)),
        in_specs=[pl.BlockSpec(memory_space=pl.ANY)] * n,
        out_specs=tuple(pl.BlockSpec(memory_space=pl.ANY) for _ in range(n)),
        scratch_shapes=[pltpu.SemaphoreType.DMA((n * (N_DEV - 1),)),
                        pltpu.SemaphoreType.DMA((n * (N_DEV - 1),)),
                        pltpu.SemaphoreType.DMA((n,))],
    )(*srcs)


_HBM = pl.BlockSpec(memory_space=pltpu.HBM)
_SEM = pl.BlockSpec(memory_space=pltpu.SEMAPHORE)


def _peer(k):
    x, y, c = lax.axis_index("x"), lax.axis_index("y"), lax.axis_index("c")
    return x ^ ((k >> 2) & 1), y ^ ((k >> 1) & 1), c ^ (k & 1)


def _exchange_start(srcs, scatter, name):
    n = len(srcs)
    shapes = [tuple(s.shape[1:]) if scatter else tuple(s.shape) for s in srcs]
    slots = N_DEV - 1 if scatter else N_DEV
    lands = [lax.empty((slots,) + shp, s.dtype) for shp, s in zip(shapes, srcs)]
    if not scatter:
        here = 4 * lax.axis_index("x") + 2 * lax.axis_index("y") + lax.axis_index("c")
        lands = [lax.dynamic_update_index_in_dim(l, s, here, 0) for l, s in zip(lands, srcs)]

    def body(*refs):
        src_refs, land_refs = refs[:n], refs[n:2 * n]
        send_sems, recv_sems = refs[2 * n], refs[2 * n + 1]
        token = refs[4 * n + 2]
        me = 4 * lax.axis_index("x") + 2 * lax.axis_index("y") + lax.axis_index("c")
        for a in range(n):
            for k in range(1, N_DEV):
                px, py, pc = _peer(k)
                peer = 4 * px + 2 * py + pc
                pltpu.make_async_remote_copy(
                    src_ref=src_refs[a].at[peer] if scatter else src_refs[a],
                    dst_ref=land_refs[a].at[k - 1] if scatter else land_refs[a].at[me],
                    send_sem=send_sems.at[a * (N_DEV - 1) + k - 1], recv_sem=recv_sems.at[a * (N_DEV - 1) + k - 1],
                    device_id=(px, py, pc), device_id_type=pl.DeviceIdType.MESH).start()
        token[...] = jnp.zeros_like(token)

    hbm = lambda arrs: [pltpu.HBM(a.shape, a.dtype) for a in arrs]
    out = pl.pallas_call(
        body, name=name,
        out_shape=(pltpu.SemaphoreType.DMA((n * (N_DEV - 1),)), pltpu.SemaphoreType.DMA((n * (N_DEV - 1),)),
                   *hbm(srcs), *hbm(lands), jax.ShapeDtypeStruct((SUBLANES, LANES), F32)),
        in_specs=[_HBM] * (2 * n),
        out_specs=(_SEM, _SEM, *([_HBM] * (2 * n)), pl.BlockSpec(memory_space=pltpu.VMEM)),
        input_output_aliases={a: 2 + a for a in range(2 * n)},
        compiler_params=pltpu.CompilerParams(has_side_effects=pltpu.SideEffectType.DATAFLOW_SIDE_EFFECTING),
    )(*[pltpu.with_memory_space_constraint(s, pltpu.HBM) for s in srcs],
      *[pltpu.with_memory_space_constraint(l, pltpu.HBM) for l in lands])
    return (out[0], out[1], list(out[2:2 + n]), list(out[2 + n:2 + 2 * n])), out[2 + 2 * n]


def _exchange_wait(handle, scatter, after, name):
    send_sems, recv_sems, srcs, lands = handle
    n = len(srcs)
    after = list(after) if isinstance(after, (list, tuple)) else [after]

    def body(*refs):
        src_refs, land_refs = refs[:n], refs[n:2 * n]
        send_sems, recv_sems = refs[2 * n], refs[2 * n + 1]
        for a in range(n):
            for k in range(1, N_DEV):
                px, py, pc = _peer(k)
                peer = 4 * px + 2 * py + pc
                cp = pltpu.make_async_remote_copy(
                    src_ref=src_refs[a].at[peer] if scatter else src_refs[a],
                    dst_ref=land_refs[a].at[k - 1] if scatter else land_refs[a].at[peer],
                    send_sem=send_sems.at[a * (N_DEV - 1) + k - 1], recv_sem=recv_sems.at[a * (N_DEV - 1) + k - 1],
                    device_id=(px, py, pc), device_id_type=pl.DeviceIdType.MESH)
                cp.wait_send()
                cp.wait_recv()

    out = pl.pallas_call(
        body, name=name,
        out_shape=tuple(pltpu.HBM(a.shape, a.dtype) for a in srcs + lands),
        in_specs=[_HBM] * (2 * n) + [_SEM, _SEM] + [pl.BlockSpec(memory_space=pl.ANY)] * len(after),
        out_specs=tuple([_HBM] * (2 * n)),
        input_output_aliases={a: a for a in range(2 * n)},
        compiler_params=pltpu.CompilerParams(has_side_effects=pltpu.SideEffectType.DATAFLOW_SIDE_EFFECTING),
    )(*srcs, *lands, send_sems, recv_sems, *after)
    return list(out[:n]), list(out[n:])


def _pack(parts, dtype, row_mult):
    flat = jnp.concatenate([p.reshape(-1).astype(dtype) for p in parts])
    n = flat.shape[0]
    rows = -(-n // PACK_COLS)
    rows = -(-rows // row_mult) * row_mult
    flat = jnp.pad(flat, (0, rows * PACK_COLS - n))
    return flat.reshape(rows, PACK_COLS)


def _unpack(flat, shapes):
    out, off = [], 0
    for shp in shapes:
        n = math.prod(shp)
        out.append(flat[..., off:off + n].reshape(flat.shape[:-1] + tuple(shp)))
        off += n
    return out


_DIMS = {"nn": (((1,), (0,)), ((), ())), "nt": (((1,), (1,)), ((), ())), "tn": (((0,), (0,)), ((), ()))}


def _matmul(a, b, mode, out_dtype, name, tm=512, tn=512, tk=None, split_n=False):
    if mode == "nn":
        (M, K), (_, N) = a.shape, b.shape
    elif mode == "nt":
        (M, K), (N, _) = a.shape, b.shape
    else:
        (K, M), (_, N) = a.shape, b.shape
    tm, tn = min(tm, M), min(tn, N)
    tk = K if tk is None else min(tk, K)
    nk = K // tk
    assert M % tm == 0 and N % tn == 0 and K % tk == 0, (name, a.shape, b.shape)

    def body(a_ref, b_ref, o_ref, *scratch):
        p = lax.dot_general(a_ref[...].astype(BF16), b_ref[...].astype(BF16), _DIMS[mode],
                            preferred_element_type=F32)
        if split_n:
            o_ref[0] = p.astype(out_dtype)
        elif nk == 1:
            o_ref[...] = p.astype(out_dtype)
        else:
            acc = scratch[0]
            k = pl.program_id(2)

            @pl.when(k == 0)
            def _():
                acc[...] = p

            @pl.when(k > 0)
            def _():
                acc[...] += p

            @pl.when(k == nk - 1)
            def _():
                o_ref[...] = acc[...].astype(out_dtype)

    a_spec = (pl.BlockSpec((tk, tm), lambda i, j, k: (k, i)) if mode == "tn"
              else pl.BlockSpec((tm, tk), lambda i, j, k: (i, k)))
    b_spec = (pl.BlockSpec((tn, tk), lambda i, j, k: (j, k)) if mode == "nt"
              else pl.BlockSpec((tk, tn), lambda i, j, k: (k, j)))
    return pl.pallas_call(
        body, name=name, grid=(M // tm, N // tn, nk),
        out_shape=jax.ShapeDtypeStruct((N // tn, M, tn) if split_n else (M, N), out_dtype),
        in_specs=[a_spec, b_spec],
        out_specs=(pl.BlockSpec((1, tm, tn), lambda i, j, k: (j, i, 0)) if split_n
                   else pl.BlockSpec((tm, tn), lambda i, j, k: (i, j))),
        scratch_shapes=[pltpu.VMEM((tm, tn), F32)] if nk > 1 else [],
        compiler_params=_cparams(),
    )(a, b)


def _ada_fwd(c_all, ada_w, ada_b_cols):
    L, D, n = ada_w.shape

    def body(c_ref, w_ref, b_ref, o_ref):
        act = _silu(c_ref[...]).astype(BF16)
        o_ref[0] = jnp.dot(act, w_ref[0].astype(BF16), preferred_element_type=F32) + b_ref[0]

    return pl.pallas_call(
        body, name="ada_fwd", grid=(L,),
        out_shape=jax.ShapeDtypeStruct((L, N_DEV, n), F32),
        in_specs=[pl.BlockSpec((N_DEV, D), lambda l: (0, 0)),
                  pl.BlockSpec((1, D, n), lambda l: (l, 0, 0)),
                  pl.BlockSpec((1, 1, n), lambda l: (l, 0, 0))],
        out_specs=pl.BlockSpec((1, N_DEV, n), lambda l: (l, 0, 0)),
        compiler_params=_cparams(),
    )(c_all, ada_w, ada_b_cols)


def _ada_bwd(c_all_t, dmod_cols):
    D = c_all_t.shape[0]
    L, _, n = dmod_cols.shape

    def body(c_ref, d_ref, o_ref):
        act = _silu(c_ref[...])
        dm = d_ref[0]
        acc = act[:, 0:1] * dm[0:1, :]
        for b in range(1, N_DEV):
            acc = acc + act[:, b:b + 1] * dm[b:b + 1, :]
        o_ref[0] = acc

    return pl.pallas_call(
        body, name="ada_bwd", grid=(L,),
        out_shape=jax.ShapeDtypeStruct((L, D, n), F32),
        in_specs=[pl.BlockSpec((D, N_DEV), lambda l: (0, 0)),
                  pl.BlockSpec((1, N_DEV, n), lambda l: (l, 0, 0))],
        out_specs=pl.BlockSpec((1, D, n), lambda l: (l, 0, 0)),
        compiler_params=_cparams(),
    )(c_all_t, dmod_cols)


def _rope_tables(pos_col, inv_lane):
    S = pos_col.shape[0]
    T = _row_tile(S)
    half = QK_ROPE // 2

    def body(p_ref, f_ref, c_ref, up_ref, dn_ref):
        ang = p_ref[...] * f_ref[...]
        lane = lax.broadcasted_iota(jnp.int32, ang.shape, 1)
        first = (lane >= QK_NOPE) & (lane < QK_NOPE + half)
        second = (lane >= QK_NOPE + half) & (lane < QK_NOPE + QK_ROPE)
        cs, sn = jnp.cos(ang), jnp.sin(ang)
        c_ref[...] = jnp.where(first | second, cs, 1.0)
        up_ref[...] = jnp.where(first, -sn, 0.0)
        dn_ref[...] = jnp.where(second, sn, 0.0)

    tab = jax.ShapeDtypeStruct((S, HEAD_PAD), F32)
    return pl.pallas_call(
        body, name="rope_tables", grid=(S // T,),
        out_shape=(tab, tab, tab),
        in_specs=[_rows(T, 1), _const((1, HEAD_PAD))],
        out_specs=(_rows(T, HEAD_PAD),) * 3,
        compiler_params=_cparams(),
    )(pos_col, inv_lane)


def _rope(blk, ct, ut, dt):
    half = QK_ROPE // 2
    up = pltpu.roll(blk, HEAD_PAD - half, 1)
    dn = pltpu.roll(blk, half, 1)
    return blk * ct + up * ut + dn * dt


def _rope_t(d, ct, ut, dt):
    half = QK_ROPE // 2
    return d * ct + pltpu.roll(d * ut, half, 1) + pltpu.roll(d * dt, HEAD_PAD - half, 1)


def _row_operands(rows, after=None):
    ops, specs = [], []
    for a in rows:
        if isinstance(a, tuple):
            table, r = a
            ops.append(table)
            specs.append(pl.BlockSpec((None, 1, table.shape[-1]), lambda *_, r=r: (r, 0, 0)))
        else:
            ops.append(a)
            specs.append(pl.BlockSpec(a.shape, lambda *_, nd=a.ndim: (0,) * nd))
    if after is not None:
        ops.append(after)
        specs.append(pl.BlockSpec(memory_space=pl.ANY))
    return ops, specs


def _pre_norm(x, g, scale, shift, name, after=None):
    S, D = x.shape
    T = _row_tile(S)
    row_ops, row_specs = _row_operands([g, scale, shift], after)

    def body(x_ref, g_ref, sc_ref, sh_ref, *rest):
        h_ref = rest[-1]
        xv = x_ref[...]
        rstd = lax.rsqrt(jnp.mean(xv * xv, axis=-1, keepdims=True) + EPS)
        h_ref[...] = ((xv * rstd) * g_ref[...] * (1.0 + sc_ref[...]) + sh_ref[...]).astype(BF16)

    return pl.pallas_call(
        body, name=name, grid=(S // T,),
        out_shape=jax.ShapeDtypeStruct((S, D), BF16),
        in_specs=[_rows(T, D)] + row_specs,
        out_specs=_rows(T, D), compiler_params=_cparams(),
    )(x, *row_ops)


def _fold8(v):
    T, C = v.shape
    return v.reshape(T // SUBLANES, SUBLANES, C).sum(axis=0)


def _col_sums(n_sums, body_fn, ins, in_specs, outs, out_specs, S, T, widths, name):
    n_in, n_out = len(ins), len(outs)
    nt = S // T

    def body(*refs):
        in_refs = refs[:n_in]
        out_refs = refs[n_in:n_in + n_out]
        sum_refs = refs[n_in + n_out:n_in + n_out + n_sums]
        accs = refs[n_in + n_out + n_sums:]
        i = pl.program_id(0)
        terms = body_fn(in_refs, out_refs)

        @pl.when(i == 0)
        def _():
            for acc, t in zip(accs, terms):
                acc[...] = _fold8(t)

        @pl.when(i > 0)
        def _():
            for acc, t in zip(accs, terms):
                acc[...] += _fold8(t)

        @pl.when(i == nt - 1)
        def _():
            for acc, s_ref in zip(accs, sum_refs):
                s_ref[...] = jnp.sum(acc[...], axis=0, keepdims=True)

    return pl.pallas_call(
        body, name=name, grid=(nt,),
        out_shape=tuple(outs) + tuple(jax.ShapeDtypeStruct((1, w), F32) for w in widths),
        in_specs=in_specs,
        out_specs=tuple(out_specs) + tuple(_const((1, w)) for w in widths),
        scratch_shapes=[pltpu.VMEM((SUBLANES, w), F32) for w in widths],
        compiler_params=_cparams(),
    )(*ins)


def _pre_norm_bwd(dh, x, dxo, g, scale, name, after=None):
    S, D = x.shape
    T = _row_tile(S)
    row_ops, row_specs = _row_operands([g, scale], after)

    def fn(ins, outs):
        dh_ref, x_ref, dxo_ref, g_ref, sc_ref = ins[:5]
        xv, dv = x_ref[...], _f32(dh_ref)
        rstd = lax.rsqrt(jnp.mean(xv * xv, axis=-1, keepdims=True) + EPS)
        xh = xv * rstd
        dr = dv * (1.0 + sc_ref[...])
        dxh = dr * g_ref[...]
        outs[0][...] = dxo_ref[...] + rstd * (dxh - xh * jnp.mean(dxh * xh, axis=-1, keepdims=True))
        return [dv, dv * (xh * g_ref[...]), dr * xh]

    return _col_sums(3, fn, [dh, x, dxo] + row_ops,
                     [_rows(T, D), _rows(T, D), _rows(T, D)] + row_specs,
                     [jax.ShapeDtypeStruct((S, D), F32)], [_rows(T, D)], S, T, [D, D, D], name)


def _post_pre_norm(x, y, g_post, gate, g_pre, scale, shift, name):
    S, D = x.shape
    T = _row_tile(S)
    row_ops, row_specs = _row_operands([g_post, gate, g_pre, scale, shift])

    def body(x_ref, y_ref, gp_ref, gt_ref, g_ref, sc_ref, sh_ref, xn_ref, h_ref):
        yv = _f32(y_ref)
        rstd_y = lax.rsqrt(jnp.mean(yv * yv, axis=-1, keepdims=True) + EPS)
        xn = x_ref[...] + gt_ref[...] * ((yv * rstd_y) * gp_ref[...])
        xn_ref[...] = xn
        rstd = lax.rsqrt(jnp.mean(xn * xn, axis=-1, keepdims=True) + EPS)
        h_ref[...] = ((xn * rstd) * g_ref[...] * (1.0 + sc_ref[...]) + sh_ref[...]).astype(BF16)

    return pl.pallas_call(
        body, name=name, grid=(S // T,),
        out_shape=(jax.ShapeDtypeStruct((S, D), F32), jax.ShapeDtypeStruct((S, D), BF16)),
        in_specs=[_rows(T, D), _rows(T, D)] + row_specs,
        out_specs=(_rows(T, D), _rows(T, D)), compiler_params=_cparams(),
    )(x, y, *row_ops)


def _pre_post_norm_bwd(dh, x, dxo, g_pre, scale, y_prev, g_post_prev, gate_prev, name, after=None):
    S, D = x.shape
    T = _row_tile(S)
    row_ops, row_specs = _row_operands([g_pre, scale, g_post_prev, gate_prev], after)

    def fn(ins, outs):
        dh_ref, x_ref, dxo_ref, y_ref, g_ref, sc_ref, gp_ref, gt_ref = ins[:8]
        xv, dv = x_ref[...], _f32(dh_ref)
        rstd = lax.rsqrt(jnp.mean(xv * xv, axis=-1, keepdims=True) + EPS)
        xh = xv * rstd
        dr = dv * (1.0 + sc_ref[...])
        dxh = dr * g_ref[...]
        dx = dxo_ref[...] + rstd * (dxh - xh * jnp.mean(dxh * xh, axis=-1, keepdims=True))
        outs[0][...] = dx
        yv = _f32(y_ref)
        rstd_y = lax.rsqrt(jnp.mean(yv * yv, axis=-1, keepdims=True) + EPS)
        yh = yv * rstd_y
        dn = dx * gt_ref[...]
        dyh = dn * gp_ref[...]
        outs[1][...] = (rstd_y * (dyh - yh * jnp.mean(dyh * yh, axis=-1, keepdims=True))).astype(BF16)
        return [dv, dv * (xh * g_ref[...]), dr * xh, dx * (yh * gp_ref[...]), dn * yh]

    return _col_sums(5, fn, [dh, x, dxo, y_prev] + row_ops,
                     [_rows(T, D), _rows(T, D), _rows(T, D), _rows(T, D)] + row_specs,
                     [jax.ShapeDtypeStruct((S, D), F32), jax.ShapeDtypeStruct((S, D), BF16)],
                     [_rows(T, D), _rows(T, D)], S, T, [D] * 5, name)


def _loss_post_norm_bwd(x, y, g_post, gate, target, name):
    S, D = x.shape
    T = _row_tile(S)
    row_ops, row_specs = _row_operands([g_post, gate])

    def fn(ins, outs):
        x_ref, y_ref, t_ref, gp_ref, gt_ref = ins
        yv = _f32(y_ref)
        rstd_y = lax.rsqrt(jnp.mean(yv * yv, axis=-1, keepdims=True) + EPS)
        yh = yv * rstd_y
        e = x_ref[...] + gt_ref[...] * (yh * gp_ref[...]) - t_ref[...]
        dx = e * (1.0 / D)
        outs[0][...] = dx
        dn = dx * gt_ref[...]
        dyh = dn * gp_ref[...]
        outs[1][...] = (rstd_y * (dyh - yh * jnp.mean(dyh * yh, axis=-1, keepdims=True))).astype(BF16)
        return [e * e, dx * (yh * gp_ref[...]), dn * yh]

    return _col_sums(3, fn, [x, y, target] + row_ops,
                     [_rows(T, D), _rows(T, D), _rows(T, D)] + row_specs,
                     [jax.ShapeDtypeStruct((S, D), F32), jax.ShapeDtypeStruct((S, D), BF16)],
                     [_rows(T, D), _rows(T, D)], S, T, [D] * 3, name)


def _scaled_total(v, coef, name):
    def body(v_ref, o_ref):
        o_ref[...] = jnp.broadcast_to(jnp.sum(v_ref[...], axis=1, keepdims=True) * coef, (1, LANES))

    return pl.pallas_call(body, name=name, out_shape=jax.ShapeDtypeStruct((1, LANES), F32))(v)


CONV_ROWS = 64


def _conv_halo(K):
    return SUBLANES if K - 1 <= SUBLANES else 32


def _conv_fwd(u, w, b, K, name):
    S, C = u.shape
    KP = w.shape[0]
    T, HB, RS = min(512, S), _conv_halo(K), CONV_ROWS
    ratio = T // HB

    def body(u_ref, h_ref, w_ref, b_ref, o_ref, ext):
        i = pl.program_id(1)
        ext[0:HB, :] = jnp.where(i > 0, h_ref[...], 0.0)
        ext[HB:HB + T, :] = u_ref[...]
        for r0 in range(0, T, RS):
            acc = jnp.broadcast_to(b_ref[...], (RS, LANES))
            for k in range(K):
                off = HB - (K - 1) + k + r0
                acc = acc + w_ref[k:k + 1, :] * ext[off:off + RS, :]
            o_ref[r0:r0 + RS, :] = acc

    return pl.pallas_call(
        body, name=name, grid=(C // LANES, S // T),
        out_shape=jax.ShapeDtypeStruct((S, C), F32),
        in_specs=[pl.BlockSpec((T, LANES), lambda c, i: (i, c)),
                  pl.BlockSpec((HB, LANES), lambda c, i: (jnp.maximum(i * ratio - 1, 0), c)),
                  pl.BlockSpec((KP, LANES), lambda c, i: (0, c)),
                  pl.BlockSpec((1, LANES), lambda c, i: (0, c))],
        out_specs=pl.BlockSpec((T, LANES), lambda c, i: (i, c)),
        scratch_shapes=[pltpu.VMEM((HB + T, LANES), F32)],
        compiler_params=_cparams(),
    )(u, u, w, b)


def _conv_bwd(d, u, w, K, name):
    S, C = u.shape
    KP = w.shape[0]
    T, HB, RS = min(512, S), _conv_halo(K), CONV_ROWS
    ratio = T // HB
    nt = S // T
    last_halo = S // HB - 1

    def body(d_ref, dn_ref, u_ref, up_ref, w_ref, du_ref, dw_ref, db_ref, extd, extu, dws, dbs):
        i = pl.program_id(1)
        extd[0:T, :] = d_ref[...]
        extd[T:T + HB, :] = jnp.where(i < nt - 1, dn_ref[...], 0.0)
        extu[0:HB, :] = jnp.where(i > 0, up_ref[...], 0.0)
        extu[HB:HB + T, :] = u_ref[...]

        @pl.when(i == 0)
        def _():
            dws[...] = jnp.zeros_like(dws)
            dbs[...] = jnp.zeros_like(dbs)

        for r0 in range(0, T, RS):
            acc = jnp.zeros((RS, LANES), F32)
            for k in range(K):
                off = (K - 1 - k) + r0
                acc = acc + w_ref[k:k + 1, :] * extd[off:off + RS, :]
            du_ref[r0:r0 + RS, :] = acc
            dch = d_ref[r0:r0 + RS, :]
            dbs[...] += _fold8(dch)
            for k in range(K):
                off = HB - (K - 1) + k + r0
                dws[k * SUBLANES:(k + 1) * SUBLANES, :] += _fold8(dch * extu[off:off + RS, :])

        @pl.when(i == nt - 1)
        def _():
            dw_ref[...] = jnp.zeros_like(dw_ref)
            for k in range(K):
                dw_ref[k:k + 1, :] = jnp.sum(dws[k * SUBLANES:(k + 1) * SUBLANES, :], axis=0, keepdims=True)
            db_ref[...] = jnp.sum(dbs[...], axis=0, keepdims=True)

    return pl.pallas_call(
        body, name=name, grid=(C // LANES, nt),
        out_shape=(jax.ShapeDtypeStruct((S, C), F32), jax.ShapeDtypeStruct((KP, C), F32),
                   jax.ShapeDtypeStruct((1, C), F32)),
        in_specs=[pl.BlockSpec((T, LANES), lambda c, i: (i, c)),
                  pl.BlockSpec((HB, LANES), lambda c, i: (jnp.minimum((i + 1) * ratio, last_halo), c)),
                  pl.BlockSpec((T, LANES), lambda c, i: (i, c)),
                  pl.BlockSpec((HB, LANES), lambda c, i: (jnp.maximum(i * ratio - 1, 0), c)),
                  pl.BlockSpec((KP, LANES), lambda c, i: (0, c))],
        out_specs=(pl.BlockSpec((T, LANES), lambda c, i: (i, c)),
                   pl.BlockSpec((KP, LANES), lambda c, i: (0, c)),
                   pl.BlockSpec((1, LANES), lambda c, i: (0, c))),
        scratch_shapes=[pltpu.VMEM((T + HB, LANES), F32), pltpu.VMEM((HB + T, LANES), F32),
                        pltpu.VMEM((KP * SUBLANES, LANES), F32), pltpu.VMEM((SUBLANES, LANES), F32)],
        compiler_params=_cparams(),
    )(d, d, u, u, w)


SCW = 512
ZE = 3072
QL = 256
KVL = 128


def _rms_rows(x, g):
    rstd = lax.rsqrt(jnp.mean(x * x, axis=-1, keepdims=True) + EPS)
    return (x * rstd) * g


def _attn_tile(S):
    return min(256, S)


_NT = (((1,), (1,)), ((), ()))


LOG2E = math.log2(math.e)
ATTN_FWD_HEADS = 8
ATTN_BWD_HEADS = 4


def _chunk_mask_t(T):
    key = lax.broadcasted_iota(jnp.int32, (T, T), 0) // CHUNK
    qry = lax.broadcasted_iota(jnp.int32, (T, T), 1) // CHUNK
    return key <= qry


W_IN_ROWS = 512


def _w_in_even(h, w_t, qg, kvg, name):
    S, D = h.shape
    tm = min(W_IN_ROWS, S)

    def body(h_ref, w_ref, qg_ref, kvg_ref, z_ref, u_ref, qn_ref, kvn_ref):
        p = lax.dot_general(h_ref[...], w_ref[...], _NT, preferred_element_type=F32)
        z_ref[...] = p.astype(BF16)
        u_ref[...] = p[:, SCW:2 * SCW] * p[:, 2 * SCW:3 * SCW]
        qn_ref[...] = _rms_rows(p[:, 5 * SCW:5 * SCW + QL], qg_ref[...]).astype(BF16)
        kvn_ref[...] = _rms_rows(p[:, 5 * SCW + QL:5 * SCW + QL + KVL], kvg_ref[...]).astype(BF16)

    return pl.pallas_call(
        body, name=name, grid=(S // tm,),
        out_shape=(jax.ShapeDtypeStruct((S, ZE), BF16), jax.ShapeDtypeStruct((S, SCW), F32),
                   jax.ShapeDtypeStruct((S, QL), BF16), jax.ShapeDtypeStruct((S, KVL), BF16)),
        in_specs=[_rows(tm, D), _const(w_t.shape), _const((1, QL)), _const((1, KVL))],
        out_specs=(_rows(tm, ZE), _rows(tm, SCW), _rows(tm, QL), _rows(tm, KVL)),
        compiler_params=_cparams(),
    )(h, w_t, qg, kvg)


def _w_in_odd(h, w, name):
    S, D = h.shape
    tm = min(W_IN_ROWS, S)

    def body(h_ref, w_ref, z_ref, u_ref):
        p = jnp.dot(h_ref[...], w_ref[...], preferred_element_type=F32)
        z_ref[...] = p.astype(BF16)
        u_ref[...] = p[:, 0:D] * _sigmoid(p[:, D:2 * D])

    return pl.pallas_call(
        body, name=name, grid=(S // tm,),
        out_shape=(jax.ShapeDtypeStruct((S, 3 * D), BF16), jax.ShapeDtypeStruct((S, D), F32)),
        in_specs=[_rows(tm, D), _const(w.shape)],
        out_specs=(_rows(tm, 3 * D), _rows(tm, D)),
        compiler_params=_cparams(),
    )(h, w)


def _qkv_fwd_t(qn, kvn, z, tabs, w_q, w_kv, name):
    S = qn.shape[0]
    T = _attn_tile(S)
    HW = HEADS * HEAD_PAD
    scale = LOG2E / math.sqrt(QK_NOPE + QK_ROPE)

    def body(qn_ref, kvn_ref, kr_ref, ct_ref, ut_ref, dt_ref, wq_ref, wkv_ref, q_ref, k_ref, v_ref, kt_ref, vt_ref):
        ct, ut, dt = ct_ref[...], ut_ref[...], dt_ref[...]
        qa = jnp.dot(qn_ref[...], wq_ref[...], preferred_element_type=F32)
        kva = jnp.dot(kvn_ref[...], wkv_ref[...], preferred_element_type=F32)
        kr = _f32(kr_ref)
        ones_row = (lax.broadcasted_iota(jnp.int32, (V_HEAD, T), 0) == 0).astype(F32)
        for h in range(HEADS):
            sl = slice(h * HEAD_PAD, (h + 1) * HEAD_PAD)
            q_ref[:, sl] = (_rope(qa[:, sl], ct, ut, dt) * scale).astype(BF16)
            kh = _rope(kva[:, sl] + kr, ct, ut, dt)
            k_ref[:, sl] = kh.astype(BF16)
            kt_ref[0, sl, :] = kh.T.astype(BF16)
        v_ref[...] = kva[:, HW:].astype(BF16)
        for p in range(HEADS // 2):
            vpt = kva[:, HW + p * LANES:HW + (p + 1) * LANES].T
            for h in range(2):
                r0 = (2 * p + h) * HEAD_PAD
                vt_ref[0, r0:r0 + V_HEAD, :] = vpt[h * V_HEAD:(h + 1) * V_HEAD, :].astype(BF16)
                vt_ref[0, r0 + V_HEAD:r0 + HEAD_PAD, :] = ones_row.astype(BF16)

    t3 = jax.ShapeDtypeStruct((S // T, HW, T), BF16)
    return pl.pallas_call(
        body, name=name, grid=(S // T,),
        out_shape=(jax.ShapeDtypeStruct((S, HW), BF16), jax.ShapeDtypeStruct((S, HW), BF16),
                   jax.ShapeDtypeStruct((S, HEADS * V_HEAD), BF16), t3, t3),
        in_specs=[_rows(T, QL), _rows(T, KVL), _rows(T, HEAD_PAD, 23),
                  _rows(T, HEAD_PAD), _rows(T, HEAD_PAD), _rows(T, HEAD_PAD),
                  _const(w_q.shape), _const(w_kv.shape)],
        out_specs=(_rows(T, HW), _rows(T, HW), _rows(T, HEADS * V_HEAD),
                   pl.BlockSpec((1, HW, T), lambda i: (i, 0, 0)), pl.BlockSpec((1, HW, T), lambda i: (i, 0, 0))),
        compiler_params=_cparams(),
    )(qn, kvn, z, *tabs, w_q, w_kv)


def _attn_fwd_t(q, k, vT3, name):
    S = q.shape[0]
    T = _attn_tile(S)
    nq = S // T
    NH = ATTN_FWD_HEADS
    PW = NH * HEAD_PAD

    def body(q_ref, k_ref, vt_ref, o_ref, lse_ref, m_s, acc_s):
        i = pl.program_id(1)
        m_s[...] = jnp.full_like(m_s, NEG)
        acc_s[...] = jnp.zeros_like(acc_s)
        qv = q_ref[...]

        def step(j, masked):
            kb = k_ref[pl.ds(pl.multiple_of(j * T, T), T), :]
            vt = vt_ref[j]
            heads = [slice(h * HEAD_PAD, (h + 1) * HEAD_PAD) for h in range(NH)]
            sts = [lax.dot_general(kb[:, sl], qv[:, sl], _NT, preferred_element_type=F32) for sl in heads]
            alphas, pvs = [], []
            for h, sl in enumerate(heads):
                st = jnp.where(_chunk_mask_t(T), sts[h], NEG) if masked else sts[h]
                m_prev = m_s[h]
                m_new = jnp.maximum(m_prev, jnp.max(st, axis=0, keepdims=True))
                alphas.append(jnp.exp2(m_prev[0:1] - m_new[0:1]))
                pt = jnp.exp2(st - m_new[0:1]).astype(BF16)
                m_s[h] = m_new
                pvs.append(jnp.dot(vt[sl, :], pt, preferred_element_type=F32))
            for h in range(NH):
                acc_s[h] = acc_s[h] * alphas[h] + pvs[h]

        def loop_body(j, carry):
            step(j, False)
            return carry

        lax.fori_loop(0, i, loop_body, 0)
        step(i, True)
        for g in range(NH // 2):
            outs = []
            for h in (2 * g, 2 * g + 1):
                acc = acc_s[h]
                l_row = acc[V_HEAD:V_HEAD + 1, :]
                outs.append(acc[0:V_HEAD, :] / l_row)
                lse_ref[0, h * SUBLANES:(h + 1) * SUBLANES, :] = m_s[h] + jnp.log2(l_row)
            o_ref[:, g * LANES:(g + 1) * LANES] = jnp.concatenate(outs, axis=0).T

    return pl.pallas_call(
        body, name=name, grid=(HEADS // NH, nq),
        out_shape=(jax.ShapeDtypeStruct((S, HEADS * V_HEAD), F32),
                   jax.ShapeDtypeStruct((nq, HEADS * SUBLANES, T), F32)),
        in_specs=[pl.BlockSpec((T, PW), lambda p, i: (i, p)),
                  pl.BlockSpec((S, PW), lambda p, i: (0, p)),
                  pl.BlockSpec((nq, PW, T), lambda p, i: (0, p, 0))],
        out_specs=(pl.BlockSpec((T, NH * V_HEAD), lambda p, i: (i, p)),
                   pl.BlockSpec((1, NH * SUBLANES, T), lambda p, i: (i, p, 0))),
        scratch_shapes=[pltpu.VMEM((NH, SUBLANES, T), F32), pltpu.VMEM((NH, HEAD_PAD, T), F32)],
        compiler_params=_cparams(),
    )(q, k, vT3)


def _attn_bwd_t(q, k, v, kT3, do, lse3, dl3, name):
    S = q.shape[0]
    T = _attn_tile(S)
    nq = S // T
    NH = ATTN_BWD_HEADS
    PW = NH * HEAD_PAD
    VW = NH * V_HEAD

    def body(q_ref, k_ref, v_ref, kt_ref, do_ref, lse_ref, dl_ref, dq_ref, dk_ref, dv_ref, dk_s, dv_s):
        j = pl.program_id(1)
        left = lax.broadcasted_iota(jnp.int32, (T, LANES), 1) < V_HEAD

        @pl.when(j == 0)
        def _():
            dq_ref[...] = jnp.zeros_like(dq_ref)

        dk_s[...] = jnp.zeros_like(dk_s)
        dv_s[...] = jnp.zeros_like(dv_s)
        kb = k_ref[...]
        vms = []
        for g in range(NH // 2):
            vb = v_ref[:, g * LANES:(g + 1) * LANES]
            vms += [jnp.where(left, vb, jnp.zeros_like(vb)), jnp.where(left, jnp.zeros_like(vb), vb)]
        kt = kt_ref[0]

        def step(i, masked):
            r0 = pl.multiple_of(i * T, T)
            qb = q_ref[pl.ds(r0, T), :]
            do_all = do_ref[pl.ds(r0, T), :]
            lse = lse_ref[i]
            dl = dl_ref[i]
            heads = [slice(h * HEAD_PAD, (h + 1) * HEAD_PAD) for h in range(NH)]
            dobs = [do_all[:, (h // 2) * LANES:(h // 2 + 1) * LANES] for h in range(NH)]
            sts = [lax.dot_general(kb[:, sl], qb[:, sl], _NT, preferred_element_type=F32) for sl in heads]
            dpts = [lax.dot_general(vms[h], dobs[h], _NT, preferred_element_type=F32) for h in range(NH)]
            res = []
            for h, sl in enumerate(heads):
                r8 = h * SUBLANES
                pt = jnp.exp2(sts[h] - lse[r8:r8 + 1, :])
                if masked:
                    pt = jnp.where(_chunk_mask_t(T), pt, 0.0)
                dst = (pt * (dpts[h] - dl[r8:r8 + 1, :])).astype(BF16)
                res.append((jnp.dot(pt.astype(BF16), dobs[h], preferred_element_type=F32),
                            jnp.dot(dst, qb[:, sl], preferred_element_type=F32),
                            jnp.dot(kt[sl, :], dst, preferred_element_type=F32)))
            for h, sl in enumerate(heads):
                dv_s[h] += res[h][0]
                dk_s[:, sl] += res[h][1]
                dq_ref[i, sl, :] += res[h][2]

        def loop_body(i, carry):
            step(i, False)
            return carry

        step(j, True)
        lax.fori_loop(j + 1, nq, loop_body, 0)
        dk_ref[...] = dk_s[...] * (1.0 / LOG2E)
        for g in range(NH // 2):
            dv_ref[:, g * LANES:(g + 1) * LANES] = jnp.where(left, dv_s[2 * g], dv_s[2 * g + 1])

    return pl.pallas_call(
        body, name=name, grid=(HEADS // NH, nq),
        out_shape=(jax.ShapeDtypeStruct((nq, HEADS * HEAD_PAD, T), F32),
                   jax.ShapeDtypeStruct((S, HEADS * HEAD_PAD), F32), jax.ShapeDtypeStruct((S, HEADS * V_HEAD), F32)),
        in_specs=[pl.BlockSpec((S, PW), lambda p, j: (0, p)),
                  pl.BlockSpec((T, PW), lambda p, j: (j, p)),
                  pl.BlockSpec((T, VW), lambda p, j: (j, p)),
                  pl.BlockSpec((1, PW, T), lambda p, j: (j, p, 0)),
                  pl.BlockSpec((S, VW), lambda p, j: (0, p)),
                  pl.BlockSpec((nq, NH * SUBLANES, T), lambda p, j: (0, p, 0)),
                  pl.BlockSpec((nq, NH * SUBLANES, T), lambda p, j: (0, p, 0))],
        out_specs=(pl.BlockSpec((nq, PW, T), lambda p, j: (0, p, 0)),
                   pl.BlockSpec((T, PW), lambda p, j: (j, p)),
                   pl.BlockSpec((T, VW), lambda p, j: (j, p))),
        scratch_shapes=[pltpu.VMEM((T, PW), F32), pltpu.VMEM((NH, T, LANES), F32)],
        compiler_params=_cparams(),
    )(q, k, v, kT3, do, lse3, dl3)


def _even_post(z, cv, o, name):
    S = z.shape[0]
    T = _wide_tile(S)

    def body(ab_ref, ag_ref, bg_ref, cv_ref, o_ref, y_ref):
        y_ref[:, 0:SCW] = (_f32(ab_ref) * cv_ref[...] * _silu(_f32(ag_ref))).astype(BF16)
        y_ref[:, SCW:2 * SCW] = (o_ref[...] * _silu(_f32(bg_ref))).astype(BF16)

    return pl.pallas_call(
        body, name=name, grid=(S // T,),
        out_shape=jax.ShapeDtypeStruct((S, 2 * SCW), BF16),
        in_specs=[_rows(T, SCW, 0), _rows(T, SCW, 3), _rows(T, SCW, 4), _rows(T, SCW), _rows(T, SCW)],
        out_specs=_rows(T, 2 * SCW), compiler_params=_cparams(),
    )(z, z, z, cv, o)


def _even_bwd_gates(dyc, z, cv, o, name):
    S = z.shape[0]
    T = _attn_tile(S)

    def body(dya_ref, dyb_ref, ab_ref, ag_ref, bg_ref, cv_ref, o_ref,
             dab_ref, dag_ref, dbg_ref, dcv_ref, do_ref, dl_ref):
        dya, ab, ag, cv = _f32(dya_ref), _f32(ab_ref), _f32(ag_ref), cv_ref[...]
        sg = _silu(ag)
        dab_ref[...] = (dya * cv * sg).astype(BF16)
        dcv_ref[...] = dya * ab * sg
        dag_ref[...] = (dya * ab * cv * _dsilu(ag)).astype(BF16)
        dyb, bg, ov = _f32(dyb_ref), _f32(bg_ref), o_ref[...]
        dov = dyb * _silu(bg)
        do_ref[...] = dov.astype(BF16)
        dbg_ref[...] = (dyb * ov * _dsilu(bg)).astype(BF16)
        prod = dov * ov
        left = lax.broadcasted_iota(jnp.int32, (T, LANES), 1) < V_HEAD
        for p in range(HEADS // 2):
            blk = prod[:, p * LANES:(p + 1) * LANES]
            s0 = jnp.sum(jnp.where(left, blk, 0.0), axis=1, keepdims=True)
            s1 = jnp.sum(jnp.where(left, 0.0, blk), axis=1, keepdims=True)
            dt = jnp.where(left, s0, s1).T
            dl_ref[0, 2 * p * SUBLANES:(2 * p + 1) * SUBLANES, :] = dt[0:SUBLANES, :]
            dl_ref[0, (2 * p + 1) * SUBLANES:(2 * p + 2) * SUBLANES, :] = dt[V_HEAD:V_HEAD + SUBLANES, :]

    assert T == _attn_tile(S)
    bf = jax.ShapeDtypeStruct((S, SCW), BF16)
    ff = jax.ShapeDtypeStruct((S, SCW), F32)
    return pl.pallas_call(
        body, name=name, grid=(S // T,),
        out_shape=(bf, bf, bf, ff, bf, jax.ShapeDtypeStruct((S // T, HEADS * SUBLANES, T), F32)),
        in_specs=[_rows(T, SCW, 0), _rows(T, SCW, 1), _rows(T, SCW, 0), _rows(T, SCW, 3), _rows(T, SCW, 4),
                  _rows(T, SCW), _rows(T, SCW)],
        out_specs=(_rows(T, SCW),) * 5 + (pl.BlockSpec((1, HEADS * SUBLANES, T), lambda i: (i, 0, 0)),),
        compiler_params=_cparams(),
    )(dyc, dyc, z, z, z, cv, o)


def _qkv_bwd(dq, dk, dv, z, tabs, w_q, w_kv, qg, kvg, name):
    S = dk.shape[0]
    T = _attn_tile(S)
    HW = HEADS * HEAD_PAD
    VW = HEADS * V_HEAD
    scale = 1.0 / math.sqrt(QK_NOPE + QK_ROPE)

    def fn(ins, outs):
        dq_ref, dk_ref, dv_ref, cq_ref, ckv_ref, ct_ref, ut_ref, dt_ref, wq_ref, wkv_ref, qg_ref, kvg_ref = ins
        dqp_ref, dkvp_ref, dcq_ref, dckv_ref, dkr_ref = outs
        ct, ut, dt = ct_ref[...], ut_ref[...], dt_ref[...]
        dkr = jnp.zeros((T, HEAD_PAD), F32)
        for h in range(HEADS):
            sl = slice(h * HEAD_PAD, (h + 1) * HEAD_PAD)
            dqp_ref[:, sl] = (_rope_t(dq_ref[0, sl, :].T, ct, ut, dt) * scale).astype(BF16)
            dkh = _rope_t(dk_ref[:, sl], ct, ut, dt)
            dkr = dkr + dkh
            dkvp_ref[:, sl] = dkh.astype(BF16)
        dkvp_ref[:, HW:] = dv_ref[...].astype(BF16)
        dkr_ref[...] = dkr.astype(BF16)
        sums = []
        for lat_ref, g_ref, dpre_ref, w_ref, dlat_ref in ((cq_ref, qg_ref, dqp_ref, wq_ref, dcq_ref),
                                                         (ckv_ref, kvg_ref, dkvp_ref, wkv_ref, dckv_ref)):
            dn = lax.dot_general(dpre_ref[...], w_ref[...], _NT, preferred_element_type=F32)
            xv = _f32(lat_ref)
            rstd = lax.rsqrt(jnp.mean(xv * xv, axis=-1, keepdims=True) + EPS)
            xh = xv * rstd
            dxh = dn * g_ref[...]
            dlat_ref[...] = (rstd * (dxh - xh * jnp.mean(dxh * xh, axis=-1, keepdims=True))).astype(BF16)
            sums.append(dn * xh)
        return sums

    return _col_sums(
        2, fn, [dq, dk, dv, z, z, *tabs, w_q, w_kv, qg, kvg],
        [pl.BlockSpec((1, HW, T), lambda i: (i, 0, 0)), _rows(T, HW), _rows(T, VW), _rows(T, QL, 10), _rows(T, KVL, 22),
         _rows(T, HEAD_PAD), _rows(T, HEAD_PAD), _rows(T, HEAD_PAD),
         _const(w_q.shape), _const(w_kv.shape), _const((1, QL)), _const((1, KVL))],
        [jax.ShapeDtypeStruct((S, HW), BF16), jax.ShapeDtypeStruct((S, HW + VW), BF16),
         jax.ShapeDtypeStruct((S, QL), BF16), jax.ShapeDtypeStruct((S, KVL), BF16),
         jax.ShapeDtypeStruct((S, HEAD_PAD), BF16)],
        [_rows(T, HW), _rows(T, HW + VW), _rows(T, QL), _rows(T, KVL), _rows(T, HEAD_PAD)],
        S, T, [QL, KVL], name)


def _even_dz(dab, du, z, dag, dbg, dcq, dckv, dkr, name):
    S = z.shape[0]
    T = _wide_tile(S)

    def body(dab_ref, du_ref, ac_ref, ax_ref, dag_ref, dbg_ref, dcq_ref, dckv_ref, dkr_ref, dz_ref):
        duv = du_ref[...]
        dz_ref[:, 0:SCW] = dab_ref[...]
        dz_ref[:, SCW:2 * SCW] = (duv * _f32(ax_ref)).astype(BF16)
        dz_ref[:, 2 * SCW:3 * SCW] = (duv * _f32(ac_ref)).astype(BF16)
        dz_ref[:, 3 * SCW:4 * SCW] = dag_ref[...]
        dz_ref[:, 4 * SCW:5 * SCW] = dbg_ref[...]
        dz_ref[:, 5 * SCW:5 * SCW + QL] = dcq_ref[...]
        dz_ref[:, 5 * SCW + QL:5 * SCW + QL + KVL] = dckv_ref[...]
        dz_ref[:, 5 * SCW + QL + KVL:ZE] = dkr_ref[...]

    return pl.pallas_call(
        body, name=name, grid=(S // T,),
        out_shape=jax.ShapeDtypeStruct((S, ZE), BF16),
        in_specs=[_rows(T, SCW), _rows(T, SCW), _rows(T, SCW, 1), _rows(T, SCW, 2), _rows(T, SCW), _rows(T, SCW),
                  _rows(T, QL), _rows(T, KVL), _rows(T, HEAD_PAD)],
        out_specs=_rows(T, ZE), compiler_params=_cparams(),
    )(dab, du, z, z, dag, dbg, dcq, dckv, dkr)


def _layer_norm_stats(cv):
    mu = jnp.mean(cv, axis=-1, keepdims=True)
    cen = cv - mu
    rstd = lax.rsqrt(jnp.mean(cen * cen, axis=-1, keepdims=True) + EPS)
    return cen * rstd, rstd


def _odd_post(cv, z, ln_g, ln_b, name):
    S, D = cv.shape
    T = _wide_tile(S)

    def body(cv_ref, sg_ref, g_ref, b_ref, y_ref):
        cvh, _ = _layer_norm_stats(cv_ref[...])
        y_ref[...] = (_silu(cvh * g_ref[...] + b_ref[...]) * _silu(_f32(sg_ref))).astype(BF16)

    return pl.pallas_call(
        body, name=name, grid=(S // T,),
        out_shape=jax.ShapeDtypeStruct((S, D), BF16),
        in_specs=[_rows(T, D), _rows(T, D, 2), _const((1, D)), _const((1, D))],
        out_specs=_rows(T, D), compiler_params=_cparams(),
    )(cv, z, ln_g, ln_b)


def _odd_bwd_norm(dyi, cv, z, ln_g, ln_b, name):
    S, D = cv.shape
    T = _row_tile(S)

    def fn(ins, outs):
        dy_ref, cv_ref, sg_ref, g_ref, b_ref = ins
        dcv_ref, dsg_ref = outs
        cvh, rstd = _layer_norm_stats(cv_ref[...])
        ln = cvh * g_ref[...] + b_ref[...]
        sgv, dy = _f32(sg_ref), _f32(dy_ref)
        dsg_ref[...] = (dy * _silu(ln) * _dsilu(sgv)).astype(BF16)
        dln = dy * _silu(sgv) * _dsilu(ln)
        dh = dln * g_ref[...]
        dcv_ref[...] = rstd * (dh - jnp.mean(dh, axis=-1, keepdims=True)
                               - cvh * jnp.mean(dh * cvh, axis=-1, keepdims=True))
        return [dln * cvh, dln]

    return _col_sums(2, fn, [dyi, cv, z, ln_g, ln_b],
                     [_rows(T, D), _rows(T, D), _rows(T, D, 2), _const((1, D)), _const((1, D))],
                     [jax.ShapeDtypeStruct((S, D), F32), jax.ShapeDtypeStruct((S, D), BF16)],
                     [_rows(T, D), _rows(T, D)], S, T, [D, D], name)


def _odd_dz(du, z, dsg, name):
    S, D = du.shape
    T = _wide_tile(S)

    def body(du_ref, val_ref, glu_ref, dsg_ref, dz_ref):
        duv = du_ref[...]
        sig = _sigmoid(_f32(glu_ref))
        dz_ref[:, 0:D] = (duv * sig).astype(BF16)
        dz_ref[:, D:2 * D] = (duv * _f32(val_ref) * sig * (1.0 - sig)).astype(BF16)
        dz_ref[:, 2 * D:3 * D] = dsg_ref[...]

    return pl.pallas_call(
        body, name=name, grid=(S // T,),
        out_shape=jax.ShapeDtypeStruct((S, 3 * D), BF16),
        in_specs=[_rows(T, D), _rows(T, D, 0), _rows(T, D, 1), _rows(T, D)],
        out_specs=_rows(T, 3 * D), compiler_params=_cparams(),
    )(du, z, z, dsg)


ADAM_BLOCK_ELEMS = 128 * 1024


def _adam_tiles(R, C):
    if R * C <= ADAM_BLOCK_ELEMS:
        return R, C
    tr = R
    for cand in range(SUBLANES, R, SUBLANES):
        if R % cand == 0 and cand * C <= ADAM_BLOCK_ELEMS:
            tr = cand
    if tr < R:
        return tr, C
    tc = C
    for cand in range(LANES, C, LANES):
        if C % cand == 0 and R * cand <= ADAM_BLOCK_ELEMS:
            tc = cand
    return R, tc


def _adamw(g_parts, w, m, v, name):
    if not isinstance(g_parts, (list, tuple)):
        g_parts = [g_parts]
    ng = len(g_parts)
    _, R, C = g_parts[0].shape
    tr, tc = _adam_tiles(R, C)

    def body(*refs):
        g_refs = refs[:ng]
        w_ref, m_ref, v_ref, go_ref, d_ref, mo_ref, vo_ref = refs[ng:]
        g = None
        for g_ref in g_refs:
            for p in range(g_ref.shape[0]):
                part = g_ref[p].astype(F32)
                g = part if g is None else g + part
        mn = ADAM_B1 * m_ref[...] + (1.0 - ADAM_B1) * g
        vn = ADAM_B2 * v_ref[...] + (1.0 - ADAM_B2) * (g * g)
        m_hat = mn / (1.0 - ADAM_B1 ** ADAM_STEP)
        v_hat = vn / (1.0 - ADAM_B2 ** ADAM_STEP)
        go_ref[...] = g
        d_ref[...] = -ADAM_LR * (m_hat / (jnp.sqrt(v_hat) + ADAM_EPS) + ADAM_WD * w_ref[...])
        mo_ref[...] = mn
        vo_ref[...] = vn

    slab = jax.ShapeDtypeStruct((R, C), F32)
    blk = pl.BlockSpec((tr, tc), lambda i, j: (i, j))
    return pl.pallas_call(
        body, name=name, grid=(R // tr, C // tc),
        out_shape=(slab,) * 4,
        in_specs=[pl.BlockSpec((g.shape[0], tr, tc), lambda i, j: (0, i, j)) for g in g_parts] + [blk, blk, blk],
        out_specs=(blk,) * 4, compiler_params=_cparams(),
    )(*g_parts, w, m, v)


def _adamw_slab(g_parts, w, m, v, layer, prev, name):
    ng = len(g_parts)
    NL, R, C = w.shape
    tr, tc = _adam_tiles(R, C)

    def body(*refs):
        g_refs = refs[:ng]
        w_ref, m_ref, v_ref = refs[ng:ng + 3]
        go_ref, d_ref, mo_ref, vo_ref = refs[-4:]
        g = None
        for g_ref in g_refs:
            for p in range(g_ref.shape[0]):
                part = g_ref[p].astype(F32)
                g = part if g is None else g + part
        mn = ADAM_B1 * m_ref[0] + (1.0 - ADAM_B1) * g
        vn = ADAM_B2 * v_ref[0] + (1.0 - ADAM_B2) * (g * g)
        m_hat = mn / (1.0 - ADAM_B1 ** ADAM_STEP)
        v_hat = vn / (1.0 - ADAM_B2 ** ADAM_STEP)
        go_ref[0] = g
        d_ref[0] = -ADAM_LR * (m_hat / (jnp.sqrt(v_hat) + ADAM_EPS) + ADAM_WD * w_ref[0])
        mo_ref[0] = mn
        vo_ref[0] = vn

    blk = pl.BlockSpec((1, tr, tc), lambda i, j: (layer, i, j))
    n_in = ng + 3
    prev = list(prev) if prev is not None else []
    return pl.pallas_call(
        body, name=name, grid=(R // tr, C // tc),
        out_shape=(jax.ShapeDtypeStruct((NL, R, C), F32),) * 4,
        in_specs=([pl.BlockSpec((g.shape[0], tr, tc), lambda i, j: (0, i, j)) for g in g_parts] + [blk, blk, blk]
                  + [pl.BlockSpec(memory_space=pl.ANY)] * len(prev)),
        out_specs=(blk,) * 4,
        input_output_aliases={n_in + k: k for k in range(len(prev))},
        compiler_params=_cparams(),
    )(*g_parts, w, m, v, *prev)


def _gather_cols(g, shape):
    nd = len(shape)
    t = jnp.moveaxis(g, 0, nd - 1)
    return t.reshape(tuple(shape[:-1]) + (N_DEV * shape[-1],))


def _scatter_cols(full, n):
    t = full.reshape(full.shape[:-1] + (N_DEV, n))
    return jnp.moveaxis(t, -2, 0)


def kernel(x, c, positions, ada_w, ada_b, pre_norm_g, post_norm_g, even_w_in, even_sc_conv_w, even_sc_conv_b, even_q_norm_g, even_kv_norm_g, even_w_uq, even_w_ukv, even_w_out, odd_w_in, odd_conv_w, odd_conv_b, odd_ln_g, odd_ln_b, odd_w_out, loss_target, m_ada_w, m_ada_b, m_pre_norm_g, m_post_norm_g, m_even_w_in, m_even_sc_conv_w, m_even_sc_conv_b, m_even_q_norm_g, m_even_kv_norm_g, m_even_w_uq, m_even_w_ukv, m_even_w_out, m_odd_w_in, m_odd_conv_w, m_odd_conv_b, m_odd_ln_g, m_odd_ln_b, m_odd_w_out, v_ada_w, v_ada_b, v_pre_norm_g, v_post_norm_g, v_even_w_in, v_even_sc_conv_w, v_even_sc_conv_b, v_even_q_norm_g, v_even_kv_norm_g, v_even_w_uq, v_even_w_ukv, v_even_w_out, v_odd_w_in, v_odd_conv_w, v_odd_conv_b, v_odd_ln_g, v_odd_ln_b, v_odd_w_out):
    S, D = x.shape[1], x.shape[2]
    L = ada_w.shape[0]
    NE, NO = even_w_in.shape[0], odd_w_in.shape[0]
    me = 4 * lax.axis_index("x") + 2 * lax.axis_index("y") + lax.axis_index("c")
    x0 = x[0]
    target = loss_target[0]

    small_parts = [c, even_sc_conv_w, odd_conv_w, odd_conv_b, odd_ln_g, odd_ln_b]
    small_shapes = [p.shape for p in small_parts]
    sg = _exchange([_pack(small_parts, F32, SUBLANES)], False, "gather_small")[0].reshape(N_DEV, -1)
    c_all, scw_g, ocw_g, ocb_g, olg_g, olb_g = _unpack(sg, small_shapes)
    c_all = c_all.reshape(N_DEV, D)
    sc_conv_w = _gather_cols(scw_g, even_sc_conv_w.shape)
    o_conv_w = _gather_cols(ocw_g, odd_conv_w.shape)
    o_conv_b = _gather_cols(ocb_g, odd_conv_b.shape)
    o_ln_g = _gather_cols(olg_g, odd_ln_g.shape)
    o_ln_b = _gather_cols(olb_g, odd_ln_b.shape)

    pad_q = HEAD_PAD - QK_NOPE - QK_ROPE
    w_local = [jnp.swapaxes(even_w_in, 1, 2).astype(BF16),
               jnp.pad(even_w_uq, ((0, 0), (0, 0), (0, pad_q))).astype(BF16),
               jnp.pad(even_w_ukv[..., :QK_NOPE], ((0, 0), (0, 0), (0, HEAD_PAD - QK_NOPE))).astype(BF16),
               even_w_ukv[..., QK_NOPE:].astype(BF16),
               even_w_out.astype(BF16), odd_w_in.astype(BF16), odd_w_out.astype(BF16)]
    n_ada = ada_w.shape[2]
    ada_b_cols = lax.dynamic_slice_in_dim(ada_b, me * n_ada, n_ada, axis=1).reshape(L, 1, n_ada)
    mod_slab = _ada_fwd(c_all, ada_w, ada_b_cols)
    mod_g = _exchange([_pack([mod_slab], F32, SUBLANES)], False, "gather_mod")[0].reshape(N_DEV, -1)
    mod_all = mod_g[:, :L * N_DEV * n_ada].reshape(N_DEV, L, N_DEV, n_ada)
    mod = lax.dynamic_index_in_dim(mod_all, me, axis=2, keepdims=False)
    mod = jnp.moveaxis(mod, 0, 1).reshape(L, 3 * D)
    shift, scale, gate = mod[:, :D], mod[:, D:2 * D], mod[:, 2 * D:]

    heads_to_cols = lambda g: jnp.moveaxis(g, 0, 1).reshape(g.shape[1], -1)
    w_handles = {}
    token = jnp.broadcast_to(jnp.minimum(jnp.abs(mod[0, 0]), 0.0), (SUBLANES, LANES))
    for layer in range(L):
        i = layer // 2
        groups = ({"in": [w_local[0][i]], "rest": [w[i] for w in w_local[1:5]]} if layer % 2 == 0
                  else {"all": [w[i] for w in w_local[5:]]})
        for key, mine in groups.items():
            mine = [w + token[0, 0].astype(BF16) for w in mine]
            w_handles[layer, key], token = _exchange_start(mine, False, f"gather_weights_start_l{layer}_{key}")
    w_token = token

    def arrived(layer, key, after):
        return _exchange_wait(w_handles[layer, key], False, after, f"gather_weights_wait_l{layer}_{key}")[1]

    e_w_in_k, e_w_q_k, e_w_kv_k, e_w_out, o_w_in, o_w_out = ([None] * NE, [None] * NE, [None] * NE, [None] * NE,
                                                             [None] * NO, [None] * NO)

    inv_freq = 1.0 / (ROPE_THETA ** (jnp.arange(0, QK_ROPE, 2, dtype=F32) / QK_ROPE))
    inv_lane = jnp.zeros((HEAD_PAD,), F32).at[QK_NOPE:QK_NOPE + QK_ROPE].set(jnp.concatenate([inv_freq, inv_freq]))
    tabs = _rope_tables(positions.astype(F32).reshape(S, 1), inv_lane.reshape(1, HEAD_PAD))

    row = lambda a: a.reshape(1, -1)
    scb = even_sc_conv_b
    KP3, KP31 = SUBLANES, 32

    saved = []
    xs = x0
    lv = jnp.stack([pre_norm_g, post_norm_g, scale, shift, gate], axis=1).reshape(L * 5, 1, D)
    PRE_G, POST_G, SCALE, SHIFT, GATE = range(5)
    vec = lambda layer, k: (lv, layer * 5 + k)
    h = _pre_norm(xs, vec(0, PRE_G), vec(0, SCALE), vec(0, SHIFT), "pre_norm_l0", after=w_token)
    for layer in range(L):
        i = layer // 2
        tag = f"l{layer}"
        first = [h, tabs[0]] if layer == 0 else h
        if layer % 2 == 0:
            wt = arrived(layer, "in", first)[0].reshape(-1, D)
            e_w_in_k[i] = jnp.concatenate([wt[:2048], wt[2464:2976], wt[2048:2432], jnp.zeros((QK_NOPE, D), BF16),
                                           wt[2432:2464], jnp.zeros((pad_q, D), BF16)], axis=0)
            z, u, qn, kvn = _w_in_even(h, e_w_in_k[i], row(even_q_norm_g[i]), row(even_kv_norm_g[i]), f"w_in_{tag}")
            eq_g, ek_g, ev_g, eout_g = arrived(layer, "rest", z)
            e_w_q_k[i] = heads_to_cols(eq_g)
            e_w_kv_k[i] = jnp.concatenate([heads_to_cols(ek_g), heads_to_cols(ev_g)], axis=-1)
            e_w_out[i] = eout_g.reshape(-1, D)
            cw = jnp.pad(sc_conv_w[i], ((0, KP3 - SC_KERNEL), (0, 0)))
            cv = _conv_fwd(u, cw, row(scb[i]), SC_KERNEL, f"conv_{tag}")
            q, k, v, kT3, vT3 = _qkv_fwd_t(qn, kvn, z, tabs, e_w_q_k[i], e_w_kv_k[i], f"qkv_{tag}")
            o, lse = _attn_fwd_t(q, k, vT3, f"attn_{tag}")
            ycat = _even_post(z, cv, o, f"even_post_{tag}")
            y = _matmul(ycat, e_w_out[i], "nn", BF16, f"w_out_{tag}", tn=1024)
            saved.append(dict(x=xs, h=h, z=z, u=u, qn=qn, kvn=kvn, cw=cw, cv=cv, q=q, k=k, v=v, kT3=kT3, o=o, lse=lse,
                              ycat=ycat, y=y))
        else:
            owin_g, oout_g = arrived(layer, "all", first)
            o_w_in[i], o_w_out[i] = heads_to_cols(owin_g), oout_g.reshape(-1, D)
            z, u = _w_in_odd(h, o_w_in[i], f"w_in_{tag}")
            cw = jnp.pad(o_conv_w[i], ((0, KP31 - CONF_KERNEL), (0, 0)))
            cv = _conv_fwd(u, cw, row(o_conv_b[i]), CONF_KERNEL, f"conv_{tag}")
            yin = _odd_post(cv, z, row(o_ln_g[i]), row(o_ln_b[i]), f"odd_post_{tag}")
            y = _matmul(yin, o_w_out[i], "nn", BF16, f"w_out_{tag}", tn=1024)
            saved.append(dict(x=xs, h=h, z=z, u=u, cw=cw, cv=cv, yin=yin, y=y))
        if layer < L - 1:
            xs, h = _post_pre_norm(xs, y, vec(layer, POST_G), vec(layer, GATE), vec(layer + 1, PRE_G),
                                   vec(layer + 1, SCALE), vec(layer + 1, SHIFT), f"post_pre_norm_{tag}")

    dx, dy, err_sq, dgate, g_post_last = _loss_post_norm_bwd(xs, y, vec(L - 1, POST_G), vec(L - 1, GATE), target,
                                                             "loss_post_norm_bwd")
    loss = lax.psum(_scaled_total(err_sq, 0.5 / D, "loss_total")[0, 0], MESH_AXES)

    g_pre, g_post, dmod = [None] * L, [None] * L, [None] * L
    g_e_w_in, g_e_w_uq, g_e_w_ukv, g_e_w_out = [None] * NE, [None] * NE, [None] * NE, [None] * NE
    g_scw, g_scb, g_qg, g_kvg = [None] * NE, [None] * NE, [None] * NE, [None] * NE
    g_o_w_in, g_o_w_out, g_ocw, g_ocb, g_olg, g_olb = ([None] * NO for _ in range(6))
    sm_w = [even_sc_conv_w, odd_conv_w, odd_conv_b, odd_ln_g, odd_ln_b]
    sm_rows = _pack(sm_w, F32, SUBLANES).shape[0]

    def small_slab():
        full = [_scatter_cols(jnp.stack(g_scw), even_sc_conv_w.shape[-1]),
                _scatter_cols(jnp.stack(g_ocw), odd_conv_w.shape[-1]),
                _scatter_cols(jnp.concatenate(g_ocb, 0), odd_conv_b.shape[-1]),
                _scatter_cols(jnp.concatenate(g_olg, 0), odd_ln_g.shape[-1]),
                _scatter_cols(jnp.concatenate(g_olb, 0), odd_ln_b.shape[-1])]
        flat = jnp.concatenate([g.reshape(N_DEV, -1) for g in full], axis=1)
        return jnp.pad(flat, ((0, 0), (0, sm_rows * PACK_COLS - flat.shape[1]))).reshape(N_DEV, sm_rows, PACK_COLS)

    scatters = []
    bw_token = jnp.zeros((SUBLANES, LANES), F32)

    def start_scatter(tag, names, parts):
        own = [lax.dynamic_slice_in_dim(g, me, 1, axis=0) for g in parts]
        handle, token = _exchange_start([g.astype(BF16) for g in parts], True, f"scatter_grads_start_{tag}")
        scatters.append((tag, names, handle, own))
        return token

    for layer in reversed(range(L)):
        i = layer // 2
        tag = f"l{layer}"
        sv = saved[layer]
        if layer == L - 1:
            g_post[layer] = g_post_last
        if layer % 2 == 0:
            dyc = _matmul(dy, e_w_out[i], "nt", BF16, f"d_ycat_{tag}", tn=1024)
            g_e_w_out[i] = _matmul(sv["ycat"], dy, "tn", BF16, f"g_w_out_{tag}", tn=1024).reshape(N_DEV, -1, D)
            if layer == 0:
                bw_token = start_scatter("l0_out", [("even_w_out", i)], [g_e_w_out[i]])
            dab, dag, dbg, dcv, do, delta = _even_bwd_gates(dyc, sv["z"], sv["cv"], sv["o"], f"even_gates_bwd_{tag}")
            du, dcw, g_scb[i] = _conv_bwd(dcv, sv["u"], sv["cw"] + bw_token[0, 0], SC_KERNEL, f"conv_bwd_{tag}")
            g_scw[i] = dcw[:SC_KERNEL]
            dq, dk, dv = _attn_bwd_t(sv["q"], sv["k"], sv["v"], sv["kT3"], do, sv["lse"], delta, f"attn_bwd_{tag}")
            (dqp, dkvp, dcq, dckv, dkr, g_qg[i], g_kvg[i]) = _qkv_bwd(
                dq, dk, dv, sv["z"], tabs, e_w_q_k[i], e_w_kv_k[i],
                row(even_q_norm_g[i]), row(even_kv_norm_g[i]), f"qkv_bwd_{tag}")
            gq = _matmul(sv["qn"], dqp, "tn", BF16, f"g_w_uq_{tag}", tn=1024)
            gkv = _matmul(sv["kvn"], dkvp, "tn", BF16, f"g_w_ukv_{tag}")
            g_e_w_uq[i] = jnp.moveaxis(gq.reshape(QL, HEADS, HEAD_PAD)[..., :QK_NOPE + QK_ROPE], 1, 0)
            g_e_w_ukv[i] = jnp.moveaxis(jnp.concatenate(
                [gkv[:, :HEADS * HEAD_PAD].reshape(KVL, HEADS, HEAD_PAD)[..., :QK_NOPE],
                 gkv[:, HEADS * HEAD_PAD:].reshape(KVL, HEADS, V_HEAD)], axis=-1), 1, 0)
            dz = _even_dz(dab, du, sv["z"], dag, dbg, dcq, dckv, dkr, f"even_dz_{tag}")
            gt = _matmul(dz, sv["h"], "tn", BF16, f"g_w_in_{tag}", tm=1024, tn=1024)
            g_e_w_in[i] = jnp.concatenate([gt[:2048], gt[2560:2944], gt[2944 + QK_NOPE:2944 + QK_NOPE + QK_ROPE],
                                           gt[2048:2560]], axis=0).reshape(N_DEV, -1, D)
            names = [("even_w_in", i), ("even_w_uq", i), ("even_w_ukv", i)]
            parts = [g_e_w_in[i], g_e_w_uq[i], g_e_w_ukv[i]]
            if layer == 0:
                names, parts = names + [("small", 0)], parts + [small_slab()]
            else:
                names, parts = names + [("even_w_out", i)], parts + [g_e_w_out[i]]
            bw_token = start_scatter(tag, names, parts)
            w_dh = e_w_in_k[i] + bw_token[0, 0].astype(BF16) if layer == 0 else e_w_in_k[i]
            dh = _matmul(dz, w_dh, "nn", BF16, f"d_h_{tag}", tn=1024)
        else:
            dyi = _matmul(dy, o_w_out[i], "nt", BF16, f"d_yin_{tag}", tn=1024)
            g_o_w_out[i] = _matmul(sv["yin"], dy, "tn", BF16, f"g_w_out_{tag}", tn=1024).reshape(N_DEV, -1, D)
            dcv, dsg, g_olg[i], g_olb[i] = _odd_bwd_norm(dyi, sv["cv"], sv["z"], row(o_ln_g[i]), row(o_ln_b[i]),
                                                         f"odd_norm_bwd_{tag}")
            du, dcw, g_ocb[i] = _conv_bwd(dcv, sv["u"], sv["cw"], CONF_KERNEL, f"conv_bwd_{tag}")
            g_ocw[i] = dcw[:CONF_KERNEL]
            dz = _odd_dz(du, sv["z"], dsg, f"odd_dz_{tag}")
            g_o_w_in[i] = _matmul(sv["h"], dz, "tn", BF16, f"g_w_in_{tag}", tm=1024, tn=odd_w_in.shape[-1],
                                  split_n=True)
            bw_token = start_scatter(tag, [("odd_w_in", i), ("odd_w_out", i)], [g_o_w_in[i], g_o_w_out[i]])
            dh = _matmul(dz, o_w_in[i], "nt", BF16, f"d_h_{tag}", tn=1024)
        if layer > 0:
            (dx, dy, dshift, dscale, g_pre[layer], dgate_prev, g_post[layer - 1]) = _pre_post_norm_bwd(
                dh, sv["x"], dx, vec(layer, PRE_G), vec(layer, SCALE), saved[layer - 1]["y"], vec(layer - 1, POST_G),
                vec(layer - 1, GATE), f"pre_post_norm_bwd_{tag}", after=bw_token)
        else:
            dx, dshift, dscale, g_pre[layer] = _pre_norm_bwd(dh, sv["x"], dx, vec(layer, PRE_G), vec(layer, SCALE),
                                                             f"pre_norm_bwd_{tag}", after=bw_token)
            dgate_prev = None
        dmod[layer] = jnp.concatenate([dshift, dscale, dgate], axis=-1)
        dgate = dgate_prev
    grad_x = dx.reshape(1, S, D)

    rep_g = [jnp.concatenate(dmod, 0), jnp.concatenate(g_pre, 0), jnp.concatenate(g_post, 0),
             jnp.stack(g_scb), jnp.stack(g_qg), jnp.stack(g_kvg)]
    rep_w = [ada_b, pre_norm_g, post_norm_g, even_sc_conv_b, even_q_norm_g, even_kv_norm_g]
    rep_m = [m_ada_b, m_pre_norm_g, m_post_norm_g, m_even_sc_conv_b, m_even_q_norm_g, m_even_kv_norm_g]
    rep_v = [v_ada_b, v_pre_norm_g, v_post_norm_g, v_even_sc_conv_b, v_even_q_norm_g, v_even_kv_norm_g]
    rep_shapes = [w.shape for w in rep_w]
    rep_all = _exchange([_pack(rep_g, F32, SUBLANES)], False, "gather_small_grads")[0]
    rep_out = _adamw(rep_all, _pack(rep_w, F32, SUBLANES), _pack(rep_m, F32, SUBLANES), _pack(rep_v, F32, SUBLANES),
                     "adamw_replicated")
    rep_res = [_unpack(o.reshape(-1), rep_shapes) for o in rep_out]

    dmod_all = rep_all.reshape(N_DEV, -1)[:, :L * 3 * D].reshape(N_DEV, L, 3 * D)
    dmod_cols = jnp.moveaxis(lax.dynamic_slice_in_dim(dmod_all, me * n_ada, n_ada, axis=2), 0, 1)
    g_ada_w = _ada_bwd(c_all.T, dmod_cols)
    ada_out = _adamw(g_ada_w.reshape(1, -1, n_ada), ada_w.reshape(-1, n_ada),
                     m_ada_w.reshape(-1, n_ada), v_ada_w.reshape(-1, n_ada), "adamw_ada_w")
    ada_res = [o.reshape(ada_w.shape) for o in ada_out]

    sm_m = [m_even_sc_conv_w, m_odd_conv_w, m_odd_conv_b, m_odd_ln_g, m_odd_ln_b]
    sm_v = [v_even_sc_conv_w, v_odd_conv_w, v_odd_conv_b, v_odd_ln_g, v_odd_ln_b]
    sm_shapes = [w.shape for w in sm_w]
    state = {"even_w_in": (even_w_in, m_even_w_in, v_even_w_in), "even_w_uq": (even_w_uq, m_even_w_uq, v_even_w_uq),
             "even_w_ukv": (even_w_ukv, m_even_w_ukv, v_even_w_ukv), "even_w_out": (even_w_out, m_even_w_out, v_even_w_out),
             "odd_w_in": (odd_w_in, m_odd_w_in, v_odd_w_in), "odd_w_out": (odd_w_out, m_odd_w_out, v_odd_w_out)}
    state["even_w_in"] = tuple(jnp.swapaxes(t, 1, 2) for t in state["even_w_in"])
    big_res = {name: None for name in state}
    after = [bw_token, grad_x, rep_out[0], ada_out[0]]
    sm_res = None
    for tag, names, handle, own in scatters:
        _, landed = _exchange_wait(handle, True, after, f"scatter_grads_wait_{tag}")
        after = []
        for a, (name, i) in enumerate(names):
            if name == "small":
                sm_out = _adamw([own[a], landed[a]], _pack(sm_w, F32, SUBLANES), _pack(sm_m, F32, SUBLANES),
                                _pack(sm_v, F32, SUBLANES), "adamw_small_sharded")
                sm_res = [_unpack(o.reshape(-1), sm_shapes) for o in sm_out]
                continue
            big_res[name] = _adamw_slab([own[a], landed[a]], *state[name], i, big_res[name], f"adamw_{name}_{i}")
            after += list(big_res[name])
    sh_res = [dict(zip(["even_sc_conv_w", "odd_conv_w", "odd_conv_b", "odd_ln_g", "odd_ln_b"], sm_res[kind]))
              for kind in range(4)]
    for name in state:
        for kind in range(4):
            res = big_res[name][kind]
            sh_res[kind][name] = jnp.swapaxes(res, 1, 2) if name == "even_w_in" else res

    order = ["ada_w", "ada_b", "pre_norm_g", "post_norm_g", "even_w_in", "even_sc_conv_w", "even_sc_conv_b",
             "even_q_norm_g", "even_kv_norm_g", "even_w_uq", "even_w_ukv", "even_w_out", "odd_w_in", "odd_conv_w",
             "odd_conv_b", "odd_ln_g", "odd_ln_b", "odd_w_out"]
    rep_names = ["ada_b", "pre_norm_g", "post_norm_g", "even_sc_conv_b", "even_q_norm_g", "even_kv_norm_g"]
    outs = [loss, grad_x]
    for kind in range(4):
        for name in order:
            if name == "ada_w":
                outs.append(ada_res[kind])
            elif name in rep_names:
                outs.append(rep_res[kind][rep_names.index(name)])
            else:
                outs.append(sh_res[kind][name])
    return tuple(outs)
```

```python
import math

import jax
import jax.numpy as jnp
from jax import lax
from jax.experimental import pallas as pl
from jax.experimental.pallas import tpu as pltpu

F32 = jnp.float32
BF16 = jnp.bfloat16
MESH_AXES = ("x", "y", "c")
N_DEV = 8
EPS = 1e-6
CHUNK = 64
HEADS = 8
QK_NOPE = 64
QK_ROPE = 32
V_HEAD = 64
HEAD_PAD = 128
ROPE_THETA = 10000.0
SC_KERNEL = 3
CONF_KERNEL = 31
LANES = 128
SUBLANES = 8
PACK_COLS = 1024
VMEM_LIMIT = 48 * 1024 * 1024
NEG = -1e30

ADAM_LR = 0.001
ADAM_B1 = 0.9
ADAM_B2 = 0.999
ADAM_EPS = 1e-08
ADAM_WD = 0.01
ADAM_STEP = 10


def _cparams():
    return pltpu.CompilerParams(vmem_limit_bytes=VMEM_LIMIT)


def _sigmoid(x):
    return 1.0 / (1.0 + jnp.exp(-x))


def _f32(ref):
    return ref[...].astype(F32)


def _silu(x):
    return x * _sigmoid(x)


def _dsilu(x):
    s = _sigmoid(x)
    return s * (1.0 + x * (1.0 - s))


def _rows(T, width, cb=0):
    return pl.BlockSpec((T, width), lambda i: (i, cb))


def _const(shape):
    nd = len(shape)
    return pl.BlockSpec(shape, lambda i: (0,) * nd)


def _wide_tile(S):
    return min(512, S)


def _row_tile(S):
    return min(512, S)


def _exchange(srcs, scatter, name):
    n = len(srcs)
    shapes = [tuple(s.shape[1:]) if scatter else tuple(s.shape) for s in srcs]

    def body(*refs):
        src_refs, out_refs = refs[:n], refs[n:2 * n]
        send_sems, recv_sems, local_sems = refs[2 * n:]
        x, y, c = lax.axis_index("x"), lax.axis_index("y"), lax.axis_index("c")
        me = 4 * x + 2 * y + c
        owns, copies = [], []
        for a in range(n):
            def piece(d, a=a):
                return src_refs[a].at[d] if scatter else src_refs[a]

            own = pltpu.make_async_copy(piece(me), out_refs[a].at[me], local_sems.at[a])
            own.start()
            owns.append(own)
            for k in range(1, N_DEV):
                px, py, pc = x ^ ((k >> 2) & 1), y ^ ((k >> 1) & 1), c ^ (k & 1)
                peer = 4 * px + 2 * py + pc
                sem = a * (N_DEV - 1) + k - 1
                cp = pltpu.make_async_remote_copy(
                    src_ref=piece(peer), dst_ref=out_refs[a].at[me],
                    send_sem=send_sems.at[sem], recv_sem=recv_sems.at[sem],
                    device_id=(px, py, pc), device_id_type=pl.DeviceIdType.MESH)
                cp.start()
                arrival = pltpu.make_async_remote_copy(
                    src_ref=piece(peer), dst_ref=out_refs[a].at[peer],
                    send_sem=send_sems.at[sem], recv_sem=recv_sems.at[sem],
                    device_id=(x, y, c), device_id_type=pl.DeviceIdType.MESH)
                copies.append((cp, arrival))
        for _, arrival in copies:
            arrival.wait_recv()
        for cp, _ in copies:
            cp.wait_send()
        for own in owns:
            own.wait()

    return pl.pallas_call(
        body, name=name,
        out_shape=tuple(jax.ShapeDtypeStruct((N_DEV,) + shp, s.dtype) for shp, s in zip(shapes, srcs)),
        in_specs=[pl.BlockSpec(memory_space=pl.ANY)] * n,
        out_specs=tuple(pl.BlockSpec(memory_space=pl.ANY) for _ in range(n)),
        scratch_shapes=[pltpu.SemaphoreType.DMA((n * (N_DEV - 1),)),
                        pltpu.SemaphoreType.DMA((n * (N_DEV - 1),)),
                        pltpu.SemaphoreType.DMA((n,))],
    )(*srcs)


_HBM = pl.BlockSpec(memory_space=pltpu.HBM)
_SEM = pl.BlockSpec(memory_space=pltpu.SEMAPHORE)


def _peer(k):
    x, y, c = lax.axis_index("x"), lax.axis_index("y"), lax.axis_index("c")
    return x ^ ((k >> 2) & 1), y ^ ((k >> 1) & 1), c ^ (k & 1)


def _exchange_start(srcs, scatter, name):
    n = len(srcs)
    shapes = [tuple(s.shape[1:]) if scatter else tuple(s.shape) for s in srcs]
    slots = N_DEV - 1 if scatter else N_DEV
    lands = [lax.empty((slots,) + shp, s.dtype) for shp, s in zip(shapes, srcs)]
    if not scatter:
        here = 4 * lax.axis_index("x") + 2 * lax.axis_index("y") + lax.axis_index("c")
        lands = [lax.dynamic_update_index_in_dim(l, s, here, 0) for l, s in zip(lands, srcs)]

    def body(*refs):
        src_refs, land_refs = refs[:n], refs[n:2 * n]
        send_sems, recv_sems = refs[2 * n], refs[2 * n + 1]
        token = refs[4 * n + 2]
        me = 4 * lax.axis_index("x") + 2 * lax.axis_index("y") + lax.axis_index("c")
        for a in range(n):
            for k in range(1, N_DEV):
                px, py, pc = _peer(k)
                peer = 4 * px + 2 * py + pc
                pltpu.make_async_remote_copy(
                    src_ref=src_refs[a].at[peer] if scatter else src_refs[a],
                    dst_ref=land_refs[a].at[k - 1] if scatter else land_refs[a].at[me],
                    send_sem=send_sems.at[a * (N_DEV - 1) + k - 1], recv_sem=recv_sems.at[a * (N_DEV - 1) + k - 1],
                    device_id=(px, py, pc), device_id_type=pl.DeviceIdType.MESH).start()
        token[...] = jnp.zeros_like(token)

    hbm = lambda arrs: [pltpu.HBM(a.shape, a.dtype) for a in arrs]
    out = pl.pallas_call(
        body, name=name,
        out_shape=(pltpu.SemaphoreType.DMA((n * (N_DEV - 1),)), pltpu.SemaphoreType.DMA((n * (N_DEV - 1),)),
                   *hbm(srcs), *hbm(lands), jax.ShapeDtypeStruct((SUBLANES, LANES), F32)),
        in_specs=[_HBM] * (2 * n),
        out_specs=(_SEM, _SEM, *([_HBM] * (2 * n)), pl.BlockSpec(memory_space=pltpu.VMEM)),
        input_output_aliases={a: 2 + a for a in range(2 * n)},
        compiler_params=pltpu.CompilerParams(has_side_effects=pltpu.SideEffectType.DATAFLOW_SIDE_EFFECTING),
    )(*[pltpu.with_memory_space_constraint(s, pltpu.HBM) for s in srcs],
      *[pltpu.with_memory_space_constraint(l, pltpu.HBM) for l in lands])
    return (out[0], out[1], list(out[2:2 + n]), list(out[2 + n:2 + 2 * n])), out[2 + 2 * n]


def _exchange_wait(handle, scatter, after, name):
    send_sems, recv_sems, srcs, lands = handle
    n = len(srcs)
    after = list(after) if isinstance(after, (list, tuple)) else [after]

    def body(*refs):
        src_refs, land_refs = refs[:n], refs[n:2 * n]
        send_sems, recv_sems = refs[2 * n], refs[2 * n + 1]
        for a in range(n):
            for k in range(1, N_DEV):
                px, py, pc = _peer(k)
                peer = 4 * px + 2 * py + pc
                cp = pltpu.make_async_remote_copy(
                    src_ref=src_refs[a].at[peer] if scatter else src_refs[a],
                    dst_ref=land_refs[a].at[k - 1] if scatter else land_refs[a].at[peer],
                    send_sem=send_sems.at[a * (N_DEV - 1) + k - 1], recv_sem=recv_sems.at[a * (N_DEV - 1) + k - 1],
                    device_id=(px, py, pc), device_id_type=pl.DeviceIdType.MESH)
                cp.wait_send()
                cp.wait_recv()

    out = pl.pallas_call(
        body, name=name,
        out_shape=tuple(pltpu.HBM(a.shape, a.dtype) for a in srcs + lands),
        in_specs=[_HBM] * (2 * n) + [_SEM, _SEM] + [pl.BlockSpec(memory_space=pl.ANY)] * len(after),
        out_specs=tuple([_HBM] * (2 * n)),
        input_output_aliases={a: a for a in range(2 * n)},
        compiler_params=pltpu.CompilerParams(has_side_effects=pltpu.SideEffectType.DATAFLOW_SIDE_EFFECTING),
    )(*srcs, *lands, send_sems, recv_sems, *after)
    return list(out[:n]), list(out[n:])


def _pack(parts, dtype, row_mult):
    flat = jnp.concatenate([p.reshape(-1).astype(dtype) for p in parts])
    n = flat.shape[0]
    rows = -(-n // PACK_COLS)
    rows = -(-rows // row_mult) * row_mult
    flat = jnp.pad(flat, (0, rows * PACK_COLS - n))
    return flat.reshape(rows, PACK_COLS)


def _unpack(flat, shapes):
    out, off = [], 0
    for shp in shapes:
        n = math.prod(shp)
        out.append(flat[..., off:off + n].reshape(flat.shape[:-1] + tuple(shp)))
        off += n
    return out


_DIMS = {"nn": (((1,), (0,)), ((), ())), "nt": (((1,), (1,)), ((), ())), "tn": (((0,), (0,)), ((), ()))}


def _matmul(a, b, mode, out_dtype, name, tm=512, tn=512, tk=None, split_n=False):
    if mode == "nn":
        (M, K), (_, N) = a.shape, b.shape
    elif mode == "nt":
        (M, K), (N, _) = a.shape, b.shape
    else:
        (K, M), (_, N) = a.shape, b.shape
    tm, tn = min(tm, M), min(tn, N)
    tk = K if tk is None else min(tk, K)
    nk = K // tk
    assert M % tm == 0 and N % tn == 0 and K % tk == 0, (name, a.shape, b.shape)

    def body(a_ref, b_ref, o_ref, *scratch):
        p = lax.dot_general(a_ref[...].astype(BF16), b_ref[...].astype(BF16), _DIMS[mode],
                            preferred_element_type=F32)
        if split_n:
            o_ref[0] = p.astype(out_dtype)
        elif nk == 1:
            o_ref[...] = p.astype(out_dtype)
        else:
            acc = scratch[0]
            k = pl.program_id(2)

            @pl.when(k == 0)
            def _():
                acc[...] = p

            @pl.when(k > 0)
            def _():
                acc[...] += p

            @pl.when(k == nk - 1)
            def _():
                o_ref[...] = acc[...].astype(out_dtype)

    a_spec = (pl.BlockSpec((tk, tm), lambda i, j, k: (k, i)) if mode == "tn"
              else pl.BlockSpec((tm, tk), lambda i, j, k: (i, k)))
    b_spec = (pl.BlockSpec((tn, tk), lambda i, j, k: (j, k)) if mode == "nt"
              else pl.BlockSpec((tk, tn), lambda i, j, k: (k, j)))
    return pl.pallas_call(
        body, name=name, grid=(M // tm, N // tn, nk),
        out_shape=jax.ShapeDtypeStruct((N // tn, M, tn) if split_n else (M, N), out_dtype),
        in_specs=[a_spec, b_spec],
        out_specs=(pl.BlockSpec((1, tm, tn), lambda i, j, k: (j, i, 0)) if split_n
                   else pl.BlockSpec((tm, tn), lambda i, j, k: (i, j))),
        scratch_shapes=[pltpu.VMEM((tm, tn), F32)] if nk > 1 else [],
        compiler_params=_cparams(),
    )(a, b)


def _ada_fwd(c_all, ada_w, ada_b_cols):
    L, D, n = ada_w.shape

    def body(c_ref, w_ref, b_ref, o_ref):
        act = _silu(c_ref[...]).astype(BF16)
        o_ref[0] = jnp.dot(act, w_ref[0].astype(BF16), preferred_element_type=F32) + b_ref[0]

    return pl.pallas_call(
        body, name="ada_fwd", grid=(L,),
        out_shape=jax.ShapeDtypeStruct((L, N_DEV, n), F32),
        in_specs=[pl.BlockSpec((N_DEV, D), lambda l: (0, 0)),
                  pl.BlockSpec((1, D, n), lambda l: (l, 0, 0)),
                  pl.BlockSpec((1, 1, n), lambda l: (l, 0, 0))],
        out_specs=pl.BlockSpec((1, N_DEV, n), lambda l: (l, 0, 0)),
        compiler_params=_cparams(),
    )(c_all, ada_w, ada_b_cols)


def _ada_bwd(c_all_t, dmod_cols):
    D = c_all_t.shape[0]
    L, _, n = dmod_cols.shape

    def body(c_ref, d_ref, o_ref):
        act = _silu(c_ref[...])
        dm = d_ref[0]
        acc = act[:, 0:1] * dm[0:1, :]
        for b in range(1, N_DEV):
            acc = acc + act[:, b:b + 1] * dm[b:b + 1, :]
        o_ref[0] = acc

    return pl.pallas_call(
        body, name="ada_bwd", grid=(L,),
        out_shape=jax.ShapeDtypeStruct((L, D, n), F32),
        in_specs=[pl.BlockSpec((D, N_DEV), lambda l: (0, 0)),
                  pl.BlockSpec((1, N_DEV, n), lambda l: (l, 0, 0))],
        out_specs=pl.BlockSpec((1, D, n), lambda l: (l, 0, 0)),
        compiler_params=_cparams(),
    )(c_all_t, dmod_cols)


def _rope_tables(pos_col, inv_lane):
    S = pos_col.shape[0]
    T = _row_tile(S)
    half = QK_ROPE // 2

    def body(p_ref, f_ref, c_ref, up_ref, dn_ref):
        ang = p_ref[...] * f_ref[...]
        lane = lax.broadcasted_iota(jnp.int32, ang.shape, 1)
        first = (lane >= QK_NOPE) & (lane < QK_NOPE + half)
        second = (lane >= QK_NOPE + half) & (lane < QK_NOPE + QK_ROPE)
        cs, sn = jnp.cos(ang), jnp.sin(ang)
        c_ref[...] = jnp.where(first | second, cs, 1.0)
        up_ref[...] = jnp.where(first, -sn, 0.0)
        dn_ref[...] = jnp.where(second, sn, 0.0)

    tab = jax.ShapeDtypeStruct((S, HEAD_PAD), F32)
    return pl.pallas_call(
        body, name="rope_tables", grid=(S // T,),
        out_shape=(tab, tab, tab),
        in_specs=[_rows(T, 1), _const((1, HEAD_PAD))],
        out_specs=(_rows(T, HEAD_PAD),) * 3,
        compiler_params=_cparams(),
    )(pos_col, inv_lane)


def _rope(blk, ct, ut, dt):
    half = QK_ROPE // 2
    up = pltpu.roll(blk, HEAD_PAD - half, 1)
    dn = pltpu.roll(blk, half, 1)
    return blk * ct + up * ut + dn * dt


def _rope_t(d, ct, ut, dt):
    half = QK_ROPE // 2
    return d * ct + pltpu.roll(d * ut, half, 1) + pltpu.roll(d * dt, HEAD_PAD - half, 1)


def _row_operands(rows, after=None):
    ops, specs = [], []
    for a in rows:
        if isinstance(a, tuple):
            table, r = a
            ops.append(table)
            specs.append(pl.BlockSpec((None, 1, table.shape[-1]), lambda *_, r=r: (r, 0, 0)))
        else:
            ops.append(a)
            specs.append(pl.BlockSpec(a.shape, lambda *_, nd=a.ndim: (0,) * nd))
    if after is not None:
        ops.append(after)
        specs.append(pl.BlockSpec(memory_space=pl.ANY))
    return ops, specs


def _pre_norm(x, g, scale, shift, name, after=None):
    S, D = x.shape
    T = _row_tile(S)
    row_ops, row_specs = _row_operands([g, scale, shift], after)

    def body(x_ref, g_ref, sc_ref, sh_ref, *rest):
        h_ref = rest[-1]
        xv = x_ref[...]
        rstd = lax.rsqrt(jnp.mean(xv * xv, axis=-1, keepdims=True) + EPS)
        h_ref[...] = ((xv * rstd) * g_ref[...] * (1.0 + sc_ref[...]) + sh_ref[...]).astype(BF16)

    return pl.pallas_call(
        body, name=name, grid=(S // T,),
        out_shape=jax.ShapeDtypeStruct((S, D), BF16),
        in_specs=[_rows(T, D)] + row_specs,
        out_specs=_rows(T, D), compiler_params=_cparams(),
    )(x, *row_ops)


def _fold8(v):
    T, C = v.shape
    return v.reshape(T // SUBLANES, SUBLANES, C).sum(axis=0)


def _col_sums(n_sums, body_fn, ins, in_specs, outs, out_specs, S, T, widths, name):
    n_in, n_out = len(ins), len(outs)
    nt = S // T

    def body(*refs):
        in_refs = refs[:n_in]
        out_refs = refs[n_in:n_in + n_out]
        sum_refs = refs[n_in + n_out:n_in + n_out + n_sums]
        accs = refs[n_in + n_out + n_sums:]
        i = pl.program_id(0)
        terms = body_fn(in_refs, out_refs)

        @pl.when(i == 0)
        def _():
            for acc, t in zip(accs, terms):
                acc[...] = _fold8(t)

        @pl.when(i > 0)
        def _():
            for acc, t in zip(accs, terms):
                acc[...] += _fold8(t)

        @pl.when(i == nt - 1)
        def _():
            for acc, s_ref in zip(accs, sum_refs):
                s_ref[...] = jnp.sum(acc[...], axis=0, keepdims=True)

    return pl.pallas_call(
        body, name=name, grid=(nt,),
        out_shape=tuple(outs) + tuple(jax.ShapeDtypeStruct((1, w), F32) for w in widths),
        in_specs=in_specs,
        out_specs=tuple(out_specs) + tuple(_const((1, w)) for w in widths),
        scratch_shapes=[pltpu.VMEM((SUBLANES, w), F32) for w in widths],
        compiler_params=_cparams(),
    )(*ins)


def _pre_norm_bwd(dh, x, dxo, g, scale, name, after=None):
    S, D = x.shape
    T = _row_tile(S)
    row_ops, row_specs = _row_operands([g, scale], after)

    def fn(ins, outs):
        dh_ref, x_ref, dxo_ref, g_ref, sc_ref = ins[:5]
        xv, dv = x_ref[...], _f32(dh_ref)
        rstd = lax.rsqrt(jnp.mean(xv * xv, axis=-1, keepdims=True) + EPS)
        xh = xv * rstd
        dr = dv * (1.0 + sc_ref[...])
        dxh = dr * g_ref[...]
        outs[0][...] = dxo_ref[...] + rstd * (dxh - xh * jnp.mean(dxh * xh, axis=-1, keepdims=True))
        return [dv, dv * (xh * g_ref[...]), dr * xh]

    return _col_sums(3, fn, [dh, x, dxo] + row_ops,
                     [_rows(T, D), _rows(T, D), _rows(T, D)] + row_specs,
                     [jax.ShapeDtypeStruct((S, D), F32)], [_rows(T, D)], S, T, [D, D, D], name)


def _post_pre_norm(x, y, g_post, gate, g_pre, scale, shift, name):
    S, D = x.shape
    T = _row_tile(S)
    row_ops, row_specs = _row_operands([g_post, gate, g_pre, scale, shift])

    def body(x_ref, y_ref, gp_ref, gt_ref, g_ref, sc_ref, sh_ref, xn_ref, h_ref):
        yv = _f32(y_ref)
        rstd_y = lax.rsqrt(jnp.mean(yv * yv, axis=-1, keepdims=True) + EPS)
        xn = x_ref[...] + gt_ref[...] * ((yv * rstd_y) * gp_ref[...])
        xn_ref[...] = xn
        rstd = lax.rsqrt(jnp.mean(xn * xn, axis=-1, keepdims=True) + EPS)
        h_ref[...] = ((xn * rstd) * g_ref[...] * (1.0 + sc_ref[...]) + sh_ref[...]).astype(BF16)

    return pl.pallas_call(
        body, name=name, grid=(S // T,),
        out_shape=(jax.ShapeDtypeStruct((S, D), F32), jax.ShapeDtypeStruct((S, D), BF16)),
        in_specs=[_rows(T, D), _rows(T, D)] + row_specs,
        out_specs=(_rows(T, D), _rows(T, D)), compiler_params=_cparams(),
    )(x, y, *row_ops)


def _pre_post_norm_bwd(dh, x, dxo, g_pre, scale, y_prev, g_post_prev, gate_prev, name, after=None):
    S, D = x.shape
    T = _row_tile(S)
    row_ops, row_specs = _row_operands([g_pre, scale, g_post_prev, gate_prev], after)

    def fn(ins, outs):
        dh_ref, x_ref, dxo_ref, y_ref, g_ref, sc_ref, gp_ref, gt_ref = ins[:8]
        xv, dv = x_ref[...], _f32(dh_ref)
        rstd = lax.rsqrt(jnp.mean(xv * xv, axis=-1, keepdims=True) + EPS)
        xh = xv * rstd
        dr = dv * (1.0 + sc_ref[...])
        dxh = dr * g_ref[...]
        dx = dxo_ref[...] + rstd * (dxh - xh * jnp.mean(dxh * xh, axis=-1, keepdims=True))
        outs[0][...] = dx
        yv = _f32(y_ref)
        rstd_y = lax.rsqrt(jnp.mean(yv * yv, axis=-1, keepdims=True) + EPS)
        yh = yv * rstd_y
        dn = dx * gt_ref[...]
        dyh = dn * gp_ref[...]
        outs[1][...] = (rstd_y * (dyh - yh * jnp.mean(dyh * yh, axis=-1, keepdims=True))).astype(BF16)
        return [dv, dv * (xh * g_ref[...]), dr * xh, dx * (yh * gp_ref[...]), dn * yh]

    return _col_sums(5, fn, [dh, x, dxo, y_prev] + row_ops,
                     [_rows(T, D), _rows(T, D), _rows(T, D), _rows(T, D)] + row_specs,
                     [jax.ShapeDtypeStruct((S, D), F32), jax.ShapeDtypeStruct((S, D), BF16)],
                     [_rows(T, D), _rows(T, D)], S, T, [D] * 5, name)


def _loss_post_norm_bwd(x, y, g_post, gate, target, name):
    S, D = x.shape
    T = _row_tile(S)
    row_ops, row_specs = _row_operands([g_post, gate])

    def fn(ins, outs):
        x_ref, y_ref, t_ref, gp_ref, gt_ref = ins
        yv = _f32(y_ref)
        rstd_y = lax.rsqrt(jnp.mean(yv * yv, axis=-1, keepdims=True) + EPS)
        yh = yv * rstd_y
        e = x_ref[...] + gt_ref[...] * (yh * gp_ref[...]) - t_ref[...]
        dx = e * (1.0 / D)
        outs[0][...] = dx
        dn = dx * gt_ref[...]
        dyh = dn * gp_ref[...]
        outs[1][...] = (rstd_y * (dyh - yh * jnp.mean(dyh * yh, axis=-1, keepdims=True))).astype(BF16)
        return [e * e, dx * (yh * gp_ref[...]), dn * yh]

    return _col_sums(3, fn, [x, y, target] + row_ops,
                     [_rows(T, D), _rows(T, D), _rows(T, D)] + row_specs,
                     [jax.ShapeDtypeStruct((S, D), F32), jax.ShapeDtypeStruct((S, D), BF16)],
                     [_rows(T, D), _rows(T, D)], S, T, [D] * 3, name)


def _scaled_total(v, coef, name):
    def body(v_ref, o_ref):
        o_ref[...] = jnp.broadcast_to(jnp.sum(v_ref[...], axis=1, keepdims=True) * coef, (1, LANES))

    return pl.pallas_call(body, name=name, out_shape=jax.ShapeDtypeStruct((1, LANES), F32))(v)


CONV_ROWS = 64


def _conv_halo(K):
    return SUBLANES if K - 1 <= SUBLANES else 32


def _conv_fwd(u, w, b, K, name):
    S, C = u.shape
    KP = w.shape[0]
    T, HB, RS = min(512, S), _conv_halo(K), CONV_ROWS
    ratio = T // HB

    def body(u_ref, h_ref, w_ref, b_ref, o_ref, ext):
        i = pl.program_id(1)
        ext[0:HB, :] = jnp.where(i > 0, h_ref[...], 0.0)
        ext[HB:HB + T, :] = u_ref[...]
        for r0 in range(0, T, RS):
            acc = jnp.broadcast_to(b_ref[...], (RS, LANES))
            for k in range(K):
                off = HB - (K - 1) + k + r0
                acc = acc + w_ref[k:k + 1, :] * ext[off:off + RS, :]
            o_ref[r0:r0 + RS, :] = acc

    return pl.pallas_call(
        body, name=name, grid=(C // LANES, S // T),
        out_shape=jax.ShapeDtypeStruct((S, C), F32),
        in_specs=[pl.BlockSpec((T, LANES), lambda c, i: (i, c)),
                  pl.BlockSpec((HB, LANES), lambda c, i: (jnp.maximum(i * ratio - 1, 0), c)),
                  pl.BlockSpec((KP, LANES), lambda c, i: (0, c)),
                  pl.BlockSpec((1, LANES), lambda c, i: (0, c))],
        out_specs=pl.BlockSpec((T, LANES), lambda c, i: (i, c)),
        scratch_shapes=[pltpu.VMEM((HB + T, LANES), F32)],
        compiler_params=_cparams(),
    )(u, u, w, b)


def _conv_bwd(d, u, w, K, name):
    S, C = u.shape
    KP = w.shape[0]
    T, HB, RS = min(512, S), _conv_halo(K), CONV_ROWS
    ratio = T // HB
    nt = S // T
    last_halo = S // HB - 1

    def body(d_ref, dn_ref, u_ref, up_ref, w_ref, du_ref, dw_ref, db_ref, extd, extu, dws, dbs):
        i = pl.program_id(1)
        extd[0:T, :] = d_ref[...]
        extd[T:T + HB, :] = jnp.where(i < nt - 1, dn_ref[...], 0.0)
        extu[0:HB, :] = jnp.where(i > 0, up_ref[...], 0.0)
        extu[HB:HB + T, :] = u_ref[...]

        @pl.when(i == 0)
        def _():
            dws[...] = jnp.zeros_like(dws)
            dbs[...] = jnp.zeros_like(dbs)

        for r0 in range(0, T, RS):
            acc = jnp.zeros((RS, LANES), F32)
            for k in range(K):
                off = (K - 1 - k) + r0
                acc = acc + w_ref[k:k + 1, :] * extd[off:off + RS, :]
            du_ref[r0:r0 + RS, :] = acc
            dch = d_ref[r0:r0 + RS, :]
            dbs[...] += _fold8(dch)
            for k in range(K):
                off = HB - (K - 1) + k + r0
                dws[k * SUBLANES:(k + 1) * SUBLANES, :] += _fold8(dch * extu[off:off + RS, :])

        @pl.when(i == nt - 1)
        def _():
            dw_ref[...] = jnp.zeros_like(dw_ref)
            for k in range(K):
                dw_ref[k:k + 1, :] = jnp.sum(dws[k * SUBLANES:(k + 1) * SUBLANES, :], axis=0, keepdims=True)
            db_ref[...] = jnp.sum(dbs[...], axis=0, keepdims=True)

    return pl.pallas_call(
        body, name=name, grid=(C // LANES, nt),
        out_shape=(jax.ShapeDtypeStruct((S, C), F32), jax.ShapeDtypeStruct((KP, C), F32),
                   jax.ShapeDtypeStruct((1, C), F32)),
        in_specs=[pl.BlockSpec((T, LANES), lambda c, i: (i, c)),
                  pl.BlockSpec((HB, LANES), lambda c, i: (jnp.minimum((i + 1) * ratio, last_halo), c)),
                  pl.BlockSpec((T, LANES), lambda c, i: (i, c)),
                  pl.BlockSpec((HB, LANES), lambda c, i: (jnp.maximum(i * ratio - 1, 0), c)),
                  pl.BlockSpec((KP, LANES), lambda c, i: (0, c))],
        out_specs=(pl.BlockSpec((T, LANES), lambda c, i: (i, c)),
                   pl.BlockSpec((KP, LANES), lambda c, i: (0, c)),
                   pl.BlockSpec((1, LANES), lambda c, i: (0, c))),
        scratch_shapes=[pltpu.VMEM((T + HB, LANES), F32), pltpu.VMEM((HB + T, LANES), F32),
                        pltpu.VMEM((KP * SUBLANES, LANES), F32), pltpu.VMEM((SUBLANES, LANES), F32)],
        compiler_params=_cparams(),
    )(d, d, u, u, w)


SCW = 512
ZE = 3072
QL = 256
KVL = 128


def _rms_rows(x, g):
    rstd = lax.rsqrt(jnp.mean(x * x, axis=-1, keepdims=True) + EPS)
    return (x * rstd) * g


def _attn_tile(S):
    return min(256, S)


_NT = (((1,), (1,)), ((), ()))


LOG2E = math.log2(math.e)
ATTN_FWD_HEADS = 8
ATTN_BWD_HEADS = 4


def _chunk_mask_t(T):
    key = lax.broadcasted_iota(jnp.int32, (T, T), 0) // CHUNK
    qry = lax.broadcasted_iota(jnp.int32, (T, T), 1) // CHUNK
    return key <= qry


W_IN_ROWS = 512


def _w_in_even(h, w_t, qg, kvg, name):
    S, D = h.shape
    tm = min(W_IN_ROWS, S)

    def body(h_ref, w_ref, qg_ref, kvg_ref, z_ref, u_ref, qn_ref, kvn_ref):
        p = lax.dot_general(h_ref[...], w_ref[...], _NT, preferred_element_type=F32)
        z_ref[...] = p.astype(BF16)
        u_ref[...] = p[:, SCW:2 * SCW] * p[:, 2 * SCW:3 * SCW]
        qn_ref[...] = _rms_rows(p[:, 5 * SCW:5 * SCW + QL], qg_ref[...]).astype(BF16)
        kvn_ref[...] = _rms_rows(p[:, 5 * SCW + QL:5 * SCW + QL + KVL], kvg_ref[...]).astype(BF16)

    return pl.pallas_call(
        body, name=name, grid=(S // tm,),
        out_shape=(jax.ShapeDtypeStruct((S, ZE), BF16), jax.ShapeDtypeStruct((S, SCW), F32),
                   jax.ShapeDtypeStruct((S, QL), BF16), jax.ShapeDtypeStruct((S, KVL), BF16)),
        in_specs=[_rows(tm, D), _const(w_t.shape), _const((1, QL)), _const((1, KVL))],
        out_specs=(_rows(tm, ZE), _rows(tm, SCW), _rows(tm, QL), _rows(tm, KVL)),
        compiler_params=_cparams(),
    )(h, w_t, qg, kvg)


def _w_in_odd(h, w, name):
    S, D = h.shape
    tm = min(W_IN_ROWS, S)

    def body(h_ref, w_ref, z_ref, u_ref):
        p = jnp.dot(h_ref[...], w_ref[...], preferred_element_type=F32)
        z_ref[...] = p.astype(BF16)
        u_ref[...] = p[:, 0:D] * _sigmoid(p[:, D:2 * D])

    return pl.pallas_call(
        body, name=name, grid=(S // tm,),
        out_shape=(jax.ShapeDtypeStruct((S, 3 * D), BF16), jax.ShapeDtypeStruct((S, D), F32)),
        in_specs=[_rows(tm, D), _const(w.shape)],
        out_specs=(_rows(tm, 3 * D), _rows(tm, D)),
        compiler_params=_cparams(),
    )(h, w)


def _qkv_fwd_t(qn, kvn, z, tabs, w_q, w_kv, name):
    S = qn.shape[0]
    T = _attn_tile(S)
    HW = HEADS * HEAD_PAD
    scale = LOG2E / math.sqrt(QK_NOPE + QK_ROPE)

    def body(qn_ref, kvn_ref, kr_ref, ct_ref, ut_ref, dt_ref, wq_ref, wkv_ref, q_ref, k_ref, v_ref, kt_ref, vt_ref):
        ct, ut, dt = ct_ref[...], ut_ref[...], dt_ref[...]
        qa = jnp.dot(qn_ref[...], wq_ref[...], preferred_element_type=F32)
        kva = jnp.dot(kvn_ref[...], wkv_ref[...], preferred_element_type=F32)
        kr = _f32(kr_ref)
        ones_row = (lax.broadcasted_iota(jnp.int32, (V_HEAD, T), 0) == 0).astype(F32)
        for h in range(HEADS):
            sl = slice(h * HEAD_PAD, (h + 1) * HEAD_PAD)
            q_ref[:, sl] = (_rope(qa[:, sl], ct, ut, dt) * scale).astype(BF16)
            kh = _rope(kva[:, sl] + kr, ct, ut, dt)
            k_ref[:, sl] = kh.astype(BF16)
            kt_ref[0, sl, :] = kh.T.astype(BF16)
        v_ref[...] = kva[:, HW:].astype(BF16)
        for p in range(HEADS // 2):
            vpt = kva[:, HW + p * LANES:HW + (p + 1) * LANES].T
            for h in range(2):
                r0 = (2 * p + h) * HEAD_PAD
                vt_ref[0, r0:r0 + V_HEAD, :] = vpt[h * V_HEAD:(h + 1) * V_HEAD, :].astype(BF16)
                vt_ref[0, r0 + V_HEAD:r0 + HEAD_PAD, :] = ones_row.astype(BF16)

    t3 = jax.ShapeDtypeStruct((S // T, HW, T), BF16)
    return pl.pallas_call(
        body, name=name, grid=(S // T,),
        out_shape=(jax.ShapeDtypeStruct((S, HW), BF16), jax.ShapeDtypeStruct((S, HW), BF16),
                   jax.ShapeDtypeStruct((S, HEADS * V_HEAD), BF16), t3, t3),
        in_specs=[_rows(T, QL), _rows(T, KVL), _rows(T, HEAD_PAD, 23),
                  _rows(T, HEAD_PAD), _rows(T, HEAD_PAD), _rows(T, HEAD_PAD),
                  _const(w_q.shape), _const(w_kv.shape)],
        out_specs=(_rows(T, HW), _rows(T, HW), _rows(T, HEADS * V_HEAD),
                   pl.BlockSpec((1, HW, T), lambda i: (i, 0, 0)), pl.BlockSpec((1, HW, T), lambda i: (i, 0, 0))),
        compiler_params=_cparams(),
    )(qn, kvn, z, *tabs, w_q, w_kv)


def _attn_fwd_t(q, k, vT3, name):
    S = q.shape[0]
    T = _attn_tile(S)
    nq = S // T
    NH = ATTN_FWD_HEADS
    PW = NH * HEAD_PAD

    def body(q_ref, k_ref, vt_ref, o_ref, lse_ref, m_s, acc_s):
        i = pl.program_id(1)
        m_s[...] = jnp.full_like(m_s, NEG)
        acc_s[...] = jnp.zeros_like(acc_s)
        qv = q_ref[...]

        def step(j, masked):
            kb = k_ref[pl.ds(pl.multiple_of(j * T, T), T), :]
            vt = vt_ref[j]
            heads = [slice(h * HEAD_PAD, (h + 1) * HEAD_PAD) for h in range(NH)]
            sts = [lax.dot_general(kb[:, sl], qv[:, sl], _NT, preferred_element_type=F32) for sl in heads]
            alphas, pvs = [], []
            for h, sl in enumerate(heads):
                st = jnp.where(_chunk_mask_t(T), sts[h], NEG) if masked else sts[h]
                m_prev = m_s[h]
                m_new = jnp.maximum(m_prev, jnp.max(st, axis=0, keepdims=True))
                alphas.append(jnp.exp2(m_prev[0:1] - m_new[0:1]))
                pt = jnp.exp2(st - m_new[0:1]).astype(BF16)
                m_s[h] = m_new
                pvs.append(jnp.dot(vt[sl, :], pt, preferred_element_type=F32))
            for h in range(NH):
                acc_s[h] = acc_s[h] * alphas[h] + pvs[h]

        def loop_body(j, carry):
            step(j, False)
            return carry

        lax.fori_loop(0, i, loop_body, 0)
        step(i, True)
        for g in range(NH // 2):
            outs = []
            for h in (2 * g, 2 * g + 1):
                acc = acc_s[h]
                l_row = acc[V_HEAD:V_HEAD + 1, :]
                outs.append(acc[0:V_HEAD, :] / l_row)
                lse_ref[0, h * SUBLANES:(h + 1) * SUBLANES, :] = m_s[h] + jnp.log2(l_row)
            o_ref[:, g * LANES:(g + 1) * LANES] = jnp.concatenate(outs, axis=0).T

    return pl.pallas_call(
        body, name=name, grid=(HEADS // NH, nq),
        out_shape=(jax.ShapeDtypeStruct((S, HEADS * V_HEAD), F32),
                   jax.ShapeDtypeStruct((nq, HEADS * SUBLANES, T), F32)),
        in_specs=[pl.BlockSpec((T, PW), lambda p, i: (i, p)),
                  pl.BlockSpec((S, PW), lambda p, i: (0, p)),
                  pl.BlockSpec((nq, PW, T), lambda p, i: (0, p, 0))],
        out_specs=(pl.BlockSpec((T, NH * V_HEAD), lambda p, i: (i, p)),
                   pl.BlockSpec((1, NH * SUBLANES, T), lambda p, i: (i, p, 0))),
        scratch_shapes=[pltpu.VMEM((NH, SUBLANES, T), F32), pltpu.VMEM((NH, HEAD_PAD, T), F32)],
        compiler_params=_cparams(),
    )(q, k, vT3)


def _attn_bwd_t(q, k, v, kT3, do, lse3, dl3, name):
    S = q.shape[0]
    T = _attn_tile(S)
    nq = S // T
    NH = ATTN_BWD_HEADS
    PW = NH * HEAD_PAD
    VW = NH * V_HEAD

    def body(q_ref, k_ref, v_ref, kt_ref, do_ref, lse_ref, dl_ref, dq_ref, dk_ref, dv_ref, dk_s, dv_s):
        j = pl.program_id(1)
        left = lax.broadcasted_iota(jnp.int32, (T, LANES), 1) < V_HEAD

        @pl.when(j == 0)
        def _():
            dq_ref[...] = jnp.zeros_like(dq_ref)

        dk_s[...] = jnp.zeros_like(dk_s)
        dv_s[...] = jnp.zeros_like(dv_s)
        kb = k_ref[...]
        vms = []
        for g in range(NH // 2):
            vb = v_ref[:, g * LANES:(g + 1) * LANES]
            vms += [jnp.where(left, vb, jnp.zeros_like(vb)), jnp.where(left, jnp.zeros_like(vb), vb)]
        kt = kt_ref[0]

        def step(i, masked):
            r0 = pl.multiple_of(i * T, T)
            qb = q_ref[pl.ds(r0, T), :]
            do_all = do_ref[pl.ds(r0, T), :]
            lse = lse_ref[i]
            dl = dl_ref[i]
            heads = [slice(h * HEAD_PAD, (h + 1) * HEAD_PAD) for h in range(NH)]
            dobs = [do_all[:, (h // 2) * LANES:(h // 2 + 1) * LANES] for h in range(NH)]
            sts = [lax.dot_general(kb[:, sl], qb[:, sl], _NT, preferred_element_type=F32) for sl in heads]
            dpts = [lax.dot_general(vms[h], dobs[h], _NT, preferred_element_type=F32) for h in range(NH)]
            res = []
            for h, sl in enumerate(heads):
                r8 = h * SUBLANES
                pt = jnp.exp2(sts[h] - lse[r8:r8 + 1, :])
                if masked:
                    pt = jnp.where(_chunk_mask_t(T), pt, 0.0)
                dst = (pt * (dpts[h] - dl[r8:r8 + 1, :])).astype(BF16)
                res.append((jnp.dot(pt.astype(BF16), dobs[h], preferred_element_type=F32),
                            jnp.dot(dst, qb[:, sl], preferred_element_type=F32),
                            jnp.dot(kt[sl, :], dst, preferred_element_type=F32)))
            for h, sl in enumerate(heads):
                dv_s[h] += res[h][0]
                dk_s[:, sl] += res[h][1]
                dq_ref[i, sl, :] += res[h][2]

        def loop_body(i, carry):
            step(i, False)
            return carry

        step(j, True)
        lax.fori_loop(j + 1, nq, loop_body, 0)
        dk_ref[...] = (dk_s[...] * (1.0 / LOG2E)).astype(BF16)
        for g in range(NH // 2):
            dv_ref[:, g * LANES:(g + 1) * LANES] = jnp.where(left, dv_s[2 * g], dv_s[2 * g + 1]).astype(BF16)

    return pl.pallas_call(
        body, name=name, grid=(HEADS // NH, nq),
        out_shape=(jax.ShapeDtypeStruct((nq, HEADS * HEAD_PAD, T), F32),
                   jax.ShapeDtypeStruct((S, HEADS * HEAD_PAD), BF16), jax.ShapeDtypeStruct((S, HEADS * V_HEAD), BF16)),
        in_specs=[pl.BlockSpec((S, PW), lambda p, j: (0, p)),
                  pl.BlockSpec((T, PW), lambda p, j: (j, p)),
                  pl.BlockSpec((T, VW), lambda p, j: (j, p)),
                  pl.BlockSpec((1, PW, T), lambda p, j: (j, p, 0)),
                  pl.BlockSpec((S, VW), lambda p, j: (0, p)),
                  pl.BlockSpec((nq, NH * SUBLANES, T), lambda p, j: (0, p, 0)),
                  pl.BlockSpec((nq, NH * SUBLANES, T), lambda p, j: (0, p, 0))],
        out_specs=(pl.BlockSpec((nq, PW, T), lambda p, j: (0, p, 0)),
                   pl.BlockSpec((T, PW), lambda p, j: (j, p)),
                   pl.BlockSpec((T, VW), lambda p, j: (j, p))),
        scratch_shapes=[pltpu.VMEM((T, PW), F32), pltpu.VMEM((NH, T, LANES), F32)],
        compiler_params=_cparams(),
    )(q, k, v, kT3, do, lse3, dl3)


def _even_post(z, cv, o, name):
    S = z.shape[0]
    T = _wide_tile(S)

    def body(ab_ref, ag_ref, bg_ref, cv_ref, o_ref, y_ref):
        y_ref[:, 0:SCW] = (_f32(ab_ref) * cv_ref[...] * _silu(_f32(ag_ref))).astype(BF16)
        y_ref[:, SCW:2 * SCW] = (o_ref[...] * _silu(_f32(bg_ref))).astype(BF16)

    return pl.pallas_call(
        body, name=name, grid=(S // T,),
        out_shape=jax.ShapeDtypeStruct((S, 2 * SCW), BF16),
        in_specs=[_rows(T, SCW, 0), _rows(T, SCW, 3), _rows(T, SCW, 4), _rows(T, SCW), _rows(T, SCW)],
        out_specs=_rows(T, 2 * SCW), compiler_params=_cparams(),
    )(z, z, z, cv, o)


def _even_bwd_gates(dyc, z, cv, o, name):
    S = z.shape[0]
    T = _attn_tile(S)

    def body(dya_ref, dyb_ref, ab_ref, ag_ref, bg_ref, cv_ref, o_ref,
             dab_ref, dag_ref, dbg_ref, dcv_ref, do_ref, dl_ref):
        dya, ab, ag, cv = _f32(dya_ref), _f32(ab_ref), _f32(ag_ref), cv_ref[...]
        sg = _silu(ag)
        dab_ref[...] = (dya * cv * sg).astype(BF16)
        dcv_ref[...] = dya * ab * sg
        dag_ref[...] = (dya * ab * cv * _dsilu(ag)).astype(BF16)
        dyb, bg, ov = _f32(dyb_ref), _f32(bg_ref), o_ref[...]
        dov = dyb * _silu(bg)
        do_ref[...] = dov.astype(BF16)
        dbg_ref[...] = (dyb * ov * _dsilu(bg)).astype(BF16)
        prod = dov * ov
        left = lax.broadcasted_iota(jnp.int32, (T, LANES), 1) < V_HEAD
        for p in range(HEADS // 2):
            blk = prod[:, p * LANES:(p + 1) * LANES]
            s0 = jnp.sum(jnp.where(left, blk, 0.0), axis=1, keepdims=True)
            s1 = jnp.sum(jnp.where(left, 0.0, blk), axis=1, keepdims=True)
            dt = jnp.where(left, s0, s1).T
            dl_ref[0, 2 * p * SUBLANES:(2 * p + 1) * SUBLANES, :] = dt[0:SUBLANES, :]
            dl_ref[0, (2 * p + 1) * SUBLANES:(2 * p + 2) * SUBLANES, :] = dt[V_HEAD:V_HEAD + SUBLANES, :]

    assert T == _attn_tile(S)
    bf = jax.ShapeDtypeStruct((S, SCW), BF16)
    ff = jax.ShapeDtypeStruct((S, SCW), F32)
    return pl.pallas_call(
        body, name=name, grid=(S // T,),
        out_shape=(bf, bf, bf, ff, bf, jax.ShapeDtypeStruct((S // T, HEADS * SUBLANES, T), F32)),
        in_specs=[_rows(T, SCW, 0), _rows(T, SCW, 1), _rows(T, SCW, 0), _rows(T, SCW, 3), _rows(T, SCW, 4),
                  _rows(T, SCW), _rows(T, SCW)],
        out_specs=(_rows(T, SCW),) * 5 + (pl.BlockSpec((1, HEADS * SUBLANES, T), lambda i: (i, 0, 0)),),
        compiler_params=_cparams(),
    )(dyc, dyc, z, z, z, cv, o)


def _qkv_bwd(dq, dk, dv, z, tabs, w_q, w_kv, qg, kvg, name):
    S = dk.shape[0]
    T = _attn_tile(S)
    HW = HEADS * HEAD_PAD
    VW = HEADS * V_HEAD
    scale = 1.0 / math.sqrt(QK_NOPE + QK_ROPE)

    def fn(ins, outs):
        dq_ref, dk_ref, dv_ref, cq_ref, ckv_ref, ct_ref, ut_ref, dt_ref, wq_ref, wkv_ref, qg_ref, kvg_ref = ins
        dqp_ref, dkvp_ref, dcq_ref, dckv_ref, dkr_ref = outs
        ct, ut, dt = ct_ref[...], ut_ref[...], dt_ref[...]
        dkr = jnp.zeros((T, HEAD_PAD), F32)
        for h in range(HEADS):
            sl = slice(h * HEAD_PAD, (h + 1) * HEAD_PAD)
            dqp_ref[:, sl] = (_rope_t(dq_ref[0, sl, :].T, ct, ut, dt) * scale).astype(BF16)
            dkh = _rope_t(dk_ref[:, sl].astype(F32), ct, ut, dt)
            dkr = dkr + dkh
            dkvp_ref[:, sl] = dkh.astype(BF16)
        dkvp_ref[:, HW:] = dv_ref[...].astype(BF16)
        dkr_ref[...] = dkr.astype(BF16)
        sums = []
        for lat_ref, g_ref, dpre_ref, w_ref, dlat_ref in ((cq_ref, qg_ref, dqp_ref, wq_ref, dcq_ref),
                                                         (ckv_ref, kvg_ref, dkvp_ref, wkv_ref, dckv_ref)):
            dn = lax.dot_general(dpre_ref[...], w_ref[...], _NT, preferred_element_type=F32)
            xv = _f32(lat_ref)
            rstd = lax.rsqrt(jnp.mean(xv * xv, axis=-1, keepdims=True) + EPS)
            xh = xv * rstd
            dxh = dn * g_ref[...]
            dlat_ref[...] = (rstd * (dxh - xh * jnp.mean(dxh * xh, axis=-1, keepdims=True))).astype(BF16)
            sums.append(dn * xh)
        return sums

    return _col_sums(
        2, fn, [dq, dk, dv, z, z, *tabs, w_q, w_kv, qg, kvg],
        [pl.BlockSpec((1, HW, T), lambda i: (i, 0, 0)), _rows(T, HW), _rows(T, VW), _rows(T, QL, 10), _rows(T, KVL, 22),
         _rows(T, HEAD_PAD), _rows(T, HEAD_PAD), _rows(T, HEAD_PAD),
         _const(w_q.shape), _const(w_kv.shape), _const((1, QL)), _const((1, KVL))],
        [jax.ShapeDtypeStruct((S, HW), BF16), jax.ShapeDtypeStruct((S, HW + VW), BF16),
         jax.ShapeDtypeStruct((S, QL), BF16), jax.ShapeDtypeStruct((S, KVL), BF16),
         jax.ShapeDtypeStruct((S, HEAD_PAD), BF16)],
        [_rows(T, HW), _rows(T, HW + VW), _rows(T, QL), _rows(T, KVL), _rows(T, HEAD_PAD)],
        S, T, [QL, KVL], name)


def _even_dz(dab, du, z, dag, dbg, dcq, dckv, dkr, name):
    S = z.shape[0]
    T = _wide_tile(S)

    def body(dab_ref, du_ref, ac_ref, ax_ref, dag_ref, dbg_ref, dcq_ref, dckv_ref, dkr_ref, dz_ref):
        duv = du_ref[...]
        dz_ref[:, 0:SCW] = dab_ref[...]
        dz_ref[:, SCW:2 * SCW] = (duv * _f32(ax_ref)).astype(BF16)
        dz_ref[:, 2 * SCW:3 * SCW] = (duv * _f32(ac_ref)).astype(BF16)
        dz_ref[:, 3 * SCW:4 * SCW] = dag_ref[...]
        dz_ref[:, 4 * SCW:5 * SCW] = dbg_ref[...]
        dz_ref[:, 5 * SCW:5 * SCW + QL] = dcq_ref[...]
        dz_ref[:, 5 * SCW + QL:5 * SCW + QL + KVL] = dckv_ref[...]
        dz_ref[:, 5 * SCW + QL + KVL:ZE] = dkr_ref[...]

    return pl.pallas_call(
        body, name=name, grid=(S // T,),
        out_shape=jax.ShapeDtypeStruct((S, ZE), BF16),
        in_specs=[_rows(T, SCW), _rows(T, SCW), _rows(T, SCW, 1), _rows(T, SCW, 2), _rows(T, SCW), _rows(T, SCW),
                  _rows(T, QL), _rows(T, KVL), _rows(T, HEAD_PAD)],
        out_specs=_rows(T, ZE), compiler_params=_cparams(),
    )(dab, du, z, z, dag, dbg, dcq, dckv, dkr)


def _layer_norm_stats(cv):
    mu = jnp.mean(cv, axis=-1, keepdims=True)
    cen = cv - mu
    rstd = lax.rsqrt(jnp.mean(cen * cen, axis=-1, keepdims=True) + EPS)
    return cen * rstd, rstd


def _odd_post(cv, z, ln_g, ln_b, name):
    S, D = cv.shape
    T = _wide_tile(S)

    def body(cv_ref, sg_ref, g_ref, b_ref, y_ref):
        cvh, _ = _layer_norm_stats(cv_ref[...])
        y_ref[...] = (_silu(cvh * g_ref[...] + b_ref[...]) * _silu(_f32(sg_ref))).astype(BF16)

    return pl.pallas_call(
        body, name=name, grid=(S // T,),
        out_shape=jax.ShapeDtypeStruct((S, D), BF16),
        in_specs=[_rows(T, D), _rows(T, D, 2), _const((1, D)), _const((1, D))],
        out_specs=_rows(T, D), compiler_params=_cparams(),
    )(cv, z, ln_g, ln_b)


def _odd_bwd_norm(dyi, cv, z, ln_g, ln_b, name):
    S, D = cv.shape
    T = _row_tile(S)

    def fn(ins, outs):
        dy_ref, cv_ref, sg_ref, g_ref, b_ref = ins
        dcv_ref, dsg_ref = outs
        cvh, rstd = _layer_norm_stats(cv_ref[...])
        ln = cvh * g_ref[...] + b_ref[...]
        sgv, dy = _f32(sg_ref), _f32(dy_ref)
        dsg_ref[...] = (dy * _silu(ln) * _dsilu(sgv)).astype(BF16)
        dln = dy * _silu(sgv) * _dsilu(ln)
        dh = dln * g_ref[...]
        dcv_ref[...] = rstd * (dh - jnp.mean(dh, axis=-1, keepdims=True)
                               - cvh * jnp.mean(dh * cvh, axis=-1, keepdims=True))
        return [dln * cvh, dln]

    return _col_sums(2, fn, [dyi, cv, z, ln_g, ln_b],
                     [_rows(T, D), _rows(T, D), _rows(T, D, 2), _const((1, D)), _const((1, D))],
                     [jax.ShapeDtypeStruct((S, D), F32), jax.ShapeDtypeStruct((S, D), BF16)],
                     [_rows(T, D), _rows(T, D)], S, T, [D, D], name)


def _odd_dz(du, z, dsg, name):
    S, D = du.shape
    T = _wide_tile(S)

    def body(du_ref, val_ref, glu_ref, dsg_ref, dz_ref):
        duv = du_ref[...]
        sig = _sigmoid(_f32(glu_ref))
        dz_ref[:, 0:D] = (duv * sig).astype(BF16)
        dz_ref[:, D:2 * D] = (duv * _f32(val_ref) * sig * (1.0 - sig)).astype(BF16)
        dz_ref[:, 2 * D:3 * D] = dsg_ref[...]

    return pl.pallas_call(
        body, name=name, grid=(S // T,),
        out_shape=jax.ShapeDtypeStruct((S, 3 * D), BF16),
        in_specs=[_rows(T, D), _rows(T, D, 0), _rows(T, D, 1), _rows(T, D)],
        out_specs=_rows(T, 3 * D), compiler_params=_cparams(),
    )(du, z, z, dsg)


ADAM_BLOCK_ELEMS = 128 * 1024


def _adam_tiles(R, C):
    if R * C <= ADAM_BLOCK_ELEMS:
        return R, C
    tr = R
    for cand in range(SUBLANES, R, SUBLANES):
        if R % cand == 0 and cand * C <= ADAM_BLOCK_ELEMS:
            tr = cand
    if tr < R:
        return tr, C
    tc = C
    for cand in range(LANES, C, LANES):
        if C % cand == 0 and R * cand <= ADAM_BLOCK_ELEMS:
            tc = cand
    return R, tc


def _adamw(g_parts, w, m, v, name):
    if not isinstance(g_parts, (list, tuple)):
        g_parts = [g_parts]
    ng = len(g_parts)
    _, R, C = g_parts[0].shape
    tr, tc = _adam_tiles(R, C)

    def body(*refs):
        g_refs = refs[:ng]
        w_ref, m_ref, v_ref, go_ref, d_ref, mo_ref, vo_ref = refs[ng:]
        g = None
        for g_ref in g_refs:
            for p in range(g_ref.shape[0]):
                part = g_ref[p].astype(F32)
                g = part if g is None else g + part
        mn = ADAM_B1 * m_ref[...] + (1.0 - ADAM_B1) * g
        vn = ADAM_B2 * v_ref[...] + (1.0 - ADAM_B2) * (g * g)
        m_hat = mn / (1.0 - ADAM_B1 ** ADAM_STEP)
        v_hat = vn / (1.0 - ADAM_B2 ** ADAM_STEP)
        go_ref[...] = g
        d_ref[...] = -ADAM_LR * (m_hat / (jnp.sqrt(v_hat) + ADAM_EPS) + ADAM_WD * w_ref[...])
        mo_ref[...] = mn
        vo_ref[...] = vn

    slab = jax.ShapeDtypeStruct((R, C), F32)
    blk = pl.BlockSpec((tr, tc), lambda i, j: (i, j))
    return pl.pallas_call(
        body, name=name, grid=(R // tr, C // tc),
        out_shape=(slab,) * 4,
        in_specs=[pl.BlockSpec((g.shape[0], tr, tc), lambda i, j: (0, i, j)) for g in g_parts] + [blk, blk, blk],
        out_specs=(blk,) * 4, compiler_params=_cparams(),
    )(*g_parts, w, m, v)


def _adamw_slab(g_parts, w, m, v, layer, prev, name):
    ng = len(g_parts)
    NL, R, C = w.shape
    tr, tc = _adam_tiles(R, C)

    def body(*refs):
        g_refs = refs[:ng]
        w_ref, m_ref, v_ref = refs[ng:ng + 3]
        go_ref, d_ref, mo_ref, vo_ref = refs[-4:]
        g = None
        for g_ref in g_refs:
            for p in range(g_ref.shape[0]):
                part = g_ref[p].astype(F32)
                g = part if g is None else g + part
        mn = ADAM_B1 * m_ref[0] + (1.0 - ADAM_B1) * g
        vn = ADAM_B2 * v_ref[0] + (1.0 - ADAM_B2) * (g * g)
        m_hat = mn / (1.0 - ADAM_B1 ** ADAM_STEP)
        v_hat = vn / (1.0 - ADAM_B2 ** ADAM_STEP)
        go_ref[0] = g
        d_ref[0] = -ADAM_LR * (m_hat / (jnp.sqrt(v_hat) + ADAM_EPS) + ADAM_WD * w_ref[0])
        mo_ref[0] = mn
        vo_ref[0] = vn

    blk = pl.BlockSpec((1, tr, tc), lambda i, j: (layer, i, j))
    n_in = ng + 3
    prev = list(prev) if prev is not None else []
    return pl.pallas_call(
        body, name=name, grid=(R // tr, C // tc),
        out_shape=(jax.ShapeDtypeStruct((NL, R, C), F32),) * 4,
        in_specs=([pl.BlockSpec((g.shape[0], tr, tc), lambda i, j: (0, i, j)) for g in g_parts] + [blk, blk, blk]
                  + [pl.BlockSpec(memory_space=pl.ANY)] * len(prev)),
        out_specs=(blk,) * 4,
        input_output_aliases={n_in + k: k for k in range(len(prev))},
        compiler_params=_cparams(),
    )(*g_parts, w, m, v, *prev)


def _gather_cols(g, shape):
    nd = len(shape)
    t = jnp.moveaxis(g, 0, nd - 1)
    return t.reshape(tuple(shape[:-1]) + (N_DEV * shape[-1],))


def _scatter_cols(full, n):
    t = full.reshape(full.shape[:-1] + (N_DEV, n))
    return jnp.moveaxis(t, -2, 0)


def kernel(x, c, positions, ada_w, ada_b, pre_norm_g, post_norm_g, even_w_in, even_sc_conv_w, even_sc_conv_b, even_q_norm_g, even_kv_norm_g, even_w_uq, even_w_ukv, even_w_out, odd_w_in, odd_conv_w, odd_conv_b, odd_ln_g, odd_ln_b, odd_w_out, loss_target, m_ada_w, m_ada_b, m_pre_norm_g, m_post_norm_g, m_even_w_in, m_even_sc_conv_w, m_even_sc_conv_b, m_even_q_norm_g, m_even_kv_norm_g, m_even_w_uq, m_even_w_ukv, m_even_w_out, m_odd_w_in, m_odd_conv_w, m_odd_conv_b, m_odd_ln_g, m_odd_ln_b, m_odd_w_out, v_ada_w, v_ada_b, v_pre_norm_g, v_post_norm_g, v_even_w_in, v_even_sc_conv_w, v_even_sc_conv_b, v_even_q_norm_g, v_even_kv_norm_g, v_even_w_uq, v_even_w_ukv, v_even_w_out, v_odd_w_in, v_odd_conv_w, v_odd_conv_b, v_odd_ln_g, v_odd_ln_b, v_odd_w_out):
    S, D = x.shape[1], x.shape[2]
    L = ada_w.shape[0]
    NE, NO = even_w_in.shape[0], odd_w_in.shape[0]
    me = 4 * lax.axis_index("x") + 2 * lax.axis_index("y") + lax.axis_index("c")
    x0 = x[0]
    target = loss_target[0]

    small_parts = [c, even_sc_conv_w, odd_conv_w, odd_conv_b, odd_ln_g, odd_ln_b]
    small_shapes = [p.shape for p in small_parts]
    sg = _exchange([_pack(small_parts, F32, SUBLANES)], False, "gather_small")[0].reshape(N_DEV, -1)
    c_all, scw_g, ocw_g, ocb_g, olg_g, olb_g = _unpack(sg, small_shapes)
    c_all = c_all.reshape(N_DEV, D)
    sc_conv_w = _gather_cols(scw_g, even_sc_conv_w.shape)
    o_conv_w = _gather_cols(ocw_g, odd_conv_w.shape)
    o_conv_b = _gather_cols(ocb_g, odd_conv_b.shape)
    o_ln_g = _gather_cols(olg_g, odd_ln_g.shape)
    o_ln_b = _gather_cols(olb_g, odd_ln_b.shape)

    pad_q = HEAD_PAD - QK_NOPE - QK_ROPE
    w_local = [jnp.swapaxes(even_w_in, 1, 2).astype(BF16),
               jnp.pad(even_w_uq, ((0, 0), (0, 0), (0, pad_q))).astype(BF16),
               jnp.pad(even_w_ukv[..., :QK_NOPE], ((0, 0), (0, 0), (0, HEAD_PAD - QK_NOPE))).astype(BF16),
               even_w_ukv[..., QK_NOPE:].astype(BF16),
               even_w_out.astype(BF16), odd_w_in.astype(BF16), odd_w_out.astype(BF16)]
    n_ada = ada_w.shape[2]
    ada_b_cols = lax.dynamic_slice_in_dim(ada_b, me * n_ada, n_ada, axis=1).reshape(L, 1, n_ada)
    mod_slab = _ada_fwd(c_all, ada_w, ada_b_cols)
    mod_g = _exchange([_pack([mod_slab], F32, SUBLANES)], False, "gather_mod")[0].reshape(N_DEV, -1)
    mod_all = mod_g[:, :L * N_DEV * n_ada].reshape(N_DEV, L, N_DEV, n_ada)
    mod = lax.dynamic_index_in_dim(mod_all, me, axis=2, keepdims=False)
    mod = jnp.moveaxis(mod, 0, 1).reshape(L, 3 * D)
    shift, scale, gate = mod[:, :D], mod[:, D:2 * D], mod[:, 2 * D:]

    heads_to_cols = lambda g: jnp.moveaxis(g, 0, 1).reshape(g.shape[1], -1)
    w_handles = {}
    token = jnp.broadcast_to(jnp.minimum(jnp.abs(mod[0, 0]), 0.0), (SUBLANES, LANES))
    for layer in range(L):
        i = layer // 2
        groups = ({"in": [w_local[0][i]], "rest": [w[i] for w in w_local[1:5]]} if layer % 2 == 0
                  else {"all": [w[i] for w in w_local[5:]]})
        for key, mine in groups.items():
            mine = [w + token[0, 0].astype(BF16) for w in mine]
            w_handles[layer, key], token = _exchange_start(mine, False, f"gather_weights_start_l{layer}_{key}")
    w_token = token

    def arrived(layer, key, after):
        return _exchange_wait(w_handles[layer, key], False, after, f"gather_weights_wait_l{layer}_{key}")[1]

    e_w_in_k, e_w_q_k, e_w_kv_k, e_w_out, o_w_in, o_w_out = ([None] * NE, [None] * NE, [None] * NE, [None] * NE,
                                                             [None] * NO, [None] * NO)

    inv_freq = 1.0 / (ROPE_THETA ** (jnp.arange(0, QK_ROPE, 2, dtype=F32) / QK_ROPE))
    inv_lane = jnp.zeros((HEAD_PAD,), F32).at[QK_NOPE:QK_NOPE + QK_ROPE].set(jnp.concatenate([inv_freq, inv_freq]))
    tabs = _rope_tables(positions.astype(F32).reshape(S, 1), inv_lane.reshape(1, HEAD_PAD))

    row = lambda a: a.reshape(1, -1)
    scb = even_sc_conv_b
    KP3, KP31 = SUBLANES, 32

    saved = []
    xs = x0
    lv = jnp.stack([pre_norm_g, post_norm_g, scale, shift, gate], axis=1).reshape(L * 5, 1, D)
    PRE_G, POST_G, SCALE, SHIFT, GATE = range(5)
    vec = lambda layer, k: (lv, layer * 5 + k)
    h = _pre_norm(xs, vec(0, PRE_G), vec(0, SCALE), vec(0, SHIFT), "pre_norm_l0", after=w_token)
    for layer in range(L):
        i = layer // 2
        tag = f"l{layer}"
        first = [h, tabs[0]] if layer == 0 else h
        if layer % 2 == 0:
            wt = arrived(layer, "in", first)[0].reshape(-1, D)
            e_w_in_k[i] = jnp.concatenate([wt[:2048], wt[2464:2976], wt[2048:2432], jnp.zeros((QK_NOPE, D), BF16),
                                           wt[2432:2464], jnp.zeros((pad_q, D), BF16)], axis=0)
            z, u, qn, kvn = _w_in_even(h, e_w_in_k[i], row(even_q_norm_g[i]), row(even_kv_norm_g[i]), f"w_in_{tag}")
            eq_g, ek_g, ev_g, eout_g = arrived(layer, "rest", z)
            e_w_q_k[i] = heads_to_cols(eq_g)
            e_w_kv_k[i] = jnp.concatenate([heads_to_cols(ek_g), heads_to_cols(ev_g)], axis=-1)
            e_w_out[i] = eout_g.reshape(-1, D)
            cw = jnp.pad(sc_conv_w[i], ((0, KP3 - SC_KERNEL), (0, 0)))
            cv = _conv_fwd(u, cw, row(scb[i]), SC_KERNEL, f"conv_{tag}")
            q, k, v, kT3, vT3 = _qkv_fwd_t(qn, kvn, z, tabs, e_w_q_k[i], e_w_kv_k[i], f"qkv_{tag}")
            o, lse = _attn_fwd_t(q, k, vT3, f"attn_{tag}")
            ycat = _even_post(z, cv, o, f"even_post_{tag}")
            y = _matmul(ycat, e_w_out[i], "nn", BF16, f"w_out_{tag}", tn=1024)
            saved.append(dict(x=xs, h=h, z=z, u=u, qn=qn, kvn=kvn, cw=cw, cv=cv, q=q, k=k, v=v, kT3=kT3, o=o, lse=lse,
                              ycat=ycat, y=y))
        else:
            owin_g, oout_g = arrived(layer, "all", first)
            o_w_in[i], o_w_out[i] = heads_to_cols(owin_g), oout_g.reshape(-1, D)
            z, u = _w_in_odd(h, o_w_in[i], f"w_in_{tag}")
            cw = jnp.pad(o_conv_w[i], ((0, KP31 - CONF_KERNEL), (0, 0)))
            cv = _conv_fwd(u, cw, row(o_conv_b[i]), CONF_KERNEL, f"conv_{tag}")
            yin = _odd_post(cv, z, row(o_ln_g[i]), row(o_ln_b[i]), f"odd_post_{tag}")
            y = _matmul(yin, o_w_out[i], "nn", BF16, f"w_out_{tag}", tn=1024)
            saved.append(dict(x=xs, h=h, z=z, u=u, cw=cw, cv=cv, yin=yin, y=y))
        if layer < L - 1:
            xs, h = _post_pre_norm(xs, y, vec(layer, POST_G), vec(layer, GATE), vec(layer + 1, PRE_G),
                                   vec(layer + 1, SCALE), vec(layer + 1, SHIFT), f"post_pre_norm_{tag}")

    dx, dy, err_sq, dgate, g_post_last = _loss_post_norm_bwd(xs, y, vec(L - 1, POST_G), vec(L - 1, GATE), target,
                                                             "loss_post_norm_bwd")
    loss = lax.psum(_scaled_total(err_sq, 0.5 / D, "loss_total")[0, 0], MESH_AXES)

    g_pre, g_post, dmod = [None] * L, [None] * L, [None] * L
    g_e_w_in, g_e_w_uq, g_e_w_ukv, g_e_w_out = [None] * NE, [None] * NE, [None] * NE, [None] * NE
    g_scw, g_scb, g_qg, g_kvg = [None] * NE, [None] * NE, [None] * NE, [None] * NE
    g_o_w_in, g_o_w_out, g_ocw, g_ocb, g_olg, g_olb = ([None] * NO for _ in range(6))
    sm_w = [even_sc_conv_w, odd_conv_w, odd_conv_b, odd_ln_g, odd_ln_b]
    sm_rows = _pack(sm_w, F32, SUBLANES).shape[0]

    def small_slab():
        full = [_scatter_cols(jnp.stack(g_scw), even_sc_conv_w.shape[-1]),
                _scatter_cols(jnp.stack(g_ocw), odd_conv_w.shape[-1]),
                _scatter_cols(jnp.concatenate(g_ocb, 0), odd_conv_b.shape[-1]),
                _scatter_cols(jnp.concatenate(g_olg, 0), odd_ln_g.shape[-1]),
                _scatter_cols(jnp.concatenate(g_olb, 0), odd_ln_b.shape[-1])]
        flat = jnp.concatenate([g.reshape(N_DEV, -1) for g in full], axis=1)
        return jnp.pad(flat, ((0, 0), (0, sm_rows * PACK_COLS - flat.shape[1]))).reshape(N_DEV, sm_rows, PACK_COLS)

    scatters = []
    bw_token = jnp.zeros((SUBLANES, LANES), F32)

    def start_scatter(tag, names, parts):
        own = [lax.dynamic_slice_in_dim(g, me, 1, axis=0) for g in parts]
        handle, token = _exchange_start([g.astype(BF16) for g in parts], True, f"scatter_grads_start_{tag}")
        scatters.append((tag, names, handle, own))
        return token

    for layer in reversed(range(L)):
        i = layer // 2
        tag = f"l{layer}"
        sv = saved[layer]
        if layer == L - 1:
            g_post[layer] = g_post_last
        if layer % 2 == 0:
            dyc = _matmul(dy, e_w_out[i], "nt", BF16, f"d_ycat_{tag}", tn=1024)
            g_e_w_out[i] = _matmul(sv["ycat"], dy, "tn", BF16, f"g_w_out_{tag}", tn=1024).reshape(N_DEV, -1, D)
            if layer == 0:
                bw_token = start_scatter("l0_out", [("even_w_out", i)], [g_e_w_out[i]])
            dab, dag, dbg, dcv, do, delta = _even_bwd_gates(dyc, sv["z"], sv["cv"], sv["o"], f"even_gates_bwd_{tag}")
            du, dcw, g_scb[i] = _conv_bwd(dcv, sv["u"], sv["cw"] + bw_token[0, 0], SC_KERNEL, f"conv_bwd_{tag}")
            g_scw[i] = dcw[:SC_KERNEL]
            dq, dk, dv = _attn_bwd_t(sv["q"], sv["k"], sv["v"], sv["kT3"], do, sv["lse"], delta, f"attn_bwd_{tag}")
            (dqp, dkvp, dcq, dckv, dkr, g_qg[i], g_kvg[i]) = _qkv_bwd(
                dq, dk, dv, sv["z"], tabs, e_w_q_k[i], e_w_kv_k[i],
                row(even_q_norm_g[i]), row(even_kv_norm_g[i]), f"qkv_bwd_{tag}")
            gq = _matmul(sv["qn"], dqp, "tn", BF16, f"g_w_uq_{tag}", tn=1024)
            gkv = _matmul(sv["kvn"], dkvp, "tn", BF16, f"g_w_ukv_{tag}")
            g_e_w_uq[i] = jnp.moveaxis(gq.reshape(QL, HEADS, HEAD_PAD)[..., :QK_NOPE + QK_ROPE], 1, 0)
            g_e_w_ukv[i] = jnp.moveaxis(jnp.concatenate(
                [gkv[:, :HEADS * HEAD_PAD].reshape(KVL, HEADS, HEAD_PAD)[..., :QK_NOPE],
                 gkv[:, HEADS * HEAD_PAD:].reshape(KVL, HEADS, V_HEAD)], axis=-1), 1, 0)
            dz = _even_dz(dab, du, sv["z"], dag, dbg, dcq, dckv, dkr, f"even_dz_{tag}")
            gt = _matmul(dz, sv["h"], "tn", BF16, f"g_w_in_{tag}", tm=1024, tn=1024)
            g_e_w_in[i] = jnp.concatenate([gt[:2048], gt[2560:2944], gt[2944 + QK_NOPE:2944 + QK_NOPE + QK_ROPE],
                                           gt[2048:2560]], axis=0).reshape(N_DEV, -1, D)
            names = [("even_w_in", i), ("even_w_uq", i), ("even_w_ukv", i)]
            parts = [g_e_w_in[i], g_e_w_uq[i], g_e_w_ukv[i]]
            if layer == 0:
                names, parts = names + [("small", 0)], parts + [small_slab()]
            else:
                names, parts = names + [("even_w_out", i)], parts + [g_e_w_out[i]]
            bw_token = start_scatter(tag, names, parts)
            w_dh = e_w_in_k[i] + bw_token[0, 0].astype(BF16) if layer == 0 else e_w_in_k[i]
            dh = _matmul(dz, w_dh, "nn", BF16, f"d_h_{tag}", tn=1024)
        else:
            dyi = _matmul(dy, o_w_out[i], "nt", BF16, f"d_yin_{tag}", tn=1024)
            g_o_w_out[i] = _matmul(sv["yin"], dy, "tn", BF16, f"g_w_out_{tag}", tn=1024).reshape(N_DEV, -1, D)
            dcv, dsg, g_olg[i], g_olb[i] = _odd_bwd_norm(dyi, sv["cv"], sv["z"], row(o_ln_g[i]), row(o_ln_b[i]),
                                                         f"odd_norm_bwd_{tag}")
            du, dcw, g_ocb[i] = _conv_bwd(dcv, sv["u"], sv["cw"], CONF_KERNEL, f"conv_bwd_{tag}")
            g_ocw[i] = dcw[:CONF_KERNEL]
            dz = _odd_dz(du, sv["z"], dsg, f"odd_dz_{tag}")
            g_o_w_in[i] = _matmul(sv["h"], dz, "tn", BF16, f"g_w_in_{tag}", tm=1024, tn=odd_w_in.shape[-1],
                                  split_n=True)
            bw_token = start_scatter(tag, [("odd_w_in", i), ("odd_w_out", i)], [g_o_w_in[i], g_o_w_out[i]])
            dh = _matmul(dz, o_w_in[i], "nt", BF16, f"d_h_{tag}", tn=1024)
        if layer > 0:
            (dx, dy, dshift, dscale, g_pre[layer], dgate_prev, g_post[layer - 1]) = _pre_post_norm_bwd(
                dh, sv["x"], dx, vec(layer, PRE_G), vec(layer, SCALE), saved[layer - 1]["y"], vec(layer - 1, POST_G),
                vec(layer - 1, GATE), f"pre_post_norm_bwd_{tag}", after=bw_token)
        else:
            dx, dshift, dscale, g_pre[layer] = _pre_norm_bwd(dh, sv["x"], dx, vec(layer, PRE_G), vec(layer, SCALE),
                                                             f"pre_norm_bwd_{tag}", after=bw_token)
            dgate_prev = None
        dmod[layer] = jnp.concatenate([dshift, dscale, dgate], axis=-1)
        dgate = dgate_prev
    grad_x = dx.reshape(1, S, D)

    rep_g = [jnp.concatenate(dmod, 0), jnp.concatenate(g_pre, 0), jnp.concatenate(g_post, 0),
             jnp.stack(g_scb), jnp.stack(g_qg), jnp.stack(g_kvg)]
    rep_w = [ada_b, pre_norm_g, post_norm_g, even_sc_conv_b, even_q_norm_g, even_kv_norm_g]
    rep_m = [m_ada_b, m_pre_norm_g, m_post_norm_g, m_even_sc_conv_b, m_even_q_norm_g, m_even_kv_norm_g]
    rep_v = [v_ada_b, v_pre_norm_g, v_post_norm_g, v_even_sc_conv_b, v_even_q_norm_g, v_even_kv_norm_g]
    rep_shapes = [w.shape for w in rep_w]
    rep_all = _exchange([_pack(rep_g, F32, SUBLANES)], False, "gather_small_grads")[0]
    rep_out = _adamw(rep_all, _pack(rep_w, F32, SUBLANES), _pack(rep_m, F32, SUBLANES), _pack(rep_v, F32, SUBLANES),
                     "adamw_replicated")
    rep_res = [_unpack(o.reshape(-1), rep_shapes) for o in rep_out]

    dmod_all = rep_all.reshape(N_DEV, -1)[:, :L * 3 * D].reshape(N_DEV, L, 3 * D)
    dmod_cols = jnp.moveaxis(lax.dynamic_slice_in_dim(dmod_all, me * n_ada, n_ada, axis=2), 0, 1)
    g_ada_w = _ada_bwd(c_all.T, dmod_cols)
    ada_out = _adamw(g_ada_w.reshape(1, -1, n_ada), ada_w.reshape(-1, n_ada),
                     m_ada_w.reshape(-1, n_ada), v_ada_w.reshape(-1, n_ada), "adamw_ada_w")
    ada_res = [o.reshape(ada_w.shape) for o in ada_out]

    sm_m = [m_even_sc_conv_w, m_odd_conv_w, m_odd_conv_b, m_odd_ln_g, m_odd_ln_b]
    sm_v = [v_even_sc_conv_w, v_odd_conv_w, v_odd_conv_b, v_odd_ln_g, v_odd_ln_b]
    sm_shapes = [w.shape for w in sm_w]
    state = {"even_w_in": (even_w_in, m_even_w_in, v_even_w_in), "even_w_uq": (even_w_uq, m_even_w_uq, v_even_w_uq),
             "even_w_ukv": (even_w_ukv, m_even_w_ukv, v_even_w_ukv), "even_w_out": (even_w_out, m_even_w_out, v_even_w_out),
             "odd_w_in": (odd_w_in, m_odd_w_in, v_odd_w_in), "odd_w_out": (odd_w_out, m_odd_w_out, v_odd_w_out)}
    state["even_w_in"] = tuple(jnp.swapaxes(t, 1, 2) for t in state["even_w_in"])
    big_res = {name: None for name in state}
    after = [bw_token, grad_x, rep_out[0], ada_out[0]]
    sm_res = None
    for tag, names, handle, own in scatters:
        _, landed = _exchange_wait(handle, True, after, f"scatter_grads_wait_{tag}")
        after = []
        for a, (name, i) in enumerate(names):
            if name == "small":
                sm_out = _adamw([own[a], landed[a]], _pack(sm_w, F32, SUBLANES), _pack(sm_m, F32, SUBLANES),
                                _pack(sm_v, F32, SUBLANES), "adamw_small_sharded")
                sm_res = [_unpack(o.reshape(-1), sm_shapes) for o in sm_out]
                continue
            big_res[name] = _adamw_slab([own[a], landed[a]], *state[name], i, big_res[name], f"adamw_{name}_{i}")
            after += list(big_res[name])
    sh_res = [dict(zip(["even_sc_conv_w", "odd_conv_w", "odd_conv_b", "odd_ln_g", "odd_ln_b"], sm_res[kind]))
              for kind in range(4)]
    for name in state:
        for kind in range(4):
            res = big_res[name][kind]
            sh_res[kind][name] = jnp.swapaxes(res, 1, 2) if name == "even_w_in" else res

    order = ["ada_w", "ada_b", "pre_norm_g", "post_norm_g", "even_w_in", "even_sc_conv_w", "even_sc_conv_b",
             "even_q_norm_g", "even_kv_norm_g", "even_w_uq", "even_w_ukv", "even_w_out", "odd_w_in", "odd_conv_w",
             "odd_conv_b", "odd_ln_g", "odd_ln_b", "odd_w_out"]
    rep_names = ["ada_b", "pre_norm_g", "post_norm_g", "even_sc_conv_b", "even_q_norm_g", "even_kv_norm_g"]
    outs = [loss, grad_x]
    for kind in range(4):
        for name in order:
            if name == "ada_w":
                outs.append(ada_res[kind])
            elif name in rep_names:
                outs.append(rep_res[kind][rep_names.index(name)])
            else:
                outs.append(sh_res[kind][name])
    return tuple(outs)
```

```python
import math

import jax
import jax.numpy as jnp
from jax import lax
from jax.experimental import pallas as pl
from jax.experimental.pallas import tpu as pltpu

F32 = jnp.float32
BF16 = jnp.bfloat16
MESH_AXES = ("x", "y", "c")
N_DEV = 8
EPS = 1e-6
CHUNK = 64
HEADS = 8
QK_NOPE = 64
QK_ROPE = 32
V_HEAD = 64
HEAD_PAD = 128
ROPE_THETA = 10000.0
SC_KERNEL = 3
CONF_KERNEL = 31
LANES = 128
SUBLANES = 8
PACK_COLS = 1024
VMEM_LIMIT = 48 * 1024 * 1024
NEG = -1e30

ADAM_LR = 0.001
ADAM_B1 = 0.9
ADAM_B2 = 0.999
ADAM_EPS = 1e-08
ADAM_WD = 0.01
ADAM_STEP = 10


def _cparams():
    return pltpu.CompilerParams(vmem_limit_bytes=VMEM_LIMIT)


def _sigmoid(x):
    return 1.0 / (1.0 + jnp.exp(-x))


def _f32(ref):
    return ref[...].astype(F32)


def _silu(x):
    return x * _sigmoid(x)


def _dsilu(x):
    s = _sigmoid(x)
    return s * (1.0 + x * (1.0 - s))


def _rows(T, width, cb=0):
    return pl.BlockSpec((T, width), lambda i: (i, cb))


def _const(shape):
    nd = len(shape)
    return pl.BlockSpec(shape, lambda i: (0,) * nd)


def _wide_tile(S):
    return min(512, S)


def _row_tile(S):
    return min(512, S)


def _exchange(srcs, scatter, name):
    n = len(srcs)
    shapes = [tuple(s.shape[1:]) if scatter else tuple(s.shape) for s in srcs]

    def body(*refs):
        src_refs, out_refs = refs[:n], refs[n:2 * n]
        send_sems, recv_sems, local_sems = refs[2 * n:]
        x, y, c = lax.axis_index("x"), lax.axis_index("y"), lax.axis_index("c")
        me = 4 * x + 2 * y + c
        owns, copies = [], []
        for a in range(n):
            def piece(d, a=a):
                return src_refs[a].at[d] if scatter else src_refs[a]

            own = pltpu.make_async_copy(piece(me), out_refs[a].at[me], local_sems.at[a])
            own.start()
            owns.append(own)
            for k in range(1, N_DEV):
                px, py, pc = x ^ ((k >> 2) & 1), y ^ ((k >> 1) & 1), c ^ (k & 1)
                peer = 4 * px + 2 * py + pc
                sem = a * (N_DEV - 1) + k - 1
                cp = pltpu.make_async_remote_copy(
                    src_ref=piece(peer), dst_ref=out_refs[a].at[me],
                    send_sem=send_sems.at[sem], recv_sem=recv_sems.at[sem],
                    device_id=(px, py, pc), device_id_type=pl.DeviceIdType.MESH)
                cp.start()
                arrival = pltpu.make_async_remote_copy(
                    src_ref=piece(peer), dst_ref=out_refs[a].at[peer],
                    send_sem=send_sems.at[sem], recv_sem=recv_sems.at[sem],
                    device_id=(x, y, c), device_id_type=pl.DeviceIdType.MESH)
                copies.append((cp, arrival))
        for _, arrival in copies:
            arrival.wait_recv()
        for cp, _ in copies:
            cp.wait_send()
        for own in owns:
            own.wait()

    return pl.pallas_call(
        body, name=name,
        out_shape=tuple(jax.ShapeDtypeStruct((N_DEV,) + shp, s.dtype) for shp, s in zip(shapes, srcs)),
        in_specs=[pl.BlockSpec(memory_space=pl.ANY)] * n,
        out_specs=tuple(pl.BlockSpec(memory_space=pl.ANY) for _ in range(n)),
        scratch_shapes=[pltpu.SemaphoreType.DMA((n * (N_DEV - 1),)),
                        pltpu.SemaphoreType.DMA((n * (N_DEV - 1),)),
                        pltpu.SemaphoreType.DMA((n,))],
    )(*srcs)


_HBM = pl.BlockSpec(memory_space=pltpu.HBM)
_SEM = pl.BlockSpec(memory_space=pltpu.SEMAPHORE)


def _peer(k):
    x, y, c = lax.axis_index("x"), lax.axis_index("y"), lax.axis_index("c")
    return x ^ ((k >> 2) & 1), y ^ ((k >> 1) & 1), c ^ (k & 1)


def _exchange_start(srcs, scatter, name):
    n = len(srcs)
    shapes = [tuple(s.shape[1:]) if scatter else tuple(s.shape) for s in srcs]
    slots = N_DEV - 1 if scatter else N_DEV
    lands = [lax.empty((slots,) + shp, s.dtype) for shp, s in zip(shapes, srcs)]
    if not scatter:
        here = 4 * lax.axis_index("x") + 2 * lax.axis_index("y") + lax.axis_index("c")
        lands = [lax.dynamic_update_index_in_dim(l, s, here, 0) for l, s in zip(lands, srcs)]

    def body(*refs):
        src_refs, land_refs = refs[:n], refs[n:2 * n]
        send_sems, recv_sems = refs[2 * n], refs[2 * n + 1]
        token = refs[4 * n + 2]
        me = 4 * lax.axis_index("x") + 2 * lax.axis_index("y") + lax.axis_index("c")
        for a in range(n):
            for k in range(1, N_DEV):
                px, py, pc = _peer(k)
                peer = 4 * px + 2 * py + pc
                pltpu.make_async_remote_copy(
                    src_ref=src_refs[a].at[peer] if scatter else src_refs[a],
                    dst_ref=land_refs[a].at[k - 1] if scatter else land_refs[a].at[me],
                    send_sem=send_sems.at[a * (N_DEV - 1) + k - 1], recv_sem=recv_sems.at[a * (N_DEV - 1) + k - 1],
                    device_id=(px, py, pc), device_id_type=pl.DeviceIdType.MESH).start()
        token[...] = jnp.zeros_like(token)

    hbm = lambda arrs: [pltpu.HBM(a.shape, a.dtype) for a in arrs]
    out = pl.pallas_call(
        body, name=name,
        out_shape=(pltpu.SemaphoreType.DMA((n * (N_DEV - 1),)), pltpu.SemaphoreType.DMA((n * (N_DEV - 1),)),
                   *hbm(srcs), *hbm(lands), jax.ShapeDtypeStruct((SUBLANES, LANES), F32)),
        in_specs=[_HBM] * (2 * n),
        out_specs=(_SEM, _SEM, *([_HBM] * (2 * n)), pl.BlockSpec(memory_space=pltpu.VMEM)),
        input_output_aliases={a: 2 + a for a in range(2 * n)},
        compiler_params=pltpu.CompilerParams(has_side_effects=pltpu.SideEffectType.DATAFLOW_SIDE_EFFECTING),
    )(*[pltpu.with_memory_space_constraint(s, pltpu.HBM) for s in srcs],
      *[pltpu.with_memory_space_constraint(l, pltpu.HBM) for l in lands])
    return (out[0], out[1], list(out[2:2 + n]), list(out[2 + n:2 + 2 * n])), out[2 + 2 * n]


def _exchange_wait(handle, scatter, after, name):
    send_sems, recv_sems, srcs, lands = handle
    n = len(srcs)
    after = list(after) if isinstance(after, (list, tuple)) else [after]

    def body(*refs):
        src_refs, land_refs = refs[:n], refs[n:2 * n]
        send_sems, recv_sems = refs[2 * n], refs[2 * n + 1]
        for a in range(n):
            for k in range(1, N_DEV):
                px, py, pc = _peer(k)
                peer = 4 * px + 2 * py + pc
                cp = pltpu.make_async_remote_copy(
                    src_ref=src_refs[a].at[peer] if scatter else src_refs[a],
                    dst_ref=land_refs[a].at[k - 1] if scatter else land_refs[a].at[peer],
                    send_sem=send_sems.at[a * (N_DEV - 1) + k - 1], recv_sem=recv_sems.at[a * (N_DEV - 1) + k - 1],
                    device_id=(px, py, pc), device_id_type=pl.DeviceIdType.MESH)
                cp.wait_send()
                cp.wait_recv()

    out = pl.pallas_call(
        body, name=name,
        out_shape=tuple(pltpu.HBM(a.shape, a.dtype) for a in srcs + lands),
        in_specs=[_HBM] * (2 * n) + [_SEM, _SEM] + [pl.BlockSpec(memory_space=pl.ANY)] * len(after),
        out_specs=tuple([_HBM] * (2 * n)),
        input_output_aliases={a: a for a in range(2 * n)},
        compiler_params=pltpu.CompilerParams(has_side_effects=pltpu.SideEffectType.DATAFLOW_SIDE_EFFECTING),
    )(*srcs, *lands, send_sems, recv_sems, *after)
    return list(out[:n]), list(out[n:])


def _pack(parts, dtype, row_mult):
    flat = jnp.concatenate([p.reshape(-1).astype(dtype) for p in parts])
    n = flat.shape[0]
    rows = -(-n // PACK_COLS)
    rows = -(-rows // row_mult) * row_mult
    flat = jnp.pad(flat, (0, rows * PACK_COLS - n))
    return flat.reshape(rows, PACK_COLS)


def _unpack(flat, shapes):
    out, off = [], 0
    for shp in shapes:
        n = math.prod(shp)
        out.append(flat[..., off:off + n].reshape(flat.shape[:-1] + tuple(shp)))
        off += n
    return out


_DIMS = {"nn": (((1,), (0,)), ((), ())), "nt": (((1,), (1,)), ((), ())), "tn": (((0,), (0,)), ((), ()))}


def _matmul(a, b, mode, out_dtype, name, tm=512, tn=512, tk=None, split_n=False):
    if mode == "nn":
        (M, K), (_, N) = a.shape, b.shape
    elif mode == "nt":
        (M, K), (N, _) = a.shape, b.shape
    else:
        (K, M), (_, N) = a.shape, b.shape
    tm, tn = min(tm, M), min(tn, N)
    tk = K if tk is None else min(tk, K)
    nk = K // tk
    assert M % tm == 0 and N % tn == 0 and K % tk == 0, (name, a.shape, b.shape)

    def body(a_ref, b_ref, o_ref, *scratch):
        p = lax.dot_general(a_ref[...].astype(BF16), b_ref[...].astype(BF16), _DIMS[mode],
                            preferred_element_type=F32)
        if split_n:
            o_ref[0] = p.astype(out_dtype)
        elif nk == 1:
            o_ref[...] = p.astype(out_dtype)
        else:
            acc = scratch[0]
            k = pl.program_id(2)

            @pl.when(k == 0)
            def _():
                acc[...] = p

            @pl.when(k > 0)
            def _():
                acc[...] += p

            @pl.when(k == nk - 1)
            def _():
                o_ref[...] = acc[...].astype(out_dtype)

    a_spec = (pl.BlockSpec((tk, tm), lambda i, j, k: (k, i)) if mode == "tn"
              else pl.BlockSpec((tm, tk), lambda i, j, k: (i, k)))
    b_spec = (pl.BlockSpec((tn, tk), lambda i, j, k: (j, k)) if mode == "nt"
              else pl.BlockSpec((tk, tn), lambda i, j, k: (k, j)))
    return pl.pallas_call(
        body, name=name, grid=(M // tm, N // tn, nk),
        out_shape=jax.ShapeDtypeStruct((N // tn, M, tn) if split_n else (M, N), out_dtype),
        in_specs=[a_spec, b_spec],
        out_specs=(pl.BlockSpec((1, tm, tn), lambda i, j, k: (j, i, 0)) if split_n
                   else pl.BlockSpec((tm, tn), lambda i, j, k: (i, j))),
        scratch_shapes=[pltpu.VMEM((tm, tn), F32)] if nk > 1 else [],
        compiler_params=_cparams(),
    )(a, b)


def _ada_fwd(c_all, ada_w, ada_b_cols):
    L, D, n = ada_w.shape

    def body(c_ref, w_ref, b_ref, o_ref):
        act = _silu(c_ref[...]).astype(BF16)
        o_ref[0] = jnp.dot(act, w_ref[0].astype(BF16), preferred_element_type=F32) + b_ref[0]

    return pl.pallas_call(
        body, name="ada_fwd", grid=(L,),
        out_shape=jax.ShapeDtypeStruct((L, N_DEV, n), F32),
        in_specs=[pl.BlockSpec((N_DEV, D), lambda l: (0, 0)),
                  pl.BlockSpec((1, D, n), lambda l: (l, 0, 0)),
                  pl.BlockSpec((1, 1, n), lambda l: (l, 0, 0))],
        out_specs=pl.BlockSpec((1, N_DEV, n), lambda l: (l, 0, 0)),
        compiler_params=_cparams(),
    )(c_all, ada_w, ada_b_cols)


def _ada_bwd(c_all_t, dmod_cols):
    D = c_all_t.shape[0]
    L, _, n = dmod_cols.shape

    def body(c_ref, d_ref, o_ref):
        act = _silu(c_ref[...])
        dm = d_ref[0]
        acc = act[:, 0:1] * dm[0:1, :]
        for b in range(1, N_DEV):
            acc = acc + act[:, b:b + 1] * dm[b:b + 1, :]
        o_ref[0] = acc

    return pl.pallas_call(
        body, name="ada_bwd", grid=(L,),
        out_shape=jax.ShapeDtypeStruct((L, D, n), F32),
        in_specs=[pl.BlockSpec((D, N_DEV), lambda l: (0, 0)),
                  pl.BlockSpec((1, N_DEV, n), lambda l: (l, 0, 0))],
        out_specs=pl.BlockSpec((1, D, n), lambda l: (l, 0, 0)),
        compiler_params=_cparams(),
    )(c_all_t, dmod_cols)


def _rope_tables(pos_col, inv_lane):
    S = pos_col.shape[0]
    T = _row_tile(S)
    half = QK_ROPE // 2

    def body(p_ref, f_ref, c_ref, up_ref, dn_ref):
        ang = p_ref[...] * f_ref[...]
        lane = lax.broadcasted_iota(jnp.int32, ang.shape, 1)
        first = (lane >= QK_NOPE) & (lane < QK_NOPE + half)
        second = (lane >= QK_NOPE + half) & (lane < QK_NOPE + QK_ROPE)
        cs, sn = jnp.cos(ang), jnp.sin(ang)
        c_ref[...] = jnp.where(first | second, cs, 1.0)
        up_ref[...] = jnp.where(first, -sn, 0.0)
        dn_ref[...] = jnp.where(second, sn, 0.0)

    tab = jax.ShapeDtypeStruct((S, HEAD_PAD), F32)
    return pl.pallas_call(
        body, name="rope_tables", grid=(S // T,),
        out_shape=(tab, tab, tab),
        in_specs=[_rows(T, 1), _const((1, HEAD_PAD))],
        out_specs=(_rows(T, HEAD_PAD),) * 3,
        compiler_params=_cparams(),
    )(pos_col, inv_lane)


def _rope(blk, ct, ut, dt):
    half = QK_ROPE // 2
    up = pltpu.roll(blk, HEAD_PAD - half, 1)
    dn = pltpu.roll(blk, half, 1)
    return blk * ct + up * ut + dn * dt


def _rope_t(d, ct, ut, dt):
    half = QK_ROPE // 2
    return d * ct + pltpu.roll(d * ut, half, 1) + pltpu.roll(d * dt, HEAD_PAD - half, 1)


def _row_operands(rows, after=None):
    ops, specs = [], []
    for a in rows:
        if isinstance(a, tuple):
            table, r = a
            ops.append(table)
            specs.append(pl.BlockSpec((None, 1, table.shape[-1]), lambda *_, r=r: (r, 0, 0)))
        else:
            ops.append(a)
            specs.append(pl.BlockSpec(a.shape, lambda *_, nd=a.ndim: (0,) * nd))
    if after is not None:
        ops.append(after)
        specs.append(pl.BlockSpec(memory_space=pl.ANY))
    return ops, specs


def _pre_norm(x, g, scale, shift, name, after=None):
    S, D = x.shape
    T = _row_tile(S)
    row_ops, row_specs = _row_operands([g, scale, shift], after)

    def body(x_ref, g_ref, sc_ref, sh_ref, *rest):
        h_ref = rest[-1]
        xv = x_ref[...]
        rstd = lax.rsqrt(jnp.mean(xv * xv, axis=-1, keepdims=True) + EPS)
        h_ref[...] = ((xv * rstd) * g_ref[...] * (1.0 + sc_ref[...]) + sh_ref[...]).astype(BF16)

    return pl.pallas_call(
        body, name=name, grid=(S // T,),
        out_shape=jax.ShapeDtypeStruct((S, D), BF16),
        in_specs=[_rows(T, D)] + row_specs,
        out_specs=_rows(T, D), compiler_params=_cparams(),
    )(x, *row_ops)


def _fold8(v):
    T, C = v.shape
    return v.reshape(T // SUBLANES, SUBLANES, C).sum(axis=0)


def _col_sums(n_sums, body_fn, ins, in_specs, outs, out_specs, S, T, widths, name):
    n_in, n_out = len(ins), len(outs)
    nt = S // T

    def body(*refs):
        in_refs = refs[:n_in]
        out_refs = refs[n_in:n_in + n_out]
        sum_refs = refs[n_in + n_out:n_in + n_out + n_sums]
        accs = refs[n_in + n_out + n_sums:]
        i = pl.program_id(0)
        terms = body_fn(in_refs, out_refs)

        @pl.when(i == 0)
        def _():
            for acc, t in zip(accs, terms):
                acc[...] = _fold8(t)

        @pl.when(i > 0)
        def _():
            for acc, t in zip(accs, terms):
                acc[...] += _fold8(t)

        @pl.when(i == nt - 1)
        def _():
            for acc, s_ref in zip(accs, sum_refs):
                s_ref[...] = jnp.sum(acc[...], axis=0, keepdims=True)

    return pl.pallas_call(
        body, name=name, grid=(nt,),
        out_shape=tuple(outs) + tuple(jax.ShapeDtypeStruct((1, w), F32) for w in widths),
        in_specs=in_specs,
        out_specs=tuple(out_specs) + tuple(_const((1, w)) for w in widths),
        scratch_shapes=[pltpu.VMEM((SUBLANES, w), F32) for w in widths],
        compiler_params=_cparams(),
    )(*ins)


def _pre_norm_bwd(dh, x, dxo, g, scale, name, after=None):
    S, D = x.shape
    T = _row_tile(S)
    row_ops, row_specs = _row_operands([g, scale], after)

    def fn(ins, outs):
        dh_ref, x_ref, dxo_ref, g_ref, sc_ref = ins[:5]
        xv, dv = x_ref[...], _f32(dh_ref)
        rstd = lax.rsqrt(jnp.mean(xv * xv, axis=-1, keepdims=True) + EPS)
        xh = xv * rstd
        dr = dv * (1.0 + sc_ref[...])
        dxh = dr * g_ref[...]
        outs[0][...] = dxo_ref[...] + rstd * (dxh - xh * jnp.mean(dxh * xh, axis=-1, keepdims=True))
        return [dv, dv * (xh * g_ref[...]), dr * xh]

    return _col_sums(3, fn, [dh, x, dxo] + row_ops,
                     [_rows(T, D), _rows(T, D), _rows(T, D)] + row_specs,
                     [jax.ShapeDtypeStruct((S, D), F32)], [_rows(T, D)], S, T, [D, D, D], name)


def _post_pre_norm(x, y, g_post, gate, g_pre, scale, shift, name):
    S, D = x.shape
    T = _row_tile(S)
    row_ops, row_specs = _row_operands([g_post, gate, g_pre, scale, shift])

    def body(x_ref, y_ref, gp_ref, gt_ref, g_ref, sc_ref, sh_ref, xn_ref, h_ref):
        yv = _f32(y_ref)
        rstd_y = lax.rsqrt(jnp.mean(yv * yv, axis=-1, keepdims=True) + EPS)
        xn = x_ref[...] + gt_ref[...] * ((yv * rstd_y) * gp_ref[...])
        xn_ref[...] = xn
        rstd = lax.rsqrt(jnp.mean(xn * xn, axis=-1, keepdims=True) + EPS)
        h_ref[...] = ((xn * rstd) * g_ref[...] * (1.0 + sc_ref[...]) + sh_ref[...]).astype(BF16)

    return pl.pallas_call(
        body, name=name, grid=(S // T,),
        out_shape=(jax.ShapeDtypeStruct((S, D), F32), jax.ShapeDtypeStruct((S, D), BF16)),
        in_specs=[_rows(T, D), _rows(T, D)] + row_specs,
        out_specs=(_rows(T, D), _rows(T, D)), compiler_params=_cparams(),
    )(x, y, *row_ops)


def _pre_post_norm_bwd(dh, x, dxo, g_pre, scale, y_prev, g_post_prev, gate_prev, name, after=None):
    S, D = x.shape
    T = _row_tile(S)
    row_ops, row_specs = _row_operands([g_pre, scale, g_post_prev, gate_prev], after)

    def fn(ins, outs):
        dh_ref, x_ref, dxo_ref, y_ref, g_ref, sc_ref, gp_ref, gt_ref = ins[:8]
        xv, dv = x_ref[...], _f32(dh_ref)
        rstd = lax.rsqrt(jnp.mean(xv * xv, axis=-1, keepdims=True) + EPS)
        xh = xv * rstd
        dr = dv * (1.0 + sc_ref[...])
        dxh = dr * g_ref[...]
        dx = dxo_ref[...] + rstd * (dxh - xh * jnp.mean(dxh * xh, axis=-1, keepdims=True))
        outs[0][...] = dx
        yv = _f32(y_ref)
        rstd_y = lax.rsqrt(jnp.mean(yv * yv, axis=-1, keepdims=True) + EPS)
        yh = yv * rstd_y
        dn = dx * gt_ref[...]
        dyh = dn * gp_ref[...]
        outs[1][...] = (rstd_y * (dyh - yh * jnp.mean(dyh * yh, axis=-1, keepdims=True))).astype(BF16)
        return [dv, dv * (xh * g_ref[...]), dr * xh, dx * (yh * gp_ref[...]), dn * yh]

    return _col_sums(5, fn, [dh, x, dxo, y_prev] + row_ops,
                     [_rows(T, D), _rows(T, D), _rows(T, D), _rows(T, D)] + row_specs,
                     [jax.ShapeDtypeStruct((S, D), F32), jax.ShapeDtypeStruct((S, D), BF16)],
                     [_rows(T, D), _rows(T, D)], S, T, [D] * 5, name)


def _loss_post_norm_bwd(x, y, g_post, gate, target, name):
    S, D = x.shape
    T = _row_tile(S)
    row_ops, row_specs = _row_operands([g_post, gate])

    def fn(ins, outs):
        x_ref, y_ref, t_ref, gp_ref, gt_ref = ins
        yv = _f32(y_ref)
        rstd_y = lax.rsqrt(jnp.mean(yv * yv, axis=-1, keepdims=True) + EPS)
        yh = yv * rstd_y
        e = x_ref[...] + gt_ref[...] * (yh * gp_ref[...]) - t_ref[...]
        dx = e * (1.0 / D)
        outs[0][...] = dx
        dn = dx * gt_ref[...]
        dyh = dn * gp_ref[...]
        outs[1][...] = (rstd_y * (dyh - yh * jnp.mean(dyh * yh, axis=-1, keepdims=True))).astype(BF16)
        return [e * e, dx * (yh * gp_ref[...]), dn * yh]

    return _col_sums(3, fn, [x, y, target] + row_ops,
                     [_rows(T, D), _rows(T, D), _rows(T, D)] + row_specs,
                     [jax.ShapeDtypeStruct((S, D), F32), jax.ShapeDtypeStruct((S, D), BF16)],
                     [_rows(T, D), _rows(T, D)], S, T, [D] * 3, name)


def _scaled_total(v, coef, name):
    def body(v_ref, o_ref):
        o_ref[...] = jnp.broadcast_to(jnp.sum(v_ref[...], axis=1, keepdims=True) * coef, (1, LANES))

    return pl.pallas_call(body, name=name, out_shape=jax.ShapeDtypeStruct((1, LANES), F32))(v)


CONV_ROWS = 64
CONV_TILE = 1024


def _conv_halo(K):
    return SUBLANES if K - 1 <= SUBLANES else 32


def _conv_fwd(u, w, b, K, name):
    S, C = u.shape
    KP = w.shape[0]
    T, HB, RS = min(CONV_TILE, S), _conv_halo(K), CONV_ROWS
    ratio = T // HB

    def body(u_ref, h_ref, w_ref, b_ref, o_ref, ext):
        i = pl.program_id(1)
        ext[0:HB, :] = jnp.where(i > 0, h_ref[...], 0.0)
        ext[HB:HB + T, :] = u_ref[...]
        for r0 in range(0, T, RS):
            acc = jnp.broadcast_to(b_ref[...], (RS, LANES))
            for k in range(K):
                off = HB - (K - 1) + k + r0
                acc = acc + w_ref[k:k + 1, :] * ext[off:off + RS, :]
            o_ref[r0:r0 + RS, :] = acc

    return pl.pallas_call(
        body, name=name, grid=(C // LANES, S // T),
        out_shape=jax.ShapeDtypeStruct((S, C), F32),
        in_specs=[pl.BlockSpec((T, LANES), lambda c, i: (i, c)),
                  pl.BlockSpec((HB, LANES), lambda c, i: (jnp.maximum(i * ratio - 1, 0), c)),
                  pl.BlockSpec((KP, LANES), lambda c, i: (0, c)),
                  pl.BlockSpec((1, LANES), lambda c, i: (0, c))],
        out_specs=pl.BlockSpec((T, LANES), lambda c, i: (i, c)),
        scratch_shapes=[pltpu.VMEM((HB + T, LANES), F32)],
        compiler_params=_cparams(),
    )(u, u, w, b)


def _conv_bwd(d, u, w, K, name):
    S, C = u.shape
    KP = w.shape[0]
    T, HB, RS = min(CONV_TILE, S), _conv_halo(K), CONV_ROWS
    ratio = T // HB
    nt = S // T
    last_halo = S // HB - 1

    def body(d_ref, dn_ref, u_ref, up_ref, w_ref, du_ref, dw_ref, db_ref, extd, extu, dws, dbs):
        i = pl.program_id(1)
        extd[0:T, :] = d_ref[...]
        extd[T:T + HB, :] = jnp.where(i < nt - 1, dn_ref[...], 0.0)
        extu[0:HB, :] = jnp.where(i > 0, up_ref[...], 0.0)
        extu[HB:HB + T, :] = u_ref[...]

        @pl.when(i == 0)
        def _():
            dws[...] = jnp.zeros_like(dws)
            dbs[...] = jnp.zeros_like(dbs)

        for r0 in range(0, T, RS):
            acc = jnp.zeros((RS, LANES), F32)
            for k in range(K):
                off = (K - 1 - k) + r0
                acc = acc + w_ref[k:k + 1, :] * extd[off:off + RS, :]
            du_ref[r0:r0 + RS, :] = acc
            dch = d_ref[r0:r0 + RS, :]
            dbs[...] += _fold8(dch)
            for k in range(K):
                off = HB - (K - 1) + k + r0
                dws[k * SUBLANES:(k + 1) * SUBLANES, :] += _fold8(dch * extu[off:off + RS, :])

        @pl.when(i == nt - 1)
        def _():
            dw_ref[...] = jnp.zeros_like(dw_ref)
            for k in range(K):
                dw_ref[k:k + 1, :] = jnp.sum(dws[k * SUBLANES:(k + 1) * SUBLANES, :], axis=0, keepdims=True)
            db_ref[...] = jnp.sum(dbs[...], axis=0, keepdims=True)

    return pl.pallas_call(
        body, name=name, grid=(C // LANES, nt),
        out_shape=(jax.ShapeDtypeStruct((S, C), F32), jax.ShapeDtypeStruct((KP, C), F32),
                   jax.ShapeDtypeStruct((1, C), F32)),
        in_specs=[pl.BlockSpec((T, LANES), lambda c, i: (i, c)),
                  pl.BlockSpec((HB, LANES), lambda c, i: (jnp.minimum((i + 1) * ratio, last_halo), c)),
                  pl.BlockSpec((T, LANES), lambda c, i: (i, c)),
                  pl.BlockSpec((HB, LANES), lambda c, i: (jnp.maximum(i * ratio - 1, 0), c)),
                  pl.BlockSpec((KP, LANES), lambda c, i: (0, c))],
        out_specs=(pl.BlockSpec((T, LANES), lambda c, i: (i, c)),
                   pl.BlockSpec((KP, LANES), lambda c, i: (0, c)),
                   pl.BlockSpec((1, LANES), lambda c, i: (0, c))),
        scratch_shapes=[pltpu.VMEM((T + HB, LANES), F32), pltpu.VMEM((HB + T, LANES), F32),
                        pltpu.VMEM((KP * SUBLANES, LANES), F32), pltpu.VMEM((SUBLANES, LANES), F32)],
        compiler_params=_cparams(),
    )(d, d, u, u, w)


SCW = 512
ZE = 3072
QL = 256
KVL = 128


def _rms_rows(x, g):
    rstd = lax.rsqrt(jnp.mean(x * x, axis=-1, keepdims=True) + EPS)
    return (x * rstd) * g


def _attn_tile(S):
    return min(256, S)


_NT = (((1,), (1,)), ((), ()))


LOG2E = math.log2(math.e)
ATTN_FWD_HEADS = 8
ATTN_BWD_HEADS = 4


def _chunk_mask_t(T):
    key = lax.broadcasted_iota(jnp.int32, (T, T), 0) // CHUNK
    qry = lax.broadcasted_iota(jnp.int32, (T, T), 1) // CHUNK
    return key <= qry


W_IN_ROWS = 512


def _w_in_even(h, w_t, qg, kvg, name):
    S, D = h.shape
    tm = min(W_IN_ROWS, S)

    def body(h_ref, w_ref, qg_ref, kvg_ref, z_ref, u_ref, qn_ref, kvn_ref):
        p = lax.dot_general(h_ref[...], w_ref[...], _NT, preferred_element_type=F32)
        z_ref[...] = p.astype(BF16)
        u_ref[...] = p[:, SCW:2 * SCW] * p[:, 2 * SCW:3 * SCW]
        qn_ref[...] = _rms_rows(p[:, 5 * SCW:5 * SCW + QL], qg_ref[...]).astype(BF16)
        kvn_ref[...] = _rms_rows(p[:, 5 * SCW + QL:5 * SCW + QL + KVL], kvg_ref[...]).astype(BF16)

    return pl.pallas_call(
        body, name=name, grid=(S // tm,),
        out_shape=(jax.ShapeDtypeStruct((S, ZE), BF16), jax.ShapeDtypeStruct((S, SCW), F32),
                   jax.ShapeDtypeStruct((S, QL), BF16), jax.ShapeDtypeStruct((S, KVL), BF16)),
        in_specs=[_rows(tm, D), _const(w_t.shape), _const((1, QL)), _const((1, KVL))],
        out_specs=(_rows(tm, ZE), _rows(tm, SCW), _rows(tm, QL), _rows(tm, KVL)),
        compiler_params=_cparams(),
    )(h, w_t, qg, kvg)


def _w_in_odd(h, w, name):
    S, D = h.shape
    tm = min(W_IN_ROWS, S)

    def body(h_ref, w_ref, z_ref, u_ref):
        p = jnp.dot(h_ref[...], w_ref[...], preferred_element_type=F32)
        z_ref[...] = p.astype(BF16)
        u_ref[...] = p[:, 0:D] * _sigmoid(p[:, D:2 * D])

    return pl.pallas_call(
        body, name=name, grid=(S // tm,),
        out_shape=(jax.ShapeDtypeStruct((S, 3 * D), BF16), jax.ShapeDtypeStruct((S, D), F32)),
        in_specs=[_rows(tm, D), _const(w.shape)],
        out_specs=(_rows(tm, 3 * D), _rows(tm, D)),
        compiler_params=_cparams(),
    )(h, w)


def _qkv_fwd_t(qn, kvn, z, tabs, w_q, w_kv, name):
    S = qn.shape[0]
    T = _attn_tile(S)
    HW = HEADS * HEAD_PAD
    scale = LOG2E / math.sqrt(QK_NOPE + QK_ROPE)

    def body(qn_ref, kvn_ref, kr_ref, ct_ref, ut_ref, dt_ref, wq_ref, wkv_ref, q_ref, k_ref, v_ref, kt_ref, vt_ref):
        ct, ut, dt = ct_ref[...], ut_ref[...], dt_ref[...]
        qa = jnp.dot(qn_ref[...], wq_ref[...], preferred_element_type=F32)
        kva = jnp.dot(kvn_ref[...], wkv_ref[...], preferred_element_type=F32)
        kr = _f32(kr_ref)
        ones_row = (lax.broadcasted_iota(jnp.int32, (V_HEAD, T), 0) == 0).astype(F32)
        for h in range(HEADS):
            sl = slice(h * HEAD_PAD, (h + 1) * HEAD_PAD)
            q_ref[:, sl] = (_rope(qa[:, sl], ct, ut, dt) * scale).astype(BF16)
            kh = _rope(kva[:, sl] + kr, ct, ut, dt)
            k_ref[:, sl] = kh.astype(BF16)
            kt_ref[0, sl, :] = kh.T.astype(BF16)
        v_ref[...] = kva[:, HW:].astype(BF16)
        for p in range(HEADS // 2):
            vpt = kva[:, HW + p * LANES:HW + (p + 1) * LANES].T
            for h in range(2):
                r0 = (2 * p + h) * HEAD_PAD
                vt_ref[0, r0:r0 + V_HEAD, :] = vpt[h * V_HEAD:(h + 1) * V_HEAD, :].astype(BF16)
                vt_ref[0, r0 + V_HEAD:r0 + HEAD_PAD, :] = ones_row.astype(BF16)

    t3 = jax.ShapeDtypeStruct((S // T, HW, T), BF16)
    return pl.pallas_call(
        body, name=name, grid=(S // T,),
        out_shape=(jax.ShapeDtypeStruct((S, HW), BF16), jax.ShapeDtypeStruct((S, HW), BF16),
                   jax.ShapeDtypeStruct((S, HEADS * V_HEAD), BF16), t3, t3),
        in_specs=[_rows(T, QL), _rows(T, KVL), _rows(T, HEAD_PAD, 23),
                  _rows(T, HEAD_PAD), _rows(T, HEAD_PAD), _rows(T, HEAD_PAD),
                  _const(w_q.shape), _const(w_kv.shape)],
        out_specs=(_rows(T, HW), _rows(T, HW), _rows(T, HEADS * V_HEAD),
                   pl.BlockSpec((1, HW, T), lambda i: (i, 0, 0)), pl.BlockSpec((1, HW, T), lambda i: (i, 0, 0))),
        compiler_params=_cparams(),
    )(qn, kvn, z, *tabs, w_q, w_kv)


def _attn_fwd_t(q, k, vT3, name):
    S = q.shape[0]
    T = _attn_tile(S)
    nq = S // T
    NH = ATTN_FWD_HEADS
    PW = NH * HEAD_PAD

    def body(q_ref, k_ref, vt_ref, o_ref, lse_ref, m_s, acc_s):
        i = pl.program_id(1)
        m_s[...] = jnp.full_like(m_s, NEG)
        acc_s[...] = jnp.zeros_like(acc_s)
        qv = q_ref[...]

        def step(j, masked):
            kb = k_ref[pl.ds(pl.multiple_of(j * T, T), T), :]
            vt = vt_ref[j]
            heads = [slice(h * HEAD_PAD, (h + 1) * HEAD_PAD) for h in range(NH)]
            sts = [lax.dot_general(kb[:, sl], qv[:, sl], _NT, preferred_element_type=F32) for sl in heads]
            alphas, pvs = [], []
            for h, sl in enumerate(heads):
                st = jnp.where(_chunk_mask_t(T), sts[h], NEG) if masked else sts[h]
                m_prev = m_s[h]
                m_new = jnp.maximum(m_prev, jnp.max(st, axis=0, keepdims=True))
                alphas.append(jnp.exp2(m_prev[0:1] - m_new[0:1]))
                pt = jnp.exp2(st - m_new[0:1]).astype(BF16)
                m_s[h] = m_new
                pvs.append(jnp.dot(vt[sl, :], pt, preferred_element_type=F32))
            for h in range(NH):
                acc_s[h] = acc_s[h] * alphas[h] + pvs[h]

        def loop_body(j, carry):
            step(j, False)
            return carry

        lax.fori_loop(0, i, loop_body, 0)
        step(i, True)
        for g in range(NH // 2):
            outs = []
            for h in (2 * g, 2 * g + 1):
                acc = acc_s[h]
                l_row = acc[V_HEAD:V_HEAD + 1, :]
                outs.append(acc[0:V_HEAD, :] / l_row)
                lse_ref[0, h * SUBLANES:(h + 1) * SUBLANES, :] = m_s[h] + jnp.log2(l_row)
            o_ref[:, g * LANES:(g + 1) * LANES] = jnp.concatenate(outs, axis=0).T

    return pl.pallas_call(
        body, name=name, grid=(HEADS // NH, nq),
        out_shape=(jax.ShapeDtypeStruct((S, HEADS * V_HEAD), F32),
                   jax.ShapeDtypeStruct((nq, HEADS * SUBLANES, T), F32)),
        in_specs=[pl.BlockSpec((T, PW), lambda p, i: (i, p)),
                  pl.BlockSpec((S, PW), lambda p, i: (0, p)),
                  pl.BlockSpec((nq, PW, T), lambda p, i: (0, p, 0))],
        out_specs=(pl.BlockSpec((T, NH * V_HEAD), lambda p, i: (i, p)),
                   pl.BlockSpec((1, NH * SUBLANES, T), lambda p, i: (i, p, 0))),
        scratch_shapes=[pltpu.VMEM((NH, SUBLANES, T), F32), pltpu.VMEM((NH, HEAD_PAD, T), F32)],
        compiler_params=_cparams(),
    )(q, k, vT3)


def _attn_bwd_t(q, k, v, kT3, do, lse3, dl3, name):
    S = q.shape[0]
    T = _attn_tile(S)
    nq = S // T
    NH = ATTN_BWD_HEADS
    PW = NH * HEAD_PAD
    VW = NH * V_HEAD

    def body(q_ref, k_ref, v_ref, kt_ref, do_ref, lse_ref, dl_ref, dq_ref, dk_ref, dv_ref, dk_s, dv_s):
        j = pl.program_id(1)
        left = lax.broadcasted_iota(jnp.int32, (T, LANES), 1) < V_HEAD

        @pl.when(j == 0)
        def _():
            dq_ref[...] = jnp.zeros_like(dq_ref)

        dk_s[...] = jnp.zeros_like(dk_s)
        dv_s[...] = jnp.zeros_like(dv_s)
        kb = k_ref[...]
        vms = []
        for g in range(NH // 2):
            vb = v_ref[:, g * LANES:(g + 1) * LANES]
            vms += [jnp.where(left, vb, jnp.zeros_like(vb)), jnp.where(left, jnp.zeros_like(vb), vb)]
        kt = kt_ref[0]

        def step(i, masked):
            r0 = pl.multiple_of(i * T, T)
            qb = q_ref[pl.ds(r0, T), :]
            do_all = do_ref[pl.ds(r0, T), :]
            lse = lse_ref[i]
            dl = dl_ref[i]
            heads = [slice(h * HEAD_PAD, (h + 1) * HEAD_PAD) for h in range(NH)]
            dobs = [do_all[:, (h // 2) * LANES:(h // 2 + 1) * LANES] for h in range(NH)]
            sts = [lax.dot_general(kb[:, sl], qb[:, sl], _NT, preferred_element_type=F32) for sl in heads]
            dpts = [lax.dot_general(vms[h], dobs[h], _NT, preferred_element_type=F32) for h in range(NH)]
            res = []
            for h, sl in enumerate(heads):
                r8 = h * SUBLANES
                pt = jnp.exp2(sts[h] - lse[r8:r8 + 1, :])
                if masked:
                    pt = jnp.where(_chunk_mask_t(T), pt, 0.0)
                dst = (pt * (dpts[h] - dl[r8:r8 + 1, :])).astype(BF16)
                res.append((jnp.dot(pt.astype(BF16), dobs[h], preferred_element_type=F32),
                            jnp.dot(dst, qb[:, sl], preferred_element_type=F32),
                            jnp.dot(kt[sl, :], dst, preferred_element_type=F32)))
            for h, sl in enumerate(heads):
                dv_s[h] += res[h][0]
                dk_s[:, sl] += res[h][1]
                dq_ref[i, sl, :] += res[h][2]

        def loop_body(i, carry):
            step(i, False)
            return carry

        step(j, True)
        lax.fori_loop(j + 1, nq, loop_body, 0)
        dk_ref[...] = (dk_s[...] * (1.0 / LOG2E)).astype(BF16)
        for g in range(NH // 2):
            dv_ref[:, g * LANES:(g + 1) * LANES] = jnp.where(left, dv_s[2 * g], dv_s[2 * g + 1]).astype(BF16)

    return pl.pallas_call(
        body, name=name, grid=(HEADS // NH, nq),
        out_shape=(jax.ShapeDtypeStruct((nq, HEADS * HEAD_PAD, T), F32),
                   jax.ShapeDtypeStruct((S, HEADS * HEAD_PAD), BF16), jax.ShapeDtypeStruct((S, HEADS * V_HEAD), BF16)),
        in_specs=[pl.BlockSpec((S, PW), lambda p, j: (0, p)),
                  pl.BlockSpec((T, PW), lambda p, j: (j, p)),
                  pl.BlockSpec((T, VW), lambda p, j: (j, p)),
                  pl.BlockSpec((1, PW, T), lambda p, j: (j, p, 0)),
                  pl.BlockSpec((S, VW), lambda p, j: (0, p)),
                  pl.BlockSpec((nq, NH * SUBLANES, T), lambda p, j: (0, p, 0)),
                  pl.BlockSpec((nq, NH * SUBLANES, T), lambda p, j: (0, p, 0))],
        out_specs=(pl.BlockSpec((nq, PW, T), lambda p, j: (0, p, 0)),
                   pl.BlockSpec((T, PW), lambda p, j: (j, p)),
                   pl.BlockSpec((T, VW), lambda p, j: (j, p))),
        scratch_shapes=[pltpu.VMEM((T, PW), F32), pltpu.VMEM((NH, T, LANES), F32)],
        compiler_params=_cparams(),
    )(q, k, v, kT3, do, lse3, dl3)


def _even_post(z, cv, o, name):
    S = z.shape[0]
    T = _wide_tile(S)

    def body(ab_ref, ag_ref, bg_ref, cv_ref, o_ref, y_ref):
        y_ref[:, 0:SCW] = (_f32(ab_ref) * cv_ref[...] * _silu(_f32(ag_ref))).astype(BF16)
        y_ref[:, SCW:2 * SCW] = (o_ref[...] * _silu(_f32(bg_ref))).astype(BF16)

    return pl.pallas_call(
        body, name=name, grid=(S // T,),
        out_shape=jax.ShapeDtypeStruct((S, 2 * SCW), BF16),
        in_specs=[_rows(T, SCW, 0), _rows(T, SCW, 3), _rows(T, SCW, 4), _rows(T, SCW), _rows(T, SCW)],
        out_specs=_rows(T, 2 * SCW), compiler_params=_cparams(),
    )(z, z, z, cv, o)


def _even_bwd_gates(dyc, z, cv, o, name):
    S = z.shape[0]
    T = _attn_tile(S)

    def body(dya_ref, dyb_ref, ab_ref, ag_ref, bg_ref, cv_ref, o_ref,
             dab_ref, dag_ref, dbg_ref, dcv_ref, do_ref, dl_ref):
        dya, ab, ag, cv = _f32(dya_ref), _f32(ab_ref), _f32(ag_ref), cv_ref[...]
        sg = _silu(ag)
        dab_ref[...] = (dya * cv * sg).astype(BF16)
        dcv_ref[...] = dya * ab * sg
        dag_ref[...] = (dya * ab * cv * _dsilu(ag)).astype(BF16)
        dyb, bg, ov = _f32(dyb_ref), _f32(bg_ref), o_ref[...]
        dov = dyb * _silu(bg)
        do_ref[...] = dov.astype(BF16)
        dbg_ref[...] = (dyb * ov * _dsilu(bg)).astype(BF16)
        prod = dov * ov
        left = lax.broadcasted_iota(jnp.int32, (T, LANES), 1) < V_HEAD
        for p in range(HEADS // 2):
            blk = prod[:, p * LANES:(p + 1) * LANES]
            s0 = jnp.sum(jnp.where(left, blk, 0.0), axis=1, keepdims=True)
            s1 = jnp.sum(jnp.where(left, 0.0, blk), axis=1, keepdims=True)
            dt = jnp.where(left, s0, s1).T
            dl_ref[0, 2 * p * SUBLANES:(2 * p + 1) * SUBLANES, :] = dt[0:SUBLANES, :]
            dl_ref[0, (2 * p + 1) * SUBLANES:(2 * p + 2) * SUBLANES, :] = dt[V_HEAD:V_HEAD + SUBLANES, :]

    assert T == _attn_tile(S)
    bf = jax.ShapeDtypeStruct((S, SCW), BF16)
    ff = jax.ShapeDtypeStruct((S, SCW), F32)
    return pl.pallas_call(
        body, name=name, grid=(S // T,),
        out_shape=(bf, bf, bf, ff, bf, jax.ShapeDtypeStruct((S // T, HEADS * SUBLANES, T), F32)),
        in_specs=[_rows(T, SCW, 0), _rows(T, SCW, 1), _rows(T, SCW, 0), _rows(T, SCW, 3), _rows(T, SCW, 4),
                  _rows(T, SCW), _rows(T, SCW)],
        out_specs=(_rows(T, SCW),) * 5 + (pl.BlockSpec((1, HEADS * SUBLANES, T), lambda i: (i, 0, 0)),),
        compiler_params=_cparams(),
    )(dyc, dyc, z, z, z, cv, o)


def _qkv_bwd(dq, dk, dv, z, tabs, w_q, w_kv, qg, kvg, name):
    S = dk.shape[0]
    T = _attn_tile(S)
    HW = HEADS * HEAD_PAD
    VW = HEADS * V_HEAD
    scale = 1.0 / math.sqrt(QK_NOPE + QK_ROPE)

    def fn(ins, outs):
        dq_ref, dk_ref, dv_ref, cq_ref, ckv_ref, ct_ref, ut_ref, dt_ref, wq_ref, wkv_ref, qg_ref, kvg_ref = ins
        dqp_ref, dkvp_ref, dcq_ref, dckv_ref, dkr_ref = outs
        ct, ut, dt = ct_ref[...], ut_ref[...], dt_ref[...]
        dkr = jnp.zeros((T, HEAD_PAD), F32)
        for h in range(HEADS):
            sl = slice(h * HEAD_PAD, (h + 1) * HEAD_PAD)
            dqp_ref[:, sl] = (_rope_t(dq_ref[0, sl, :].T, ct, ut, dt) * scale).astype(BF16)
            dkh = _rope_t(dk_ref[:, sl].astype(F32), ct, ut, dt)
            dkr = dkr + dkh
            dkvp_ref[:, sl] = dkh.astype(BF16)
        dkvp_ref[:, HW:] = dv_ref[...].astype(BF16)
        dkr_ref[...] = dkr.astype(BF16)
        sums = []
        for lat_ref, g_ref, dpre_ref, w_ref, dlat_ref in ((cq_ref, qg_ref, dqp_ref, wq_ref, dcq_ref),
                                                         (ckv_ref, kvg_ref, dkvp_ref, wkv_ref, dckv_ref)):
            dn = lax.dot_general(dpre_ref[...], w_ref[...], _NT, preferred_element_type=F32)
            xv = _f32(lat_ref)
            rstd = lax.rsqrt(jnp.mean(xv * xv, axis=-1, keepdims=True) + EPS)
            xh = xv * rstd
            dxh = dn * g_ref[...]
            dlat_ref[...] = (rstd * (dxh - xh * jnp.mean(dxh * xh, axis=-1, keepdims=True))).astype(BF16)
            sums.append(dn * xh)
        return sums

    return _col_sums(
        2, fn, [dq, dk, dv, z, z, *tabs, w_q, w_kv, qg, kvg],
        [pl.BlockSpec((1, HW, T), lambda i: (i, 0, 0)), _rows(T, HW), _rows(T, VW), _rows(T, QL, 10), _rows(T, KVL, 22),
         _rows(T, HEAD_PAD), _rows(T, HEAD_PAD), _rows(T, HEAD_PAD),
         _const(w_q.shape), _const(w_kv.shape), _const((1, QL)), _const((1, KVL))],
        [jax.ShapeDtypeStruct((S, HW), BF16), jax.ShapeDtypeStruct((S, HW + VW), BF16),
         jax.ShapeDtypeStruct((S, QL), BF16), jax.ShapeDtypeStruct((S, KVL), BF16),
         jax.ShapeDtypeStruct((S, HEAD_PAD), BF16)],
        [_rows(T, HW), _rows(T, HW + VW), _rows(T, QL), _rows(T, KVL), _rows(T, HEAD_PAD)],
        S, T, [QL, KVL], name)


def _even_dz(dab, du, z, dag, dbg, dcq, dckv, dkr, name):
    S = z.shape[0]
    T = _wide_tile(S)

    def body(dab_ref, du_ref, ac_ref, ax_ref, dag_ref, dbg_ref, dcq_ref, dckv_ref, dkr_ref, dz_ref):
        duv = du_ref[...]
        dz_ref[:, 0:SCW] = dab_ref[...]
        dz_ref[:, SCW:2 * SCW] = (duv * _f32(ax_ref)).astype(BF16)
        dz_ref[:, 2 * SCW:3 * SCW] = (duv * _f32(ac_ref)).astype(BF16)
        dz_ref[:, 3 * SCW:4 * SCW] = dag_ref[...]
        dz_ref[:, 4 * SCW:5 * SCW] = dbg_ref[...]
        dz_ref[:, 5 * SCW:5 * SCW + QL] = dcq_ref[...]
        dz_ref[:, 5 * SCW + QL:5 * SCW + QL + KVL] = dckv_ref[...]
        dz_ref[:, 5 * SCW + QL + KVL:ZE] = dkr_ref[...]

    return pl.pallas_call(
        body, name=name, grid=(S // T,),
        out_shape=jax.ShapeDtypeStruct((S, ZE), BF16),
        in_specs=[_rows(T, SCW), _rows(T, SCW), _rows(T, SCW, 1), _rows(T, SCW, 2), _rows(T, SCW), _rows(T, SCW),
                  _rows(T, QL), _rows(T, KVL), _rows(T, HEAD_PAD)],
        out_specs=_rows(T, ZE), compiler_params=_cparams(),
    )(dab, du, z, z, dag, dbg, dcq, dckv, dkr)


def _layer_norm_stats(cv):
    mu = jnp.mean(cv, axis=-1, keepdims=True)
    cen = cv - mu
    rstd = lax.rsqrt(jnp.mean(cen * cen, axis=-1, keepdims=True) + EPS)
    return cen * rstd, rstd


def _odd_post(cv, z, ln_g, ln_b, name):
    S, D = cv.shape
    T = _wide_tile(S)

    def body(cv_ref, sg_ref, g_ref, b_ref, y_ref):
        cvh, _ = _layer_norm_stats(cv_ref[...])
        y_ref[...] = (_silu(cvh * g_ref[...] + b_ref[...]) * _silu(_f32(sg_ref))).astype(BF16)

    return pl.pallas_call(
        body, name=name, grid=(S // T,),
        out_shape=jax.ShapeDtypeStruct((S, D), BF16),
        in_specs=[_rows(T, D), _rows(T, D, 2), _const((1, D)), _const((1, D))],
        out_specs=_rows(T, D), compiler_params=_cparams(),
    )(cv, z, ln_g, ln_b)


def _odd_bwd_norm(dyi, cv, z, ln_g, ln_b, name):
    S, D = cv.shape
    T = _row_tile(S)

    def fn(ins, outs):
        dy_ref, cv_ref, sg_ref, g_ref, b_ref = ins
        dcv_ref, dsg_ref = outs
        cvh, rstd = _layer_norm_stats(cv_ref[...])
        ln = cvh * g_ref[...] + b_ref[...]
        sgv, dy = _f32(sg_ref), _f32(dy_ref)
        dsg_ref[...] = (dy * _silu(ln) * _dsilu(sgv)).astype(BF16)
        dln = dy * _silu(sgv) * _dsilu(ln)
        dh = dln * g_ref[...]
        dcv_ref[...] = rstd * (dh - jnp.mean(dh, axis=-1, keepdims=True)
                               - cvh * jnp.mean(dh * cvh, axis=-1, keepdims=True))
        return [dln * cvh, dln]

    return _col_sums(2, fn, [dyi, cv, z, ln_g, ln_b],
                     [_rows(T, D), _rows(T, D), _rows(T, D, 2), _const((1, D)), _const((1, D))],
                     [jax.ShapeDtypeStruct((S, D), F32), jax.ShapeDtypeStruct((S, D), BF16)],
                     [_rows(T, D), _rows(T, D)], S, T, [D, D], name)


def _odd_dz(du, z, dsg, name):
    S, D = du.shape
    T = _wide_tile(S)

    def body(du_ref, val_ref, glu_ref, dsg_ref, dz_ref):
        duv = du_ref[...]
        sig = _sigmoid(_f32(glu_ref))
        dz_ref[:, 0:D] = (duv * sig).astype(BF16)
        dz_ref[:, D:2 * D] = (duv * _f32(val_ref) * sig * (1.0 - sig)).astype(BF16)
        dz_ref[:, 2 * D:3 * D] = dsg_ref[...]

    return pl.pallas_call(
        body, name=name, grid=(S // T,),
        out_shape=jax.ShapeDtypeStruct((S, 3 * D), BF16),
        in_specs=[_rows(T, D), _rows(T, D, 0), _rows(T, D, 1), _rows(T, D)],
        out_specs=_rows(T, 3 * D), compiler_params=_cparams(),
    )(du, z, z, dsg)


ADAM_BLOCK_ELEMS = 128 * 1024


def _adam_tiles(R, C):
    if R * C <= ADAM_BLOCK_ELEMS:
        return R, C
    tr = R
    for cand in range(SUBLANES, R, SUBLANES):
        if R % cand == 0 and cand * C <= ADAM_BLOCK_ELEMS:
            tr = cand
    if tr < R:
        return tr, C
    tc = C
    for cand in range(LANES, C, LANES):
        if C % cand == 0 and R * cand <= ADAM_BLOCK_ELEMS:
            tc = cand
    return R, tc


def _adamw(g_parts, w, m, v, name):
    if not isinstance(g_parts, (list, tuple)):
        g_parts = [g_parts]
    ng = len(g_parts)
    _, R, C = g_parts[0].shape
    tr, tc = _adam_tiles(R, C)

    def body(*refs):
        g_refs = refs[:ng]
        w_ref, m_ref, v_ref, go_ref, d_ref, mo_ref, vo_ref = refs[ng:]
        g = None
        for g_ref in g_refs:
            for p in range(g_ref.shape[0]):
                part = g_ref[p].astype(F32)
                g = part if g is None else g + part
        mn = ADAM_B1 * m_ref[...] + (1.0 - ADAM_B1) * g
        vn = ADAM_B2 * v_ref[...] + (1.0 - ADAM_B2) * (g * g)
        m_hat = mn / (1.0 - ADAM_B1 ** ADAM_STEP)
        v_hat = vn / (1.0 - ADAM_B2 ** ADAM_STEP)
        go_ref[...] = g
        d_ref[...] = -ADAM_LR * (m_hat / (jnp.sqrt(v_hat) + ADAM_EPS) + ADAM_WD * w_ref[...])
        mo_ref[...] = mn
        vo_ref[...] = vn

    slab = jax.ShapeDtypeStruct((R, C), F32)
    blk = pl.BlockSpec((tr, tc), lambda i, j: (i, j))
    return pl.pallas_call(
        body, name=name, grid=(R // tr, C // tc),
        out_shape=(slab,) * 4,
        in_specs=[pl.BlockSpec((g.shape[0], tr, tc), lambda i, j: (0, i, j)) for g in g_parts] + [blk, blk, blk],
        out_specs=(blk,) * 4, compiler_params=_cparams(),
    )(*g_parts, w, m, v)


def _adamw_slab(g_parts, w, m, v, layer, prev, name):
    ng = len(g_parts)
    NL, R, C = w.shape
    tr, tc = _adam_tiles(R, C)

    def body(*refs):
        g_refs = refs[:ng]
        w_ref, m_ref, v_ref = refs[ng:ng + 3]
        go_ref, d_ref, mo_ref, vo_ref = refs[-4:]
        g = None
        for g_ref in g_refs:
            for p in range(g_ref.shape[0]):
                part = g_ref[p].astype(F32)
                g = part if g is None else g + part
        mn = ADAM_B1 * m_ref[0] + (1.0 - ADAM_B1) * g
        vn = ADAM_B2 * v_ref[0] + (1.0 - ADAM_B2) * (g * g)
        m_hat = mn / (1.0 - ADAM_B1 ** ADAM_STEP)
        v_hat = vn / (1.0 - ADAM_B2 ** ADAM_STEP)
        go_ref[0] = g
        d_ref[0] = -ADAM_LR * (m_hat / (jnp.sqrt(v_hat) + ADAM_EPS) + ADAM_WD * w_ref[0])
        mo_ref[0] = mn
        vo_ref[0] = vn

    blk = pl.BlockSpec((1, tr, tc), lambda i, j: (layer, i, j))
    n_in = ng + 3
    prev = list(prev) if prev is not None else []
    return pl.pallas_call(
        body, name=name, grid=(R // tr, C // tc),
        out_shape=(jax.ShapeDtypeStruct((NL, R, C), F32),) * 4,
        in_specs=([pl.BlockSpec((g.shape[0], tr, tc), lambda i, j: (0, i, j)) for g in g_parts] + [blk, blk, blk]
                  + [pl.BlockSpec(memory_space=pl.ANY)] * len(prev)),
        out_specs=(blk,) * 4,
        input_output_aliases={n_in + k: k for k in range(len(prev))},
        compiler_params=_cparams(),
    )(*g_parts, w, m, v, *prev)


def _gather_cols(g, shape):
    nd = len(shape)
    t = jnp.moveaxis(g, 0, nd - 1)
    return t.reshape(tuple(shape[:-1]) + (N_DEV * shape[-1],))


def _scatter_cols(full, n):
    t = full.reshape(full.shape[:-1] + (N_DEV, n))
    return jnp.moveaxis(t, -2, 0)


def kernel(x, c, positions, ada_w, ada_b, pre_norm_g, post_norm_g, even_w_in, even_sc_conv_w, even_sc_conv_b, even_q_norm_g, even_kv_norm_g, even_w_uq, even_w_ukv, even_w_out, odd_w_in, odd_conv_w, odd_conv_b, odd_ln_g, odd_ln_b, odd_w_out, loss_target, m_ada_w, m_ada_b, m_pre_norm_g, m_post_norm_g, m_even_w_in, m_even_sc_conv_w, m_even_sc_conv_b, m_even_q_norm_g, m_even_kv_norm_g, m_even_w_uq, m_even_w_ukv, m_even_w_out, m_odd_w_in, m_odd_conv_w, m_odd_conv_b, m_odd_ln_g, m_odd_ln_b, m_odd_w_out, v_ada_w, v_ada_b, v_pre_norm_g, v_post_norm_g, v_even_w_in, v_even_sc_conv_w, v_even_sc_conv_b, v_even_q_norm_g, v_even_kv_norm_g, v_even_w_uq, v_even_w_ukv, v_even_w_out, v_odd_w_in, v_odd_conv_w, v_odd_conv_b, v_odd_ln_g, v_odd_ln_b, v_odd_w_out):
    S, D = x.shape[1], x.shape[2]
    L = ada_w.shape[0]
    NE, NO = even_w_in.shape[0], odd_w_in.shape[0]
    me = 4 * lax.axis_index("x") + 2 * lax.axis_index("y") + lax.axis_index("c")
    x0 = x[0]
    target = loss_target[0]

    small_parts = [c, even_sc_conv_w, odd_conv_w, odd_conv_b, odd_ln_g, odd_ln_b]
    small_shapes = [p.shape for p in small_parts]
    sg = _exchange([_pack(small_parts, F32, SUBLANES)], False, "gather_small")[0].reshape(N_DEV, -1)
    c_all, scw_g, ocw_g, ocb_g, olg_g, olb_g = _unpack(sg, small_shapes)
    c_all = c_all.reshape(N_DEV, D)
    sc_conv_w = _gather_cols(scw_g, even_sc_conv_w.shape)
    o_conv_w = _gather_cols(ocw_g, odd_conv_w.shape)
    o_conv_b = _gather_cols(ocb_g, odd_conv_b.shape)
    o_ln_g = _gather_cols(olg_g, odd_ln_g.shape)
    o_ln_b = _gather_cols(olb_g, odd_ln_b.shape)

    pad_q = HEAD_PAD - QK_NOPE - QK_ROPE
    w_local = [jnp.swapaxes(even_w_in, 1, 2).astype(BF16),
               jnp.pad(even_w_uq, ((0, 0), (0, 0), (0, pad_q))).astype(BF16),
               jnp.pad(even_w_ukv[..., :QK_NOPE], ((0, 0), (0, 0), (0, HEAD_PAD - QK_NOPE))).astype(BF16),
               even_w_ukv[..., QK_NOPE:].astype(BF16),
               even_w_out.astype(BF16), odd_w_in.astype(BF16), odd_w_out.astype(BF16)]
    n_ada = ada_w.shape[2]
    ada_b_cols = lax.dynamic_slice_in_dim(ada_b, me * n_ada, n_ada, axis=1).reshape(L, 1, n_ada)
    mod_slab = _ada_fwd(c_all, ada_w, ada_b_cols)
    mod_g = _exchange([_pack([mod_slab], F32, SUBLANES)], False, "gather_mod")[0].reshape(N_DEV, -1)
    mod_all = mod_g[:, :L * N_DEV * n_ada].reshape(N_DEV, L, N_DEV, n_ada)
    mod = lax.dynamic_index_in_dim(mod_all, me, axis=2, keepdims=False)
    mod = jnp.moveaxis(mod, 0, 1).reshape(L, 3 * D)
    shift, scale, gate = mod[:, :D], mod[:, D:2 * D], mod[:, 2 * D:]

    heads_to_cols = lambda g: jnp.moveaxis(g, 0, 1).reshape(g.shape[1], -1)
    w_handles = {}
    token = jnp.broadcast_to(jnp.minimum(jnp.abs(mod[0, 0]), 0.0), (SUBLANES, LANES))
    for layer in range(L):
        i = layer // 2
        groups = ({"in": [w_local[0][i]], "rest": [w[i] for w in w_local[1:5]]} if layer % 2 == 0
                  else {"all": [w[i] for w in w_local[5:]]})
        for key, mine in groups.items():
            mine = [w + token[0, 0].astype(BF16) for w in mine]
            w_handles[layer, key], token = _exchange_start(mine, False, f"gather_weights_start_l{layer}_{key}")
    w_token = token

    def arrived(layer, key, after):
        return _exchange_wait(w_handles[layer, key], False, after, f"gather_weights_wait_l{layer}_{key}")[1]

    e_w_in_k, e_w_q_k, e_w_kv_k, e_w_out, o_w_in, o_w_out = ([None] * NE, [None] * NE, [None] * NE, [None] * NE,
                                                             [None] * NO, [None] * NO)

    inv_freq = 1.0 / (ROPE_THETA ** (jnp.arange(0, QK_ROPE, 2, dtype=F32) / QK_ROPE))
    inv_lane = jnp.zeros((HEAD_PAD,), F32).at[QK_NOPE:QK_NOPE + QK_ROPE].set(jnp.concatenate([inv_freq, inv_freq]))
    tabs = _rope_tables(positions.astype(F32).reshape(S, 1), inv_lane.reshape(1, HEAD_PAD))

    row = lambda a: a.reshape(1, -1)
    scb = even_sc_conv_b
    KP3, KP31 = SUBLANES, 32

    saved = []
    xs = x0
    lv = jnp.stack([pre_norm_g, post_norm_g, scale, shift, gate], axis=1).reshape(L * 5, 1, D)
    PRE_G, POST_G, SCALE, SHIFT, GATE = range(5)
    vec = lambda layer, k: (lv, layer * 5 + k)
    h = _pre_norm(xs, vec(0, PRE_G), vec(0, SCALE), vec(0, SHIFT), "pre_norm_l0", after=w_token)
    for layer in range(L):
        i = layer // 2
        tag = f"l{layer}"
        first = [h, tabs[0]] if layer == 0 else h
        if layer % 2 == 0:
            wt = arrived(layer, "in", first)[0].reshape(-1, D)
            e_w_in_k[i] = jnp.concatenate([wt[:2048], wt[2464:2976], wt[2048:2432], jnp.zeros((QK_NOPE, D), BF16),
                                           wt[2432:2464], jnp.zeros((pad_q, D), BF16)], axis=0)
            z, u, qn, kvn = _w_in_even(h, e_w_in_k[i], row(even_q_norm_g[i]), row(even_kv_norm_g[i]), f"w_in_{tag}")
            eq_g, ek_g, ev_g, eout_g = arrived(layer, "rest", z)
            e_w_q_k[i] = heads_to_cols(eq_g)
            e_w_kv_k[i] = jnp.concatenate([heads_to_cols(ek_g), heads_to_cols(ev_g)], axis=-1)
            e_w_out[i] = eout_g.reshape(-1, D)
            cw = jnp.pad(sc_conv_w[i], ((0, KP3 - SC_KERNEL), (0, 0)))
            cv = _conv_fwd(u, cw, row(scb[i]), SC_KERNEL, f"conv_{tag}")
            q, k, v, kT3, vT3 = _qkv_fwd_t(qn, kvn, z, tabs, e_w_q_k[i], e_w_kv_k[i], f"qkv_{tag}")
            o, lse = _attn_fwd_t(q, k, vT3, f"attn_{tag}")
            ycat = _even_post(z, cv, o, f"even_post_{tag}")
            y = _matmul(ycat, e_w_out[i], "nn", BF16, f"w_out_{tag}", tn=1024)
            saved.append(dict(x=xs, h=h, z=z, u=u, qn=qn, kvn=kvn, cw=cw, cv=cv, q=q, k=k, v=v, kT3=kT3, o=o, lse=lse,
                              ycat=ycat, y=y))
        else:
            owin_g, oout_g = arrived(layer, "all", first)
            o_w_in[i], o_w_out[i] = heads_to_cols(owin_g), oout_g.reshape(-1, D)
            z, u = _w_in_odd(h, o_w_in[i], f"w_in_{tag}")
            cw = jnp.pad(o_conv_w[i], ((0, KP31 - CONF_KERNEL), (0, 0)))
            cv = _conv_fwd(u, cw, row(o_conv_b[i]), CONF_KERNEL, f"conv_{tag}")
            yin = _odd_post(cv, z, row(o_ln_g[i]), row(o_ln_b[i]), f"odd_post_{tag}")
            y = _matmul(yin, o_w_out[i], "nn", BF16, f"w_out_{tag}", tn=1024)
            saved.append(dict(x=xs, h=h, z=z, u=u, cw=cw, cv=cv, yin=yin, y=y))
        if layer < L - 1:
            xs, h = _post_pre_norm(xs, y, vec(layer, POST_G), vec(layer, GATE), vec(layer + 1, PRE_G),
                                   vec(layer + 1, SCALE), vec(layer + 1, SHIFT), f"post_pre_norm_{tag}")

    dx, dy, err_sq, dgate, g_post_last = _loss_post_norm_bwd(xs, y, vec(L - 1, POST_G), vec(L - 1, GATE), target,
                                                             "loss_post_norm_bwd")
    loss = lax.psum(_scaled_total(err_sq, 0.5 / D, "loss_total")[0, 0], MESH_AXES)

    g_pre, g_post, dmod = [None] * L, [None] * L, [None] * L
    g_e_w_in, g_e_w_uq, g_e_w_ukv, g_e_w_out = [None] * NE, [None] * NE, [None] * NE, [None] * NE
    g_scw, g_scb, g_qg, g_kvg = [None] * NE, [None] * NE, [None] * NE, [None] * NE
    g_o_w_in, g_o_w_out, g_ocw, g_ocb, g_olg, g_olb = ([None] * NO for _ in range(6))
    sm_w = [even_sc_conv_w, odd_conv_w, odd_conv_b, odd_ln_g, odd_ln_b]
    sm_rows = _pack(sm_w, F32, SUBLANES).shape[0]

    def small_slab():
        full = [_scatter_cols(jnp.stack(g_scw), even_sc_conv_w.shape[-1]),
                _scatter_cols(jnp.stack(g_ocw), odd_conv_w.shape[-1]),
                _scatter_cols(jnp.concatenate(g_ocb, 0), odd_conv_b.shape[-1]),
                _scatter_cols(jnp.concatenate(g_olg, 0), odd_ln_g.shape[-1]),
                _scatter_cols(jnp.concatenate(g_olb, 0), odd_ln_b.shape[-1])]
        flat = jnp.concatenate([g.reshape(N_DEV, -1) for g in full], axis=1)
        return jnp.pad(flat, ((0, 0), (0, sm_rows * PACK_COLS - flat.shape[1]))).reshape(N_DEV, sm_rows, PACK_COLS)

    scatters = []
    bw_token = jnp.zeros((SUBLANES, LANES), F32)

    def start_scatter(tag, names, parts):
        own = [lax.dynamic_slice_in_dim(g, me, 1, axis=0) for g in parts]
        handle, token = _exchange_start([g.astype(BF16) for g in parts], True, f"scatter_grads_start_{tag}")
        scatters.append((tag, names, handle, own))
        return token

    for layer in reversed(range(L)):
        i = layer // 2
        tag = f"l{layer}"
        sv = saved[layer]
        if layer == L - 1:
            g_post[layer] = g_post_last
        if layer % 2 == 0:
            dyc = _matmul(dy, e_w_out[i], "nt", BF16, f"d_ycat_{tag}", tn=1024)
            g_e_w_out[i] = _matmul(sv["ycat"], dy, "tn", BF16, f"g_w_out_{tag}", tn=1024).reshape(N_DEV, -1, D)
            if layer == 0:
                bw_token = start_scatter("l0_out", [("even_w_out", i)], [g_e_w_out[i]])
            dab, dag, dbg, dcv, do, delta = _even_bwd_gates(dyc, sv["z"], sv["cv"], sv["o"], f"even_gates_bwd_{tag}")
            du, dcw, g_scb[i] = _conv_bwd(dcv, sv["u"], sv["cw"] + bw_token[0, 0], SC_KERNEL, f"conv_bwd_{tag}")
            g_scw[i] = dcw[:SC_KERNEL]
            dq, dk, dv = _attn_bwd_t(sv["q"], sv["k"], sv["v"], sv["kT3"], do, sv["lse"], delta, f"attn_bwd_{tag}")
            (dqp, dkvp, dcq, dckv, dkr, g_qg[i], g_kvg[i]) = _qkv_bwd(
                dq, dk, dv, sv["z"], tabs, e_w_q_k[i], e_w_kv_k[i],
                row(even_q_norm_g[i]), row(even_kv_norm_g[i]), f"qkv_bwd_{tag}")
            gq = _matmul(sv["qn"], dqp, "tn", BF16, f"g_w_uq_{tag}", tn=1024)
            gkv = _matmul(sv["kvn"], dkvp, "tn", BF16, f"g_w_ukv_{tag}")
            g_e_w_uq[i] = jnp.moveaxis(gq.reshape(QL, HEADS, HEAD_PAD)[..., :QK_NOPE + QK_ROPE], 1, 0)
            g_e_w_ukv[i] = jnp.moveaxis(jnp.concatenate(
                [gkv[:, :HEADS * HEAD_PAD].reshape(KVL, HEADS, HEAD_PAD)[..., :QK_NOPE],
                 gkv[:, HEADS * HEAD_PAD:].reshape(KVL, HEADS, V_HEAD)], axis=-1), 1, 0)
            dz = _even_dz(dab, du, sv["z"], dag, dbg, dcq, dckv, dkr, f"even_dz_{tag}")
            gt = _matmul(dz, sv["h"], "tn", BF16, f"g_w_in_{tag}", tm=1024, tn=1024)
            g_e_w_in[i] = jnp.concatenate([gt[:2048], gt[2560:2944], gt[2944 + QK_NOPE:2944 + QK_NOPE + QK_ROPE],
                                           gt[2048:2560]], axis=0).reshape(N_DEV, -1, D)
            names = [("even_w_in", i), ("even_w_uq", i), ("even_w_ukv", i)]
            parts = [g_e_w_in[i], g_e_w_uq[i], g_e_w_ukv[i]]
            if layer == 0:
                names, parts = names + [("small", 0)], parts + [small_slab()]
            else:
                names, parts = names + [("even_w_out", i)], parts + [g_e_w_out[i]]
            bw_token = start_scatter(tag, names, parts)
            w_dh = e_w_in_k[i] + bw_token[0, 0].astype(BF16) if layer == 0 else e_w_in_k[i]
            dh = _matmul(dz, w_dh, "nn", BF16, f"d_h_{tag}", tn=1024)
        else:
            dyi = _matmul(dy, o_w_out[i], "nt", BF16, f"d_yin_{tag}", tn=1024)
            g_o_w_out[i] = _matmul(sv["yin"], dy, "tn", BF16, f"g_w_out_{tag}", tn=1024).reshape(N_DEV, -1, D)
            dcv, dsg, g_olg[i], g_olb[i] = _odd_bwd_norm(dyi, sv["cv"], sv["z"], row(o_ln_g[i]), row(o_ln_b[i]),
                                                         f"odd_norm_bwd_{tag}")
            du, dcw, g_ocb[i] = _conv_bwd(dcv, sv["u"], sv["cw"], CONF_KERNEL, f"conv_bwd_{tag}")
            g_ocw[i] = dcw[:CONF_KERNEL]
            dz = _odd_dz(du, sv["z"], dsg, f"odd_dz_{tag}")
            g_o_w_in[i] = _matmul(sv["h"], dz, "tn", BF16, f"g_w_in_{tag}", tm=1024, tn=odd_w_in.shape[-1],
                                  split_n=True)
            bw_token = start_scatter(tag, [("odd_w_in", i), ("odd_w_out", i)], [g_o_w_in[i], g_o_w_out[i]])
            dh = _matmul(dz, o_w_in[i], "nt", BF16, f"d_h_{tag}", tn=1024)
        if layer > 0:
            (dx, dy, dshift, dscale, g_pre[layer], dgate_prev, g_post[layer - 1]) = _pre_post_norm_bwd(
                dh, sv["x"], dx, vec(layer, PRE_G), vec(layer, SCALE), saved[layer - 1]["y"], vec(layer - 1, POST_G),
                vec(layer - 1, GATE), f"pre_post_norm_bwd_{tag}", after=bw_token)
        else:
            dx, dshift, dscale, g_pre[layer] = _pre_norm_bwd(dh, sv["x"], dx, vec(layer, PRE_G), vec(layer, SCALE),
                                                             f"pre_norm_bwd_{tag}", after=bw_token)
            dgate_prev = None
        dmod[layer] = jnp.concatenate([dshift, dscale, dgate], axis=-1)
        dgate = dgate_prev
    grad_x = dx.reshape(1, S, D)

    rep_g = [jnp.concatenate(dmod, 0), jnp.concatenate(g_pre, 0), jnp.concatenate(g_post, 0),
             jnp.stack(g_scb), jnp.stack(g_qg), jnp.stack(g_kvg)]
    rep_w = [ada_b, pre_norm_g, post_norm_g, even_sc_conv_b, even_q_norm_g, even_kv_norm_g]
    rep_m = [m_ada_b, m_pre_norm_g, m_post_norm_g, m_even_sc_conv_b, m_even_q_norm_g, m_even_kv_norm_g]
    rep_v = [v_ada_b, v_pre_norm_g, v_post_norm_g, v_even_sc_conv_b, v_even_q_norm_g, v_even_kv_norm_g]
    rep_shapes = [w.shape for w in rep_w]
    rep_all = _exchange([_pack(rep_g, F32, SUBLANES)], False, "gather_small_grads")[0]
    rep_out = _adamw(rep_all, _pack(rep_w, F32, SUBLANES), _pack(rep_m, F32, SUBLANES), _pack(rep_v, F32, SUBLANES),
                     "adamw_replicated")
    rep_res = [_unpack(o.reshape(-1), rep_shapes) for o in rep_out]

    dmod_all = rep_all.reshape(N_DEV, -1)[:, :L * 3 * D].reshape(N_DEV, L, 3 * D)
    dmod_cols = jnp.moveaxis(lax.dynamic_slice_in_dim(dmod_all, me * n_ada, n_ada, axis=2), 0, 1)
    g_ada_w = _ada_bwd(c_all.T, dmod_cols)
    ada_out = _adamw(g_ada_w.reshape(1, -1, n_ada), ada_w.reshape(-1, n_ada),
                     m_ada_w.reshape(-1, n_ada), v_ada_w.reshape(-1, n_ada), "adamw_ada_w")
    ada_res = [o.reshape(ada_w.shape) for o in ada_out]

    sm_m = [m_even_sc_conv_w, m_odd_conv_w, m_odd_conv_b, m_odd_ln_g, m_odd_ln_b]
    sm_v = [v_even_sc_conv_w, v_odd_conv_w, v_odd_conv_b, v_odd_ln_g, v_odd_ln_b]
    sm_shapes = [w.shape for w in sm_w]
    state = {"even_w_in": (even_w_in, m_even_w_in, v_even_w_in), "even_w_uq": (even_w_uq, m_even_w_uq, v_even_w_uq),
             "even_w_ukv": (even_w_ukv, m_even_w_ukv, v_even_w_ukv), "even_w_out": (even_w_out, m_even_w_out, v_even_w_out),
             "odd_w_in": (odd_w_in, m_odd_w_in, v_odd_w_in), "odd_w_out": (odd_w_out, m_odd_w_out, v_odd_w_out)}
    state["even_w_in"] = tuple(jnp.swapaxes(t, 1, 2) for t in state["even_w_in"])
    big_res = {name: None for name in state}
    after = [bw_token, grad_x, rep_out[0], ada_out[0]]
    sm_res = None
    for tag, names, handle, own in scatters:
        _, landed = _exchange_wait(handle, True, after, f"scatter_grads_wait_{tag}")
        after = []
        for a, (name, i) in enumerate(names):
            if name == "small":
                sm_out = _adamw([own[a], landed[a]], _pack(sm_w, F32, SUBLANES), _pack(sm_m, F32, SUBLANES),
                                _pack(sm_v, F32, SUBLANES), "adamw_small_sharded")
                sm_res = [_unpack(o.reshape(-1), sm_shapes) for o in sm_out]
                continue
            big_res[name] = _adamw_slab([own[a], landed[a]], *state[name], i, big_res[name], f"adamw_{name}_{i}")
            after += list(big_res[name])
    sh_res = [dict(zip(["even_sc_conv_w", "odd_conv_w", "odd_conv_b", "odd_ln_g", "odd_ln_b"], sm_res[kind]))
              for kind in range(4)]
    for name in state:
        for kind in range(4):
            res = big_res[name][kind]
            sh_res[kind][name] = jnp.swapaxes(res, 1, 2) if name == "even_w_in" else res

    order = ["ada_w", "ada_b", "pre_norm_g", "post_norm_g", "even_w_in", "even_sc_conv_w", "even_sc_conv_b",
             "even_q_norm_g", "even_kv_norm_g", "even_w_uq", "even_w_ukv", "even_w_out", "odd_w_in", "odd_conv_w",
             "odd_conv_b", "odd_ln_g", "odd_ln_b", "odd_w_out"]
    rep_names = ["ada_b", "pre_norm_g", "post_norm_g", "even_sc_conv_b", "even_q_norm_g", "even_kv_norm_g"]
    outs = [loss, grad_x]
    for kind in range(4):
        for name in order:
            if name == "ada_w":
                outs.append(ada_res[kind])
            elif name in rep_names:
                outs.append(rep_res[kind][rep_names.index(name)])
            else:
                outs.append(sh_res[kind][name])
    return tuple(outs)
```

```python
import math

import jax
import jax.numpy as jnp
from jax import lax
from jax.experimental import pallas as pl
from jax.experimental.pallas import tpu as pltpu

F32 = jnp.float32
BF16 = jnp.bfloat16
MESH_AXES = ("x", "y", "c")
N_DEV = 8
EPS = 1e-6
CHUNK = 64
HEADS = 8
QK_NOPE = 64
QK_ROPE = 32
V_HEAD = 64
HEAD_PAD = 128
ROPE_THETA = 10000.0
SC_KERNEL = 3
CONF_KERNEL = 31
LANES = 128
SUBLANES = 8
PACK_COLS = 1024
VMEM_LIMIT = 48 * 1024 * 1024
NEG = -1e30

ADAM_LR = 0.001
ADAM_B1 = 0.9
ADAM_B2 = 0.999
ADAM_EPS = 1e-08
ADAM_WD = 0.01
ADAM_STEP = 10


def _cparams():
    return pltpu.CompilerParams(vmem_limit_bytes=VMEM_LIMIT)


def _sigmoid(x):
    return 1.0 / (1.0 + jnp.exp(-x))


def _f32(ref):
    return ref[...].astype(F32)


def _silu(x):
    return x * _sigmoid(x)


def _dsilu(x):
    s = _sigmoid(x)
    return s * (1.0 + x * (1.0 - s))


def _rows(T, width, cb=0):
    return pl.BlockSpec((T, width), lambda i: (i, cb))


def _const(shape):
    nd = len(shape)
    return pl.BlockSpec(shape, lambda i: (0,) * nd)


def _wide_tile(S):
    return min(512, S)


def _row_tile(S):
    return min(512, S)


def _exchange(srcs, scatter, name):
    n = len(srcs)
    shapes = [tuple(s.shape[1:]) if scatter else tuple(s.shape) for s in srcs]

    def body(*refs):
        src_refs, out_refs = refs[:n], refs[n:2 * n]
        send_sems, recv_sems, local_sems = refs[2 * n:]
        x, y, c = lax.axis_index("x"), lax.axis_index("y"), lax.axis_index("c")
        me = 4 * x + 2 * y + c
        owns, copies = [], []
        for a in range(n):
            def piece(d, a=a):
                return src_refs[a].at[d] if scatter else src_refs[a]

            own = pltpu.make_async_copy(piece(me), out_refs[a].at[me], local_sems.at[a])
            own.start()
            owns.append(own)
            for k in range(1, N_DEV):
                px, py, pc = x ^ ((k >> 2) & 1), y ^ ((k >> 1) & 1), c ^ (k & 1)
                peer = 4 * px + 2 * py + pc
                sem = a * (N_DEV - 1) + k - 1
                cp = pltpu.make_async_remote_copy(
                    src_ref=piece(peer), dst_ref=out_refs[a].at[me],
                    send_sem=send_sems.at[sem], recv_sem=recv_sems.at[sem],
                    device_id=(px, py, pc), device_id_type=pl.DeviceIdType.MESH)
                cp.start()
                arrival = pltpu.make_async_remote_copy(
                    src_ref=piece(peer), dst_ref=out_refs[a].at[peer],
                    send_sem=send_sems.at[sem], recv_sem=recv_sems.at[sem],
                    device_id=(x, y, c), device_id_type=pl.DeviceIdType.MESH)
                copies.append((cp, arrival))
        for _, arrival in copies:
            arrival.wait_recv()
        for cp, _ in copies:
            cp.wait_send()
        for own in owns:
            own.wait()

    return pl.pallas_call(
        body, name=name,
        out_shape=tuple(jax.ShapeDtypeStruct((N_DEV,) + shp, s.dtype) for shp, s in zip(shapes, srcs)),
        in_specs=[pl.BlockSpec(memory_space=pl.ANY)] * n,
        out_specs=tuple(pl.BlockSpec(memory_space=pl.ANY) for _ in range(n)),
        scratch_shapes=[pltpu.SemaphoreType.DMA((n * (N_DEV - 1),)),
                        pltpu.SemaphoreType.DMA((n * (N_DEV - 1),)),
                        pltpu.SemaphoreType.DMA((n,))],
    )(*srcs)


_HBM = pl.BlockSpec(memory_space=pltpu.HBM)
_SEM = pl.BlockSpec(memory_space=pltpu.SEMAPHORE)


def _peer(k):
    x, y, c = lax.axis_index("x"), lax.axis_index("y"), lax.axis_index("c")
    return x ^ ((k >> 2) & 1), y ^ ((k >> 1) & 1), c ^ (k & 1)


def _exchange_start(srcs, scatter, name):
    n = len(srcs)
    shapes = [tuple(s.shape[1:]) if scatter else tuple(s.shape) for s in srcs]
    slots = N_DEV - 1 if scatter else N_DEV
    lands = [lax.empty((slots,) + shp, s.dtype) for shp, s in zip(shapes, srcs)]
    if not scatter:
        here = 4 * lax.axis_index("x") + 2 * lax.axis_index("y") + lax.axis_index("c")
        lands = [lax.dynamic_update_index_in_dim(l, s, here, 0) for l, s in zip(lands, srcs)]

    def body(*refs):
        src_refs, land_refs = refs[:n], refs[n:2 * n]
        send_sems, recv_sems = refs[2 * n], refs[2 * n + 1]
        token = refs[4 * n + 2]
        me = 4 * lax.axis_index("x") + 2 * lax.axis_index("y") + lax.axis_index("c")
        for a in range(n):
            for k in range(1, N_DEV):
                px, py, pc = _peer(k)
                peer = 4 * px + 2 * py + pc
                pltpu.make_async_remote_copy(
                    src_ref=src_refs[a].at[peer] if scatter else src_refs[a],
                    dst_ref=land_refs[a].at[k - 1] if scatter else land_refs[a].at[me],
                    send_sem=send_sems.at[a * (N_DEV - 1) + k - 1], recv_sem=recv_sems.at[a * (N_DEV - 1) + k - 1],
                    device_id=(px, py, pc), device_id_type=pl.DeviceIdType.MESH).start()
        token[...] = jnp.zeros_like(token)

    hbm = lambda arrs: [pltpu.HBM(a.shape, a.dtype) for a in arrs]
    out = pl.pallas_call(
        body, name=name,
        out_shape=(pltpu.SemaphoreType.DMA((n * (N_DEV - 1),)), pltpu.SemaphoreType.DMA((n * (N_DEV - 1),)),
                   *hbm(srcs), *hbm(lands), jax.ShapeDtypeStruct((SUBLANES, LANES), F32)),
        in_specs=[_HBM] * (2 * n),
        out_specs=(_SEM, _SEM, *([_HBM] * (2 * n)), pl.BlockSpec(memory_space=pltpu.VMEM)),
        input_output_aliases={a: 2 + a for a in range(2 * n)},
        compiler_params=pltpu.CompilerParams(has_side_effects=pltpu.SideEffectType.DATAFLOW_SIDE_EFFECTING),
    )(*[pltpu.with_memory_space_constraint(s, pltpu.HBM) for s in srcs],
      *[pltpu.with_memory_space_constraint(l, pltpu.HBM) for l in lands])
    return (out[0], out[1], list(out[2:2 + n]), list(out[2 + n:2 + 2 * n])), out[2 + 2 * n]


def _exchange_wait(handle, scatter, after, name):
    send_sems, recv_sems, srcs, lands = handle
    n = len(srcs)
    after = list(after) if isinstance(after, (list, tuple)) else [after]

    def body(*refs):
        src_refs, land_refs = refs[:n], refs[n:2 * n]
        send_sems, recv_sems = refs[2 * n], refs[2 * n + 1]
        for a in range(n):
            for k in range(1, N_DEV):
                px, py, pc = _peer(k)
                peer = 4 * px + 2 * py + pc
                cp = pltpu.make_async_remote_copy(
                    src_ref=src_refs[a].at[peer] if scatter else src_refs[a],
                    dst_ref=land_refs[a].at[k - 1] if scatter else land_refs[a].at[peer],
                    send_sem=send_sems.at[a * (N_DEV - 1) + k - 1], recv_sem=recv_sems.at[a * (N_DEV - 1) + k - 1],
                    device_id=(px, py, pc), device_id_type=pl.DeviceIdType.MESH)
                cp.wait_send()
                cp.wait_recv()

    out = pl.pallas_call(
        body, name=name,
        out_shape=tuple(pltpu.HBM(a.shape, a.dtype) for a in srcs + lands),
        in_specs=[_HBM] * (2 * n) + [_SEM, _SEM] + [pl.BlockSpec(memory_space=pl.ANY)] * len(after),
        out_specs=tuple([_HBM] * (2 * n)),
        input_output_aliases={a: a for a in range(2 * n)},
        compiler_params=pltpu.CompilerParams(has_side_effects=pltpu.SideEffectType.DATAFLOW_SIDE_EFFECTING),
    )(*srcs, *lands, send_sems, recv_sems, *after)
    return list(out[:n]), list(out[n:])


def _pack(parts, dtype, row_mult):
    flat = jnp.concatenate([p.reshape(-1).astype(dtype) for p in parts])
    n = flat.shape[0]
    rows = -(-n // PACK_COLS)
    rows = -(-rows // row_mult) * row_mult
    flat = jnp.pad(flat, (0, rows * PACK_COLS - n))
    return flat.reshape(rows, PACK_COLS)


def _unpack(flat, shapes):
    out, off = [], 0
    for shp in shapes:
        n = math.prod(shp)
        out.append(flat[..., off:off + n].reshape(flat.shape[:-1] + tuple(shp)))
        off += n
    return out


_DIMS = {"nn": (((1,), (0,)), ((), ())), "nt": (((1,), (1,)), ((), ())), "tn": (((0,), (0,)), ((), ()))}


def _matmul(a, b, mode, out_dtype, name, tm=512, tn=512, tk=None, split_n=False):
    if mode == "nn":
        (M, K), (_, N) = a.shape, b.shape
    elif mode == "nt":
        (M, K), (N, _) = a.shape, b.shape
    else:
        (K, M), (_, N) = a.shape, b.shape
    tm, tn = min(tm, M), min(tn, N)
    tk = K if tk is None else min(tk, K)
    nk = K // tk
    assert M % tm == 0 and N % tn == 0 and K % tk == 0, (name, a.shape, b.shape)

    def body(a_ref, b_ref, o_ref, *scratch):
        p = lax.dot_general(a_ref[...].astype(BF16), b_ref[...].astype(BF16), _DIMS[mode],
                            preferred_element_type=F32)
        if split_n:
            o_ref[0] = p.astype(out_dtype)
        elif nk == 1:
            o_ref[...] = p.astype(out_dtype)
        else:
            acc = scratch[0]
            k = pl.program_id(2)

            @pl.when(k == 0)
            def _():
                acc[...] = p

            @pl.when(k > 0)
            def _():
                acc[...] += p

            @pl.when(k == nk - 1)
            def _():
                o_ref[...] = acc[...].astype(out_dtype)

    a_spec = (pl.BlockSpec((tk, tm), lambda i, j, k: (k, i)) if mode == "tn"
              else pl.BlockSpec((tm, tk), lambda i, j, k: (i, k)))
    b_spec = (pl.BlockSpec((tn, tk), lambda i, j, k: (j, k)) if mode == "nt"
              else pl.BlockSpec((tk, tn), lambda i, j, k: (k, j)))
    return pl.pallas_call(
        body, name=name, grid=(M // tm, N // tn, nk),
        out_shape=jax.ShapeDtypeStruct((N // tn, M, tn) if split_n else (M, N), out_dtype),
        in_specs=[a_spec, b_spec],
        out_specs=(pl.BlockSpec((1, tm, tn), lambda i, j, k: (j, i, 0)) if split_n
                   else pl.BlockSpec((tm, tn), lambda i, j, k: (i, j))),
        scratch_shapes=[pltpu.VMEM((tm, tn), F32)] if nk > 1 else [],
        compiler_params=_cparams(),
    )(a, b)


def _ada_fwd(c_all, ada_w, ada_b_cols):
    L, D, n = ada_w.shape

    def body(c_ref, w_ref, b_ref, o_ref):
        act = _silu(c_ref[...]).astype(BF16)
        o_ref[0] = jnp.dot(act, w_ref[0].astype(BF16), preferred_element_type=F32) + b_ref[0]

    return pl.pallas_call(
        body, name="ada_fwd", grid=(L,),
        out_shape=jax.ShapeDtypeStruct((L, N_DEV, n), F32),
        in_specs=[pl.BlockSpec((N_DEV, D), lambda l: (0, 0)),
                  pl.BlockSpec((1, D, n), lambda l: (l, 0, 0)),
                  pl.BlockSpec((1, 1, n), lambda l: (l, 0, 0))],
        out_specs=pl.BlockSpec((1, N_DEV, n), lambda l: (l, 0, 0)),
        compiler_params=_cparams(),
    )(c_all, ada_w, ada_b_cols)


def _ada_bwd(c_all_t, dmod_cols):
    D = c_all_t.shape[0]
    L, _, n = dmod_cols.shape

    def body(c_ref, d_ref, o_ref):
        act = _silu(c_ref[...])
        dm = d_ref[0]
        acc = act[:, 0:1] * dm[0:1, :]
        for b in range(1, N_DEV):
            acc = acc + act[:, b:b + 1] * dm[b:b + 1, :]
        o_ref[0] = acc

    return pl.pallas_call(
        body, name="ada_bwd", grid=(L,),
        out_shape=jax.ShapeDtypeStruct((L, D, n), F32),
        in_specs=[pl.BlockSpec((D, N_DEV), lambda l: (0, 0)),
                  pl.BlockSpec((1, N_DEV, n), lambda l: (l, 0, 0))],
        out_specs=pl.BlockSpec((1, D, n), lambda l: (l, 0, 0)),
        compiler_params=_cparams(),
    )(c_all_t, dmod_cols)


def _rope_tables(pos_col, inv_lane):
    S = pos_col.shape[0]
    T = _row_tile(S)
    half = QK_ROPE // 2

    def body(p_ref, f_ref, c_ref, up_ref, dn_ref):
        ang = p_ref[...] * f_ref[...]
        lane = lax.broadcasted_iota(jnp.int32, ang.shape, 1)
        first = (lane >= QK_NOPE) & (lane < QK_NOPE + half)
        second = (lane >= QK_NOPE + half) & (lane < QK_NOPE + QK_ROPE)
        cs, sn = jnp.cos(ang), jnp.sin(ang)
        c_ref[...] = jnp.where(first | second, cs, 1.0)
        up_ref[...] = jnp.where(first, -sn, 0.0)
        dn_ref[...] = jnp.where(second, sn, 0.0)

    tab = jax.ShapeDtypeStruct((S, HEAD_PAD), F32)
    return pl.pallas_call(
        body, name="rope_tables", grid=(S // T,),
        out_shape=(tab, tab, tab),
        in_specs=[_rows(T, 1), _const((1, HEAD_PAD))],
        out_specs=(_rows(T, HEAD_PAD),) * 3,
        compiler_params=_cparams(),
    )(pos_col, inv_lane)


def _rope(blk, ct, ut, dt):
    half = QK_ROPE // 2
    up = pltpu.roll(blk, HEAD_PAD - half, 1)
    dn = pltpu.roll(blk, half, 1)
    return blk * ct + up * ut + dn * dt


def _rope_t(d, ct, ut, dt):
    half = QK_ROPE // 2
    return d * ct + pltpu.roll(d * ut, half, 1) + pltpu.roll(d * dt, HEAD_PAD - half, 1)


def _row_operands(rows, after=None):
    ops, specs = [], []
    for a in rows:
        if isinstance(a, tuple):
            table, r = a
            ops.append(table)
            specs.append(pl.BlockSpec((None, 1, table.shape[-1]), lambda *_, r=r: (r, 0, 0)))
        else:
            ops.append(a)
            specs.append(pl.BlockSpec(a.shape, lambda *_, nd=a.ndim: (0,) * nd))
    if after is not None:
        ops.append(after)
        specs.append(pl.BlockSpec(memory_space=pl.ANY))
    return ops, specs


def _pre_norm(x, g, scale, shift, name, after=None):
    S, D = x.shape
    T = _row_tile(S)
    row_ops, row_specs = _row_operands([g, scale, shift], after)

    def body(x_ref, g_ref, sc_ref, sh_ref, *rest):
        h_ref = rest[-1]
        xv = x_ref[...]
        rstd = lax.rsqrt(jnp.mean(xv * xv, axis=-1, keepdims=True) + EPS)
        h_ref[...] = ((xv * rstd) * g_ref[...] * (1.0 + sc_ref[...]) + sh_ref[...]).astype(BF16)

    return pl.pallas_call(
        body, name=name, grid=(S // T,),
        out_shape=jax.ShapeDtypeStruct((S, D), BF16),
        in_specs=[_rows(T, D)] + row_specs,
        out_specs=_rows(T, D), compiler_params=_cparams(),
    )(x, *row_ops)


def _fold8(v):
    T, C = v.shape
    return v.reshape(T // SUBLANES, SUBLANES, C).sum(axis=0)


def _col_sums(n_sums, body_fn, ins, in_specs, outs, out_specs, S, T, widths, name):
    n_in, n_out = len(ins), len(outs)
    nt = S // T

    def body(*refs):
        in_refs = refs[:n_in]
        out_refs = refs[n_in:n_in + n_out]
        sum_refs = refs[n_in + n_out:n_in + n_out + n_sums]
        accs = refs[n_in + n_out + n_sums:]
        i = pl.program_id(0)
        terms = body_fn(in_refs, out_refs)

        @pl.when(i == 0)
        def _():
            for acc, t in zip(accs, terms):
                acc[...] = _fold8(t)

        @pl.when(i > 0)
        def _():
            for acc, t in zip(accs, terms):
                acc[...] += _fold8(t)

        @pl.when(i == nt - 1)
        def _():
            for acc, s_ref in zip(accs, sum_refs):
                s_ref[...] = jnp.sum(acc[...], axis=0, keepdims=True)

    return pl.pallas_call(
        body, name=name, grid=(nt,),
        out_shape=tuple(outs) + tuple(jax.ShapeDtypeStruct((1, w), F32) for w in widths),
        in_specs=in_specs,
        out_specs=tuple(out_specs) + tuple(_const((1, w)) for w in widths),
        scratch_shapes=[pltpu.VMEM((SUBLANES, w), F32) for w in widths],
        compiler_params=_cparams(),
    )(*ins)


def _pre_norm_bwd(dh, x, dxo, g, scale, name, after=None):
    S, D = x.shape
    T = _row_tile(S)
    row_ops, row_specs = _row_operands([g, scale], after)

    def fn(ins, outs):
        dh_ref, x_ref, dxo_ref, g_ref, sc_ref = ins[:5]
        xv, dv = x_ref[...], _f32(dh_ref)
        rstd = lax.rsqrt(jnp.mean(xv * xv, axis=-1, keepdims=True) + EPS)
        xh = xv * rstd
        dr = dv * (1.0 + sc_ref[...])
        dxh = dr * g_ref[...]
        outs[0][...] = dxo_ref[...] + rstd * (dxh - xh * jnp.mean(dxh * xh, axis=-1, keepdims=True))
        return [dv, dv * (xh * g_ref[...]), dr * xh]

    return _col_sums(3, fn, [dh, x, dxo] + row_ops,
                     [_rows(T, D), _rows(T, D), _rows(T, D)] + row_specs,
                     [jax.ShapeDtypeStruct((S, D), F32)], [_rows(T, D)], S, T, [D, D, D], name)


def _post_pre_norm(x, y, g_post, gate, g_pre, scale, shift, name):
    S, D = x.shape
    T = _row_tile(S)
    row_ops, row_specs = _row_operands([g_post, gate, g_pre, scale, shift])

    def body(x_ref, y_ref, gp_ref, gt_ref, g_ref, sc_ref, sh_ref, xn_ref, h_ref):
        yv = _f32(y_ref)
        rstd_y = lax.rsqrt(jnp.mean(yv * yv, axis=-1, keepdims=True) + EPS)
        xn = x_ref[...] + gt_ref[...] * ((yv * rstd_y) * gp_ref[...])
        xn_ref[...] = xn
        rstd = lax.rsqrt(jnp.mean(xn * xn, axis=-1, keepdims=True) + EPS)
        h_ref[...] = ((xn * rstd) * g_ref[...] * (1.0 + sc_ref[...]) + sh_ref[...]).astype(BF16)

    return pl.pallas_call(
        body, name=name, grid=(S // T,),
        out_shape=(jax.ShapeDtypeStruct((S, D), F32), jax.ShapeDtypeStruct((S, D), BF16)),
        in_specs=[_rows(T, D), _rows(T, D)] + row_specs,
        out_specs=(_rows(T, D), _rows(T, D)), compiler_params=_cparams(),
    )(x, y, *row_ops)


def _pre_post_norm_bwd(dh, x, dxo, g_pre, scale, y_prev, g_post_prev, gate_prev, name, after=None):
    S, D = x.shape
    T = _row_tile(S)
    row_ops, row_specs = _row_operands([g_pre, scale, g_post_prev, gate_prev], after)

    def fn(ins, outs):
        dh_ref, x_ref, dxo_ref, y_ref, g_ref, sc_ref, gp_ref, gt_ref = ins[:8]
        xv, dv = x_ref[...], _f32(dh_ref)
        rstd = lax.rsqrt(jnp.mean(xv * xv, axis=-1, keepdims=True) + EPS)
        xh = xv * rstd
        dr = dv * (1.0 + sc_ref[...])
        dxh = dr * g_ref[...]
        dx = dxo_ref[...] + rstd * (dxh - xh * jnp.mean(dxh * xh, axis=-1, keepdims=True))
        outs[0][...] = dx
        yv = _f32(y_ref)
        rstd_y = lax.rsqrt(jnp.mean(yv * yv, axis=-1, keepdims=True) + EPS)
        yh = yv * rstd_y
        dn = dx * gt_ref[...]
        dyh = dn * gp_ref[...]
        outs[1][...] = (rstd_y * (dyh - yh * jnp.mean(dyh * yh, axis=-1, keepdims=True))).astype(BF16)
        return [dv, dv * (xh * g_ref[...]), dr * xh, dx * (yh * gp_ref[...]), dn * yh]

    return _col_sums(5, fn, [dh, x, dxo, y_prev] + row_ops,
                     [_rows(T, D), _rows(T, D), _rows(T, D), _rows(T, D)] + row_specs,
                     [jax.ShapeDtypeStruct((S, D), F32), jax.ShapeDtypeStruct((S, D), BF16)],
                     [_rows(T, D), _rows(T, D)], S, T, [D] * 5, name)


def _loss_post_norm_bwd(x, y, g_post, gate, target, name):
    S, D = x.shape
    T = _row_tile(S)
    row_ops, row_specs = _row_operands([g_post, gate])

    def fn(ins, outs):
        x_ref, y_ref, t_ref, gp_ref, gt_ref = ins
        yv = _f32(y_ref)
        rstd_y = lax.rsqrt(jnp.mean(yv * yv, axis=-1, keepdims=True) + EPS)
        yh = yv * rstd_y
        e = x_ref[...] + gt_ref[...] * (yh * gp_ref[...]) - t_ref[...]
        dx = e * (1.0 / D)
        outs[0][...] = dx
        dn = dx * gt_ref[...]
        dyh = dn * gp_ref[...]
        outs[1][...] = (rstd_y * (dyh - yh * jnp.mean(dyh * yh, axis=-1, keepdims=True))).astype(BF16)
        return [e * e, dx * (yh * gp_ref[...]), dn * yh]

    return _col_sums(3, fn, [x, y, target] + row_ops,
                     [_rows(T, D), _rows(T, D), _rows(T, D)] + row_specs,
                     [jax.ShapeDtypeStruct((S, D), F32), jax.ShapeDtypeStruct((S, D), BF16)],
                     [_rows(T, D), _rows(T, D)], S, T, [D] * 3, name)


def _scaled_total(v, coef, name):
    def body(v_ref, o_ref):
        o_ref[...] = jnp.broadcast_to(jnp.sum(v_ref[...], axis=1, keepdims=True) * coef, (1, LANES))

    return pl.pallas_call(body, name=name, out_shape=jax.ShapeDtypeStruct((1, LANES), F32))(v)


CONV_ROWS = 64
CONV_TILE = 2048


def _conv_halo(K):
    return SUBLANES if K - 1 <= SUBLANES else 32


def _conv_fwd(u, w, b, K, name):
    S, C = u.shape
    KP = w.shape[0]
    T, HB, RS = min(CONV_TILE, S), _conv_halo(K), CONV_ROWS
    ratio = T // HB

    def body(u_ref, h_ref, w_ref, b_ref, o_ref, ext):
        i = pl.program_id(1)
        ext[0:HB, :] = jnp.where(i > 0, h_ref[...], 0.0)
        ext[HB:HB + T, :] = u_ref[...]
        for r0 in range(0, T, RS):
            acc = jnp.broadcast_to(b_ref[...], (RS, LANES))
            for k in range(K):
                off = HB - (K - 1) + k + r0
                acc = acc + w_ref[k:k + 1, :] * ext[off:off + RS, :]
            o_ref[r0:r0 + RS, :] = acc

    return pl.pallas_call(
        body, name=name, grid=(C // LANES, S // T),
        out_shape=jax.ShapeDtypeStruct((S, C), F32),
        in_specs=[pl.BlockSpec((T, LANES), lambda c, i: (i, c)),
                  pl.BlockSpec((HB, LANES), lambda c, i: (jnp.maximum(i * ratio - 1, 0), c)),
                  pl.BlockSpec((KP, LANES), lambda c, i: (0, c)),
                  pl.BlockSpec((1, LANES), lambda c, i: (0, c))],
        out_specs=pl.BlockSpec((T, LANES), lambda c, i: (i, c)),
        scratch_shapes=[pltpu.VMEM((HB + T, LANES), F32)],
        compiler_params=_cparams(),
    )(u, u, w, b)


def _conv_bwd(d, u, w, K, name):
    S, C = u.shape
    KP = w.shape[0]
    T, HB, RS = min(CONV_TILE, S), _conv_halo(K), CONV_ROWS
    ratio = T // HB
    nt = S // T
    last_halo = S // HB - 1

    def body(d_ref, dn_ref, u_ref, up_ref, w_ref, du_ref, dw_ref, db_ref, extd, extu, dws, dbs):
        i = pl.program_id(1)
        extd[0:T, :] = d_ref[...]
        extd[T:T + HB, :] = jnp.where(i < nt - 1, dn_ref[...], 0.0)
        extu[0:HB, :] = jnp.where(i > 0, up_ref[...], 0.0)
        extu[HB:HB + T, :] = u_ref[...]

        @pl.when(i == 0)
        def _():
            dws[...] = jnp.zeros_like(dws)
            dbs[...] = jnp.zeros_like(dbs)

        for r0 in range(0, T, RS):
            acc = jnp.zeros((RS, LANES), F32)
            for k in range(K):
                off = (K - 1 - k) + r0
                acc = acc + w_ref[k:k + 1, :] * extd[off:off + RS, :]
            du_ref[r0:r0 + RS, :] = acc
            dch = d_ref[r0:r0 + RS, :]
            dbs[...] += _fold8(dch)
            for k in range(K):
                off = HB - (K - 1) + k + r0
                dws[k * SUBLANES:(k + 1) * SUBLANES, :] += _fold8(dch * extu[off:off + RS, :])

        @pl.when(i == nt - 1)
        def _():
            dw_ref[...] = jnp.zeros_like(dw_ref)
            for k in range(K):
                dw_ref[k:k + 1, :] = jnp.sum(dws[k * SUBLANES:(k + 1) * SUBLANES, :], axis=0, keepdims=True)
            db_ref[...] = jnp.sum(dbs[...], axis=0, keepdims=True)

    return pl.pallas_call(
        body, name=name, grid=(C // LANES, nt),
        out_shape=(jax.ShapeDtypeStruct((S, C), F32), jax.ShapeDtypeStruct((KP, C), F32),
                   jax.ShapeDtypeStruct((1, C), F32)),
        in_specs=[pl.BlockSpec((T, LANES), lambda c, i: (i, c)),
                  pl.BlockSpec((HB, LANES), lambda c, i: (jnp.minimum((i + 1) * ratio, last_halo), c)),
                  pl.BlockSpec((T, LANES), lambda c, i: (i, c)),
                  pl.BlockSpec((HB, LANES), lambda c, i: (jnp.maximum(i * ratio - 1, 0), c)),
                  pl.BlockSpec((KP, LANES), lambda c, i: (0, c))],
        out_specs=(pl.BlockSpec((T, LANES), lambda c, i: (i, c)),
                   pl.BlockSpec((KP, LANES), lambda c, i: (0, c)),
                   pl.BlockSpec((1, LANES), lambda c, i: (0, c))),
        scratch_shapes=[pltpu.VMEM((T + HB, LANES), F32), pltpu.VMEM((HB + T, LANES), F32),
                        pltpu.VMEM((KP * SUBLANES, LANES), F32), pltpu.VMEM((SUBLANES, LANES), F32)],
        compiler_params=_cparams(),
    )(d, d, u, u, w)


SCW = 512
ZE = 3072
QL = 256
KVL = 128


def _rms_rows(x, g):
    rstd = lax.rsqrt(jnp.mean(x * x, axis=-1, keepdims=True) + EPS)
    return (x * rstd) * g


def _attn_tile(S):
    return min(256, S)


_NT = (((1,), (1,)), ((), ()))


LOG2E = math.log2(math.e)
ATTN_FWD_HEADS = 8
ATTN_BWD_HEADS = 4


def _chunk_mask_t(T):
    key = lax.broadcasted_iota(jnp.int32, (T, T), 0) // CHUNK
    qry = lax.broadcasted_iota(jnp.int32, (T, T), 1) // CHUNK
    return key <= qry


W_IN_ROWS = 512


def _w_in_even(h, w_t, qg, kvg, name):
    S, D = h.shape
    tm = min(W_IN_ROWS, S)

    def body(h_ref, w_ref, qg_ref, kvg_ref, z_ref, u_ref, qn_ref, kvn_ref):
        p = lax.dot_general(h_ref[...], w_ref[...], _NT, preferred_element_type=F32)
        z_ref[...] = p.astype(BF16)
        u_ref[...] = p[:, SCW:2 * SCW] * p[:, 2 * SCW:3 * SCW]
        qn_ref[...] = _rms_rows(p[:, 5 * SCW:5 * SCW + QL], qg_ref[...]).astype(BF16)
        kvn_ref[...] = _rms_rows(p[:, 5 * SCW + QL:5 * SCW + QL + KVL], kvg_ref[...]).astype(BF16)

    return pl.pallas_call(
        body, name=name, grid=(S // tm,),
        out_shape=(jax.ShapeDtypeStruct((S, ZE), BF16), jax.ShapeDtypeStruct((S, SCW), F32),
                   jax.ShapeDtypeStruct((S, QL), BF16), jax.ShapeDtypeStruct((S, KVL), BF16)),
        in_specs=[_rows(tm, D), _const(w_t.shape), _const((1, QL)), _const((1, KVL))],
        out_specs=(_rows(tm, ZE), _rows(tm, SCW), _rows(tm, QL), _rows(tm, KVL)),
        compiler_params=_cparams(),
    )(h, w_t, qg, kvg)


def _w_in_odd(h, w, name):
    S, D = h.shape
    tm = min(W_IN_ROWS, S)

    def body(h_ref, w_ref, z_ref, u_ref):
        p = jnp.dot(h_ref[...], w_ref[...], preferred_element_type=F32)
        z_ref[...] = p.astype(BF16)
        u_ref[...] = p[:, 0:D] * _sigmoid(p[:, D:2 * D])

    return pl.pallas_call(
        body, name=name, grid=(S // tm,),
        out_shape=(jax.ShapeDtypeStruct((S, 3 * D), BF16), jax.ShapeDtypeStruct((S, D), F32)),
        in_specs=[_rows(tm, D), _const(w.shape)],
        out_specs=(_rows(tm, 3 * D), _rows(tm, D)),
        compiler_params=_cparams(),
    )(h, w)


def _qkv_fwd_t(qn, kvn, z, tabs, w_q, w_kv, name):
    S = qn.shape[0]
    T = _attn_tile(S)
    HW = HEADS * HEAD_PAD
    scale = LOG2E / math.sqrt(QK_NOPE + QK_ROPE)

    def body(qn_ref, kvn_ref, kr_ref, ct_ref, ut_ref, dt_ref, wq_ref, wkv_ref, q_ref, k_ref, v_ref, kt_ref, vt_ref):
        ct, ut, dt = ct_ref[...], ut_ref[...], dt_ref[...]
        qa = jnp.dot(qn_ref[...], wq_ref[...], preferred_element_type=F32)
        kva = jnp.dot(kvn_ref[...], wkv_ref[...], preferred_element_type=F32)
        kr = _f32(kr_ref)
        ones_row = (lax.broadcasted_iota(jnp.int32, (V_HEAD, T), 0) == 0).astype(F32)
        for h in range(HEADS):
            sl = slice(h * HEAD_PAD, (h + 1) * HEAD_PAD)
            q_ref[:, sl] = (_rope(qa[:, sl], ct, ut, dt) * scale).astype(BF16)
            kh = _rope(kva[:, sl] + kr, ct, ut, dt)
            k_ref[:, sl] = kh.astype(BF16)
            kt_ref[0, sl, :] = kh.T.astype(BF16)
        v_ref[...] = kva[:, HW:].astype(BF16)
        for p in range(HEADS // 2):
            vpt = kva[:, HW + p * LANES:HW + (p + 1) * LANES].T
            for h in range(2):
                r0 = (2 * p + h) * HEAD_PAD
                vt_ref[0, r0:r0 + V_HEAD, :] = vpt[h * V_HEAD:(h + 1) * V_HEAD, :].astype(BF16)
                vt_ref[0, r0 + V_HEAD:r0 + HEAD_PAD, :] = ones_row.astype(BF16)

    t3 = jax.ShapeDtypeStruct((S // T, HW, T), BF16)
    return pl.pallas_call(
        body, name=name, grid=(S // T,),
        out_shape=(jax.ShapeDtypeStruct((S, HW), BF16), jax.ShapeDtypeStruct((S, HW), BF16),
                   jax.ShapeDtypeStruct((S, HEADS * V_HEAD), BF16), t3, t3),
        in_specs=[_rows(T, QL), _rows(T, KVL), _rows(T, HEAD_PAD, 23),
                  _rows(T, HEAD_PAD), _rows(T, HEAD_PAD), _rows(T, HEAD_PAD),
                  _const(w_q.shape), _const(w_kv.shape)],
        out_specs=(_rows(T, HW), _rows(T, HW), _rows(T, HEADS * V_HEAD),
                   pl.BlockSpec((1, HW, T), lambda i: (i, 0, 0)), pl.BlockSpec((1, HW, T), lambda i: (i, 0, 0))),
        compiler_params=_cparams(),
    )(qn, kvn, z, *tabs, w_q, w_kv)


def _attn_fwd_t(q, k, vT3, name):
    S = q.shape[0]
    T = _attn_tile(S)
    nq = S // T
    NH = ATTN_FWD_HEADS
    PW = NH * HEAD_PAD

    def body(q_ref, k_ref, vt_ref, o_ref, lse_ref, m_s, acc_s):
        i = pl.program_id(1)
        m_s[...] = jnp.full_like(m_s, NEG)
        acc_s[...] = jnp.zeros_like(acc_s)
        qv = q_ref[...]

        def step(j, masked):
            kb = k_ref[pl.ds(pl.multiple_of(j * T, T), T), :]
            vt = vt_ref[j]
            heads = [slice(h * HEAD_PAD, (h + 1) * HEAD_PAD) for h in range(NH)]
            sts = [lax.dot_general(kb[:, sl], qv[:, sl], _NT, preferred_element_type=F32) for sl in heads]
            alphas, pvs = [], []
            for h, sl in enumerate(heads):
                st = jnp.where(_chunk_mask_t(T), sts[h], NEG) if masked else sts[h]
                m_prev = m_s[h]
                m_new = jnp.maximum(m_prev, jnp.max(st, axis=0, keepdims=True))
                alphas.append(jnp.exp2(m_prev[0:1] - m_new[0:1]))
                pt = jnp.exp2(st - m_new[0:1]).astype(BF16)
                m_s[h] = m_new
                pvs.append(jnp.dot(vt[sl, :], pt, preferred_element_type=F32))
            for h in range(NH):
                acc_s[h] = acc_s[h] * alphas[h] + pvs[h]

        def loop_body(j, carry):
            step(j, False)
            return carry

        lax.fori_loop(0, i, loop_body, 0)
        step(i, True)
        for g in range(NH // 2):
            outs = []
            for h in (2 * g, 2 * g + 1):
                acc = acc_s[h]
                l_row = acc[V_HEAD:V_HEAD + 1, :]
                outs.append(acc[0:V_HEAD, :] / l_row)
                lse_ref[0, h * SUBLANES:(h + 1) * SUBLANES, :] = m_s[h] + jnp.log2(l_row)
            o_ref[:, g * LANES:(g + 1) * LANES] = jnp.concatenate(outs, axis=0).T

    return pl.pallas_call(
        body, name=name, grid=(HEADS // NH, nq),
        out_shape=(jax.ShapeDtypeStruct((S, HEADS * V_HEAD), F32),
                   jax.ShapeDtypeStruct((nq, HEADS * SUBLANES, T), F32)),
        in_specs=[pl.BlockSpec((T, PW), lambda p, i: (i, p)),
                  pl.BlockSpec((S, PW), lambda p, i: (0, p)),
                  pl.BlockSpec((nq, PW, T), lambda p, i: (0, p, 0))],
        out_specs=(pl.BlockSpec((T, NH * V_HEAD), lambda p, i: (i, p)),
                   pl.BlockSpec((1, NH * SUBLANES, T), lambda p, i: (i, p, 0))),
        scratch_shapes=[pltpu.VMEM((NH, SUBLANES, T), F32), pltpu.VMEM((NH, HEAD_PAD, T), F32)],
        compiler_params=_cparams(),
    )(q, k, vT3)


def _attn_bwd_t(q, k, v, kT3, do, lse3, dl3, name):
    S = q.shape[0]
    T = _attn_tile(S)
    nq = S // T
    NH = ATTN_BWD_HEADS
    PW = NH * HEAD_PAD
    VW = NH * V_HEAD

    def body(q_ref, k_ref, v_ref, kt_ref, do_ref, lse_ref, dl_ref, dq_ref, dk_ref, dv_ref, dk_s, dv_s):
        j = pl.program_id(1)
        left = lax.broadcasted_iota(jnp.int32, (T, LANES), 1) < V_HEAD

        @pl.when(j == 0)
        def _():
            dq_ref[...] = jnp.zeros_like(dq_ref)

        dk_s[...] = jnp.zeros_like(dk_s)
        dv_s[...] = jnp.zeros_like(dv_s)
        kb = k_ref[...]
        vms = []
        for g in range(NH // 2):
            vb = v_ref[:, g * LANES:(g + 1) * LANES]
            vms += [jnp.where(left, vb, jnp.zeros_like(vb)), jnp.where(left, jnp.zeros_like(vb), vb)]
        kt = kt_ref[0]

        def step(i, masked):
            r0 = pl.multiple_of(i * T, T)
            qb = q_ref[pl.ds(r0, T), :]
            do_all = do_ref[pl.ds(r0, T), :]
            lse = lse_ref[i]
            dl = dl_ref[i]
            heads = [slice(h * HEAD_PAD, (h + 1) * HEAD_PAD) for h in range(NH)]
            dobs = [do_all[:, (h // 2) * LANES:(h // 2 + 1) * LANES] for h in range(NH)]
            sts = [lax.dot_general(kb[:, sl], qb[:, sl], _NT, preferred_element_type=F32) for sl in heads]
            dpts = [lax.dot_general(vms[h], dobs[h], _NT, preferred_element_type=F32) for h in range(NH)]
            res = []
            for h, sl in enumerate(heads):
                r8 = h * SUBLANES
                pt = jnp.exp2(sts[h] - lse[r8:r8 + 1, :])
                if masked:
                    pt = jnp.where(_chunk_mask_t(T), pt, 0.0)
                dst = (pt * (dpts[h] - dl[r8:r8 + 1, :])).astype(BF16)
                res.append((jnp.dot(pt.astype(BF16), dobs[h], preferred_element_type=F32),
                            jnp.dot(dst, qb[:, sl], preferred_element_type=F32),
                            jnp.dot(kt[sl, :], dst, preferred_element_type=F32)))
            for h, sl in enumerate(heads):
                dv_s[h] += res[h][0]
                dk_s[:, sl] += res[h][1]
                dq_ref[i, sl, :] += res[h][2]

        def loop_body(i, carry):
            step(i, False)
            return carry

        step(j, True)
        lax.fori_loop(j + 1, nq, loop_body, 0)
        dk_ref[...] = (dk_s[...] * (1.0 / LOG2E)).astype(BF16)
        for g in range(NH // 2):
            dv_ref[:, g * LANES:(g + 1) * LANES] = jnp.where(left, dv_s[2 * g], dv_s[2 * g + 1]).astype(BF16)

    return pl.pallas_call(
        body, name=name, grid=(HEADS // NH, nq),
        out_shape=(jax.ShapeDtypeStruct((nq, HEADS * HEAD_PAD, T), F32),
                   jax.ShapeDtypeStruct((S, HEADS * HEAD_PAD), BF16), jax.ShapeDtypeStruct((S, HEADS * V_HEAD), BF16)),
        in_specs=[pl.BlockSpec((S, PW), lambda p, j: (0, p)),
                  pl.BlockSpec((T, PW), lambda p, j: (j, p)),
                  pl.BlockSpec((T, VW), lambda p, j: (j, p)),
                  pl.BlockSpec((1, PW, T), lambda p, j: (j, p, 0)),
                  pl.BlockSpec((S, VW), lambda p, j: (0, p)),
                  pl.BlockSpec((nq, NH * SUBLANES, T), lambda p, j: (0, p, 0)),
                  pl.BlockSpec((nq, NH * SUBLANES, T), lambda p, j: (0, p, 0))],
        out_specs=(pl.BlockSpec((nq, PW, T), lambda p, j: (0, p, 0)),
                   pl.BlockSpec((T, PW), lambda p, j: (j, p)),
                   pl.BlockSpec((T, VW), lambda p, j: (j, p))),
        scratch_shapes=[pltpu.VMEM((T, PW), F32), pltpu.VMEM((NH, T, LANES), F32)],
        compiler_params=_cparams(),
    )(q, k, v, kT3, do, lse3, dl3)


def _even_post(z, cv, o, name):
    S = z.shape[0]
    T = _wide_tile(S)

    def body(ab_ref, ag_ref, bg_ref, cv_ref, o_ref, y_ref):
        y_ref[:, 0:SCW] = (_f32(ab_ref) * cv_ref[...] * _silu(_f32(ag_ref))).astype(BF16)
        y_ref[:, SCW:2 * SCW] = (o_ref[...] * _silu(_f32(bg_ref))).astype(BF16)

    return pl.pallas_call(
        body, name=name, grid=(S // T,),
        out_shape=jax.ShapeDtypeStruct((S, 2 * SCW), BF16),
        in_specs=[_rows(T, SCW, 0), _rows(T, SCW, 3), _rows(T, SCW, 4), _rows(T, SCW), _rows(T, SCW)],
        out_specs=_rows(T, 2 * SCW), compiler_params=_cparams(),
    )(z, z, z, cv, o)


def _even_bwd_gates(dyc, z, cv, o, name):
    S = z.shape[0]
    T = _attn_tile(S)

    def body(dya_ref, dyb_ref, ab_ref, ag_ref, bg_ref, cv_ref, o_ref,
             dab_ref, dag_ref, dbg_ref, dcv_ref, do_ref, dl_ref):
        dya, ab, ag, cv = _f32(dya_ref), _f32(ab_ref), _f32(ag_ref), cv_ref[...]
        sg = _silu(ag)
        dab_ref[...] = (dya * cv * sg).astype(BF16)
        dcv_ref[...] = dya * ab * sg
        dag_ref[...] = (dya * ab * cv * _dsilu(ag)).astype(BF16)
        dyb, bg, ov = _f32(dyb_ref), _f32(bg_ref), o_ref[...]
        dov = dyb * _silu(bg)
        do_ref[...] = dov.astype(BF16)
        dbg_ref[...] = (dyb * ov * _dsilu(bg)).astype(BF16)
        prod = dov * ov
        left = lax.broadcasted_iota(jnp.int32, (T, LANES), 1) < V_HEAD
        for p in range(HEADS // 2):
            blk = prod[:, p * LANES:(p + 1) * LANES]
            s0 = jnp.sum(jnp.where(left, blk, 0.0), axis=1, keepdims=True)
            s1 = jnp.sum(jnp.where(left, 0.0, blk), axis=1, keepdims=True)
            dt = jnp.where(left, s0, s1).T
            dl_ref[0, 2 * p * SUBLANES:(2 * p + 1) * SUBLANES, :] = dt[0:SUBLANES, :]
            dl_ref[0, (2 * p + 1) * SUBLANES:(2 * p + 2) * SUBLANES, :] = dt[V_HEAD:V_HEAD + SUBLANES, :]

    assert T == _attn_tile(S)
    bf = jax.ShapeDtypeStruct((S, SCW), BF16)
    ff = jax.ShapeDtypeStruct((S, SCW), F32)
    return pl.pallas_call(
        body, name=name, grid=(S // T,),
        out_shape=(bf, bf, bf, ff, bf, jax.ShapeDtypeStruct((S // T, HEADS * SUBLANES, T), F32)),
        in_specs=[_rows(T, SCW, 0), _rows(T, SCW, 1), _rows(T, SCW, 0), _rows(T, SCW, 3), _rows(T, SCW, 4),
                  _rows(T, SCW), _rows(T, SCW)],
        out_specs=(_rows(T, SCW),) * 5 + (pl.BlockSpec((1, HEADS * SUBLANES, T), lambda i: (i, 0, 0)),),
        compiler_params=_cparams(),
    )(dyc, dyc, z, z, z, cv, o)


def _qkv_bwd(dq, dk, dv, z, tabs, w_q, w_kv, qg, kvg, name):
    S = dk.shape[0]
    T = _attn_tile(S)
    HW = HEADS * HEAD_PAD
    VW = HEADS * V_HEAD
    scale = 1.0 / math.sqrt(QK_NOPE + QK_ROPE)

    def fn(ins, outs):
        dq_ref, dk_ref, dv_ref, cq_ref, ckv_ref, ct_ref, ut_ref, dt_ref, wq_ref, wkv_ref, qg_ref, kvg_ref = ins
        dqp_ref, dkvp_ref, dcq_ref, dckv_ref, dkr_ref = outs
        ct, ut, dt = ct_ref[...], ut_ref[...], dt_ref[...]
        dkr = jnp.zeros((T, HEAD_PAD), F32)
        for h in range(HEADS):
            sl = slice(h * HEAD_PAD, (h + 1) * HEAD_PAD)
            dqp_ref[:, sl] = (_rope_t(dq_ref[0, sl, :].T, ct, ut, dt) * scale).astype(BF16)
            dkh = _rope_t(dk_ref[:, sl].astype(F32), ct, ut, dt)
            dkr = dkr + dkh
            dkvp_ref[:, sl] = dkh.astype(BF16)
        dkvp_ref[:, HW:] = dv_ref[...].astype(BF16)
        dkr_ref[...] = dkr.astype(BF16)
        sums = []
        for lat_ref, g_ref, dpre_ref, w_ref, dlat_ref in ((cq_ref, qg_ref, dqp_ref, wq_ref, dcq_ref),
                                                         (ckv_ref, kvg_ref, dkvp_ref, wkv_ref, dckv_ref)):
            dn = lax.dot_general(dpre_ref[...], w_ref[...], _NT, preferred_element_type=F32)
            xv = _f32(lat_ref)
            rstd = lax.rsqrt(jnp.mean(xv * xv, axis=-1, keepdims=True) + EPS)
            xh = xv * rstd
            dxh = dn * g_ref[...]
            dlat_ref[...] = (rstd * (dxh - xh * jnp.mean(dxh * xh, axis=-1, keepdims=True))).astype(BF16)
            sums.append(dn * xh)
        return sums

    return _col_sums(
        2, fn, [dq, dk, dv, z, z, *tabs, w_q, w_kv, qg, kvg],
        [pl.BlockSpec((1, HW, T), lambda i: (i, 0, 0)), _rows(T, HW), _rows(T, VW), _rows(T, QL, 10), _rows(T, KVL, 22),
         _rows(T, HEAD_PAD), _rows(T, HEAD_PAD), _rows(T, HEAD_PAD),
         _const(w_q.shape), _const(w_kv.shape), _const((1, QL)), _const((1, KVL))],
        [jax.ShapeDtypeStruct((S, HW), BF16), jax.ShapeDtypeStruct((S, HW + VW), BF16),
         jax.ShapeDtypeStruct((S, QL), BF16), jax.ShapeDtypeStruct((S, KVL), BF16),
         jax.ShapeDtypeStruct((S, HEAD_PAD), BF16)],
        [_rows(T, HW), _rows(T, HW + VW), _rows(T, QL), _rows(T, KVL), _rows(T, HEAD_PAD)],
        S, T, [QL, KVL], name)


def _even_dz(dab, du, z, dag, dbg, dcq, dckv, dkr, name):
    S = z.shape[0]
    T = _wide_tile(S)

    def body(dab_ref, du_ref, ac_ref, ax_ref, dag_ref, dbg_ref, dcq_ref, dckv_ref, dkr_ref, dz_ref):
        duv = du_ref[...]
        dz_ref[:, 0:SCW] = dab_ref[...]
        dz_ref[:, SCW:2 * SCW] = (duv * _f32(ax_ref)).astype(BF16)
        dz_ref[:, 2 * SCW:3 * SCW] = (duv * _f32(ac_ref)).astype(BF16)
        dz_ref[:, 3 * SCW:4 * SCW] = dag_ref[...]
        dz_ref[:, 4 * SCW:5 * SCW] = dbg_ref[...]
        dz_ref[:, 5 * SCW:5 * SCW + QL] = dcq_ref[...]
        dz_ref[:, 5 * SCW + QL:5 * SCW + QL + KVL] = dckv_ref[...]
        dz_ref[:, 5 * SCW + QL + KVL:ZE] = dkr_ref[...]

    return pl.pallas_call(
        body, name=name, grid=(S // T,),
        out_shape=jax.ShapeDtypeStruct((S, ZE), BF16),
        in_specs=[_rows(T, SCW), _rows(T, SCW), _rows(T, SCW, 1), _rows(T, SCW, 2), _rows(T, SCW), _rows(T, SCW),
                  _rows(T, QL), _rows(T, KVL), _rows(T, HEAD_PAD)],
        out_specs=_rows(T, ZE), compiler_params=_cparams(),
    )(dab, du, z, z, dag, dbg, dcq, dckv, dkr)


def _layer_norm_stats(cv):
    mu = jnp.mean(cv, axis=-1, keepdims=True)
    cen = cv - mu
    rstd = lax.rsqrt(jnp.mean(cen * cen, axis=-1, keepdims=True) + EPS)
    return cen * rstd, rstd


def _odd_post(cv, z, ln_g, ln_b, name):
    S, D = cv.shape
    T = _wide_tile(S)

    def body(cv_ref, sg_ref, g_ref, b_ref, y_ref):
        cvh, _ = _layer_norm_stats(cv_ref[...])
        y_ref[...] = (_silu(cvh * g_ref[...] + b_ref[...]) * _silu(_f32(sg_ref))).astype(BF16)

    return pl.pallas_call(
        body, name=name, grid=(S // T,),
        out_shape=jax.ShapeDtypeStruct((S, D), BF16),
        in_specs=[_rows(T, D), _rows(T, D, 2), _const((1, D)), _const((1, D))],
        out_specs=_rows(T, D), compiler_params=_cparams(),
    )(cv, z, ln_g, ln_b)


def _odd_bwd_norm(dyi, cv, z, ln_g, ln_b, name):
    S, D = cv.shape
    T = _row_tile(S)

    def fn(ins, outs):
        dy_ref, cv_ref, sg_ref, g_ref, b_ref = ins
        dcv_ref, dsg_ref = outs
        cvh, rstd = _layer_norm_stats(cv_ref[...])
        ln = cvh * g_ref[...] + b_ref[...]
        sgv, dy = _f32(sg_ref), _f32(dy_ref)
        dsg_ref[...] = (dy * _silu(ln) * _dsilu(sgv)).astype(BF16)
        dln = dy * _silu(sgv) * _dsilu(ln)
        dh = dln * g_ref[...]
        dcv_ref[...] = rstd * (dh - jnp.mean(dh, axis=-1, keepdims=True)
                               - cvh * jnp.mean(dh * cvh, axis=-1, keepdims=True))
        return [dln * cvh, dln]

    return _col_sums(2, fn, [dyi, cv, z, ln_g, ln_b],
                     [_rows(T, D), _rows(T, D), _rows(T, D, 2), _const((1, D)), _const((1, D))],
                     [jax.ShapeDtypeStruct((S, D), F32), jax.ShapeDtypeStruct((S, D), BF16)],
                     [_rows(T, D), _rows(T, D)], S, T, [D, D], name)


def _odd_dz(du, z, dsg, name):
    S, D = du.shape
    T = _wide_tile(S)

    def body(du_ref, val_ref, glu_ref, dsg_ref, dz_ref):
        duv = du_ref[...]
        sig = _sigmoid(_f32(glu_ref))
        dz_ref[:, 0:D] = (duv * sig).astype(BF16)
        dz_ref[:, D:2 * D] = (duv * _f32(val_ref) * sig * (1.0 - sig)).astype(BF16)
        dz_ref[:, 2 * D:3 * D] = dsg_ref[...]

    return pl.pallas_call(
        body, name=name, grid=(S // T,),
        out_shape=jax.ShapeDtypeStruct((S, 3 * D), BF16),
        in_specs=[_rows(T, D), _rows(T, D, 0), _rows(T, D, 1), _rows(T, D)],
        out_specs=_rows(T, 3 * D), compiler_params=_cparams(),
    )(du, z, z, dsg)


ADAM_BLOCK_ELEMS = 128 * 1024


def _adam_tiles(R, C):
    if R * C <= ADAM_BLOCK_ELEMS:
        return R, C
    tr = R
    for cand in range(SUBLANES, R, SUBLANES):
        if R % cand == 0 and cand * C <= ADAM_BLOCK_ELEMS:
            tr = cand
    if tr < R:
        return tr, C
    tc = C
    for cand in range(LANES, C, LANES):
        if C % cand == 0 and R * cand <= ADAM_BLOCK_ELEMS:
            tc = cand
    return R, tc


def _adamw(g_parts, w, m, v, name):
    if not isinstance(g_parts, (list, tuple)):
        g_parts = [g_parts]
    ng = len(g_parts)
    _, R, C = g_parts[0].shape
    tr, tc = _adam_tiles(R, C)

    def body(*refs):
        g_refs = refs[:ng]
        w_ref, m_ref, v_ref, go_ref, d_ref, mo_ref, vo_ref = refs[ng:]
        g = None
        for g_ref in g_refs:
            for p in range(g_ref.shape[0]):
                part = g_ref[p].astype(F32)
                g = part if g is None else g + part
        mn = ADAM_B1 * m_ref[...] + (1.0 - ADAM_B1) * g
        vn = ADAM_B2 * v_ref[...] + (1.0 - ADAM_B2) * (g * g)
        m_hat = mn / (1.0 - ADAM_B1 ** ADAM_STEP)
        v_hat = vn / (1.0 - ADAM_B2 ** ADAM_STEP)
        go_ref[...] = g
        d_ref[...] = -ADAM_LR * (m_hat / (jnp.sqrt(v_hat) + ADAM_EPS) + ADAM_WD * w_ref[...])
        mo_ref[...] = mn
        vo_ref[...] = vn

    slab = jax.ShapeDtypeStruct((R, C), F32)
    blk = pl.BlockSpec((tr, tc), lambda i, j: (i, j))
    return pl.pallas_call(
        body, name=name, grid=(R // tr, C // tc),
        out_shape=(slab,) * 4,
        in_specs=[pl.BlockSpec((g.shape[0], tr, tc), lambda i, j: (0, i, j)) for g in g_parts] + [blk, blk, blk],
        out_specs=(blk,) * 4, compiler_params=_cparams(),
    )(*g_parts, w, m, v)


def _adamw_slab(g_parts, w, m, v, layer, prev, name):
    ng = len(g_parts)
    NL, R, C = w.shape
    tr, tc = _adam_tiles(R, C)

    def body(*refs):
        g_refs = refs[:ng]
        w_ref, m_ref, v_ref = refs[ng:ng + 3]
        go_ref, d_ref, mo_ref, vo_ref = refs[-4:]
        g = None
        for g_ref in g_refs:
            for p in range(g_ref.shape[0]):
                part = g_ref[p].astype(F32)
                g = part if g is None else g + part
        mn = ADAM_B1 * m_ref[0] + (1.0 - ADAM_B1) * g
        vn = ADAM_B2 * v_ref[0] + (1.0 - ADAM_B2) * (g * g)
        m_hat = mn / (1.0 - ADAM_B1 ** ADAM_STEP)
        v_hat = vn / (1.0 - ADAM_B2 ** ADAM_STEP)
        go_ref[0] = g
        d_ref[0] = -ADAM_LR * (m_hat / (jnp.sqrt(v_hat) + ADAM_EPS) + ADAM_WD * w_ref[0])
        mo_ref[0] = mn
        vo_ref[0] = vn

    blk = pl.BlockSpec((1, tr, tc), lambda i, j: (layer, i, j))
    n_in = ng + 3
    prev = list(prev) if prev is not None else []
    return pl.pallas_call(
        body, name=name, grid=(R // tr, C // tc),
        out_shape=(jax.ShapeDtypeStruct((NL, R, C), F32),) * 4,
        in_specs=([pl.BlockSpec((g.shape[0], tr, tc), lambda i, j: (0, i, j)) for g in g_parts] + [blk, blk, blk]
                  + [pl.BlockSpec(memory_space=pl.ANY)] * len(prev)),
        out_specs=(blk,) * 4,
        input_output_aliases={n_in + k: k for k in range(len(prev))},
        compiler_params=_cparams(),
    )(*g_parts, w, m, v, *prev)


def _gather_cols(g, shape):
    nd = len(shape)
    t = jnp.moveaxis(g, 0, nd - 1)
    return t.reshape(tuple(shape[:-1]) + (N_DEV * shape[-1],))


def _scatter_cols(full, n):
    t = full.reshape(full.shape[:-1] + (N_DEV, n))
    return jnp.moveaxis(t, -2, 0)


def kernel(x, c, positions, ada_w, ada_b, pre_norm_g, post_norm_g, even_w_in, even_sc_conv_w, even_sc_conv_b, even_q_norm_g, even_kv_norm_g, even_w_uq, even_w_ukv, even_w_out, odd_w_in, odd_conv_w, odd_conv_b, odd_ln_g, odd_ln_b, odd_w_out, loss_target, m_ada_w, m_ada_b, m_pre_norm_g, m_post_norm_g, m_even_w_in, m_even_sc_conv_w, m_even_sc_conv_b, m_even_q_norm_g, m_even_kv_norm_g, m_even_w_uq, m_even_w_ukv, m_even_w_out, m_odd_w_in, m_odd_conv_w, m_odd_conv_b, m_odd_ln_g, m_odd_ln_b, m_odd_w_out, v_ada_w, v_ada_b, v_pre_norm_g, v_post_norm_g, v_even_w_in, v_even_sc_conv_w, v_even_sc_conv_b, v_even_q_norm_g, v_even_kv_norm_g, v_even_w_uq, v_even_w_ukv, v_even_w_out, v_odd_w_in, v_odd_conv_w, v_odd_conv_b, v_odd_ln_g, v_odd_ln_b, v_odd_w_out):
    S, D = x.shape[1], x.shape[2]
    L = ada_w.shape[0]
    NE, NO = even_w_in.shape[0], odd_w_in.shape[0]
    me = 4 * lax.axis_index("x") + 2 * lax.axis_index("y") + lax.axis_index("c")
    x0 = x[0]
    target = loss_target[0]

    small_parts = [c, even_sc_conv_w, odd_conv_w, odd_conv_b, odd_ln_g, odd_ln_b]
    small_shapes = [p.shape for p in small_parts]
    sg = _exchange([_pack(small_parts, F32, SUBLANES)], False, "gather_small")[0].reshape(N_DEV, -1)
    c_all, scw_g, ocw_g, ocb_g, olg_g, olb_g = _unpack(sg, small_shapes)
    c_all = c_all.reshape(N_DEV, D)
    sc_conv_w = _gather_cols(scw_g, even_sc_conv_w.shape)
    o_conv_w = _gather_cols(ocw_g, odd_conv_w.shape)
    o_conv_b = _gather_cols(ocb_g, odd_conv_b.shape)
    o_ln_g = _gather_cols(olg_g, odd_ln_g.shape)
    o_ln_b = _gather_cols(olb_g, odd_ln_b.shape)

    pad_q = HEAD_PAD - QK_NOPE - QK_ROPE
    w_local = [jnp.swapaxes(even_w_in, 1, 2).astype(BF16),
               jnp.pad(even_w_uq, ((0, 0), (0, 0), (0, pad_q))).astype(BF16),
               jnp.pad(even_w_ukv[..., :QK_NOPE], ((0, 0), (0, 0), (0, HEAD_PAD - QK_NOPE))).astype(BF16),
               even_w_ukv[..., QK_NOPE:].astype(BF16),
               even_w_out.astype(BF16), odd_w_in.astype(BF16), odd_w_out.astype(BF16)]
    n_ada = ada_w.shape[2]
    ada_b_cols = lax.dynamic_slice_in_dim(ada_b, me * n_ada, n_ada, axis=1).reshape(L, 1, n_ada)
    mod_slab = _ada_fwd(c_all, ada_w, ada_b_cols)
    mod_g = _exchange([_pack([mod_slab], F32, SUBLANES)], False, "gather_mod")[0].reshape(N_DEV, -1)
    mod_all = mod_g[:, :L * N_DEV * n_ada].reshape(N_DEV, L, N_DEV, n_ada)
    mod = lax.dynamic_index_in_dim(mod_all, me, axis=2, keepdims=False)
    mod = jnp.moveaxis(mod, 0, 1).reshape(L, 3 * D)
    shift, scale, gate = mod[:, :D], mod[:, D:2 * D], mod[:, 2 * D:]

    heads_to_cols = lambda g: jnp.moveaxis(g, 0, 1).reshape(g.shape[1], -1)
    w_handles = {}
    token = jnp.broadcast_to(jnp.minimum(jnp.abs(mod[0, 0]), 0.0), (SUBLANES, LANES))
    for layer in range(L):
        i = layer // 2
        groups = ({"in": [w_local[0][i]], "rest": [w[i] for w in w_local[1:5]]} if layer % 2 == 0
                  else {"all": [w[i] for w in w_local[5:]]})
        for key, mine in groups.items():
            mine = [w + token[0, 0].astype(BF16) for w in mine]
            w_handles[layer, key], token = _exchange_start(mine, False, f"gather_weights_start_l{layer}_{key}")
    w_token = token

    def arrived(layer, key, after):
        return _exchange_wait(w_handles[layer, key], False, after, f"gather_weights_wait_l{layer}_{key}")[1]

    e_w_in_k, e_w_q_k, e_w_kv_k, e_w_out, o_w_in, o_w_out = ([None] * NE, [None] * NE, [None] * NE, [None] * NE,
                                                             [None] * NO, [None] * NO)

    inv_freq = 1.0 / (ROPE_THETA ** (jnp.arange(0, QK_ROPE, 2, dtype=F32) / QK_ROPE))
    inv_lane = jnp.zeros((HEAD_PAD,), F32).at[QK_NOPE:QK_NOPE + QK_ROPE].set(jnp.concatenate([inv_freq, inv_freq]))
    tabs = _rope_tables(positions.astype(F32).reshape(S, 1), inv_lane.reshape(1, HEAD_PAD))

    row = lambda a: a.reshape(1, -1)
    scb = even_sc_conv_b
    KP3, KP31 = SUBLANES, 32

    saved = []
    xs = x0
    lv = jnp.stack([pre_norm_g, post_norm_g, scale, shift, gate], axis=1).reshape(L * 5, 1, D)
    PRE_G, POST_G, SCALE, SHIFT, GATE = range(5)
    vec = lambda layer, k: (lv, layer * 5 + k)
    h = _pre_norm(xs, vec(0, PRE_G), vec(0, SCALE), vec(0, SHIFT), "pre_norm_l0", after=w_token)
    for layer in range(L):
        i = layer // 2
        tag = f"l{layer}"
        first = [h, tabs[0]] if layer == 0 else h
        if layer % 2 == 0:
            wt = arrived(layer, "in", first)[0].reshape(-1, D)
            e_w_in_k[i] = jnp.concatenate([wt[:2048], wt[2464:2976], wt[2048:2432], jnp.zeros((QK_NOPE, D), BF16),
                                           wt[2432:2464], jnp.zeros((pad_q, D), BF16)], axis=0)
            z, u, qn, kvn = _w_in_even(h, e_w_in_k[i], row(even_q_norm_g[i]), row(even_kv_norm_g[i]), f"w_in_{tag}")
            eq_g, ek_g, ev_g, eout_g = arrived(layer, "rest", z)
            e_w_q_k[i] = heads_to_cols(eq_g)
            e_w_kv_k[i] = jnp.concatenate([heads_to_cols(ek_g), heads_to_cols(ev_g)], axis=-1)
            e_w_out[i] = eout_g.reshape(-1, D)
            cw = jnp.pad(sc_conv_w[i], ((0, KP3 - SC_KERNEL), (0, 0)))
            cv = _conv_fwd(u, cw, row(scb[i]), SC_KERNEL, f"conv_{tag}")
            q, k, v, kT3, vT3 = _qkv_fwd_t(qn, kvn, z, tabs, e_w_q_k[i], e_w_kv_k[i], f"qkv_{tag}")
            o, lse = _attn_fwd_t(q, k, vT3, f"attn_{tag}")
            ycat = _even_post(z, cv, o, f"even_post_{tag}")
            y = _matmul(ycat, e_w_out[i], "nn", BF16, f"w_out_{tag}", tn=1024)
            saved.append(dict(x=xs, h=h, z=z, u=u, qn=qn, kvn=kvn, cw=cw, cv=cv, q=q, k=k, v=v, kT3=kT3, o=o, lse=lse,
                              ycat=ycat, y=y))
        else:
            owin_g, oout_g = arrived(layer, "all", first)
            o_w_in[i], o_w_out[i] = heads_to_cols(owin_g), oout_g.reshape(-1, D)
            z, u = _w_in_odd(h, o_w_in[i], f"w_in_{tag}")
            cw = jnp.pad(o_conv_w[i], ((0, KP31 - CONF_KERNEL), (0, 0)))
            cv = _conv_fwd(u, cw, row(o_conv_b[i]), CONF_KERNEL, f"conv_{tag}")
            yin = _odd_post(cv, z, row(o_ln_g[i]), row(o_ln_b[i]), f"odd_post_{tag}")
            y = _matmul(yin, o_w_out[i], "nn", BF16, f"w_out_{tag}", tn=1024)
            saved.append(dict(x=xs, h=h, z=z, u=u, cw=cw, cv=cv, yin=yin, y=y))
        if layer < L - 1:
            xs, h = _post_pre_norm(xs, y, vec(layer, POST_G), vec(layer, GATE), vec(layer + 1, PRE_G),
                                   vec(layer + 1, SCALE), vec(layer + 1, SHIFT), f"post_pre_norm_{tag}")

    dx, dy, err_sq, dgate, g_post_last = _loss_post_norm_bwd(xs, y, vec(L - 1, POST_G), vec(L - 1, GATE), target,
                                                             "loss_post_norm_bwd")
    loss = lax.psum(_scaled_total(err_sq, 0.5 / D, "loss_total")[0, 0], MESH_AXES)

    g_pre, g_post, dmod = [None] * L, [None] * L, [None] * L
    g_e_w_in, g_e_w_uq, g_e_w_ukv, g_e_w_out = [None] * NE, [None] * NE, [None] * NE, [None] * NE
    g_scw, g_scb, g_qg, g_kvg = [None] * NE, [None] * NE, [None] * NE, [None] * NE
    g_o_w_in, g_o_w_out, g_ocw, g_ocb, g_olg, g_olb = ([None] * NO for _ in range(6))
    sm_w = [even_sc_conv_w, odd_conv_w, odd_conv_b, odd_ln_g, odd_ln_b]
    sm_rows = _pack(sm_w, F32, SUBLANES).shape[0]

    def small_slab():
        full = [_scatter_cols(jnp.stack(g_scw), even_sc_conv_w.shape[-1]),
                _scatter_cols(jnp.stack(g_ocw), odd_conv_w.shape[-1]),
                _scatter_cols(jnp.concatenate(g_ocb, 0), odd_conv_b.shape[-1]),
                _scatter_cols(jnp.concatenate(g_olg, 0), odd_ln_g.shape[-1]),
                _scatter_cols(jnp.concatenate(g_olb, 0), odd_ln_b.shape[-1])]
        flat = jnp.concatenate([g.reshape(N_DEV, -1) for g in full], axis=1)
        return jnp.pad(flat, ((0, 0), (0, sm_rows * PACK_COLS - flat.shape[1]))).reshape(N_DEV, sm_rows, PACK_COLS)

    scatters = []
    bw_token = jnp.zeros((SUBLANES, LANES), F32)

    def start_scatter(tag, names, parts):
        own = [lax.dynamic_slice_in_dim(g, me, 1, axis=0) for g in parts]
        handle, token = _exchange_start([g.astype(BF16) for g in parts], True, f"scatter_grads_start_{tag}")
        scatters.append((tag, names, handle, own))
        return token

    for layer in reversed(range(L)):
        i = layer // 2
        tag = f"l{layer}"
        sv = saved[layer]
        if layer == L - 1:
            g_post[layer] = g_post_last
        if layer % 2 == 0:
            dyc = _matmul(dy, e_w_out[i], "nt", BF16, f"d_ycat_{tag}", tn=1024)
            g_e_w_out[i] = _matmul(sv["ycat"], dy, "tn", BF16, f"g_w_out_{tag}", tn=1024).reshape(N_DEV, -1, D)
            if layer == 0:
                bw_token = start_scatter("l0_out", [("even_w_out", i)], [g_e_w_out[i]])
            dab, dag, dbg, dcv, do, delta = _even_bwd_gates(dyc, sv["z"], sv["cv"], sv["o"], f"even_gates_bwd_{tag}")
            du, dcw, g_scb[i] = _conv_bwd(dcv, sv["u"], sv["cw"] + bw_token[0, 0], SC_KERNEL, f"conv_bwd_{tag}")
            g_scw[i] = dcw[:SC_KERNEL]
            dq, dk, dv = _attn_bwd_t(sv["q"], sv["k"], sv["v"], sv["kT3"], do, sv["lse"], delta, f"attn_bwd_{tag}")
            (dqp, dkvp, dcq, dckv, dkr, g_qg[i], g_kvg[i]) = _qkv_bwd(
                dq, dk, dv, sv["z"], tabs, e_w_q_k[i], e_w_kv_k[i],
                row(even_q_norm_g[i]), row(even_kv_norm_g[i]), f"qkv_bwd_{tag}")
            gq = _matmul(sv["qn"], dqp, "tn", BF16, f"g_w_uq_{tag}", tn=1024)
            gkv = _matmul(sv["kvn"], dkvp, "tn", BF16, f"g_w_ukv_{tag}")
            g_e_w_uq[i] = jnp.moveaxis(gq.reshape(QL, HEADS, HEAD_PAD)[..., :QK_NOPE + QK_ROPE], 1, 0)
            g_e_w_ukv[i] = jnp.moveaxis(jnp.concatenate(
                [gkv[:, :HEADS * HEAD_PAD].reshape(KVL, HEADS, HEAD_PAD)[..., :QK_NOPE],
                 gkv[:, HEADS * HEAD_PAD:].reshape(KVL, HEADS, V_HEAD)], axis=-1), 1, 0)
            dz = _even_dz(dab, du, sv["z"], dag, dbg, dcq, dckv, dkr, f"even_dz_{tag}")
            gt = _matmul(dz, sv["h"], "tn", BF16, f"g_w_in_{tag}", tm=1024, tn=1024)
            g_e_w_in[i] = jnp.concatenate([gt[:2048], gt[2560:2944], gt[2944 + QK_NOPE:2944 + QK_NOPE + QK_ROPE],
                                           gt[2048:2560]], axis=0).reshape(N_DEV, -1, D)
            names = [("even_w_in", i), ("even_w_uq", i), ("even_w_ukv", i)]
            parts = [g_e_w_in[i], g_e_w_uq[i], g_e_w_ukv[i]]
            if layer == 0:
                names, parts = names + [("small", 0)], parts + [small_slab()]
            else:
                names, parts = names + [("even_w_out", i)], parts + [g_e_w_out[i]]
            bw_token = start_scatter(tag, names, parts)
            w_dh = e_w_in_k[i] + bw_token[0, 0].astype(BF16) if layer == 0 else e_w_in_k[i]
            dh = _matmul(dz, w_dh, "nn", BF16, f"d_h_{tag}", tn=1024)
        else:
            dyi = _matmul(dy, o_w_out[i], "nt", BF16, f"d_yin_{tag}", tn=1024)
            g_o_w_out[i] = _matmul(sv["yin"], dy, "tn", BF16, f"g_w_out_{tag}", tn=1024).reshape(N_DEV, -1, D)
            dcv, dsg, g_olg[i], g_olb[i] = _odd_bwd_norm(dyi, sv["cv"], sv["z"], row(o_ln_g[i]), row(o_ln_b[i]),
                                                         f"odd_norm_bwd_{tag}")
            du, dcw, g_ocb[i] = _conv_bwd(dcv, sv["u"], sv["cw"], CONF_KERNEL, f"conv_bwd_{tag}")
            g_ocw[i] = dcw[:CONF_KERNEL]
            dz = _odd_dz(du, sv["z"], dsg, f"odd_dz_{tag}")
            g_o_w_in[i] = _matmul(sv["h"], dz, "tn", BF16, f"g_w_in_{tag}", tm=1024, tn=odd_w_in.shape[-1],
                                  split_n=True)
            bw_token = start_scatter(tag, [("odd_w_in", i), ("odd_w_out", i)], [g_o_w_in[i], g_o_w_out[i]])
            dh = _matmul(dz, o_w_in[i], "nt", BF16, f"d_h_{tag}", tn=1024)
        if layer > 0:
            (dx, dy, dshift, dscale, g_pre[layer], dgate_prev, g_post[layer - 1]) = _pre_post_norm_bwd(
                dh, sv["x"], dx, vec(layer, PRE_G), vec(layer, SCALE), saved[layer - 1]["y"], vec(layer - 1, POST_G),
                vec(layer - 1, GATE), f"pre_post_norm_bwd_{tag}", after=bw_token)
        else:
            dx, dshift, dscale, g_pre[layer] = _pre_norm_bwd(dh, sv["x"], dx, vec(layer, PRE_G), vec(layer, SCALE),
                                                             f"pre_norm_bwd_{tag}", after=bw_token)
            dgate_prev = None
        dmod[layer] = jnp.concatenate([dshift, dscale, dgate], axis=-1)
        dgate = dgate_prev
    grad_x = dx.reshape(1, S, D)

    rep_g = [jnp.concatenate(dmod, 0), jnp.concatenate(g_pre, 0), jnp.concatenate(g_post, 0),
             jnp.stack(g_scb), jnp.stack(g_qg), jnp.stack(g_kvg)]
    rep_w = [ada_b, pre_norm_g, post_norm_g, even_sc_conv_b, even_q_norm_g, even_kv_norm_g]
    rep_m = [m_ada_b, m_pre_norm_g, m_post_norm_g, m_even_sc_conv_b, m_even_q_norm_g, m_even_kv_norm_g]
    rep_v = [v_ada_b, v_pre_norm_g, v_post_norm_g, v_even_sc_conv_b, v_even_q_norm_g, v_even_kv_norm_g]
    rep_shapes = [w.shape for w in rep_w]
    rep_all = _exchange([_pack(rep_g, F32, SUBLANES)], False, "gather_small_grads")[0]
    rep_out = _adamw(rep_all, _pack(rep_w, F32, SUBLANES), _pack(rep_m, F32, SUBLANES), _pack(rep_v, F32, SUBLANES),
                     "adamw_replicated")
    rep_res = [_unpack(o.reshape(-1), rep_shapes) for o in rep_out]

    dmod_all = rep_all.reshape(N_DEV, -1)[:, :L * 3 * D].reshape(N_DEV, L, 3 * D)
    dmod_cols = jnp.moveaxis(lax.dynamic_slice_in_dim(dmod_all, me * n_ada, n_ada, axis=2), 0, 1)
    g_ada_w = _ada_bwd(c_all.T, dmod_cols)
    ada_out = _adamw(g_ada_w.reshape(1, -1, n_ada), ada_w.reshape(-1, n_ada),
                     m_ada_w.reshape(-1, n_ada), v_ada_w.reshape(-1, n_ada), "adamw_ada_w")
    ada_res = [o.reshape(ada_w.shape) for o in ada_out]

    sm_m = [m_even_sc_conv_w, m_odd_conv_w, m_odd_conv_b, m_odd_ln_g, m_odd_ln_b]
    sm_v = [v_even_sc_conv_w, v_odd_conv_w, v_odd_conv_b, v_odd_ln_g, v_odd_ln_b]
    sm_shapes = [w.shape for w in sm_w]
    state = {"even_w_in": (even_w_in, m_even_w_in, v_even_w_in), "even_w_uq": (even_w_uq, m_even_w_uq, v_even_w_uq),
             "even_w_ukv": (even_w_ukv, m_even_w_ukv, v_even_w_ukv), "even_w_out": (even_w_out, m_even_w_out, v_even_w_out),
             "odd_w_in": (odd_w_in, m_odd_w_in, v_odd_w_in), "odd_w_out": (odd_w_out, m_odd_w_out, v_odd_w_out)}
    state["even_w_in"] = tuple(jnp.swapaxes(t, 1, 2) for t in state["even_w_in"])
    big_res = {name: None for name in state}
    after = [bw_token, grad_x, rep_out[0], ada_out[0]]
    sm_res = None
    for tag, names, handle, own in scatters:
        _, landed = _exchange_wait(handle, True, after, f"scatter_grads_wait_{tag}")
        after = []
        for a, (name, i) in enumerate(names):
            if name == "small":
                sm_out = _adamw([own[a], landed[a]], _pack(sm_w, F32, SUBLANES), _pack(sm_m, F32, SUBLANES),
                                _pack(sm_v, F32, SUBLANES), "adamw_small_sharded")
                sm_res = [_unpack(o.reshape(-1), sm_shapes) for o in sm_out]
                continue
            big_res[name] = _adamw_slab([own[a], landed[a]], *state[name], i, big_res[name], f"adamw_{name}_{i}")
            after += list(big_res[name])
    sh_res = [dict(zip(["even_sc_conv_w", "odd_conv_w", "odd_conv_b", "odd_ln_g", "odd_ln_b"], sm_res[kind]))
              for kind in range(4)]
    for name in state:
        for kind in range(4):
            res = big_res[name][kind]
            sh_res[kind][name] = jnp.swapaxes(res, 1, 2) if name == "even_w_in" else res

    order = ["ada_w", "ada_b", "pre_norm_g", "post_norm_g", "even_w_in", "even_sc_conv_w", "even_sc_conv_b",
             "even_q_norm_g", "even_kv_norm_g", "even_w_uq", "even_w_ukv", "even_w_out", "odd_w_in", "odd_conv_w",
             "odd_conv_b", "odd_ln_g", "odd_ln_b", "odd_w_out"]
    rep_names = ["ada_b", "pre_norm_g", "post_norm_g", "even_sc_conv_b", "even_q_norm_g", "even_kv_norm_g"]
    outs = [loss, grad_x]
    for kind in range(4):
        for name in order:
            if name == "ada_w":
                outs.append(ada_res[kind])
            elif name in rep_names:
                outs.append(rep_res[kind][rep_names.index(name)])
            else:
                outs.append(sh_res[kind][name])
    return tuple(outs)
```
